```python
import math
import jax, jax.numpy as jnp
from jax import lax
import numpy as np

D_MODEL = 1024
BATCH = 8
SEQ = 4096
DEPTH = 2

D_MIX = D_MODEL
S5_WIDTH = D_MIX // 2
S5_GROUP = 16
S5_GROUPS = S5_WIDTH // S5_GROUP
S5_STATE = 64
DT_MIN = 0.001
DT_MAX = 0.1
SB_WIDTH = D_MIX - S5_WIDTH
SB_HEAD_DIM = 64
SB_HEADS = SB_WIDTH // SB_HEAD_DIM
SB_BLOCK = 128
CONV_CH = D_MODEL
CONV_K = 3
D_FF = 2752
N_EVEN = (DEPTH + 1) // 2
N_ODD = DEPTH // 2
EPS = 1e-6

kernel_name = "hybrid_s5_stickbreak_shortconv_macaron"


def rmsnorm(x, g):
    xf = x.astype(jnp.float32)
    r = lax.rsqrt(jnp.mean(xf * xf, axis=-1, keepdims=True) + EPS)
    return (xf * r * g.astype(jnp.float32)).astype(x.dtype)


def swiglu(h, w_gate, w_up, w_down):
    return (jax.nn.silu(h @ w_gate) * (h @ w_up)) @ w_down


def _complex_linear_combine(e1, e2):
    a1r, a1i, b1r, b1i = e1
    a2r, a2i, b2r, b2i = e2
    return (a2r * a1r - a2i * a1i,
            a2r * a1i + a2i * a1r,
            a2r * b1r - a2i * b1i + b2r,
            a2r * b1i + a2i * b1r + b2i)


def s5_mixer(u, lam_re, lam_im, log_dt, b_re, b_im, c_re, c_im, d, w_glu):
    bsz, seq, _ = u.shape
    uf = u.astype(jnp.float32).reshape(bsz, seq, S5_GROUPS, S5_GROUP)
    lr = lam_re.astype(jnp.float32)
    li = lam_im.astype(jnp.float32)
    dt = jnp.exp(log_dt.astype(jnp.float32))[:, None]
    mag = jnp.exp(lr * dt)
    ab_re = mag * jnp.cos(li * dt)
    ab_im = mag * jnp.sin(li * dt)
    den = lr * lr + li * li
    nr = ab_re - 1.0
    coef_re = (nr * lr + ab_im * li) / den
    coef_im = (ab_im * lr - nr * li) / den
    br = b_re.astype(jnp.float32)
    bi = b_im.astype(jnp.float32)
    bb_re = coef_re[..., None] * br - coef_im[..., None] * bi
    bb_im = coef_re[..., None] * bi + coef_im[..., None] * br
    bu_re = jnp.einsum('blgp,gnp->blgn', uf, bb_re)
    bu_im = jnp.einsum('blgp,gnp->blgn', uf, bb_im)
    a_re = jnp.broadcast_to(ab_re, (seq,) + ab_re.shape)[None]
    a_im = jnp.broadcast_to(ab_im, (seq,) + ab_im.shape)[None]
    _, _, h_re, h_im = lax.associative_scan(
        _complex_linear_combine, (a_re, a_im, bu_re, bu_im), axis=1)
    y = (jnp.einsum('blgn,gpn->blgp', h_re, c_re.astype(jnp.float32))
         - jnp.einsum('blgn,gpn->blgp', h_im, c_im.astype(jnp.float32))
         + d.astype(jnp.float32).reshape(S5_GROUPS, S5_GROUP) * uf)
    y = y.reshape(bsz, seq, S5_WIDTH)
    z = jax.nn.gelu(y)
    out = z * jax.nn.sigmoid(z @ w_glu.astype(jnp.float32))
    return out.astype(u.dtype)


def stick_breaking_attention(q, k, v):
    bsz, seq, nh, dh = q.shape
    qh = q.transpose(0, 2, 1, 3).astype(jnp.float32)
    kh = k.transpose(0, 2, 1, 3).astype(jnp.float32)
    vh = v.transpose(0, 2, 1, 3)
    scale = 1.0 / math.sqrt(dh)
    key_pos = jnp.arange(seq)
    n_blocks = seq // SB_BLOCK

    def block(i):
        start = i * SB_BLOCK
        qb = lax.dynamic_slice_in_dim(qh, start, SB_BLOCK, axis=2)
        z = jnp.einsum('bhqd,bhkd->bhqk', qb, kh) * scale
        q_pos = start + jnp.arange(SB_BLOCK)
        mask = key_pos[None, :] < q_pos[:, None]
        log_keep = jnp.where(mask, jax.nn.log_sigmoid(-z), 0.0)
        later = lax.cumsum(log_keep, axis=3, reverse=True) - log_keep
        w = jnp.where(mask, jnp.exp(jax.nn.log_sigmoid(z) + later), 0.0)
        return jnp.einsum('bhqk,bhkd->bhqd', w.astype(vh.dtype), vh)

    out = lax.map(block, jnp.arange(n_blocks))
    out = out.transpose(1, 0, 3, 2, 4).reshape(bsz, seq, nh * dh)
    return out


def parallel_s5_stickbreak(h, w_in, lam_re, lam_im, log_dt, b_re, b_im, c_re, c_im, d, w_glu, w_out):
    bsz, seq, _ = h.shape
    proj = h @ w_in
    u = proj[..., :S5_WIDTH]
    qkv = proj[..., S5_WIDTH:].reshape(bsz, seq, 3, SB_HEADS, SB_HEAD_DIM)
    y_a = s5_mixer(u, lam_re, lam_im, log_dt, b_re, b_im, c_re, c_im, d, w_glu)
    y_b = stick_breaking_attention(qkv[:, :, 0], qkv[:, :, 1], qkv[:, :, 2]).astype(y_a.dtype)
    return jnp.concatenate([y_a, y_b], axis=-1) @ w_out


def short_conv_mixer(h, w_in, conv_w, w_out):
    proj = h @ w_in
    b_gate, c_gate, v = jnp.split(proj, 3, axis=-1)
    y = lax.conv_general_dilated(
        c_gate * v, conv_w[:, None, :], window_strides=(1,),
        padding=[(CONV_K - 1, 0)], dimension_numbers=('NWC', 'WIO', 'NWC'),
        feature_group_count=CONV_CH)
    return (b_gate * y) @ w_out


def _fwd_setup_inputs(seed: int = 0) -> dict:
    key = jax.random.key(seed)
    ks = jax.random.split(key, 32)
    nrm = jax.random.normal
    f32 = jnp.float32
    D, F = D_MODEL, D_FF
    G, N, P = S5_GROUPS, S5_STATE, S5_GROUP
    inp = {}
    inp['x'] = nrm(ks[0], (BATCH, SEQ, D), f32)
    inp['ffn1_norm'] = 1.0 + 0.02 * nrm(ks[1], (DEPTH, D), f32)
    inp['ffn1_w_gate'] = nrm(ks[2], (DEPTH, D, F), f32) * D ** -0.5
    inp['ffn1_w_up'] = nrm(ks[3], (DEPTH, D, F), f32) * D ** -0.5
    inp['ffn1_w_down'] = nrm(ks[4], (DEPTH, F, D), f32) * F ** -0.5
    inp['mix_norm'] = 1.0 + 0.02 * nrm(ks[5], (DEPTH, D), f32)
    inp['ffn2_norm'] = 1.0 + 0.02 * nrm(ks[6], (DEPTH, D), f32)
    inp['ffn2_w_gate'] = nrm(ks[7], (DEPTH, D, F), f32) * D ** -0.5
    inp['ffn2_w_up'] = nrm(ks[8], (DEPTH, D, F), f32) * D ** -0.5
    inp['ffn2_w_down'] = nrm(ks[9], (DEPTH, F, D), f32) * F ** -0.5
    inp['ab_w_in'] = nrm(ks[10], (N_EVEN, D, S5_WIDTH + 3 * SB_WIDTH), f32) * D ** -0.5
    inp['s5_lambda_re'] = -0.5 + 0.01 * nrm(ks[11], (N_EVEN, G, N), f32)
    inp['s5_lambda_im'] = jnp.broadcast_to(math.pi * jnp.arange(N, dtype=f32), (N_EVEN, G, N)) \
        + 0.01 * nrm(ks[12], (N_EVEN, G, N), f32)
    inp['s5_log_dt'] = jax.random.uniform(ks[13], (N_EVEN, G), f32, math.log(DT_MIN), math.log(DT_MAX))
    inp['s5_b_re'] = nrm(ks[14], (N_EVEN, G, N, P), f32) * (2 * P) ** -0.5
    inp['s5_b_im'] = nrm(ks[15], (N_EVEN, G, N, P), f32) * (2 * P) ** -0.5
    inp['s5_c_re'] = nrm(ks[16], (N_EVEN, G, P, N), f32) * N ** -0.5
    inp['s5_c_im'] = nrm(ks[17], (N_EVEN, G, P, N), f32) * N ** -0.5
    inp['s5_d'] = nrm(ks[18], (N_EVEN, S5_WIDTH), f32)
    inp['s5_w_glu'] = nrm(ks[19], (N_EVEN, S5_WIDTH, S5_WIDTH), f32) * S5_WIDTH ** -0.5
    inp['ab_w_out'] = nrm(ks[20], (N_EVEN, D_MIX, D), f32) * D_MIX ** -0.5
    inp['sc_w_in'] = nrm(ks[21], (N_ODD, D, 3 * CONV_CH), f32) * D ** -0.5
    inp['sc_conv_w'] = nrm(ks[22], (N_ODD, CONV_K, CONV_CH), f32) * CONV_K ** -0.5
    inp['sc_w_out'] = nrm(ks[23], (N_ODD, CONV_CH, D), f32) * CONV_CH ** -0.5
    inp['final_norm'] = 1.0 + 0.02 * nrm(ks[24], (D,), f32)
    return inp


def _fwd_reference(x, ffn1_norm, ffn1_w_gate, ffn1_w_up, ffn1_w_down, mix_norm,
              ffn2_norm, ffn2_w_gate, ffn2_w_up, ffn2_w_down,
              ab_w_in, s5_lambda_re, s5_lambda_im, s5_log_dt, s5_b_re, s5_b_im,
              s5_c_re, s5_c_im, s5_d, s5_w_glu, ab_w_out,
              sc_w_in, sc_conv_w, sc_w_out, final_norm):
    for layer in range(DEPTH):
        x = x + 0.5 * swiglu(rmsnorm(x, ffn1_norm[layer]),
                             ffn1_w_gate[layer], ffn1_w_up[layer], ffn1_w_down[layer])
        h = rmsnorm(x, mix_norm[layer])
        if layer % 2 == 0:
            e = layer // 2
            x = x + parallel_s5_stickbreak(
                h, ab_w_in[e], s5_lambda_re[e], s5_lambda_im[e], s5_log_dt[e],
                s5_b_re[e], s5_b_im[e], s5_c_re[e], s5_c_im[e], s5_d[e], s5_w_glu[e], ab_w_out[e])
        else:
            o = layer // 2
            x = x + short_conv_mixer(h, sc_w_in[o], sc_conv_w[o], sc_w_out[o])
        x = x + 0.5 * swiglu(rmsnorm(x, ffn2_norm[layer]),
                             ffn2_w_gate[layer], ffn2_w_up[layer], ffn2_w_down[layer])
    return rmsnorm(x, final_norm)


import jax as _jax
import jax.numpy as _jnp

TWIN_FORMAT = 'train_step'
FWD_PARAMS = ['x', 'ffn1_norm', 'ffn1_w_gate', 'ffn1_w_up', 'ffn1_w_down', 'mix_norm', 'ffn2_norm', 'ffn2_w_gate', 'ffn2_w_up', 'ffn2_w_down', 'ab_w_in', 's5_lambda_re', 's5_lambda_im', 's5_log_dt', 's5_b_re', 's5_b_im', 's5_c_re', 's5_c_im', 's5_d', 's5_w_glu', 'ab_w_out', 'sc_w_in', 'sc_conv_w', 'sc_w_out', 'final_norm']
TWIN_WEIGHTS = ['ffn1_norm', 'ffn1_w_gate', 'ffn1_w_up', 'ffn1_w_down', 'mix_norm', 'ffn2_norm', 'ffn2_w_gate', 'ffn2_w_up', 'ffn2_w_down', 'ab_w_in', 's5_lambda_re', 's5_lambda_im', 's5_log_dt', 's5_b_re', 's5_b_im', 's5_c_re', 's5_c_im', 's5_d', 's5_w_glu', 'ab_w_out', 'sc_w_in', 'sc_conv_w', 'sc_w_out', 'final_norm']
TWIN_DIFF_INPUT = 'x'
TWIN_INPUTS = ['x', 'ffn1_norm', 'ffn1_w_gate', 'ffn1_w_up', 'ffn1_w_down', 'mix_norm', 'ffn2_norm', 'ffn2_w_gate', 'ffn2_w_up', 'ffn2_w_down', 'ab_w_in', 's5_lambda_re', 's5_lambda_im', 's5_log_dt', 's5_b_re', 's5_b_im', 's5_c_re', 's5_c_im', 's5_d', 's5_w_glu', 'ab_w_out', 'sc_w_in', 'sc_conv_w', 'sc_w_out', 'final_norm', 'loss_target', 'm_ffn1_norm', 'm_ffn1_w_gate', 'm_ffn1_w_up', 'm_ffn1_w_down', 'm_mix_norm', 'm_ffn2_norm', 'm_ffn2_w_gate', 'm_ffn2_w_up', 'm_ffn2_w_down', 'm_ab_w_in', 'm_s5_lambda_re', 'm_s5_lambda_im', 'm_s5_log_dt', 'm_s5_b_re', 'm_s5_b_im', 'm_s5_c_re', 'm_s5_c_im', 'm_s5_d', 'm_s5_w_glu', 'm_ab_w_out', 'm_sc_w_in', 'm_sc_conv_w', 'm_sc_w_out', 'm_final_norm', 'v_ffn1_norm', 'v_ffn1_w_gate', 'v_ffn1_w_up', 'v_ffn1_w_down', 'v_mix_norm', 'v_ffn2_norm', 'v_ffn2_w_gate', 'v_ffn2_w_up', 'v_ffn2_w_down', 'v_ab_w_in', 'v_s5_lambda_re', 'v_s5_lambda_im', 'v_s5_log_dt', 'v_s5_b_re', 'v_s5_b_im', 'v_s5_c_re', 'v_s5_c_im', 'v_s5_d', 'v_s5_w_glu', 'v_ab_w_out', 'v_sc_w_in', 'v_sc_conv_w', 'v_sc_w_out', 'v_final_norm']
TWIN_OUTPUTS = ['loss', 'grad_x', 'grad_ffn1_norm', 'grad_ffn1_w_gate', 'grad_ffn1_w_up', 'grad_ffn1_w_down', 'grad_mix_norm', 'grad_ffn2_norm', 'grad_ffn2_w_gate', 'grad_ffn2_w_up', 'grad_ffn2_w_down', 'grad_ab_w_in', 'grad_s5_lambda_re', 'grad_s5_lambda_im', 'grad_s5_log_dt', 'grad_s5_b_re', 'grad_s5_b_im', 'grad_s5_c_re', 'grad_s5_c_im', 'grad_s5_d', 'grad_s5_w_glu', 'grad_ab_w_out', 'grad_sc_w_in', 'grad_sc_conv_w', 'grad_sc_w_out', 'grad_final_norm', 'delta_ffn1_norm', 'delta_ffn1_w_gate', 'delta_ffn1_w_up', 'delta_ffn1_w_down', 'delta_mix_norm', 'delta_ffn2_norm', 'delta_ffn2_w_gate', 'delta_ffn2_w_up', 'delta_ffn2_w_down', 'delta_ab_w_in', 'delta_s5_lambda_re', 'delta_s5_lambda_im', 'delta_s5_log_dt', 'delta_s5_b_re', 'delta_s5_b_im', 'delta_s5_c_re', 'delta_s5_c_im', 'delta_s5_d', 'delta_s5_w_glu', 'delta_ab_w_out', 'delta_sc_w_in', 'delta_sc_conv_w', 'delta_sc_w_out', 'delta_final_norm', 'new_m_ffn1_norm', 'new_m_ffn1_w_gate', 'new_m_ffn1_w_up', 'new_m_ffn1_w_down', 'new_m_mix_norm', 'new_m_ffn2_norm', 'new_m_ffn2_w_gate', 'new_m_ffn2_w_up', 'new_m_ffn2_w_down', 'new_m_ab_w_in', 'new_m_s5_lambda_re', 'new_m_s5_lambda_im', 'new_m_s5_log_dt', 'new_m_s5_b_re', 'new_m_s5_b_im', 'new_m_s5_c_re', 'new_m_s5_c_im', 'new_m_s5_d', 'new_m_s5_w_glu', 'new_m_ab_w_out', 'new_m_sc_w_in', 'new_m_sc_conv_w', 'new_m_sc_w_out', 'new_m_final_norm', 'new_v_ffn1_norm', 'new_v_ffn1_w_gate', 'new_v_ffn1_w_up', 'new_v_ffn1_w_down', 'new_v_mix_norm', 'new_v_ffn2_norm', 'new_v_ffn2_w_gate', 'new_v_ffn2_w_up', 'new_v_ffn2_w_down', 'new_v_ab_w_in', 'new_v_s5_lambda_re', 'new_v_s5_lambda_im', 'new_v_s5_log_dt', 'new_v_s5_b_re', 'new_v_s5_b_im', 'new_v_s5_c_re', 'new_v_s5_c_im', 'new_v_s5_d', 'new_v_s5_w_glu', 'new_v_ab_w_out', 'new_v_sc_w_in', 'new_v_sc_conv_w', 'new_v_sc_w_out', 'new_v_final_norm']
TWIN_LEAF_KINDS = {'loss': 'loss', 'grad_x': 'grad_x', 'grad_ffn1_norm': 'grad_w', 'grad_ffn1_w_gate': 'grad_w', 'grad_ffn1_w_up': 'grad_w', 'grad_ffn1_w_down': 'grad_w', 'grad_mix_norm': 'grad_w', 'grad_ffn2_norm': 'grad_w', 'grad_ffn2_w_gate': 'grad_w', 'grad_ffn2_w_up': 'grad_w', 'grad_ffn2_w_down': 'grad_w', 'grad_ab_w_in': 'grad_w', 'grad_s5_lambda_re': 'grad_w', 'grad_s5_lambda_im': 'grad_w', 'grad_s5_log_dt': 'grad_w', 'grad_s5_b_re': 'grad_w', 'grad_s5_b_im': 'grad_w', 'grad_s5_c_re': 'grad_w', 'grad_s5_c_im': 'grad_w', 'grad_s5_d': 'grad_w', 'grad_s5_w_glu': 'grad_w', 'grad_ab_w_out': 'grad_w', 'grad_sc_w_in': 'grad_w', 'grad_sc_conv_w': 'grad_w', 'grad_sc_w_out': 'grad_w', 'grad_final_norm': 'grad_w', 'delta_ffn1_norm': 'delta_w', 'delta_ffn1_w_gate': 'delta_w', 'delta_ffn1_w_up': 'delta_w', 'delta_ffn1_w_down': 'delta_w', 'delta_mix_norm': 'delta_w', 'delta_ffn2_norm': 'delta_w', 'delta_ffn2_w_gate': 'delta_w', 'delta_ffn2_w_up': 'delta_w', 'delta_ffn2_w_down': 'delta_w', 'delta_ab_w_in': 'delta_w', 'delta_s5_lambda_re': 'delta_w', 'delta_s5_lambda_im': 'delta_w', 'delta_s5_log_dt': 'delta_w', 'delta_s5_b_re': 'delta_w', 'delta_s5_b_im': 'delta_w', 'delta_s5_c_re': 'delta_w', 'delta_s5_c_im': 'delta_w', 'delta_s5_d': 'delta_w', 'delta_s5_w_glu': 'delta_w', 'delta_ab_w_out': 'delta_w', 'delta_sc_w_in': 'delta_w', 'delta_sc_conv_w': 'delta_w', 'delta_sc_w_out': 'delta_w', 'delta_final_norm': 'delta_w', 'new_m_ffn1_norm': 'new_m', 'new_m_ffn1_w_gate': 'new_m', 'new_m_ffn1_w_up': 'new_m', 'new_m_ffn1_w_down': 'new_m', 'new_m_mix_norm': 'new_m', 'new_m_ffn2_norm': 'new_m', 'new_m_ffn2_w_gate': 'new_m', 'new_m_ffn2_w_up': 'new_m', 'new_m_ffn2_w_down': 'new_m', 'new_m_ab_w_in': 'new_m', 'new_m_s5_lambda_re': 'new_m', 'new_m_s5_lambda_im': 'new_m', 'new_m_s5_log_dt': 'new_m', 'new_m_s5_b_re': 'new_m', 'new_m_s5_b_im': 'new_m', 'new_m_s5_c_re': 'new_m', 'new_m_s5_c_im': 'new_m', 'new_m_s5_d': 'new_m', 'new_m_s5_w_glu': 'new_m', 'new_m_ab_w_out': 'new_m', 'new_m_sc_w_in': 'new_m', 'new_m_sc_conv_w': 'new_m', 'new_m_sc_w_out': 'new_m', 'new_m_final_norm': 'new_m', 'new_v_ffn1_norm': 'new_v', 'new_v_ffn1_w_gate': 'new_v', 'new_v_ffn1_w_up': 'new_v', 'new_v_ffn1_w_down': 'new_v', 'new_v_mix_norm': 'new_v', 'new_v_ffn2_norm': 'new_v', 'new_v_ffn2_w_gate': 'new_v', 'new_v_ffn2_w_up': 'new_v', 'new_v_ffn2_w_down': 'new_v', 'new_v_ab_w_in': 'new_v', 'new_v_s5_lambda_re': 'new_v', 'new_v_s5_lambda_im': 'new_v', 'new_v_s5_log_dt': 'new_v', 'new_v_s5_b_re': 'new_v', 'new_v_s5_b_im': 'new_v', 'new_v_s5_c_re': 'new_v', 'new_v_s5_c_im': 'new_v', 'new_v_s5_d': 'new_v', 'new_v_s5_w_glu': 'new_v', 'new_v_ab_w_out': 'new_v', 'new_v_sc_w_in': 'new_v', 'new_v_sc_conv_w': 'new_v', 'new_v_sc_w_out': 'new_v', 'new_v_final_norm': 'new_v'}


def _forward(args):
    return _fwd_reference(*[args[k] for k in FWD_PARAMS])


def _output_shape():
    def fwd():
        inp = _fwd_setup_inputs(0)
        return _fwd_reference(*[inp[k] for k in FWD_PARAMS])
    out = _jax.eval_shape(fwd)
    return out.shape, out.dtype

N_MICROBATCH = 1
ADAM_LR = 0.001
ADAM_B1 = 0.9
ADAM_B2 = 0.999
ADAM_EPS = 1e-08
ADAM_WD = 0.01
ADAM_STEP = 10
PER_EXAMPLE_BATCH_AXIS = {'x': 0, 'loss_target': 0}
SHARED_INPUTS = []
_WEIGHT_DTYPES = {'ffn1_norm': _jnp.float32, 'ffn1_w_gate': _jnp.float32, 'ffn1_w_up': _jnp.float32, 'ffn1_w_down': _jnp.float32, 'mix_norm': _jnp.float32, 'ffn2_norm': _jnp.float32, 'ffn2_w_gate': _jnp.float32, 'ffn2_w_up': _jnp.float32, 'ffn2_w_down': _jnp.float32, 'ab_w_in': _jnp.float32, 's5_lambda_re': _jnp.float32, 's5_lambda_im': _jnp.float32, 's5_log_dt': _jnp.float32, 's5_b_re': _jnp.float32, 's5_b_im': _jnp.float32, 's5_c_re': _jnp.float32, 's5_c_im': _jnp.float32, 's5_d': _jnp.float32, 's5_w_glu': _jnp.float32, 'ab_w_out': _jnp.float32, 'sc_w_in': _jnp.float32, 'sc_conv_w': _jnp.float32, 'sc_w_out': _jnp.float32, 'final_norm': _jnp.float32}
MOMENT_SCALE = {'ffn1_norm': 1.071380e-01, 'ffn1_w_gate': 4.263738e-02, 'ffn1_w_up': 4.130370e-02, 'ffn1_w_down': 6.768500e-02, 'mix_norm': 1.678843e-01, 'ffn2_norm': 7.143907e-02, 'ffn2_w_gate': 3.117830e-02, 'ffn2_w_up': 3.024584e-02, 'ffn2_w_down': 4.954049e-02, 'ab_w_in': 9.372345e-02, 's5_lambda_re': 5.997771e-03, 's5_lambda_im': 5.445478e-03, 's5_log_dt': 3.376204e+00, 's5_b_re': 3.756170e-03, 's5_b_im': 3.821452e-03, 's5_c_re': 5.424243e-03, 's5_c_im': 5.359513e-03, 's5_d': 8.213421e-02, 's5_w_glu': 2.093047e-02, 'ab_w_out': 1.128692e-01, 'sc_w_in': 1.126547e-01, 'sc_conv_w': 1.119492e-01, 'sc_w_out': 1.131918e-01, 'final_norm': 3.202923e+01}


def _to_microbatches(a, axis):
    t = _jnp.moveaxis(a, axis, 0)
    t = t.reshape((N_MICROBATCH, t.shape[0] // N_MICROBATCH) + t.shape[1:])
    return _jnp.moveaxis(t, 1, axis + 1)


def setup_inputs(seed: int = 0) -> dict:
    inp = _fwd_setup_inputs(seed)
    key = _jax.random.fold_in(_jax.random.key(seed), 7919)
    shape, _ = _output_shape()
    out = dict(inp)
    out["loss_target"] = _jax.random.normal(_jax.random.fold_in(key, 0), shape, _jnp.float32)
    for i, name in enumerate(TWIN_WEIGHTS):
        w = inp[name].astype(_jnp.float32)
        if MOMENT_SCALE is None:
            s = _jnp.sqrt(_jnp.mean(_jnp.square(w)) + 1e-30)
        else:
            s = MOMENT_SCALE[name]
        km, kv = _jax.random.split(_jax.random.fold_in(key, i + 1))
        out[name] = w
        out["m_" + name] = s * _jax.random.normal(km, w.shape, _jnp.float32)
        out["v_" + name] = (s * s) * _jax.random.uniform(kv, w.shape, _jnp.float32, 0.5, 1.5)
    if N_MICROBATCH > 1:
        for name, axis in PER_EXAMPLE_BATCH_AXIS.items():
            out[name] = _to_microbatches(out[name], axis)
    return {'x': out['x'], 'ffn1_norm': out['ffn1_norm'], 'ffn1_w_gate': out['ffn1_w_gate'], 'ffn1_w_up': out['ffn1_w_up'], 'ffn1_w_down': out['ffn1_w_down'], 'mix_norm': out['mix_norm'], 'ffn2_norm': out['ffn2_norm'], 'ffn2_w_gate': out['ffn2_w_gate'], 'ffn2_w_up': out['ffn2_w_up'], 'ffn2_w_down': out['ffn2_w_down'], 'ab_w_in': out['ab_w_in'], 's5_lambda_re': out['s5_lambda_re'], 's5_lambda_im': out['s5_lambda_im'], 's5_log_dt': out['s5_log_dt'], 's5_b_re': out['s5_b_re'], 's5_b_im': out['s5_b_im'], 's5_c_re': out['s5_c_re'], 's5_c_im': out['s5_c_im'], 's5_d': out['s5_d'], 's5_w_glu': out['s5_w_glu'], 'ab_w_out': out['ab_w_out'], 'sc_w_in': out['sc_w_in'], 'sc_conv_w': out['sc_conv_w'], 'sc_w_out': out['sc_w_out'], 'final_norm': out['final_norm'], 'loss_target': out['loss_target'], 'm_ffn1_norm': out['m_ffn1_norm'], 'm_ffn1_w_gate': out['m_ffn1_w_gate'], 'm_ffn1_w_up': out['m_ffn1_w_up'], 'm_ffn1_w_down': out['m_ffn1_w_down'], 'm_mix_norm': out['m_mix_norm'], 'm_ffn2_norm': out['m_ffn2_norm'], 'm_ffn2_w_gate': out['m_ffn2_w_gate'], 'm_ffn2_w_up': out['m_ffn2_w_up'], 'm_ffn2_w_down': out['m_ffn2_w_down'], 'm_ab_w_in': out['m_ab_w_in'], 'm_s5_lambda_re': out['m_s5_lambda_re'], 'm_s5_lambda_im': out['m_s5_lambda_im'], 'm_s5_log_dt': out['m_s5_log_dt'], 'm_s5_b_re': out['m_s5_b_re'], 'm_s5_b_im': out['m_s5_b_im'], 'm_s5_c_re': out['m_s5_c_re'], 'm_s5_c_im': out['m_s5_c_im'], 'm_s5_d': out['m_s5_d'], 'm_s5_w_glu': out['m_s5_w_glu'], 'm_ab_w_out': out['m_ab_w_out'], 'm_sc_w_in': out['m_sc_w_in'], 'm_sc_conv_w': out['m_sc_conv_w'], 'm_sc_w_out': out['m_sc_w_out'], 'm_final_norm': out['m_final_norm'], 'v_ffn1_norm': out['v_ffn1_norm'], 'v_ffn1_w_gate': out['v_ffn1_w_gate'], 'v_ffn1_w_up': out['v_ffn1_w_up'], 'v_ffn1_w_down': out['v_ffn1_w_down'], 'v_mix_norm': out['v_mix_norm'], 'v_ffn2_norm': out['v_ffn2_norm'], 'v_ffn2_w_gate': out['v_ffn2_w_gate'], 'v_ffn2_w_up': out['v_ffn2_w_up'], 'v_ffn2_w_down': out['v_ffn2_w_down'], 'v_ab_w_in': out['v_ab_w_in'], 'v_s5_lambda_re': out['v_s5_lambda_re'], 'v_s5_lambda_im': out['v_s5_lambda_im'], 'v_s5_log_dt': out['v_s5_log_dt'], 'v_s5_b_re': out['v_s5_b_re'], 'v_s5_b_im': out['v_s5_b_im'], 'v_s5_c_re': out['v_s5_c_re'], 'v_s5_c_im': out['v_s5_c_im'], 'v_s5_d': out['v_s5_d'], 'v_s5_w_glu': out['v_s5_w_glu'], 'v_ab_w_out': out['v_ab_w_out'], 'v_sc_w_in': out['v_sc_w_in'], 'v_sc_conv_w': out['v_sc_conv_w'], 'v_sc_w_out': out['v_sc_w_out'], 'v_final_norm': out['v_final_norm']}


def _loss(weights, diff, rest, loss_target):
    with _jax.named_scope("forward"):
        args = {**rest, TWIN_DIFF_INPUT: diff, **{k: w.astype(_WEIGHT_DTYPES[k]) for k, w in weights.items()}}
        y = _forward(args)
    with _jax.named_scope("loss_head"):
        err = _jnp.square(y.astype(_jnp.float32) - loss_target)
        return 0.5 * _jnp.sum(_jnp.mean(err, axis=-1)) if err.ndim else 0.5 * err


def _adamw(w, g, m, v):
    m = ADAM_B1 * m + (1.0 - ADAM_B1) * g
    v = ADAM_B2 * v + (1.0 - ADAM_B2) * _jnp.square(g)
    m_hat = m / (1.0 - ADAM_B1 ** ADAM_STEP)
    v_hat = v / (1.0 - ADAM_B2 ** ADAM_STEP)
    delta = -ADAM_LR * (m_hat / (_jnp.sqrt(v_hat) + ADAM_EPS) + ADAM_WD * w)
    return delta, m, v


def reference(x, ffn1_norm, ffn1_w_gate, ffn1_w_up, ffn1_w_down, mix_norm, ffn2_norm, ffn2_w_gate, ffn2_w_up, ffn2_w_down, ab_w_in, s5_lambda_re, s5_lambda_im, s5_log_dt, s5_b_re, s5_b_im, s5_c_re, s5_c_im, s5_d, s5_w_glu, ab_w_out, sc_w_in, sc_conv_w, sc_w_out, final_norm, loss_target, m_ffn1_norm, m_ffn1_w_gate, m_ffn1_w_up, m_ffn1_w_down, m_mix_norm, m_ffn2_norm, m_ffn2_w_gate, m_ffn2_w_up, m_ffn2_w_down, m_ab_w_in, m_s5_lambda_re, m_s5_lambda_im, m_s5_log_dt, m_s5_b_re, m_s5_b_im, m_s5_c_re, m_s5_c_im, m_s5_d, m_s5_w_glu, m_ab_w_out, m_sc_w_in, m_sc_conv_w, m_sc_w_out, m_final_norm, v_ffn1_norm, v_ffn1_w_gate, v_ffn1_w_up, v_ffn1_w_down, v_mix_norm, v_ffn2_norm, v_ffn2_w_gate, v_ffn2_w_up, v_ffn2_w_down, v_ab_w_in, v_s5_lambda_re, v_s5_lambda_im, v_s5_log_dt, v_s5_b_re, v_s5_b_im, v_s5_c_re, v_s5_c_im, v_s5_d, v_s5_w_glu, v_ab_w_out, v_sc_w_in, v_sc_conv_w, v_sc_w_out, v_final_norm):
    given = dict(x=x, ffn1_norm=ffn1_norm, ffn1_w_gate=ffn1_w_gate, ffn1_w_up=ffn1_w_up, ffn1_w_down=ffn1_w_down, mix_norm=mix_norm, ffn2_norm=ffn2_norm, ffn2_w_gate=ffn2_w_gate, ffn2_w_up=ffn2_w_up, ffn2_w_down=ffn2_w_down, ab_w_in=ab_w_in, s5_lambda_re=s5_lambda_re, s5_lambda_im=s5_lambda_im, s5_log_dt=s5_log_dt, s5_b_re=s5_b_re, s5_b_im=s5_b_im, s5_c_re=s5_c_re, s5_c_im=s5_c_im, s5_d=s5_d, s5_w_glu=s5_w_glu, ab_w_out=ab_w_out, sc_w_in=sc_w_in, sc_conv_w=sc_conv_w, sc_w_out=sc_w_out, final_norm=final_norm, loss_target=loss_target, m_ffn1_norm=m_ffn1_norm, m_ffn1_w_gate=m_ffn1_w_gate, m_ffn1_w_up=m_ffn1_w_up, m_ffn1_w_down=m_ffn1_w_down, m_mix_norm=m_mix_norm, m_ffn2_norm=m_ffn2_norm, m_ffn2_w_gate=m_ffn2_w_gate, m_ffn2_w_up=m_ffn2_w_up, m_ffn2_w_down=m_ffn2_w_down, m_ab_w_in=m_ab_w_in, m_s5_lambda_re=m_s5_lambda_re, m_s5_lambda_im=m_s5_lambda_im, m_s5_log_dt=m_s5_log_dt, m_s5_b_re=m_s5_b_re, m_s5_b_im=m_s5_b_im, m_s5_c_re=m_s5_c_re, m_s5_c_im=m_s5_c_im, m_s5_d=m_s5_d, m_s5_w_glu=m_s5_w_glu, m_ab_w_out=m_ab_w_out, m_sc_w_in=m_sc_w_in, m_sc_conv_w=m_sc_conv_w, m_sc_w_out=m_sc_w_out, m_final_norm=m_final_norm, v_ffn1_norm=v_ffn1_norm, v_ffn1_w_gate=v_ffn1_w_gate, v_ffn1_w_up=v_ffn1_w_up, v_ffn1_w_down=v_ffn1_w_down, v_mix_norm=v_mix_norm, v_ffn2_norm=v_ffn2_norm, v_ffn2_w_gate=v_ffn2_w_gate, v_ffn2_w_up=v_ffn2_w_up, v_ffn2_w_down=v_ffn2_w_down, v_ab_w_in=v_ab_w_in, v_s5_lambda_re=v_s5_lambda_re, v_s5_lambda_im=v_s5_lambda_im, v_s5_log_dt=v_s5_log_dt, v_s5_b_re=v_s5_b_re, v_s5_b_im=v_s5_b_im, v_s5_c_re=v_s5_c_re, v_s5_c_im=v_s5_c_im, v_s5_d=v_s5_d, v_s5_w_glu=v_s5_w_glu, v_ab_w_out=v_ab_w_out, v_sc_w_in=v_sc_w_in, v_sc_conv_w=v_sc_conv_w, v_sc_w_out=v_sc_w_out, v_final_norm=v_final_norm)
    weights = {n: given[n] for n in TWIN_WEIGHTS}
    shared = {n: given[n] for n in SHARED_INPUTS}
    per_example = {n: given[n] for n in ['x']}
    grad_fn = _jax.value_and_grad(_loss, argnums=(0, 1))

    def one_microbatch(ex, loss_target):
        ex = dict(ex)
        diff = ex.pop(TWIN_DIFF_INPUT)
        return grad_fn(weights, diff, {**shared, **ex}, loss_target)

    if N_MICROBATCH == 1:
        loss, (grad_w, grad_x) = one_microbatch(per_example, given["loss_target"])
    else:
        def body(carry, xs):
            loss_sum, grad_sum = carry
            l_k, (gw_k, gx_k) = one_microbatch(xs[0], xs[1])
            with _jax.named_scope("update"):
                return (loss_sum + l_k, _jax.tree.map(_jnp.add, grad_sum, gw_k)), gx_k

        init = (_jnp.zeros((), _jnp.float32), _jax.tree.map(_jnp.zeros_like, weights))
        (loss, grad_w), grad_x = _jax.lax.scan(body, init, (per_example, given["loss_target"]))
    with _jax.named_scope("update"):
        delta_w, new_m, new_v = {}, {}, {}
        for n in TWIN_WEIGHTS:
            delta_w[n], new_m[n], new_v[n] = _adamw(weights[n], grad_w[n], given["m_" + n], given["v_" + n])
    return (loss, grad_x, *[grad_w[n] for n in TWIN_WEIGHTS], *[delta_w[n] for n in TWIN_WEIGHTS],
            *[new_m[n] for n in TWIN_WEIGHTS], *[new_v[n] for n in TWIN_WEIGHTS])
```

```python
import functools
import math

import jax
import jax.numpy as jnp
from jax import lax
from jax.experimental import pallas as pl
from jax.experimental.pallas import tpu as pltpu

F32 = jnp.float32
MXU_DTYPE = jnp.bfloat16
WIRE_DTYPE = jnp.float32
MESH_ID = pl.DeviceIdType.MESH

D_MODEL = 1024
D_FF = 2752
N_CHIPS = 4
FF_SHARD = D_FF // N_CHIPS
FF_SLOT = 768
FF_PAD = N_CHIPS * FF_SLOT
S5_WIDTH = 512
S5_GROUP = 16
S5_GROUPS = 32
S5_STATE = 64
S5_LANES = S5_GROUPS * S5_STATE
SB_HEADS = 8
SB_DH = 64
EPS = 1e-6
ADAM_LR, ADAM_B1, ADAM_B2, ADAM_EPS, ADAM_WD, ADAM_STEP = 0.001, 0.9, 0.999, 1e-08, 0.01, 10
VMEM_LIMIT = 56 * 1024 * 1024

ANY_SPEC = pl.BlockSpec(memory_space=pl.ANY)


def _params(*sem):
    return pltpu.CompilerParams(dimension_semantics=sem or None, vmem_limit_bytes=VMEM_LIMIT)


def _mm(a, b, *, name, ta=False, tb=False, out_dtype=F32, epilogue=None, extras=(), tm=512, tn=512, tk=512):
    m, k = (a.shape[1], a.shape[0]) if ta else a.shape
    n = b.shape[0] if tb else b.shape[1]
    tm, tn, tk = min(tm, m), min(tn, n), min(tk, k)
    assert m % tm == 0 and n % tn == 0 and k % tk == 0, (name, m, n, k)
    nk = k // tk
    a_spec = pl.BlockSpec((tk, tm), lambda i, j, kk: (kk, i)) if ta else pl.BlockSpec((tm, tk), lambda i, j, kk: (i, kk))
    b_spec = pl.BlockSpec((tn, tk), lambda i, j, kk: (j, kk)) if tb else pl.BlockSpec((tk, tn), lambda i, j, kk: (kk, j))
    ex_specs = []
    for e in extras:
        if e.shape == (m, n):
            ex_specs.append(pl.BlockSpec((tm, tn), lambda i, j, kk: (i, j)))
        elif e.shape == (1, n):
            ex_specs.append(pl.BlockSpec((1, tn), lambda i, j, kk: (0, j)))
        else:
            assert e.shape == (m, 1), (name, e.shape)
            ex_specs.append(pl.BlockSpec((tm, 1), lambda i, j, kk: (i, 0)))
    dims = (((0 if ta else 1,), (1 if tb else 0,)), ((), ()))
    n_ex = len(extras)

    def body(a_ref, b_ref, *rest):
        ex_refs, o_ref, acc_ref = rest[:n_ex], rest[n_ex], rest[n_ex + 1]
        kk = pl.program_id(2)

        @pl.when(kk == 0)
        def _():
            acc_ref[...] = jnp.zeros_like(acc_ref)

        acc_ref[...] += lax.dot_general(a_ref[...].astype(MXU_DTYPE), b_ref[...].astype(MXU_DTYPE), dims,
                                        preferred_element_type=F32)

        @pl.when(kk == nk - 1)
        def _():
            r = acc_ref[...]
            if epilogue is not None:
                r = epilogue(r, *[e[...] for e in ex_refs])
            o_ref[...] = r.astype(out_dtype)

    return pl.pallas_call(
        body, name=name, grid=(m // tm, n // tn, nk),
        in_specs=[a_spec, b_spec, *ex_specs],
        out_specs=pl.BlockSpec((tm, tn), lambda i, j, kk: (i, j)),
        out_shape=jax.ShapeDtypeStruct((m, n), out_dtype),
        scratch_shapes=[pltpu.VMEM((tm, tn), F32)],
        compiler_params=_params("parallel", "parallel", "arbitrary"),
    )(a, b, *extras)


def _row_block(rows, want):
    for tl in range(min(want, rows), 7, -1):
        if rows % tl == 0 and tl % 8 == 0:
            return tl
    return rows


def _rowmap(fn, ins, in_kinds, outs, *, name, tl):
    rows = next(x.shape[0] for x, kd in zip(ins, in_kinds) if kd == "r")
    tl = _row_block(rows, tl)
    n_in = len(ins)

    def spec(shape, kind):
        if kind == "r":
            return pl.BlockSpec((tl,) + tuple(shape[1:]), lambda i: (i,) + (0,) * (len(shape) - 1))
        return pl.BlockSpec(tuple(shape), lambda i: (0,) * len(shape))

    def body(*refs):
        in_refs, out_refs = refs[:n_in], refs[n_in:]
        res = fn(*[r[...] for r in in_refs])
        if not isinstance(res, (tuple, list)):
            res = (res,)
        for o_ref, val, (_, dt, kind) in zip(out_refs, res, outs):
            if kind == "r":
                o_ref[...] = val.astype(dt)
            else:
                @pl.when(pl.program_id(0) == 0)
                def _():
                    o_ref[...] = jnp.zeros_like(o_ref)

                o_ref[...] += val.astype(dt)

    has_acc = any(kd == "a" for _, _, kd in outs)
    res = pl.pallas_call(
        body, name=name, grid=(rows // tl,),
        in_specs=[spec(x.shape, kd) for x, kd in zip(ins, in_kinds)],
        out_specs=[spec(s, kd) for s, _, kd in outs],
        out_shape=[jax.ShapeDtypeStruct(s, dt) for s, dt, _ in outs],
        compiler_params=_params("arbitrary" if has_acc else "parallel"),
    )(*ins)
    return res[0] if len(outs) == 1 else res


def _rms_fwd(x, g):
    r = lax.rsqrt(jnp.mean(x * x, axis=-1, keepdims=True) + EPS)
    return x * r * g


def _rms_bwd(dh, x, g):
    r = lax.rsqrt(jnp.mean(x * x, axis=-1, keepdims=True) + EPS)
    xh = x * r
    dxh = dh * g
    dx = r * (dxh - xh * jnp.mean(dxh * xh, axis=-1, keepdims=True))
    return dx, jnp.sum(dh * xh, axis=0, keepdims=True)


def _norm(x, g, *, name):
    return _rowmap(lambda xv, gv: _rms_fwd(xv, gv), [x, g], "rc", [(x.shape, MXU_DTYPE, "r")], name=name, tl=256)


def _norm_bwd(dh, x, g, dres, *, name):
    def fn(dhv, xv, gv, drv):
        dx, dg = _rms_bwd(dhv, xv, gv)
        return dx + drv, dg
    return _rowmap(fn, [dh, x, g, dres], "rrcr", [(x.shape, F32, "r"), (g.shape, F32, "a")], name=name, tl=256)


def _swiglu_act(a, b):
    return jax.nn.silu(a) * b


def _ffn_fwd(x, g, wg, wu, wd, tag):
    h = _norm(x, g, name=f"{tag}_norm")
    a = _mm(h, wg, name=f"{tag}_gate")
    b = _mm(h, wu, name=f"{tag}_up")
    s = _rowmap(_swiglu_act, [a, b], "rr", [(a.shape, MXU_DTYPE, "r")], name=f"{tag}_act", tl=128)
    x2 = _mm(s, wd, name=f"{tag}_down", epilogue=lambda acc, xv: xv + 0.5 * acc, extras=[x])
    return x2, (x, h, a, b, s)


def _ffn_bwd(dx2, saved, g, wg, wu, wd, tag):
    x, h, a, b, s = saved
    ds = _mm(dx2, wd, tb=True, name=f"{tag}_dact")

    def act_bwd(dsv, av, bv):
        _, vjp = jax.vjp(_swiglu_act, av, bv)
        return vjp(0.5 * dsv)

    da, db = _rowmap(act_bwd, [ds, a, b], "rrr", [(a.shape, MXU_DTYPE, "r")] * 2, name=f"{tag}_act_bwd", tl=128)
    dwd = _mm(s, dx2, ta=True, name=f"{tag}_dwd", out_dtype=WIRE_DTYPE, epilogue=lambda acc: 0.5 * acc)
    dwg = _mm(h, da, ta=True, name=f"{tag}_dwg", out_dtype=WIRE_DTYPE)
    dwu = _mm(h, db, ta=True, name=f"{tag}_dwu", out_dtype=WIRE_DTYPE)
    dh = _mm(da, wg, tb=True, name=f"{tag}_dh1")
    dh = _mm(db, wu, tb=True, name=f"{tag}_dh2", epilogue=lambda acc, prev: acc + prev, extras=[dh])
    dx, dg = _norm_bwd(dh, x, g, dx2, name=f"{tag}_norm_bwd")
    return dx, (dg, dwg, dwu, dwd)


def _softplus(z):
    return jnp.maximum(z, 0.0) + jnp.log(1.0 + jnp.exp(-jnp.abs(z)))


def _ones_dot(x, tri):
    if MXU_DTYPE == F32:
        return jnp.dot(x, tri, preferred_element_type=F32)
    hi = x.astype(MXU_DTYPE)
    lo = (x - hi.astype(F32)).astype(MXU_DTYPE)
    return jnp.dot(hi, tri, preferred_element_type=F32) + jnp.dot(lo, tri, preferred_element_type=F32)


NT_DIMS = (((1,), (1,)), ((), ()))
TN_DIMS = (((0,), (0,)), ((), ()))


def _sb_fwd(q, k, v, *, tq=256):
    nh, seq, dh = q.shape
    tq = min(tq, seq)
    scale = 1.0 / math.sqrt(dh)

    def body(q_ref, k_ref, v_ref, o_ref, ls_ref):
        i = pl.program_id(1)
        qv = q_ref[0]
        r_idx = lax.broadcasted_iota(jnp.int32, (tq, tq), 0)
        c_idx = lax.broadcasted_iota(jnp.int32, (tq, tq), 1)
        after = (r_idx > c_idx).astype(MXU_DTYPE)

        def step(n, carry):
            c, acc = carry
            j = i - n
            off = pl.multiple_of(j * tq, tq)
            kv = k_ref[0, pl.ds(off, tq), :]
            vv = v_ref[0, pl.ds(off, tq), :]
            z = lax.dot_general(qv, kv, NT_DIMS, preferred_element_type=F32) * scale
            mask = (j * tq + c_idx) < (i * tq + r_idx)
            sp = _softplus(z)
            lk = jnp.where(mask, -sp, 0.0)
            later = _ones_dot(lk, after) + c
            w = jnp.where(mask, jnp.exp(z - sp + later), 0.0)
            acc = acc + jnp.dot(w.astype(MXU_DTYPE), vv, preferred_element_type=F32)
            return c + jnp.sum(lk, axis=1, keepdims=True), acc

        c, acc = lax.fori_loop(0, i + 1, step, (jnp.zeros((tq, 1), F32), jnp.zeros((tq, dh), F32)))
        o_ref[0] = acc
        ls_ref[0] = c

    whole = pl.BlockSpec((1, seq, dh), lambda h, i: (h, 0, 0))
    return pl.pallas_call(
        body, name="sb_fwd", grid=(nh, seq // tq),
        in_specs=[pl.BlockSpec((1, tq, dh), lambda h, i: (h, i, 0)), whole, whole],
        out_specs=[pl.BlockSpec((1, tq, dh), lambda h, i: (h, i, 0)), pl.BlockSpec((1, tq, 1), lambda h, i: (h, i, 0))],
        out_shape=[jax.ShapeDtypeStruct((nh, seq, dh), F32), jax.ShapeDtypeStruct((nh, seq, 1), F32)],
        compiler_params=_params("parallel", "parallel"),
    )(q, k, v)


def _sb_bwd(q, k, v, lsum, do, *, tq=256):
    nh, seq, dh = q.shape
    tq = min(tq, seq)
    scale = 1.0 / math.sqrt(dh)

    def body(q_ref, k_ref, v_ref, ls_ref, do_ref, dq_ref, dk_ref, dv_ref):
        i = pl.program_id(1)

        @pl.when(i == 0)
        def _():
            dk_ref[...] = jnp.zeros_like(dk_ref)
            dv_ref[...] = jnp.zeros_like(dv_ref)

        qv = q_ref[0]
        dov = do_ref[0].astype(MXU_DTYPE)
        total = ls_ref[0]
        r_idx = lax.broadcasted_iota(jnp.int32, (tq, tq), 0)
        c_idx = lax.broadcasted_iota(jnp.int32, (tq, tq), 1)
        upto = (r_idx <= c_idx).astype(MXU_DTYPE)
        before = (r_idx < c_idx).astype(MXU_DTYPE)

        def step(j, carry):
            cp, ce, dq = carry
            off = pl.multiple_of(j * tq, tq)
            kv = k_ref[0, pl.ds(off, tq), :]
            vv = v_ref[0, pl.ds(off, tq), :]
            z = lax.dot_general(qv, kv, NT_DIMS, preferred_element_type=F32) * scale
            mask = (j * tq + c_idx) < (i * tq + r_idx)
            sp = _softplus(z)
            lk = jnp.where(mask, -sp, 0.0)
            later = total - (_ones_dot(lk, upto) + cp)
            w = jnp.where(mask, jnp.exp(z - sp + later), 0.0)
            dw = lax.dot_general(dov, vv, NT_DIMS, preferred_element_type=F32)
            e = w * dw
            earlier = _ones_dot(e, before) + ce
            keep = jnp.exp(-sp)
            dz = jnp.where(mask, e * keep - (1.0 - keep) * earlier, 0.0) * scale
            dzm = dz.astype(MXU_DTYPE)
            dq = dq + jnp.dot(dzm, kv, preferred_element_type=F32)
            dk_ref[0, pl.ds(off, tq), :] += lax.dot_general(dzm, qv, TN_DIMS, preferred_element_type=F32)
            dv_ref[0, pl.ds(off, tq), :] += lax.dot_general(w.astype(MXU_DTYPE), dov, TN_DIMS, preferred_element_type=F32)
            return cp + jnp.sum(lk, axis=1, keepdims=True), ce + jnp.sum(e, axis=1, keepdims=True), dq

        zero = jnp.zeros((tq, 1), F32)
        _, _, dq = lax.fori_loop(0, i + 1, step, (zero, zero, jnp.zeros((tq, dh), F32)))
        dq_ref[0] = dq

    whole = pl.BlockSpec((1, seq, dh), lambda h, i: (h, 0, 0))
    tile = pl.BlockSpec((1, tq, dh), lambda h, i: (h, i, 0))
    return pl.pallas_call(
        body, name="sb_bwd", grid=(nh, seq // tq),
        in_specs=[tile, whole, whole, pl.BlockSpec((1, tq, 1), lambda h, i: (h, i, 0)), tile],
        out_specs=[tile, whole, whole],
        out_shape=[jax.ShapeDtypeStruct((nh, seq, dh), F32)] * 3,
        compiler_params=_params("parallel", "arbitrary"),
    )(q, k, v, lsum, do)


def _heads(t):
    return t.reshape(t.shape[0], SB_HEADS, SB_DH).transpose(1, 0, 2)


def _unheads(t):
    return t.transpose(1, 0, 2).reshape(t.shape[1], SB_HEADS * SB_DH)


def _s5_disc(lr, li, ldt, br, bi):
    dt = jnp.exp(ldt)
    mag = jnp.exp(lr * dt)
    ar = mag * jnp.cos(li * dt)
    ai = mag * jnp.sin(li * dt)
    den = lr * lr + li * li
    nr = ar - 1.0
    cr = (nr * lr + ai * li) / den
    ci = (ai * lr - nr * li) / den
    return ar, ai, cr[None] * br - ci[None] * bi, cr[None] * bi + ci[None] * br


def _s5_prep(lr, li, ldt, br, bi):
    shapes = [lr.shape, lr.shape, br.shape, br.shape]

    def body(lr_ref, li_ref, ldt_ref, br_ref, bi_ref, *outs):
        for o, val in zip(outs, _s5_disc(lr_ref[...], li_ref[...], ldt_ref[...], br_ref[...], bi_ref[...])):
            o[...] = val

    return pl.pallas_call(body, name="s5_prep", out_shape=[jax.ShapeDtypeStruct(s, F32) for s in shapes])(lr, li, ldt, br, bi)


def _s5_prep_bwd(lr, li, ldt, br, bi, cts):
    args = (lr, li, ldt, br, bi)

    def body(*refs):
        ins, ct_refs, outs = refs[:5], refs[5:9], refs[9:]
        _, vjp = jax.vjp(_s5_disc, *[r[...] for r in ins])
        for o, val in zip(outs, vjp(tuple(r[...] for r in ct_refs))):
            o[...] = val

    return pl.pallas_call(body, name="s5_prep_bwd", out_shape=[jax.ShapeDtypeStruct(a.shape, F32) for a in args])(*args, *cts)


def _s5_scan(bu, a, *, tc=512, tw=512):
    seq, w2 = bu.shape
    w = w2 // 2
    tc, tw = min(tc, seq), min(tw, w)
    nw = w // tw

    def body(bur_ref, bui_ref, ar_ref, ai_ref, hr_ref, hi_ref, cr_ref, ci_ref):
        @pl.when(pl.program_id(1) == 0)
        def _():
            cr_ref[...] = jnp.zeros_like(cr_ref)
            ci_ref[...] = jnp.zeros_like(ci_ref)

        ar, ai = ar_ref[...], ai_ref[...]

        def step(t, carry):
            hr, hi = carry
            nr = ar * hr - ai * hi + bur_ref[pl.ds(t, 1), :]
            ni = ar * hi + ai * hr + bui_ref[pl.ds(t, 1), :]
            hr_ref[pl.ds(t, 1), :] = nr
            hi_ref[pl.ds(t, 1), :] = ni
            return nr, ni

        hr, hi = lax.fori_loop(0, tc, step, (cr_ref[...], ci_ref[...]), unroll=8)
        cr_ref[...] = hr
        ci_ref[...] = hi

    re = pl.BlockSpec((tc, tw), lambda j, t: (t, j))
    im = pl.BlockSpec((tc, tw), lambda j, t: (t, nw + j))
    return pl.pallas_call(
        body, name="s5_scan", grid=(nw, seq // tc),
        in_specs=[re, im, pl.BlockSpec((1, tw), lambda j, t: (0, j)), pl.BlockSpec((1, tw), lambda j, t: (0, nw + j))],
        out_specs=[re, re],
        out_shape=[jax.ShapeDtypeStruct((seq, w), F32)] * 2,
        scratch_shapes=[pltpu.VMEM((1, tw), F32)] * 2,
        compiler_params=_params("parallel", "arbitrary"),
    )(bu, bu, a, a)


def _s5_scan_bwd(d, hr, hi, a, *, tc=512, tw=512):
    seq, w2 = d.shape
    w = w2 // 2
    tc, tw = min(tc, seq), min(tw, w)
    nw, nt = w // tw, seq // tc

    def body(dr_ref, di_ref, hr_ref, hi_ref, ar_ref, ai_ref, gr_ref, gi_ref, dar_ref, dai_ref, cr_ref, ci_ref):
        @pl.when(pl.program_id(1) == 0)
        def _():
            cr_ref[...] = jnp.zeros_like(cr_ref)
            ci_ref[...] = jnp.zeros_like(ci_ref)
            dar_ref[...] = jnp.zeros_like(dar_ref)
            dai_ref[...] = jnp.zeros_like(dai_ref)

        ar, ai = ar_ref[...], ai_ref[...]

        def step(n, carry):
            gr, gi, sr, si = carry
            t = tc - 1 - n
            hrt, hit = hr_ref[pl.ds(t, 1), :], hi_ref[pl.ds(t, 1), :]
            sr = sr + gr * hrt + gi * hit
            si = si + gi * hrt - gr * hit
            ngr = dr_ref[pl.ds(t, 1), :] + ar * gr + ai * gi
            ngi = di_ref[pl.ds(t, 1), :] + ar * gi - ai * gr
            gr_ref[pl.ds(t, 1), :] = ngr
            gi_ref[pl.ds(t, 1), :] = ngi
            return ngr, ngi, sr, si

        gr, gi, sr, si = lax.fori_loop(0, tc, step, (cr_ref[...], ci_ref[...], dar_ref[...], dai_ref[...]), unroll=8)
        cr_ref[...] = gr
        ci_ref[...] = gi
        dar_ref[...] = sr
        dai_ref[...] = si

    re = pl.BlockSpec((tc, tw), lambda j, t: (nt - 1 - t, j))
    im = pl.BlockSpec((tc, tw), lambda j, t: (nt - 1 - t, nw + j))
    row = pl.BlockSpec((1, tw), lambda j, t: (0, j))
    return pl.pallas_call(
        body, name="s5_scan_bwd", grid=(nw, nt),
        in_specs=[re, im, re, re, row, pl.BlockSpec((1, tw), lambda j, t: (0, nw + j))],
        out_specs=[re, re, row, row],
        out_shape=[jax.ShapeDtypeStruct((seq, w), F32)] * 2 + [jax.ShapeDtypeStruct((1, w), F32)] * 2,
        scratch_shapes=[pltpu.VMEM((1, tw), F32)] * 2,
        compiler_params=_params("parallel", "arbitrary"),
    )(d, d, hr, hi, a, a)


def _block_diag(t):
    g, a, b = t.shape
    eye = jnp.eye(g, dtype=t.dtype)
    return (t[:, :, None, :] * eye[:, None, :, None]).reshape(g * a, g * b)


def _block_diag_part(m, g):
    a, b = m.shape[0] // g, m.shape[1] // g
    return jnp.moveaxis(jnp.diagonal(m.reshape(g, a, g, b), axis1=0, axis2=2), -1, 0)


def _gelu_glu(y, gate_pre):
    z = jax.nn.gelu(y)
    return z * jax.nn.sigmoid(gate_pre)


def _s5_fwd(u, p, w_glu):
    lr, li = p["s5_lambda_re"][0], p["s5_lambda_im"][0]
    ldt = p["s5_log_dt"][0][:, None]
    br = p["s5_b_re"][0].transpose(2, 0, 1)
    bi = p["s5_b_im"][0].transpose(2, 0, 1)
    ar, ai, bbr, bbi = _s5_prep(lr, li, ldt, br, bi)
    a = jnp.concatenate([ar.reshape(1, S5_LANES), ai.reshape(1, S5_LANES)], axis=1)
    bmat = jnp.concatenate([_block_diag(bbr.transpose(1, 0, 2)), _block_diag(bbi.transpose(1, 0, 2))], axis=1)
    cmat = jnp.concatenate([_block_diag(p["s5_c_re"][0].transpose(0, 2, 1)),
                            -_block_diag(p["s5_c_im"][0].transpose(0, 2, 1))], axis=0)
    bmat, cmat = bmat.astype(MXU_DTYPE), cmat.astype(MXU_DTYPE)
    bu = _mm(u, bmat, name="s5_bu")
    hr, hi = _s5_scan(bu, a)
    h = jnp.concatenate([hr, hi], axis=1)
    d = p["s5_d"]
    y = _mm(h, cmat, name="s5_y", epilogue=lambda acc, uv, dv: acc + dv * uv, extras=[u, d])
    z = _rowmap(jax.nn.gelu, [y], "r", [(y.shape, MXU_DTYPE, "r")], name="s5_gelu", tl=512)
    gate_pre = _mm(z, w_glu, name="s5_glu")
    out = _rowmap(_gelu_glu, [y, gate_pre], "rr", [(y.shape, F32, "r")], name="s5_gate", tl=512)
    return out, (u, lr, li, ldt, br, bi, a, bmat, cmat, hr, hi, h, y, z, gate_pre)


def _s5_bwd(dout, saved, p, w_glu):
    u, lr, li, ldt, br, bi, a, bmat, cmat, hr, hi, h, y, z, gate_pre = saved
    d = p["s5_d"]

    def gate_bwd(dov, yv, gv):
        zv = jax.nn.gelu(yv)
        sg = jax.nn.sigmoid(gv)
        return dov * sg, dov * zv * sg * (1.0 - sg)

    dz_direct, dgate = _rowmap(gate_bwd, [dout, y, gate_pre], "rrr", [(y.shape, F32, "r"), (y.shape, MXU_DTYPE, "r")],
                               name="s5_gate_bwd", tl=512)
    dw_glu = _mm(z, dgate, ta=True, name="s5_dwglu", out_dtype=WIRE_DTYPE)
    dz = _mm(dgate, w_glu, tb=True, name="s5_dz", epilogue=lambda acc, prev: acc + prev, extras=[dz_direct])

    def gelu_bwd(dzv, yv, uv, dvv):
        _, vjp = jax.vjp(jax.nn.gelu, yv)
        dy = vjp(dzv)[0]
        return dy, dy * dvv, jnp.sum(dy * uv, axis=0, keepdims=True)

    dy, du_skip, dd = _rowmap(gelu_bwd, [dz, y, u, d], "rrrc",
                              [(y.shape, F32, "r"), (y.shape, F32, "r"), (d.shape, F32, "a")], name="s5_gelu_bwd", tl=512)
    dcmat = _mm(h, dy, ta=True, name="s5_dc")
    dstate = _mm(dy, cmat, tb=True, name="s5_dstate")
    gr, gi, dar, dai = _s5_scan_bwd(dstate, hr, hi, a)
    g = jnp.concatenate([gr, gi], axis=1)
    du = _mm(g, bmat, tb=True, name="s5_du", epilogue=lambda acc, prev: acc + prev, extras=[du_skip])
    dbmat = _mm(u, g, ta=True, name="s5_db")
    dbbr = _block_diag_part(dbmat[:, :S5_LANES], S5_GROUPS).transpose(1, 0, 2)
    dbbi = _block_diag_part(dbmat[:, S5_LANES:], S5_GROUPS).transpose(1, 0, 2)
    cts = (dar.reshape(S5_GROUPS, S5_STATE), dai.reshape(S5_GROUPS, S5_STATE), dbbr, dbbi)
    dlr, dli, dldt, dbr, dbi = _s5_prep_bwd(lr, li, ldt, br, bi, cts)
    grads = {
        "s5_lambda_re": dlr[None], "s5_lambda_im": dli[None], "s5_log_dt": dldt[:, 0][None],
        "s5_b_re": dbr.transpose(1, 2, 0)[None], "s5_b_im": dbi.transpose(1, 2, 0)[None],
        "s5_c_re": _block_diag_part(dcmat[:S5_LANES], S5_GROUPS).transpose(0, 2, 1)[None],
        "s5_c_im": -_block_diag_part(dcmat[S5_LANES:], S5_GROUPS).transpose(0, 2, 1)[None],
        "s5_d": dd,
    }
    return du, dw_glu, grads


def _mix0_fwd(x, g, p, w_in, w_glu, w_out):
    h = _norm(x, g, name="mix0_norm")
    proj = _mm(h, w_in, name="mix0_in")
    u = proj[:, :S5_WIDTH]
    q, k, v = (_heads(proj[:, S5_WIDTH * (1 + n):S5_WIDTH * (2 + n)]).astype(MXU_DTYPE) for n in range(3))
    y_a, s5_saved = _s5_fwd(u, p, w_glu)
    o, lsum = _sb_fwd(q, k, v)
    mix = jnp.concatenate([y_a, _unheads(o)], axis=1).astype(MXU_DTYPE)
    x2 = _mm(mix, w_out, name="mix0_out", epilogue=lambda acc, xv: xv + acc, extras=[x])
    return x2, (x, h, q, k, v, lsum, mix, s5_saved)


def _mix0_bwd(dx2, saved, g, p, w_in, w_glu, w_out):
    x, h, q, k, v, lsum, mix, s5_saved = saved
    dmix = _mm(dx2, w_out, tb=True, name="mix0_dmix")
    dw_out = _mm(mix, dx2, ta=True, name="mix0_dwout", out_dtype=WIRE_DTYPE)
    du, dw_glu, s5_grads = _s5_bwd(dmix[:, :S5_WIDTH], s5_saved, p, w_glu)
    dq, dk, dv = _sb_bwd(q, k, v, lsum, _heads(dmix[:, S5_WIDTH:]))
    dproj = jnp.concatenate([du, _unheads(dq), _unheads(dk), _unheads(dv)], axis=1)
    dw_in = _mm(h, dproj, ta=True, name="mix0_dwin", out_dtype=WIRE_DTYPE)
    dh = _mm(dproj, w_in, tb=True, name="mix0_dh")
    dx, dg = _norm_bwd(dh, x, g, dx2, name="mix0_norm_bwd")
    return dx, dg, dw_in, dw_glu, dw_out, s5_grads


def _shift_down(t, n):
    rows = lax.broadcasted_iota(jnp.int32, t.shape, 0)
    return jnp.where(rows >= n, pltpu.roll(t, n, 0), 0.0)


def _shift_up(t, n):
    rows = lax.broadcasted_iota(jnp.int32, t.shape, 0)
    return jnp.where(rows < t.shape[0] - n, pltpu.roll(t, t.shape[0] - n, 0), 0.0)


def _conv_fwd(proj, cw, *, tc=128):
    seq, c3 = proj.shape
    ch = c3 // 3
    nb = ch // tc

    def body(b_ref, c_ref, v_ref, w_ref, m_ref):
        pv = c_ref[...] * v_ref[...]
        w = w_ref[...]
        y = w[2:3] * pv + w[1:2] * _shift_down(pv, 1) + w[0:1] * _shift_down(pv, 2)
        m_ref[...] = (b_ref[...] * y).astype(m_ref.dtype)

    col = lambda part: pl.BlockSpec((seq, tc), lambda j: (0, part * nb + j))
    return pl.pallas_call(
        body, name="conv_fwd", grid=(nb,),
        in_specs=[col(0), col(1), col(2), pl.BlockSpec((3, tc), lambda j: (0, j))],
        out_specs=pl.BlockSpec((seq, tc), lambda j: (0, j)),
        out_shape=jax.ShapeDtypeStruct((seq, ch), MXU_DTYPE),
        compiler_params=_params("parallel"),
    )(proj, proj, proj, cw)


def _conv_bwd(proj, cw, dm, *, tc=128):
    seq, c3 = proj.shape
    ch = c3 // 3
    nb = ch // tc

    def body(b_ref, c_ref, v_ref, w_ref, dm_ref, db_ref, dc_ref, dv_ref, dw_ref):
        cv, vv, dmv = c_ref[...], v_ref[...], dm_ref[...]
        pv = cv * vv
        w = w_ref[...]
        p1, p2 = _shift_down(pv, 1), _shift_down(pv, 2)
        y = w[2:3] * pv + w[1:2] * p1 + w[0:1] * p2
        db_ref[...] = dmv * y
        dy = dmv * b_ref[...]
        dp = w[2:3] * dy + w[1:2] * _shift_up(dy, 1) + w[0:1] * _shift_up(dy, 2)
        dc_ref[...] = dp * vv
        dv_ref[...] = dp * cv
        dw_ref[...] = jnp.concatenate([jnp.sum(dy * p2, axis=0, keepdims=True), jnp.sum(dy * p1, axis=0, keepdims=True),
                                       jnp.sum(dy * pv, axis=0, keepdims=True)], axis=0)

    col = lambda part: pl.BlockSpec((seq, tc), lambda j: (0, part * nb + j))
    one = pl.BlockSpec((seq, tc), lambda j: (0, j))
    small = pl.BlockSpec((3, tc), lambda j: (0, j))
    return pl.pallas_call(
        body, name="conv_bwd", grid=(nb,),
        in_specs=[col(0), col(1), col(2), small, one],
        out_specs=[one, one, one, small],
        out_shape=[jax.ShapeDtypeStruct((seq, ch), F32)] * 3 + [jax.ShapeDtypeStruct((3, ch), F32)],
        compiler_params=_params("parallel"),
    )(proj, proj, proj, cw, dm)


def _mix1_fwd(x, g, w_in, cw, w_out):
    h = _norm(x, g, name="mix1_norm")
    proj = _mm(h, w_in, name="mix1_in")
    m = _conv_fwd(proj, cw)
    x2 = _mm(m, w_out, name="mix1_out", epilogue=lambda acc, xv: xv + acc, extras=[x])
    return x2, (x, h, proj, m)


def _mix1_bwd(dx2, saved, g, w_in, cw, w_out):
    x, h, proj, m = saved
    dm = _mm(dx2, w_out, tb=True, name="mix1_dm")
    dw_out = _mm(m, dx2, ta=True, name="mix1_dwout", out_dtype=WIRE_DTYPE)
    db, dc, dv, dcw = _conv_bwd(proj, cw, dm)
    dproj = jnp.concatenate([db, dc, dv], axis=1)
    dw_in = _mm(h, dproj, ta=True, name="mix1_dwin", out_dtype=WIRE_DTYPE)
    dh = _mm(dproj, w_in, tb=True, name="mix1_dh")
    dx, dg = _norm_bwd(dh, x, g, dx2, name="mix1_norm_bwd")
    return dx, dg, dw_in, dcw, dw_out


def _loss_head(x, g, target):
    feat = x.shape[1]

    def fn(xv, gv, tv):
        err = _rms_fwd(xv, gv) - tv
        dx, dg = _rms_bwd(err / feat, xv, gv)
        return jnp.sum(err * err, keepdims=True) * (0.5 / feat), dx, dg

    return _rowmap(fn, [x, g, target], "rcr", [((1, 1), F32, "a"), (x.shape, F32, "r"), (g.shape, F32, "a")],
                   name="loss_head", tl=256)


def _window(ref, place, chip):
    axis, stride, width = place
    idx = [slice(None)] * len(ref.shape)
    idx[axis] = pl.ds(pl.multiple_of(chip * stride, 8), width)
    return ref.at[tuple(idx)]


def _other_chips(x, y):
    return [(1 - x, y), (x, 1 - y), (1 - x, 1 - y)]


def _gather_weights(shards, places, full_shapes):
    n = len(shards)
    zeros = [jnp.zeros(s, sh.dtype) for s, sh in zip(full_shapes, shards)]

    def body(*refs):
        ins, outs = refs[:n], refs[2 * n:3 * n]
        send, recv, own = refs[3 * n:]
        x, y, c = lax.axis_index("x"), lax.axis_index("y"), lax.axis_index("c")
        copies = []
        for a in range(n):
            mine = _window(outs[a], places[a], 2 * x + y)
            local = pltpu.make_async_copy(ins[a], mine, own.at[a])
            local.start()
            copies.append(local)
            for r, (px, py) in enumerate(_other_chips(x, y)):
                cp = pltpu.make_async_remote_copy(src_ref=ins[a], dst_ref=mine, send_sem=send.at[3 * a + r],
                                                  recv_sem=recv.at[3 * a + r], device_id=(px, py, c), device_id_type=MESH_ID)
                cp.start()
                copies.append(cp)
        for cp in copies:
            cp.wait()

    return pl.pallas_call(
        body, name="gather_weights",
        in_specs=[ANY_SPEC] * (2 * n), out_specs=[ANY_SPEC] * n,
        out_shape=[jax.ShapeDtypeStruct(s, sh.dtype) for s, sh in zip(full_shapes, shards)],
        input_output_aliases={n + a: a for a in range(n)},
        scratch_shapes=[pltpu.SemaphoreType.DMA((3 * n,)), pltpu.SemaphoreType.DMA((3 * n,)), pltpu.SemaphoreType.DMA((n,))],
        compiler_params=pltpu.CompilerParams(has_side_effects=True),
    )(*shards, *zeros)


def _scatter_grads(grads, places, shard_shapes):
    n = len(grads)

    def body(*refs):
        ins, outs = refs[:n], refs[n:2 * n]
        send, recv, own = refs[2 * n:]
        x, y, c = lax.axis_index("x"), lax.axis_index("y"), lax.axis_index("c")
        copies = []
        for a in range(n):
            local = pltpu.make_async_copy(_window(ins[a], places[a], 2 * x + y), outs[a].at[3], own.at[a])
            local.start()
            copies.append(local)
            for r, (px, py) in enumerate(_other_chips(x, y)):
                cp = pltpu.make_async_remote_copy(src_ref=_window(ins[a], places[a], 2 * px + py), dst_ref=outs[a].at[r],
                                                  send_sem=send.at[3 * a + r], recv_sem=recv.at[3 * a + r],
                                                  device_id=(px, py, c), device_id_type=MESH_ID)
                cp.start()
                copies.append(cp)
        for cp in copies:
            cp.wait()

    return pl.pallas_call(
        body, name="scatter_grads",
        in_specs=[ANY_SPEC] * n, out_specs=[ANY_SPEC] * n,
        out_shape=[jax.ShapeDtypeStruct((N_CHIPS,) + tuple(s), g.dtype) for s, g in zip(shard_shapes, grads)],
        scratch_shapes=[pltpu.SemaphoreType.DMA((3 * n,)), pltpu.SemaphoreType.DMA((3 * n,)), pltpu.SemaphoreType.DMA((n,))],
        compiler_params=pltpu.CompilerParams(has_side_effects=True),
    )(*grads)


def _swap_with_sibling(parts):
    n = len(parts)

    def body(*refs):
        ins, outs = refs[:n], refs[n:2 * n]
        send, recv = refs[2 * n:]
        sibling = (lax.axis_index("x"), lax.axis_index("y"), 1 - lax.axis_index("c"))
        copies = [pltpu.make_async_remote_copy(src_ref=ins[a], dst_ref=outs[a], send_sem=send.at[a], recv_sem=recv.at[a],
                                               device_id=sibling, device_id_type=MESH_ID) for a in range(n)]
        for cp in copies:
            cp.start()
        for cp in copies:
            cp.wait()

    return pl.pallas_call(
        body, name="swap_with_sibling",
        in_specs=[ANY_SPEC] * n, out_specs=[ANY_SPEC] * n,
        out_shape=[jax.ShapeDtypeStruct(p.shape, p.dtype) for p in parts],
        scratch_shapes=[pltpu.SemaphoreType.DMA((n,)), pltpu.SemaphoreType.DMA((n,))],
        compiler_params=pltpu.CompilerParams(has_side_effects=True),
    )(*parts)


def _sum_all_devices(t):
    rows = t.shape[0]

    def body(t_ref, o_ref, slots, send, recv):
        x, y, c = lax.axis_index("x"), lax.axis_index("y"), lax.axis_index("c")
        me = 4 * x + 2 * y + c
        slots[me] = t_ref[...]
        copies = []
        for m in range(1, 8):
            peer = (x ^ (m >> 2), y ^ ((m >> 1) & 1), c ^ (m & 1))
            cp = pltpu.make_async_remote_copy(src_ref=t_ref, dst_ref=slots.at[me], send_sem=send.at[m - 1],
                                              recv_sem=recv.at[m - 1], device_id=peer, device_id_type=MESH_ID)
            cp.start()
            copies.append(cp)
        for cp in copies:
            cp.wait()
        acc = slots[0]
        for dev in range(1, 8):
            acc = acc + slots[dev]
        o_ref[...] = acc

    vmem = pl.BlockSpec(memory_space=pltpu.VMEM)
    return pl.pallas_call(
        body, name="sum_all_devices", in_specs=[vmem], out_specs=vmem,
        out_shape=jax.ShapeDtypeStruct(t.shape, F32),
        scratch_shapes=[pltpu.VMEM((8, rows, 128), F32), pltpu.SemaphoreType.DMA((7,)), pltpu.SemaphoreType.DMA((7,))],
        compiler_params=pltpu.CompilerParams(vmem_limit_bytes=VMEM_LIMIT, has_side_effects=True),
    )(t)


def _adamw(w, g, m, v):
    m = ADAM_B1 * m + (1.0 - ADAM_B1) * g
    v = ADAM_B2 * v + (1.0 - ADAM_B2) * jnp.square(g)
    m_hat = m / (1.0 - ADAM_B1 ** ADAM_STEP)
    v_hat = v / (1.0 - ADAM_B2 ** ADAM_STEP)
    return -ADAM_LR * (m_hat / (jnp.sqrt(v_hat) + ADAM_EPS) + ADAM_WD * w), m, v


def _chip_sum(received, name):
    rows, cols = received.shape[1:]
    tl = _row_block(rows, 512)

    def body(r_ref, o_ref):
        o_ref[...] = ((r_ref[0].astype(F32) + r_ref[1].astype(F32)) + r_ref[2].astype(F32)) + r_ref[3].astype(F32)

    return pl.pallas_call(body, name=name, grid=(rows // tl,),
                          in_specs=[pl.BlockSpec((N_CHIPS, tl, cols), lambda i: (0, i, 0))],
                          out_specs=pl.BlockSpec((tl, cols), lambda i: (i, 0)),
                          out_shape=jax.ShapeDtypeStruct((rows, cols), F32), compiler_params=_params("parallel"))(received)


def _adamw_pair(w, p_mine, p_other, m, v, name):
    def fn(wv, pa, pb, mv, vv):
        g = pa + pb
        return (g,) + _adamw(wv, g, mv, vv)

    return _rowmap(fn, [w, p_mine, p_other, m, v], "rrrrr", [(w.shape, F32, "r")] * 4, name=name, tl=256)


def _adamw_small(w, g, m, v):
    def fn(wv, gv, mv, vv):
        return _adamw(wv, gv, mv, vv)

    return _rowmap(fn, [w, g, m, v], "rrrr", [(w.shape, F32, "r")] * 3, name="adamw_small", tl=w.shape[0])


WEIGHTS = ['ffn1_norm', 'ffn1_w_gate', 'ffn1_w_up', 'ffn1_w_down', 'mix_norm', 'ffn2_norm', 'ffn2_w_gate', 'ffn2_w_up',
           'ffn2_w_down', 'ab_w_in', 's5_lambda_re', 's5_lambda_im', 's5_log_dt', 's5_b_re', 's5_b_im', 's5_c_re', 's5_c_im',
           's5_d', 's5_w_glu', 'ab_w_out', 'sc_w_in', 'sc_conv_w', 'sc_w_out', 'final_norm']
SHARDED = {'ffn1_w_gate': (1, FF_SLOT), 'ffn1_w_up': (1, FF_SLOT), 'ffn1_w_down': (0, FF_SLOT),
           'ffn2_w_gate': (1, FF_SLOT), 'ffn2_w_up': (1, FF_SLOT), 'ffn2_w_down': (0, FF_SLOT),
           'ab_w_in': (1, 512), 's5_w_glu': (0, 128), 'ab_w_out': (0, 256), 'sc_w_in': (1, 768), 'sc_conv_w': (1, 256),
           'sc_w_out': (0, 256)}
SMALL = [n for n in WEIGHTS if n not in SHARDED]


def _pack(arrays):
    rows = []
    for t in arrays:
        flat = t.reshape(-1)
        rows.append(jnp.pad(flat, (0, (-flat.shape[0]) % 128)))
    flat = jnp.concatenate(rows)
    return jnp.pad(flat, (0, (-flat.shape[0]) % 1024)).reshape(-1, 128)


def _unpack(packed, like):
    flat, out, pos = packed.reshape(-1), [], 0
    for t in like:
        out.append(flat[pos:pos + t.size].reshape(t.shape))
        pos += t.size + (-t.size) % 128
    return out


def _local_grads(x, target, p, full):
    grads = {}
    saved = []
    for layer in range(2):
        f1 = [full[f"ffn1_w_{n}"][layer] for n in ("gate", "up", "down")]
        f2 = [full[f"ffn2_w_{n}"][layer] for n in ("gate", "up", "down")]
        x, s1 = _ffn_fwd(x, p["ffn1_norm"][layer:layer + 1], *f1, tag=f"ffn1_{layer}")
        if layer == 0:
            x, sm = _mix0_fwd(x, p["mix_norm"][0:1], p, full["ab_w_in"][0], full["s5_w_glu"][0], full["ab_w_out"][0])
        else:
            x, sm = _mix1_fwd(x, p["mix_norm"][1:2], full["sc_w_in"][0], full["sc_conv_w"][0], full["sc_w_out"][0])
        x, s2 = _ffn_fwd(x, p["ffn2_norm"][layer:layer + 1], *f2, tag=f"ffn2_{layer}")
        saved.append((f1, s1, sm, f2, s2))
    loss, dx, dg_final = _loss_head(x, p["final_norm"][None], target)
    grads["final_norm"] = dg_final[0]
    per_layer = {n: [None, None] for n in ("ffn1_norm", "mix_norm", "ffn2_norm", "ffn1_w_gate", "ffn1_w_up", "ffn1_w_down",
                                           "ffn2_w_gate", "ffn2_w_up", "ffn2_w_down")}
    for layer in (1, 0):
        f1, s1, sm, f2, s2 = saved[layer]
        dx, (dg, dwg, dwu, dwd) = _ffn_bwd(dx, s2, p["ffn2_norm"][layer:layer + 1], *f2, tag=f"ffn2_{layer}")
        for n, t in zip(("ffn2_norm", "ffn2_w_gate", "ffn2_w_up", "ffn2_w_down"), (dg[0], dwg, dwu, dwd)):
            per_layer[n][layer] = t
        if layer == 0:
            dx, dg, dw_in, dw_glu, dw_out, s5_grads = _mix0_bwd(dx, sm, p["mix_norm"][0:1], p, full["ab_w_in"][0],
                                                                full["s5_w_glu"][0], full["ab_w_out"][0])
            grads.update(s5_grads)
            grads.update({"ab_w_in": [dw_in], "s5_w_glu": [dw_glu], "ab_w_out": [dw_out]})
        else:
            dx, dg, dw_in, dcw, dw_out = _mix1_bwd(dx, sm, p["mix_norm"][1:2], full["sc_w_in"][0], full["sc_conv_w"][0],
                                                   full["sc_w_out"][0])
            grads.update({"sc_w_in": [dw_in], "sc_conv_w": [dcw.astype(WIRE_DTYPE)], "sc_w_out": [dw_out]})
        per_layer["mix_norm"][layer] = dg[0]
        dx, (dg, dwg, dwu, dwd) = _ffn_bwd(dx, s1, p["ffn1_norm"][layer:layer + 1], *f1, tag=f"ffn1_{layer}")
        for n, t in zip(("ffn1_norm", "ffn1_w_gate", "ffn1_w_up", "ffn1_w_down"), (dg[0], dwg, dwu, dwd)):
            per_layer[n][layer] = t
    for n, pair in per_layer.items():
        grads[n] = pair if n in SHARDED else jnp.stack(pair)
    return loss, dx, grads


def _full_shape(name, shard):
    axis, stride = SHARDED[name]
    shape = list(shard.shape)
    shape[1 + axis] = N_CHIPS * stride
    return tuple(shape)


def kernel(x, ffn1_norm, ffn1_w_gate, ffn1_w_up, ffn1_w_down, mix_norm, ffn2_norm, ffn2_w_gate, ffn2_w_up, ffn2_w_down, ab_w_in, s5_lambda_re, s5_lambda_im, s5_log_dt, s5_b_re, s5_b_im, s5_c_re, s5_c_im, s5_d, s5_w_glu, ab_w_out, sc_w_in, sc_conv_w, sc_w_out, final_norm, loss_target, m_ffn1_norm, m_ffn1_w_gate, m_ffn1_w_up, m_ffn1_w_down, m_mix_norm, m_ffn2_norm, m_ffn2_w_gate, m_ffn2_w_up, m_ffn2_w_down, m_ab_w_in, m_s5_lambda_re, m_s5_lambda_im, m_s5_log_dt, m_s5_b_re, m_s5_b_im, m_s5_c_re, m_s5_c_im, m_s5_d, m_s5_w_glu, m_ab_w_out, m_sc_w_in, m_sc_conv_w, m_sc_w_out, m_final_norm, v_ffn1_norm, v_ffn1_w_gate, v_ffn1_w_up, v_ffn1_w_down, v_mix_norm, v_ffn2_norm, v_ffn2_w_gate, v_ffn2_w_up, v_ffn2_w_down, v_ab_w_in, v_s5_lambda_re, v_s5_lambda_im, v_s5_log_dt, v_s5_b_re, v_s5_b_im, v_s5_c_re, v_s5_c_im, v_s5_d, v_s5_w_glu, v_ab_w_out, v_sc_w_in, v_sc_conv_w, v_sc_w_out, v_final_norm):
    args = dict(locals())
    p = {n: args[n] for n in WEIGHTS}
    mom = {n: args["m_" + n] for n in WEIGHTS}
    var = {n: args["v_" + n] for n in WEIGHTS}

    names = list(SHARDED)
    wire = {n: -(-p[n].shape[2] // 128) * 128 if SHARDED[n][0] == 1 else p[n].shape[1] for n in names}
    shards = [p[n] if n == "sc_conv_w" else p[n].astype(MXU_DTYPE) for n in names]
    shards = [jnp.pad(t, ((0, 0), (0, 0), (0, wire[n] - t.shape[2]))) if SHARDED[n][0] == 1 else t for n, t in zip(names, shards)]
    places = [(1 + SHARDED[n][0], SHARDED[n][1], wire[n]) for n in names]
    full = dict(zip(names, _gather_weights(shards, places, [_full_shape(n, p[n]) for n in names])))

    loss, dx, grads = _local_grads(x[0], loss_target[0], p, full)
    loss = lax.psum(loss[0, 0], ("x", "y", "c"))

    flat_names, flat_grads, flat_places, flat_shapes = [], [], [], []
    for n in names:
        axis, stride = SHARDED[n]
        for layer, g in enumerate(grads[n]):
            flat_names.append((n, layer))
            flat_grads.append(g)
            flat_places.append((axis, stride, wire[n]))
            flat_shapes.append((p[n].shape[1], wire[n]) if axis == 1 else p[n].shape[1:])
    received = _scatter_grads(flat_grads, flat_places, flat_shapes)
    partial = [_chip_sum(r, name=f"chip_sum_{n}_{layer}") for (n, layer), r in zip(flat_names, received)]
    other = _swap_with_sibling(partial)
    out = {}
    stacked = {n: [] for n in names}
    for (n, layer), mine, theirs in zip(flat_names, partial, other):
        cols = p[n].shape[2]
        stacked[n].append(_adamw_pair(p[n][layer], mine[:, :cols], theirs[:, :cols], mom[n][layer], var[n][layer],
                                      name=f"adamw_{n}_{layer}"))
    for n in names:
        out[n] = [jnp.stack([res[k] for res in stacked[n]]) for k in range(4)]

    like = [p[n] for n in SMALL]
    g_small = _sum_all_devices(_pack([grads[n] for n in SMALL]))
    d_small, m_small, v_small = _adamw_small(_pack(like), g_small, _pack([mom[n] for n in SMALL]), _pack([var[n] for n in SMALL]))
    for k, packed in enumerate((g_small, d_small, m_small, v_small)):
        for n, t in zip(SMALL, _unpack(packed, like)):
            out.setdefault(n, [None] * 4)[k] = t

    return (loss, dx[None], *[out[n][0] for n in WEIGHTS], *[out[n][1] for n in WEIGHTS],
            *[out[n][2] for n in WEIGHTS], *[out[n][3] for n in WEIGHTS])
```

```python
import functools
import math

import jax
import jax.numpy as jnp
from jax import lax
from jax.experimental import pallas as pl
from jax.experimental.pallas import tpu as pltpu

F32 = jnp.float32
MXU_DTYPE = jnp.bfloat16
WIRE_DTYPE = jnp.bfloat16
MESH_ID = pl.DeviceIdType.MESH

D_MODEL = 1024
D_FF = 2752
N_CHIPS = 4
FF_SHARD = D_FF // N_CHIPS
FF_SLOT = 768
FF_PAD = N_CHIPS * FF_SLOT
S5_WIDTH = 512
S5_GROUP = 16
S5_GROUPS = 32
S5_STATE = 64
S5_LANES = S5_GROUPS * S5_STATE
S5_BLOCK = 512
SB_HEADS = 8
SB_DH = 64
EPS = 1e-6
ADAM_LR, ADAM_B1, ADAM_B2, ADAM_EPS, ADAM_WD, ADAM_STEP = 0.001, 0.9, 0.999, 1e-08, 0.01, 10
VMEM_LIMIT = 56 * 1024 * 1024

ANY_SPEC = pl.BlockSpec(memory_space=pl.ANY)


def _params(*sem):
    return pltpu.CompilerParams(dimension_semantics=sem or None, vmem_limit_bytes=VMEM_LIMIT)


def _mm(a, b, *, name, ta=False, tb=False, out_dtype=F32, epilogue=None, extras=(), tm=1024, tn=1024, tk=1024):
    m, k = (a.shape[1], a.shape[0]) if ta else a.shape
    n = b.shape[0] if tb else b.shape[1]
    tm, tn, tk = min(tm, m), min(tn, n), min(tk, k)
    assert m % tm == 0 and n % tn == 0 and k % tk == 0, (name, m, n, k)
    nk = k // tk
    a_spec = pl.BlockSpec((tk, tm), lambda i, j, kk: (kk, i)) if ta else pl.BlockSpec((tm, tk), lambda i, j, kk: (i, kk))
    b_spec = pl.BlockSpec((tn, tk), lambda i, j, kk: (j, kk)) if tb else pl.BlockSpec((tk, tn), lambda i, j, kk: (kk, j))
    ex_specs = []
    for e in extras:
        if e.shape == (m, n):
            ex_specs.append(pl.BlockSpec((tm, tn), lambda i, j, kk: (i, j)))
        elif e.shape == (1, n):
            ex_specs.append(pl.BlockSpec((1, tn), lambda i, j, kk: (0, j)))
        else:
            assert e.shape == (m, 1), (name, e.shape)
            ex_specs.append(pl.BlockSpec((tm, 1), lambda i, j, kk: (i, 0)))
    dims = (((0 if ta else 1,), (1 if tb else 0,)), ((), ()))
    n_ex = len(extras)

    def body(a_ref, b_ref, *rest):
        ex_refs, o_ref = rest[:n_ex], rest[n_ex]
        part = lax.dot_general(a_ref[...].astype(MXU_DTYPE), b_ref[...].astype(MXU_DTYPE), dims, preferred_element_type=F32)

        def finish(r):
            if epilogue is not None:
                r = epilogue(r, *[e[...] for e in ex_refs])
            o_ref[...] = r.astype(out_dtype)

        if nk == 1:
            finish(part)
            return
        acc_ref, kk = rest[n_ex + 1], pl.program_id(2)

        @pl.when(kk == 0)
        def _():
            acc_ref[...] = part

        @pl.when(jnp.logical_and(kk > 0, kk < nk - 1))
        def _():
            acc_ref[...] += part

        @pl.when(kk == nk - 1)
        def _():
            finish(acc_ref[...] + part)

    return pl.pallas_call(
        body, name=name, grid=(m // tm, n // tn, nk),
        in_specs=[a_spec, b_spec, *ex_specs],
        out_specs=pl.BlockSpec((tm, tn), lambda i, j, kk: (i, j)),
        out_shape=jax.ShapeDtypeStruct((m, n), out_dtype),
        scratch_shapes=[pltpu.VMEM((tm, tn), F32)] if nk > 1 else [],
        compiler_params=_params("parallel", "parallel", "arbitrary"),
    )(a, b, *extras)


def _row_block(rows, want):
    for tl in range(min(want, rows), 7, -1):
        if rows % tl == 0 and tl % 8 == 0:
            return tl
    return rows


def _rowmap(fn, ins, in_kinds, outs, *, name, tl):
    rows = next(x.shape[0] for x, kd in zip(ins, in_kinds) if kd == "r")
    tl = _row_block(rows, tl)
    n_in = len(ins)

    def spec(shape, kind):
        if kind == "r":
            return pl.BlockSpec((tl,) + tuple(shape[1:]), lambda i: (i,) + (0,) * (len(shape) - 1))
        return pl.BlockSpec(tuple(shape), lambda i: (0,) * len(shape))

    def body(*refs):
        in_refs, out_refs = refs[:n_in], refs[n_in:]
        res = fn(*[r[...] for r in in_refs])
        if not isinstance(res, (tuple, list)):
            res = (res,)
        for o_ref, val, (_, dt, kind) in zip(out_refs, res, outs):
            if kind == "r":
                o_ref[...] = val.astype(dt)
            else:
                @pl.when(pl.program_id(0) == 0)
                def _():
                    o_ref[...] = jnp.zeros_like(o_ref)

                o_ref[...] += val.astype(dt)

    has_acc = any(kd == "a" for _, _, kd in outs)
    res = pl.pallas_call(
        body, name=name, grid=(rows // tl,),
        in_specs=[spec(x.shape, kd) for x, kd in zip(ins, in_kinds)],
        out_specs=[spec(s, kd) for s, _, kd in outs],
        out_shape=[jax.ShapeDtypeStruct(s, dt) for s, dt, _ in outs],
        compiler_params=_params("arbitrary" if has_acc else "parallel"),
    )(*ins)
    return res[0] if len(outs) == 1 else res


def _rms_fwd(x, g):
    r = lax.rsqrt(jnp.mean(x * x, axis=-1, keepdims=True) + EPS)
    return x * r * g


def _rms_bwd(dh, x, g):
    r = lax.rsqrt(jnp.mean(x * x, axis=-1, keepdims=True) + EPS)
    xh = x * r
    dxh = dh * g
    dx = r * (dxh - xh * jnp.mean(dxh * xh, axis=-1, keepdims=True))
    return dx, jnp.sum(dh * xh, axis=0, keepdims=True)


def _norm(x, g, *, name):
    return _rowmap(lambda xv, gv: _rms_fwd(xv, gv), [x, g], "rc", [(x.shape, MXU_DTYPE, "r")], name=name, tl=256)


def _norm_bwd(dh, x, g, dres, *, name):
    def fn(dhv, xv, gv, drv):
        dx, dg = _rms_bwd(dhv, xv, gv)
        return dx + drv, dg
    return _rowmap(fn, [dh, x, g, dres], "rrcr", [(x.shape, F32, "r"), (g.shape, F32, "a")], name=name, tl=256)


def _swiglu_act(a, b):
    return jax.nn.silu(a) * b


def _ffn_fwd(x, g, wg, wu, wd, tag):
    h = _norm(x, g, name=f"{tag}_norm")
    a = _mm(h, wg, name=f"{tag}_gate")
    b = _mm(h, wu, name=f"{tag}_up")
    s = _rowmap(_swiglu_act, [a, b], "rr", [(a.shape, MXU_DTYPE, "r")], name=f"{tag}_act", tl=128)
    x2 = _mm(s, wd, name=f"{tag}_down", epilogue=lambda acc, xv: xv + 0.5 * acc, extras=[x])
    return x2, (x, h, a, b, s)


def _ffn_bwd(dx2, saved, g, wg, wu, wd, tag):
    x, h, a, b, s = saved
    ds = _mm(dx2, wd, tb=True, name=f"{tag}_dact")

    def act_bwd(dsv, av, bv):
        _, vjp = jax.vjp(_swiglu_act, av, bv)
        return vjp(0.5 * dsv)

    da, db = _rowmap(act_bwd, [ds, a, b], "rrr", [(a.shape, MXU_DTYPE, "r")] * 2, name=f"{tag}_act_bwd", tl=128)
    dwd = _mm(s, dx2, ta=True, name=f"{tag}_dwd", out_dtype=WIRE_DTYPE, epilogue=lambda acc: 0.5 * acc)
    dwg = _mm(h, da, ta=True, name=f"{tag}_dwg", out_dtype=WIRE_DTYPE)
    dwu = _mm(h, db, ta=True, name=f"{tag}_dwu", out_dtype=WIRE_DTYPE)
    dh = _mm(da, wg, tb=True, name=f"{tag}_dh1")
    dh = _mm(db, wu, tb=True, name=f"{tag}_dh2", epilogue=lambda acc, prev: acc + prev, extras=[dh])
    dx, dg = _norm_bwd(dh, x, g, dx2, name=f"{tag}_norm_bwd")
    return dx, (dg, dwg, dwu, dwd)


def _softplus(z):
    return jnp.maximum(z, 0.0) + jnp.log(1.0 + jnp.exp(-jnp.abs(z)))


def _ones_dot(x, tri):
    if MXU_DTYPE == F32:
        return jnp.dot(x, tri, preferred_element_type=F32)
    hi = x.astype(MXU_DTYPE)
    lo = (x - hi.astype(F32)).astype(MXU_DTYPE)
    return jnp.dot(hi, tri, preferred_element_type=F32) + jnp.dot(lo, tri, preferred_element_type=F32)


NT_DIMS = (((1,), (1,)), ((), ()))
TN_DIMS = (((0,), (0,)), ((), ()))


def _sb_fwd(q, k, v, *, tq=256):
    nh, seq, dh = q.shape
    tq = min(tq, seq)
    scale = 1.0 / math.sqrt(dh)

    def body(q_ref, k_ref, v_ref, o_ref, ls_ref):
        i = pl.program_id(1)
        qv = q_ref[0]
        r_idx = lax.broadcasted_iota(jnp.int32, (tq, tq), 0)
        c_idx = lax.broadcasted_iota(jnp.int32, (tq, tq), 1)
        after = (r_idx > c_idx).astype(MXU_DTYPE)

        def step(n, carry):
            c, acc = carry
            j = i - n
            off = pl.multiple_of(j * tq, tq)
            kv = k_ref[0, pl.ds(off, tq), :]
            vv = v_ref[0, pl.ds(off, tq), :]
            z = lax.dot_general(qv, kv, NT_DIMS, preferred_element_type=F32) * scale
            mask = (j * tq + c_idx) < (i * tq + r_idx)
            sp = _softplus(z)
            lk = jnp.where(mask, -sp, 0.0)
            later = _ones_dot(lk, after) + c
            w = jnp.where(mask, jnp.exp(z - sp + later), 0.0)
            acc = acc + jnp.dot(w.astype(MXU_DTYPE), vv, preferred_element_type=F32)
            return c + jnp.sum(lk, axis=1, keepdims=True), acc

        c, acc = lax.fori_loop(0, i + 1, step, (jnp.zeros((tq, 1), F32), jnp.zeros((tq, dh), F32)))
        o_ref[0] = acc
        ls_ref[0] = c

    whole = pl.BlockSpec((1, seq, dh), lambda h, i: (h, 0, 0))
    return pl.pallas_call(
        body, name="sb_fwd", grid=(nh, seq // tq),
        in_specs=[pl.BlockSpec((1, tq, dh), lambda h, i: (h, i, 0)), whole, whole],
        out_specs=[pl.BlockSpec((1, tq, dh), lambda h, i: (h, i, 0)), pl.BlockSpec((1, tq, 1), lambda h, i: (h, i, 0))],
        out_shape=[jax.ShapeDtypeStruct((nh, seq, dh), F32), jax.ShapeDtypeStruct((nh, seq, 1), F32)],
        compiler_params=_params("parallel", "parallel"),
    )(q, k, v)


def _sb_bwd(q, k, v, lsum, do, *, tq=256):
    nh, seq, dh = q.shape
    tq = min(tq, seq)
    scale = 1.0 / math.sqrt(dh)

    def body(q_ref, k_ref, v_ref, ls_ref, do_ref, dq_ref, dk_ref, dv_ref):
        i = pl.program_id(1)

        @pl.when(i == 0)
        def _():
            dk_ref[...] = jnp.zeros_like(dk_ref)
            dv_ref[...] = jnp.zeros_like(dv_ref)

        qv = q_ref[0]
        dov = do_ref[0].astype(MXU_DTYPE)
        total = ls_ref[0]
        r_idx = lax.broadcasted_iota(jnp.int32, (tq, tq), 0)
        c_idx = lax.broadcasted_iota(jnp.int32, (tq, tq), 1)
        upto = (r_idx <= c_idx).astype(MXU_DTYPE)
        before = (r_idx < c_idx).astype(MXU_DTYPE)

        def step(j, carry):
            cp, ce, dq = carry
            off = pl.multiple_of(j * tq, tq)
            kv = k_ref[0, pl.ds(off, tq), :]
            vv = v_ref[0, pl.ds(off, tq), :]
            z = lax.dot_general(qv, kv, NT_DIMS, preferred_element_type=F32) * scale
            mask = (j * tq + c_idx) < (i * tq + r_idx)
            sp = _softplus(z)
            lk = jnp.where(mask, -sp, 0.0)
            later = total - (_ones_dot(lk, upto) + cp)
            w = jnp.where(mask, jnp.exp(z - sp + later), 0.0)
            dw = lax.dot_general(dov, vv, NT_DIMS, preferred_element_type=F32)
            e = w * dw
            earlier = _ones_dot(e, before) + ce
            keep = jnp.exp(-sp)
            dz = jnp.where(mask, e * keep - (1.0 - keep) * earlier, 0.0) * scale
            dzm = dz.astype(MXU_DTYPE)
            dq = dq + jnp.dot(dzm, kv, preferred_element_type=F32)
            dk_ref[0, pl.ds(off, tq), :] += lax.dot_general(dzm, qv, TN_DIMS, preferred_element_type=F32)
            dv_ref[0, pl.ds(off, tq), :] += lax.dot_general(w.astype(MXU_DTYPE), dov, TN_DIMS, preferred_element_type=F32)
            return cp + jnp.sum(lk, axis=1, keepdims=True), ce + jnp.sum(e, axis=1, keepdims=True), dq

        zero = jnp.zeros((tq, 1), F32)
        _, _, dq = lax.fori_loop(0, i + 1, step, (zero, zero, jnp.zeros((tq, dh), F32)))
        dq_ref[0] = dq

    whole = pl.BlockSpec((1, seq, dh), lambda h, i: (h, 0, 0))
    tile = pl.BlockSpec((1, tq, dh), lambda h, i: (h, i, 0))
    return pl.pallas_call(
        body, name="sb_bwd", grid=(nh, seq // tq),
        in_specs=[tile, whole, whole, pl.BlockSpec((1, tq, 1), lambda h, i: (h, i, 0)), tile],
        out_specs=[tile, whole, whole],
        out_shape=[jax.ShapeDtypeStruct((nh, seq, dh), F32)] * 3,
        compiler_params=_params("parallel", "arbitrary"),
    )(q, k, v, lsum, do)


def _heads(t):
    return t.reshape(t.shape[0], SB_HEADS, SB_DH).transpose(1, 0, 2)


def _unheads(t):
    return t.transpose(1, 0, 2).reshape(t.shape[1], SB_HEADS * SB_DH)


def _s5_disc(lr, li, ldt, br, bi):
    dt = jnp.exp(ldt)
    mag = jnp.exp(lr * dt)
    ar = mag * jnp.cos(li * dt)
    ai = mag * jnp.sin(li * dt)
    den = lr * lr + li * li
    nr = ar - 1.0
    cr = (nr * lr + ai * li) / den
    ci = (ai * lr - nr * li) / den
    return ar, ai, cr[None] * br - ci[None] * bi, cr[None] * bi + ci[None] * br


def _s5_prep(lr, li, ldt, br, bi):
    shapes = [lr.shape, lr.shape, br.shape, br.shape]

    def body(lr_ref, li_ref, ldt_ref, br_ref, bi_ref, *outs):
        for o, val in zip(outs, _s5_disc(lr_ref[...], li_ref[...], ldt_ref[...], br_ref[...], bi_ref[...])):
            o[...] = val

    return pl.pallas_call(body, name="s5_prep", out_shape=[jax.ShapeDtypeStruct(s, F32) for s in shapes])(lr, li, ldt, br, bi)


def _s5_prep_bwd(lr, li, ldt, br, bi, cts):
    args = (lr, li, ldt, br, bi)

    def body(*refs):
        ins, ct_refs, outs = refs[:5], refs[5:9], refs[9:]
        _, vjp = jax.vjp(_s5_disc, *[r[...] for r in ins])
        for o, val in zip(outs, vjp(tuple(r[...] for r in ct_refs))):
            o[...] = val

    return pl.pallas_call(body, name="s5_prep_bwd", out_shape=[jax.ShapeDtypeStruct(a.shape, F32) for a in args])(*args, *cts)


def _s5_scan(bu, a, *, tc=512):
    seq, w2 = bu.shape
    tw = S5_BLOCK
    tc = min(tc, seq)
    assert seq % tc == 0 and w2 % (2 * tw) == 0

    def body(bu_ref, a_ref, h_ref, cr_ref, ci_ref):
        @pl.when(pl.program_id(1) == 0)
        def _():
            cr_ref[...] = jnp.zeros_like(cr_ref)
            ci_ref[...] = jnp.zeros_like(ci_ref)

        re, im = pl.ds(0, tw), pl.ds(tw, tw)
        ar, ai = a_ref[:, re], a_ref[:, im]

        def step(t, carry):
            hr, hi = carry
            row = pl.ds(t, 1)
            nr = ar * hr - ai * hi + bu_ref[row, re]
            ni = ar * hi + ai * hr + bu_ref[row, im]
            h_ref[row, re] = nr
            h_ref[row, im] = ni
            return nr, ni

        hr, hi = lax.fori_loop(0, tc, step, (cr_ref[...], ci_ref[...]), unroll=8)
        cr_ref[...] = hr
        ci_ref[...] = hi

    blk = pl.BlockSpec((tc, 2 * tw), lambda j, t: (t, j))
    return pl.pallas_call(
        body, name="s5_scan", grid=(w2 // (2 * tw), seq // tc),
        in_specs=[blk, pl.BlockSpec((1, 2 * tw), lambda j, t: (0, j))],
        out_specs=blk,
        out_shape=jax.ShapeDtypeStruct((seq, w2), F32),
        scratch_shapes=[pltpu.VMEM((1, tw), F32)] * 2,
        compiler_params=_params("parallel", "arbitrary"),
    )(bu, a)


def _s5_scan_bwd(d, h, a, *, tc=512):
    seq, w2 = d.shape
    tw = S5_BLOCK
    tc = min(tc, seq)
    assert seq % tc == 0 and w2 % (2 * tw) == 0
    nt = seq // tc

    def body(d_ref, h_ref, a_ref, g_ref, da_ref, cr_ref, ci_ref):
        @pl.when(pl.program_id(1) == 0)
        def _():
            cr_ref[...] = jnp.zeros_like(cr_ref)
            ci_ref[...] = jnp.zeros_like(ci_ref)
            da_ref[...] = jnp.zeros_like(da_ref)

        re, im = pl.ds(0, tw), pl.ds(tw, tw)
        ar, ai = a_ref[:, re], a_ref[:, im]

        def step(n, carry):
            gr, gi, sr, si = carry
            row = pl.ds(tc - 1 - n, 1)
            hrt, hit = h_ref[row, re], h_ref[row, im]
            sr = sr + gr * hrt + gi * hit
            si = si + gi * hrt - gr * hit
            ngr = d_ref[row, re] + ar * gr + ai * gi
            ngi = d_ref[row, im] + ar * gi - ai * gr
            g_ref[row, re] = ngr
            g_ref[row, im] = ngi
            return ngr, ngi, sr, si

        gr, gi, sr, si = lax.fori_loop(0, tc, step, (cr_ref[...], ci_ref[...], da_ref[:, re], da_ref[:, im]), unroll=8)
        cr_ref[...] = gr
        ci_ref[...] = gi
        da_ref[:, re] = sr
        da_ref[:, im] = si

    blk = pl.BlockSpec((tc, 2 * tw), lambda j, t: (nt - 1 - t, j))
    row = pl.BlockSpec((1, 2 * tw), lambda j, t: (0, j))
    return pl.pallas_call(
        body, name="s5_scan_bwd", grid=(w2 // (2 * tw), nt),
        in_specs=[blk, blk, row],
        out_specs=[blk, row],
        out_shape=[jax.ShapeDtypeStruct((seq, w2), F32), jax.ShapeDtypeStruct((1, w2), F32)],
        scratch_shapes=[pltpu.VMEM((1, tw), F32)] * 2,
        compiler_params=_params("parallel", "arbitrary"),
    )(d, h, a)


def _pair_columns(re, im, axis):
    shape = re.shape
    split = shape[:axis] + (shape[axis] // S5_BLOCK, S5_BLOCK) + shape[axis + 1:]
    both = jnp.stack([re.reshape(split), im.reshape(split)], axis=axis + 1)
    return both.reshape(shape[:axis] + (2 * shape[axis],) + shape[axis + 1:])


def _unpair_columns(t, axis):
    shape = t.shape
    both = t.reshape(shape[:axis] + (shape[axis] // (2 * S5_BLOCK), 2, S5_BLOCK) + shape[axis + 1:])
    half = shape[:axis] + (shape[axis] // 2,) + shape[axis + 1:]
    return (lax.index_in_dim(both, 0, axis + 1, keepdims=False).reshape(half),
            lax.index_in_dim(both, 1, axis + 1, keepdims=False).reshape(half))


def _block_diag(t):
    g, a, b = t.shape
    eye = jnp.eye(g, dtype=t.dtype)
    return (t[:, :, None, :] * eye[:, None, :, None]).reshape(g * a, g * b)


def _block_diag_part(m, g):
    a, b = m.shape[0] // g, m.shape[1] // g
    return jnp.moveaxis(jnp.diagonal(m.reshape(g, a, g, b), axis1=0, axis2=2), -1, 0)


def _gelu_glu(y, gate_pre):
    z = jax.nn.gelu(y)
    return z * jax.nn.sigmoid(gate_pre)


def _s5_fwd(u, p, w_glu):
    lr, li = p["s5_lambda_re"][0], p["s5_lambda_im"][0]
    ldt = p["s5_log_dt"][0][:, None]
    br = p["s5_b_re"][0].transpose(2, 0, 1)
    bi = p["s5_b_im"][0].transpose(2, 0, 1)
    ar, ai, bbr, bbi = _s5_prep(lr, li, ldt, br, bi)
    a = _pair_columns(ar.reshape(1, S5_LANES), ai.reshape(1, S5_LANES), 1)
    bmat = _pair_columns(_block_diag(bbr.transpose(1, 0, 2)), _block_diag(bbi.transpose(1, 0, 2)), 1)
    cmat = _pair_columns(_block_diag(p["s5_c_re"][0].transpose(0, 2, 1)),
                         -_block_diag(p["s5_c_im"][0].transpose(0, 2, 1)), 0)
    bmat, cmat = bmat.astype(MXU_DTYPE), cmat.astype(MXU_DTYPE)
    bu = _mm(u, bmat, name="s5_bu")
    h = _s5_scan(bu, a)
    d = p["s5_d"]
    y = _mm(h, cmat, name="s5_y", epilogue=lambda acc, uv, dv: acc + dv * uv, extras=[u, d])
    z = _rowmap(jax.nn.gelu, [y], "r", [(y.shape, MXU_DTYPE, "r")], name="s5_gelu", tl=512)
    gate_pre = _mm(z, w_glu, name="s5_glu")
    out = _rowmap(_gelu_glu, [y, gate_pre], "rr", [(y.shape, F32, "r")], name="s5_gate", tl=512)
    return out, (u, lr, li, ldt, br, bi, a, bmat, cmat, h, y, z, gate_pre)


def _s5_bwd(dout, saved, p, w_glu):
    u, lr, li, ldt, br, bi, a, bmat, cmat, h, y, z, gate_pre = saved
    d = p["s5_d"]

    def gate_bwd(dov, yv, gv):
        zv = jax.nn.gelu(yv)
        sg = jax.nn.sigmoid(gv)
        return dov * sg, dov * zv * sg * (1.0 - sg)

    dz_direct, dgate = _rowmap(gate_bwd, [dout, y, gate_pre], "rrr", [(y.shape, F32, "r"), (y.shape, MXU_DTYPE, "r")],
                               name="s5_gate_bwd", tl=512)
    dw_glu = _mm(z, dgate, ta=True, name="s5_dwglu", out_dtype=WIRE_DTYPE)
    dz = _mm(dgate, w_glu, tb=True, name="s5_dz", epilogue=lambda acc, prev: acc + prev, extras=[dz_direct])

    def gelu_bwd(dzv, yv, uv, dvv):
        _, vjp = jax.vjp(jax.nn.gelu, yv)
        dy = vjp(dzv)[0]
        return dy, dy * dvv, jnp.sum(dy * uv, axis=0, keepdims=True)

    dy, du_skip, dd = _rowmap(gelu_bwd, [dz, y, u, d], "rrrc",
                              [(y.shape, F32, "r"), (y.shape, F32, "r"), (d.shape, F32, "a")], name="s5_gelu_bwd", tl=512)
    dcmat = _mm(h, dy, ta=True, name="s5_dc")
    dstate = _mm(dy, cmat, tb=True, name="s5_dstate")
    g, da = _s5_scan_bwd(dstate, h, a)
    du = _mm(g, bmat, tb=True, name="s5_du", epilogue=lambda acc, prev: acc + prev, extras=[du_skip])
    dbmat = _mm(u, g, ta=True, name="s5_db")
    dbbr, dbbi = (_block_diag_part(t, S5_GROUPS).transpose(1, 0, 2) for t in _unpair_columns(dbmat, 1))
    dar, dai = _unpair_columns(da, 1)
    cts = (dar.reshape(S5_GROUPS, S5_STATE), dai.reshape(S5_GROUPS, S5_STATE), dbbr, dbbi)
    dlr, dli, dldt, dbr, dbi = _s5_prep_bwd(lr, li, ldt, br, bi, cts)
    dcr, dci = (_block_diag_part(t, S5_GROUPS).transpose(0, 2, 1) for t in _unpair_columns(dcmat, 0))
    grads = {
        "s5_lambda_re": dlr[None], "s5_lambda_im": dli[None], "s5_log_dt": dldt[:, 0][None],
        "s5_b_re": dbr.transpose(1, 2, 0)[None], "s5_b_im": dbi.transpose(1, 2, 0)[None],
        "s5_c_re": dcr[None], "s5_c_im": -dci[None], "s5_d": dd,
    }
    return du, dw_glu, grads


def _mix0_fwd(x, g, p, w_in, w_glu, w_out):
    h = _norm(x, g, name="mix0_norm")
    proj = _mm(h, w_in, name="mix0_in")
    u = proj[:, :S5_WIDTH]
    q, k, v = (_heads(proj[:, S5_WIDTH * (1 + n):S5_WIDTH * (2 + n)]).astype(MXU_DTYPE) for n in range(3))
    y_a, s5_saved = _s5_fwd(u, p, w_glu)
    o, lsum = _sb_fwd(q, k, v)
    mix = jnp.concatenate([y_a, _unheads(o)], axis=1).astype(MXU_DTYPE)
    x2 = _mm(mix, w_out, name="mix0_out", epilogue=lambda acc, xv: xv + acc, extras=[x])
    return x2, (x, h, q, k, v, lsum, mix, s5_saved)


def _mix0_bwd(dx2, saved, g, p, w_in, w_glu, w_out):
    x, h, q, k, v, lsum, mix, s5_saved = saved
    dmix = _mm(dx2, w_out, tb=True, name="mix0_dmix")
    dw_out = _mm(mix, dx2, ta=True, name="mix0_dwout", out_dtype=WIRE_DTYPE)
    du, dw_glu, s5_grads = _s5_bwd(dmix[:, :S5_WIDTH], s5_saved, p, w_glu)
    dq, dk, dv = _sb_bwd(q, k, v, lsum, _heads(dmix[:, S5_WIDTH:]))
    dproj = jnp.concatenate([du, _unheads(dq), _unheads(dk), _unheads(dv)], axis=1)
    dw_in = _mm(h, dproj, ta=True, name="mix0_dwin", out_dtype=WIRE_DTYPE)
    dh = _mm(dproj, w_in, tb=True, name="mix0_dh")
    dx, dg = _norm_bwd(dh, x, g, dx2, name="mix0_norm_bwd")
    return dx, dg, dw_in, dw_glu, dw_out, s5_grads


def _shift_down(t, n):
    rows = lax.broadcasted_iota(jnp.int32, t.shape, 0)
    return jnp.where(rows >= n, pltpu.roll(t, n, 0), 0.0)


def _shift_up(t, n):
    rows = lax.broadcasted_iota(jnp.int32, t.shape, 0)
    return jnp.where(rows < t.shape[0] - n, pltpu.roll(t, t.shape[0] - n, 0), 0.0)


def _conv_fwd(proj, cw, *, tc=128):
    seq, c3 = proj.shape
    ch = c3 // 3
    nb = ch // tc

    def body(b_ref, c_ref, v_ref, w_ref, m_ref):
        pv = c_ref[...] * v_ref[...]
        w = w_ref[...]
        y = w[2:3] * pv + w[1:2] * _shift_down(pv, 1) + w[0:1] * _shift_down(pv, 2)
        m_ref[...] = (b_ref[...] * y).astype(m_ref.dtype)

    col = lambda part: pl.BlockSpec((seq, tc), lambda j: (0, part * nb + j))
    return pl.pallas_call(
        body, name="conv_fwd", grid=(nb,),
        in_specs=[col(0), col(1), col(2), pl.BlockSpec((3, tc), lambda j: (0, j))],
        out_specs=pl.BlockSpec((seq, tc), lambda j: (0, j)),
        out_shape=jax.ShapeDtypeStruct((seq, ch), MXU_DTYPE),
        compiler_params=_params("parallel"),
    )(proj, proj, proj, cw)


def _conv_bwd(proj, cw, dm, *, tc=128):
    seq, c3 = proj.shape
    ch = c3 // 3
    nb = ch // tc

    def body(b_ref, c_ref, v_ref, w_ref, dm_ref, db_ref, dc_ref, dv_ref, dw_ref):
        cv, vv, dmv = c_ref[...], v_ref[...], dm_ref[...]
        pv = cv * vv
        w = w_ref[...]
        p1, p2 = _shift_down(pv, 1), _shift_down(pv, 2)
        y = w[2:3] * pv + w[1:2] * p1 + w[0:1] * p2
        db_ref[...] = dmv * y
        dy = dmv * b_ref[...]
        dp = w[2:3] * dy + w[1:2] * _shift_up(dy, 1) + w[0:1] * _shift_up(dy, 2)
        dc_ref[...] = dp * vv
        dv_ref[...] = dp * cv
        dw_ref[...] = jnp.concatenate([jnp.sum(dy * p2, axis=0, keepdims=True), jnp.sum(dy * p1, axis=0, keepdims=True),
                                       jnp.sum(dy * pv, axis=0, keepdims=True)], axis=0)

    col = lambda part: pl.BlockSpec((seq, tc), lambda j: (0, part * nb + j))
    one = pl.BlockSpec((seq, tc), lambda j: (0, j))
    small = pl.BlockSpec((3, tc), lambda j: (0, j))
    return pl.pallas_call(
        body, name="conv_bwd", grid=(nb,),
        in_specs=[col(0), col(1), col(2), small, one],
        out_specs=[one, one, one, small],
        out_shape=[jax.ShapeDtypeStruct((seq, ch), F32)] * 3 + [jax.ShapeDtypeStruct((3, ch), F32)],
        compiler_params=_params("parallel"),
    )(proj, proj, proj, cw, dm)


def _mix1_fwd(x, g, w_in, cw, w_out):
    h = _norm(x, g, name="mix1_norm")
    proj = _mm(h, w_in, name="mix1_in")
    m = _conv_fwd(proj, cw)
    x2 = _mm(m, w_out, name="mix1_out", epilogue=lambda acc, xv: xv + acc, extras=[x])
    return x2, (x, h, proj, m)


def _mix1_bwd(dx2, saved, g, w_in, cw, w_out):
    x, h, proj, m = saved
    dm = _mm(dx2, w_out, tb=True, name="mix1_dm")
    dw_out = _mm(m, dx2, ta=True, name="mix1_dwout", out_dtype=WIRE_DTYPE)
    db, dc, dv, dcw = _conv_bwd(proj, cw, dm)
    dproj = jnp.concatenate([db, dc, dv], axis=1)
    dw_in = _mm(h, dproj, ta=True, name="mix1_dwin", out_dtype=WIRE_DTYPE)
    dh = _mm(dproj, w_in, tb=True, name="mix1_dh")
    dx, dg = _norm_bwd(dh, x, g, dx2, name="mix1_norm_bwd")
    return dx, dg, dw_in, dcw, dw_out


def _loss_head(x, g, target):
    feat = x.shape[1]

    def fn(xv, gv, tv):
        err = _rms_fwd(xv, gv) - tv
        dx, dg = _rms_bwd(err / feat, xv, gv)
        return jnp.sum(err * err, keepdims=True) * (0.5 / feat), dx, dg

    return _rowmap(fn, [x, g, target], "rcr", [((1, 1), F32, "a"), (x.shape, F32, "r"), (g.shape, F32, "a")],
                   name="loss_head", tl=256)


def _window(ref, place, chip):
    axis, stride, width = place
    idx = [slice(None)] * len(ref.shape)
    idx[axis] = pl.ds(pl.multiple_of(chip * stride, 8), width)
    return ref.at[tuple(idx)]


def _other_chips(x, y):
    return [(1 - x, y), (x, 1 - y), (1 - x, 1 - y)]


def _gather_weights(shards, places, full_shapes):
    n = len(shards)
    zeros = [jnp.zeros(s, sh.dtype) for s, sh in zip(full_shapes, shards)]

    def body(*refs):
        ins, outs = refs[:n], refs[2 * n:3 * n]
        send, recv, own = refs[3 * n:]
        x, y, c = lax.axis_index("x"), lax.axis_index("y"), lax.axis_index("c")
        copies = []
        for a in range(n):
            mine = _window(outs[a], places[a], 2 * x + y)
            local = pltpu.make_async_copy(ins[a], mine, own.at[a])
            local.start()
            copies.append(local)
            for r, (px, py) in enumerate(_other_chips(x, y)):
                cp = pltpu.make_async_remote_copy(src_ref=ins[a], dst_ref=mine, send_sem=send.at[3 * a + r],
                                                  recv_sem=recv.at[3 * a + r], device_id=(px, py, c), device_id_type=MESH_ID)
                cp.start()
                copies.append(cp)
        for cp in copies:
            cp.wait()

    return pl.pallas_call(
        body, name="gather_weights",
        in_specs=[ANY_SPEC] * (2 * n), out_specs=[ANY_SPEC] * n,
        out_shape=[jax.ShapeDtypeStruct(s, sh.dtype) for s, sh in zip(full_shapes, shards)],
        input_output_aliases={n + a: a for a in range(n)},
        scratch_shapes=[pltpu.SemaphoreType.DMA((3 * n,)), pltpu.SemaphoreType.DMA((3 * n,)), pltpu.SemaphoreType.DMA((n,))],
        compiler_params=pltpu.CompilerParams(has_side_effects=True),
    )(*shards, *zeros)


def _scatter_grads(grads, places, shard_shapes):
    n = len(grads)

    def body(*refs):
        ins, outs = refs[:n], refs[n:2 * n]
        send, recv, own = refs[2 * n:]
        x, y, c = lax.axis_index("x"), lax.axis_index("y"), lax.axis_index("c")
        copies = []
        for a in range(n):
            local = pltpu.make_async_copy(_window(ins[a], places[a], 2 * x + y), outs[a].at[3], own.at[a])
            local.start()
            copies.append(local)
            for r, (px, py) in enumerate(_other_chips(x, y)):
                cp = pltpu.make_async_remote_copy(src_ref=_window(ins[a], places[a], 2 * px + py), dst_ref=outs[a].at[r],
                                                  send_sem=send.at[3 * a + r], recv_sem=recv.at[3 * a + r],
                                                  device_id=(px, py, c), device_id_type=MESH_ID)
                cp.start()
                copies.append(cp)
        for cp in copies:
            cp.wait()

    return pl.pallas_call(
        body, name="scatter_grads",
        in_specs=[ANY_SPEC] * n, out_specs=[ANY_SPEC] * n,
        out_shape=[jax.ShapeDtypeStruct((N_CHIPS,) + tuple(s), g.dtype) for s, g in zip(shard_shapes, grads)],
        scratch_shapes=[pltpu.SemaphoreType.DMA((3 * n,)), pltpu.SemaphoreType.DMA((3 * n,)), pltpu.SemaphoreType.DMA((n,))],
        compiler_params=pltpu.CompilerParams(has_side_effects=True),
    )(*grads)


def _swap_with_sibling(parts):
    n = len(parts)

    def body(*refs):
        ins, outs = refs[:n], refs[n:2 * n]
        send, recv = refs[2 * n:]
        sibling = (lax.axis_index("x"), lax.axis_index("y"), 1 - lax.axis_index("c"))
        copies = [pltpu.make_async_remote_copy(src_ref=ins[a], dst_ref=outs[a], send_sem=send.at[a], recv_sem=recv.at[a],
                                               device_id=sibling, device_id_type=MESH_ID) for a in range(n)]
        for cp in copies:
            cp.start()
        for cp in copies:
            cp.wait()

    return pl.pallas_call(
        body, name="swap_with_sibling",
        in_specs=[ANY_SPEC] * n, out_specs=[ANY_SPEC] * n,
        out_shape=[jax.ShapeDtypeStruct(p.shape, p.dtype) for p in parts],
        scratch_shapes=[pltpu.SemaphoreType.DMA((n,)), pltpu.SemaphoreType.DMA((n,))],
        compiler_params=pltpu.CompilerParams(has_side_effects=True),
    )(*parts)


def _sum_all_devices(t):
    rows = t.shape[0]

    def body(t_ref, o_ref, slots, send, recv):
        x, y, c = lax.axis_index("x"), lax.axis_index("y"), lax.axis_index("c")
        me = 4 * x + 2 * y + c
        slots[me] = t_ref[...]
        copies = []
        for m in range(1, 8):
            peer = (x ^ (m >> 2), y ^ ((m >> 1) & 1), c ^ (m & 1))
            cp = pltpu.make_async_remote_copy(src_ref=t_ref, dst_ref=slots.at[me], send_sem=send.at[m - 1],
                                              recv_sem=recv.at[m - 1], device_id=peer, device_id_type=MESH_ID)
            cp.start()
            copies.append(cp)
        for cp in copies:
            cp.wait()
        acc = slots[0]
        for dev in range(1, 8):
            acc = acc + slots[dev]
        o_ref[...] = acc

    vmem = pl.BlockSpec(memory_space=pltpu.VMEM)
    return pl.pallas_call(
        body, name="sum_all_devices", in_specs=[vmem], out_specs=vmem,
        out_shape=jax.ShapeDtypeStruct(t.shape, F32),
        scratch_shapes=[pltpu.VMEM((8, rows, 128), F32), pltpu.SemaphoreType.DMA((7,)), pltpu.SemaphoreType.DMA((7,))],
        compiler_params=pltpu.CompilerParams(vmem_limit_bytes=VMEM_LIMIT, has_side_effects=True),
    )(t)


def _adamw(w, g, m, v):
    m = ADAM_B1 * m + (1.0 - ADAM_B1) * g
    v = ADAM_B2 * v + (1.0 - ADAM_B2) * jnp.square(g)
    m_hat = m / (1.0 - ADAM_B1 ** ADAM_STEP)
    v_hat = v / (1.0 - ADAM_B2 ** ADAM_STEP)
    return -ADAM_LR * (m_hat / (jnp.sqrt(v_hat) + ADAM_EPS) + ADAM_WD * w), m, v


def _chip_sum(received, name):
    rows, cols = received.shape[1:]
    tl = _row_block(rows, 512)

    def body(r_ref, o_ref):
        o_ref[...] = ((r_ref[0].astype(F32) + r_ref[1].astype(F32)) + r_ref[2].astype(F32)) + r_ref[3].astype(F32)

    return pl.pallas_call(body, name=name, grid=(rows // tl,),
                          in_specs=[pl.BlockSpec((N_CHIPS, tl, cols), lambda i: (0, i, 0))],
                          out_specs=pl.BlockSpec((tl, cols), lambda i: (i, 0)),
                          out_shape=jax.ShapeDtypeStruct((rows, cols), F32), compiler_params=_params("parallel"))(received)


def _adamw_pair(w, p_mine, p_other, m, v, name):
    def fn(wv, pa, pb, mv, vv):
        g = pa + pb
        return (g,) + _adamw(wv, g, mv, vv)

    return _rowmap(fn, [w, p_mine, p_other, m, v], "rrrrr", [(w.shape, F32, "r")] * 4, name=name, tl=256)


def _adamw_small(w, g, m, v):
    def fn(wv, gv, mv, vv):
        return _adamw(wv, gv, mv, vv)

    return _rowmap(fn, [w, g, m, v], "rrrr", [(w.shape, F32, "r")] * 3, name="adamw_small", tl=w.shape[0])


WEIGHTS = ['ffn1_norm', 'ffn1_w_gate', 'ffn1_w_up', 'ffn1_w_down', 'mix_norm', 'ffn2_norm', 'ffn2_w_gate', 'ffn2_w_up',
           'ffn2_w_down', 'ab_w_in', 's5_lambda_re', 's5_lambda_im', 's5_log_dt', 's5_b_re', 's5_b_im', 's5_c_re', 's5_c_im',
           's5_d', 's5_w_glu', 'ab_w_out', 'sc_w_in', 'sc_conv_w', 'sc_w_out', 'final_norm']
SHARDED = {'ffn1_w_gate': (1, FF_SLOT), 'ffn1_w_up': (1, FF_SLOT), 'ffn1_w_down': (0, FF_SLOT),
           'ffn2_w_gate': (1, FF_SLOT), 'ffn2_w_up': (1, FF_SLOT), 'ffn2_w_down': (0, FF_SLOT),
           'ab_w_in': (1, 512), 's5_w_glu': (0, 128), 'ab_w_out': (0, 256), 'sc_w_in': (1, 768), 'sc_conv_w': (1, 256),
           'sc_w_out': (0, 256)}
SMALL = [n for n in WEIGHTS if n not in SHARDED]


def _pack(arrays):
    rows = []
    for t in arrays:
        flat = t.reshape(-1)
        rows.append(jnp.pad(flat, (0, (-flat.shape[0]) % 128)))
    flat = jnp.concatenate(rows)
    return jnp.pad(flat, (0, (-flat.shape[0]) % 1024)).reshape(-1, 128)


def _unpack(packed, like):
    flat, out, pos = packed.reshape(-1), [], 0
    for t in like:
        out.append(flat[pos:pos + t.size].reshape(t.shape))
        pos += t.size + (-t.size) % 128
    return out


def _local_grads(x, target, p, full):
    grads = {}
    saved = []
    for layer in range(2):
        f1 = [full[f"ffn1_w_{n}"][layer] for n in ("gate", "up", "down")]
        f2 = [full[f"ffn2_w_{n}"][layer] for n in ("gate", "up", "down")]
        x, s1 = _ffn_fwd(x, p["ffn1_norm"][layer:layer + 1], *f1, tag=f"ffn1_{layer}")
        if layer == 0:
            x, sm = _mix0_fwd(x, p["mix_norm"][0:1], p, full["ab_w_in"][0], full["s5_w_glu"][0], full["ab_w_out"][0])
        else:
            x, sm = _mix1_fwd(x, p["mix_norm"][1:2], full["sc_w_in"][0], full["sc_conv_w"][0], full["sc_w_out"][0])
        x, s2 = _ffn_fwd(x, p["ffn2_norm"][layer:layer + 1], *f2, tag=f"ffn2_{layer}")
        saved.append((f1, s1, sm, f2, s2))
    loss, dx, dg_final = _loss_head(x, p["final_norm"][None], target)
    grads["final_norm"] = dg_final[0]
    per_layer = {n: [None, None] for n in ("ffn1_norm", "mix_norm", "ffn2_norm", "ffn1_w_gate", "ffn1_w_up", "ffn1_w_down",
                                           "ffn2_w_gate", "ffn2_w_up", "ffn2_w_down")}
    for layer in (1, 0):
        f1, s1, sm, f2, s2 = saved[layer]
        dx, (dg, dwg, dwu, dwd) = _ffn_bwd(dx, s2, p["ffn2_norm"][layer:layer + 1], *f2, tag=f"ffn2_{layer}")
        for n, t in zip(("ffn2_norm", "ffn2_w_gate", "ffn2_w_up", "ffn2_w_down"), (dg[0], dwg, dwu, dwd)):
            per_layer[n][layer] = t
        if layer == 0:
            dx, dg, dw_in, dw_glu, dw_out, s5_grads = _mix0_bwd(dx, sm, p["mix_norm"][0:1], p, full["ab_w_in"][0],
                                                                full["s5_w_glu"][0], full["ab_w_out"][0])
            grads.update(s5_grads)
            grads.update({"ab_w_in": [dw_in], "s5_w_glu": [dw_glu], "ab_w_out": [dw_out]})
        else:
            dx, dg, dw_in, dcw, dw_out = _mix1_bwd(dx, sm, p["mix_norm"][1:2], full["sc_w_in"][0], full["sc_conv_w"][0],
                                                   full["sc_w_out"][0])
            grads.update({"sc_w_in": [dw_in], "sc_conv_w": [dcw.astype(WIRE_DTYPE)], "sc_w_out": [dw_out]})
        per_layer["mix_norm"][layer] = dg[0]
        dx, (dg, dwg, dwu, dwd) = _ffn_bwd(dx, s1, p["ffn1_norm"][layer:layer + 1], *f1, tag=f"ffn1_{layer}")
        for n, t in zip(("ffn1_norm", "ffn1_w_gate", "ffn1_w_up", "ffn1_w_down"), (dg[0], dwg, dwu, dwd)):
            per_layer[n][layer] = t
    for n, pair in per_layer.items():
        grads[n] = pair if n in SHARDED else jnp.stack(pair)
    return loss, dx, grads


def _full_shape(name, shard):
    axis, stride = SHARDED[name]
    shape = list(shard.shape)
    shape[1 + axis] = N_CHIPS * stride
    return tuple(shape)


def kernel(x, ffn1_norm, ffn1_w_gate, ffn1_w_up, ffn1_w_down, mix_norm, ffn2_norm, ffn2_w_gate, ffn2_w_up, ffn2_w_down, ab_w_in, s5_lambda_re, s5_lambda_im, s5_log_dt, s5_b_re, s5_b_im, s5_c_re, s5_c_im, s5_d, s5_w_glu, ab_w_out, sc_w_in, sc_conv_w, sc_w_out, final_norm, loss_target, m_ffn1_norm, m_ffn1_w_gate, m_ffn1_w_up, m_ffn1_w_down, m_mix_norm, m_ffn2_norm, m_ffn2_w_gate, m_ffn2_w_up, m_ffn2_w_down, m_ab_w_in, m_s5_lambda_re, m_s5_lambda_im, m_s5_log_dt, m_s5_b_re, m_s5_b_im, m_s5_c_re, m_s5_c_im, m_s5_d, m_s5_w_glu, m_ab_w_out, m_sc_w_in, m_sc_conv_w, m_sc_w_out, m_final_norm, v_ffn1_norm, v_ffn1_w_gate, v_ffn1_w_up, v_ffn1_w_down, v_mix_norm, v_ffn2_norm, v_ffn2_w_gate, v_ffn2_w_up, v_ffn2_w_down, v_ab_w_in, v_s5_lambda_re, v_s5_lambda_im, v_s5_log_dt, v_s5_b_re, v_s5_b_im, v_s5_c_re, v_s5_c_im, v_s5_d, v_s5_w_glu, v_ab_w_out, v_sc_w_in, v_sc_conv_w, v_sc_w_out, v_final_norm):
    args = dict(locals())
    p = {n: args[n] for n in WEIGHTS}
    mom = {n: args["m_" + n] for n in WEIGHTS}
    var = {n: args["v_" + n] for n in WEIGHTS}

    names = list(SHARDED)
    wire = {n: -(-p[n].shape[2] // 128) * 128 if SHARDED[n][0] == 1 else p[n].shape[1] for n in names}
    shards = [p[n] if n == "sc_conv_w" else p[n].astype(MXU_DTYPE) for n in names]
    shards = [jnp.pad(t, ((0, 0), (0, 0), (0, wire[n] - t.shape[2]))) if SHARDED[n][0] == 1 else t for n, t in zip(names, shards)]
    places = [(1 + SHARDED[n][0], SHARDED[n][1], wire[n]) for n in names]
    full = dict(zip(names, _gather_weights(shards, places, [_full_shape(n, p[n]) for n in names])))

    loss, dx, grads = _local_grads(x[0], loss_target[0], p, full)
    loss = lax.psum(loss[0, 0], ("x", "y", "c"))

    flat_names, flat_grads, flat_places, flat_shapes = [], [], [], []
    for n in names:
        axis, stride = SHARDED[n]
        for layer, g in enumerate(grads[n]):
            flat_names.append((n, layer))
            flat_grads.append(g)
            flat_places.append((axis, stride, wire[n]))
            flat_shapes.append((p[n].shape[1], wire[n]) if axis == 1 else p[n].shape[1:])
    received = _scatter_grads(flat_grads, flat_places, flat_shapes)
    partial = [_chip_sum(r, name=f"chip_sum_{n}_{layer}") for (n, layer), r in zip(flat_names, received)]
    other = _swap_with_sibling(partial)
    out = {}
    stacked = {n: [] for n in names}
    for (n, layer), mine, theirs in zip(flat_names, partial, other):
        cols = p[n].shape[2]
        stacked[n].append(_adamw_pair(p[n][layer], mine[:, :cols], theirs[:, :cols], mom[n][layer], var[n][layer],
                                      name=f"adamw_{n}_{layer}"))
    for n in names:
        out[n] = [jnp.stack([res[k] for res in stacked[n]]) for k in range(4)]

    like = [p[n] for n in SMALL]
    g_small = _sum_all_devices(_pack([grads[n] for n in SMALL]))
    d_small, m_small, v_small = _adamw_small(_pack(like), g_small, _pack([mom[n] for n in SMALL]), _pack([var[n] for n in SMALL]))
    for k, packed in enumerate((g_small, d_small, m_small, v_small)):
        for n, t in zip(SMALL, _unpack(packed, like)):
            out.setdefault(n, [None] * 4)[k] = t

    return (loss, dx[None], *[out[n][0] for n in WEIGHTS], *[out[n][1] for n in WEIGHTS],
            *[out[n][2] for n in WEIGHTS], *[out[n][3] for n in WEIGHTS])
```

```python
import functools
import math

import jax
import jax.numpy as jnp
from jax import lax
from jax.experimental import pallas as pl
from jax.experimental.pallas import tpu as pltpu

F32 = jnp.float32
MXU_DTYPE = jnp.bfloat16
WIRE_DTYPE = jnp.bfloat16
MESH_ID = pl.DeviceIdType.MESH

D_MODEL = 1024
D_FF = 2752
N_CHIPS = 4
FF_SHARD = D_FF // N_CHIPS
FF_SLOT = 768
FF_PAD = N_CHIPS * FF_SLOT
S5_WIDTH = 512
S5_GROUP = 16
S5_GROUPS = 32
S5_STATE = 64
S5_LANES = S5_GROUPS * S5_STATE
S5_BLOCK = 512
SB_HEADS = 8
SB_DH = 64
EPS = 1e-6
ADAM_LR, ADAM_B1, ADAM_B2, ADAM_EPS, ADAM_WD, ADAM_STEP = 0.001, 0.9, 0.999, 1e-08, 0.01, 10
VMEM_LIMIT = 56 * 1024 * 1024

ANY_SPEC = pl.BlockSpec(memory_space=pl.ANY)


def _params(*sem):
    return pltpu.CompilerParams(dimension_semantics=sem or None, vmem_limit_bytes=VMEM_LIMIT)


def _mm(a, b, *, name, ta=False, tb=False, out_dtype=F32, epilogue=None, extras=(), tm=1024, tn=1024, tk=1024):
    m, k = (a.shape[1], a.shape[0]) if ta else a.shape
    n = b.shape[0] if tb else b.shape[1]
    tm, tn, tk = min(tm, m), min(tn, n), min(tk, k)
    assert m % tm == 0 and n % tn == 0 and k % tk == 0, (name, m, n, k)
    nk = k // tk
    a_spec = pl.BlockSpec((tk, tm), lambda i, j, kk: (kk, i)) if ta else pl.BlockSpec((tm, tk), lambda i, j, kk: (i, kk))
    b_spec = pl.BlockSpec((tn, tk), lambda i, j, kk: (j, kk)) if tb else pl.BlockSpec((tk, tn), lambda i, j, kk: (kk, j))
    ex_specs = []
    for e in extras:
        if e.shape == (m, n):
            ex_specs.append(pl.BlockSpec((tm, tn), lambda i, j, kk: (i, j)))
        elif e.shape == (1, n):
            ex_specs.append(pl.BlockSpec((1, tn), lambda i, j, kk: (0, j)))
        else:
            assert e.shape == (m, 1), (name, e.shape)
            ex_specs.append(pl.BlockSpec((tm, 1), lambda i, j, kk: (i, 0)))
    dims = (((0 if ta else 1,), (1 if tb else 0,)), ((), ()))
    n_ex = len(extras)

    def body(a_ref, b_ref, *rest):
        ex_refs, o_ref = rest[:n_ex], rest[n_ex]
        part = lax.dot_general(a_ref[...].astype(MXU_DTYPE), b_ref[...].astype(MXU_DTYPE), dims, preferred_element_type=F32)

        def finish(r):
            if epilogue is not None:
                r = epilogue(r, *[e[...] for e in ex_refs])
            o_ref[...] = r.astype(out_dtype)

        if nk == 1:
            finish(part)
            return
        acc_ref, kk = rest[n_ex + 1], pl.program_id(2)

        @pl.when(kk == 0)
        def _():
            acc_ref[...] = part

        @pl.when(jnp.logical_and(kk > 0, kk < nk - 1))
        def _():
            acc_ref[...] += part

        @pl.when(kk == nk - 1)
        def _():
            finish(acc_ref[...] + part)

    return pl.pallas_call(
        body, name=name, grid=(m // tm, n // tn, nk),
        in_specs=[a_spec, b_spec, *ex_specs],
        out_specs=pl.BlockSpec((tm, tn), lambda i, j, kk: (i, j)),
        out_shape=jax.ShapeDtypeStruct((m, n), out_dtype),
        scratch_shapes=[pltpu.VMEM((tm, tn), F32)] if nk > 1 else [],
        compiler_params=_params("parallel", "parallel", "arbitrary"),
    )(a, b, *extras)


def _row_block(rows, want):
    for tl in range(min(want, rows), 7, -1):
        if rows % tl == 0 and tl % 8 == 0:
            return tl
    return rows


def _rowmap(fn, ins, in_kinds, outs, *, name, tl):
    rows = next(x.shape[0] for x, kd in zip(ins, in_kinds) if kd == "r")
    tl = _row_block(rows, tl)
    n_in = len(ins)

    def spec(shape, kind):
        if kind == "r":
            return pl.BlockSpec((tl,) + tuple(shape[1:]), lambda i: (i,) + (0,) * (len(shape) - 1))
        return pl.BlockSpec(tuple(shape), lambda i: (0,) * len(shape))

    def body(*refs):
        in_refs, out_refs = refs[:n_in], refs[n_in:]
        res = fn(*[r[...] for r in in_refs])
        if not isinstance(res, (tuple, list)):
            res = (res,)
        for o_ref, val, (_, dt, kind) in zip(out_refs, res, outs):
            if kind == "r":
                o_ref[...] = val.astype(dt)
            else:
                @pl.when(pl.program_id(0) == 0)
                def _():
                    o_ref[...] = jnp.zeros_like(o_ref)

                o_ref[...] += val.astype(dt)

    has_acc = any(kd == "a" for _, _, kd in outs)
    res = pl.pallas_call(
        body, name=name, grid=(rows // tl,),
        in_specs=[spec(x.shape, kd) for x, kd in zip(ins, in_kinds)],
        out_specs=[spec(s, kd) for s, _, kd in outs],
        out_shape=[jax.ShapeDtypeStruct(s, dt) for s, dt, _ in outs],
        compiler_params=_params("arbitrary" if has_acc else "parallel"),
    )(*ins)
    return res[0] if len(outs) == 1 else res


def _rms_fwd(x, g):
    r = lax.rsqrt(jnp.mean(x * x, axis=-1, keepdims=True) + EPS)
    return x * r * g


def _rms_bwd(dh, x, g):
    r = lax.rsqrt(jnp.mean(x * x, axis=-1, keepdims=True) + EPS)
    xh = x * r
    dxh = dh * g
    dx = r * (dxh - xh * jnp.mean(dxh * xh, axis=-1, keepdims=True))
    return dx, jnp.sum(dh * xh, axis=0, keepdims=True)


def _norm(x, g, *, name):
    return _rowmap(lambda xv, gv: _rms_fwd(xv, gv), [x, g], "rc", [(x.shape, MXU_DTYPE, "r")], name=name, tl=256)


def _norm_bwd(dh, x, g, dres, *, name):
    def fn(dhv, xv, gv, drv):
        dx, dg = _rms_bwd(dhv, xv, gv)
        return dx + drv, dg
    return _rowmap(fn, [dh, x, g, dres], "rrcr", [(x.shape, F32, "r"), (g.shape, F32, "a")], name=name, tl=256)


def _swiglu_act(a, b):
    return jax.nn.silu(a) * b


def _ffn_fwd(x, g, wg, wu, wd, tag):
    h = _norm(x, g, name=f"{tag}_norm")
    a = _mm(h, wg, name=f"{tag}_gate")
    b = _mm(h, wu, name=f"{tag}_up")
    s = _rowmap(_swiglu_act, [a, b], "rr", [(a.shape, MXU_DTYPE, "r")], name=f"{tag}_act", tl=128)
    x2 = _mm(s, wd, name=f"{tag}_down", epilogue=lambda acc, xv: xv + 0.5 * acc, extras=[x])
    return x2, (x, h, a, b, s)


def _ffn_bwd(dx2, saved, g, wg, wu, wd, tag):
    x, h, a, b, s = saved
    ds = _mm(dx2, wd, tb=True, name=f"{tag}_dact")

    def act_bwd(dsv, av, bv):
        _, vjp = jax.vjp(_swiglu_act, av, bv)
        return vjp(0.5 * dsv)

    da, db = _rowmap(act_bwd, [ds, a, b], "rrr", [(a.shape, MXU_DTYPE, "r")] * 2, name=f"{tag}_act_bwd", tl=128)
    dwd = _mm(s, dx2, ta=True, name=f"{tag}_dwd", out_dtype=WIRE_DTYPE, epilogue=lambda acc: 0.5 * acc)
    dwg = _mm(h, da, ta=True, name=f"{tag}_dwg", out_dtype=WIRE_DTYPE)
    dwu = _mm(h, db, ta=True, name=f"{tag}_dwu", out_dtype=WIRE_DTYPE)
    dh = _mm(da, wg, tb=True, name=f"{tag}_dh1")
    dh = _mm(db, wu, tb=True, name=f"{tag}_dh2", epilogue=lambda acc, prev: acc + prev, extras=[dh])
    dx, dg = _norm_bwd(dh, x, g, dx2, name=f"{tag}_norm_bwd")
    return dx, (dg, dwg, dwu, dwd)


def _softplus(z):
    return jnp.maximum(z, 0.0) + jnp.log(1.0 + jnp.exp(-jnp.abs(z)))


def _ones_dot(x, tri):
    if MXU_DTYPE == F32:
        return jnp.dot(x, tri, preferred_element_type=F32)
    hi = x.astype(MXU_DTYPE)
    lo = (x - hi.astype(F32)).astype(MXU_DTYPE)
    return jnp.dot(hi, tri, preferred_element_type=F32) + jnp.dot(lo, tri, preferred_element_type=F32)


NT_DIMS = (((1,), (1,)), ((), ()))
TN_DIMS = (((0,), (0,)), ((), ()))


def _sb_fwd(q, k, v, *, tq=256, job=None):
    nh, seq, dh = q.shape
    tq = min(tq, seq)
    scale = 1.0 / math.sqrt(dh)

    def body(q_ref, k_ref, v_ref, o_ref, ls_ref):
        i = pl.program_id(1)
        qv = q_ref[0]
        r_idx = lax.broadcasted_iota(jnp.int32, (tq, tq), 0)
        c_idx = lax.broadcasted_iota(jnp.int32, (tq, tq), 1)
        after = (r_idx > c_idx).astype(MXU_DTYPE)

        def step(n, carry):
            c, acc = carry
            j = i - n
            off = pl.multiple_of(j * tq, tq)
            kv = k_ref[0, pl.ds(off, tq), :]
            vv = v_ref[0, pl.ds(off, tq), :]
            z = lax.dot_general(qv, kv, NT_DIMS, preferred_element_type=F32) * scale
            mask = (j * tq + c_idx) < (i * tq + r_idx)
            sp = _softplus(z)
            lk = jnp.where(mask, -sp, 0.0)
            later = _ones_dot(lk, after) + c
            w = jnp.where(mask, jnp.exp(z - sp + later), 0.0)
            acc = acc + jnp.dot(w.astype(MXU_DTYPE), vv, preferred_element_type=F32)
            return c + jnp.sum(lk, axis=1, keepdims=True), acc

        c, acc = lax.fori_loop(0, i + 1, step, (jnp.zeros((tq, 1), F32), jnp.zeros((tq, dh), F32)))
        o_ref[0] = acc
        ls_ref[0] = c

    whole = pl.BlockSpec((1, seq, dh), lambda h, i: (h, 0, 0))
    return _carried_call(
        body, name="sb_fwd", grid=(nh, seq // tq),
        in_specs=[pl.BlockSpec((1, tq, dh), lambda h, i: (h, i, 0)), whole, whole],
        out_specs=[pl.BlockSpec((1, tq, dh), lambda h, i: (h, i, 0)), pl.BlockSpec((1, tq, 1), lambda h, i: (h, i, 0))],
        out_shape=[jax.ShapeDtypeStruct((nh, seq, dh), F32), jax.ShapeDtypeStruct((nh, seq, 1), F32)],
        semantics=("parallel", "parallel"), operands=(q, k, v), job=job)


def _sb_bwd(q, k, v, lsum, do, *, tq=256, job=None):
    nh, seq, dh = q.shape
    tq = min(tq, seq)
    scale = 1.0 / math.sqrt(dh)

    def body(q_ref, k_ref, v_ref, ls_ref, do_ref, dq_ref, dk_ref, dv_ref):
        i = pl.program_id(1)

        @pl.when(i == 0)
        def _():
            dk_ref[...] = jnp.zeros_like(dk_ref)
            dv_ref[...] = jnp.zeros_like(dv_ref)

        qv = q_ref[0]
        dov = do_ref[0].astype(MXU_DTYPE)
        total = ls_ref[0]
        r_idx = lax.broadcasted_iota(jnp.int32, (tq, tq), 0)
        c_idx = lax.broadcasted_iota(jnp.int32, (tq, tq), 1)
        upto = (r_idx <= c_idx).astype(MXU_DTYPE)
        before = (r_idx < c_idx).astype(MXU_DTYPE)

        def step(j, carry):
            cp, ce, dq = carry
            off = pl.multiple_of(j * tq, tq)
            kv = k_ref[0, pl.ds(off, tq), :]
            vv = v_ref[0, pl.ds(off, tq), :]
            z = lax.dot_general(qv, kv, NT_DIMS, preferred_element_type=F32) * scale
            mask = (j * tq + c_idx) < (i * tq + r_idx)
            sp = _softplus(z)
            lk = jnp.where(mask, -sp, 0.0)
            later = total - (_ones_dot(lk, upto) + cp)
            w = jnp.where(mask, jnp.exp(z - sp + later), 0.0)
            dw = lax.dot_general(dov, vv, NT_DIMS, preferred_element_type=F32)
            e = w * dw
            earlier = _ones_dot(e, before) + ce
            keep = jnp.exp(-sp)
            dz = jnp.where(mask, e * keep - (1.0 - keep) * earlier, 0.0) * scale
            dzm = dz.astype(MXU_DTYPE)
            dq = dq + jnp.dot(dzm, kv, preferred_element_type=F32)
            dk_ref[0, pl.ds(off, tq), :] += lax.dot_general(dzm, qv, TN_DIMS, preferred_element_type=F32)
            dv_ref[0, pl.ds(off, tq), :] += lax.dot_general(w.astype(MXU_DTYPE), dov, TN_DIMS, preferred_element_type=F32)
            return cp + jnp.sum(lk, axis=1, keepdims=True), ce + jnp.sum(e, axis=1, keepdims=True), dq

        zero = jnp.zeros((tq, 1), F32)
        _, _, dq = lax.fori_loop(0, i + 1, step, (zero, zero, jnp.zeros((tq, dh), F32)))
        dq_ref[0] = dq

    whole = pl.BlockSpec((1, seq, dh), lambda h, i: (h, 0, 0))
    tile = pl.BlockSpec((1, tq, dh), lambda h, i: (h, i, 0))
    return _carried_call(
        body, name="sb_bwd", grid=(nh, seq // tq),
        in_specs=[tile, whole, whole, pl.BlockSpec((1, tq, 1), lambda h, i: (h, i, 0)), tile],
        out_specs=[tile, whole, whole],
        out_shape=[jax.ShapeDtypeStruct((nh, seq, dh), F32)] * 3,
        semantics=("parallel", "arbitrary"), operands=(q, k, v, lsum, do), job=job)


def _heads(t):
    return t.reshape(t.shape[0], SB_HEADS, SB_DH).transpose(1, 0, 2)


def _unheads(t):
    return t.transpose(1, 0, 2).reshape(t.shape[1], SB_HEADS * SB_DH)


def _s5_disc(lr, li, ldt, br, bi):
    dt = jnp.exp(ldt)
    mag = jnp.exp(lr * dt)
    ar = mag * jnp.cos(li * dt)
    ai = mag * jnp.sin(li * dt)
    den = lr * lr + li * li
    nr = ar - 1.0
    cr = (nr * lr + ai * li) / den
    ci = (ai * lr - nr * li) / den
    return ar, ai, cr[None] * br - ci[None] * bi, cr[None] * bi + ci[None] * br


def _s5_prep(lr, li, ldt, br, bi):
    shapes = [lr.shape, lr.shape, br.shape, br.shape]

    def body(lr_ref, li_ref, ldt_ref, br_ref, bi_ref, *outs):
        for o, val in zip(outs, _s5_disc(lr_ref[...], li_ref[...], ldt_ref[...], br_ref[...], bi_ref[...])):
            o[...] = val

    return pl.pallas_call(body, name="s5_prep", out_shape=[jax.ShapeDtypeStruct(s, F32) for s in shapes])(lr, li, ldt, br, bi)


def _s5_prep_bwd(lr, li, ldt, br, bi, cts):
    args = (lr, li, ldt, br, bi)

    def body(*refs):
        ins, ct_refs, outs = refs[:5], refs[5:9], refs[9:]
        _, vjp = jax.vjp(_s5_disc, *[r[...] for r in ins])
        for o, val in zip(outs, vjp(tuple(r[...] for r in ct_refs))):
            o[...] = val

    return pl.pallas_call(body, name="s5_prep_bwd", out_shape=[jax.ShapeDtypeStruct(a.shape, F32) for a in args])(*args, *cts)


def _s5_scan(bu, a, *, tc=512):
    seq, w2 = bu.shape
    tw = S5_BLOCK
    tc = min(tc, seq)
    assert seq % tc == 0 and w2 % (2 * tw) == 0

    def body(bu_ref, a_ref, h_ref, cr_ref, ci_ref):
        @pl.when(pl.program_id(1) == 0)
        def _():
            cr_ref[...] = jnp.zeros_like(cr_ref)
            ci_ref[...] = jnp.zeros_like(ci_ref)

        re, im = pl.ds(0, tw), pl.ds(tw, tw)
        ar, ai = a_ref[:, re], a_ref[:, im]

        def step(t, carry):
            hr, hi = carry
            row = pl.ds(t, 1)
            nr = ar * hr - ai * hi + bu_ref[row, re]
            ni = ar * hi + ai * hr + bu_ref[row, im]
            h_ref[row, re] = nr
            h_ref[row, im] = ni
            return nr, ni

        hr, hi = lax.fori_loop(0, tc, step, (cr_ref[...], ci_ref[...]), unroll=8)
        cr_ref[...] = hr
        ci_ref[...] = hi

    blk = pl.BlockSpec((tc, 2 * tw), lambda j, t: (t, j))
    return pl.pallas_call(
        body, name="s5_scan", grid=(w2 // (2 * tw), seq // tc),
        in_specs=[blk, pl.BlockSpec((1, 2 * tw), lambda j, t: (0, j))],
        out_specs=blk,
        out_shape=jax.ShapeDtypeStruct((seq, w2), F32),
        scratch_shapes=[pltpu.VMEM((1, tw), F32)] * 2,
        compiler_params=_params("parallel", "arbitrary"),
    )(bu, a)


def _s5_scan_bwd(d, h, a, *, tc=512):
    seq, w2 = d.shape
    tw = S5_BLOCK
    tc = min(tc, seq)
    assert seq % tc == 0 and w2 % (2 * tw) == 0
    nt = seq // tc

    def body(d_ref, h_ref, a_ref, g_ref, da_ref, cr_ref, ci_ref):
        @pl.when(pl.program_id(1) == 0)
        def _():
            cr_ref[...] = jnp.zeros_like(cr_ref)
            ci_ref[...] = jnp.zeros_like(ci_ref)
            da_ref[...] = jnp.zeros_like(da_ref)

        re, im = pl.ds(0, tw), pl.ds(tw, tw)
        ar, ai = a_ref[:, re], a_ref[:, im]

        def step(n, carry):
            gr, gi, sr, si = carry
            row = pl.ds(tc - 1 - n, 1)
            hrt, hit = h_ref[row, re], h_ref[row, im]
            sr = sr + gr * hrt + gi * hit
            si = si + gi * hrt - gr * hit
            ngr = d_ref[row, re] + ar * gr + ai * gi
            ngi = d_ref[row, im] + ar * gi - ai * gr
            g_ref[row, re] = ngr
            g_ref[row, im] = ngi
            return ngr, ngi, sr, si

        gr, gi, sr, si = lax.fori_loop(0, tc, step, (cr_ref[...], ci_ref[...], da_ref[:, re], da_ref[:, im]), unroll=8)
        cr_ref[...] = gr
        ci_ref[...] = gi
        da_ref[:, re] = sr
        da_ref[:, im] = si

    blk = pl.BlockSpec((tc, 2 * tw), lambda j, t: (nt - 1 - t, j))
    row = pl.BlockSpec((1, 2 * tw), lambda j, t: (0, j))
    return pl.pallas_call(
        body, name="s5_scan_bwd", grid=(w2 // (2 * tw), nt),
        in_specs=[blk, blk, row],
        out_specs=[blk, row],
        out_shape=[jax.ShapeDtypeStruct((seq, w2), F32), jax.ShapeDtypeStruct((1, w2), F32)],
        scratch_shapes=[pltpu.VMEM((1, tw), F32)] * 2,
        compiler_params=_params("parallel", "arbitrary"),
    )(d, h, a)


def _pair_columns(re, im, axis):
    shape = re.shape
    split = shape[:axis] + (shape[axis] // S5_BLOCK, S5_BLOCK) + shape[axis + 1:]
    both = jnp.stack([re.reshape(split), im.reshape(split)], axis=axis + 1)
    return both.reshape(shape[:axis] + (2 * shape[axis],) + shape[axis + 1:])


def _unpair_columns(t, axis):
    shape = t.shape
    both = t.reshape(shape[:axis] + (shape[axis] // (2 * S5_BLOCK), 2, S5_BLOCK) + shape[axis + 1:])
    half = shape[:axis] + (shape[axis] // 2,) + shape[axis + 1:]
    return (lax.index_in_dim(both, 0, axis + 1, keepdims=False).reshape(half),
            lax.index_in_dim(both, 1, axis + 1, keepdims=False).reshape(half))


def _block_diag(t):
    g, a, b = t.shape
    eye = jnp.eye(g, dtype=t.dtype)
    return (t[:, :, None, :] * eye[:, None, :, None]).reshape(g * a, g * b)


def _block_diag_part(m, g):
    a, b = m.shape[0] // g, m.shape[1] // g
    return jnp.moveaxis(jnp.diagonal(m.reshape(g, a, g, b), axis1=0, axis2=2), -1, 0)


def _gelu_glu(y, gate_pre):
    z = jax.nn.gelu(y)
    return z * jax.nn.sigmoid(gate_pre)


def _s5_fwd(u, p, w_glu):
    lr, li = p["s5_lambda_re"][0], p["s5_lambda_im"][0]
    ldt = p["s5_log_dt"][0][:, None]
    br = p["s5_b_re"][0].transpose(2, 0, 1)
    bi = p["s5_b_im"][0].transpose(2, 0, 1)
    ar, ai, bbr, bbi = _s5_prep(lr, li, ldt, br, bi)
    a = _pair_columns(ar.reshape(1, S5_LANES), ai.reshape(1, S5_LANES), 1)
    bmat = _pair_columns(_block_diag(bbr.transpose(1, 0, 2)), _block_diag(bbi.transpose(1, 0, 2)), 1)
    cmat = _pair_columns(_block_diag(p["s5_c_re"][0].transpose(0, 2, 1)),
                         -_block_diag(p["s5_c_im"][0].transpose(0, 2, 1)), 0)
    bmat, cmat = bmat.astype(MXU_DTYPE), cmat.astype(MXU_DTYPE)
    bu = _mm(u, bmat, name="s5_bu")
    h = _s5_scan(bu, a)
    d = p["s5_d"]
    y = _mm(h, cmat, name="s5_y", epilogue=lambda acc, uv, dv: acc + dv * uv, extras=[u, d])
    z = _rowmap(jax.nn.gelu, [y], "r", [(y.shape, MXU_DTYPE, "r")], name="s5_gelu", tl=512)
    gate_pre = _mm(z, w_glu, name="s5_glu")
    out = _rowmap(_gelu_glu, [y, gate_pre], "rr", [(y.shape, F32, "r")], name="s5_gate", tl=512)
    return out, (u, lr, li, ldt, br, bi, a, bmat, cmat, h, y, z, gate_pre)


def _s5_bwd(dout, saved, p, w_glu):
    u, lr, li, ldt, br, bi, a, bmat, cmat, h, y, z, gate_pre = saved
    d = p["s5_d"]

    def gate_bwd(dov, yv, gv):
        zv = jax.nn.gelu(yv)
        sg = jax.nn.sigmoid(gv)
        return dov * sg, dov * zv * sg * (1.0 - sg)

    dz_direct, dgate = _rowmap(gate_bwd, [dout, y, gate_pre], "rrr", [(y.shape, F32, "r"), (y.shape, MXU_DTYPE, "r")],
                               name="s5_gate_bwd", tl=512)
    dw_glu = _mm(z, dgate, ta=True, name="s5_dwglu", out_dtype=WIRE_DTYPE)
    dz = _mm(dgate, w_glu, tb=True, name="s5_dz", epilogue=lambda acc, prev: acc + prev, extras=[dz_direct])

    def gelu_bwd(dzv, yv, uv, dvv):
        _, vjp = jax.vjp(jax.nn.gelu, yv)
        dy = vjp(dzv)[0]
        return dy, dy * dvv, jnp.sum(dy * uv, axis=0, keepdims=True)

    dy, du_skip, dd = _rowmap(gelu_bwd, [dz, y, u, d], "rrrc",
                              [(y.shape, F32, "r"), (y.shape, F32, "r"), (d.shape, F32, "a")], name="s5_gelu_bwd", tl=512)
    dcmat = _mm(h, dy, ta=True, name="s5_dc")
    dstate = _mm(dy, cmat, tb=True, name="s5_dstate")
    g, da = _s5_scan_bwd(dstate, h, a)
    du = _mm(g, bmat, tb=True, name="s5_du", epilogue=lambda acc, prev: acc + prev, extras=[du_skip])
    dbmat = _mm(u, g, ta=True, name="s5_db")
    dbbr, dbbi = (_block_diag_part(t, S5_GROUPS).transpose(1, 0, 2) for t in _unpair_columns(dbmat, 1))
    dar, dai = _unpair_columns(da, 1)
    cts = (dar.reshape(S5_GROUPS, S5_STATE), dai.reshape(S5_GROUPS, S5_STATE), dbbr, dbbi)
    dlr, dli, dldt, dbr, dbi = _s5_prep_bwd(lr, li, ldt, br, bi, cts)
    dcr, dci = (_block_diag_part(t, S5_GROUPS).transpose(0, 2, 1) for t in _unpair_columns(dcmat, 0))
    grads = {
        "s5_lambda_re": dlr[None], "s5_lambda_im": dli[None], "s5_log_dt": dldt[:, 0][None],
        "s5_b_re": dbr.transpose(1, 2, 0)[None], "s5_b_im": dbi.transpose(1, 2, 0)[None],
        "s5_c_re": dcr[None], "s5_c_im": -dci[None], "s5_d": dd,
    }
    return du, dw_glu, grads


def _mix0_fwd(x, g, p, full, late):
    h = _norm(x, g, name="mix0_norm")
    proj = _mm(h, full[("ab_w_in", 0)], name="mix0_in")
    u = proj[:, :S5_WIDTH]
    q, k, v = (_heads(proj[:, S5_WIDTH * (1 + n):S5_WIDTH * (2 + n)]).astype(MXU_DTYPE) for n in range(3))
    (o, lsum), got = _sb_fwd(q, k, v, job=late.gather_job() if late else None)
    if late:
        full.update(zip(late.gather_keys, got))
    w_glu, w_out = full[("s5_w_glu", 0)], full[("ab_w_out", 0)]
    y_a, s5_saved = _s5_fwd(u, p, w_glu)
    mix = jnp.concatenate([y_a, _unheads(o)], axis=1).astype(MXU_DTYPE)
    x2 = _mm(mix, w_out, name="mix0_out", epilogue=lambda acc, xv: xv + acc, extras=[x])
    return x2, (x, h, q, k, v, lsum, mix, s5_saved)


def _mix0_bwd(dx2, saved, g, p, full, grads, late):
    x, h, q, k, v, lsum, mix, s5_saved = saved
    w_in, w_glu, w_out = full[("ab_w_in", 0)], full[("s5_w_glu", 0)], full[("ab_w_out", 0)]
    dmix = _mm(dx2, w_out, tb=True, name="mix0_dmix")
    grads[("ab_w_out", 0)] = _mm(mix, dx2, ta=True, name="mix0_dwout", out_dtype=WIRE_DTYPE)
    du, grads[("s5_w_glu", 0)], s5_grads = _s5_bwd(dmix[:, :S5_WIDTH], s5_saved, p, w_glu)
    (dq, dk, dv), got = _sb_bwd(q, k, v, lsum, _heads(dmix[:, S5_WIDTH:]), job=late.scatter_job(grads) if late else None)
    if late:
        late.received.update(zip(late.scatter_keys, got))
    dproj = jnp.concatenate([du, _unheads(dq), _unheads(dk), _unheads(dv)], axis=1)
    grads[("ab_w_in", 0)] = _mm(h, dproj, ta=True, name="mix0_dwin", out_dtype=WIRE_DTYPE)
    dh = _mm(dproj, w_in, tb=True, name="mix0_dh")
    dx, dg = _norm_bwd(dh, x, g, dx2, name="mix0_norm_bwd")
    return dx, dg, s5_grads


def _shift_down(t, n):
    rows = lax.broadcasted_iota(jnp.int32, t.shape, 0)
    return jnp.where(rows >= n, pltpu.roll(t, n, 0), 0.0)


def _shift_up(t, n):
    rows = lax.broadcasted_iota(jnp.int32, t.shape, 0)
    return jnp.where(rows < t.shape[0] - n, pltpu.roll(t, t.shape[0] - n, 0), 0.0)


def _conv_fwd(proj, cw, *, tc=128):
    seq, c3 = proj.shape
    ch = c3 // 3
    nb = ch // tc

    def body(b_ref, c_ref, v_ref, w_ref, m_ref):
        pv = c_ref[...] * v_ref[...]
        w = w_ref[...]
        y = w[2:3] * pv + w[1:2] * _shift_down(pv, 1) + w[0:1] * _shift_down(pv, 2)
        m_ref[...] = (b_ref[...] * y).astype(m_ref.dtype)

    col = lambda part: pl.BlockSpec((seq, tc), lambda j: (0, part * nb + j))
    return pl.pallas_call(
        body, name="conv_fwd", grid=(nb,),
        in_specs=[col(0), col(1), col(2), pl.BlockSpec((3, tc), lambda j: (0, j))],
        out_specs=pl.BlockSpec((seq, tc), lambda j: (0, j)),
        out_shape=jax.ShapeDtypeStruct((seq, ch), MXU_DTYPE),
        compiler_params=_params("parallel"),
    )(proj, proj, proj, cw)


def _conv_bwd(proj, cw, dm, *, tc=128):
    seq, c3 = proj.shape
    ch = c3 // 3
    nb = ch // tc

    def body(b_ref, c_ref, v_ref, w_ref, dm_ref, db_ref, dc_ref, dv_ref, dw_ref):
        cv, vv, dmv = c_ref[...], v_ref[...], dm_ref[...]
        pv = cv * vv
        w = w_ref[...]
        p1, p2 = _shift_down(pv, 1), _shift_down(pv, 2)
        y = w[2:3] * pv + w[1:2] * p1 + w[0:1] * p2
        db_ref[...] = dmv * y
        dy = dmv * b_ref[...]
        dp = w[2:3] * dy + w[1:2] * _shift_up(dy, 1) + w[0:1] * _shift_up(dy, 2)
        dc_ref[...] = dp * vv
        dv_ref[...] = dp * cv
        dw_ref[...] = jnp.concatenate([jnp.sum(dy * p2, axis=0, keepdims=True), jnp.sum(dy * p1, axis=0, keepdims=True),
                                       jnp.sum(dy * pv, axis=0, keepdims=True)], axis=0)

    col = lambda part: pl.BlockSpec((seq, tc), lambda j: (0, part * nb + j))
    one = pl.BlockSpec((seq, tc), lambda j: (0, j))
    small = pl.BlockSpec((3, tc), lambda j: (0, j))
    return pl.pallas_call(
        body, name="conv_bwd", grid=(nb,),
        in_specs=[col(0), col(1), col(2), small, one],
        out_specs=[one, one, one, small],
        out_shape=[jax.ShapeDtypeStruct((seq, ch), F32)] * 3 + [jax.ShapeDtypeStruct((3, ch), F32)],
        compiler_params=_params("parallel"),
    )(proj, proj, proj, cw, dm)


def _mix1_fwd(x, g, w_in, cw, w_out):
    h = _norm(x, g, name="mix1_norm")
    proj = _mm(h, w_in, name="mix1_in")
    m = _conv_fwd(proj, cw)
    x2 = _mm(m, w_out, name="mix1_out", epilogue=lambda acc, xv: xv + acc, extras=[x])
    return x2, (x, h, proj, m)


def _mix1_bwd(dx2, saved, g, w_in, cw, w_out):
    x, h, proj, m = saved
    dm = _mm(dx2, w_out, tb=True, name="mix1_dm")
    dw_out = _mm(m, dx2, ta=True, name="mix1_dwout", out_dtype=WIRE_DTYPE)
    db, dc, dv, dcw = _conv_bwd(proj, cw, dm)
    dproj = jnp.concatenate([db, dc, dv], axis=1)
    dw_in = _mm(h, dproj, ta=True, name="mix1_dwin", out_dtype=WIRE_DTYPE)
    dh = _mm(dproj, w_in, tb=True, name="mix1_dh")
    dx, dg = _norm_bwd(dh, x, g, dx2, name="mix1_norm_bwd")
    return dx, dg, dw_in, dcw, dw_out


def _loss_head(x, g, target):
    feat = x.shape[1]

    def fn(xv, gv, tv):
        err = _rms_fwd(xv, gv) - tv
        dx, dg = _rms_bwd(err / feat, xv, gv)
        return jnp.sum(err * err, keepdims=True) * (0.5 / feat), dx, dg

    return _rowmap(fn, [x, g, target], "rcr", [((1, 1), F32, "a"), (x.shape, F32, "r"), (g.shape, F32, "a")],
                   name="loss_head", tl=256)


def _window(ref, place, chip):
    axis, width = place
    idx = [slice(None)] * len(ref.shape)
    idx[axis] = pl.ds(pl.multiple_of(chip * width, 8), width)
    return ref.at[tuple(idx)]


class _Exchange:
    def __init__(self, kind, arrays, places):
        self.kind, self.arrays, self.places, self.n = kind, list(arrays), list(places), len(arrays)
        self.out_shape = []
        for t, (axis, width) in zip(self.arrays, self.places):
            if kind == "gather":
                shape = list(t.shape)
                shape[axis] = N_CHIPS * width
            else:
                shape = [N_CHIPS] + list(t.shape)
                shape[1 + axis] = width
            self.out_shape.append(jax.ShapeDtypeStruct(tuple(shape), t.dtype))
        n = self.n
        self.scratch = [pltpu.SemaphoreType.DMA((3 * n,)), pltpu.SemaphoreType.DMA((3 * n,)), pltpu.SemaphoreType.DMA((n,))]

    def copies(self, ins, outs, send, recv, own):
        x, y, c = lax.axis_index("x"), lax.axis_index("y"), lax.axis_index("c")
        peers = [(1 - x, y), (x, 1 - y), (1 - x, 1 - y)]
        made = []
        for a in range(self.n):
            place = self.places[a]
            if self.kind == "gather":
                slot = _window(outs[a], place, 2 * x + y)
                made.append(pltpu.make_async_copy(ins[a], slot, own.at[a]))
                pairs = [(ins[a], slot)] * 3
            else:
                made.append(pltpu.make_async_copy(_window(ins[a], place, 2 * x + y), outs[a].at[3], own.at[a]))
                pairs = [(_window(ins[a], place, 2 * px + py), outs[a].at[r]) for r, (px, py) in enumerate(peers)]
            for r, ((px, py), (src, dst)) in enumerate(zip(peers, pairs)):
                made.append(pltpu.make_async_remote_copy(src_ref=src, dst_ref=dst, send_sem=send.at[3 * a + r],
                                                         recv_sem=recv.at[3 * a + r], device_id=(px, py, c),
                                                         device_id_type=MESH_ID))
        return made


def _exchange_call(job, name):
    n = job.n

    def body(*refs):
        copies = job.copies(refs[:n], refs[n:2 * n], *refs[2 * n:])
        for cp in copies:
            cp.start()
        for cp in copies:
            cp.wait()

    return pl.pallas_call(
        body, name=name, in_specs=[ANY_SPEC] * n, out_specs=[ANY_SPEC] * n, out_shape=job.out_shape,
        scratch_shapes=job.scratch, compiler_params=pltpu.CompilerParams(has_side_effects=True),
    )(*job.arrays)


def _carried_call(body, *, name, grid, in_specs, out_specs, out_shape, semantics, operands, job=None):
    if job is None:
        return pl.pallas_call(body, name=name, grid=grid, in_specs=in_specs, out_specs=out_specs, out_shape=out_shape,
                              compiler_params=_params(*semantics))(*operands), []
    n_in, n_out, n = len(in_specs), len(out_specs), job.n

    def wrapped(*refs):
        ins, job_ins = refs[:n_in], refs[n_in:n_in + n]
        outs, job_outs = refs[n_in + n:n_in + n + n_out], refs[n_in + n + n_out:n_in + 2 * n + n_out]
        sems = refs[n_in + 2 * n + n_out:]
        pids = [pl.program_id(d) for d in range(len(grid))]
        first = functools.reduce(jnp.logical_and, [pid == 0 for pid in pids])
        last = functools.reduce(jnp.logical_and, [pid == size - 1 for pid, size in zip(pids, grid)])

        @pl.when(first)
        def _():
            for cp in job.copies(job_ins, job_outs, *sems):
                cp.start()

        body(*ins, *outs)

        @pl.when(last)
        def _():
            for cp in job.copies(job_ins, job_outs, *sems):
                cp.wait()

    res = pl.pallas_call(
        wrapped, name=name, grid=grid, in_specs=list(in_specs) + [ANY_SPEC] * n, out_specs=list(out_specs) + [ANY_SPEC] * n,
        out_shape=list(out_shape) + job.out_shape, scratch_shapes=job.scratch,
        compiler_params=pltpu.CompilerParams(dimension_semantics=("arbitrary",) * len(grid), vmem_limit_bytes=VMEM_LIMIT,
                                             has_side_effects=True),
    )(*operands, *job.arrays)
    return res[:n_out], res[n_out:]


def _swap_with_sibling(parts):
    n = len(parts)

    def body(*refs):
        ins, outs = refs[:n], refs[n:2 * n]
        send, recv = refs[2 * n:]
        sibling = (lax.axis_index("x"), lax.axis_index("y"), 1 - lax.axis_index("c"))
        copies = [pltpu.make_async_remote_copy(src_ref=ins[a], dst_ref=outs[a], send_sem=send.at[a], recv_sem=recv.at[a],
                                               device_id=sibling, device_id_type=MESH_ID) for a in range(n)]
        for cp in copies:
            cp.start()
        for cp in copies:
            cp.wait()

    return pl.pallas_call(
        body, name="swap_with_sibling",
        in_specs=[ANY_SPEC] * n, out_specs=[ANY_SPEC] * n,
        out_shape=[jax.ShapeDtypeStruct(p.shape, p.dtype) for p in parts],
        scratch_shapes=[pltpu.SemaphoreType.DMA((n,)), pltpu.SemaphoreType.DMA((n,))],
        compiler_params=pltpu.CompilerParams(has_side_effects=True),
    )(*parts)


def _sum_all_devices(t):
    rows = t.shape[0]

    def body(t_ref, o_ref, slots, send, recv):
        x, y, c = lax.axis_index("x"), lax.axis_index("y"), lax.axis_index("c")
        me = 4 * x + 2 * y + c
        slots[me] = t_ref[...]
        copies = []
        for m in range(1, 8):
            peer = (x ^ (m >> 2), y ^ ((m >> 1) & 1), c ^ (m & 1))
            cp = pltpu.make_async_remote_copy(src_ref=t_ref, dst_ref=slots.at[me], send_sem=send.at[m - 1],
                                              recv_sem=recv.at[m - 1], device_id=peer, device_id_type=MESH_ID)
            cp.start()
            copies.append(cp)
        for cp in copies:
            cp.wait()
        acc = slots[0]
        for dev in range(1, 8):
            acc = acc + slots[dev]
        o_ref[...] = acc

    vmem = pl.BlockSpec(memory_space=pltpu.VMEM)
    return pl.pallas_call(
        body, name="sum_all_devices", in_specs=[vmem], out_specs=vmem,
        out_shape=jax.ShapeDtypeStruct(t.shape, F32),
        scratch_shapes=[pltpu.VMEM((8, rows, 128), F32), pltpu.SemaphoreType.DMA((7,)), pltpu.SemaphoreType.DMA((7,))],
        compiler_params=pltpu.CompilerParams(vmem_limit_bytes=VMEM_LIMIT, has_side_effects=True),
    )(t)


def _adamw(w, g, m, v):
    m = ADAM_B1 * m + (1.0 - ADAM_B1) * g
    v = ADAM_B2 * v + (1.0 - ADAM_B2) * jnp.square(g)
    m_hat = m / (1.0 - ADAM_B1 ** ADAM_STEP)
    v_hat = v / (1.0 - ADAM_B2 ** ADAM_STEP)
    return -ADAM_LR * (m_hat / (jnp.sqrt(v_hat) + ADAM_EPS) + ADAM_WD * w), m, v


def _chip_sum(received, name):
    rows, cols = received.shape[1:]
    tl = _row_block(rows, 512)

    def body(r_ref, o_ref):
        o_ref[...] = ((r_ref[0].astype(F32) + r_ref[1].astype(F32)) + r_ref[2].astype(F32)) + r_ref[3].astype(F32)

    return pl.pallas_call(body, name=name, grid=(rows // tl,),
                          in_specs=[pl.BlockSpec((N_CHIPS, tl, cols), lambda i: (0, i, 0))],
                          out_specs=pl.BlockSpec((tl, cols), lambda i: (i, 0)),
                          out_shape=jax.ShapeDtypeStruct((rows, cols), F32), compiler_params=_params("parallel"))(received)


def _adamw_pair(w, p_mine, p_other, m, v, name):
    def fn(wv, pa, pb, mv, vv):
        g = pa + pb
        return (g,) + _adamw(wv, g, mv, vv)

    return _rowmap(fn, [w, p_mine, p_other, m, v], "rrrrr", [(w.shape, F32, "r")] * 4, name=name, tl=256)


def _adamw_small(w, g, m, v):
    def fn(wv, gv, mv, vv):
        return _adamw(wv, gv, mv, vv)

    return _rowmap(fn, [w, g, m, v], "rrrr", [(w.shape, F32, "r")] * 3, name="adamw_small", tl=w.shape[0])


WEIGHTS = ['ffn1_norm', 'ffn1_w_gate', 'ffn1_w_up', 'ffn1_w_down', 'mix_norm', 'ffn2_norm', 'ffn2_w_gate', 'ffn2_w_up',
           'ffn2_w_down', 'ab_w_in', 's5_lambda_re', 's5_lambda_im', 's5_log_dt', 's5_b_re', 's5_b_im', 's5_c_re', 's5_c_im',
           's5_d', 's5_w_glu', 'ab_w_out', 'sc_w_in', 'sc_conv_w', 'sc_w_out', 'final_norm']
SHARDED = {'ffn1_w_gate': (1, FF_SLOT), 'ffn1_w_up': (1, FF_SLOT), 'ffn1_w_down': (0, FF_SLOT),
           'ffn2_w_gate': (1, FF_SLOT), 'ffn2_w_up': (1, FF_SLOT), 'ffn2_w_down': (0, FF_SLOT),
           'ab_w_in': (1, 512), 's5_w_glu': (0, 128), 'ab_w_out': (0, 256), 'sc_w_in': (1, 768), 'sc_conv_w': (1, 256),
           'sc_w_out': (0, 256)}
SMALL = [n for n in WEIGHTS if n not in SHARDED]


def _pack(arrays):
    rows = []
    for t in arrays:
        flat = t.reshape(-1)
        rows.append(jnp.pad(flat, (0, (-flat.shape[0]) % 128)))
    flat = jnp.concatenate(rows)
    return jnp.pad(flat, (0, (-flat.shape[0]) % 1024)).reshape(-1, 128)


def _unpack(packed, like):
    flat, out, pos = packed.reshape(-1), [], 0
    for t in like:
        out.append(flat[pos:pos + t.size].reshape(t.shape))
        pos += t.size + (-t.size) % 128
    return out


def _local_grads(x, target, p, full, late=None):
    small, grads, saved = {}, {}, []
    ffn = lambda which, layer: [full[(f"{which}_w_{n}", layer)] for n in ("gate", "up", "down")]
    for layer in range(2):
        x, s1 = _ffn_fwd(x, p["ffn1_norm"][layer:layer + 1], *ffn("ffn1", layer), tag=f"ffn1_{layer}")
        if layer == 0:
            x, sm = _mix0_fwd(x, p["mix_norm"][0:1], p, full, late)
        else:
            x, sm = _mix1_fwd(x, p["mix_norm"][1:2], full[("sc_w_in", 0)], full[("sc_conv_w", 0)], full[("sc_w_out", 0)])
        x, s2 = _ffn_fwd(x, p["ffn2_norm"][layer:layer + 1], *ffn("ffn2", layer), tag=f"ffn2_{layer}")
        saved.append((s1, sm, s2))
    loss, dx, dg_final = _loss_head(x, p["final_norm"][None], target)
    small["final_norm"] = dg_final[0]
    gains = {n: [None, None] for n in ("ffn1_norm", "mix_norm", "ffn2_norm")}

    def ffn_bwd(which, layer, dx, s):
        dx, (dg, dwg, dwu, dwd) = _ffn_bwd(dx, s, p[f"{which}_norm"][layer:layer + 1], *ffn(which, layer), tag=f"{which}_{layer}")
        gains[f"{which}_norm"][layer] = dg[0]
        grads.update({(f"{which}_w_gate", layer): dwg, (f"{which}_w_up", layer): dwu, (f"{which}_w_down", layer): dwd})
        return dx

    for layer in (1, 0):
        s1, sm, s2 = saved[layer]
        dx = ffn_bwd("ffn2", layer, dx, s2)
        if layer == 0:
            dx, dg, s5_grads = _mix0_bwd(dx, sm, p["mix_norm"][0:1], p, full, grads, late)
            small.update(s5_grads)
        else:
            dx, dg, dw_in, dcw, dw_out = _mix1_bwd(dx, sm, p["mix_norm"][1:2], full[("sc_w_in", 0)], full[("sc_conv_w", 0)],
                                                   full[("sc_w_out", 0)])
            grads.update({("sc_w_in", 0): dw_in, ("sc_conv_w", 0): dcw.astype(WIRE_DTYPE), ("sc_w_out", 0): dw_out})
        gains["mix_norm"][layer] = dg[0]
        dx = ffn_bwd("ffn1", layer, dx, s1)
    small.update({n: jnp.stack(pair) for n, pair in gains.items()})
    return loss, dx, small, grads


_EARLY = [("ffn1_w_gate", 0), ("ffn1_w_up", 0), ("ffn1_w_down", 0), ("ab_w_in", 0)]


class _Late:
    def __init__(self, shards, places):
        self.shards, self.places = shards, places
        self.gather_keys = [k for k in shards if k not in _EARLY]
        self.scatter_keys, self.received = [], {}

    def gather_job(self):
        return _Exchange("gather", [self.shards[k] for k in self.gather_keys], [self.places[k] for k in self.gather_keys])

    def scatter_job(self, grads):
        self.scatter_keys = list(grads)
        return _Exchange("scatter", [grads[k] for k in self.scatter_keys], [self.places[k] for k in self.scatter_keys])


def kernel(x, ffn1_norm, ffn1_w_gate, ffn1_w_up, ffn1_w_down, mix_norm, ffn2_norm, ffn2_w_gate, ffn2_w_up, ffn2_w_down, ab_w_in, s5_lambda_re, s5_lambda_im, s5_log_dt, s5_b_re, s5_b_im, s5_c_re, s5_c_im, s5_d, s5_w_glu, ab_w_out, sc_w_in, sc_conv_w, sc_w_out, final_norm, loss_target, m_ffn1_norm, m_ffn1_w_gate, m_ffn1_w_up, m_ffn1_w_down, m_mix_norm, m_ffn2_norm, m_ffn2_w_gate, m_ffn2_w_up, m_ffn2_w_down, m_ab_w_in, m_s5_lambda_re, m_s5_lambda_im, m_s5_log_dt, m_s5_b_re, m_s5_b_im, m_s5_c_re, m_s5_c_im, m_s5_d, m_s5_w_glu, m_ab_w_out, m_sc_w_in, m_sc_conv_w, m_sc_w_out, m_final_norm, v_ffn1_norm, v_ffn1_w_gate, v_ffn1_w_up, v_ffn1_w_down, v_mix_norm, v_ffn2_norm, v_ffn2_w_gate, v_ffn2_w_up, v_ffn2_w_down, v_ab_w_in, v_s5_lambda_re, v_s5_lambda_im, v_s5_log_dt, v_s5_b_re, v_s5_b_im, v_s5_c_re, v_s5_c_im, v_s5_d, v_s5_w_glu, v_ab_w_out, v_sc_w_in, v_sc_conv_w, v_sc_w_out, v_final_norm):
    args = dict(locals())
    p = {n: args[n] for n in WEIGHTS}
    mom = {n: args["m_" + n] for n in WEIGHTS}
    var = {n: args["v_" + n] for n in WEIGHTS}

    keys = [(n, layer) for n in SHARDED for layer in range(p[n].shape[0])]
    shards, places = {}, {}
    for n, layer in keys:
        axis, width = SHARDED[n]
        t = p[n][layer] if n == "sc_conv_w" else p[n][layer].astype(MXU_DTYPE)
        pad = [(0, 0), (0, 0)]
        pad[axis] = (0, width - t.shape[axis])
        shards[(n, layer)], places[(n, layer)] = jnp.pad(t, pad), (axis, width)
    job = _Exchange("gather", [shards[k] for k in _EARLY], [places[k] for k in _EARLY])
    full = dict(zip(_EARLY, _exchange_call(job, "gather_early")))
    late = _Late(shards, places)

    loss, dx, small, grads = _local_grads(x[0], loss_target[0], p, full, late)
    loss = lax.psum(loss[0, 0], ("x", "y", "c"))

    rest = [k for k in keys if k not in late.received]
    job = _Exchange("scatter", [grads[k] for k in rest], [places[k] for k in rest])
    late.received.update(zip(rest, _exchange_call(job, "scatter_last")))
    partial = [_chip_sum(late.received[(n, layer)], name=f"chip_sum_{n}_{layer}") for n, layer in keys]
    other = _swap_with_sibling(partial)
    out = {}
    stacked = {n: [] for n in SHARDED}
    for (n, layer), mine, theirs in zip(keys, partial, other):
        rows, cols = p[n].shape[1:]
        stacked[n].append(_adamw_pair(p[n][layer], mine[:rows, :cols], theirs[:rows, :cols], mom[n][layer], var[n][layer],
                                      name=f"adamw_{n}_{layer}"))
    for n in SHARDED:
        out[n] = [jnp.stack([res[k] for res in stacked[n]]) for k in range(4)]

    like = [p[n] for n in SMALL]
    g_small = _sum_all_devices(_pack([small[n] for n in SMALL]))
    d_small, m_small, v_small = _adamw_small(_pack(like), g_small, _pack([mom[n] for n in SMALL]), _pack([var[n] for n in SMALL]))
    for k, packed in enumerate((g_small, d_small, m_small, v_small)):
        for n, t in zip(SMALL, _unpack(packed, like)):
            out.setdefault(n, [None] * 4)[k] = t

    return (loss, dx[None], *[out[n][0] for n in WEIGHTS], *[out[n][1] for n in WEIGHTS],
            *[out[n][2] for n in WEIGHTS], *[out[n][3] for n in WEIGHTS])
```

```python
import functools
import math

import jax
import jax.numpy as jnp
from jax import lax
from jax.experimental import pallas as pl
from jax.experimental.pallas import tpu as pltpu

F32 = jnp.float32
MXU_DTYPE = jnp.bfloat16
WIRE_DTYPE = jnp.bfloat16
MESH_ID = pl.DeviceIdType.MESH

D_MODEL = 1024
D_FF = 2752
N_CHIPS = 4
FF_SHARD = D_FF // N_CHIPS
FF_SLOT = 768
FF_PAD = N_CHIPS * FF_SLOT
S5_WIDTH = 512
S5_GROUP = 16
S5_GROUPS = 32
S5_STATE = 64
S5_LANES = S5_GROUPS * S5_STATE
S5_BLOCK = 512
SB_HEADS = 8
SB_DH = 64
SB_SCALE = 0.125
SB_PACK = 2
SB_QUERIES = 1024
SB_KEYS = 256
EPS = 1e-6
ADAM_LR, ADAM_B1, ADAM_B2, ADAM_EPS, ADAM_WD, ADAM_STEP = 0.001, 0.9, 0.999, 1e-08, 0.01, 10
VMEM_LIMIT = 56 * 1024 * 1024

ANY_SPEC = pl.BlockSpec(memory_space=pl.ANY)


def _params(*sem):
    return pltpu.CompilerParams(dimension_semantics=sem or None, vmem_limit_bytes=VMEM_LIMIT)


def _mm(a, b, *, name, ta=False, tb=False, out_dtype=F32, epilogue=None, extras=(), tm=1024, tn=1024, tk=1024):
    m, k = (a.shape[1], a.shape[0]) if ta else a.shape
    n = b.shape[0] if tb else b.shape[1]
    tm, tn, tk = min(tm, m), min(tn, n), min(tk, k)
    assert m % tm == 0 and n % tn == 0 and k % tk == 0, (name, m, n, k)
    nk = k // tk
    a_spec = pl.BlockSpec((tk, tm), lambda i, j, kk: (kk, i)) if ta else pl.BlockSpec((tm, tk), lambda i, j, kk: (i, kk))
    b_spec = pl.BlockSpec((tn, tk), lambda i, j, kk: (j, kk)) if tb else pl.BlockSpec((tk, tn), lambda i, j, kk: (kk, j))
    ex_specs = []
    for e in extras:
        if e.shape == (m, n):
            ex_specs.append(pl.BlockSpec((tm, tn), lambda i, j, kk: (i, j)))
        elif e.shape == (1, n):
            ex_specs.append(pl.BlockSpec((1, tn), lambda i, j, kk: (0, j)))
        else:
            assert e.shape == (m, 1), (name, e.shape)
            ex_specs.append(pl.BlockSpec((tm, 1), lambda i, j, kk: (i, 0)))
    dims = (((0 if ta else 1,), (1 if tb else 0,)), ((), ()))
    n_ex = len(extras)

    def body(a_ref, b_ref, *rest):
        ex_refs, o_ref = rest[:n_ex], rest[n_ex]
        part = lax.dot_general(a_ref[...].astype(MXU_DTYPE), b_ref[...].astype(MXU_DTYPE), dims, preferred_element_type=F32)

        def finish(r):
            if epilogue is not None:
                r = epilogue(r, *[e[...] for e in ex_refs])
            o_ref[...] = r.astype(out_dtype)

        if nk == 1:
            finish(part)
            return
        acc_ref, kk = rest[n_ex + 1], pl.program_id(2)

        @pl.when(kk == 0)
        def _():
            acc_ref[...] = part

        @pl.when(jnp.logical_and(kk > 0, kk < nk - 1))
        def _():
            acc_ref[...] += part

        @pl.when(kk == nk - 1)
        def _():
            finish(acc_ref[...] + part)

    return pl.pallas_call(
        body, name=name, grid=(m // tm, n // tn, nk),
        in_specs=[a_spec, b_spec, *ex_specs],
        out_specs=pl.BlockSpec((tm, tn), lambda i, j, kk: (i, j)),
        out_shape=jax.ShapeDtypeStruct((m, n), out_dtype),
        scratch_shapes=[pltpu.VMEM((tm, tn), F32)] if nk > 1 else [],
        compiler_params=_params("parallel", "parallel", "arbitrary"),
    )(a, b, *extras)


def _row_block(rows, want):
    for tl in range(min(want, rows), 7, -1):
        if rows % tl == 0 and tl % 8 == 0:
            return tl
    return rows


def _rowmap(fn, ins, in_kinds, outs, *, name, tl):
    rows = next(x.shape[0] for x, kd in zip(ins, in_kinds) if kd == "r")
    tl = _row_block(rows, tl)
    n_in = len(ins)

    def spec(shape, kind):
        if kind == "r":
            return pl.BlockSpec((tl,) + tuple(shape[1:]), lambda i: (i,) + (0,) * (len(shape) - 1))
        return pl.BlockSpec(tuple(shape), lambda i: (0,) * len(shape))

    def body(*refs):
        in_refs, out_refs = refs[:n_in], refs[n_in:]
        res = fn(*[r[...] for r in in_refs])
        if not isinstance(res, (tuple, list)):
            res = (res,)
        for o_ref, val, (_, dt, kind) in zip(out_refs, res, outs):
            if kind == "r":
                o_ref[...] = val.astype(dt)
            else:
                @pl.when(pl.program_id(0) == 0)
                def _():
                    o_ref[...] = jnp.zeros_like(o_ref)

                o_ref[...] += val.astype(dt)

    has_acc = any(kd == "a" for _, _, kd in outs)
    res = pl.pallas_call(
        body, name=name, grid=(rows // tl,),
        in_specs=[spec(x.shape, kd) for x, kd in zip(ins, in_kinds)],
        out_specs=[spec(s, kd) for s, _, kd in outs],
        out_shape=[jax.ShapeDtypeStruct(s, dt) for s, dt, _ in outs],
        compiler_params=_params("arbitrary" if has_acc else "parallel"),
    )(*ins)
    return res[0] if len(outs) == 1 else res


def _rms_fwd(x, g):
    r = lax.rsqrt(jnp.mean(x * x, axis=-1, keepdims=True) + EPS)
    return x * r * g


def _rms_bwd(dh, x, g):
    r = lax.rsqrt(jnp.mean(x * x, axis=-1, keepdims=True) + EPS)
    xh = x * r
    dxh = dh * g
    dx = r * (dxh - xh * jnp.mean(dxh * xh, axis=-1, keepdims=True))
    return dx, jnp.sum(dh * xh, axis=0, keepdims=True)


def _norm(x, g, *, name):
    return _rowmap(lambda xv, gv: _rms_fwd(xv, gv), [x, g], "rc", [(x.shape, MXU_DTYPE, "r")], name=name, tl=256)


def _norm_bwd(dh, x, g, dres, *, name):
    def fn(dhv, xv, gv, drv):
        dx, dg = _rms_bwd(dhv, xv, gv)
        return dx + drv, dg
    return _rowmap(fn, [dh, x, g, dres], "rrcr", [(x.shape, F32, "r"), (g.shape, F32, "a")], name=name, tl=256)


def _swiglu_act(a, b):
    return jax.nn.silu(a) * b


def _ffn_fwd(x, g, wg, wu, wd, tag):
    h = _norm(x, g, name=f"{tag}_norm")
    a = _mm(h, wg, name=f"{tag}_gate")
    b = _mm(h, wu, name=f"{tag}_up")
    s = _rowmap(_swiglu_act, [a, b], "rr", [(a.shape, MXU_DTYPE, "r")], name=f"{tag}_act", tl=128)
    x2 = _mm(s, wd, name=f"{tag}_down", epilogue=lambda acc, xv: xv + 0.5 * acc, extras=[x])
    return x2, (x, h, a, b, s)


def _ffn_bwd(dx2, saved, g, wg, wu, wd, tag):
    x, h, a, b, s = saved
    ds = _mm(dx2, wd, tb=True, name=f"{tag}_dact")

    def act_bwd(dsv, av, bv):
        _, vjp = jax.vjp(_swiglu_act, av, bv)
        return vjp(0.5 * dsv)

    da, db = _rowmap(act_bwd, [ds, a, b], "rrr", [(a.shape, MXU_DTYPE, "r")] * 2, name=f"{tag}_act_bwd", tl=128)
    dwd = _mm(s, dx2, ta=True, name=f"{tag}_dwd", out_dtype=WIRE_DTYPE, epilogue=lambda acc: 0.5 * acc)
    dwg = _mm(h, da, ta=True, name=f"{tag}_dwg", out_dtype=WIRE_DTYPE)
    dwu = _mm(h, db, ta=True, name=f"{tag}_dwu", out_dtype=WIRE_DTYPE)
    dh = _mm(da, wg, tb=True, name=f"{tag}_dh1")
    dh = _mm(db, wu, tb=True, name=f"{tag}_dh2", epilogue=lambda acc, prev: acc + prev, extras=[dh])
    dx, dg = _norm_bwd(dh, x, g, dx2, name=f"{tag}_norm_bwd")
    return dx, (dg, dwg, dwu, dwd)


def _softplus(z):
    return jnp.maximum(z, 0.0) + jnp.log(1.0 + jnp.exp(-jnp.abs(z)))


def _ones_dot(x, tri):
    if MXU_DTYPE == F32:
        return jnp.dot(x, tri, preferred_element_type=F32)
    hi = x.astype(MXU_DTYPE)
    lo = (x - hi.astype(F32)).astype(MXU_DTYPE)
    return jnp.dot(hi, tri, preferred_element_type=F32) + jnp.dot(lo, tri, preferred_element_type=F32)


NT_DIMS = (((1,), (1,)), ((), ()))
TN_DIMS = (((0,), (0,)), ((), ()))


def _sb_fwd(q, k, v, *, tq=SB_QUERIES, job=None):
    nh, seq, dh = q.shape
    tq = min(tq, seq)
    hp = SB_PACK

    tk = min(SB_KEYS, tq)
    per = tq // tk

    def body(q_ref, k_ref, v_ref, o_ref, ls_ref):
        i = pl.program_id(1)
        r_idx = lax.broadcasted_iota(jnp.int32, (tk, tk), 0)
        c_idx = lax.broadcasted_iota(jnp.int32, (tk, tk), 1)
        after = (r_idx > c_idx).astype(MXU_DTYPE)
        q_pos = lax.broadcasted_iota(jnp.int32, (tq, tk), 0)
        k_pos = lax.broadcasted_iota(jnp.int32, (tq, tk), 1)

        def block(hd, j, c, acc, straddles):
            off = pl.multiple_of(j * tk, tk)
            kv = k_ref[hd, pl.ds(off, tk), :]
            vv = v_ref[hd, pl.ds(off, tk), :]
            z = lax.dot_general(q_ref[hd], kv, NT_DIMS, preferred_element_type=F32)
            sp = _softplus(z)
            if straddles is None:
                lk = -sp
                w = jnp.exp(z - sp + _ones_dot(lk, after) + c)
            else:
                before = k_pos + straddles * tk < q_pos
                lk = jnp.where(before, -sp, 0.0)
                w = jnp.where(before, jnp.exp(z - sp + _ones_dot(lk, after) + c), 0.0)
            acc = acc + jnp.dot(w.astype(MXU_DTYPE), vv, preferred_element_type=F32)
            return c + jnp.sum(lk, axis=1, keepdims=True), acc

        def step(n, carry):
            return tuple(block(hd, i * per - 1 - n, *carry[hd], None) for hd in range(hp))

        carry = tuple((jnp.zeros((tq, 1), F32), jnp.zeros((tq, dh), F32)) for _ in range(hp))
        for s in reversed(range(per)):
            carry = tuple(block(hd, i * per + s, *carry[hd], s) for hd in range(hp))
        for hd, (c, acc) in enumerate(lax.fori_loop(0, i * per, step, carry)):
            o_ref[hd] = acc
            ls_ref[hd] = c

    whole = pl.BlockSpec((hp, seq, dh), lambda h, i: (h, 0, 0))
    return _carried_call(
        body, name="sb_fwd", grid=(nh // hp, seq // tq),
        in_specs=[pl.BlockSpec((hp, tq, dh), lambda h, i: (h, i, 0)), whole, whole],
        out_specs=[pl.BlockSpec((hp, tq, dh), lambda h, i: (h, i, 0)), pl.BlockSpec((hp, tq, 1), lambda h, i: (h, i, 0))],
        out_shape=[jax.ShapeDtypeStruct((nh, seq, dh), F32), jax.ShapeDtypeStruct((nh, seq, 1), F32)],
        semantics=("parallel", "parallel"), operands=(q, k, v), job=job)


def _sb_bwd(q, k, v, lsum, do, *, tq=SB_QUERIES, job=None):
    nh, seq, dh = q.shape
    tq = min(tq, seq)
    tk = min(SB_KEYS, tq)
    per = tq // tk
    hp = SB_PACK
    scale = SB_SCALE

    def body(q_ref, k_ref, v_ref, ls_ref, do_ref, dq_ref, dk_ref, dv_ref):
        i = pl.program_id(1)

        @pl.when(i == 0)
        def _():
            dk_ref[...] = jnp.zeros_like(dk_ref)
            dv_ref[...] = jnp.zeros_like(dv_ref)

        r_idx = lax.broadcasted_iota(jnp.int32, (tk, tk), 0)
        c_idx = lax.broadcasted_iota(jnp.int32, (tk, tk), 1)
        upto = (r_idx <= c_idx).astype(MXU_DTYPE)
        before = (r_idx < c_idx).astype(MXU_DTYPE)
        q_pos = lax.broadcasted_iota(jnp.int32, (tq, tk), 0)
        k_pos = lax.broadcasted_iota(jnp.int32, (tq, tk), 1)

        def block(hd, j, cp, ce, dq, straddles):
            off = pl.multiple_of(j * tk, tk)
            qv, dov = q_ref[hd], do_ref[hd].astype(MXU_DTYPE)
            kv = k_ref[hd, pl.ds(off, tk), :]
            vv = v_ref[hd, pl.ds(off, tk), :]
            z = lax.dot_general(qv, kv, NT_DIMS, preferred_element_type=F32)
            sp = _softplus(z)
            valid = None if straddles is None else k_pos + straddles * tk < q_pos
            lk = -sp if valid is None else jnp.where(valid, -sp, 0.0)
            w = jnp.exp(z - sp + (ls_ref[hd] - cp) - _ones_dot(lk, upto))
            if valid is not None:
                w = jnp.where(valid, w, 0.0)
            e = w * lax.dot_general(dov, vv, NT_DIMS, preferred_element_type=F32)
            earlier = _ones_dot(e, before) + ce
            keep = jnp.exp(-sp)
            dz = e * keep - (1.0 - keep) * earlier
            if valid is not None:
                dz = jnp.where(valid, dz, 0.0)
            dzm = dz.astype(MXU_DTYPE)
            dq = dq + jnp.dot(dzm, kv, preferred_element_type=F32)
            dk_ref[hd, pl.ds(off, tk), :] += lax.dot_general(dzm, qv, TN_DIMS, preferred_element_type=F32)
            dv_ref[hd, pl.ds(off, tk), :] += lax.dot_general(w.astype(MXU_DTYPE), dov, TN_DIMS, preferred_element_type=F32)
            return cp + jnp.sum(lk, axis=1, keepdims=True), ce + jnp.sum(e, axis=1, keepdims=True), dq

        def step(j, carry):
            return tuple(block(hd, j, *carry[hd], None) for hd in range(hp))

        zero = jnp.zeros((tq, 1), F32)
        carry = lax.fori_loop(0, i * per, step, tuple((zero, zero, jnp.zeros((tq, dh), F32)) for _ in range(hp)))
        for s in range(per):
            carry = tuple(block(hd, i * per + s, *carry[hd], s) for hd in range(hp))
        for hd in range(hp):
            dq_ref[hd] = carry[hd][2] * scale

    whole = pl.BlockSpec((hp, seq, dh), lambda h, i: (h, 0, 0))
    tile = pl.BlockSpec((hp, tq, dh), lambda h, i: (h, i, 0))
    return _carried_call(
        body, name="sb_bwd", grid=(nh // hp, seq // tq),
        in_specs=[tile, whole, whole, pl.BlockSpec((hp, tq, 1), lambda h, i: (h, i, 0)), tile],
        out_specs=[tile, whole, whole],
        out_shape=[jax.ShapeDtypeStruct((nh, seq, dh), F32)] * 3,
        semantics=("parallel", "arbitrary"), operands=(q, k, v, lsum, do), job=job)


def _heads(t):
    return t.reshape(t.shape[0], SB_HEADS, SB_DH).transpose(1, 0, 2)


def _unheads(t):
    return t.transpose(1, 0, 2).reshape(t.shape[1], SB_HEADS * SB_DH)


def _s5_disc(lr, li, ldt, br, bi):
    dt = jnp.exp(ldt)
    mag = jnp.exp(lr * dt)
    ar = mag * jnp.cos(li * dt)
    ai = mag * jnp.sin(li * dt)
    den = lr * lr + li * li
    nr = ar - 1.0
    cr = (nr * lr + ai * li) / den
    ci = (ai * lr - nr * li) / den
    return ar, ai, cr[None] * br - ci[None] * bi, cr[None] * bi + ci[None] * br


def _s5_prep(lr, li, ldt, br, bi):
    shapes = [lr.shape, lr.shape, br.shape, br.shape]

    def body(lr_ref, li_ref, ldt_ref, br_ref, bi_ref, *outs):
        for o, val in zip(outs, _s5_disc(lr_ref[...], li_ref[...], ldt_ref[...], br_ref[...], bi_ref[...])):
            o[...] = val

    return pl.pallas_call(body, name="s5_prep", out_shape=[jax.ShapeDtypeStruct(s, F32) for s in shapes])(lr, li, ldt, br, bi)


def _s5_prep_bwd(lr, li, ldt, br, bi, cts):
    args = (lr, li, ldt, br, bi)

    def body(*refs):
        ins, ct_refs, outs = refs[:5], refs[5:9], refs[9:]
        _, vjp = jax.vjp(_s5_disc, *[r[...] for r in ins])
        for o, val in zip(outs, vjp(tuple(r[...] for r in ct_refs))):
            o[...] = val

    return pl.pallas_call(body, name="s5_prep_bwd", out_shape=[jax.ShapeDtypeStruct(a.shape, F32) for a in args])(*args, *cts)


def _s5_scan(bu, a, *, tc=512):
    seq, w2 = bu.shape
    tw = S5_BLOCK
    tc = min(tc, seq)
    assert seq % tc == 0 and w2 % (2 * tw) == 0

    def body(bu_ref, a_ref, h_ref, cr_ref, ci_ref):
        @pl.when(pl.program_id(1) == 0)
        def _():
            cr_ref[...] = jnp.zeros_like(cr_ref)
            ci_ref[...] = jnp.zeros_like(ci_ref)

        re, im = pl.ds(0, tw), pl.ds(tw, tw)
        ar, ai = a_ref[:, re], a_ref[:, im]

        def step(t, carry):
            hr, hi = carry
            row = pl.ds(t, 1)
            nr = ar * hr - ai * hi + bu_ref[row, re]
            ni = ar * hi + ai * hr + bu_ref[row, im]
            h_ref[row, re] = nr
            h_ref[row, im] = ni
            return nr, ni

        hr, hi = lax.fori_loop(0, tc, step, (cr_ref[...], ci_ref[...]), unroll=8)
        cr_ref[...] = hr
        ci_ref[...] = hi

    blk = pl.BlockSpec((tc, 2 * tw), lambda j, t: (t, j))
    return pl.pallas_call(
        body, name="s5_scan", grid=(w2 // (2 * tw), seq // tc),
        in_specs=[blk, pl.BlockSpec((1, 2 * tw), lambda j, t: (0, j))],
        out_specs=blk,
        out_shape=jax.ShapeDtypeStruct((seq, w2), F32),
        scratch_shapes=[pltpu.VMEM((1, tw), F32)] * 2,
        compiler_params=_params("parallel", "arbitrary"),
    )(bu, a)


def _s5_scan_bwd(d, h, a, *, tc=512):
    seq, w2 = d.shape
    tw = S5_BLOCK
    tc = min(tc, seq)
    assert seq % tc == 0 and w2 % (2 * tw) == 0
    nt = seq // tc

    def body(d_ref, h_ref, a_ref, g_ref, da_ref, cr_ref, ci_ref):
        @pl.when(pl.program_id(1) == 0)
        def _():
            cr_ref[...] = jnp.zeros_like(cr_ref)
            ci_ref[...] = jnp.zeros_like(ci_ref)
            da_ref[...] = jnp.zeros_like(da_ref)

        re, im = pl.ds(0, tw), pl.ds(tw, tw)
        ar, ai = a_ref[:, re], a_ref[:, im]

        def step(n, carry):
            gr, gi, sr, si = carry
            row = pl.ds(tc - 1 - n, 1)
            hrt, hit = h_ref[row, re], h_ref[row, im]
            sr = sr + gr * hrt + gi * hit
            si = si + gi * hrt - gr * hit
            ngr = d_ref[row, re] + ar * gr + ai * gi
            ngi = d_ref[row, im] + ar * gi - ai * gr
            g_ref[row, re] = ngr
            g_ref[row, im] = ngi
            return ngr, ngi, sr, si

        gr, gi, sr, si = lax.fori_loop(0, tc, step, (cr_ref[...], ci_ref[...], da_ref[:, re], da_ref[:, im]), unroll=8)
        cr_ref[...] = gr
        ci_ref[...] = gi
        da_ref[:, re] = sr
        da_ref[:, im] = si

    blk = pl.BlockSpec((tc, 2 * tw), lambda j, t: (nt - 1 - t, j))
    row = pl.BlockSpec((1, 2 * tw), lambda j, t: (0, j))
    return pl.pallas_call(
        body, name="s5_scan_bwd", grid=(w2 // (2 * tw), nt),
        in_specs=[blk, blk, row],
        out_specs=[blk, row],
        out_shape=[jax.ShapeDtypeStruct((seq, w2), F32), jax.ShapeDtypeStruct((1, w2), F32)],
        scratch_shapes=[pltpu.VMEM((1, tw), F32)] * 2,
        compiler_params=_params("parallel", "arbitrary"),
    )(d, h, a)


def _pair_columns(re, im, axis):
    shape = re.shape
    split = shape[:axis] + (shape[axis] // S5_BLOCK, S5_BLOCK) + shape[axis + 1:]
    both = jnp.stack([re.reshape(split), im.reshape(split)], axis=axis + 1)
    return both.reshape(shape[:axis] + (2 * shape[axis],) + shape[axis + 1:])


def _unpair_columns(t, axis):
    shape = t.shape
    both = t.reshape(shape[:axis] + (shape[axis] // (2 * S5_BLOCK), 2, S5_BLOCK) + shape[axis + 1:])
    half = shape[:axis] + (shape[axis] // 2,) + shape[axis + 1:]
    return (lax.index_in_dim(both, 0, axis + 1, keepdims=False).reshape(half),
            lax.index_in_dim(both, 1, axis + 1, keepdims=False).reshape(half))


def _block_diag(t):
    g, a, b = t.shape
    eye = jnp.eye(g, dtype=t.dtype)
    return (t[:, :, None, :] * eye[:, None, :, None]).reshape(g * a, g * b)


def _block_diag_part(m, g):
    a, b = m.shape[0] // g, m.shape[1] // g
    return jnp.moveaxis(jnp.diagonal(m.reshape(g, a, g, b), axis1=0, axis2=2), -1, 0)


def _gelu_glu(y, gate_pre):
    z = jax.nn.gelu(y)
    return z * jax.nn.sigmoid(gate_pre)


def _s5_fwd(u, p, w_glu):
    lr, li = p["s5_lambda_re"][0], p["s5_lambda_im"][0]
    ldt = p["s5_log_dt"][0][:, None]
    br = p["s5_b_re"][0].transpose(2, 0, 1)
    bi = p["s5_b_im"][0].transpose(2, 0, 1)
    ar, ai, bbr, bbi = _s5_prep(lr, li, ldt, br, bi)
    a = _pair_columns(ar.reshape(1, S5_LANES), ai.reshape(1, S5_LANES), 1)
    bmat = _pair_columns(_block_diag(bbr.transpose(1, 0, 2)), _block_diag(bbi.transpose(1, 0, 2)), 1)
    cmat = _pair_columns(_block_diag(p["s5_c_re"][0].transpose(0, 2, 1)),
                         -_block_diag(p["s5_c_im"][0].transpose(0, 2, 1)), 0)
    bmat, cmat = bmat.astype(MXU_DTYPE), cmat.astype(MXU_DTYPE)
    bu = _mm(u, bmat, name="s5_bu")
    h = _s5_scan(bu, a)
    d = p["s5_d"]
    y = _mm(h, cmat, name="s5_y", epilogue=lambda acc, uv, dv: acc + dv * uv, extras=[u, d])
    z = _rowmap(jax.nn.gelu, [y], "r", [(y.shape, MXU_DTYPE, "r")], name="s5_gelu", tl=512)
    gate_pre = _mm(z, w_glu, name="s5_glu")
    out = _rowmap(_gelu_glu, [y, gate_pre], "rr", [(y.shape, F32, "r")], name="s5_gate", tl=512)
    return out, (u, lr, li, ldt, br, bi, a, bmat, cmat, h, y, z, gate_pre)


def _s5_bwd(dout, saved, p, w_glu):
    u, lr, li, ldt, br, bi, a, bmat, cmat, h, y, z, gate_pre = saved
    d = p["s5_d"]

    def gate_bwd(dov, yv, gv):
        zv = jax.nn.gelu(yv)
        sg = jax.nn.sigmoid(gv)
        return dov * sg, dov * zv * sg * (1.0 - sg)

    dz_direct, dgate = _rowmap(gate_bwd, [dout, y, gate_pre], "rrr", [(y.shape, F32, "r"), (y.shape, MXU_DTYPE, "r")],
                               name="s5_gate_bwd", tl=512)
    dw_glu = _mm(z, dgate, ta=True, name="s5_dwglu", out_dtype=WIRE_DTYPE)
    dz = _mm(dgate, w_glu, tb=True, name="s5_dz", epilogue=lambda acc, prev: acc + prev, extras=[dz_direct])

    def gelu_bwd(dzv, yv, uv, dvv):
        _, vjp = jax.vjp(jax.nn.gelu, yv)
        dy = vjp(dzv)[0]
        return dy, dy * dvv, jnp.sum(dy * uv, axis=0, keepdims=True)

    dy, du_skip, dd = _rowmap(gelu_bwd, [dz, y, u, d], "rrrc",
                              [(y.shape, F32, "r"), (y.shape, F32, "r"), (d.shape, F32, "a")], name="s5_gelu_bwd", tl=512)
    dcmat = _mm(h, dy, ta=True, name="s5_dc")
    dstate = _mm(dy, cmat, tb=True, name="s5_dstate")
    g, da = _s5_scan_bwd(dstate, h, a)
    du = _mm(g, bmat, tb=True, name="s5_du", epilogue=lambda acc, prev: acc + prev, extras=[du_skip])
    dbmat = _mm(u, g, ta=True, name="s5_db")
    dbbr, dbbi = (_block_diag_part(t, S5_GROUPS).transpose(1, 0, 2) for t in _unpair_columns(dbmat, 1))
    dar, dai = _unpair_columns(da, 1)
    cts = (dar.reshape(S5_GROUPS, S5_STATE), dai.reshape(S5_GROUPS, S5_STATE), dbbr, dbbi)
    dlr, dli, dldt, dbr, dbi = _s5_prep_bwd(lr, li, ldt, br, bi, cts)
    dcr, dci = (_block_diag_part(t, S5_GROUPS).transpose(0, 2, 1) for t in _unpair_columns(dcmat, 0))
    grads = {
        "s5_lambda_re": dlr[None], "s5_lambda_im": dli[None], "s5_log_dt": dldt[:, 0][None],
        "s5_b_re": dbr.transpose(1, 2, 0)[None], "s5_b_im": dbi.transpose(1, 2, 0)[None],
        "s5_c_re": dcr[None], "s5_c_im": -dci[None], "s5_d": dd,
    }
    return du, dw_glu, grads


def _mix0_fwd(x, g, p, full, late):
    h = _norm(x, g, name="mix0_norm")
    proj = _mm(h, full[("ab_w_in", 0)], name="mix0_in")
    u = proj[:, :S5_WIDTH]
    q, k, v = (_heads(proj[:, S5_WIDTH * (1 + n):S5_WIDTH * (2 + n)] * (SB_SCALE if n == 0 else 1.0)).astype(MXU_DTYPE)
               for n in range(3))
    (o, lsum), got = _sb_fwd(q, k, v, job=late.gather_job() if late else None)
    if late:
        full.update(zip(late.gather_keys, got))
    w_glu, w_out = full[("s5_w_glu", 0)], full[("ab_w_out", 0)]
    y_a, s5_saved = _s5_fwd(u, p, w_glu)
    mix = jnp.concatenate([y_a, _unheads(o)], axis=1).astype(MXU_DTYPE)
    x2 = _mm(mix, w_out, name="mix0_out", epilogue=lambda acc, xv: xv + acc, extras=[x])
    return x2, (x, h, q, k, v, lsum, mix, s5_saved)


def _mix0_bwd(dx2, saved, g, p, full, grads, late):
    x, h, q, k, v, lsum, mix, s5_saved = saved
    w_in, w_glu, w_out = full[("ab_w_in", 0)], full[("s5_w_glu", 0)], full[("ab_w_out", 0)]
    dmix = _mm(dx2, w_out, tb=True, name="mix0_dmix")
    grads[("ab_w_out", 0)] = _mm(mix, dx2, ta=True, name="mix0_dwout", out_dtype=WIRE_DTYPE)
    du, grads[("s5_w_glu", 0)], s5_grads = _s5_bwd(dmix[:, :S5_WIDTH], s5_saved, p, w_glu)
    (dq, dk, dv), got = _sb_bwd(q, k, v, lsum, _heads(dmix[:, S5_WIDTH:]), job=late.scatter_job(grads) if late else None)
    if late:
        late.received.update(zip(late.scatter_keys, got))
    dproj = jnp.concatenate([du, _unheads(dq), _unheads(dk), _unheads(dv)], axis=1)
    grads[("ab_w_in", 0)] = _mm(h, dproj, ta=True, name="mix0_dwin", out_dtype=WIRE_DTYPE)
    dh = _mm(dproj, w_in, tb=True, name="mix0_dh")
    dx, dg = _norm_bwd(dh, x, g, dx2, name="mix0_norm_bwd")
    return dx, dg, s5_grads


def _shift_down(t, n):
    rows = lax.broadcasted_iota(jnp.int32, t.shape, 0)
    return jnp.where(rows >= n, pltpu.roll(t, n, 0), 0.0)


def _shift_up(t, n):
    rows = lax.broadcasted_iota(jnp.int32, t.shape, 0)
    return jnp.where(rows < t.shape[0] - n, pltpu.roll(t, t.shape[0] - n, 0), 0.0)


def _conv_fwd(proj, cw, *, tc=128):
    seq, c3 = proj.shape
    ch = c3 // 3
    nb = ch // tc

    def body(b_ref, c_ref, v_ref, w_ref, m_ref):
        pv = c_ref[...] * v_ref[...]
        w = w_ref[...]
        y = w[2:3] * pv + w[1:2] * _shift_down(pv, 1) + w[0:1] * _shift_down(pv, 2)
        m_ref[...] = (b_ref[...] * y).astype(m_ref.dtype)

    col = lambda part: pl.BlockSpec((seq, tc), lambda j: (0, part * nb + j))
    return pl.pallas_call(
        body, name="conv_fwd", grid=(nb,),
        in_specs=[col(0), col(1), col(2), pl.BlockSpec((3, tc), lambda j: (0, j))],
        out_specs=pl.BlockSpec((seq, tc), lambda j: (0, j)),
        out_shape=jax.ShapeDtypeStruct((seq, ch), MXU_DTYPE),
        compiler_params=_params("parallel"),
    )(proj, proj, proj, cw)


def _conv_bwd(proj, cw, dm, *, tc=128):
    seq, c3 = proj.shape
    ch = c3 // 3
    nb = ch // tc

    def body(b_ref, c_ref, v_ref, w_ref, dm_ref, db_ref, dc_ref, dv_ref, dw_ref):
        cv, vv, dmv = c_ref[...], v_ref[...], dm_ref[...]
        pv = cv * vv
        w = w_ref[...]
        p1, p2 = _shift_down(pv, 1), _shift_down(pv, 2)
        y = w[2:3] * pv + w[1:2] * p1 + w[0:1] * p2
        db_ref[...] = dmv * y
        dy = dmv * b_ref[...]
        dp = w[2:3] * dy + w[1:2] * _shift_up(dy, 1) + w[0:1] * _shift_up(dy, 2)
        dc_ref[...] = dp * vv
        dv_ref[...] = dp * cv
        dw_ref[...] = jnp.concatenate([jnp.sum(dy * p2, axis=0, keepdims=True), jnp.sum(dy * p1, axis=0, keepdims=True),
                                       jnp.sum(dy * pv, axis=0, keepdims=True)], axis=0)

    col = lambda part: pl.BlockSpec((seq, tc), lambda j: (0, part * nb + j))
    one = pl.BlockSpec((seq, tc), lambda j: (0, j))
    small = pl.BlockSpec((3, tc), lambda j: (0, j))
    return pl.pallas_call(
        body, name="conv_bwd", grid=(nb,),
        in_specs=[col(0), col(1), col(2), small, one],
        out_specs=[one, one, one, small],
        out_shape=[jax.ShapeDtypeStruct((seq, ch), F32)] * 3 + [jax.ShapeDtypeStruct((3, ch), F32)],
        compiler_params=_params("parallel"),
    )(proj, proj, proj, cw, dm)


def _mix1_fwd(x, g, w_in, cw, w_out):
    h = _norm(x, g, name="mix1_norm")
    proj = _mm(h, w_in, name="mix1_in")
    m = _conv_fwd(proj, cw)
    x2 = _mm(m, w_out, name="mix1_out", epilogue=lambda acc, xv: xv + acc, extras=[x])
    return x2, (x, h, proj, m)


def _mix1_bwd(dx2, saved, g, w_in, cw, w_out):
    x, h, proj, m = saved
    dm = _mm(dx2, w_out, tb=True, name="mix1_dm")
    dw_out = _mm(m, dx2, ta=True, name="mix1_dwout", out_dtype=WIRE_DTYPE)
    db, dc, dv, dcw = _conv_bwd(proj, cw, dm)
    dproj = jnp.concatenate([db, dc, dv], axis=1)
    dw_in = _mm(h, dproj, ta=True, name="mix1_dwin", out_dtype=WIRE_DTYPE)
    dh = _mm(dproj, w_in, tb=True, name="mix1_dh")
    dx, dg = _norm_bwd(dh, x, g, dx2, name="mix1_norm_bwd")
    return dx, dg, dw_in, dcw, dw_out


def _loss_head(x, g, target):
    feat = x.shape[1]

    def fn(xv, gv, tv):
        err = _rms_fwd(xv, gv) - tv
        dx, dg = _rms_bwd(err / feat, xv, gv)
        return jnp.sum(err * err, keepdims=True) * (0.5 / feat), dx, dg

    return _rowmap(fn, [x, g, target], "rcr", [((1, 1), F32, "a"), (x.shape, F32, "r"), (g.shape, F32, "a")],
                   name="loss_head", tl=256)


def _slot(ref, place, chip=None, half=None):
    axis, width = place
    shape = list(ref.shape)
    start = [0, 0]
    if chip is not None:
        start[axis], shape[axis] = chip * width, width
    if half is not None:
        h_axis = 0 if shape[0] % 32 == 0 else 1
        shape[h_axis] //= 2
        start[h_axis] = start[h_axis] + half * shape[h_axis]
    hint = lambda s, d: s if isinstance(s, int) else pl.multiple_of(s, 128 if d == 1 else 8)
    return ref.at[tuple(pl.ds(hint(s, d), n) for d, (s, n) in enumerate(zip(start, shape)))]


class _Exchange:
    def __init__(self, kind, arrays, places):
        self.kind, self.arrays, self.places, self.n = kind, list(arrays), list(places), len(arrays)
        self.out_shape = []
        for t, (axis, width) in zip(self.arrays, self.places):
            if kind == "gather":
                shape = list(t.shape)
                shape[axis] = N_CHIPS * width
            else:
                shape = [N_CHIPS] + list(t.shape)
                shape[1 + axis] = width
            self.out_shape.append(jax.ShapeDtypeStruct(tuple(shape), t.dtype))
        n = self.n
        self.scratch = [pltpu.SemaphoreType.DMA((3 * n,)) for _ in range(4 if kind == "gather" else 2)]
        self.scratch.append(pltpu.SemaphoreType.DMA((n,)))

    def _copies(self, ins, outs, sems):
        x, y, c = lax.axis_index("x"), lax.axis_index("y"), lax.axis_index("c")
        peers = [(1 - x, y), (x, 1 - y), (1 - x, 1 - y)]
        remote = lambda src, dst, send, recv, k, to: pltpu.make_async_remote_copy(
            src_ref=src, dst_ref=dst, send_sem=send.at[k], recv_sem=recv.at[k], device_id=to, device_id_type=MESH_ID)
        local, ici, d2d = [], [], []
        for a in range(self.n):
            place = self.places[a]
            if self.kind == "gather":
                local.append(pltpu.make_async_copy(ins[a], _slot(outs[a], place, 2 * x + y), sems[4].at[a]))
                for r, (px, py) in enumerate(peers):
                    ici.append(remote(_slot(ins[a], place, None, c), _slot(outs[a], place, 2 * x + y, c),
                                      sems[0], sems[1], 3 * a + r, (px, py, c)))
                    landed = _slot(outs[a], place, 2 * px + py, c)
                    d2d.append(remote(landed, landed, sems[2], sems[3], 3 * a + r, (x, y, 1 - c)))
            else:
                local.append(pltpu.make_async_copy(_slot(ins[a], place, 2 * x + y), outs[a].at[3], sems[2].at[a]))
                for r, (px, py) in enumerate(peers):
                    ici.append(remote(_slot(ins[a], place, 2 * px + py), outs[a].at[r], sems[0], sems[1], 3 * a + r, (px, py, c)))
        return local, ici, d2d

    def start(self, ins, outs, sems):
        local, ici, _ = self._copies(ins, outs, sems)
        for cp in local + ici:
            cp.start()

    def relay(self, ins, outs, sems):
        _, ici, d2d = self._copies(ins, outs, sems)
        for arrived, onward in zip(ici, d2d):
            arrived.wait_recv()
            onward.start()

    def finish(self, ins, outs, sems):
        local, ici, d2d = self._copies(ins, outs, sems)
        for cp in local + d2d:
            cp.wait()
        for cp in ici:
            cp.wait_send() if d2d else cp.wait()


def _exchange_call(job, name):
    n = job.n

    def body(*refs):
        ins, outs, sems = refs[:n], refs[n:2 * n], refs[2 * n:]
        job.start(ins, outs, sems)
        job.relay(ins, outs, sems)
        job.finish(ins, outs, sems)

    return pl.pallas_call(
        body, name=name, in_specs=[ANY_SPEC] * n, out_specs=[ANY_SPEC] * n, out_shape=job.out_shape,
        scratch_shapes=job.scratch, compiler_params=pltpu.CompilerParams(has_side_effects=True),
    )(*job.arrays)


def _carried_call(body, *, name, grid, in_specs, out_specs, out_shape, semantics, operands, job=None):
    if job is None:
        return pl.pallas_call(body, name=name, grid=grid, in_specs=in_specs, out_specs=out_specs, out_shape=out_shape,
                              compiler_params=_params(*semantics))(*operands), []
    n_in, n_out, n = len(in_specs), len(out_specs), job.n
    steps = math.prod(grid)

    def wrapped(*refs):
        ins, job_ins = refs[:n_in], refs[n_in:n_in + n]
        outs, job_outs = refs[n_in + n:n_in + n + n_out], refs[n_in + n + n_out:n_in + 2 * n + n_out]
        sems = refs[n_in + 2 * n + n_out:]
        step = functools.reduce(lambda acc, d: acc * grid[d] + pl.program_id(d), range(len(grid)), 0)

        @pl.when(step == 0)
        def _():
            job.start(job_ins, job_outs, sems)

        @pl.when(step == (3 * steps) // 4)
        def _():
            job.relay(job_ins, job_outs, sems)

        body(*ins, *outs)

        @pl.when(step == steps - 1)
        def _():
            job.finish(job_ins, job_outs, sems)

    res = pl.pallas_call(
        wrapped, name=name, grid=grid, in_specs=list(in_specs) + [ANY_SPEC] * n, out_specs=list(out_specs) + [ANY_SPEC] * n,
        out_shape=list(out_shape) + job.out_shape, scratch_shapes=job.scratch,
        compiler_params=pltpu.CompilerParams(dimension_semantics=("arbitrary",) * len(grid), vmem_limit_bytes=VMEM_LIMIT,
                                             has_side_effects=True),
    )(*operands, *job.arrays)
    return res[:n_out], res[n_out:]


def _swap_with_sibling(parts):
    n = len(parts)

    def body(*refs):
        ins, outs = refs[:n], refs[n:2 * n]
        send, recv = refs[2 * n:]
        sibling = (lax.axis_index("x"), lax.axis_index("y"), 1 - lax.axis_index("c"))
        copies = [pltpu.make_async_remote_copy(src_ref=ins[a], dst_ref=outs[a], send_sem=send.at[a], recv_sem=recv.at[a],
                                               device_id=sibling, device_id_type=MESH_ID) for a in range(n)]
        for cp in copies:
            cp.start()
        for cp in copies:
            cp.wait()

    return pl.pallas_call(
        body, name="swap_with_sibling",
        in_specs=[ANY_SPEC] * n, out_specs=[ANY_SPEC] * n,
        out_shape=[jax.ShapeDtypeStruct(p.shape, p.dtype) for p in parts],
        scratch_shapes=[pltpu.SemaphoreType.DMA((n,)), pltpu.SemaphoreType.DMA((n,))],
        compiler_params=pltpu.CompilerParams(has_side_effects=True),
    )(*parts)


def _sum_all_devices(t):
    rows = t.shape[0]

    def body(t_ref, o_ref, slots, send, recv):
        x, y, c = lax.axis_index("x"), lax.axis_index("y"), lax.axis_index("c")
        me = 4 * x + 2 * y + c
        slots[me] = t_ref[...]
        copies = []
        for m in range(1, 8):
            peer = (x ^ (m >> 2), y ^ ((m >> 1) & 1), c ^ (m & 1))
            cp = pltpu.make_async_remote_copy(src_ref=t_ref, dst_ref=slots.at[me], send_sem=send.at[m - 1],
                                              recv_sem=recv.at[m - 1], device_id=peer, device_id_type=MESH_ID)
            cp.start()
            copies.append(cp)
        for cp in copies:
            cp.wait()
        acc = slots[0]
        for dev in range(1, 8):
            acc = acc + slots[dev]
        o_ref[...] = acc

    vmem = pl.BlockSpec(memory_space=pltpu.VMEM)
    return pl.pallas_call(
        body, name="sum_all_devices", in_specs=[vmem], out_specs=vmem,
        out_shape=jax.ShapeDtypeStruct(t.shape, F32),
        scratch_shapes=[pltpu.VMEM((8, rows, 128), F32), pltpu.SemaphoreType.DMA((7,)), pltpu.SemaphoreType.DMA((7,))],
        compiler_params=pltpu.CompilerParams(vmem_limit_bytes=VMEM_LIMIT, has_side_effects=True),
    )(t)


def _adamw(w, g, m, v):
    m = ADAM_B1 * m + (1.0 - ADAM_B1) * g
    v = ADAM_B2 * v + (1.0 - ADAM_B2) * jnp.square(g)
    m_hat = m / (1.0 - ADAM_B1 ** ADAM_STEP)
    v_hat = v / (1.0 - ADAM_B2 ** ADAM_STEP)
    return -ADAM_LR * (m_hat / (jnp.sqrt(v_hat) + ADAM_EPS) + ADAM_WD * w), m, v


def _chip_sum(received, name):
    rows, cols = received.shape[1:]
    tl = _row_block(rows, 512)

    def body(r_ref, o_ref):
        o_ref[...] = ((r_ref[0].astype(F32) + r_ref[1].astype(F32)) + r_ref[2].astype(F32)) + r_ref[3].astype(F32)

    return pl.pallas_call(body, name=name, grid=(rows // tl,),
                          in_specs=[pl.BlockSpec((N_CHIPS, tl, cols), lambda i: (0, i, 0))],
                          out_specs=pl.BlockSpec((tl, cols), lambda i: (i, 0)),
                          out_shape=jax.ShapeDtypeStruct((rows, cols), F32), compiler_params=_params("parallel"))(received)


def _adamw_pair(w, p_mine, p_other, m, v, name):
    def fn(wv, pa, pb, mv, vv):
        g = pa + pb
        return (g,) + _adamw(wv, g, mv, vv)

    return _rowmap(fn, [w, p_mine, p_other, m, v], "rrrrr", [(w.shape, F32, "r")] * 4, name=name, tl=256)


def _adamw_small(w, g, m, v):
    def fn(wv, gv, mv, vv):
        return _adamw(wv, gv, mv, vv)

    return _rowmap(fn, [w, g, m, v], "rrrr", [(w.shape, F32, "r")] * 3, name="adamw_small", tl=w.shape[0])


WEIGHTS = ['ffn1_norm', 'ffn1_w_gate', 'ffn1_w_up', 'ffn1_w_down', 'mix_norm', 'ffn2_norm', 'ffn2_w_gate', 'ffn2_w_up',
           'ffn2_w_down', 'ab_w_in', 's5_lambda_re', 's5_lambda_im', 's5_log_dt', 's5_b_re', 's5_b_im', 's5_c_re', 's5_c_im',
           's5_d', 's5_w_glu', 'ab_w_out', 'sc_w_in', 'sc_conv_w', 'sc_w_out', 'final_norm']
SHARDED = {'ffn1_w_gate': (1, FF_SLOT), 'ffn1_w_up': (1, FF_SLOT), 'ffn1_w_down': (0, FF_SLOT),
           'ffn2_w_gate': (1, FF_SLOT), 'ffn2_w_up': (1, FF_SLOT), 'ffn2_w_down': (0, FF_SLOT),
           'ab_w_in': (1, 512), 's5_w_glu': (0, 128), 'ab_w_out': (0, 256), 'sc_w_in': (1, 768), 'sc_conv_w': (1, 256),
           'sc_w_out': (0, 256)}
SMALL = [n for n in WEIGHTS if n not in SHARDED]


def _pack(arrays):
    rows = []
    for t in arrays:
        flat = t.reshape(-1)
        rows.append(jnp.pad(flat, (0, (-flat.shape[0]) % 128)))
    flat = jnp.concatenate(rows)
    return jnp.pad(flat, (0, (-flat.shape[0]) % 1024)).reshape(-1, 128)


def _unpack(packed, like):
    flat, out, pos = packed.reshape(-1), [], 0
    for t in like:
        out.append(flat[pos:pos + t.size].reshape(t.shape))
        pos += t.size + (-t.size) % 128
    return out


def _local_grads(x, target, p, full, late=None):
    small, grads, saved = {}, {}, []
    ffn = lambda which, layer: [full[(f"{which}_w_{n}", layer)] for n in ("gate", "up", "down")]
    for layer in range(2):
        x, s1 = _ffn_fwd(x, p["ffn1_norm"][layer:layer + 1], *ffn("ffn1", layer), tag=f"ffn1_{layer}")
        if layer == 0:
            x, sm = _mix0_fwd(x, p["mix_norm"][0:1], p, full, late)
        else:
            x, sm = _mix1_fwd(x, p["mix_norm"][1:2], full[("sc_w_in", 0)], full[("sc_conv_w", 0)], full[("sc_w_out", 0)])
        x, s2 = _ffn_fwd(x, p["ffn2_norm"][layer:layer + 1], *ffn("ffn2", layer), tag=f"ffn2_{layer}")
        saved.append((s1, sm, s2))
    loss, dx, dg_final = _loss_head(x, p["final_norm"][None], target)
    small["final_norm"] = dg_final[0]
    gains = {n: [None, None] for n in ("ffn1_norm", "mix_norm", "ffn2_norm")}

    def ffn_bwd(which, layer, dx, s):
        dx, (dg, dwg, dwu, dwd) = _ffn_bwd(dx, s, p[f"{which}_norm"][layer:layer + 1], *ffn(which, layer), tag=f"{which}_{layer}")
        gains[f"{which}_norm"][layer] = dg[0]
        grads.update({(f"{which}_w_gate", layer): dwg, (f"{which}_w_up", layer): dwu, (f"{which}_w_down", layer): dwd})
        return dx

    for layer in (1, 0):
        s1, sm, s2 = saved[layer]
        dx = ffn_bwd("ffn2", layer, dx, s2)
        if layer == 0:
            dx, dg, s5_grads = _mix0_bwd(dx, sm, p["mix_norm"][0:1], p, full, grads, late)
            small.update(s5_grads)
        else:
            dx, dg, dw_in, dcw, dw_out = _mix1_bwd(dx, sm, p["mix_norm"][1:2], full[("sc_w_in", 0)], full[("sc_conv_w", 0)],
                                                   full[("sc_w_out", 0)])
            grads.update({("sc_w_in", 0): dw_in, ("sc_conv_w", 0): dcw.astype(WIRE_DTYPE), ("sc_w_out", 0): dw_out})
        gains["mix_norm"][layer] = dg[0]
        dx = ffn_bwd("ffn1", layer, dx, s1)
    small.update({n: jnp.stack(pair) for n, pair in gains.items()})
    return loss, dx, small, grads


_EARLY = [("ffn1_w_gate", 0), ("ffn1_w_up", 0), ("ffn1_w_down", 0), ("ab_w_in", 0)]


class _Late:
    def __init__(self, shards, places):
        self.shards, self.places = shards, places
        self.gather_keys = [k for k in shards if k not in _EARLY]
        self.scatter_keys, self.received = [], {}

    def gather_job(self):
        return _Exchange("gather", [self.shards[k] for k in self.gather_keys], [self.places[k] for k in self.gather_keys])

    def scatter_job(self, grads):
        self.scatter_keys = list(grads)
        return _Exchange("scatter", [grads[k] for k in self.scatter_keys], [self.places[k] for k in self.scatter_keys])


def kernel(x, ffn1_norm, ffn1_w_gate, ffn1_w_up, ffn1_w_down, mix_norm, ffn2_norm, ffn2_w_gate, ffn2_w_up, ffn2_w_down, ab_w_in, s5_lambda_re, s5_lambda_im, s5_log_dt, s5_b_re, s5_b_im, s5_c_re, s5_c_im, s5_d, s5_w_glu, ab_w_out, sc_w_in, sc_conv_w, sc_w_out, final_norm, loss_target, m_ffn1_norm, m_ffn1_w_gate, m_ffn1_w_up, m_ffn1_w_down, m_mix_norm, m_ffn2_norm, m_ffn2_w_gate, m_ffn2_w_up, m_ffn2_w_down, m_ab_w_in, m_s5_lambda_re, m_s5_lambda_im, m_s5_log_dt, m_s5_b_re, m_s5_b_im, m_s5_c_re, m_s5_c_im, m_s5_d, m_s5_w_glu, m_ab_w_out, m_sc_w_in, m_sc_conv_w, m_sc_w_out, m_final_norm, v_ffn1_norm, v_ffn1_w_gate, v_ffn1_w_up, v_ffn1_w_down, v_mix_norm, v_ffn2_norm, v_ffn2_w_gate, v_ffn2_w_up, v_ffn2_w_down, v_ab_w_in, v_s5_lambda_re, v_s5_lambda_im, v_s5_log_dt, v_s5_b_re, v_s5_b_im, v_s5_c_re, v_s5_c_im, v_s5_d, v_s5_w_glu, v_ab_w_out, v_sc_w_in, v_sc_conv_w, v_sc_w_out, v_final_norm):
    args = dict(locals())
    p = {n: args[n] for n in WEIGHTS}
    mom = {n: args["m_" + n] for n in WEIGHTS}
    var = {n: args["v_" + n] for n in WEIGHTS}

    keys = [(n, layer) for n in SHARDED for layer in range(p[n].shape[0])]
    shards, places = {}, {}
    for n, layer in keys:
        axis, width = SHARDED[n]
        t = p[n][layer] if n == "sc_conv_w" else p[n][layer].astype(MXU_DTYPE)
        pad = [(0, 0), (0, 0)]
        pad[axis] = (0, width - t.shape[axis])
        shards[(n, layer)], places[(n, layer)] = jnp.pad(t, pad), (axis, width)
    job = _Exchange("gather", [shards[k] for k in _EARLY], [places[k] for k in _EARLY])
    full = dict(zip(_EARLY, _exchange_call(job, "gather_early")))
    late = _Late(shards, places)

    loss, dx, small, grads = _local_grads(x[0], loss_target[0], p, full, late)
    loss = lax.psum(loss[0, 0], ("x", "y", "c"))

    rest = [k for k in keys if k not in late.received]
    job = _Exchange("scatter", [grads[k] for k in rest], [places[k] for k in rest])
    late.received.update(zip(rest, _exchange_call(job, "scatter_last")))
    partial = [_chip_sum(late.received[(n, layer)], name=f"chip_sum_{n}_{layer}") for n, layer in keys]
    other = _swap_with_sibling(partial)
    out = {}
    stacked = {n: [] for n in SHARDED}
    for (n, layer), mine, theirs in zip(keys, partial, other):
        rows, cols = p[n].shape[1:]
        stacked[n].append(_adamw_pair(p[n][layer], mine[:rows, :cols], theirs[:rows, :cols], mom[n][layer], var[n][layer],
                                      name=f"adamw_{n}_{layer}"))
    for n in SHARDED:
        out[n] = [jnp.stack([res[k] for res in stacked[n]]) for k in range(4)]

    like = [p[n] for n in SMALL]
    g_small = _sum_all_devices(_pack([small[n] for n in SMALL]))
    d_small, m_small, v_small = _adamw_small(_pack(like), g_small, _pack([mom[n] for n in SMALL]), _pack([var[n] for n in SMALL]))
    for k, packed in enumerate((g_small, d_small, m_small, v_small)):
        for n, t in zip(SMALL, _unpack(packed, like)):
            out.setdefault(n, [None] * 4)[k] = t

    return (loss, dx[None], *[out[n][0] for n in WEIGHTS], *[out[n][1] for n in WEIGHTS],
            *[out[n][2] for n in WEIGHTS], *[out[n][3] for n in WEIGHTS])
```

```python
import functools
import math

import jax
import jax.numpy as jnp
from jax import lax
from jax.experimental import pallas as pl
from jax.experimental.pallas import tpu as pltpu

F32 = jnp.float32
MXU_DTYPE = jnp.bfloat16
WIRE_DTYPE = jnp.bfloat16
MESH_ID = pl.DeviceIdType.MESH

D_MODEL = 1024
D_FF = 2752
N_CHIPS = 4
FF_SHARD = D_FF // N_CHIPS
FF_SLOT = 768
FF_PAD = N_CHIPS * FF_SLOT
S5_WIDTH = 512
S5_GROUP = 16
S5_GROUPS = 32
S5_STATE = 64
S5_LANES = S5_GROUPS * S5_STATE
S5_BLOCK = 512
SB_HEADS = 8
SB_DH = 64
SB_SCALE = 0.125
SB_PACK = 2
SB_QUERIES = 1024
SB_KEYS = 256
EPS = 1e-6
ADAM_LR, ADAM_B1, ADAM_B2, ADAM_EPS, ADAM_WD, ADAM_STEP = 0.001, 0.9, 0.999, 1e-08, 0.01, 10
VMEM_LIMIT = 56 * 1024 * 1024

ANY_SPEC = pl.BlockSpec(memory_space=pl.ANY)


def _params(*sem):
    return pltpu.CompilerParams(dimension_semantics=sem or None, vmem_limit_bytes=VMEM_LIMIT)


def _mm(a, b, *, name, ta=False, tb=False, out_dtype=F32, epilogue=None, extras=(), tm=1024, tn=1024, tk=1024):
    m, k = (a.shape[1], a.shape[0]) if ta else a.shape
    n = b.shape[0] if tb else b.shape[1]
    tm, tn, tk = min(tm, m), min(tn, n), min(tk, k)
    assert m % tm == 0 and n % tn == 0 and k % tk == 0, (name, m, n, k)
    nk = k // tk
    a_spec = pl.BlockSpec((tk, tm), lambda i, j, kk: (kk, i)) if ta else pl.BlockSpec((tm, tk), lambda i, j, kk: (i, kk))
    b_spec = pl.BlockSpec((tn, tk), lambda i, j, kk: (j, kk)) if tb else pl.BlockSpec((tk, tn), lambda i, j, kk: (kk, j))
    ex_specs = []
    for e in extras:
        if e.shape == (m, n):
            ex_specs.append(pl.BlockSpec((tm, tn), lambda i, j, kk: (i, j)))
        elif e.shape == (1, n):
            ex_specs.append(pl.BlockSpec((1, tn), lambda i, j, kk: (0, j)))
        else:
            assert e.shape == (m, 1), (name, e.shape)
            ex_specs.append(pl.BlockSpec((tm, 1), lambda i, j, kk: (i, 0)))
    dims = (((0 if ta else 1,), (1 if tb else 0,)), ((), ()))
    n_ex = len(extras)

    out_dtypes = list(out_dtype) if isinstance(out_dtype, (list, tuple)) else [out_dtype]
    n_out = len(out_dtypes)

    def body(a_ref, b_ref, *rest):
        ex_refs, o_refs = rest[:n_ex], rest[n_ex:n_ex + n_out]
        part = lax.dot_general(a_ref[...].astype(MXU_DTYPE), b_ref[...].astype(MXU_DTYPE), dims, preferred_element_type=F32)

        def finish(r):
            if epilogue is not None:
                r = epilogue(r, *[e[...] for e in ex_refs])
            for o_ref, val in zip(o_refs, r if isinstance(r, (tuple, list)) else (r,)):
                o_ref[...] = val.astype(o_ref.dtype)

        if nk == 1:
            finish(part)
            return
        acc_ref, kk = rest[n_ex + n_out], pl.program_id(2)

        @pl.when(kk == 0)
        def _():
            acc_ref[...] = part

        @pl.when(jnp.logical_and(kk > 0, kk < nk - 1))
        def _():
            acc_ref[...] += part

        @pl.when(kk == nk - 1)
        def _():
            finish(acc_ref[...] + part)

    res = pl.pallas_call(
        body, name=name, grid=(m // tm, n // tn, nk),
        in_specs=[a_spec, b_spec, *ex_specs],
        out_specs=[pl.BlockSpec((tm, tn), lambda i, j, kk: (i, j))] * n_out,
        out_shape=[jax.ShapeDtypeStruct((m, n), dt) for dt in out_dtypes],
        scratch_shapes=[pltpu.VMEM((tm, tn), F32)] if nk > 1 else [],
        compiler_params=_params("parallel", "parallel", "arbitrary"),
    )(a, b, *extras)
    return res if isinstance(out_dtype, (list, tuple)) else res[0]


def _row_block(rows, want):
    for tl in range(min(want, rows), 7, -1):
        if rows % tl == 0 and tl % 8 == 0:
            return tl
    return rows


def _rowmap(fn, ins, in_kinds, outs, *, name, tl):
    rows = next(x.shape[0] for x, kd in zip(ins, in_kinds) if kd == "r")
    tl = _row_block(rows, tl)
    n_in = len(ins)

    def spec(shape, kind):
        if kind == "r":
            return pl.BlockSpec((tl,) + tuple(shape[1:]), lambda i: (i,) + (0,) * (len(shape) - 1))
        return pl.BlockSpec(tuple(shape), lambda i: (0,) * len(shape))

    def body(*refs):
        in_refs, out_refs = refs[:n_in], refs[n_in:]
        res = fn(*[r[...] for r in in_refs])
        if not isinstance(res, (tuple, list)):
            res = (res,)
        for o_ref, val, (_, dt, kind) in zip(out_refs, res, outs):
            if kind == "r":
                o_ref[...] = val.astype(dt)
            else:
                @pl.when(pl.program_id(0) == 0)
                def _():
                    o_ref[...] = jnp.zeros_like(o_ref)

                o_ref[...] += val.astype(dt)

    has_acc = any(kd == "a" for _, _, kd in outs)
    res = pl.pallas_call(
        body, name=name, grid=(rows // tl,),
        in_specs=[spec(x.shape, kd) for x, kd in zip(ins, in_kinds)],
        out_specs=[spec(s, kd) for s, _, kd in outs],
        out_shape=[jax.ShapeDtypeStruct(s, dt) for s, dt, _ in outs],
        compiler_params=_params("arbitrary" if has_acc else "parallel"),
    )(*ins)
    return res[0] if len(outs) == 1 else res


def _rms_fwd(x, g):
    r = lax.rsqrt(jnp.mean(x * x, axis=-1, keepdims=True) + EPS)
    return x * r * g


def _rms_bwd(dh, x, g):
    r = lax.rsqrt(jnp.mean(x * x, axis=-1, keepdims=True) + EPS)
    xh = x * r
    dxh = dh * g
    dx = r * (dxh - xh * jnp.mean(dxh * xh, axis=-1, keepdims=True))
    return dx, jnp.sum(dh * xh, axis=0, keepdims=True)


def _norm(x, g, *, name):
    return _rowmap(lambda xv, gv: _rms_fwd(xv, gv), [x, g], "rc", [(x.shape, MXU_DTYPE, "r")], name=name, tl=256)


def _norm_bwd(dh, x, g, dres, *, name):
    def fn(dhv, xv, gv, drv):
        dx, dg = _rms_bwd(dhv, xv, gv)
        return dx + drv, dg
    return _rowmap(fn, [dh, x, g, dres], "rrcr", [(x.shape, F32, "r"), (g.shape, F32, "a")], name=name, tl=256)


def _swiglu_act(a, b):
    return jax.nn.silu(a) * b


def _ffn_up(x, g, wg, wu, *, name, tm=512, tn=1024):
    m, d = x.shape
    n = wg.shape[1]
    tm, tn = min(tm, m), min(tn, n)
    assert m % tm == 0 and n % tn == 0, (name, m, n)

    def body(x_ref, g_ref, wg_ref, wu_ref, h_ref, a_ref, b_ref, s_ref):
        @pl.when(pl.program_id(1) == 0)
        def _():
            h_ref[...] = _rms_fwd(x_ref[...], g_ref[...]).astype(h_ref.dtype)

        hv = h_ref[...]
        av = jnp.dot(hv, wg_ref[...], preferred_element_type=F32)
        bv = jnp.dot(hv, wu_ref[...], preferred_element_type=F32)
        a_ref[...] = av
        b_ref[...] = bv
        s_ref[...] = _swiglu_act(av, bv).astype(s_ref.dtype)

    rows = pl.BlockSpec((tm, d), lambda i, j: (i, 0))
    cols = pl.BlockSpec((d, tn), lambda i, j: (0, j))
    tile = pl.BlockSpec((tm, tn), lambda i, j: (i, j))
    return pl.pallas_call(
        body, name=name, grid=(m // tm, n // tn),
        in_specs=[rows, pl.BlockSpec((1, d), lambda i, j: (0, 0)), cols, cols],
        out_specs=[rows, tile, tile, tile],
        out_shape=[jax.ShapeDtypeStruct((m, d), MXU_DTYPE), jax.ShapeDtypeStruct((m, n), F32),
                   jax.ShapeDtypeStruct((m, n), F32), jax.ShapeDtypeStruct((m, n), MXU_DTYPE)],
        compiler_params=_params("parallel", "arbitrary"),
    )(x, g, wg, wu)


def _ffn_dx(da, db, wg, wu, x, g, dres, *, name, tm=512, tk=1024):
    m, f = da.shape
    d = wg.shape[0]
    tm, tk = min(tm, m), min(tk, f)
    assert m % tm == 0 and f % tk == 0, (name, m, f)
    nk = f // tk

    def body(da_ref, db_ref, wg_ref, wu_ref, x_ref, g_ref, dr_ref, dx_ref, dg_ref, acc_ref):
        i, kk = pl.program_id(0), pl.program_id(1)
        part = (lax.dot_general(da_ref[...], wg_ref[...], NT_DIMS, preferred_element_type=F32)
                + lax.dot_general(db_ref[...], wu_ref[...], NT_DIMS, preferred_element_type=F32))

        @pl.when(kk == 0)
        def _():
            acc_ref[...] = part

        @pl.when(kk > 0)
        def _():
            acc_ref[...] += part

        @pl.when(jnp.logical_and(i == 0, kk == 0))
        def _():
            dg_ref[...] = jnp.zeros_like(dg_ref)

        @pl.when(kk == nk - 1)
        def _():
            dx, dg = _rms_bwd(acc_ref[...], x_ref[...], g_ref[...])
            dx_ref[...] = dx + dr_ref[...]
            dg_ref[...] += dg

    act = pl.BlockSpec((tm, tk), lambda i, kk: (i, kk))
    wgt = pl.BlockSpec((d, tk), lambda i, kk: (0, kk))
    rows = pl.BlockSpec((tm, d), lambda i, kk: (i, 0))
    one = pl.BlockSpec((1, d), lambda i, kk: (0, 0))
    return pl.pallas_call(
        body, name=name, grid=(m // tm, nk),
        in_specs=[act, act, wgt, wgt, rows, one, rows],
        out_specs=[rows, one],
        out_shape=[jax.ShapeDtypeStruct((m, d), F32), jax.ShapeDtypeStruct((1, d), F32)],
        scratch_shapes=[pltpu.VMEM((tm, d), F32)],
        compiler_params=_params("arbitrary", "arbitrary"),
    )(da, db, wg, wu, x, g, dres)


def _ffn_fwd(x, g, wg, wu, wd, tag):
    h, a, b, s = _ffn_up(x, g, wg, wu, name=f"{tag}_up")
    x2 = _mm(s, wd, name=f"{tag}_down", epilogue=lambda acc, xv: xv + 0.5 * acc, extras=[x])
    return x2, (x, h, a, b, s)


def _ffn_bwd(dx2, saved, g, wg, wu, wd, tag):
    x, h, a, b, s = saved

    def act_bwd(ds, av, bv):
        _, vjp = jax.vjp(_swiglu_act, av, bv)
        return vjp(0.5 * ds)

    da, db = _mm(dx2, wd, tb=True, name=f"{tag}_dact", epilogue=act_bwd, extras=[a, b], out_dtype=[MXU_DTYPE, MXU_DTYPE])
    dwd = _mm(s, dx2, ta=True, name=f"{tag}_dwd", out_dtype=WIRE_DTYPE, epilogue=lambda acc: 0.5 * acc)
    dwg = _mm(h, da, ta=True, name=f"{tag}_dwg", out_dtype=WIRE_DTYPE)
    dwu = _mm(h, db, ta=True, name=f"{tag}_dwu", out_dtype=WIRE_DTYPE)
    dx, dg = _ffn_dx(da, db, wg, wu, x, g, dx2, name=f"{tag}_dx")
    return dx, (dg, dwg, dwu, dwd)


def _softplus(z):
    return jnp.maximum(z, 0.0) + jnp.log(1.0 + jnp.exp(-jnp.abs(z)))


def _ones_dot(x, tri):
    if MXU_DTYPE == F32:
        return jnp.dot(x, tri, preferred_element_type=F32)
    hi = x.astype(MXU_DTYPE)
    lo = (x - hi.astype(F32)).astype(MXU_DTYPE)
    return jnp.dot(hi, tri, preferred_element_type=F32) + jnp.dot(lo, tri, preferred_element_type=F32)


NT_DIMS = (((1,), (1,)), ((), ()))
TN_DIMS = (((0,), (0,)), ((), ()))


def _sb_fwd(q, k, v, *, tq=SB_QUERIES, job=None):
    nh, seq, dh = q.shape
    tq = min(tq, seq)
    hp = SB_PACK

    tk = min(SB_KEYS, tq)
    per = tq // tk

    def body(q_ref, k_ref, v_ref, o_ref, ls_ref):
        i = pl.program_id(1)
        r_idx = lax.broadcasted_iota(jnp.int32, (tk, tk), 0)
        c_idx = lax.broadcasted_iota(jnp.int32, (tk, tk), 1)
        after = (r_idx > c_idx).astype(MXU_DTYPE)
        q_pos = lax.broadcasted_iota(jnp.int32, (tq, tk), 0)
        k_pos = lax.broadcasted_iota(jnp.int32, (tq, tk), 1)

        def block(hd, j, c, acc, straddles):
            off = pl.multiple_of(j * tk, tk)
            kv = k_ref[hd, pl.ds(off, tk), :]
            vv = v_ref[hd, pl.ds(off, tk), :]
            z = lax.dot_general(q_ref[hd], kv, NT_DIMS, preferred_element_type=F32)
            sp = _softplus(z)
            if straddles is None:
                lk = -sp
                w = jnp.exp(z - sp + _ones_dot(lk, after) + c)
            else:
                before = k_pos + straddles * tk < q_pos
                lk = jnp.where(before, -sp, 0.0)
                w = jnp.where(before, jnp.exp(z - sp + _ones_dot(lk, after) + c), 0.0)
            acc = acc + jnp.dot(w.astype(MXU_DTYPE), vv, preferred_element_type=F32)
            return c + jnp.sum(lk, axis=1, keepdims=True), acc

        def step(n, carry):
            return tuple(block(hd, i * per - 1 - n, *carry[hd], None) for hd in range(hp))

        carry = tuple((jnp.zeros((tq, 1), F32), jnp.zeros((tq, dh), F32)) for _ in range(hp))
        for s in reversed(range(per)):
            carry = tuple(block(hd, i * per + s, *carry[hd], s) for hd in range(hp))
        for hd, (c, acc) in enumerate(lax.fori_loop(0, i * per, step, carry)):
            o_ref[hd] = acc
            ls_ref[hd] = c

    whole = pl.BlockSpec((hp, seq, dh), lambda h, i: (h, 0, 0))
    return _carried_call(
        body, name="sb_fwd", grid=(nh // hp, seq // tq),
        in_specs=[pl.BlockSpec((hp, tq, dh), lambda h, i: (h, i, 0)), whole, whole],
        out_specs=[pl.BlockSpec((hp, tq, dh), lambda h, i: (h, i, 0)), pl.BlockSpec((hp, tq, 1), lambda h, i: (h, i, 0))],
        out_shape=[jax.ShapeDtypeStruct((nh, seq, dh), F32), jax.ShapeDtypeStruct((nh, seq, 1), F32)],
        semantics=("parallel", "parallel"), operands=(q, k, v), job=job)


def _sb_bwd(q, k, v, lsum, do, *, tq=SB_QUERIES, job=None):
    nh, seq, dh = q.shape
    tq = min(tq, seq)
    tk = min(SB_KEYS, tq)
    per = tq // tk
    hp = SB_PACK
    scale = SB_SCALE

    def body(q_ref, k_ref, v_ref, ls_ref, do_ref, dq_ref, dk_ref, dv_ref):
        i = pl.program_id(1)

        @pl.when(i == 0)
        def _():
            dk_ref[...] = jnp.zeros_like(dk_ref)
            dv_ref[...] = jnp.zeros_like(dv_ref)

        r_idx = lax.broadcasted_iota(jnp.int32, (tk, tk), 0)
        c_idx = lax.broadcasted_iota(jnp.int32, (tk, tk), 1)
        upto = (r_idx <= c_idx).astype(MXU_DTYPE)
        before = (r_idx < c_idx).astype(MXU_DTYPE)
        q_pos = lax.broadcasted_iota(jnp.int32, (tq, tk), 0)
        k_pos = lax.broadcasted_iota(jnp.int32, (tq, tk), 1)

        def block(hd, j, cp, ce, dq, straddles):
            off = pl.multiple_of(j * tk, tk)
            qv, dov = q_ref[hd], do_ref[hd].astype(MXU_DTYPE)
            kv = k_ref[hd, pl.ds(off, tk), :]
            vv = v_ref[hd, pl.ds(off, tk), :]
            z = lax.dot_general(qv, kv, NT_DIMS, preferred_element_type=F32)
            sp = _softplus(z)
            valid = None if straddles is None else k_pos + straddles * tk < q_pos
            lk = -sp if valid is None else jnp.where(valid, -sp, 0.0)
            w = jnp.exp(z - sp + (ls_ref[hd] - cp) - _ones_dot(lk, upto))
            if valid is not None:
                w = jnp.where(valid, w, 0.0)
            e = w * lax.dot_general(dov, vv, NT_DIMS, preferred_element_type=F32)
            earlier = _ones_dot(e, before) + ce
            keep = jnp.exp(-sp)
            dz = e * keep - (1.0 - keep) * earlier
            if valid is not None:
                dz = jnp.where(valid, dz, 0.0)
            dzm = dz.astype(MXU_DTYPE)
            dq = dq + jnp.dot(dzm, kv, preferred_element_type=F32)
            dk_ref[hd, pl.ds(off, tk), :] += lax.dot_general(dzm, qv, TN_DIMS, preferred_element_type=F32)
            dv_ref[hd, pl.ds(off, tk), :] += lax.dot_general(w.astype(MXU_DTYPE), dov, TN_DIMS, preferred_element_type=F32)
            return cp + jnp.sum(lk, axis=1, keepdims=True), ce + jnp.sum(e, axis=1, keepdims=True), dq

        def step(j, carry):
            return tuple(block(hd, j, *carry[hd], None) for hd in range(hp))

        zero = jnp.zeros((tq, 1), F32)
        carry = lax.fori_loop(0, i * per, step, tuple((zero, zero, jnp.zeros((tq, dh), F32)) for _ in range(hp)))
        for s in range(per):
            carry = tuple(block(hd, i * per + s, *carry[hd], s) for hd in range(hp))
        for hd in range(hp):
            dq_ref[hd] = carry[hd][2] * scale

    whole = pl.BlockSpec((hp, seq, dh), lambda h, i: (h, 0, 0))
    tile = pl.BlockSpec((hp, tq, dh), lambda h, i: (h, i, 0))
    return _carried_call(
        body, name="sb_bwd", grid=(nh // hp, seq // tq),
        in_specs=[tile, whole, whole, pl.BlockSpec((hp, tq, 1), lambda h, i: (h, i, 0)), tile],
        out_specs=[tile, whole, whole],
        out_shape=[jax.ShapeDtypeStruct((nh, seq, dh), F32)] * 3,
        semantics=("parallel", "arbitrary"), operands=(q, k, v, lsum, do), job=job)


def _heads(t):
    return t.reshape(t.shape[0], SB_HEADS, SB_DH).transpose(1, 0, 2)


def _unheads(t):
    return t.transpose(1, 0, 2).reshape(t.shape[1], SB_HEADS * SB_DH)


def _s5_disc(lr, li, ldt, br, bi):
    dt = jnp.exp(ldt)
    mag = jnp.exp(lr * dt)
    ar = mag * jnp.cos(li * dt)
    ai = mag * jnp.sin(li * dt)
    den = lr * lr + li * li
    nr = ar - 1.0
    cr = (nr * lr + ai * li) / den
    ci = (ai * lr - nr * li) / den
    return ar, ai, cr[None] * br - ci[None] * bi, cr[None] * bi + ci[None] * br


def _s5_prep(lr, li, ldt, br, bi):
    shapes = [lr.shape, lr.shape, br.shape, br.shape]

    def body(lr_ref, li_ref, ldt_ref, br_ref, bi_ref, *outs):
        for o, val in zip(outs, _s5_disc(lr_ref[...], li_ref[...], ldt_ref[...], br_ref[...], bi_ref[...])):
            o[...] = val

    return pl.pallas_call(body, name="s5_prep", out_shape=[jax.ShapeDtypeStruct(s, F32) for s in shapes])(lr, li, ldt, br, bi)


def _s5_prep_bwd(lr, li, ldt, br, bi, cts):
    args = (lr, li, ldt, br, bi)

    def body(*refs):
        ins, ct_refs, outs = refs[:5], refs[5:9], refs[9:]
        _, vjp = jax.vjp(_s5_disc, *[r[...] for r in ins])
        for o, val in zip(outs, vjp(tuple(r[...] for r in ct_refs))):
            o[...] = val

    return pl.pallas_call(body, name="s5_prep_bwd", out_shape=[jax.ShapeDtypeStruct(a.shape, F32) for a in args])(*args, *cts)


def _s5_scan(bu, a, *, tc=512):
    seq, w2 = bu.shape
    tw = S5_BLOCK
    tc = min(tc, seq)
    assert seq % tc == 0 and w2 % (2 * tw) == 0

    def body(bu_ref, a_ref, h_ref, cr_ref, ci_ref):
        @pl.when(pl.program_id(1) == 0)
        def _():
            cr_ref[...] = jnp.zeros_like(cr_ref)
            ci_ref[...] = jnp.zeros_like(ci_ref)

        re, im = pl.ds(0, tw), pl.ds(tw, tw)
        ar, ai = a_ref[:, re], a_ref[:, im]

        def step(t, carry):
            hr, hi = carry
            row = pl.ds(t, 1)
            nr = ar * hr - ai * hi + bu_ref[row, re]
            ni = ar * hi + ai * hr + bu_ref[row, im]
            h_ref[row, re] = nr
            h_ref[row, im] = ni
            return nr, ni

        hr, hi = lax.fori_loop(0, tc, step, (cr_ref[...], ci_ref[...]), unroll=8)
        cr_ref[...] = hr
        ci_ref[...] = hi

    blk = pl.BlockSpec((tc, 2 * tw), lambda j, t: (t, j))
    return pl.pallas_call(
        body, name="s5_scan", grid=(w2 // (2 * tw), seq // tc),
        in_specs=[blk, pl.BlockSpec((1, 2 * tw), lambda j, t: (0, j))],
        out_specs=blk,
        out_shape=jax.ShapeDtypeStruct((seq, w2), F32),
        scratch_shapes=[pltpu.VMEM((1, tw), F32)] * 2,
        compiler_params=_params("parallel", "arbitrary"),
    )(bu, a)


def _s5_scan_bwd(d, h, a, *, tc=512):
    seq, w2 = d.shape
    tw = S5_BLOCK
    tc = min(tc, seq)
    assert seq % tc == 0 and w2 % (2 * tw) == 0
    nt = seq // tc

    def body(d_ref, h_ref, a_ref, g_ref, da_ref, cr_ref, ci_ref):
        @pl.when(pl.program_id(1) == 0)
        def _():
            cr_ref[...] = jnp.zeros_like(cr_ref)
            ci_ref[...] = jnp.zeros_like(ci_ref)
            da_ref[...] = jnp.zeros_like(da_ref)

        re, im = pl.ds(0, tw), pl.ds(tw, tw)
        ar, ai = a_ref[:, re], a_ref[:, im]

        def step(n, carry):
            gr, gi, sr, si = carry
            row = pl.ds(tc - 1 - n, 1)
            hrt, hit = h_ref[row, re], h_ref[row, im]
            sr = sr + gr * hrt + gi * hit
            si = si + gi * hrt - gr * hit
            ngr = d_ref[row, re] + ar * gr + ai * gi
            ngi = d_ref[row, im] + ar * gi - ai * gr
            g_ref[row, re] = ngr
            g_ref[row, im] = ngi
            return ngr, ngi, sr, si

        gr, gi, sr, si = lax.fori_loop(0, tc, step, (cr_ref[...], ci_ref[...], da_ref[:, re], da_ref[:, im]), unroll=8)
        cr_ref[...] = gr
        ci_ref[...] = gi
        da_ref[:, re] = sr
        da_ref[:, im] = si

    blk = pl.BlockSpec((tc, 2 * tw), lambda j, t: (nt - 1 - t, j))
    row = pl.BlockSpec((1, 2 * tw), lambda j, t: (0, j))
    return pl.pallas_call(
        body, name="s5_scan_bwd", grid=(w2 // (2 * tw), nt),
        in_specs=[blk, blk, row],
        out_specs=[blk, row],
        out_shape=[jax.ShapeDtypeStruct((seq, w2), F32), jax.ShapeDtypeStruct((1, w2), F32)],
        scratch_shapes=[pltpu.VMEM((1, tw), F32)] * 2,
        compiler_params=_params("parallel", "arbitrary"),
    )(d, h, a)


def _pair_columns(re, im, axis):
    shape = re.shape
    split = shape[:axis] + (shape[axis] // S5_BLOCK, S5_BLOCK) + shape[axis + 1:]
    both = jnp.stack([re.reshape(split), im.reshape(split)], axis=axis + 1)
    return both.reshape(shape[:axis] + (2 * shape[axis],) + shape[axis + 1:])


def _unpair_columns(t, axis):
    shape = t.shape
    both = t.reshape(shape[:axis] + (shape[axis] // (2 * S5_BLOCK), 2, S5_BLOCK) + shape[axis + 1:])
    half = shape[:axis] + (shape[axis] // 2,) + shape[axis + 1:]
    return (lax.index_in_dim(both, 0, axis + 1, keepdims=False).reshape(half),
            lax.index_in_dim(both, 1, axis + 1, keepdims=False).reshape(half))


def _block_diag(t):
    g, a, b = t.shape
    eye = jnp.eye(g, dtype=t.dtype)
    return (t[:, :, None, :] * eye[:, None, :, None]).reshape(g * a, g * b)


def _block_diag_part(m, g):
    a, b = m.shape[0] // g, m.shape[1] // g
    return jnp.moveaxis(jnp.diagonal(m.reshape(g, a, g, b), axis1=0, axis2=2), -1, 0)


def _gelu_glu(y, gate_pre):
    z = jax.nn.gelu(y)
    return z * jax.nn.sigmoid(gate_pre)


def _s5_fwd(u, p, w_glu):
    lr, li = p["s5_lambda_re"][0], p["s5_lambda_im"][0]
    ldt = p["s5_log_dt"][0][:, None]
    br = p["s5_b_re"][0].transpose(2, 0, 1)
    bi = p["s5_b_im"][0].transpose(2, 0, 1)
    ar, ai, bbr, bbi = _s5_prep(lr, li, ldt, br, bi)
    a = _pair_columns(ar.reshape(1, S5_LANES), ai.reshape(1, S5_LANES), 1)
    bmat = _pair_columns(_block_diag(bbr.transpose(1, 0, 2)), _block_diag(bbi.transpose(1, 0, 2)), 1)
    cmat = _pair_columns(_block_diag(p["s5_c_re"][0].transpose(0, 2, 1)),
                         -_block_diag(p["s5_c_im"][0].transpose(0, 2, 1)), 0)
    bmat, cmat = bmat.astype(MXU_DTYPE), cmat.astype(MXU_DTYPE)
    bu = _mm(u, bmat, name="s5_bu")
    h = _s5_scan(bu, a)
    d = p["s5_d"]
    y = _mm(h, cmat, name="s5_y", epilogue=lambda acc, uv, dv: acc + dv * uv, extras=[u, d])
    z = _rowmap(jax.nn.gelu, [y], "r", [(y.shape, MXU_DTYPE, "r")], name="s5_gelu", tl=512)
    gate_pre = _mm(z, w_glu, name="s5_glu")
    out = _rowmap(_gelu_glu, [y, gate_pre], "rr", [(y.shape, F32, "r")], name="s5_gate", tl=512)
    return out, (u, lr, li, ldt, br, bi, a, bmat, cmat, h, y, z, gate_pre)


def _s5_bwd(dout, saved, p, w_glu):
    u, lr, li, ldt, br, bi, a, bmat, cmat, h, y, z, gate_pre = saved
    d = p["s5_d"]

    def gate_bwd(dov, yv, gv):
        zv = jax.nn.gelu(yv)
        sg = jax.nn.sigmoid(gv)
        return dov * sg, dov * zv * sg * (1.0 - sg)

    dz_direct, dgate = _rowmap(gate_bwd, [dout, y, gate_pre], "rrr", [(y.shape, F32, "r"), (y.shape, MXU_DTYPE, "r")],
                               name="s5_gate_bwd", tl=512)
    dw_glu = _mm(z, dgate, ta=True, name="s5_dwglu", out_dtype=WIRE_DTYPE)
    dz = _mm(dgate, w_glu, tb=True, name="s5_dz", epilogue=lambda acc, prev: acc + prev, extras=[dz_direct])

    def gelu_bwd(dzv, yv, uv, dvv):
        _, vjp = jax.vjp(jax.nn.gelu, yv)
        dy = vjp(dzv)[0]
        return dy, dy * dvv, jnp.sum(dy * uv, axis=0, keepdims=True)

    dy, du_skip, dd = _rowmap(gelu_bwd, [dz, y, u, d], "rrrc",
                              [(y.shape, F32, "r"), (y.shape, F32, "r"), (d.shape, F32, "a")], name="s5_gelu_bwd", tl=512)
    dcmat = _mm(h, dy, ta=True, name="s5_dc")
    dstate = _mm(dy, cmat, tb=True, name="s5_dstate")
    g, da = _s5_scan_bwd(dstate, h, a)
    du = _mm(g, bmat, tb=True, name="s5_du", epilogue=lambda acc, prev: acc + prev, extras=[du_skip], out_dtype=MXU_DTYPE)
    dbmat = _mm(u, g, ta=True, name="s5_db")
    dbbr, dbbi = (_block_diag_part(t, S5_GROUPS).transpose(1, 0, 2) for t in _unpair_columns(dbmat, 1))
    dar, dai = _unpair_columns(da, 1)
    cts = (dar.reshape(S5_GROUPS, S5_STATE), dai.reshape(S5_GROUPS, S5_STATE), dbbr, dbbi)
    dlr, dli, dldt, dbr, dbi = _s5_prep_bwd(lr, li, ldt, br, bi, cts)
    dcr, dci = (_block_diag_part(t, S5_GROUPS).transpose(0, 2, 1) for t in _unpair_columns(dcmat, 0))
    grads = {
        "s5_lambda_re": dlr[None], "s5_lambda_im": dli[None], "s5_log_dt": dldt[:, 0][None],
        "s5_b_re": dbr.transpose(1, 2, 0)[None], "s5_b_im": dbi.transpose(1, 2, 0)[None],
        "s5_c_re": dcr[None], "s5_c_im": -dci[None], "s5_d": dd,
    }
    return du, dw_glu, grads


def _mix0_fwd(x, g, p, full, late):
    h = _norm(x, g, name="mix0_norm")
    proj = _mm(h, full[("ab_w_in", 0)], name="mix0_in")
    u = proj[:, :S5_WIDTH]
    q, k, v = (_heads(proj[:, S5_WIDTH * (1 + n):S5_WIDTH * (2 + n)] * (SB_SCALE if n == 0 else 1.0)).astype(MXU_DTYPE)
               for n in range(3))
    (o, lsum), got = _sb_fwd(q, k, v, job=late.gather_job() if late else None)
    if late:
        full.update(zip(late.gather_keys, got))
    w_glu, w_out = full[("s5_w_glu", 0)], full[("ab_w_out", 0)]
    y_a, s5_saved = _s5_fwd(u, p, w_glu)
    mix = jnp.concatenate([y_a, _unheads(o)], axis=1).astype(MXU_DTYPE)
    x2 = _mm(mix, w_out, name="mix0_out", epilogue=lambda acc, xv: xv + acc, extras=[x])
    return x2, (x, h, q, k, v, lsum, mix, s5_saved)


def _mix0_bwd(dx2, saved, g, p, full, grads, late):
    x, h, q, k, v, lsum, mix, s5_saved = saved
    w_in, w_glu, w_out = full[("ab_w_in", 0)], full[("s5_w_glu", 0)], full[("ab_w_out", 0)]
    dmix = _mm(dx2, w_out, tb=True, name="mix0_dmix")
    grads[("ab_w_out", 0)] = _mm(mix, dx2, ta=True, name="mix0_dwout", out_dtype=WIRE_DTYPE)
    du, grads[("s5_w_glu", 0)], s5_grads = _s5_bwd(dmix[:, :S5_WIDTH], s5_saved, p, w_glu)
    (dq, dk, dv), got = _sb_bwd(q, k, v, lsum, _heads(dmix[:, S5_WIDTH:]), job=late.scatter_job(grads) if late else None)
    if late:
        late.received.update(zip(late.scatter_keys, got))
    dproj = jnp.concatenate([du] + [_unheads(t).astype(MXU_DTYPE) for t in (dq, dk, dv)], axis=1)
    grads[("ab_w_in", 0)] = _mm(h, dproj, ta=True, name="mix0_dwin", out_dtype=WIRE_DTYPE)
    dh = _mm(dproj, w_in, tb=True, name="mix0_dh")
    dx, dg = _norm_bwd(dh, x, g, dx2, name="mix0_norm_bwd")
    return dx, dg, s5_grads


def _shift_down(t, n):
    rows = lax.broadcasted_iota(jnp.int32, t.shape, 0)
    return jnp.where(rows >= n, pltpu.roll(t, n, 0), 0.0)


def _shift_up(t, n):
    rows = lax.broadcasted_iota(jnp.int32, t.shape, 0)
    return jnp.where(rows < t.shape[0] - n, pltpu.roll(t, t.shape[0] - n, 0), 0.0)


def _conv_fwd(proj, cw, *, tc=128):
    seq, c3 = proj.shape
    ch = c3 // 3
    nb = ch // tc

    def body(b_ref, c_ref, v_ref, w_ref, m_ref):
        pv = c_ref[...] * v_ref[...]
        w = w_ref[...]
        y = w[2:3] * pv + w[1:2] * _shift_down(pv, 1) + w[0:1] * _shift_down(pv, 2)
        m_ref[...] = (b_ref[...] * y).astype(m_ref.dtype)

    col = lambda part: pl.BlockSpec((seq, tc), lambda j: (0, part * nb + j))
    return pl.pallas_call(
        body, name="conv_fwd", grid=(nb,),
        in_specs=[col(0), col(1), col(2), pl.BlockSpec((3, tc), lambda j: (0, j))],
        out_specs=pl.BlockSpec((seq, tc), lambda j: (0, j)),
        out_shape=jax.ShapeDtypeStruct((seq, ch), MXU_DTYPE),
        compiler_params=_params("parallel"),
    )(proj, proj, proj, cw)


def _conv_bwd(proj, cw, dm, *, tc=128):
    seq, c3 = proj.shape
    ch = c3 // 3
    nb = ch // tc

    def body(b_ref, c_ref, v_ref, w_ref, dm_ref, dproj_ref, dw_ref, dc_ref, dv_ref):
        part = pl.program_id(1)

        @pl.when(part == 0)
        def _():
            cv, vv, dmv = c_ref[...], v_ref[...], dm_ref[...]
            pv = cv * vv
            w = w_ref[...]
            p1, p2 = _shift_down(pv, 1), _shift_down(pv, 2)
            y = w[2:3] * pv + w[1:2] * p1 + w[0:1] * p2
            dproj_ref[...] = (dmv * y).astype(dproj_ref.dtype)
            dy = dmv * b_ref[...]
            dp = w[2:3] * dy + w[1:2] * _shift_up(dy, 1) + w[0:1] * _shift_up(dy, 2)
            dc_ref[...] = (dp * vv).astype(dc_ref.dtype)
            dv_ref[...] = (dp * cv).astype(dv_ref.dtype)
            dw_ref[...] = jnp.concatenate([jnp.sum(dy * p2, axis=0, keepdims=True), jnp.sum(dy * p1, axis=0, keepdims=True),
                                           jnp.sum(dy * pv, axis=0, keepdims=True)], axis=0)

        @pl.when(part == 1)
        def _():
            dproj_ref[...] = dc_ref[...]

        @pl.when(part == 2)
        def _():
            dproj_ref[...] = dv_ref[...]

    col = lambda part: pl.BlockSpec((seq, tc), lambda j, t: (0, part * nb + j))
    small = pl.BlockSpec((3, tc), lambda j, t: (0, j))
    return pl.pallas_call(
        body, name="conv_bwd", grid=(nb, 3),
        in_specs=[col(0), col(1), col(2), small, pl.BlockSpec((seq, tc), lambda j, t: (0, j))],
        out_specs=[pl.BlockSpec((seq, tc), lambda j, t: (0, t * nb + j)), small],
        out_shape=[jax.ShapeDtypeStruct((seq, c3), MXU_DTYPE), jax.ShapeDtypeStruct((3, ch), F32)],
        scratch_shapes=[pltpu.VMEM((seq, tc), MXU_DTYPE)] * 2,
        compiler_params=_params("parallel", "arbitrary"),
    )(proj, proj, proj, cw, dm)


def _mix1_fwd(x, g, w_in, cw, w_out):
    h = _norm(x, g, name="mix1_norm")
    proj = _mm(h, w_in, name="mix1_in")
    m = _conv_fwd(proj, cw)
    x2 = _mm(m, w_out, name="mix1_out", epilogue=lambda acc, xv: xv + acc, extras=[x])
    return x2, (x, h, proj, m)


def _mix1_bwd(dx2, saved, g, w_in, cw, w_out):
    x, h, proj, m = saved
    dm = _mm(dx2, w_out, tb=True, name="mix1_dm")
    dw_out = _mm(m, dx2, ta=True, name="mix1_dwout", out_dtype=WIRE_DTYPE)
    dproj, dcw = _conv_bwd(proj, cw, dm)
    dw_in = _mm(h, dproj, ta=True, name="mix1_dwin", out_dtype=WIRE_DTYPE)
    dh = _mm(dproj, w_in, tb=True, name="mix1_dh")
    dx, dg = _norm_bwd(dh, x, g, dx2, name="mix1_norm_bwd")
    return dx, dg, dw_in, dcw, dw_out


def _loss_head(x, g, target):
    feat = x.shape[1]

    def fn(xv, gv, tv):
        err = _rms_fwd(xv, gv) - tv
        dx, dg = _rms_bwd(err / feat, xv, gv)
        return jnp.sum(err * err, keepdims=True) * (0.5 / feat), dx, dg

    return _rowmap(fn, [x, g, target], "rcr", [((1, 1), F32, "a"), (x.shape, F32, "r"), (g.shape, F32, "a")],
                   name="loss_head", tl=256)


def _slot(ref, place, chip=None, half=None):
    axis, width = place
    shape = list(ref.shape)
    start = [0, 0]
    if chip is not None:
        start[axis], shape[axis] = chip * width, width
    if half is not None:
        h_axis = 0 if shape[0] % 32 == 0 else 1
        shape[h_axis] //= 2
        start[h_axis] = start[h_axis] + half * shape[h_axis]
    hint = lambda s, d: s if isinstance(s, int) else pl.multiple_of(s, 128 if d == 1 else 8)
    return ref.at[tuple(pl.ds(hint(s, d), n) for d, (s, n) in enumerate(zip(start, shape)))]


class _Exchange:
    def __init__(self, kind, arrays, places):
        self.kind, self.arrays, self.places, self.n = kind, list(arrays), list(places), len(arrays)
        self.out_shape = []
        for t, (axis, width) in zip(self.arrays, self.places):
            if kind == "gather":
                shape = list(t.shape)
                shape[axis] = N_CHIPS * width
            else:
                shape = [N_CHIPS] + list(t.shape)
                shape[1 + axis] = width
            self.out_shape.append(jax.ShapeDtypeStruct(tuple(shape), t.dtype))
        n = self.n
        self.scratch = [pltpu.SemaphoreType.DMA((3 * n,)) for _ in range(4 if kind == "gather" else 2)]
        self.scratch.append(pltpu.SemaphoreType.DMA((n,)))

    def _copies(self, ins, outs, sems):
        x, y, c = lax.axis_index("x"), lax.axis_index("y"), lax.axis_index("c")
        peers = [(1 - x, y), (x, 1 - y), (1 - x, 1 - y)]
        remote = lambda src, dst, send, recv, k, to: pltpu.make_async_remote_copy(
            src_ref=src, dst_ref=dst, send_sem=send.at[k], recv_sem=recv.at[k], device_id=to, device_id_type=MESH_ID)
        local, ici, d2d = [], [], []
        for a in range(self.n):
            place = self.places[a]
            if self.kind == "gather":
                local.append(pltpu.make_async_copy(ins[a], _slot(outs[a], place, 2 * x + y), sems[4].at[a]))
                for r, (px, py) in enumerate(peers):
                    ici.append(remote(_slot(ins[a], place, None, c), _slot(outs[a], place, 2 * x + y, c),
                                      sems[0], sems[1], 3 * a + r, (px, py, c)))
                    landed = _slot(outs[a], place, 2 * px + py, c)
                    d2d.append(remote(landed, landed, sems[2], sems[3], 3 * a + r, (x, y, 1 - c)))
            else:
                local.append(pltpu.make_async_copy(_slot(ins[a], place, 2 * x + y), outs[a].at[3], sems[2].at[a]))
                for r, (px, py) in enumerate(peers):
                    ici.append(remote(_slot(ins[a], place, 2 * px + py), outs[a].at[r], sems[0], sems[1], 3 * a + r, (px, py, c)))
        return local, ici, d2d

    def start(self, ins, outs, sems):
        local, ici, _ = self._copies(ins, outs, sems)
        for cp in local + ici:
            cp.start()

    def relay(self, ins, outs, sems):
        _, ici, d2d = self._copies(ins, outs, sems)
        for arrived, onward in zip(ici, d2d):
            arrived.wait_recv()
            onward.start()

    def finish(self, ins, outs, sems):
        local, ici, d2d = self._copies(ins, outs, sems)
        for cp in local + d2d:
            cp.wait()
        for cp in ici:
            cp.wait_send() if d2d else cp.wait()


def _exchange_call(job, name):
    n = job.n

    def body(*refs):
        ins, outs, sems = refs[:n], refs[n:2 * n], refs[2 * n:]
        job.start(ins, outs, sems)
        job.relay(ins, outs, sems)
        job.finish(ins, outs, sems)

    return pl.pallas_call(
        body, name=name, in_specs=[ANY_SPEC] * n, out_specs=[ANY_SPEC] * n, out_shape=job.out_shape,
        scratch_shapes=job.scratch, compiler_params=pltpu.CompilerParams(has_side_effects=True),
    )(*job.arrays)


def _carried_call(body, *, name, grid, in_specs, out_specs, out_shape, semantics, operands, job=None):
    if job is None:
        return pl.pallas_call(body, name=name, grid=grid, in_specs=in_specs, out_specs=out_specs, out_shape=out_shape,
                              compiler_params=_params(*semantics))(*operands), []
    n_in, n_out, n = len(in_specs), len(out_specs), job.n
    steps = math.prod(grid)

    def wrapped(*refs):
        ins, job_ins = refs[:n_in], refs[n_in:n_in + n]
        outs, job_outs = refs[n_in + n:n_in + n + n_out], refs[n_in + n + n_out:n_in + 2 * n + n_out]
        sems = refs[n_in + 2 * n + n_out:]
        step = functools.reduce(lambda acc, d: acc * grid[d] + pl.program_id(d), range(len(grid)), 0)

        @pl.when(step == 0)
        def _():
            job.start(job_ins, job_outs, sems)

        @pl.when(step == (3 * steps) // 4)
        def _():
            job.relay(job_ins, job_outs, sems)

        body(*ins, *outs)

        @pl.when(step == steps - 1)
        def _():
            job.finish(job_ins, job_outs, sems)

    res = pl.pallas_call(
        wrapped, name=name, grid=grid, in_specs=list(in_specs) + [ANY_SPEC] * n, out_specs=list(out_specs) + [ANY_SPEC] * n,
        out_shape=list(out_shape) + job.out_shape, scratch_shapes=job.scratch,
        compiler_params=pltpu.CompilerParams(dimension_semantics=("arbitrary",) * len(grid), vmem_limit_bytes=VMEM_LIMIT,
                                             has_side_effects=True),
    )(*operands, *job.arrays)
    return res[:n_out], res[n_out:]


def _swap_with_sibling(parts):
    n = len(parts)

    def body(*refs):
        ins, outs = refs[:n], refs[n:2 * n]
        send, recv = refs[2 * n:]
        sibling = (lax.axis_index("x"), lax.axis_index("y"), 1 - lax.axis_index("c"))
        copies = [pltpu.make_async_remote_copy(src_ref=ins[a], dst_ref=outs[a], send_sem=send.at[a], recv_sem=recv.at[a],
                                               device_id=sibling, device_id_type=MESH_ID) for a in range(n)]
        for cp in copies:
            cp.start()
        for cp in copies:
            cp.wait()

    return pl.pallas_call(
        body, name="swap_with_sibling",
        in_specs=[ANY_SPEC] * n, out_specs=[ANY_SPEC] * n,
        out_shape=[jax.ShapeDtypeStruct(p.shape, p.dtype) for p in parts],
        scratch_shapes=[pltpu.SemaphoreType.DMA((n,)), pltpu.SemaphoreType.DMA((n,))],
        compiler_params=pltpu.CompilerParams(has_side_effects=True),
    )(*parts)


def _sum_all_devices(t):
    rows = t.shape[0]

    def body(t_ref, o_ref, slots, send, recv):
        x, y, c = lax.axis_index("x"), lax.axis_index("y"), lax.axis_index("c")
        me = 4 * x + 2 * y + c
        slots[me] = t_ref[...]
        copies = []
        for m in range(1, 8):
            peer = (x ^ (m >> 2), y ^ ((m >> 1) & 1), c ^ (m & 1))
            cp = pltpu.make_async_remote_copy(src_ref=t_ref, dst_ref=slots.at[me], send_sem=send.at[m - 1],
                                              recv_sem=recv.at[m - 1], device_id=peer, device_id_type=MESH_ID)
            cp.start()
            copies.append(cp)
        for cp in copies:
            cp.wait()
        acc = slots[0]
        for dev in range(1, 8):
            acc = acc + slots[dev]
        o_ref[...] = acc

    vmem = pl.BlockSpec(memory_space=pltpu.VMEM)
    return pl.pallas_call(
        body, name="sum_all_devices", in_specs=[vmem], out_specs=vmem,
        out_shape=jax.ShapeDtypeStruct(t.shape, F32),
        scratch_shapes=[pltpu.VMEM((8, rows, 128), F32), pltpu.SemaphoreType.DMA((7,)), pltpu.SemaphoreType.DMA((7,))],
        compiler_params=pltpu.CompilerParams(vmem_limit_bytes=VMEM_LIMIT, has_side_effects=True),
    )(t)


def _adamw(w, g, m, v):
    m = ADAM_B1 * m + (1.0 - ADAM_B1) * g
    v = ADAM_B2 * v + (1.0 - ADAM_B2) * jnp.square(g)
    m_hat = m / (1.0 - ADAM_B1 ** ADAM_STEP)
    v_hat = v / (1.0 - ADAM_B2 ** ADAM_STEP)
    return -ADAM_LR * (m_hat / (jnp.sqrt(v_hat) + ADAM_EPS) + ADAM_WD * w), m, v


def _chip_sum(received, name):
    rows, cols = received.shape[1:]
    tl = _row_block(rows, 512)

    def body(r_ref, o_ref):
        o_ref[...] = ((r_ref[0].astype(F32) + r_ref[1].astype(F32)) + r_ref[2].astype(F32)) + r_ref[3].astype(F32)

    return pl.pallas_call(body, name=name, grid=(rows // tl,),
                          in_specs=[pl.BlockSpec((N_CHIPS, tl, cols), lambda i: (0, i, 0))],
                          out_specs=pl.BlockSpec((tl, cols), lambda i: (i, 0)),
                          out_shape=jax.ShapeDtypeStruct((rows, cols), F32), compiler_params=_params("parallel"))(received)


def _adamw_layer(w, m, v, p_mine, p_other, layer, prev, name):
    _, rows, cols = w.shape
    tl = _row_block(rows, 256)

    def body(w_ref, m_ref, v_ref, pa_ref, pb_ref, *rest):
        g = pa_ref[:, :cols] + pb_ref[:, :cols]
        for o_ref, val in zip(rest[-4:], (g,) + _adamw(w_ref[...], g, m_ref[...], v_ref[...])):
            o_ref[...] = val

    stacked = pl.BlockSpec((None, tl, cols), lambda i: (layer, i, 0))
    part = pl.BlockSpec((tl, p_mine.shape[1]), lambda i: (i, 0))
    kept = list(prev) if prev else []
    return pl.pallas_call(
        body, name=name, grid=(rows // tl,),
        in_specs=[stacked] * 3 + [part] * 2 + [ANY_SPEC] * len(kept),
        out_specs=[stacked] * 4, out_shape=[jax.ShapeDtypeStruct(w.shape, F32)] * 4,
        input_output_aliases={5 + k: k for k in range(len(kept))},
        compiler_params=_params("parallel"),
    )(w, m, v, p_mine, p_other, *kept)


def _adamw_small(w, g, m, v):
    def fn(wv, gv, mv, vv):
        return _adamw(wv, gv, mv, vv)

    return _rowmap(fn, [w, g, m, v], "rrrr", [(w.shape, F32, "r")] * 3, name="adamw_small", tl=w.shape[0])


WEIGHTS = ['ffn1_norm', 'ffn1_w_gate', 'ffn1_w_up', 'ffn1_w_down', 'mix_norm', 'ffn2_norm', 'ffn2_w_gate', 'ffn2_w_up',
           'ffn2_w_down', 'ab_w_in', 's5_lambda_re', 's5_lambda_im', 's5_log_dt', 's5_b_re', 's5_b_im', 's5_c_re', 's5_c_im',
           's5_d', 's5_w_glu', 'ab_w_out', 'sc_w_in', 'sc_conv_w', 'sc_w_out', 'final_norm']
SHARDED = {'ffn1_w_gate': (1, FF_SLOT), 'ffn1_w_up': (1, FF_SLOT), 'ffn1_w_down': (0, FF_SLOT),
           'ffn2_w_gate': (1, FF_SLOT), 'ffn2_w_up': (1, FF_SLOT), 'ffn2_w_down': (0, FF_SLOT),
           'ab_w_in': (1, 512), 's5_w_glu': (0, 128), 'ab_w_out': (0, 256), 'sc_w_in': (1, 768), 'sc_conv_w': (1, 256),
           'sc_w_out': (0, 256)}
SMALL = [n for n in WEIGHTS if n not in SHARDED]


def _pack(arrays):
    rows = []
    for t in arrays:
        flat = t.reshape(-1)
        rows.append(jnp.pad(flat, (0, (-flat.shape[0]) % 128)))
    flat = jnp.concatenate(rows)
    return jnp.pad(flat, (0, (-flat.shape[0]) % 1024)).reshape(-1, 128)


def _unpack(packed, like):
    flat, out, pos = packed.reshape(-1), [], 0
    for t in like:
        out.append(flat[pos:pos + t.size].reshape(t.shape))
        pos += t.size + (-t.size) % 128
    return out


def _local_grads(x, target, p, full, late=None):
    small, grads, saved = {}, {}, []
    ffn = lambda which, layer: [full[(f"{which}_w_{n}", layer)] for n in ("gate", "up", "down")]
    for layer in range(2):
        x, s1 = _ffn_fwd(x, p["ffn1_norm"][layer:layer + 1], *ffn("ffn1", layer), tag=f"ffn1_{layer}")
        if layer == 0:
            x, sm = _mix0_fwd(x, p["mix_norm"][0:1], p, full, late)
        else:
            x, sm = _mix1_fwd(x, p["mix_norm"][1:2], full[("sc_w_in", 0)], full[("sc_conv_w", 0)], full[("sc_w_out", 0)])
        x, s2 = _ffn_fwd(x, p["ffn2_norm"][layer:layer + 1], *ffn("ffn2", layer), tag=f"ffn2_{layer}")
        saved.append((s1, sm, s2))
    loss, dx, dg_final = _loss_head(x, p["final_norm"][None], target)
    small["final_norm"] = dg_final[0]
    gains = {n: [None, None] for n in ("ffn1_norm", "mix_norm", "ffn2_norm")}

    def ffn_bwd(which, layer, dx, s):
        dx, (dg, dwg, dwu, dwd) = _ffn_bwd(dx, s, p[f"{which}_norm"][layer:layer + 1], *ffn(which, layer), tag=f"{which}_{layer}")
        gains[f"{which}_norm"][layer] = dg[0]
        grads.update({(f"{which}_w_gate", layer): dwg, (f"{which}_w_up", layer): dwu, (f"{which}_w_down", layer): dwd})
        return dx

    for layer in (1, 0):
        s1, sm, s2 = saved[layer]
        dx = ffn_bwd("ffn2", layer, dx, s2)
        if layer == 0:
            dx, dg, s5_grads = _mix0_bwd(dx, sm, p["mix_norm"][0:1], p, full, grads, late)
            small.update(s5_grads)
        else:
            dx, dg, dw_in, dcw, dw_out = _mix1_bwd(dx, sm, p["mix_norm"][1:2], full[("sc_w_in", 0)], full[("sc_conv_w", 0)],
                                                   full[("sc_w_out", 0)])
            grads.update({("sc_w_in", 0): dw_in, ("sc_conv_w", 0): dcw.astype(WIRE_DTYPE), ("sc_w_out", 0): dw_out})
        gains["mix_norm"][layer] = dg[0]
        dx = ffn_bwd("ffn1", layer, dx, s1)
    small.update({n: jnp.stack(pair) for n, pair in gains.items()})
    return loss, dx, small, grads


_EARLY = [("ffn1_w_gate", 0), ("ffn1_w_up", 0), ("ffn1_w_down", 0), ("ab_w_in", 0)]


class _Late:
    def __init__(self, shards, places):
        self.shards, self.places = shards, places
        self.gather_keys = [k for k in shards if k not in _EARLY]
        self.scatter_keys, self.received = [], {}

    def gather_job(self):
        return _Exchange("gather", [self.shards[k] for k in self.gather_keys], [self.places[k] for k in self.gather_keys])

    def scatter_job(self, grads):
        self.scatter_keys = list(grads)
        return _Exchange("scatter", [grads[k] for k in self.scatter_keys], [self.places[k] for k in self.scatter_keys])


def kernel(x, ffn1_norm, ffn1_w_gate, ffn1_w_up, ffn1_w_down, mix_norm, ffn2_norm, ffn2_w_gate, ffn2_w_up, ffn2_w_down, ab_w_in, s5_lambda_re, s5_lambda_im, s5_log_dt, s5_b_re, s5_b_im, s5_c_re, s5_c_im, s5_d, s5_w_glu, ab_w_out, sc_w_in, sc_conv_w, sc_w_out, final_norm, loss_target, m_ffn1_norm, m_ffn1_w_gate, m_ffn1_w_up, m_ffn1_w_down, m_mix_norm, m_ffn2_norm, m_ffn2_w_gate, m_ffn2_w_up, m_ffn2_w_down, m_ab_w_in, m_s5_lambda_re, m_s5_lambda_im, m_s5_log_dt, m_s5_b_re, m_s5_b_im, m_s5_c_re, m_s5_c_im, m_s5_d, m_s5_w_glu, m_ab_w_out, m_sc_w_in, m_sc_conv_w, m_sc_w_out, m_final_norm, v_ffn1_norm, v_ffn1_w_gate, v_ffn1_w_up, v_ffn1_w_down, v_mix_norm, v_ffn2_norm, v_ffn2_w_gate, v_ffn2_w_up, v_ffn2_w_down, v_ab_w_in, v_s5_lambda_re, v_s5_lambda_im, v_s5_log_dt, v_s5_b_re, v_s5_b_im, v_s5_c_re, v_s5_c_im, v_s5_d, v_s5_w_glu, v_ab_w_out, v_sc_w_in, v_sc_conv_w, v_sc_w_out, v_final_norm):
    args = dict(locals())
    p = {n: args[n] for n in WEIGHTS}
    mom = {n: args["m_" + n] for n in WEIGHTS}
    var = {n: args["v_" + n] for n in WEIGHTS}

    keys = [(n, layer) for n in SHARDED for layer in range(p[n].shape[0])]
    shards, places = {}, {}
    for n, layer in keys:
        axis, width = SHARDED[n]
        t = p[n][layer] if n == "sc_conv_w" else p[n][layer].astype(MXU_DTYPE)
        pad = [(0, 0), (0, 0)]
        pad[axis] = (0, width - t.shape[axis])
        shards[(n, layer)], places[(n, layer)] = jnp.pad(t, pad), (axis, width)
    job = _Exchange("gather", [shards[k] for k in _EARLY], [places[k] for k in _EARLY])
    full = dict(zip(_EARLY, _exchange_call(job, "gather_early")))
    late = _Late(shards, places)

    loss, dx, small, grads = _local_grads(x[0], loss_target[0], p, full, late)
    loss = lax.psum(loss[0, 0], ("x", "y", "c"))

    rest = [k for k in keys if k not in late.received]
    job = _Exchange("scatter", [grads[k] for k in rest], [places[k] for k in rest])
    late.received.update(zip(rest, _exchange_call(job, "scatter_last")))
    partial = [_chip_sum(late.received[(n, layer)], name=f"chip_sum_{n}_{layer}") for n, layer in keys]
    other = _swap_with_sibling(partial)
    out = {}
    for (n, layer), mine, theirs in zip(keys, partial, other):
        out[n] = _adamw_layer(p[n], mom[n], var[n], mine, theirs, layer, out.get(n), name=f"adamw_{n}_{layer}")

    like = [p[n] for n in SMALL]
    g_small = _sum_all_devices(_pack([small[n] for n in SMALL]))
    d_small, m_small, v_small = _adamw_small(_pack(like), g_small, _pack([mom[n] for n in SMALL]), _pack([var[n] for n in SMALL]))
    for k, packed in enumerate((g_small, d_small, m_small, v_small)):
        for n, t in zip(SMALL, _unpack(packed, like)):
            out.setdefault(n, [None] * 4)[k] = t

    return (loss, dx[None], *[out[n][0] for n in WEIGHTS], *[out[n][1] for n in WEIGHTS],
            *[out[n][2] for n in WEIGHTS], *[out[n][3] for n in WEIGHTS])
```

```python
import functools
import math

import jax
import jax.numpy as jnp
from jax import lax
from jax.experimental import pallas as pl
from jax.experimental.pallas import tpu as pltpu

F32 = jnp.float32
MXU_DTYPE = jnp.bfloat16
WIRE_DTYPE = jnp.bfloat16
MESH_ID = pl.DeviceIdType.MESH

D_MODEL = 1024
D_FF = 2752
N_CHIPS = 4
FF_SHARD = D_FF // N_CHIPS
FF_SLOT = 768
FF_PAD = N_CHIPS * FF_SLOT
S5_WIDTH = 512
S5_GROUP = 16
S5_GROUPS = 32
S5_STATE = 64
S5_LANES = S5_GROUPS * S5_STATE
S5_BLOCK = 512
SB_HEADS = 8
SB_DH = 64
SB_SCALE = 0.125
SB_PACK = 2
SB_QUERIES = 1024
SB_KEYS = 256
EPS = 1e-6
ADAM_LR, ADAM_B1, ADAM_B2, ADAM_EPS, ADAM_WD, ADAM_STEP = 0.001, 0.9, 0.999, 1e-08, 0.01, 10
VMEM_LIMIT = 56 * 1024 * 1024

ANY_SPEC = pl.BlockSpec(memory_space=pl.ANY)


def _params(*sem):
    return pltpu.CompilerParams(dimension_semantics=sem or None, vmem_limit_bytes=VMEM_LIMIT)


def _mm(a, b, *, name, ta=False, tb=False, out_dtype=F32, epilogue=None, extras=(), tm=1024, tn=1024, tk=1024):
    m, k = (a.shape[1], a.shape[0]) if ta else a.shape
    n = b.shape[0] if tb else b.shape[1]
    tm, tn, tk = min(tm, m), min(tn, n), min(tk, k)
    assert m % tm == 0 and n % tn == 0 and k % tk == 0, (name, m, n, k)
    nk = k // tk
    a_spec = pl.BlockSpec((tk, tm), lambda i, j, kk: (kk, i)) if ta else pl.BlockSpec((tm, tk), lambda i, j, kk: (i, kk))
    b_spec = pl.BlockSpec((tn, tk), lambda i, j, kk: (j, kk)) if tb else pl.BlockSpec((tk, tn), lambda i, j, kk: (kk, j))
    ex_specs = []
    for e in extras:
        if e.shape == (m, n):
            ex_specs.append(pl.BlockSpec((tm, tn), lambda i, j, kk: (i, j)))
        elif e.shape == (1, n):
            ex_specs.append(pl.BlockSpec((1, tn), lambda i, j, kk: (0, j)))
        else:
            assert e.shape == (m, 1), (name, e.shape)
            ex_specs.append(pl.BlockSpec((tm, 1), lambda i, j, kk: (i, 0)))
    dims = (((0 if ta else 1,), (1 if tb else 0,)), ((), ()))
    n_ex = len(extras)

    out_dtypes = list(out_dtype) if isinstance(out_dtype, (list, tuple)) else [out_dtype]
    n_out = len(out_dtypes)

    def body(a_ref, b_ref, *rest):
        ex_refs, o_refs = rest[:n_ex], rest[n_ex:n_ex + n_out]
        part = lax.dot_general(a_ref[...].astype(MXU_DTYPE), b_ref[...].astype(MXU_DTYPE), dims, preferred_element_type=F32)

        def finish(r):
            if epilogue is not None:
                r = epilogue(r, *[e[...] for e in ex_refs])
            for o_ref, val in zip(o_refs, r if isinstance(r, (tuple, list)) else (r,)):
                o_ref[...] = val.astype(o_ref.dtype)

        if nk == 1:
            finish(part)
            return
        acc_ref, kk = rest[n_ex + n_out], pl.program_id(2)

        @pl.when(kk == 0)
        def _():
            acc_ref[...] = part

        @pl.when(jnp.logical_and(kk > 0, kk < nk - 1))
        def _():
            acc_ref[...] += part

        @pl.when(kk == nk - 1)
        def _():
            finish(acc_ref[...] + part)

    res = pl.pallas_call(
        body, name=name, grid=(m // tm, n // tn, nk),
        in_specs=[a_spec, b_spec, *ex_specs],
        out_specs=[pl.BlockSpec((tm, tn), lambda i, j, kk: (i, j))] * n_out,
        out_shape=[jax.ShapeDtypeStruct((m, n), dt) for dt in out_dtypes],
        scratch_shapes=[pltpu.VMEM((tm, tn), F32)] if nk > 1 else [],
        compiler_params=_params("parallel", "parallel", "arbitrary"),
    )(a, b, *extras)
    return res if isinstance(out_dtype, (list, tuple)) else res[0]


def _row_block(rows, want):
    for tl in range(min(want, rows), 7, -1):
        if rows % tl == 0 and tl % 8 == 0:
            return tl
    return rows


def _rowmap(fn, ins, in_kinds, outs, *, name, tl):
    rows = next(x.shape[0] for x, kd in zip(ins, in_kinds) if kd == "r")
    tl = _row_block(rows, tl)
    n_in = len(ins)

    def spec(shape, kind):
        if kind == "r":
            return pl.BlockSpec((tl,) + tuple(shape[1:]), lambda i: (i,) + (0,) * (len(shape) - 1))
        return pl.BlockSpec(tuple(shape), lambda i: (0,) * len(shape))

    def body(*refs):
        in_refs, out_refs = refs[:n_in], refs[n_in:]
        res = fn(*[r[...] for r in in_refs])
        if not isinstance(res, (tuple, list)):
            res = (res,)
        for o_ref, val, (_, dt, kind) in zip(out_refs, res, outs):
            if kind == "r":
                o_ref[...] = val.astype(dt)
            else:
                @pl.when(pl.program_id(0) == 0)
                def _():
                    o_ref[...] = jnp.zeros_like(o_ref)

                o_ref[...] += val.astype(dt)

    has_acc = any(kd == "a" for _, _, kd in outs)
    res = pl.pallas_call(
        body, name=name, grid=(rows // tl,),
        in_specs=[spec(x.shape, kd) for x, kd in zip(ins, in_kinds)],
        out_specs=[spec(s, kd) for s, _, kd in outs],
        out_shape=[jax.ShapeDtypeStruct(s, dt) for s, dt, _ in outs],
        compiler_params=_params("arbitrary" if has_acc else "parallel"),
    )(*ins)
    return res[0] if len(outs) == 1 else res


def _rms_fwd(x, g):
    r = lax.rsqrt(jnp.mean(x * x, axis=-1, keepdims=True) + EPS)
    return x * r * g


def _rms_bwd(dh, x, g):
    r = lax.rsqrt(jnp.mean(x * x, axis=-1, keepdims=True) + EPS)
    xh = x * r
    dxh = dh * g
    dx = r * (dxh - xh * jnp.mean(dxh * xh, axis=-1, keepdims=True))
    return dx, jnp.sum(dh * xh, axis=0, keepdims=True)


def _norm(x, g, *, name):
    return _rowmap(lambda xv, gv: _rms_fwd(xv, gv), [x, g], "rc", [(x.shape, MXU_DTYPE, "r")], name=name, tl=256)


def _norm_bwd(dh, x, g, dres, *, name):
    def fn(dhv, xv, gv, drv):
        dx, dg = _rms_bwd(dhv, xv, gv)
        return dx + drv, dg
    return _rowmap(fn, [dh, x, g, dres], "rrcr", [(x.shape, F32, "r"), (g.shape, F32, "a")], name=name, tl=256)


def _swiglu_act(a, b):
    return jax.nn.silu(a) * b


def _ffn_up(x, g, wg, wu, *, name, tm=512, tn=1024):
    m, d = x.shape
    n = wg.shape[0]
    tm, tn = min(tm, m), min(tn, n)
    assert m % tm == 0 and n % tn == 0, (name, m, n)

    def body(x_ref, g_ref, wg_ref, wu_ref, h_ref, a_ref, b_ref, s_ref):
        @pl.when(pl.program_id(1) == 0)
        def _():
            h_ref[...] = _rms_fwd(x_ref[...], g_ref[...]).astype(h_ref.dtype)

        hv = h_ref[...]
        av = lax.dot_general(hv, wg_ref[...], NT_DIMS, preferred_element_type=F32)
        bv = lax.dot_general(hv, wu_ref[...], NT_DIMS, preferred_element_type=F32)
        a_ref[...] = av.astype(a_ref.dtype)
        b_ref[...] = bv.astype(b_ref.dtype)
        s_ref[...] = _swiglu_act(av, bv).astype(s_ref.dtype)

    rows = pl.BlockSpec((tm, d), lambda i, j: (i, 0))
    wgt = pl.BlockSpec((tn, d), lambda i, j: (j, 0))
    tile = pl.BlockSpec((tm, tn), lambda i, j: (i, j))
    return pl.pallas_call(
        body, name=name, grid=(m // tm, n // tn),
        in_specs=[rows, pl.BlockSpec((1, d), lambda i, j: (0, 0)), wgt, wgt],
        out_specs=[rows, tile, tile, tile],
        out_shape=[jax.ShapeDtypeStruct((m, d), MXU_DTYPE)] + [jax.ShapeDtypeStruct((m, n), MXU_DTYPE)] * 3,
        compiler_params=_params("parallel", "arbitrary"),
    )(x, g, wg, wu)


def _ffn_dx(da, db, wg, wu, x, g, dres, *, name, tm=512, tk=1024):
    m, f = da.shape
    d = wg.shape[1]
    tm, tk = min(tm, m), min(tk, f)
    assert m % tm == 0 and f % tk == 0, (name, m, f)
    nk = f // tk

    def body(da_ref, db_ref, wg_ref, wu_ref, x_ref, g_ref, dr_ref, dx_ref, dg_ref, acc_ref):
        i, kk = pl.program_id(0), pl.program_id(1)
        part = (jnp.dot(da_ref[...], wg_ref[...], preferred_element_type=F32)
                + jnp.dot(db_ref[...], wu_ref[...], preferred_element_type=F32))

        @pl.when(kk == 0)
        def _():
            acc_ref[...] = part

        @pl.when(kk > 0)
        def _():
            acc_ref[...] += part

        @pl.when(jnp.logical_and(i == 0, kk == 0))
        def _():
            dg_ref[...] = jnp.zeros_like(dg_ref)

        @pl.when(kk == nk - 1)
        def _():
            dx, dg = _rms_bwd(acc_ref[...], x_ref[...], g_ref[...])
            dx_ref[...] = dx + dr_ref[...]
            dg_ref[...] += dg

    act = pl.BlockSpec((tm, tk), lambda i, kk: (i, kk))
    wgt = pl.BlockSpec((tk, d), lambda i, kk: (kk, 0))
    rows = pl.BlockSpec((tm, d), lambda i, kk: (i, 0))
    one = pl.BlockSpec((1, d), lambda i, kk: (0, 0))
    return pl.pallas_call(
        body, name=name, grid=(m // tm, nk),
        in_specs=[act, act, wgt, wgt, rows, one, rows],
        out_specs=[rows, one],
        out_shape=[jax.ShapeDtypeStruct((m, d), F32), jax.ShapeDtypeStruct((1, d), F32)],
        scratch_shapes=[pltpu.VMEM((tm, d), F32)],
        compiler_params=_params("arbitrary", "arbitrary"),
    )(da, db, wg, wu, x, g, dres)


def _ffn_fwd(x, g, wg, wu, wd, tag):
    h, a, b, s = _ffn_up(x, g, wg, wu, name=f"{tag}_up")
    x2 = _mm(s, wd, name=f"{tag}_down", epilogue=lambda acc, xv: xv + 0.5 * acc, extras=[x])
    return x2, (x, h, a, b, s)


def _ffn_bwd(dx2, saved, g, wg, wu, wd, tag):
    x, h, a, b, s = saved

    def act_bwd(ds, av, bv):
        _, vjp = jax.vjp(_swiglu_act, av.astype(F32), bv.astype(F32))
        return vjp(0.5 * ds)

    da, db = _mm(dx2, wd, tb=True, name=f"{tag}_dact", epilogue=act_bwd, extras=[a, b], out_dtype=[MXU_DTYPE, MXU_DTYPE])
    dwd = _mm(s, dx2, ta=True, name=f"{tag}_dwd", out_dtype=WIRE_DTYPE, epilogue=lambda acc: 0.5 * acc)
    dwg = _mm(da, h, ta=True, name=f"{tag}_dwg", out_dtype=WIRE_DTYPE)
    dwu = _mm(db, h, ta=True, name=f"{tag}_dwu", out_dtype=WIRE_DTYPE)
    dx, dg = _ffn_dx(da, db, wg, wu, x, g, dx2, name=f"{tag}_dx")
    return dx, (dg, dwg, dwu, dwd)


def _softplus(z):
    return jnp.maximum(z, 0.0) + jnp.log(1.0 + jnp.exp(-jnp.abs(z)))


def _ones_dot(x, tri):
    if MXU_DTYPE == F32:
        return jnp.dot(x, tri, preferred_element_type=F32)
    hi = x.astype(MXU_DTYPE)
    lo = (x - hi.astype(F32)).astype(MXU_DTYPE)
    return jnp.dot(hi, tri, preferred_element_type=F32) + jnp.dot(lo, tri, preferred_element_type=F32)


NT_DIMS = (((1,), (1,)), ((), ()))
TN_DIMS = (((0,), (0,)), ((), ()))


def _sb_fwd(q, k, v, *, tq=SB_QUERIES, job=None):
    nh, seq, dh = q.shape
    tq = min(tq, seq)
    hp = SB_PACK

    tk = min(SB_KEYS, tq)
    per = tq // tk

    def body(q_ref, k_ref, v_ref, o_ref, ls_ref):
        i = pl.program_id(1)
        r_idx = lax.broadcasted_iota(jnp.int32, (tk, tk), 0)
        c_idx = lax.broadcasted_iota(jnp.int32, (tk, tk), 1)
        after = (r_idx > c_idx).astype(MXU_DTYPE)
        q_pos = lax.broadcasted_iota(jnp.int32, (tq, tk), 0)
        k_pos = lax.broadcasted_iota(jnp.int32, (tq, tk), 1)

        def block(hd, j, c, acc, straddles):
            off = pl.multiple_of(j * tk, tk)
            kv = k_ref[hd, pl.ds(off, tk), :]
            vv = v_ref[hd, pl.ds(off, tk), :]
            z = lax.dot_general(q_ref[hd], kv, NT_DIMS, preferred_element_type=F32)
            sp = _softplus(z)
            if straddles is None:
                lk = -sp
                w = jnp.exp(z - sp + _ones_dot(lk, after) + c)
            else:
                before = k_pos + straddles * tk < q_pos
                lk = jnp.where(before, -sp, 0.0)
                w = jnp.where(before, jnp.exp(z - sp + _ones_dot(lk, after) + c), 0.0)
            acc = acc + jnp.dot(w.astype(MXU_DTYPE), vv, preferred_element_type=F32)
            return c + jnp.sum(lk, axis=1, keepdims=True), acc

        def step(n, carry):
            return tuple(block(hd, i * per - 1 - n, *carry[hd], None) for hd in range(hp))

        carry = tuple((jnp.zeros((tq, 1), F32), jnp.zeros((tq, dh), F32)) for _ in range(hp))
        for s in reversed(range(per)):
            carry = tuple(block(hd, i * per + s, *carry[hd], s) for hd in range(hp))
        for hd, (c, acc) in enumerate(lax.fori_loop(0, i * per, step, carry)):
            o_ref[hd] = acc
            ls_ref[hd] = c

    whole = pl.BlockSpec((hp, seq, dh), lambda h, i: (h, 0, 0))
    return _carried_call(
        body, name="sb_fwd", grid=(nh // hp, seq // tq),
        in_specs=[pl.BlockSpec((hp, tq, dh), lambda h, i: (h, i, 0)), whole, whole],
        out_specs=[pl.BlockSpec((hp, tq, dh), lambda h, i: (h, i, 0)), pl.BlockSpec((hp, tq, 1), lambda h, i: (h, i, 0))],
        out_shape=[jax.ShapeDtypeStruct((nh, seq, dh), F32), jax.ShapeDtypeStruct((nh, seq, 1), F32)],
        semantics=("parallel", "parallel"), operands=(q, k, v), job=job)


def _sb_bwd(q, k, v, lsum, do, *, tq=SB_QUERIES, job=None):
    nh, seq, dh = q.shape
    tq = min(tq, seq)
    tk = min(SB_KEYS, tq)
    per = tq // tk
    hp = SB_PACK
    scale = SB_SCALE

    def body(q_ref, k_ref, v_ref, ls_ref, do_ref, dq_ref, dk_ref, dv_ref):
        i = pl.program_id(1)

        @pl.when(i == 0)
        def _():
            dk_ref[...] = jnp.zeros_like(dk_ref)
            dv_ref[...] = jnp.zeros_like(dv_ref)

        r_idx = lax.broadcasted_iota(jnp.int32, (tk, tk), 0)
        c_idx = lax.broadcasted_iota(jnp.int32, (tk, tk), 1)
        upto = (r_idx <= c_idx).astype(MXU_DTYPE)
        before = (r_idx < c_idx).astype(MXU_DTYPE)
        q_pos = lax.broadcasted_iota(jnp.int32, (tq, tk), 0)
        k_pos = lax.broadcasted_iota(jnp.int32, (tq, tk), 1)

        def block(hd, j, cp, ce, dq, straddles):
            off = pl.multiple_of(j * tk, tk)
            qv, dov = q_ref[hd], do_ref[hd].astype(MXU_DTYPE)
            kv = k_ref[hd, pl.ds(off, tk), :]
            vv = v_ref[hd, pl.ds(off, tk), :]
            z = lax.dot_general(qv, kv, NT_DIMS, preferred_element_type=F32)
            sp = _softplus(z)
            valid = None if straddles is None else k_pos + straddles * tk < q_pos
            lk = -sp if valid is None else jnp.where(valid, -sp, 0.0)
            w = jnp.exp(z - sp + (ls_ref[hd] - cp) - _ones_dot(lk, upto))
            if valid is not None:
                w = jnp.where(valid, w, 0.0)
            e = w * lax.dot_general(dov, vv, NT_DIMS, preferred_element_type=F32)
            earlier = _ones_dot(e, before) + ce
            keep = jnp.exp(-sp)
            dz = e * keep - (1.0 - keep) * earlier
            if valid is not None:
                dz = jnp.where(valid, dz, 0.0)
            dzm = dz.astype(MXU_DTYPE)
            dq = dq + jnp.dot(dzm, kv, preferred_element_type=F32)
            dk_ref[hd, pl.ds(off, tk), :] += lax.dot_general(dzm, qv, TN_DIMS, preferred_element_type=F32)
            dv_ref[hd, pl.ds(off, tk), :] += lax.dot_general(w.astype(MXU_DTYPE), dov, TN_DIMS, preferred_element_type=F32)
            return cp + jnp.sum(lk, axis=1, keepdims=True), ce + jnp.sum(e, axis=1, keepdims=True), dq

        def step(j, carry):
            return tuple(block(hd, j, *carry[hd], None) for hd in range(hp))

        zero = jnp.zeros((tq, 1), F32)
        carry = lax.fori_loop(0, i * per, step, tuple((zero, zero, jnp.zeros((tq, dh), F32)) for _ in range(hp)))
        for s in range(per):
            carry = tuple(block(hd, i * per + s, *carry[hd], s) for hd in range(hp))
        for hd in range(hp):
            dq_ref[hd] = carry[hd][2] * scale

    whole = pl.BlockSpec((hp, seq, dh), lambda h, i: (h, 0, 0))
    tile = pl.BlockSpec((hp, tq, dh), lambda h, i: (h, i, 0))
    return _carried_call(
        body, name="sb_bwd", grid=(nh // hp, seq // tq),
        in_specs=[tile, whole, whole, pl.BlockSpec((hp, tq, 1), lambda h, i: (h, i, 0)), tile],
        out_specs=[tile, whole, whole],
        out_shape=[jax.ShapeDtypeStruct((nh, seq, dh), F32)] * 3,
        semantics=("parallel", "arbitrary"), operands=(q, k, v, lsum, do), job=job)


def _heads(t):
    return t.reshape(t.shape[0], SB_HEADS, SB_DH).transpose(1, 0, 2)


def _unheads(t):
    return t.transpose(1, 0, 2).reshape(t.shape[1], SB_HEADS * SB_DH)


def _s5_disc(lr, li, ldt, br, bi):
    dt = jnp.exp(ldt)
    mag = jnp.exp(lr * dt)
    ar = mag * jnp.cos(li * dt)
    ai = mag * jnp.sin(li * dt)
    den = lr * lr + li * li
    nr = ar - 1.0
    cr = (nr * lr + ai * li) / den
    ci = (ai * lr - nr * li) / den
    return ar, ai, cr[None] * br - ci[None] * bi, cr[None] * bi + ci[None] * br


def _s5_prep(lr, li, ldt, br, bi):
    shapes = [lr.shape, lr.shape, br.shape, br.shape]

    def body(lr_ref, li_ref, ldt_ref, br_ref, bi_ref, *outs):
        for o, val in zip(outs, _s5_disc(lr_ref[...], li_ref[...], ldt_ref[...], br_ref[...], bi_ref[...])):
            o[...] = val

    return pl.pallas_call(body, name="s5_prep", out_shape=[jax.ShapeDtypeStruct(s, F32) for s in shapes])(lr, li, ldt, br, bi)


def _s5_prep_bwd(lr, li, ldt, br, bi, cts):
    args = (lr, li, ldt, br, bi)

    def body(*refs):
        ins, ct_refs, outs = refs[:5], refs[5:9], refs[9:]
        _, vjp = jax.vjp(_s5_disc, *[r[...] for r in ins])
        for o, val in zip(outs, vjp(tuple(r[...] for r in ct_refs))):
            o[...] = val

    return pl.pallas_call(body, name="s5_prep_bwd", out_shape=[jax.ShapeDtypeStruct(a.shape, F32) for a in args])(*args, *cts)


def _s5_scan(bu, a, *, tc=512):
    seq, w2 = bu.shape
    tw = S5_BLOCK
    tc = min(tc, seq)
    assert seq % tc == 0 and w2 % (2 * tw) == 0

    def body(bu_ref, a_ref, h_ref, cr_ref, ci_ref):
        @pl.when(pl.program_id(1) == 0)
        def _():
            cr_ref[...] = jnp.zeros_like(cr_ref)
            ci_ref[...] = jnp.zeros_like(ci_ref)

        re, im = pl.ds(0, tw), pl.ds(tw, tw)
        ar, ai = a_ref[:, re], a_ref[:, im]

        def step(t, carry):
            hr, hi = carry
            row = pl.ds(t, 1)
            nr = ar * hr - ai * hi + bu_ref[row, re]
            ni = ar * hi + ai * hr + bu_ref[row, im]
            h_ref[row, re] = nr
            h_ref[row, im] = ni
            return nr, ni

        hr, hi = lax.fori_loop(0, tc, step, (cr_ref[...], ci_ref[...]), unroll=8)
        cr_ref[...] = hr
        ci_ref[...] = hi

    blk = pl.BlockSpec((tc, 2 * tw), lambda j, t: (t, j))
    return pl.pallas_call(
        body, name="s5_scan", grid=(w2 // (2 * tw), seq // tc),
        in_specs=[blk, pl.BlockSpec((1, 2 * tw), lambda j, t: (0, j))],
        out_specs=blk,
        out_shape=jax.ShapeDtypeStruct((seq, w2), F32),
        scratch_shapes=[pltpu.VMEM((1, tw), F32)] * 2,
        compiler_params=_params("parallel", "arbitrary"),
    )(bu, a)


def _s5_scan_bwd(d, h, a, *, tc=512):
    seq, w2 = d.shape
    tw = S5_BLOCK
    tc = min(tc, seq)
    assert seq % tc == 0 and w2 % (2 * tw) == 0
    nt = seq // tc

    def body(d_ref, h_ref, a_ref, g_ref, da_ref, cr_ref, ci_ref):
        @pl.when(pl.program_id(1) == 0)
        def _():
            cr_ref[...] = jnp.zeros_like(cr_ref)
            ci_ref[...] = jnp.zeros_like(ci_ref)
            da_ref[...] = jnp.zeros_like(da_ref)

        re, im = pl.ds(0, tw), pl.ds(tw, tw)
        ar, ai = a_ref[:, re], a_ref[:, im]

        def step(n, carry):
            gr, gi, sr, si = carry
            row = pl.ds(tc - 1 - n, 1)
            hrt, hit = h_ref[row, re], h_ref[row, im]
            sr = sr + gr * hrt + gi * hit
            si = si + gi * hrt - gr * hit
            ngr = d_ref[row, re] + ar * gr + ai * gi
            ngi = d_ref[row, im] + ar * gi - ai * gr
            g_ref[row, re] = ngr
            g_ref[row, im] = ngi
            return ngr, ngi, sr, si

        gr, gi, sr, si = lax.fori_loop(0, tc, step, (cr_ref[...], ci_ref[...], da_ref[:, re], da_ref[:, im]), unroll=8)
        cr_ref[...] = gr
        ci_ref[...] = gi
        da_ref[:, re] = sr
        da_ref[:, im] = si

    blk = pl.BlockSpec((tc, 2 * tw), lambda j, t: (nt - 1 - t, j))
    row = pl.BlockSpec((1, 2 * tw), lambda j, t: (0, j))
    return pl.pallas_call(
        body, name="s5_scan_bwd", grid=(w2 // (2 * tw), nt),
        in_specs=[blk, blk, row],
        out_specs=[blk, row],
        out_shape=[jax.ShapeDtypeStruct((seq, w2), F32), jax.ShapeDtypeStruct((1, w2), F32)],
        scratch_shapes=[pltpu.VMEM((1, tw), F32)] * 2,
        compiler_params=_params("parallel", "arbitrary"),
    )(d, h, a)


def _pair_columns(re, im, axis):
    shape = re.shape
    split = shape[:axis] + (shape[axis] // S5_BLOCK, S5_BLOCK) + shape[axis + 1:]
    both = jnp.stack([re.reshape(split), im.reshape(split)], axis=axis + 1)
    return both.reshape(shape[:axis] + (2 * shape[axis],) + shape[axis + 1:])


def _unpair_columns(t, axis):
    shape = t.shape
    both = t.reshape(shape[:axis] + (shape[axis] // (2 * S5_BLOCK), 2, S5_BLOCK) + shape[axis + 1:])
    half = shape[:axis] + (shape[axis] // 2,) + shape[axis + 1:]
    return (lax.index_in_dim(both, 0, axis + 1, keepdims=False).reshape(half),
            lax.index_in_dim(both, 1, axis + 1, keepdims=False).reshape(half))


def _block_diag(t):
    g, a, b = t.shape
    eye = jnp.eye(g, dtype=t.dtype)
    return (t[:, :, None, :] * eye[:, None, :, None]).reshape(g * a, g * b)


def _block_diag_part(m, g):
    a, b = m.shape[0] // g, m.shape[1] // g
    return jnp.moveaxis(jnp.diagonal(m.reshape(g, a, g, b), axis1=0, axis2=2), -1, 0)


def _gelu_glu(y, gate_pre):
    z = jax.nn.gelu(y)
    return z * jax.nn.sigmoid(gate_pre)


def _s5_fwd(u, p, w_glu):
    lr, li = p["s5_lambda_re"][0], p["s5_lambda_im"][0]
    ldt = p["s5_log_dt"][0][:, None]
    br = p["s5_b_re"][0].transpose(2, 0, 1)
    bi = p["s5_b_im"][0].transpose(2, 0, 1)
    ar, ai, bbr, bbi = _s5_prep(lr, li, ldt, br, bi)
    a = _pair_columns(ar.reshape(1, S5_LANES), ai.reshape(1, S5_LANES), 1)
    bmat = _pair_columns(_block_diag(bbr.transpose(1, 0, 2)), _block_diag(bbi.transpose(1, 0, 2)), 1)
    cmat = _pair_columns(_block_diag(p["s5_c_re"][0].transpose(0, 2, 1)),
                         -_block_diag(p["s5_c_im"][0].transpose(0, 2, 1)), 0)
    bmat, cmat = bmat.astype(MXU_DTYPE), cmat.astype(MXU_DTYPE)
    bu = _mm(u, bmat, name="s5_bu")
    h = _s5_scan(bu, a)
    d = p["s5_d"]
    y = _mm(h, cmat, name="s5_y", epilogue=lambda acc, uv, dv: acc + dv * uv, extras=[u, d])
    z = _rowmap(jax.nn.gelu, [y], "r", [(y.shape, MXU_DTYPE, "r")], name="s5_gelu", tl=512)
    gate_pre = _mm(z, w_glu, name="s5_glu")
    out = _rowmap(_gelu_glu, [y, gate_pre], "rr", [(y.shape, F32, "r")], name="s5_gate", tl=512)
    return out, (u, lr, li, ldt, br, bi, a, bmat, cmat, h, y, z, gate_pre)


def _s5_bwd(dout, saved, p, w_glu):
    u, lr, li, ldt, br, bi, a, bmat, cmat, h, y, z, gate_pre = saved
    d = p["s5_d"]

    def gate_bwd(dov, yv, gv):
        zv = jax.nn.gelu(yv)
        sg = jax.nn.sigmoid(gv)
        return dov * sg, dov * zv * sg * (1.0 - sg)

    dz_direct, dgate = _rowmap(gate_bwd, [dout, y, gate_pre], "rrr", [(y.shape, F32, "r"), (y.shape, MXU_DTYPE, "r")],
                               name="s5_gate_bwd", tl=512)
    dw_glu = _mm(z, dgate, ta=True, name="s5_dwglu", out_dtype=WIRE_DTYPE)
    dz = _mm(dgate, w_glu, tb=True, name="s5_dz", epilogue=lambda acc, prev: acc + prev, extras=[dz_direct])

    def gelu_bwd(dzv, yv, uv, dvv):
        _, vjp = jax.vjp(jax.nn.gelu, yv)
        dy = vjp(dzv)[0]
        return dy, dy * dvv, jnp.sum(dy * uv, axis=0, keepdims=True)

    dy, du_skip, dd = _rowmap(gelu_bwd, [dz, y, u, d], "rrrc",
                              [(y.shape, F32, "r"), (y.shape, F32, "r"), (d.shape, F32, "a")], name="s5_gelu_bwd", tl=512)
    dcmat = _mm(h, dy, ta=True, name="s5_dc")
    dstate = _mm(dy, cmat, tb=True, name="s5_dstate")
    g, da = _s5_scan_bwd(dstate, h, a)
    du = _mm(g, bmat, tb=True, name="s5_du", epilogue=lambda acc, prev: acc + prev, extras=[du_skip], out_dtype=MXU_DTYPE)
    dbmat = _mm(u, g, ta=True, name="s5_db")
    dbbr, dbbi = (_block_diag_part(t, S5_GROUPS).transpose(1, 0, 2) for t in _unpair_columns(dbmat, 1))
    dar, dai = _unpair_columns(da, 1)
    cts = (dar.reshape(S5_GROUPS, S5_STATE), dai.reshape(S5_GROUPS, S5_STATE), dbbr, dbbi)
    dlr, dli, dldt, dbr, dbi = _s5_prep_bwd(lr, li, ldt, br, bi, cts)
    dcr, dci = (_block_diag_part(t, S5_GROUPS).transpose(0, 2, 1) for t in _unpair_columns(dcmat, 0))
    grads = {
        "s5_lambda_re": dlr[None], "s5_lambda_im": dli[None], "s5_log_dt": dldt[:, 0][None],
        "s5_b_re": dbr.transpose(1, 2, 0)[None], "s5_b_im": dbi.transpose(1, 2, 0)[None],
        "s5_c_re": dcr[None], "s5_c_im": -dci[None], "s5_d": dd,
    }
    return du, dw_glu, grads


def _mix0_fwd(x, g, p, full, late):
    h = _norm(x, g, name="mix0_norm")
    proj = _mm(h, full[("ab_w_in", 0)], name="mix0_in")
    u = proj[:, :S5_WIDTH]
    q, k, v = (_heads(proj[:, S5_WIDTH * (1 + n):S5_WIDTH * (2 + n)] * (SB_SCALE if n == 0 else 1.0)).astype(MXU_DTYPE)
               for n in range(3))
    (o, lsum), got = _sb_fwd(q, k, v, job=late.gather_job() if late else None)
    if late:
        full.update(zip(late.gather_keys, got))
    w_glu, w_out = full[("s5_w_glu", 0)], full[("ab_w_out", 0)]
    y_a, s5_saved = _s5_fwd(u, p, w_glu)
    mix = jnp.concatenate([y_a, _unheads(o)], axis=1).astype(MXU_DTYPE)
    x2 = _mm(mix, w_out, name="mix0_out", epilogue=lambda acc, xv: xv + acc, extras=[x])
    return x2, (x, h, q, k, v, lsum, mix, s5_saved)


def _mix0_bwd(dx2, saved, g, p, full, grads, late):
    x, h, q, k, v, lsum, mix, s5_saved = saved
    w_in, w_glu, w_out = full[("ab_w_in", 0)], full[("s5_w_glu", 0)], full[("ab_w_out", 0)]
    dmix = _mm(dx2, w_out, tb=True, name="mix0_dmix")
    grads[("ab_w_out", 0)] = _mm(mix, dx2, ta=True, name="mix0_dwout", out_dtype=WIRE_DTYPE)
    du, grads[("s5_w_glu", 0)], s5_grads = _s5_bwd(dmix[:, :S5_WIDTH], s5_saved, p, w_glu)
    (dq, dk, dv), got = _sb_bwd(q, k, v, lsum, _heads(dmix[:, S5_WIDTH:]), job=late.scatter_job(grads) if late else None)
    if late:
        late.received.update(zip(late.scatter_keys, got))
    dproj = jnp.concatenate([du] + [_unheads(t).astype(MXU_DTYPE) for t in (dq, dk, dv)], axis=1)
    grads[("ab_w_in", 0)] = _mm(h, dproj, ta=True, name="mix0_dwin", out_dtype=WIRE_DTYPE)
    dh = _mm(dproj, w_in, tb=True, name="mix0_dh")
    dx, dg = _norm_bwd(dh, x, g, dx2, name="mix0_norm_bwd")
    return dx, dg, s5_grads


def _shift_down(t, n):
    rows = lax.broadcasted_iota(jnp.int32, t.shape, 0)
    return jnp.where(rows >= n, pltpu.roll(t, n, 0), 0.0)


def _shift_up(t, n):
    rows = lax.broadcasted_iota(jnp.int32, t.shape, 0)
    return jnp.where(rows < t.shape[0] - n, pltpu.roll(t, t.shape[0] - n, 0), 0.0)


def _conv_fwd(proj, cw, *, tc=128):
    seq, c3 = proj.shape
    ch = c3 // 3
    nb = ch // tc

    def body(b_ref, c_ref, v_ref, w_ref, m_ref):
        pv = c_ref[...] * v_ref[...]
        w = w_ref[...]
        y = w[2:3] * pv + w[1:2] * _shift_down(pv, 1) + w[0:1] * _shift_down(pv, 2)
        m_ref[...] = (b_ref[...] * y).astype(m_ref.dtype)

    col = lambda part: pl.BlockSpec((seq, tc), lambda j: (0, part * nb + j))
    return pl.pallas_call(
        body, name="conv_fwd", grid=(nb,),
        in_specs=[col(0), col(1), col(2), pl.BlockSpec((3, tc), lambda j: (0, j))],
        out_specs=pl.BlockSpec((seq, tc), lambda j: (0, j)),
        out_shape=jax.ShapeDtypeStruct((seq, ch), MXU_DTYPE),
        compiler_params=_params("parallel"),
    )(proj, proj, proj, cw)


def _conv_bwd(proj, cw, dm, *, tc=128):
    seq, c3 = proj.shape
    ch = c3 // 3
    nb = ch // tc

    def body(b_ref, c_ref, v_ref, w_ref, dm_ref, dproj_ref, dw_ref, dc_ref, dv_ref):
        part = pl.program_id(1)

        @pl.when(part == 0)
        def _():
            cv, vv, dmv = c_ref[...], v_ref[...], dm_ref[...]
            pv = cv * vv
            w = w_ref[...]
            p1, p2 = _shift_down(pv, 1), _shift_down(pv, 2)
            y = w[2:3] * pv + w[1:2] * p1 + w[0:1] * p2
            dproj_ref[...] = (dmv * y).astype(dproj_ref.dtype)
            dy = dmv * b_ref[...]
            dp = w[2:3] * dy + w[1:2] * _shift_up(dy, 1) + w[0:1] * _shift_up(dy, 2)
            dc_ref[...] = (dp * vv).astype(dc_ref.dtype)
            dv_ref[...] = (dp * cv).astype(dv_ref.dtype)
            dw_ref[...] = jnp.concatenate([jnp.sum(dy * p2, axis=0, keepdims=True), jnp.sum(dy * p1, axis=0, keepdims=True),
                                           jnp.sum(dy * pv, axis=0, keepdims=True)], axis=0)

        @pl.when(part == 1)
        def _():
            dproj_ref[...] = dc_ref[...]

        @pl.when(part == 2)
        def _():
            dproj_ref[...] = dv_ref[...]

    col = lambda part: pl.BlockSpec((seq, tc), lambda j, t: (0, part * nb + j))
    small = pl.BlockSpec((3, tc), lambda j, t: (0, j))
    return pl.pallas_call(
        body, name="conv_bwd", grid=(nb, 3),
        in_specs=[col(0), col(1), col(2), small, pl.BlockSpec((seq, tc), lambda j, t: (0, j))],
        out_specs=[pl.BlockSpec((seq, tc), lambda j, t: (0, t * nb + j)), small],
        out_shape=[jax.ShapeDtypeStruct((seq, c3), MXU_DTYPE), jax.ShapeDtypeStruct((3, ch), F32)],
        scratch_shapes=[pltpu.VMEM((seq, tc), MXU_DTYPE)] * 2,
        compiler_params=_params("parallel", "arbitrary"),
    )(proj, proj, proj, cw, dm)


def _mix1_fwd(x, g, w_in, cw, w_out):
    h = _norm(x, g, name="mix1_norm")
    proj = _mm(h, w_in, name="mix1_in")
    m = _conv_fwd(proj, cw)
    x2 = _mm(m, w_out, name="mix1_out", epilogue=lambda acc, xv: xv + acc, extras=[x])
    return x2, (x, h, proj, m)


def _mix1_bwd(dx2, saved, g, w_in, cw, w_out):
    x, h, proj, m = saved
    dm = _mm(dx2, w_out, tb=True, name="mix1_dm")
    dw_out = _mm(m, dx2, ta=True, name="mix1_dwout", out_dtype=WIRE_DTYPE)
    dproj, dcw = _conv_bwd(proj, cw, dm)
    dw_in = _mm(h, dproj, ta=True, name="mix1_dwin", out_dtype=WIRE_DTYPE)
    dh = _mm(dproj, w_in, tb=True, name="mix1_dh")
    dx, dg = _norm_bwd(dh, x, g, dx2, name="mix1_norm_bwd")
    return dx, dg, dw_in, dcw, dw_out


def _loss_head(x, g, target):
    feat = x.shape[1]

    def fn(xv, gv, tv):
        err = _rms_fwd(xv, gv) - tv
        dx, dg = _rms_bwd(err / feat, xv, gv)
        return jnp.sum(err * err, keepdims=True) * (0.5 / feat), dx, dg

    return _rowmap(fn, [x, g, target], "rcr", [((1, 1), F32, "a"), (x.shape, F32, "r"), (g.shape, F32, "a")],
                   name="loss_head", tl=256)


def _slot(ref, place, chip=None, half=None):
    axis, width = place
    shape = list(ref.shape)
    start = [0, 0]
    if chip is not None:
        start[axis], shape[axis] = chip * width, width
    if half is not None:
        h_axis = 0 if shape[0] % 32 == 0 else 1
        shape[h_axis] //= 2
        start[h_axis] = start[h_axis] + half * shape[h_axis]
    hint = lambda s, d: s if isinstance(s, int) else pl.multiple_of(s, 128 if d == 1 else 8)
    return ref.at[tuple(pl.ds(hint(s, d), n) for d, (s, n) in enumerate(zip(start, shape)))]


class _Exchange:
    def __init__(self, kind, arrays, places):
        self.kind, self.arrays, self.places, self.n = kind, list(arrays), list(places), len(arrays)
        self.out_shape = []
        for t, (axis, width) in zip(self.arrays, self.places):
            if kind == "gather":
                shape = list(t.shape)
                shape[axis] = N_CHIPS * width
            else:
                shape = [N_CHIPS] + list(t.shape)
                shape[1 + axis] = width
            self.out_shape.append(jax.ShapeDtypeStruct(tuple(shape), t.dtype))
        n = self.n
        self.scratch = [pltpu.SemaphoreType.DMA((3 * n,)) for _ in range(4 if kind == "gather" else 2)]
        self.scratch.append(pltpu.SemaphoreType.DMA((n,)))

    def _copies(self, ins, outs, sems):
        x, y, c = lax.axis_index("x"), lax.axis_index("y"), lax.axis_index("c")
        peers = [(1 - x, y), (x, 1 - y), (1 - x, 1 - y)]
        remote = lambda src, dst, send, recv, k, to: pltpu.make_async_remote_copy(
            src_ref=src, dst_ref=dst, send_sem=send.at[k], recv_sem=recv.at[k], device_id=to, device_id_type=MESH_ID)
        local, ici, d2d = [], [], []
        for a in range(self.n):
            place = self.places[a]
            if self.kind == "gather":
                local.append(pltpu.make_async_copy(ins[a], _slot(outs[a], place, 2 * x + y), sems[4].at[a]))
                for r, (px, py) in enumerate(peers):
                    ici.append(remote(_slot(ins[a], place, None, c), _slot(outs[a], place, 2 * x + y, c),
                                      sems[0], sems[1], 3 * a + r, (px, py, c)))
                    landed = _slot(outs[a], place, 2 * px + py, c)
                    d2d.append(remote(landed, landed, sems[2], sems[3], 3 * a + r, (x, y, 1 - c)))
            else:
                local.append(pltpu.make_async_copy(_slot(ins[a], place, 2 * x + y), outs[a].at[3], sems[2].at[a]))
                for r, (px, py) in enumerate(peers):
                    ici.append(remote(_slot(ins[a], place, 2 * px + py), outs[a].at[r], sems[0], sems[1], 3 * a + r, (px, py, c)))
        return local, ici, d2d

    def start(self, ins, outs, sems):
        local, ici, _ = self._copies(ins, outs, sems)
        for cp in local + ici:
            cp.start()

    def relay(self, ins, outs, sems):
        _, ici, d2d = self._copies(ins, outs, sems)
        for arrived, onward in zip(ici, d2d):
            arrived.wait_recv()
            onward.start()

    def finish(self, ins, outs, sems):
        local, ici, d2d = self._copies(ins, outs, sems)
        for cp in local + d2d:
            cp.wait()
        for cp in ici:
            cp.wait_send() if d2d else cp.wait()


def _exchange_call(job, name):
    n = job.n

    def body(*refs):
        ins, outs, sems = refs[:n], refs[n:2 * n], refs[2 * n:]
        job.start(ins, outs, sems)
        job.relay(ins, outs, sems)
        job.finish(ins, outs, sems)

    return pl.pallas_call(
        body, name=name, in_specs=[ANY_SPEC] * n, out_specs=[ANY_SPEC] * n, out_shape=job.out_shape,
        scratch_shapes=job.scratch, compiler_params=pltpu.CompilerParams(has_side_effects=True),
    )(*job.arrays)


def _carried_call(body, *, name, grid, in_specs, out_specs, out_shape, semantics, operands, job=None):
    if job is None:
        return pl.pallas_call(body, name=name, grid=grid, in_specs=in_specs, out_specs=out_specs, out_shape=out_shape,
                              compiler_params=_params(*semantics))(*operands), []
    n_in, n_out, n = len(in_specs), len(out_specs), job.n
    steps = math.prod(grid)

    def wrapped(*refs):
        ins, job_ins = refs[:n_in], refs[n_in:n_in + n]
        outs, job_outs = refs[n_in + n:n_in + n + n_out], refs[n_in + n + n_out:n_in + 2 * n + n_out]
        sems = refs[n_in + 2 * n + n_out:]
        step = functools.reduce(lambda acc, d: acc * grid[d] + pl.program_id(d), range(len(grid)), 0)

        @pl.when(step == 0)
        def _():
            job.start(job_ins, job_outs, sems)

        @pl.when(step == (3 * steps) // 4)
        def _():
            job.relay(job_ins, job_outs, sems)

        body(*ins, *outs)

        @pl.when(step == steps - 1)
        def _():
            job.finish(job_ins, job_outs, sems)

    res = pl.pallas_call(
        wrapped, name=name, grid=grid, in_specs=list(in_specs) + [ANY_SPEC] * n, out_specs=list(out_specs) + [ANY_SPEC] * n,
        out_shape=list(out_shape) + job.out_shape, scratch_shapes=job.scratch,
        compiler_params=pltpu.CompilerParams(dimension_semantics=("arbitrary",) * len(grid), vmem_limit_bytes=VMEM_LIMIT,
                                             has_side_effects=True),
    )(*operands, *job.arrays)
    return res[:n_out], res[n_out:]


def _swap_with_sibling(parts):
    n = len(parts)

    def body(*refs):
        ins, outs = refs[:n], refs[n:2 * n]
        send, recv = refs[2 * n:]
        sibling = (lax.axis_index("x"), lax.axis_index("y"), 1 - lax.axis_index("c"))
        copies = [pltpu.make_async_remote_copy(src_ref=ins[a], dst_ref=outs[a], send_sem=send.at[a], recv_sem=recv.at[a],
                                               device_id=sibling, device_id_type=MESH_ID) for a in range(n)]
        for cp in copies:
            cp.start()
        for cp in copies:
            cp.wait()

    return pl.pallas_call(
        body, name="swap_with_sibling",
        in_specs=[ANY_SPEC] * n, out_specs=[ANY_SPEC] * n,
        out_shape=[jax.ShapeDtypeStruct(p.shape, p.dtype) for p in parts],
        scratch_shapes=[pltpu.SemaphoreType.DMA((n,)), pltpu.SemaphoreType.DMA((n,))],
        compiler_params=pltpu.CompilerParams(has_side_effects=True),
    )(*parts)


def _sum_all_devices(t):
    rows = t.shape[0]

    def body(t_ref, o_ref, slots, send, recv):
        x, y, c = lax.axis_index("x"), lax.axis_index("y"), lax.axis_index("c")
        me = 4 * x + 2 * y + c
        slots[me] = t_ref[...]
        copies = []
        for m in range(1, 8):
            peer = (x ^ (m >> 2), y ^ ((m >> 1) & 1), c ^ (m & 1))
            cp = pltpu.make_async_remote_copy(src_ref=t_ref, dst_ref=slots.at[me], send_sem=send.at[m - 1],
                                              recv_sem=recv.at[m - 1], device_id=peer, device_id_type=MESH_ID)
            cp.start()
            copies.append(cp)
        for cp in copies:
            cp.wait()
        acc = slots[0]
        for dev in range(1, 8):
            acc = acc + slots[dev]
        o_ref[...] = acc

    vmem = pl.BlockSpec(memory_space=pltpu.VMEM)
    return pl.pallas_call(
        body, name="sum_all_devices", in_specs=[vmem], out_specs=vmem,
        out_shape=jax.ShapeDtypeStruct(t.shape, F32),
        scratch_shapes=[pltpu.VMEM((8, rows, 128), F32), pltpu.SemaphoreType.DMA((7,)), pltpu.SemaphoreType.DMA((7,))],
        compiler_params=pltpu.CompilerParams(vmem_limit_bytes=VMEM_LIMIT, has_side_effects=True),
    )(t)


def _adamw(w, g, m, v):
    m = ADAM_B1 * m + (1.0 - ADAM_B1) * g
    v = ADAM_B2 * v + (1.0 - ADAM_B2) * jnp.square(g)
    m_hat = m / (1.0 - ADAM_B1 ** ADAM_STEP)
    v_hat = v / (1.0 - ADAM_B2 ** ADAM_STEP)
    return -ADAM_LR * (m_hat / (jnp.sqrt(v_hat) + ADAM_EPS) + ADAM_WD * w), m, v


def _chip_sum(received, name):
    rows, cols = received.shape[1:]
    tl = _row_block(rows, 512)

    def body(r_ref, o_ref):
        o_ref[...] = ((r_ref[0].astype(F32) + r_ref[1].astype(F32)) + r_ref[2].astype(F32)) + r_ref[3].astype(F32)

    return pl.pallas_call(body, name=name, grid=(rows // tl,),
                          in_specs=[pl.BlockSpec((N_CHIPS, tl, cols), lambda i: (0, i, 0))],
                          out_specs=pl.BlockSpec((tl, cols), lambda i: (i, 0)),
                          out_shape=jax.ShapeDtypeStruct((rows, cols), F32), compiler_params=_params("parallel"))(received)


def _adamw_layer(w, m, v, p_mine, p_other, layer, prev, name):
    _, rows, cols = w.shape
    tl = _row_block(rows, 512)

    def body(w_ref, m_ref, v_ref, pa_ref, pb_ref, *rest):
        g = pa_ref[:, :cols] + pb_ref[:, :cols]
        for o_ref, val in zip(rest[-4:], (g,) + _adamw(w_ref[...], g, m_ref[...], v_ref[...])):
            o_ref[...] = val

    stacked = pl.BlockSpec((None, tl, cols), lambda i: (layer, i, 0))
    part = pl.BlockSpec((tl, p_mine.shape[1]), lambda i: (i, 0))
    kept = list(prev) if prev else []
    return pl.pallas_call(
        body, name=name, grid=(rows // tl,),
        in_specs=[stacked] * 3 + [part] * 2 + [ANY_SPEC] * len(kept),
        out_specs=[stacked] * 4, out_shape=[jax.ShapeDtypeStruct(w.shape, F32)] * 4,
        input_output_aliases={5 + k: k for k in range(len(kept))},
        compiler_params=_params("parallel"),
    )(w, m, v, p_mine, p_other, *kept)


def _adamw_small(w, g, m, v):
    def fn(wv, gv, mv, vv):
        return _adamw(wv, gv, mv, vv)

    return _rowmap(fn, [w, g, m, v], "rrrr", [(w.shape, F32, "r")] * 3, name="adamw_small", tl=w.shape[0])


WEIGHTS = ['ffn1_norm', 'ffn1_w_gate', 'ffn1_w_up', 'ffn1_w_down', 'mix_norm', 'ffn2_norm', 'ffn2_w_gate', 'ffn2_w_up',
           'ffn2_w_down', 'ab_w_in', 's5_lambda_re', 's5_lambda_im', 's5_log_dt', 's5_b_re', 's5_b_im', 's5_c_re', 's5_c_im',
           's5_d', 's5_w_glu', 'ab_w_out', 'sc_w_in', 'sc_conv_w', 'sc_w_out', 'final_norm']
SHARDED = {'ffn1_w_gate': (0, FF_SLOT), 'ffn1_w_up': (0, FF_SLOT), 'ffn1_w_down': (0, FF_SLOT),
           'ffn2_w_gate': (0, FF_SLOT), 'ffn2_w_up': (0, FF_SLOT), 'ffn2_w_down': (0, FF_SLOT),
           'ab_w_in': (1, 512), 's5_w_glu': (0, 128), 'ab_w_out': (0, 256), 'sc_w_in': (1, 768), 'sc_conv_w': (1, 256),
           'sc_w_out': (0, 256)}
SWAPPED = ('ffn1_w_gate', 'ffn1_w_up', 'ffn2_w_gate', 'ffn2_w_up')
SMALL = [n for n in WEIGHTS if n not in SHARDED]


def _held(name, t):
    return jnp.swapaxes(t, 1, 2) if name in SWAPPED else t


def _pack(arrays):
    rows = []
    for t in arrays:
        flat = t.reshape(-1)
        rows.append(jnp.pad(flat, (0, (-flat.shape[0]) % 128)))
    flat = jnp.concatenate(rows)
    return jnp.pad(flat, (0, (-flat.shape[0]) % 1024)).reshape(-1, 128)


def _unpack(packed, like):
    flat, out, pos = packed.reshape(-1), [], 0
    for t in like:
        out.append(flat[pos:pos + t.size].reshape(t.shape))
        pos += t.size + (-t.size) % 128
    return out


def _local_grads(x, target, p, full, late=None):
    small, grads, saved = {}, {}, []
    ffn = lambda which, layer: [full[(f"{which}_w_{n}", layer)] for n in ("gate", "up", "down")]
    for layer in range(2):
        x, s1 = _ffn_fwd(x, p["ffn1_norm"][layer:layer + 1], *ffn("ffn1", layer), tag=f"ffn1_{layer}")
        if layer == 0:
            x, sm = _mix0_fwd(x, p["mix_norm"][0:1], p, full, late)
        else:
            x, sm = _mix1_fwd(x, p["mix_norm"][1:2], full[("sc_w_in", 0)], full[("sc_conv_w", 0)], full[("sc_w_out", 0)])
        x, s2 = _ffn_fwd(x, p["ffn2_norm"][layer:layer + 1], *ffn("ffn2", layer), tag=f"ffn2_{layer}")
        saved.append((s1, sm, s2))
    loss, dx, dg_final = _loss_head(x, p["final_norm"][None], target)
    small["final_norm"] = dg_final[0]
    gains = {n: [None, None] for n in ("ffn1_norm", "mix_norm", "ffn2_norm")}

    def ffn_bwd(which, layer, dx, s):
        dx, (dg, dwg, dwu, dwd) = _ffn_bwd(dx, s, p[f"{which}_norm"][layer:layer + 1], *ffn(which, layer), tag=f"{which}_{layer}")
        gains[f"{which}_norm"][layer] = dg[0]
        grads.update({(f"{which}_w_gate", layer): dwg, (f"{which}_w_up", layer): dwu, (f"{which}_w_down", layer): dwd})
        return dx

    for layer in (1, 0):
        s1, sm, s2 = saved[layer]
        dx = ffn_bwd("ffn2", layer, dx, s2)
        if layer == 0:
            dx, dg, s5_grads = _mix0_bwd(dx, sm, p["mix_norm"][0:1], p, full, grads, late)
            small.update(s5_grads)
        else:
            dx, dg, dw_in, dcw, dw_out = _mix1_bwd(dx, sm, p["mix_norm"][1:2], full[("sc_w_in", 0)], full[("sc_conv_w", 0)],
                                                   full[("sc_w_out", 0)])
            grads.update({("sc_w_in", 0): dw_in, ("sc_conv_w", 0): dcw.astype(WIRE_DTYPE), ("sc_w_out", 0): dw_out})
        gains["mix_norm"][layer] = dg[0]
        dx = ffn_bwd("ffn1", layer, dx, s1)
    small.update({n: jnp.stack(pair) for n, pair in gains.items()})
    return loss, dx, small, grads


_EARLY = [("ffn1_w_gate", 0), ("ffn1_w_up", 0), ("ffn1_w_down", 0), ("ab_w_in", 0)]


class _Late:
    def __init__(self, shards, places):
        self.shards, self.places = shards, places
        self.gather_keys = [k for k in shards if k not in _EARLY]
        self.scatter_keys, self.received = [], {}

    def gather_job(self):
        return _Exchange("gather", [self.shards[k] for k in self.gather_keys], [self.places[k] for k in self.gather_keys])

    def scatter_job(self, grads):
        self.scatter_keys = list(grads)
        return _Exchange("scatter", [grads[k] for k in self.scatter_keys], [self.places[k] for k in self.scatter_keys])


def kernel(x, ffn1_norm, ffn1_w_gate, ffn1_w_up, ffn1_w_down, mix_norm, ffn2_norm, ffn2_w_gate, ffn2_w_up, ffn2_w_down, ab_w_in, s5_lambda_re, s5_lambda_im, s5_log_dt, s5_b_re, s5_b_im, s5_c_re, s5_c_im, s5_d, s5_w_glu, ab_w_out, sc_w_in, sc_conv_w, sc_w_out, final_norm, loss_target, m_ffn1_norm, m_ffn1_w_gate, m_ffn1_w_up, m_ffn1_w_down, m_mix_norm, m_ffn2_norm, m_ffn2_w_gate, m_ffn2_w_up, m_ffn2_w_down, m_ab_w_in, m_s5_lambda_re, m_s5_lambda_im, m_s5_log_dt, m_s5_b_re, m_s5_b_im, m_s5_c_re, m_s5_c_im, m_s5_d, m_s5_w_glu, m_ab_w_out, m_sc_w_in, m_sc_conv_w, m_sc_w_out, m_final_norm, v_ffn1_norm, v_ffn1_w_gate, v_ffn1_w_up, v_ffn1_w_down, v_mix_norm, v_ffn2_norm, v_ffn2_w_gate, v_ffn2_w_up, v_ffn2_w_down, v_ab_w_in, v_s5_lambda_re, v_s5_lambda_im, v_s5_log_dt, v_s5_b_re, v_s5_b_im, v_s5_c_re, v_s5_c_im, v_s5_d, v_s5_w_glu, v_ab_w_out, v_sc_w_in, v_sc_conv_w, v_sc_w_out, v_final_norm):
    args = dict(locals())
    p = {n: _held(n, args[n]) for n in WEIGHTS}
    mom = {n: _held(n, args["m_" + n]) for n in WEIGHTS}
    var = {n: _held(n, args["v_" + n]) for n in WEIGHTS}

    keys = [(n, layer) for n in SHARDED for layer in range(p[n].shape[0])]
    shards, places = {}, {}
    for n, layer in keys:
        axis, width = SHARDED[n]
        t = p[n][layer] if n == "sc_conv_w" else p[n][layer].astype(MXU_DTYPE)
        pad = [(0, 0), (0, 0)]
        pad[axis] = (0, width - t.shape[axis])
        shards[(n, layer)], places[(n, layer)] = jnp.pad(t, pad), (axis, width)
    job = _Exchange("gather", [shards[k] for k in _EARLY], [places[k] for k in _EARLY])
    full = dict(zip(_EARLY, _exchange_call(job, "gather_early")))
    late = _Late(shards, places)

    loss, dx, small, grads = _local_grads(x[0], loss_target[0], p, full, late)
    loss = lax.psum(loss[0, 0], ("x", "y", "c"))

    rest = [k for k in keys if k not in late.received]
    job = _Exchange("scatter", [grads[k] for k in rest], [places[k] for k in rest])
    late.received.update(zip(rest, _exchange_call(job, "scatter_last")))
    partial = [_chip_sum(late.received[(n, layer)], name=f"chip_sum_{n}_{layer}") for n, layer in keys]
    other = _swap_with_sibling(partial)
    out = {}
    for (n, layer), mine, theirs in zip(keys, partial, other):
        out[n] = _adamw_layer(p[n], mom[n], var[n], mine, theirs, layer, out.get(n), name=f"adamw_{n}_{layer}")
    out = {n: [_held(n, t) for t in res] for n, res in out.items()}

    like = [p[n] for n in SMALL]
    g_small = _sum_all_devices(_pack([small[n] for n in SMALL]))
    d_small, m_small, v_small = _adamw_small(_pack(like), g_small, _pack([mom[n] for n in SMALL]), _pack([var[n] for n in SMALL]))
    for k, packed in enumerate((g_small, d_small, m_small, v_small)):
        for n, t in zip(SMALL, _unpack(packed, like)):
            out.setdefault(n, [None] * 4)[k] = t

    return (loss, dx[None], *[out[n][0] for n in WEIGHTS], *[out[n][1] for n in WEIGHTS],
            *[out[n][2] for n in WEIGHTS], *[out[n][3] for n in WEIGHTS])
```

```python
import functools
import math

import jax
import jax.numpy as jnp
from jax import lax
from jax.experimental import pallas as pl
from jax.experimental.pallas import tpu as pltpu

F32 = jnp.float32
MXU_DTYPE = jnp.bfloat16
WIRE_DTYPE = jnp.bfloat16
MESH_ID = pl.DeviceIdType.MESH

D_MODEL = 1024
D_FF = 2752
N_CHIPS = 4
FF_SHARD = D_FF // N_CHIPS
FF_SLOT = 768
FF_PAD = N_CHIPS * FF_SLOT
S5_WIDTH = 512
S5_GROUP = 16
S5_GROUPS = 32
S5_STATE = 64
S5_LANES = S5_GROUPS * S5_STATE
S5_BLOCK = 512
SB_HEADS = 8
SB_DH = 64
SB_SCALE = 0.125
SB_PACK = 2
SB_QUERIES = 1024
SB_KEYS = 256
EPS = 1e-6
ADAM_LR, ADAM_B1, ADAM_B2, ADAM_EPS, ADAM_WD, ADAM_STEP = 0.001, 0.9, 0.999, 1e-08, 0.01, 10
VMEM_LIMIT = 56 * 1024 * 1024

ANY_SPEC = pl.BlockSpec(memory_space=pl.ANY)


def _params(*sem):
    return pltpu.CompilerParams(dimension_semantics=sem or None, vmem_limit_bytes=VMEM_LIMIT)


def _mm(a, b, *, name, ta=False, tb=False, out_dtype=F32, epilogue=None, extras=(), tm=1024, tn=1024, tk=1024, job=None):
    m, k = (a.shape[1], a.shape[0]) if ta else a.shape
    n = b.shape[0] if tb else b.shape[1]
    tm, tn, tk = min(tm, m), min(tn, n), min(tk, k)
    assert m % tm == 0 and n % tn == 0 and k % tk == 0, (name, m, n, k)
    nk = k // tk
    a_spec = pl.BlockSpec((tk, tm), lambda i, j, kk: (kk, i)) if ta else pl.BlockSpec((tm, tk), lambda i, j, kk: (i, kk))
    b_spec = pl.BlockSpec((tn, tk), lambda i, j, kk: (j, kk)) if tb else pl.BlockSpec((tk, tn), lambda i, j, kk: (kk, j))
    ex_specs = []
    for e in extras:
        if e.shape == (m, n):
            ex_specs.append(pl.BlockSpec((tm, tn), lambda i, j, kk: (i, j)))
        elif e.shape == (1, n):
            ex_specs.append(pl.BlockSpec((1, tn), lambda i, j, kk: (0, j)))
        else:
            assert e.shape == (m, 1), (name, e.shape)
            ex_specs.append(pl.BlockSpec((tm, 1), lambda i, j, kk: (i, 0)))
    dims = (((0 if ta else 1,), (1 if tb else 0,)), ((), ()))
    n_ex = len(extras)

    out_dtypes = list(out_dtype) if isinstance(out_dtype, (list, tuple)) else [out_dtype]
    n_out = len(out_dtypes)

    def body(a_ref, b_ref, *rest):
        ex_refs, o_refs = rest[:n_ex], rest[n_ex:n_ex + n_out]

        def product():
            return lax.dot_general(a_ref[...].astype(MXU_DTYPE), b_ref[...].astype(MXU_DTYPE), dims, preferred_element_type=F32)

        def finish(r):
            if epilogue is not None:
                r = epilogue(r, *[e[...] for e in ex_refs])
            for o_ref, val in zip(o_refs, r if isinstance(r, (tuple, list)) else (r,)):
                o_ref[...] = val.astype(o_ref.dtype)

        if nk == 1:
            finish(product())
            return
        acc_ref, kk = rest[n_ex + n_out], pl.program_id(2)

        @pl.when(kk == 0)
        def _():
            acc_ref[...] = jnp.zeros_like(acc_ref)

        acc_ref[...] += product()

        @pl.when(kk == nk - 1)
        def _():
            finish(acc_ref[...])

    res, got = _carried_call(
        body, name=name, grid=(m // tm, n // tn, nk),
        in_specs=[a_spec, b_spec, *ex_specs],
        out_specs=[pl.BlockSpec((tm, tn), lambda i, j, kk: (i, j))] * n_out,
        out_shape=[jax.ShapeDtypeStruct((m, n), dt) for dt in out_dtypes],
        scratch_shapes=[pltpu.VMEM((tm, tn), F32)] if nk > 1 else [],
        semantics=("parallel", "parallel", "arbitrary"), operands=(a, b, *extras), job=job)
    res = res if isinstance(out_dtype, (list, tuple)) else res[0]
    return res if job is None else (res, got)


def _row_block(rows, want):
    for tl in range(min(want, rows), 7, -1):
        if rows % tl == 0 and tl % 8 == 0:
            return tl
    return rows


def _rowmap(fn, ins, in_kinds, outs, *, name, tl):
    rows = next(x.shape[0] for x, kd in zip(ins, in_kinds) if kd == "r")
    tl = _row_block(rows, tl)
    n_in = len(ins)

    def spec(shape, kind):
        if kind == "r":
            return pl.BlockSpec((tl,) + tuple(shape[1:]), lambda i: (i,) + (0,) * (len(shape) - 1))
        return pl.BlockSpec(tuple(shape), lambda i: (0,) * len(shape))

    def body(*refs):
        in_refs, out_refs = refs[:n_in], refs[n_in:]
        res = fn(*[r[...] for r in in_refs])
        if not isinstance(res, (tuple, list)):
            res = (res,)
        for o_ref, val, (_, dt, kind) in zip(out_refs, res, outs):
            if kind == "r":
                o_ref[...] = val.astype(dt)
            else:
                @pl.when(pl.program_id(0) == 0)
                def _():
                    o_ref[...] = jnp.zeros_like(o_ref)

                o_ref[...] += val.astype(dt)

    has_acc = any(kd == "a" for _, _, kd in outs)
    res = pl.pallas_call(
        body, name=name, grid=(rows // tl,),
        in_specs=[spec(x.shape, kd) for x, kd in zip(ins, in_kinds)],
        out_specs=[spec(s, kd) for s, _, kd in outs],
        out_shape=[jax.ShapeDtypeStruct(s, dt) for s, dt, _ in outs],
        compiler_params=_params("arbitrary" if has_acc else "parallel"),
    )(*ins)
    return res[0] if len(outs) == 1 else res


def _rms_fwd(x, g):
    r = lax.rsqrt(jnp.mean(x * x, axis=-1, keepdims=True) + EPS)
    return x * r * g


def _rms_bwd(dh, x, g):
    r = lax.rsqrt(jnp.mean(x * x, axis=-1, keepdims=True) + EPS)
    xh = x * r
    dxh = dh * g
    dx = r * (dxh - xh * jnp.mean(dxh * xh, axis=-1, keepdims=True))
    return dx, jnp.sum(dh * xh, axis=0, keepdims=True)


def _norm(x, g, *, name):
    return _rowmap(lambda xv, gv: _rms_fwd(xv, gv), [x, g], "rc", [(x.shape, MXU_DTYPE, "r")], name=name, tl=256)


def _norm_bwd(dh, x, g, dres, *, name):
    def fn(dhv, xv, gv, drv):
        dx, dg = _rms_bwd(dhv, xv, gv)
        return dx + drv, dg
    return _rowmap(fn, [dh, x, g, dres], "rrcr", [(x.shape, F32, "r"), (g.shape, F32, "a")], name=name, tl=256)


def _swiglu_act(a, b):
    return jax.nn.silu(a) * b


def _ffn_up(x, g, wg, wu, *, name, tm=512, tn=1024, job=None):
    m, d = x.shape
    n = wg.shape[0]
    tm, tn = min(tm, m), min(tn, n)
    assert m % tm == 0 and n % tn == 0, (name, m, n)

    def body(x_ref, g_ref, wg_ref, wu_ref, h_ref, a_ref, b_ref, s_ref):
        @pl.when(pl.program_id(1) == 0)
        def _():
            h_ref[...] = _rms_fwd(x_ref[...], g_ref[...]).astype(h_ref.dtype)

        hv = h_ref[...]
        av = lax.dot_general(hv, wg_ref[...], NT_DIMS, preferred_element_type=F32)
        bv = lax.dot_general(hv, wu_ref[...], NT_DIMS, preferred_element_type=F32)
        a_ref[...] = av.astype(a_ref.dtype)
        b_ref[...] = bv.astype(b_ref.dtype)
        s_ref[...] = _swiglu_act(av, bv).astype(s_ref.dtype)

    rows = pl.BlockSpec((tm, d), lambda i, j: (i, 0))
    wgt = pl.BlockSpec((tn, d), lambda i, j: (j, 0))
    tile = pl.BlockSpec((tm, tn), lambda i, j: (i, j))
    return _carried_call(
        body, name=name, grid=(m // tm, n // tn),
        in_specs=[rows, pl.BlockSpec((1, d), lambda i, j: (0, 0)), wgt, wgt],
        out_specs=[rows, tile, tile, tile],
        out_shape=[jax.ShapeDtypeStruct((m, d), MXU_DTYPE)] + [jax.ShapeDtypeStruct((m, n), MXU_DTYPE)] * 3,
        semantics=("parallel", "arbitrary"), operands=(x, g, wg, wu), job=job)


def _ffn_dx(da, db, wg, wu, x, g, dres, *, name, tm=512, tk=1024, job=None):
    m, f = da.shape
    d = wg.shape[1]
    tm, tk = min(tm, m), min(tk, f)
    assert m % tm == 0 and f % tk == 0, (name, m, f)
    nk = f // tk

    def body(da_ref, db_ref, wg_ref, wu_ref, x_ref, g_ref, dr_ref, dx_ref, dg_ref, acc_ref):
        i, kk = pl.program_id(0), pl.program_id(1)

        @pl.when(kk == 0)
        def _():
            acc_ref[...] = jnp.zeros_like(acc_ref)

        acc_ref[...] += jnp.dot(da_ref[...], wg_ref[...], preferred_element_type=F32)
        acc_ref[...] += jnp.dot(db_ref[...], wu_ref[...], preferred_element_type=F32)

        @pl.when(jnp.logical_and(i == 0, kk == 0))
        def _():
            dg_ref[...] = jnp.zeros_like(dg_ref)

        @pl.when(kk == nk - 1)
        def _():
            dx, dg = _rms_bwd(acc_ref[...], x_ref[...], g_ref[...])
            dx_ref[...] = dx + dr_ref[...]
            dg_ref[...] += dg

    act = pl.BlockSpec((tm, tk), lambda i, kk: (i, kk))
    wgt = pl.BlockSpec((tk, d), lambda i, kk: (kk, 0))
    rows = pl.BlockSpec((tm, d), lambda i, kk: (i, 0))
    one = pl.BlockSpec((1, d), lambda i, kk: (0, 0))
    return _carried_call(
        body, name=name, grid=(m // tm, nk),
        in_specs=[act, act, wgt, wgt, rows, one, rows],
        out_specs=[rows, one],
        out_shape=[jax.ShapeDtypeStruct((m, d), F32), jax.ShapeDtypeStruct((1, d), F32)],
        scratch_shapes=[pltpu.VMEM((tm, d), F32)],
        semantics=("arbitrary", "arbitrary"), operands=(da, db, wg, wu, x, g, dres), job=job)


def _ffn_fwd(x, g, full, which, layer, late):
    tag = f"{which}_{layer}"
    job, keys = late.gather_job(f"{tag}_up") if late else (None, [])
    (h, a, b, s), got = _ffn_up(x, g, full[(f"{which}_w_gate", layer)], full[(f"{which}_w_up", layer)], name=f"{tag}_up", job=job)
    full.update(zip(keys, got))
    x2 = _mm(s, full[(f"{which}_w_down", layer)], name=f"{tag}_down", epilogue=lambda acc, xv: xv + 0.5 * acc, extras=[x])
    return x2, (x, h, a, b, s)


def _ffn_bwd(dx2, saved, g, full, which, layer, grads, late, inline):
    x, h, a, b, s = saved
    tag = f"{which}_{layer}"
    kg, ku, kd = [(f"{which}_w_{n}", layer) for n in ("gate", "up", "down")]
    wg, wu, wd = full[kg], full[ku], full[kd]
    send = (lambda: late.scatter_job(grads)) if (late and inline) else (lambda: (None, []))

    def act_bwd(ds, av, bv):
        _, vjp = jax.vjp(_swiglu_act, av.astype(F32), bv.astype(F32))
        return vjp(0.5 * ds)

    grads[kd] = _mm(s, dx2, ta=True, name=f"{tag}_dwd", out_dtype=WIRE_DTYPE, epilogue=lambda acc: 0.5 * acc)
    job, keys = send()
    (da, db), got = _carried(_mm, dx2, wd, tb=True, name=f"{tag}_dact", epilogue=act_bwd, extras=[a, b],
                             out_dtype=[MXU_DTYPE, MXU_DTYPE], job=job)
    _note(late, keys, got)
    grads[kg] = _mm(da, h, ta=True, name=f"{tag}_dwg", out_dtype=WIRE_DTYPE)
    job, keys = send()
    grads[ku], got = _carried(_mm, db, h, ta=True, name=f"{tag}_dwu", out_dtype=WIRE_DTYPE, job=job)
    _note(late, keys, got)
    job, keys = send()
    (dx, dg), got = _ffn_dx(da, db, wg, wu, x, g, dx2, name=f"{tag}_dx", job=job)
    _note(late, keys, got)
    return dx, dg


def _carried(fn, *args, job, **kwargs):
    return fn(*args, job=job, **kwargs) if job is not None else (fn(*args, **kwargs), [])


def _note(late, keys, got):
    if late:
        late.received.update(zip(keys, got))


def _softplus(z):
    return jnp.maximum(z, 0.0) + jnp.log(1.0 + jnp.exp(-jnp.abs(z)))


def _ones_dot(x, tri):
    if MXU_DTYPE == F32:
        return jnp.dot(x, tri, preferred_element_type=F32)
    hi = x.astype(MXU_DTYPE)
    lo = (x - hi.astype(F32)).astype(MXU_DTYPE)
    return jnp.dot(hi, tri, preferred_element_type=F32) + jnp.dot(lo, tri, preferred_element_type=F32)


NT_DIMS = (((1,), (1,)), ((), ()))
TN_DIMS = (((0,), (0,)), ((), ()))


def _sb_fwd(q, k, v, *, tq=SB_QUERIES, job=None):
    nh, seq, dh = q.shape
    tq = min(tq, seq)
    hp = SB_PACK

    tk = min(SB_KEYS, tq)
    per = tq // tk

    def body(q_ref, k_ref, v_ref, o_ref, ls_ref):
        i = pl.program_id(1)
        r_idx = lax.broadcasted_iota(jnp.int32, (tk, tk), 0)
        c_idx = lax.broadcasted_iota(jnp.int32, (tk, tk), 1)
        after = (r_idx > c_idx).astype(MXU_DTYPE)
        q_pos = lax.broadcasted_iota(jnp.int32, (tq, tk), 0)
        k_pos = lax.broadcasted_iota(jnp.int32, (tq, tk), 1)

        def block(hd, j, c, acc, straddles):
            off = pl.multiple_of(j * tk, tk)
            kv = k_ref[hd, pl.ds(off, tk), :]
            vv = v_ref[hd, pl.ds(off, tk), :]
            z = lax.dot_general(q_ref[hd], kv, NT_DIMS, preferred_element_type=F32)
            sp = _softplus(z)
            if straddles is None:
                lk = -sp
                w = jnp.exp(z - sp + _ones_dot(lk, after) + c)
            else:
                before = k_pos + straddles * tk < q_pos
                lk = jnp.where(before, -sp, 0.0)
                w = jnp.where(before, jnp.exp(z - sp + _ones_dot(lk, after) + c), 0.0)
            acc = acc + jnp.dot(w.astype(MXU_DTYPE), vv, preferred_element_type=F32)
            return c + jnp.sum(lk, axis=1, keepdims=True), acc

        def step(n, carry):
            return tuple(block(hd, i * per - 1 - n, *carry[hd], None) for hd in range(hp))

        carry = tuple((jnp.zeros((tq, 1), F32), jnp.zeros((tq, dh), F32)) for _ in range(hp))
        for s in reversed(range(per)):
            carry = tuple(block(hd, i * per + s, *carry[hd], s) for hd in range(hp))
        for hd, (c, acc) in enumerate(lax.fori_loop(0, i * per, step, carry)):
            o_ref[hd] = acc
            ls_ref[hd] = c

    whole = pl.BlockSpec((hp, seq, dh), lambda h, i: (h, 0, 0))
    return _carried_call(
        body, name="sb_fwd", grid=(nh // hp, seq // tq),
        in_specs=[pl.BlockSpec((hp, tq, dh), lambda h, i: (h, i, 0)), whole, whole],
        out_specs=[pl.BlockSpec((hp, tq, dh), lambda h, i: (h, i, 0)), pl.BlockSpec((hp, tq, 1), lambda h, i: (h, i, 0))],
        out_shape=[jax.ShapeDtypeStruct((nh, seq, dh), F32), jax.ShapeDtypeStruct((nh, seq, 1), F32)],
        semantics=("parallel", "parallel"), operands=(q, k, v), job=job)


def _sb_bwd(q, k, v, lsum, do, *, tq=SB_QUERIES, job=None):
    nh, seq, dh = q.shape
    tq = min(tq, seq)
    tk = min(SB_KEYS, tq)
    per = tq // tk
    hp = SB_PACK
    scale = SB_SCALE

    def body(q_ref, k_ref, v_ref, ls_ref, do_ref, dq_ref, dk_ref, dv_ref):
        i = pl.program_id(1)

        @pl.when(i == 0)
        def _():
            dk_ref[...] = jnp.zeros_like(dk_ref)
            dv_ref[...] = jnp.zeros_like(dv_ref)

        r_idx = lax.broadcasted_iota(jnp.int32, (tk, tk), 0)
        c_idx = lax.broadcasted_iota(jnp.int32, (tk, tk), 1)
        upto = (r_idx <= c_idx).astype(MXU_DTYPE)
        before = (r_idx < c_idx).astype(MXU_DTYPE)
        q_pos = lax.broadcasted_iota(jnp.int32, (tq, tk), 0)
        k_pos = lax.broadcasted_iota(jnp.int32, (tq, tk), 1)

        def block(hd, j, cp, ce, dq, straddles):
            off = pl.multiple_of(j * tk, tk)
            qv, dov = q_ref[hd], do_ref[hd].astype(MXU_DTYPE)
            kv = k_ref[hd, pl.ds(off, tk), :]
            vv = v_ref[hd, pl.ds(off, tk), :]
            z = lax.dot_general(qv, kv, NT_DIMS, preferred_element_type=F32)
            sp = _softplus(z)
            valid = None if straddles is None else k_pos + straddles * tk < q_pos
            lk = -sp if valid is None else jnp.where(valid, -sp, 0.0)
            w = jnp.exp(z - sp + (ls_ref[hd] - cp) - _ones_dot(lk, upto))
            if valid is not None:
                w = jnp.where(valid, w, 0.0)
            e = w * lax.dot_general(dov, vv, NT_DIMS, preferred_element_type=F32)
            earlier = _ones_dot(e, before) + ce
            keep = jnp.exp(-sp)
            dz = e * keep - (1.0 - keep) * earlier
            if valid is not None:
                dz = jnp.where(valid, dz, 0.0)
            dzm = dz.astype(MXU_DTYPE)
            dq = dq + jnp.dot(dzm, kv, preferred_element_type=F32)
            dk_ref[hd, pl.ds(off, tk), :] += lax.dot_general(dzm, qv, TN_DIMS, preferred_element_type=F32)
            dv_ref[hd, pl.ds(off, tk), :] += lax.dot_general(w.astype(MXU_DTYPE), dov, TN_DIMS, preferred_element_type=F32)
            return cp + jnp.sum(lk, axis=1, keepdims=True), ce + jnp.sum(e, axis=1, keepdims=True), dq

        def step(j, carry):
            return tuple(block(hd, j, *carry[hd], None) for hd in range(hp))

        zero = jnp.zeros((tq, 1), F32)
        carry = lax.fori_loop(0, i * per, step, tuple((zero, zero, jnp.zeros((tq, dh), F32)) for _ in range(hp)))
        for s in range(per):
            carry = tuple(block(hd, i * per + s, *carry[hd], s) for hd in range(hp))
        for hd in range(hp):
            dq_ref[hd] = carry[hd][2] * scale

    whole = pl.BlockSpec((hp, seq, dh), lambda h, i: (h, 0, 0))
    tile = pl.BlockSpec((hp, tq, dh), lambda h, i: (h, i, 0))
    return _carried_call(
        body, name="sb_bwd", grid=(nh // hp, seq // tq),
        in_specs=[tile, whole, whole, pl.BlockSpec((hp, tq, 1), lambda h, i: (h, i, 0)), tile],
        out_specs=[tile, whole, whole],
        out_shape=[jax.ShapeDtypeStruct((nh, seq, dh), F32)] * 3,
        semantics=("parallel", "arbitrary"), operands=(q, k, v, lsum, do), job=job)


def _heads(t):
    return t.reshape(t.shape[0], SB_HEADS, SB_DH).transpose(1, 0, 2)


def _unheads(t):
    return t.transpose(1, 0, 2).reshape(t.shape[1], SB_HEADS * SB_DH)


def _s5_disc(lr, li, ldt, br, bi):
    dt = jnp.exp(ldt)
    mag = jnp.exp(lr * dt)
    ar = mag * jnp.cos(li * dt)
    ai = mag * jnp.sin(li * dt)
    den = lr * lr + li * li
    nr = ar - 1.0
    cr = (nr * lr + ai * li) / den
    ci = (ai * lr - nr * li) / den
    return ar, ai, cr[None] * br - ci[None] * bi, cr[None] * bi + ci[None] * br


def _s5_prep(lr, li, ldt, br, bi):
    shapes = [lr.shape, lr.shape, br.shape, br.shape]

    def body(lr_ref, li_ref, ldt_ref, br_ref, bi_ref, *outs):
        for o, val in zip(outs, _s5_disc(lr_ref[...], li_ref[...], ldt_ref[...], br_ref[...], bi_ref[...])):
            o[...] = val

    return pl.pallas_call(body, name="s5_prep", out_shape=[jax.ShapeDtypeStruct(s, F32) for s in shapes])(lr, li, ldt, br, bi)


def _s5_prep_bwd(lr, li, ldt, br, bi, cts):
    args = (lr, li, ldt, br, bi)

    def body(*refs):
        ins, ct_refs, outs = refs[:5], refs[5:9], refs[9:]
        _, vjp = jax.vjp(_s5_disc, *[r[...] for r in ins])
        for o, val in zip(outs, vjp(tuple(r[...] for r in ct_refs))):
            o[...] = val

    return pl.pallas_call(body, name="s5_prep_bwd", out_shape=[jax.ShapeDtypeStruct(a.shape, F32) for a in args])(*args, *cts)


def _s5_scan(bu, a, *, tc=512):
    seq, w2 = bu.shape
    tw = S5_BLOCK
    tc = min(tc, seq)
    assert seq % tc == 0 and w2 % (2 * tw) == 0

    def body(bu_ref, a_ref, h_ref, cr_ref, ci_ref):
        @pl.when(pl.program_id(1) == 0)
        def _():
            cr_ref[...] = jnp.zeros_like(cr_ref)
            ci_ref[...] = jnp.zeros_like(ci_ref)

        re, im = pl.ds(0, tw), pl.ds(tw, tw)
        ar, ai = a_ref[:, re], a_ref[:, im]

        def step(t, carry):
            hr, hi = carry
            row = pl.ds(t, 1)
            nr = ar * hr - ai * hi + bu_ref[row, re]
            ni = ar * hi + ai * hr + bu_ref[row, im]
            h_ref[row, re] = nr
            h_ref[row, im] = ni
            return nr, ni

        hr, hi = lax.fori_loop(0, tc, step, (cr_ref[...], ci_ref[...]), unroll=8)
        cr_ref[...] = hr
        ci_ref[...] = hi

    blk = pl.BlockSpec((tc, 2 * tw), lambda j, t: (t, j))
    return pl.pallas_call(
        body, name="s5_scan", grid=(w2 // (2 * tw), seq // tc),
        in_specs=[blk, pl.BlockSpec((1, 2 * tw), lambda j, t: (0, j))],
        out_specs=blk,
        out_shape=jax.ShapeDtypeStruct((seq, w2), F32),
        scratch_shapes=[pltpu.VMEM((1, tw), F32)] * 2,
        compiler_params=_params("parallel", "arbitrary"),
    )(bu, a)


def _s5_scan_bwd(d, h, a, *, tc=512):
    seq, w2 = d.shape
    tw = S5_BLOCK
    tc = min(tc, seq)
    assert seq % tc == 0 and w2 % (2 * tw) == 0
    nt = seq // tc

    def body(d_ref, h_ref, a_ref, g_ref, da_ref, cr_ref, ci_ref):
        @pl.when(pl.program_id(1) == 0)
        def _():
            cr_ref[...] = jnp.zeros_like(cr_ref)
            ci_ref[...] = jnp.zeros_like(ci_ref)
            da_ref[...] = jnp.zeros_like(da_ref)

        re, im = pl.ds(0, tw), pl.ds(tw, tw)
        ar, ai = a_ref[:, re], a_ref[:, im]

        def step(n, carry):
            gr, gi, sr, si = carry
            row = pl.ds(tc - 1 - n, 1)
            hrt, hit = h_ref[row, re], h_ref[row, im]
            sr = sr + gr * hrt + gi * hit
            si = si + gi * hrt - gr * hit
            ngr = d_ref[row, re] + ar * gr + ai * gi
            ngi = d_ref[row, im] + ar * gi - ai * gr
            g_ref[row, re] = ngr
            g_ref[row, im] = ngi
            return ngr, ngi, sr, si

        gr, gi, sr, si = lax.fori_loop(0, tc, step, (cr_ref[...], ci_ref[...], da_ref[:, re], da_ref[:, im]), unroll=8)
        cr_ref[...] = gr
        ci_ref[...] = gi
        da_ref[:, re] = sr
        da_ref[:, im] = si

    blk = pl.BlockSpec((tc, 2 * tw), lambda j, t: (nt - 1 - t, j))
    row = pl.BlockSpec((1, 2 * tw), lambda j, t: (0, j))
    return pl.pallas_call(
        body, name="s5_scan_bwd", grid=(w2 // (2 * tw), nt),
        in_specs=[blk, blk, row],
        out_specs=[blk, row],
        out_shape=[jax.ShapeDtypeStruct((seq, w2), F32), jax.ShapeDtypeStruct((1, w2), F32)],
        scratch_shapes=[pltpu.VMEM((1, tw), F32)] * 2,
        compiler_params=_params("parallel", "arbitrary"),
    )(d, h, a)


def _pair_columns(re, im, axis):
    shape = re.shape
    split = shape[:axis] + (shape[axis] // S5_BLOCK, S5_BLOCK) + shape[axis + 1:]
    both = jnp.stack([re.reshape(split), im.reshape(split)], axis=axis + 1)
    return both.reshape(shape[:axis] + (2 * shape[axis],) + shape[axis + 1:])


def _unpair_columns(t, axis):
    shape = t.shape
    both = t.reshape(shape[:axis] + (shape[axis] // (2 * S5_BLOCK), 2, S5_BLOCK) + shape[axis + 1:])
    half = shape[:axis] + (shape[axis] // 2,) + shape[axis + 1:]
    return (lax.index_in_dim(both, 0, axis + 1, keepdims=False).reshape(half),
            lax.index_in_dim(both, 1, axis + 1, keepdims=False).reshape(half))


def _block_diag(t):
    g, a, b = t.shape
    eye = jnp.eye(g, dtype=t.dtype)
    return (t[:, :, None, :] * eye[:, None, :, None]).reshape(g * a, g * b)


def _block_diag_part(m, g):
    a, b = m.shape[0] // g, m.shape[1] // g
    return jnp.moveaxis(jnp.diagonal(m.reshape(g, a, g, b), axis1=0, axis2=2), -1, 0)


def _gelu_glu(y, gate_pre):
    z = jax.nn.gelu(y)
    return z * jax.nn.sigmoid(gate_pre)


def _s5_fwd(u, p, w_glu):
    lr, li = p["s5_lambda_re"][0], p["s5_lambda_im"][0]
    ldt = p["s5_log_dt"][0][:, None]
    br = p["s5_b_re"][0].transpose(2, 0, 1)
    bi = p["s5_b_im"][0].transpose(2, 0, 1)
    ar, ai, bbr, bbi = _s5_prep(lr, li, ldt, br, bi)
    a = _pair_columns(ar.reshape(1, S5_LANES), ai.reshape(1, S5_LANES), 1)
    bmat = _pair_columns(_block_diag(bbr.transpose(1, 0, 2)), _block_diag(bbi.transpose(1, 0, 2)), 1)
    cmat = _pair_columns(_block_diag(p["s5_c_re"][0].transpose(0, 2, 1)),
                         -_block_diag(p["s5_c_im"][0].transpose(0, 2, 1)), 0)
    bmat, cmat = bmat.astype(MXU_DTYPE), cmat.astype(MXU_DTYPE)
    bu = _mm(u, bmat, name="s5_bu")
    h = _s5_scan(bu, a)
    d = p["s5_d"]
    y = _mm(h, cmat, name="s5_y", epilogue=lambda acc, uv, dv: acc + dv * uv, extras=[u, d])
    z = _rowmap(jax.nn.gelu, [y], "r", [(y.shape, MXU_DTYPE, "r")], name="s5_gelu", tl=512)
    gate_pre = _mm(z, w_glu, name="s5_glu")
    out = _rowmap(_gelu_glu, [y, gate_pre], "rr", [(y.shape, F32, "r")], name="s5_gate", tl=512)
    return out, (u, lr, li, ldt, br, bi, a, bmat, cmat, h, y, z, gate_pre)


def _s5_bwd(dout, saved, p, w_glu):
    u, lr, li, ldt, br, bi, a, bmat, cmat, h, y, z, gate_pre = saved
    d = p["s5_d"]

    def gate_bwd(dov, yv, gv):
        zv = jax.nn.gelu(yv)
        sg = jax.nn.sigmoid(gv)
        return dov * sg, dov * zv * sg * (1.0 - sg)

    dz_direct, dgate = _rowmap(gate_bwd, [dout, y, gate_pre], "rrr", [(y.shape, F32, "r"), (y.shape, MXU_DTYPE, "r")],
                               name="s5_gate_bwd", tl=512)
    dw_glu = _mm(z, dgate, ta=True, name="s5_dwglu", out_dtype=WIRE_DTYPE)
    dz = _mm(dgate, w_glu, tb=True, name="s5_dz", epilogue=lambda acc, prev: acc + prev, extras=[dz_direct])

    def gelu_bwd(dzv, yv, uv, dvv):
        _, vjp = jax.vjp(jax.nn.gelu, yv)
        dy = vjp(dzv)[0]
        return dy, dy * dvv, jnp.sum(dy * uv, axis=0, keepdims=True)

    dy, du_skip, dd = _rowmap(gelu_bwd, [dz, y, u, d], "rrrc",
                              [(y.shape, F32, "r"), (y.shape, F32, "r"), (d.shape, F32, "a")], name="s5_gelu_bwd", tl=512)
    dcmat = _mm(h, dy, ta=True, name="s5_dc")
    dstate = _mm(dy, cmat, tb=True, name="s5_dstate")
    g, da = _s5_scan_bwd(dstate, h, a)
    du = _mm(g, bmat, tb=True, name="s5_du", epilogue=lambda acc, prev: acc + prev, extras=[du_skip], out_dtype=MXU_DTYPE)
    dbmat = _mm(u, g, ta=True, name="s5_db")
    dbbr, dbbi = (_block_diag_part(t, S5_GROUPS).transpose(1, 0, 2) for t in _unpair_columns(dbmat, 1))
    dar, dai = _unpair_columns(da, 1)
    cts = (dar.reshape(S5_GROUPS, S5_STATE), dai.reshape(S5_GROUPS, S5_STATE), dbbr, dbbi)
    dlr, dli, dldt, dbr, dbi = _s5_prep_bwd(lr, li, ldt, br, bi, cts)
    dcr, dci = (_block_diag_part(t, S5_GROUPS).transpose(0, 2, 1) for t in _unpair_columns(dcmat, 0))
    grads = {
        "s5_lambda_re": dlr[None], "s5_lambda_im": dli[None], "s5_log_dt": dldt[:, 0][None],
        "s5_b_re": dbr.transpose(1, 2, 0)[None], "s5_b_im": dbi.transpose(1, 2, 0)[None],
        "s5_c_re": dcr[None], "s5_c_im": -dci[None], "s5_d": dd,
    }
    return du, dw_glu, grads


def _mix0_fwd(x, g, p, full, late):
    h = _norm(x, g, name="mix0_norm")
    proj = _mm(h, full[("ab_w_in", 0)], name="mix0_in")
    u = proj[:, :S5_WIDTH]
    q, k, v = (_heads(proj[:, S5_WIDTH * (1 + n):S5_WIDTH * (2 + n)] * (SB_SCALE if n == 0 else 1.0)).astype(MXU_DTYPE)
               for n in range(3))
    job, keys = late.gather_job("sb_fwd") if late else (None, [])
    (o, lsum), got = _sb_fwd(q, k, v, job=job)
    full.update(zip(keys, got))
    w_glu, w_out = full[("s5_w_glu", 0)], full[("ab_w_out", 0)]
    y_a, s5_saved = _s5_fwd(u, p, w_glu)
    mix = jnp.concatenate([y_a, _unheads(o)], axis=1).astype(MXU_DTYPE)
    x2 = _mm(mix, w_out, name="mix0_out", epilogue=lambda acc, xv: xv + acc, extras=[x])
    return x2, (x, h, q, k, v, lsum, mix, s5_saved)


def _mix0_bwd(dx2, saved, g, p, full, grads, late):
    x, h, q, k, v, lsum, mix, s5_saved = saved
    w_in, w_glu, w_out = full[("ab_w_in", 0)], full[("s5_w_glu", 0)], full[("ab_w_out", 0)]
    dmix = _mm(dx2, w_out, tb=True, name="mix0_dmix")
    grads[("ab_w_out", 0)] = _mm(mix, dx2, ta=True, name="mix0_dwout", out_dtype=WIRE_DTYPE)
    du, grads[("s5_w_glu", 0)], s5_grads = _s5_bwd(dmix[:, :S5_WIDTH], s5_saved, p, w_glu)
    job, keys = late.scatter_job(grads) if late else (None, [])
    (dq, dk, dv), got = _sb_bwd(q, k, v, lsum, _heads(dmix[:, S5_WIDTH:]), job=job)
    _note(late, keys, got)
    dproj = jnp.concatenate([du] + [_unheads(t).astype(MXU_DTYPE) for t in (dq, dk, dv)], axis=1)
    grads[("ab_w_in", 0)] = _mm(h, dproj, ta=True, name="mix0_dwin", out_dtype=WIRE_DTYPE)
    job, keys = late.scatter_job(grads) if late else (None, [])
    dh, got = _carried(_mm, dproj, w_in, tb=True, name="mix0_dh", job=job)
    _note(late, keys, got)
    dx, dg = _norm_bwd(dh, x, g, dx2, name="mix0_norm_bwd")
    return dx, dg, s5_grads


def _shift_down(t, n):
    rows = lax.broadcasted_iota(jnp.int32, t.shape, 0)
    return jnp.where(rows >= n, pltpu.roll(t, n, 0), 0.0)


def _shift_up(t, n):
    rows = lax.broadcasted_iota(jnp.int32, t.shape, 0)
    return jnp.where(rows < t.shape[0] - n, pltpu.roll(t, t.shape[0] - n, 0), 0.0)


def _conv_fwd(proj, cw, *, tc=128):
    seq, c3 = proj.shape
    ch = c3 // 3
    nb = ch // tc

    def body(b_ref, c_ref, v_ref, w_ref, m_ref):
        pv = c_ref[...] * v_ref[...]
        w = w_ref[...]
        y = w[2:3] * pv + w[1:2] * _shift_down(pv, 1) + w[0:1] * _shift_down(pv, 2)
        m_ref[...] = (b_ref[...] * y).astype(m_ref.dtype)

    col = lambda part: pl.BlockSpec((seq, tc), lambda j: (0, part * nb + j))
    return pl.pallas_call(
        body, name="conv_fwd", grid=(nb,),
        in_specs=[col(0), col(1), col(2), pl.BlockSpec((3, tc), lambda j: (0, j))],
        out_specs=pl.BlockSpec((seq, tc), lambda j: (0, j)),
        out_shape=jax.ShapeDtypeStruct((seq, ch), MXU_DTYPE),
        compiler_params=_params("parallel"),
    )(proj, proj, proj, cw)


def _conv_bwd(proj, cw, dm, *, tc=128):
    seq, c3 = proj.shape
    ch = c3 // 3
    nb = ch // tc

    def body(b_ref, c_ref, v_ref, w_ref, dm_ref, dproj_ref, dw_ref, dc_ref, dv_ref):
        part = pl.program_id(1)

        @pl.when(part == 0)
        def _():
            cv, vv, dmv = c_ref[...], v_ref[...], dm_ref[...]
            pv = cv * vv
            w = w_ref[...]
            p1, p2 = _shift_down(pv, 1), _shift_down(pv, 2)
            y = w[2:3] * pv + w[1:2] * p1 + w[0:1] * p2
            dproj_ref[...] = (dmv * y).astype(dproj_ref.dtype)
            dy = dmv * b_ref[...]
            dp = w[2:3] * dy + w[1:2] * _shift_up(dy, 1) + w[0:1] * _shift_up(dy, 2)
            dc_ref[...] = (dp * vv).astype(dc_ref.dtype)
            dv_ref[...] = (dp * cv).astype(dv_ref.dtype)
            dw_ref[...] = jnp.concatenate([jnp.sum(dy * p2, axis=0, keepdims=True), jnp.sum(dy * p1, axis=0, keepdims=True),
                                           jnp.sum(dy * pv, axis=0, keepdims=True)], axis=0)

        @pl.when(part == 1)
        def _():
            dproj_ref[...] = dc_ref[...]

        @pl.when(part == 2)
        def _():
            dproj_ref[...] = dv_ref[...]

    col = lambda part: pl.BlockSpec((seq, tc), lambda j, t: (0, part * nb + j))
    small = pl.BlockSpec((3, tc), lambda j, t: (0, j))
    return pl.pallas_call(
        body, name="conv_bwd", grid=(nb, 3),
        in_specs=[col(0), col(1), col(2), small, pl.BlockSpec((seq, tc), lambda j, t: (0, j))],
        out_specs=[pl.BlockSpec((seq, tc), lambda j, t: (0, t * nb + j)), small],
        out_shape=[jax.ShapeDtypeStruct((seq, c3), MXU_DTYPE), jax.ShapeDtypeStruct((3, ch), F32)],
        scratch_shapes=[pltpu.VMEM((seq, tc), MXU_DTYPE)] * 2,
        compiler_params=_params("parallel", "arbitrary"),
    )(proj, proj, proj, cw, dm)


def _mix1_fwd(x, g, full, late):
    h = _norm(x, g, name="mix1_norm")
    job, keys = late.gather_job("mix1_in") if late else (None, [])
    proj, got = _carried(_mm, h, full[("sc_w_in", 0)], name="mix1_in", job=job)
    full.update(zip(keys, got))
    m = _conv_fwd(proj, full[("sc_conv_w", 0)])
    x2 = _mm(m, full[("sc_w_out", 0)], name="mix1_out", epilogue=lambda acc, xv: xv + acc, extras=[x])
    return x2, (x, h, proj, m)


def _mix1_bwd(dx2, saved, g, w_in, cw, w_out):
    x, h, proj, m = saved
    dm = _mm(dx2, w_out, tb=True, name="mix1_dm")
    dw_out = _mm(m, dx2, ta=True, name="mix1_dwout", out_dtype=WIRE_DTYPE)
    dproj, dcw = _conv_bwd(proj, cw, dm)
    dw_in = _mm(h, dproj, ta=True, name="mix1_dwin", out_dtype=WIRE_DTYPE)
    dh = _mm(dproj, w_in, tb=True, name="mix1_dh")
    dx, dg = _norm_bwd(dh, x, g, dx2, name="mix1_norm_bwd")
    return dx, dg, dw_in, dcw, dw_out


def _loss_head(x, g, target):
    feat = x.shape[1]

    def fn(xv, gv, tv):
        err = _rms_fwd(xv, gv) - tv
        dx, dg = _rms_bwd(err / feat, xv, gv)
        return jnp.sum(err * err, keepdims=True) * (0.5 / feat), dx, dg

    return _rowmap(fn, [x, g, target], "rcr", [((1, 1), F32, "a"), (x.shape, F32, "r"), (g.shape, F32, "a")],
                   name="loss_head", tl=256)


def _slot(ref, place, chip=None, half=None):
    axis, width = place
    shape = list(ref.shape)
    start = [0, 0]
    if chip is not None:
        start[axis], shape[axis] = chip * width, width
    if half is not None:
        h_axis = 0 if shape[0] % 32 == 0 else 1
        shape[h_axis] //= 2
        start[h_axis] = start[h_axis] + half * shape[h_axis]
    hint = lambda s, d: s if isinstance(s, int) else pl.multiple_of(s, 128 if d == 1 else 8)
    return ref.at[tuple(pl.ds(hint(s, d), n) for d, (s, n) in enumerate(zip(start, shape)))]


class _Exchange:
    def __init__(self, kind, arrays, places):
        self.kind, self.arrays, self.places, self.n = kind, list(arrays), list(places), len(arrays)
        self.out_shape = []
        for t, (axis, width) in zip(self.arrays, self.places):
            if kind == "gather":
                shape = list(t.shape)
                shape[axis] = N_CHIPS * width
            else:
                shape = [N_CHIPS] + list(t.shape)
                shape[1 + axis] = width
            self.out_shape.append(jax.ShapeDtypeStruct(tuple(shape), t.dtype))
        n = self.n
        self.scratch = [pltpu.SemaphoreType.DMA((3 * n,)) for _ in range(4 if kind == "gather" else 2)]
        self.scratch.append(pltpu.SemaphoreType.DMA((n,)))

    def _copies(self, ins, outs, sems):
        x, y, c = lax.axis_index("x"), lax.axis_index("y"), lax.axis_index("c")
        peers = [(1 - x, y), (x, 1 - y), (1 - x, 1 - y)]
        remote = lambda src, dst, send, recv, k, to: pltpu.make_async_remote_copy(
            src_ref=src, dst_ref=dst, send_sem=send.at[k], recv_sem=recv.at[k], device_id=to, device_id_type=MESH_ID)
        local, ici, d2d = [], [], []
        for a in range(self.n):
            place = self.places[a]
            if self.kind == "gather":
                local.append(pltpu.make_async_copy(ins[a], _slot(outs[a], place, 2 * x + y), sems[4].at[a]))
                for r, (px, py) in enumerate(peers):
                    ici.append(remote(_slot(ins[a], place, None, c), _slot(outs[a], place, 2 * x + y, c),
                                      sems[0], sems[1], 3 * a + r, (px, py, c)))
                    landed = _slot(outs[a], place, 2 * px + py, c)
                    d2d.append(remote(landed, landed, sems[2], sems[3], 3 * a + r, (x, y, 1 - c)))
            else:
                local.append(pltpu.make_async_copy(_slot(ins[a], place, 2 * x + y), outs[a].at[3], sems[2].at[a]))
                for r, (px, py) in enumerate(peers):
                    ici.append(remote(_slot(ins[a], place, 2 * px + py), outs[a].at[r], sems[0], sems[1], 3 * a + r, (px, py, c)))
        return local, ici, d2d

    def start(self, ins, outs, sems):
        local, ici, _ = self._copies(ins, outs, sems)
        for cp in local + ici:
            cp.start()

    def relay(self, ins, outs, sems):
        _, ici, d2d = self._copies(ins, outs, sems)
        for arrived, onward in zip(ici, d2d):
            arrived.wait_recv()
            onward.start()

    def finish(self, ins, outs, sems):
        local, ici, d2d = self._copies(ins, outs, sems)
        for cp in local + d2d:
            cp.wait()
        for cp in ici:
            cp.wait_send() if d2d else cp.wait()


def _exchange_call(job, name):
    n = job.n

    def body(*refs):
        ins, outs, sems = refs[:n], refs[n:2 * n], refs[2 * n:]
        job.start(ins, outs, sems)
        job.relay(ins, outs, sems)
        job.finish(ins, outs, sems)

    return pl.pallas_call(
        body, name=name, in_specs=[ANY_SPEC] * n, out_specs=[ANY_SPEC] * n, out_shape=job.out_shape,
        scratch_shapes=job.scratch, compiler_params=pltpu.CompilerParams(has_side_effects=True),
    )(*job.arrays)


def _carried_call(body, *, name, grid, in_specs, out_specs, out_shape, semantics, operands, scratch_shapes=(), job=None):
    scratch_shapes = list(scratch_shapes)
    if job is None:
        return pl.pallas_call(body, name=name, grid=grid, in_specs=in_specs, out_specs=out_specs, out_shape=out_shape,
                              scratch_shapes=scratch_shapes, compiler_params=_params(*semantics))(*operands), []
    n_in, n_out, n, n_scr = len(in_specs), len(out_specs), job.n, len(scratch_shapes)
    steps = math.prod(grid)

    def wrapped(*refs):
        ins, job_ins = refs[:n_in], refs[n_in:n_in + n]
        outs, job_outs = refs[n_in + n:n_in + n + n_out], refs[n_in + n + n_out:n_in + 2 * n + n_out]
        outs = outs + refs[n_in + 2 * n + n_out:n_in + 2 * n + n_out + n_scr]
        sems = refs[n_in + 2 * n + n_out + n_scr:]
        step = functools.reduce(lambda acc, d: acc * grid[d] + pl.program_id(d), range(len(grid)), 0)

        @pl.when(step == 0)
        def _():
            job.start(job_ins, job_outs, sems)

        @pl.when(step == (3 * steps) // 4)
        def _():
            job.relay(job_ins, job_outs, sems)

        body(*ins, *outs)

        @pl.when(step == steps - 1)
        def _():
            job.finish(job_ins, job_outs, sems)

    res = pl.pallas_call(
        wrapped, name=name, grid=grid, in_specs=list(in_specs) + [ANY_SPEC] * n, out_specs=list(out_specs) + [ANY_SPEC] * n,
        out_shape=list(out_shape) + job.out_shape, scratch_shapes=scratch_shapes + job.scratch,
        compiler_params=pltpu.CompilerParams(dimension_semantics=("arbitrary",) * len(grid), vmem_limit_bytes=VMEM_LIMIT,
                                             has_side_effects=True),
    )(*operands, *job.arrays)
    return res[:n_out], res[n_out:]


def _swap_with_sibling(parts):
    n = len(parts)

    def body(*refs):
        ins, outs = refs[:n], refs[n:2 * n]
        send, recv = refs[2 * n:]
        sibling = (lax.axis_index("x"), lax.axis_index("y"), 1 - lax.axis_index("c"))
        copies = [pltpu.make_async_remote_copy(src_ref=ins[a], dst_ref=outs[a], send_sem=send.at[a], recv_sem=recv.at[a],
                                               device_id=sibling, device_id_type=MESH_ID) for a in range(n)]
        for cp in copies:
            cp.start()
        for cp in copies:
            cp.wait()

    return pl.pallas_call(
        body, name="swap_with_sibling",
        in_specs=[ANY_SPEC] * n, out_specs=[ANY_SPEC] * n,
        out_shape=[jax.ShapeDtypeStruct(p.shape, p.dtype) for p in parts],
        scratch_shapes=[pltpu.SemaphoreType.DMA((n,)), pltpu.SemaphoreType.DMA((n,))],
        compiler_params=pltpu.CompilerParams(has_side_effects=True),
    )(*parts)


def _sum_all_devices(t):
    rows = t.shape[0]

    def body(t_ref, o_ref, slots, send, recv):
        x, y, c = lax.axis_index("x"), lax.axis_index("y"), lax.axis_index("c")
        me = 4 * x + 2 * y + c
        slots[me] = t_ref[...]
        copies = []
        for m in range(1, 8):
            peer = (x ^ (m >> 2), y ^ ((m >> 1) & 1), c ^ (m & 1))
            cp = pltpu.make_async_remote_copy(src_ref=t_ref, dst_ref=slots.at[me], send_sem=send.at[m - 1],
                                              recv_sem=recv.at[m - 1], device_id=peer, device_id_type=MESH_ID)
            cp.start()
            copies.append(cp)
        for cp in copies:
            cp.wait()
        acc = slots[0]
        for dev in range(1, 8):
            acc = acc + slots[dev]
        o_ref[...] = acc

    vmem = pl.BlockSpec(memory_space=pltpu.VMEM)
    return pl.pallas_call(
        body, name="sum_all_devices", in_specs=[vmem], out_specs=vmem,
        out_shape=jax.ShapeDtypeStruct(t.shape, F32),
        scratch_shapes=[pltpu.VMEM((8, rows, 128), F32), pltpu.SemaphoreType.DMA((7,)), pltpu.SemaphoreType.DMA((7,))],
        compiler_params=pltpu.CompilerParams(vmem_limit_bytes=VMEM_LIMIT, has_side_effects=True),
    )(t)


def _adamw(w, g, m, v):
    m = ADAM_B1 * m + (1.0 - ADAM_B1) * g
    v = ADAM_B2 * v + (1.0 - ADAM_B2) * jnp.square(g)
    m_hat = m / (1.0 - ADAM_B1 ** ADAM_STEP)
    v_hat = v / (1.0 - ADAM_B2 ** ADAM_STEP)
    return -ADAM_LR * (m_hat / (jnp.sqrt(v_hat) + ADAM_EPS) + ADAM_WD * w), m, v


def _chip_sum(received, name):
    rows, cols = received.shape[1:]
    tl = _row_block(rows, 512)

    def body(r_ref, o_ref):
        o_ref[...] = ((r_ref[0].astype(F32) + r_ref[1].astype(F32)) + r_ref[2].astype(F32)) + r_ref[3].astype(F32)

    return pl.pallas_call(body, name=name, grid=(rows // tl,),
                          in_specs=[pl.BlockSpec((N_CHIPS, tl, cols), lambda i: (0, i, 0))],
                          out_specs=pl.BlockSpec((tl, cols), lambda i: (i, 0)),
                          out_shape=jax.ShapeDtypeStruct((rows, cols), F32), compiler_params=_params("parallel"))(received)


def _adamw_layer(w, m, v, p_mine, p_other, layer, prev, name):
    _, rows, cols = w.shape
    tl = _row_block(rows, 512)

    def body(w_ref, m_ref, v_ref, pa_ref, pb_ref, *rest):
        g = pa_ref[:, :cols] + pb_ref[:, :cols]
        for o_ref, val in zip(rest[-4:], (g,) + _adamw(w_ref[...], g, m_ref[...], v_ref[...])):
            o_ref[...] = val

    stacked = pl.BlockSpec((None, tl, cols), lambda i: (layer, i, 0))
    part = pl.BlockSpec((tl, p_mine.shape[1]), lambda i: (i, 0))
    kept = list(prev) if prev else []
    return pl.pallas_call(
        body, name=name, grid=(rows // tl,),
        in_specs=[stacked] * 3 + [part] * 2 + [ANY_SPEC] * len(kept),
        out_specs=[stacked] * 4, out_shape=[jax.ShapeDtypeStruct(w.shape, F32)] * 4,
        input_output_aliases={5 + k: k for k in range(len(kept))},
        compiler_params=_params("parallel"),
    )(w, m, v, p_mine, p_other, *kept)


def _adamw_small(w, g, m, v):
    def fn(wv, gv, mv, vv):
        return _adamw(wv, gv, mv, vv)

    return _rowmap(fn, [w, g, m, v], "rrrr", [(w.shape, F32, "r")] * 3, name="adamw_small", tl=w.shape[0])


WEIGHTS = ['ffn1_norm', 'ffn1_w_gate', 'ffn1_w_up', 'ffn1_w_down', 'mix_norm', 'ffn2_norm', 'ffn2_w_gate', 'ffn2_w_up',
           'ffn2_w_down', 'ab_w_in', 's5_lambda_re', 's5_lambda_im', 's5_log_dt', 's5_b_re', 's5_b_im', 's5_c_re', 's5_c_im',
           's5_d', 's5_w_glu', 'ab_w_out', 'sc_w_in', 'sc_conv_w', 'sc_w_out', 'final_norm']
SHARDED = {'ffn1_w_gate': (0, FF_SLOT), 'ffn1_w_up': (0, FF_SLOT), 'ffn1_w_down': (0, FF_SLOT),
           'ffn2_w_gate': (0, FF_SLOT), 'ffn2_w_up': (0, FF_SLOT), 'ffn2_w_down': (0, FF_SLOT),
           'ab_w_in': (1, 512), 's5_w_glu': (0, 128), 'ab_w_out': (0, 256), 'sc_w_in': (1, 768), 'sc_conv_w': (1, 256),
           'sc_w_out': (0, 256)}
SWAPPED = ('ffn1_w_gate', 'ffn1_w_up', 'ffn2_w_gate', 'ffn2_w_up')
SMALL = [n for n in WEIGHTS if n not in SHARDED]


def _held(name, t):
    return jnp.swapaxes(t, 1, 2) if name in SWAPPED else t


def _pack(arrays):
    rows = []
    for t in arrays:
        flat = t.reshape(-1)
        rows.append(jnp.pad(flat, (0, (-flat.shape[0]) % 128)))
    flat = jnp.concatenate(rows)
    return jnp.pad(flat, (0, (-flat.shape[0]) % 1024)).reshape(-1, 128)


def _unpack(packed, like):
    flat, out, pos = packed.reshape(-1), [], 0
    for t in like:
        out.append(flat[pos:pos + t.size].reshape(t.shape))
        pos += t.size + (-t.size) % 128
    return out


def _local_grads(x, target, p, full, late=None):
    small, grads, saved = {}, {}, []
    for layer in range(2):
        x, s1 = _ffn_fwd(x, p["ffn1_norm"][layer:layer + 1], full, "ffn1", layer, late)
        if layer == 0:
            x, sm = _mix0_fwd(x, p["mix_norm"][0:1], p, full, late)
        else:
            x, sm = _mix1_fwd(x, p["mix_norm"][1:2], full, late)
        x, s2 = _ffn_fwd(x, p["ffn2_norm"][layer:layer + 1], full, "ffn2", layer, late)
        saved.append((s1, sm, s2))
    loss, dx, dg_final = _loss_head(x, p["final_norm"][None], target)
    small["final_norm"] = dg_final[0]
    gains = {n: [None, None] for n in ("ffn1_norm", "mix_norm", "ffn2_norm")}

    def ffn_bwd(which, layer, dx, s):
        dx, dg = _ffn_bwd(dx, s, p[f"{which}_norm"][layer:layer + 1], full, which, layer, grads, late,
                          inline=(which, layer) in (("ffn2", 1), ("ffn1", 0)))
        gains[f"{which}_norm"][layer] = dg[0]
        return dx

    for layer in (1, 0):
        s1, sm, s2 = saved[layer]
        dx = ffn_bwd("ffn2", layer, dx, s2)
        if layer == 0:
            dx, dg, s5_grads = _mix0_bwd(dx, sm, p["mix_norm"][0:1], p, full, grads, late)
            small.update(s5_grads)
        else:
            dx, dg, dw_in, dcw, dw_out = _mix1_bwd(dx, sm, p["mix_norm"][1:2], full[("sc_w_in", 0)], full[("sc_conv_w", 0)],
                                                   full[("sc_w_out", 0)])
            grads.update({("sc_w_in", 0): dw_in, ("sc_conv_w", 0): dcw.astype(WIRE_DTYPE), ("sc_w_out", 0): dw_out})
        gains["mix_norm"][layer] = dg[0]
        dx = ffn_bwd("ffn1", layer, dx, s1)
    small.update({n: jnp.stack(pair) for n, pair in gains.items()})
    return loss, dx, small, grads


_GATHER_PLAN = {
    "gather_early": [("ffn1_w_gate", 0), ("ffn1_w_up", 0)],
    "ffn1_0_up": [("ffn1_w_down", 0), ("ab_w_in", 0)],
    "sb_fwd": [("s5_w_glu", 0), ("ab_w_out", 0), ("ffn2_w_gate", 0), ("ffn2_w_up", 0), ("ffn2_w_down", 0),
               ("ffn1_w_gate", 1), ("ffn1_w_up", 1), ("ffn1_w_down", 1)],
    "ffn2_0_up": [("sc_w_in", 0), ("sc_conv_w", 0), ("sc_w_out", 0)],
    "ffn1_1_up": [("ffn2_w_gate", 1), ("ffn2_w_up", 1)],
    "mix1_in": [("ffn2_w_down", 1)],
}


class _Late:
    def __init__(self, shards, places):
        self.shards, self.places = shards, places
        self.sent, self.received = set(), {}

    def gather_job(self, carrier):
        keys = _GATHER_PLAN.get(carrier, [])
        if not keys:
            return None, []
        return _Exchange("gather", [self.shards[k] for k in keys], [self.places[k] for k in keys]), keys

    def scatter_job(self, grads):
        keys = [k for k in grads if k not in self.sent]
        if not keys:
            return None, []
        self.sent.update(keys)
        return _Exchange("scatter", [grads[k] for k in keys], [self.places[k] for k in keys]), keys


def kernel(x, ffn1_norm, ffn1_w_gate, ffn1_w_up, ffn1_w_down, mix_norm, ffn2_norm, ffn2_w_gate, ffn2_w_up, ffn2_w_down, ab_w_in, s5_lambda_re, s5_lambda_im, s5_log_dt, s5_b_re, s5_b_im, s5_c_re, s5_c_im, s5_d, s5_w_glu, ab_w_out, sc_w_in, sc_conv_w, sc_w_out, final_norm, loss_target, m_ffn1_norm, m_ffn1_w_gate, m_ffn1_w_up, m_ffn1_w_down, m_mix_norm, m_ffn2_norm, m_ffn2_w_gate, m_ffn2_w_up, m_ffn2_w_down, m_ab_w_in, m_s5_lambda_re, m_s5_lambda_im, m_s5_log_dt, m_s5_b_re, m_s5_b_im, m_s5_c_re, m_s5_c_im, m_s5_d, m_s5_w_glu, m_ab_w_out, m_sc_w_in, m_sc_conv_w, m_sc_w_out, m_final_norm, v_ffn1_norm, v_ffn1_w_gate, v_ffn1_w_up, v_ffn1_w_down, v_mix_norm, v_ffn2_norm, v_ffn2_w_gate, v_ffn2_w_up, v_ffn2_w_down, v_ab_w_in, v_s5_lambda_re, v_s5_lambda_im, v_s5_log_dt, v_s5_b_re, v_s5_b_im, v_s5_c_re, v_s5_c_im, v_s5_d, v_s5_w_glu, v_ab_w_out, v_sc_w_in, v_sc_conv_w, v_sc_w_out, v_final_norm):
    args = dict(locals())
    p = {n: _held(n, args[n]) for n in WEIGHTS}
    mom = {n: _held(n, args["m_" + n]) for n in WEIGHTS}
    var = {n: _held(n, args["v_" + n]) for n in WEIGHTS}

    keys = [(n, layer) for n in SHARDED for layer in range(p[n].shape[0])]
    shards, places = {}, {}
    for n, layer in keys:
        axis, width = SHARDED[n]
        t = p[n][layer] if n == "sc_conv_w" else p[n][layer].astype(MXU_DTYPE)
        pad = [(0, 0), (0, 0)]
        pad[axis] = (0, width - t.shape[axis])
        shards[(n, layer)], places[(n, layer)] = jnp.pad(t, pad), (axis, width)
    late = _Late(shards, places)
    job, first = late.gather_job("gather_early")
    full = dict(zip(first, _exchange_call(job, "gather_early")))

    loss, dx, small, grads = _local_grads(x[0], loss_target[0], p, full, late)
    loss = lax.psum(loss[0, 0], ("x", "y", "c"))
    assert set(late.received) == set(keys), "a gradient was left without a carrier"

    partial = [_chip_sum(late.received[(n, layer)], name=f"chip_sum_{n}_{layer}") for n, layer in keys]
    other = _swap_with_sibling(partial)
    out = {}
    for (n, layer), mine, theirs in zip(keys, partial, other):
        out[n] = _adamw_layer(p[n], mom[n], var[n], mine, theirs, layer, out.get(n), name=f"adamw_{n}_{layer}")
    out = {n: [_held(n, t) for t in res] for n, res in out.items()}

    like = [p[n] for n in SMALL]
    g_small = _sum_all_devices(_pack([small[n] for n in SMALL]))
    d_small, m_small, v_small = _adamw_small(_pack(like), g_small, _pack([mom[n] for n in SMALL]), _pack([var[n] for n in SMALL]))
    for k, packed in enumerate((g_small, d_small, m_small, v_small)):
        for n, t in zip(SMALL, _unpack(packed, like)):
            out.setdefault(n, [None] * 4)[k] = t

    return (loss, dx[None], *[out[n][0] for n in WEIGHTS], *[out[n][1] for n in WEIGHTS],
            *[out[n][2] for n in WEIGHTS], *[out[n][3] for n in WEIGHTS])
```

```python
import functools
import math

import jax
import jax.numpy as jnp
from jax import lax
from jax.experimental import pallas as pl
from jax.experimental.pallas import tpu as pltpu

F32 = jnp.float32
MXU_DTYPE = jnp.bfloat16
WIRE_DTYPE = jnp.bfloat16
MESH_ID = pl.DeviceIdType.MESH

D_MODEL = 1024
D_FF = 2752
N_CHIPS = 4
FF_SHARD = D_FF // N_CHIPS
FF_SLOT = 768
FF_PAD = N_CHIPS * FF_SLOT
S5_WIDTH = 512
S5_GROUP = 16
S5_GROUPS = 32
S5_STATE = 64
S5_LANES = S5_GROUPS * S5_STATE
S5_BLOCK = 512
SB_HEADS = 8
SB_DH = 64
SB_SCALE = 0.125
SB_PACK = 2
SB_QUERIES = 1024
SB_KEYS = 256
EPS = 1e-6
ADAM_LR, ADAM_B1, ADAM_B2, ADAM_EPS, ADAM_WD, ADAM_STEP = 0.001, 0.9, 0.999, 1e-08, 0.01, 10
VMEM_LIMIT = 56 * 1024 * 1024

ANY_SPEC = pl.BlockSpec(memory_space=pl.ANY)


def _params(*sem):
    return pltpu.CompilerParams(dimension_semantics=sem or None, vmem_limit_bytes=VMEM_LIMIT)


def _mm(a, b, *, name, ta=False, tb=False, out_dtype=F32, epilogue=None, extras=(), tm=1024, tn=1024, tk=1024, job=None):
    m, k = (a.shape[1], a.shape[0]) if ta else a.shape
    n = b.shape[0] if tb else b.shape[1]
    tm, tn, tk = min(tm, m), min(tn, n), min(tk, k)
    assert m % tm == 0 and n % tn == 0 and k % tk == 0, (name, m, n, k)
    nk = k // tk
    a_spec = pl.BlockSpec((tk, tm), lambda i, j, kk: (kk, i)) if ta else pl.BlockSpec((tm, tk), lambda i, j, kk: (i, kk))
    b_spec = pl.BlockSpec((tn, tk), lambda i, j, kk: (j, kk)) if tb else pl.BlockSpec((tk, tn), lambda i, j, kk: (kk, j))
    ex_specs = []
    for e in extras:
        if e.shape == (m, n):
            ex_specs.append(pl.BlockSpec((tm, tn), lambda i, j, kk: (i, j)))
        elif e.shape == (1, n):
            ex_specs.append(pl.BlockSpec((1, tn), lambda i, j, kk: (0, j)))
        else:
            assert e.shape == (m, 1), (name, e.shape)
            ex_specs.append(pl.BlockSpec((tm, 1), lambda i, j, kk: (i, 0)))
    dims = (((0 if ta else 1,), (1 if tb else 0,)), ((), ()))
    n_ex = len(extras)

    out_dtypes = list(out_dtype) if isinstance(out_dtype, (list, tuple)) else [out_dtype]
    n_out = len(out_dtypes)

    def body(a_ref, b_ref, *rest):
        ex_refs, o_refs = rest[:n_ex], rest[n_ex:n_ex + n_out]

        def product():
            return lax.dot_general(a_ref[...].astype(MXU_DTYPE), b_ref[...].astype(MXU_DTYPE), dims, preferred_element_type=F32)

        def finish(r):
            if epilogue is not None:
                r = epilogue(r, *[e[...] for e in ex_refs])
            for o_ref, val in zip(o_refs, r if isinstance(r, (tuple, list)) else (r,)):
                o_ref[...] = val.astype(o_ref.dtype)

        if nk == 1:
            finish(product())
            return
        acc_ref, kk = rest[n_ex + n_out], pl.program_id(2)

        @pl.when(kk == 0)
        def _():
            acc_ref[...] = jnp.zeros_like(acc_ref)

        acc_ref[...] += product()

        @pl.when(kk == nk - 1)
        def _():
            finish(acc_ref[...])

    res, got = _carried_call(
        body, name=name, grid=(m // tm, n // tn, nk),
        in_specs=[a_spec, b_spec, *ex_specs],
        out_specs=[pl.BlockSpec((tm, tn), lambda i, j, kk: (i, j))] * n_out,
        out_shape=[jax.ShapeDtypeStruct((m, n), dt) for dt in out_dtypes],
        scratch_shapes=[pltpu.VMEM((tm, tn), F32)] if nk > 1 else [],
        semantics=("parallel", "parallel", "arbitrary"), operands=(a, b, *extras), job=job)
    res = res if isinstance(out_dtype, (list, tuple)) else res[0]
    return res if job is None else (res, got)


def _row_block(rows, want):
    for tl in range(min(want, rows), 7, -1):
        if rows % tl == 0 and tl % 8 == 0:
            return tl
    return rows


def _rowmap(fn, ins, in_kinds, outs, *, name, tl):
    rows = next(x.shape[0] for x, kd in zip(ins, in_kinds) if kd == "r")
    tl = _row_block(rows, tl)
    n_in = len(ins)

    def spec(shape, kind):
        if kind == "r":
            return pl.BlockSpec((tl,) + tuple(shape[1:]), lambda i: (i,) + (0,) * (len(shape) - 1))
        return pl.BlockSpec(tuple(shape), lambda i: (0,) * len(shape))

    def body(*refs):
        in_refs, out_refs = refs[:n_in], refs[n_in:]
        res = fn(*[r[...] for r in in_refs])
        if not isinstance(res, (tuple, list)):
            res = (res,)
        for o_ref, val, (_, dt, kind) in zip(out_refs, res, outs):
            if kind == "r":
                o_ref[...] = val.astype(dt)
            else:
                @pl.when(pl.program_id(0) == 0)
                def _():
                    o_ref[...] = jnp.zeros_like(o_ref)

                o_ref[...] += val.astype(dt)

    has_acc = any(kd == "a" for _, _, kd in outs)
    res = pl.pallas_call(
        body, name=name, grid=(rows // tl,),
        in_specs=[spec(x.shape, kd) for x, kd in zip(ins, in_kinds)],
        out_specs=[spec(s, kd) for s, _, kd in outs],
        out_shape=[jax.ShapeDtypeStruct(s, dt) for s, dt, _ in outs],
        compiler_params=_params("arbitrary" if has_acc else "parallel"),
    )(*ins)
    return res[0] if len(outs) == 1 else res


def _rms_fwd(x, g):
    r = lax.rsqrt(jnp.mean(x * x, axis=-1, keepdims=True) + EPS)
    return x * r * g


def _rms_bwd(dh, x, g):
    r = lax.rsqrt(jnp.mean(x * x, axis=-1, keepdims=True) + EPS)
    xh = x * r
    dxh = dh * g
    dx = r * (dxh - xh * jnp.mean(dxh * xh, axis=-1, keepdims=True))
    return dx, jnp.sum(dh * xh, axis=0, keepdims=True)


def _norm(x, g, *, name):
    return _rowmap(lambda xv, gv: _rms_fwd(xv, gv), [x, g], "rc", [(x.shape, MXU_DTYPE, "r")], name=name, tl=256)


def _norm_bwd(dh, x, g, dres, *, name):
    def fn(dhv, xv, gv, drv):
        dx, dg = _rms_bwd(dhv, xv, gv)
        return dx + drv, dg
    return _rowmap(fn, [dh, x, g, dres], "rrcr", [(x.shape, F32, "r"), (g.shape, F32, "a")], name=name, tl=256)


def _swiglu_act(a, b):
    return jax.nn.silu(a) * b


def _ffn_up(x, g, wg, wu, *, name, tm=512, tn=1024, job=None):
    m, d = x.shape
    n = wg.shape[0]
    tm, tn = min(tm, m), min(tn, n)
    assert m % tm == 0 and n % tn == 0, (name, m, n)

    def body(x_ref, g_ref, wg_ref, wu_ref, h_ref, a_ref, b_ref, s_ref):
        @pl.when(pl.program_id(1) == 0)
        def _():
            h_ref[...] = _rms_fwd(x_ref[...], g_ref[...]).astype(h_ref.dtype)

        hv = h_ref[...]
        av = lax.dot_general(hv, wg_ref[...], NT_DIMS, preferred_element_type=F32)
        bv = lax.dot_general(hv, wu_ref[...], NT_DIMS, preferred_element_type=F32)
        a_ref[...] = av.astype(a_ref.dtype)
        b_ref[...] = bv.astype(b_ref.dtype)
        s_ref[...] = _swiglu_act(av, bv).astype(s_ref.dtype)

    rows = pl.BlockSpec((tm, d), lambda i, j: (i, 0))
    wgt = pl.BlockSpec((tn, d), lambda i, j: (j, 0))
    tile = pl.BlockSpec((tm, tn), lambda i, j: (i, j))
    return _carried_call(
        body, name=name, grid=(m // tm, n // tn),
        in_specs=[rows, pl.BlockSpec((1, d), lambda i, j: (0, 0)), wgt, wgt],
        out_specs=[rows, tile, tile, tile],
        out_shape=[jax.ShapeDtypeStruct((m, d), MXU_DTYPE)] + [jax.ShapeDtypeStruct((m, n), MXU_DTYPE)] * 3,
        semantics=("parallel", "arbitrary"), operands=(x, g, wg, wu), job=job)


def _ffn_dx(da, db, wg, wu, x, g, dres, *, name, tm=512, tk=1024, job=None):
    m, f = da.shape
    d = wg.shape[1]
    tm, tk = min(tm, m), min(tk, f)
    assert m % tm == 0 and f % tk == 0, (name, m, f)
    nk = f // tk

    def body(da_ref, db_ref, wg_ref, wu_ref, x_ref, g_ref, dr_ref, dx_ref, dg_ref, acc_ref):
        i, kk = pl.program_id(0), pl.program_id(1)

        @pl.when(kk == 0)
        def _():
            acc_ref[...] = jnp.zeros_like(acc_ref)

        acc_ref[...] += jnp.dot(da_ref[...], wg_ref[...], preferred_element_type=F32)
        acc_ref[...] += jnp.dot(db_ref[...], wu_ref[...], preferred_element_type=F32)

        @pl.when(jnp.logical_and(i == 0, kk == 0))
        def _():
            dg_ref[...] = jnp.zeros_like(dg_ref)

        @pl.when(kk == nk - 1)
        def _():
            dx, dg = _rms_bwd(acc_ref[...], x_ref[...], g_ref[...])
            dx_ref[...] = dx + dr_ref[...]
            dg_ref[...] += dg

    act = pl.BlockSpec((tm, tk), lambda i, kk: (i, kk))
    wgt = pl.BlockSpec((tk, d), lambda i, kk: (kk, 0))
    rows = pl.BlockSpec((tm, d), lambda i, kk: (i, 0))
    one = pl.BlockSpec((1, d), lambda i, kk: (0, 0))
    return _carried_call(
        body, name=name, grid=(m // tm, nk),
        in_specs=[act, act, wgt, wgt, rows, one, rows],
        out_specs=[rows, one],
        out_shape=[jax.ShapeDtypeStruct((m, d), F32), jax.ShapeDtypeStruct((1, d), F32)],
        scratch_shapes=[pltpu.VMEM((tm, d), F32)],
        semantics=("arbitrary", "arbitrary"), operands=(da, db, wg, wu, x, g, dres), job=job)


def _ffn_fwd(x, g, full, which, layer, late):
    tag = f"{which}_{layer}"
    job, keys = late.gather_job(f"{tag}_up") if late else (None, [])
    (h, a, b, s), got = _ffn_up(x, g, full[(f"{which}_w_gate", layer)], full[(f"{which}_w_up", layer)], name=f"{tag}_up", job=job)
    full.update(zip(keys, got))
    x2 = _mm(s, full[(f"{which}_w_down", layer)], name=f"{tag}_down", epilogue=lambda acc, xv: xv + 0.5 * acc, extras=[x])
    return x2, (x, h, a, b, s)


def _ffn_bwd(dx2, saved, g, full, which, layer, grads, late, inline):
    x, h, a, b, s = saved
    tag = f"{which}_{layer}"
    kg, ku, kd = [(f"{which}_w_{n}", layer) for n in ("gate", "up", "down")]
    wg, wu, wd = full[kg], full[ku], full[kd]
    send = (lambda: late.scatter_job(grads)) if (late and inline) else (lambda: (None, []))

    def act_bwd(ds, av, bv):
        _, vjp = jax.vjp(_swiglu_act, av.astype(F32), bv.astype(F32))
        return vjp(0.5 * ds)

    grads[kd] = _mm(s, dx2, ta=True, name=f"{tag}_dwd", out_dtype=WIRE_DTYPE, epilogue=lambda acc: 0.5 * acc, tk=2048)
    job, keys = send()
    (da, db), got = _carried(_mm, dx2, wd, tb=True, name=f"{tag}_dact", epilogue=act_bwd, extras=[a, b],
                             out_dtype=[MXU_DTYPE, MXU_DTYPE], job=job)
    _note(late, keys, got)
    grads[kg] = _mm(da, h, ta=True, name=f"{tag}_dwg", out_dtype=WIRE_DTYPE, tk=2048)
    job, keys = send()
    grads[ku], got = _carried(_mm, db, h, ta=True, name=f"{tag}_dwu", out_dtype=WIRE_DTYPE, tk=2048, job=job)
    _note(late, keys, got)
    job, keys = send()
    (dx, dg), got = _ffn_dx(da, db, wg, wu, x, g, dx2, name=f"{tag}_dx", job=job)
    _note(late, keys, got)
    return dx, dg


def _carried(fn, *args, job, **kwargs):
    return fn(*args, job=job, **kwargs) if job is not None else (fn(*args, **kwargs), [])


def _note(late, keys, got):
    if late:
        late.received.update(zip(keys, got))


def _softplus(z):
    return jnp.maximum(z, 0.0) + jnp.log(1.0 + jnp.exp(-jnp.abs(z)))


def _ones_dot(x, tri):
    if MXU_DTYPE == F32:
        return jnp.dot(x, tri, preferred_element_type=F32)
    hi = x.astype(MXU_DTYPE)
    lo = (x - hi.astype(F32)).astype(MXU_DTYPE)
    return jnp.dot(hi, tri, preferred_element_type=F32) + jnp.dot(lo, tri, preferred_element_type=F32)


NT_DIMS = (((1,), (1,)), ((), ()))
TN_DIMS = (((0,), (0,)), ((), ()))


def _sb_fwd(q, k, v, *, tq=SB_QUERIES, job=None):
    nh, seq, dh = q.shape
    tq = min(tq, seq)
    hp = SB_PACK

    tk = min(SB_KEYS, tq)
    per = tq // tk

    def body(q_ref, k_ref, v_ref, o_ref, ls_ref):
        i = pl.program_id(1)
        r_idx = lax.broadcasted_iota(jnp.int32, (tk, tk), 0)
        c_idx = lax.broadcasted_iota(jnp.int32, (tk, tk), 1)
        after = (r_idx > c_idx).astype(MXU_DTYPE)

        def block(hd, j, c, acc, straddles):
            off = pl.multiple_of(j * tk, tk)
            kv = k_ref[hd, pl.ds(off, tk), :]
            vv = v_ref[hd, pl.ds(off, tk), :]
            top = 0 if straddles is None else straddles * tk
            rows = tq - top
            z = lax.dot_general(q_ref[hd, pl.ds(top, rows), :], kv, NT_DIMS, preferred_element_type=F32)
            sp = _softplus(z)
            c_in = c[top:]
            if straddles is None:
                lk = -sp
                w = jnp.exp(z - sp + _ones_dot(lk, after) + c_in)
            else:
                before = lax.broadcasted_iota(jnp.int32, (rows, tk), 1) < lax.broadcasted_iota(jnp.int32, (rows, tk), 0)
                lk = jnp.where(before, -sp, 0.0)
                w = jnp.where(before, jnp.exp(z - sp + _ones_dot(lk, after) + c_in), 0.0)
            acc_new = acc[top:] + jnp.dot(w.astype(MXU_DTYPE), vv, preferred_element_type=F32)
            c_new = c_in + jnp.sum(lk, axis=1, keepdims=True)
            if top:
                c_new, acc_new = jnp.concatenate([c[:top], c_new], axis=0), jnp.concatenate([acc[:top], acc_new], axis=0)
            return c_new, acc_new

        def step(n, carry):
            return tuple(block(hd, i * per - 1 - n, *carry[hd], None) for hd in range(hp))

        carry = tuple((jnp.zeros((tq, 1), F32), jnp.zeros((tq, dh), F32)) for _ in range(hp))
        for s in reversed(range(per)):
            carry = tuple(block(hd, i * per + s, *carry[hd], s) for hd in range(hp))
        for hd, (c, acc) in enumerate(lax.fori_loop(0, i * per, step, carry)):
            o_ref[hd] = acc
            ls_ref[hd] = c

    whole = pl.BlockSpec((hp, seq, dh), lambda h, i: (h, 0, 0))
    return _carried_call(
        body, name="sb_fwd", grid=(nh // hp, seq // tq),
        in_specs=[pl.BlockSpec((hp, tq, dh), lambda h, i: (h, i, 0)), whole, whole],
        out_specs=[pl.BlockSpec((hp, tq, dh), lambda h, i: (h, i, 0)), pl.BlockSpec((hp, tq, 1), lambda h, i: (h, i, 0))],
        out_shape=[jax.ShapeDtypeStruct((nh, seq, dh), F32), jax.ShapeDtypeStruct((nh, seq, 1), F32)],
        semantics=("parallel", "parallel"), operands=(q, k, v), job=job)


def _sb_bwd(q, k, v, lsum, do, *, tq=SB_QUERIES, job=None):
    nh, seq, dh = q.shape
    tq = min(tq, seq)
    tk = min(SB_KEYS, tq)
    per = tq // tk
    hp = SB_PACK
    scale = SB_SCALE

    def body(q_ref, k_ref, v_ref, ls_ref, do_ref, dq_ref, dk_ref, dv_ref):
        i = pl.program_id(1)

        @pl.when(i == 0)
        def _():
            dk_ref[...] = jnp.zeros_like(dk_ref)
            dv_ref[...] = jnp.zeros_like(dv_ref)

        r_idx = lax.broadcasted_iota(jnp.int32, (tk, tk), 0)
        c_idx = lax.broadcasted_iota(jnp.int32, (tk, tk), 1)
        upto = (r_idx <= c_idx).astype(MXU_DTYPE)
        before = (r_idx < c_idx).astype(MXU_DTYPE)

        def block(hd, j, cp, ce, dq, straddles):
            off = pl.multiple_of(j * tk, tk)
            top = 0 if straddles is None else straddles * tk
            rows = tq - top
            part = pl.ds(top, rows)
            qv, dov = q_ref[hd, part, :], do_ref[hd, part, :].astype(MXU_DTYPE)
            kv = k_ref[hd, pl.ds(off, tk), :]
            vv = v_ref[hd, pl.ds(off, tk), :]
            z = lax.dot_general(qv, kv, NT_DIMS, preferred_element_type=F32)
            sp = _softplus(z)
            valid = None
            if straddles is not None:
                valid = lax.broadcasted_iota(jnp.int32, (rows, tk), 1) < lax.broadcasted_iota(jnp.int32, (rows, tk), 0)
            lk = -sp if valid is None else jnp.where(valid, -sp, 0.0)
            w = jnp.exp(z - sp + (ls_ref[hd, part, :] - cp[top:]) - _ones_dot(lk, upto))
            if valid is not None:
                w = jnp.where(valid, w, 0.0)
            e = w * lax.dot_general(dov, vv, NT_DIMS, preferred_element_type=F32)
            earlier = _ones_dot(e, before) + ce[top:]
            keep = jnp.exp(-sp)
            dz = e * keep - (1.0 - keep) * earlier
            if valid is not None:
                dz = jnp.where(valid, dz, 0.0)
            dzm = dz.astype(MXU_DTYPE)
            dk_ref[hd, pl.ds(off, tk), :] += lax.dot_general(dzm, qv, TN_DIMS, preferred_element_type=F32)
            dv_ref[hd, pl.ds(off, tk), :] += lax.dot_general(w.astype(MXU_DTYPE), dov, TN_DIMS, preferred_element_type=F32)
            new = (cp[top:] + jnp.sum(lk, axis=1, keepdims=True), ce[top:] + jnp.sum(e, axis=1, keepdims=True),
                   dq[top:] + jnp.dot(dzm, kv, preferred_element_type=F32))
            if top:
                new = tuple(jnp.concatenate([old[:top], val], axis=0) for old, val in zip((cp, ce, dq), new))
            return new

        def step(j, carry):
            return tuple(block(hd, j, *carry[hd], None) for hd in range(hp))

        zero = jnp.zeros((tq, 1), F32)
        carry = lax.fori_loop(0, i * per, step, tuple((zero, zero, jnp.zeros((tq, dh), F32)) for _ in range(hp)))
        for s in range(per):
            carry = tuple(block(hd, i * per + s, *carry[hd], s) for hd in range(hp))
        for hd in range(hp):
            dq_ref[hd] = carry[hd][2] * scale

    whole = pl.BlockSpec((hp, seq, dh), lambda h, i: (h, 0, 0))
    tile = pl.BlockSpec((hp, tq, dh), lambda h, i: (h, i, 0))
    return _carried_call(
        body, name="sb_bwd", grid=(nh // hp, seq // tq),
        in_specs=[tile, whole, whole, pl.BlockSpec((hp, tq, 1), lambda h, i: (h, i, 0)), tile],
        out_specs=[tile, whole, whole],
        out_shape=[jax.ShapeDtypeStruct((nh, seq, dh), F32)] * 3,
        semantics=("parallel", "arbitrary"), operands=(q, k, v, lsum, do), job=job)


def _heads(t):
    return t.reshape(t.shape[0], SB_HEADS, SB_DH).transpose(1, 0, 2)


def _unheads(t):
    return t.transpose(1, 0, 2).reshape(t.shape[1], SB_HEADS * SB_DH)


def _s5_disc(lr, li, ldt, br, bi):
    dt = jnp.exp(ldt)
    mag = jnp.exp(lr * dt)
    ar = mag * jnp.cos(li * dt)
    ai = mag * jnp.sin(li * dt)
    den = lr * lr + li * li
    nr = ar - 1.0
    cr = (nr * lr + ai * li) / den
    ci = (ai * lr - nr * li) / den
    return ar, ai, cr[None] * br - ci[None] * bi, cr[None] * bi + ci[None] * br


def _s5_prep(lr, li, ldt, br, bi):
    shapes = [lr.shape, lr.shape, br.shape, br.shape]

    def body(lr_ref, li_ref, ldt_ref, br_ref, bi_ref, *outs):
        for o, val in zip(outs, _s5_disc(lr_ref[...], li_ref[...], ldt_ref[...], br_ref[...], bi_ref[...])):
            o[...] = val

    return pl.pallas_call(body, name="s5_prep", out_shape=[jax.ShapeDtypeStruct(s, F32) for s in shapes])(lr, li, ldt, br, bi)


def _s5_prep_bwd(lr, li, ldt, br, bi, cts):
    args = (lr, li, ldt, br, bi)

    def body(*refs):
        ins, ct_refs, outs = refs[:5], refs[5:9], refs[9:]
        _, vjp = jax.vjp(_s5_disc, *[r[...] for r in ins])
        for o, val in zip(outs, vjp(tuple(r[...] for r in ct_refs))):
            o[...] = val

    return pl.pallas_call(body, name="s5_prep_bwd", out_shape=[jax.ShapeDtypeStruct(a.shape, F32) for a in args])(*args, *cts)


def _s5_scan(bu, a, *, tc=512):
    seq, w2 = bu.shape
    tw = S5_BLOCK
    tc = min(tc, seq)
    assert seq % tc == 0 and w2 % (2 * tw) == 0

    def body(bu_ref, a_ref, h_ref, cr_ref, ci_ref):
        @pl.when(pl.program_id(1) == 0)
        def _():
            cr_ref[...] = jnp.zeros_like(cr_ref)
            ci_ref[...] = jnp.zeros_like(ci_ref)

        re, im = pl.ds(0, tw), pl.ds(tw, tw)
        ar, ai = a_ref[:, re], a_ref[:, im]

        def step(t, carry):
            hr, hi = carry
            row = pl.ds(t, 1)
            nr = ar * hr - ai * hi + bu_ref[row, re]
            ni = ar * hi + ai * hr + bu_ref[row, im]
            h_ref[row, re] = nr
            h_ref[row, im] = ni
            return nr, ni

        hr, hi = lax.fori_loop(0, tc, step, (cr_ref[...], ci_ref[...]), unroll=8)
        cr_ref[...] = hr
        ci_ref[...] = hi

    blk = pl.BlockSpec((tc, 2 * tw), lambda j, t: (t, j))
    return pl.pallas_call(
        body, name="s5_scan", grid=(w2 // (2 * tw), seq // tc),
        in_specs=[blk, pl.BlockSpec((1, 2 * tw), lambda j, t: (0, j))],
        out_specs=blk,
        out_shape=jax.ShapeDtypeStruct((seq, w2), F32),
        scratch_shapes=[pltpu.VMEM((1, tw), F32)] * 2,
        compiler_params=_params("parallel", "arbitrary"),
    )(bu, a)


def _s5_scan_bwd(d, h, a, *, tc=512):
    seq, w2 = d.shape
    tw = S5_BLOCK
    tc = min(tc, seq)
    assert seq % tc == 0 and w2 % (2 * tw) == 0
    nt = seq // tc

    def body(d_ref, h_ref, a_ref, g_ref, da_ref, cr_ref, ci_ref):
        @pl.when(pl.program_id(1) == 0)
        def _():
            cr_ref[...] = jnp.zeros_like(cr_ref)
            ci_ref[...] = jnp.zeros_like(ci_ref)
            da_ref[...] = jnp.zeros_like(da_ref)

        re, im = pl.ds(0, tw), pl.ds(tw, tw)
        ar, ai = a_ref[:, re], a_ref[:, im]

        def step(n, carry):
            gr, gi, sr, si = carry
            row = pl.ds(tc - 1 - n, 1)
            hrt, hit = h_ref[row, re], h_ref[row, im]
            sr = sr + gr * hrt + gi * hit
            si = si + gi * hrt - gr * hit
            ngr = d_ref[row, re] + ar * gr + ai * gi
            ngi = d_ref[row, im] + ar * gi - ai * gr
            g_ref[row, re] = ngr
            g_ref[row, im] = ngi
            return ngr, ngi, sr, si

        gr, gi, sr, si = lax.fori_loop(0, tc, step, (cr_ref[...], ci_ref[...], da_ref[:, re], da_ref[:, im]), unroll=8)
        cr_ref[...] = gr
        ci_ref[...] = gi
        da_ref[:, re] = sr
        da_ref[:, im] = si

    blk = pl.BlockSpec((tc, 2 * tw), lambda j, t: (nt - 1 - t, j))
    row = pl.BlockSpec((1, 2 * tw), lambda j, t: (0, j))
    return pl.pallas_call(
        body, name="s5_scan_bwd", grid=(w2 // (2 * tw), nt),
        in_specs=[blk, blk, row],
        out_specs=[blk, row],
        out_shape=[jax.ShapeDtypeStruct((seq, w2), F32), jax.ShapeDtypeStruct((1, w2), F32)],
        scratch_shapes=[pltpu.VMEM((1, tw), F32)] * 2,
        compiler_params=_params("parallel", "arbitrary"),
    )(d, h, a)


def _pair_columns(re, im, axis):
    shape = re.shape
    split = shape[:axis] + (shape[axis] // S5_BLOCK, S5_BLOCK) + shape[axis + 1:]
    both = jnp.stack([re.reshape(split), im.reshape(split)], axis=axis + 1)
    return both.reshape(shape[:axis] + (2 * shape[axis],) + shape[axis + 1:])


def _unpair_columns(t, axis):
    shape = t.shape
    both = t.reshape(shape[:axis] + (shape[axis] // (2 * S5_BLOCK), 2, S5_BLOCK) + shape[axis + 1:])
    half = shape[:axis] + (shape[axis] // 2,) + shape[axis + 1:]
    return (lax.index_in_dim(both, 0, axis + 1, keepdims=False).reshape(half),
            lax.index_in_dim(both, 1, axis + 1, keepdims=False).reshape(half))


def _block_diag(t):
    g, a, b = t.shape
    eye = jnp.eye(g, dtype=t.dtype)
    return (t[:, :, None, :] * eye[:, None, :, None]).reshape(g * a, g * b)


def _block_diag_part(m, g):
    a, b = m.shape[0] // g, m.shape[1] // g
    return jnp.moveaxis(jnp.diagonal(m.reshape(g, a, g, b), axis1=0, axis2=2), -1, 0)


def _gelu_glu(y, gate_pre):
    z = jax.nn.gelu(y)
    return z * jax.nn.sigmoid(gate_pre)


def _s5_fwd(u, p, w_glu):
    lr, li = p["s5_lambda_re"][0], p["s5_lambda_im"][0]
    ldt = p["s5_log_dt"][0][:, None]
    br = p["s5_b_re"][0].transpose(2, 0, 1)
    bi = p["s5_b_im"][0].transpose(2, 0, 1)
    ar, ai, bbr, bbi = _s5_prep(lr, li, ldt, br, bi)
    a = _pair_columns(ar.reshape(1, S5_LANES), ai.reshape(1, S5_LANES), 1)
    bmat = _pair_columns(_block_diag(bbr.transpose(1, 0, 2)), _block_diag(bbi.transpose(1, 0, 2)), 1)
    cmat = _pair_columns(_block_diag(p["s5_c_re"][0].transpose(0, 2, 1)),
                         -_block_diag(p["s5_c_im"][0].transpose(0, 2, 1)), 0)
    bmat, cmat = bmat.astype(MXU_DTYPE), cmat.astype(MXU_DTYPE)
    bu = _mm(u, bmat, name="s5_bu")
    h = _s5_scan(bu, a)
    d = p["s5_d"]
    y = _mm(h, cmat, name="s5_y", epilogue=lambda acc, uv, dv: acc + dv * uv, extras=[u, d])
    z = _rowmap(jax.nn.gelu, [y], "r", [(y.shape, MXU_DTYPE, "r")], name="s5_gelu", tl=512)
    gate_pre = _mm(z, w_glu, name="s5_glu")
    out = _rowmap(_gelu_glu, [y, gate_pre], "rr", [(y.shape, F32, "r")], name="s5_gate", tl=512)
    return out, (u, lr, li, ldt, br, bi, a, bmat, cmat, h, y, z, gate_pre)


def _s5_bwd(dout, saved, p, w_glu):
    u, lr, li, ldt, br, bi, a, bmat, cmat, h, y, z, gate_pre = saved
    d = p["s5_d"]

    def gate_bwd(dov, yv, gv):
        zv = jax.nn.gelu(yv)
        sg = jax.nn.sigmoid(gv)
        return dov * sg, dov * zv * sg * (1.0 - sg)

    dz_direct, dgate = _rowmap(gate_bwd, [dout, y, gate_pre], "rrr", [(y.shape, F32, "r"), (y.shape, MXU_DTYPE, "r")],
                               name="s5_gate_bwd", tl=512)
    dw_glu = _mm(z, dgate, ta=True, name="s5_dwglu", out_dtype=WIRE_DTYPE)
    dz = _mm(dgate, w_glu, tb=True, name="s5_dz", epilogue=lambda acc, prev: acc + prev, extras=[dz_direct])

    def gelu_bwd(dzv, yv, uv, dvv):
        _, vjp = jax.vjp(jax.nn.gelu, yv)
        dy = vjp(dzv)[0]
        return dy, dy * dvv, jnp.sum(dy * uv, axis=0, keepdims=True)

    dy, du_skip, dd = _rowmap(gelu_bwd, [dz, y, u, d], "rrrc",
                              [(y.shape, F32, "r"), (y.shape, F32, "r"), (d.shape, F32, "a")], name="s5_gelu_bwd", tl=512)
    dcmat = _mm(h, dy, ta=True, name="s5_dc")
    dstate = _mm(dy, cmat, tb=True, name="s5_dstate")
    g, da = _s5_scan_bwd(dstate, h, a)
    du = _mm(g, bmat, tb=True, name="s5_du", epilogue=lambda acc, prev: acc + prev, extras=[du_skip], out_dtype=MXU_DTYPE)
    dbmat = _mm(u, g, ta=True, name="s5_db")
    dbbr, dbbi = (_block_diag_part(t, S5_GROUPS).transpose(1, 0, 2) for t in _unpair_columns(dbmat, 1))
    dar, dai = _unpair_columns(da, 1)
    cts = (dar.reshape(S5_GROUPS, S5_STATE), dai.reshape(S5_GROUPS, S5_STATE), dbbr, dbbi)
    dlr, dli, dldt, dbr, dbi = _s5_prep_bwd(lr, li, ldt, br, bi, cts)
    dcr, dci = (_block_diag_part(t, S5_GROUPS).transpose(0, 2, 1) for t in _unpair_columns(dcmat, 0))
    grads = {
        "s5_lambda_re": dlr[None], "s5_lambda_im": dli[None], "s5_log_dt": dldt[:, 0][None],
        "s5_b_re": dbr.transpose(1, 2, 0)[None], "s5_b_im": dbi.transpose(1, 2, 0)[None],
        "s5_c_re": dcr[None], "s5_c_im": -dci[None], "s5_d": dd,
    }
    return du, dw_glu, grads


def _mix0_fwd(x, g, p, full, late):
    h = _norm(x, g, name="mix0_norm")
    proj = _mm(h, full[("ab_w_in", 0)], name="mix0_in")
    u = proj[:, :S5_WIDTH]
    q, k, v = (_heads(proj[:, S5_WIDTH * (1 + n):S5_WIDTH * (2 + n)] * (SB_SCALE if n == 0 else 1.0)).astype(MXU_DTYPE)
               for n in range(3))
    job, keys = late.gather_job("sb_fwd") if late else (None, [])
    (o, lsum), got = _sb_fwd(q, k, v, job=job)
    full.update(zip(keys, got))
    w_glu, w_out = full[("s5_w_glu", 0)], full[("ab_w_out", 0)]
    y_a, s5_saved = _s5_fwd(u, p, w_glu)
    mix = jnp.concatenate([y_a, _unheads(o)], axis=1).astype(MXU_DTYPE)
    x2 = _mm(mix, w_out, name="mix0_out", epilogue=lambda acc, xv: xv + acc, extras=[x])
    return x2, (x, h, q, k, v, lsum, mix, s5_saved)


def _mix0_bwd(dx2, saved, g, p, full, grads, late):
    x, h, q, k, v, lsum, mix, s5_saved = saved
    w_in, w_glu, w_out = full[("ab_w_in", 0)], full[("s5_w_glu", 0)], full[("ab_w_out", 0)]
    dmix = _mm(dx2, w_out, tb=True, name="mix0_dmix")
    grads[("ab_w_out", 0)] = _mm(mix, dx2, ta=True, name="mix0_dwout", out_dtype=WIRE_DTYPE)
    du, grads[("s5_w_glu", 0)], s5_grads = _s5_bwd(dmix[:, :S5_WIDTH], s5_saved, p, w_glu)
    job, keys = late.scatter_job(grads) if late else (None, [])
    (dq, dk, dv), got = _sb_bwd(q, k, v, lsum, _heads(dmix[:, S5_WIDTH:]), job=job)
    _note(late, keys, got)
    dproj = jnp.concatenate([du] + [_unheads(t).astype(MXU_DTYPE) for t in (dq, dk, dv)], axis=1)
    grads[("ab_w_in", 0)] = _mm(h, dproj, ta=True, name="mix0_dwin", out_dtype=WIRE_DTYPE)
    job, keys = late.scatter_job(grads) if late else (None, [])
    dh, got = _carried(_mm, dproj, w_in, tb=True, name="mix0_dh", job=job)
    _note(late, keys, got)
    dx, dg = _norm_bwd(dh, x, g, dx2, name="mix0_norm_bwd")
    return dx, dg, s5_grads


def _shift_down(t, n):
    rows = lax.broadcasted_iota(jnp.int32, t.shape, 0)
    return jnp.where(rows >= n, pltpu.roll(t, n, 0), 0.0)


def _shift_up(t, n):
    rows = lax.broadcasted_iota(jnp.int32, t.shape, 0)
    return jnp.where(rows < t.shape[0] - n, pltpu.roll(t, t.shape[0] - n, 0), 0.0)


def _conv_fwd(proj, cw, *, tc=128):
    seq, c3 = proj.shape
    ch = c3 // 3
    nb = ch // tc

    def body(b_ref, c_ref, v_ref, w_ref, m_ref):
        pv = c_ref[...] * v_ref[...]
        w = w_ref[...]
        y = w[2:3] * pv + w[1:2] * _shift_down(pv, 1) + w[0:1] * _shift_down(pv, 2)
        m_ref[...] = (b_ref[...] * y).astype(m_ref.dtype)

    col = lambda part: pl.BlockSpec((seq, tc), lambda j: (0, part * nb + j))
    return pl.pallas_call(
        body, name="conv_fwd", grid=(nb,),
        in_specs=[col(0), col(1), col(2), pl.BlockSpec((3, tc), lambda j: (0, j))],
        out_specs=pl.BlockSpec((seq, tc), lambda j: (0, j)),
        out_shape=jax.ShapeDtypeStruct((seq, ch), MXU_DTYPE),
        compiler_params=_params("parallel"),
    )(proj, proj, proj, cw)


def _conv_bwd(proj, cw, dm, *, tc=128):
    seq, c3 = proj.shape
    ch = c3 // 3
    nb = ch // tc

    def body(b_ref, c_ref, v_ref, w_ref, dm_ref, dproj_ref, dw_ref, dc_ref, dv_ref):
        part = pl.program_id(1)

        @pl.when(part == 0)
        def _():
            cv, vv, dmv = c_ref[...], v_ref[...], dm_ref[...]
            pv = cv * vv
            w = w_ref[...]
            p1, p2 = _shift_down(pv, 1), _shift_down(pv, 2)
            y = w[2:3] * pv + w[1:2] * p1 + w[0:1] * p2
            dproj_ref[...] = (dmv * y).astype(dproj_ref.dtype)
            dy = dmv * b_ref[...]
            dp = w[2:3] * dy + w[1:2] * _shift_up(dy, 1) + w[0:1] * _shift_up(dy, 2)
            dc_ref[...] = (dp * vv).astype(dc_ref.dtype)
            dv_ref[...] = (dp * cv).astype(dv_ref.dtype)
            dw_ref[...] = jnp.concatenate([jnp.sum(dy * p2, axis=0, keepdims=True), jnp.sum(dy * p1, axis=0, keepdims=True),
                                           jnp.sum(dy * pv, axis=0, keepdims=True)], axis=0)

        @pl.when(part == 1)
        def _():
            dproj_ref[...] = dc_ref[...]

        @pl.when(part == 2)
        def _():
            dproj_ref[...] = dv_ref[...]

    col = lambda part: pl.BlockSpec((seq, tc), lambda j, t: (0, part * nb + j))
    small = pl.BlockSpec((3, tc), lambda j, t: (0, j))
    return pl.pallas_call(
        body, name="conv_bwd", grid=(nb, 3),
        in_specs=[col(0), col(1), col(2), small, pl.BlockSpec((seq, tc), lambda j, t: (0, j))],
        out_specs=[pl.BlockSpec((seq, tc), lambda j, t: (0, t * nb + j)), small],
        out_shape=[jax.ShapeDtypeStruct((seq, c3), MXU_DTYPE), jax.ShapeDtypeStruct((3, ch), F32)],
        scratch_shapes=[pltpu.VMEM((seq, tc), MXU_DTYPE)] * 2,
        compiler_params=_params("parallel", "arbitrary"),
    )(proj, proj, proj, cw, dm)


def _mix1_fwd(x, g, full, late):
    h = _norm(x, g, name="mix1_norm")
    job, keys = late.gather_job("mix1_in") if late else (None, [])
    proj, got = _carried(_mm, h, full[("sc_w_in", 0)], name="mix1_in", job=job)
    full.update(zip(keys, got))
    m = _conv_fwd(proj, full[("sc_conv_w", 0)])
    x2 = _mm(m, full[("sc_w_out", 0)], name="mix1_out", epilogue=lambda acc, xv: xv + acc, extras=[x])
    return x2, (x, h, proj, m)


def _mix1_bwd(dx2, saved, g, w_in, cw, w_out):
    x, h, proj, m = saved
    dm = _mm(dx2, w_out, tb=True, name="mix1_dm")
    dw_out = _mm(m, dx2, ta=True, name="mix1_dwout", out_dtype=WIRE_DTYPE)
    dproj, dcw = _conv_bwd(proj, cw, dm)
    dw_in = _mm(h, dproj, ta=True, name="mix1_dwin", out_dtype=WIRE_DTYPE)
    dh = _mm(dproj, w_in, tb=True, name="mix1_dh")
    dx, dg = _norm_bwd(dh, x, g, dx2, name="mix1_norm_bwd")
    return dx, dg, dw_in, dcw, dw_out


def _loss_head(x, g, target):
    feat = x.shape[1]

    def fn(xv, gv, tv):
        err = _rms_fwd(xv, gv) - tv
        dx, dg = _rms_bwd(err / feat, xv, gv)
        return jnp.sum(err * err, keepdims=True) * (0.5 / feat), dx, dg

    return _rowmap(fn, [x, g, target], "rcr", [((1, 1), F32, "a"), (x.shape, F32, "r"), (g.shape, F32, "a")],
                   name="loss_head", tl=256)


def _slot(ref, place, chip=None, half=None):
    axis, width = place
    shape = list(ref.shape)
    start = [0, 0]
    if chip is not None:
        start[axis], shape[axis] = chip * width, width
    if half is not None:
        h_axis = 0 if shape[0] % 32 == 0 else 1
        shape[h_axis] //= 2
        start[h_axis] = start[h_axis] + half * shape[h_axis]
    hint = lambda s, d: s if isinstance(s, int) else pl.multiple_of(s, 128 if d == 1 else 8)
    return ref.at[tuple(pl.ds(hint(s, d), n) for d, (s, n) in enumerate(zip(start, shape)))]


class _Exchange:
    def __init__(self, kind, arrays, places):
        self.kind, self.arrays, self.places, self.n = kind, list(arrays), list(places), len(arrays)
        self.out_shape = []
        for t, (axis, width) in zip(self.arrays, self.places):
            if kind == "gather":
                shape = list(t.shape)
                shape[axis] = N_CHIPS * width
            else:
                shape = [N_CHIPS] + list(t.shape)
                shape[1 + axis] = width
            self.out_shape.append(jax.ShapeDtypeStruct(tuple(shape), t.dtype))
        n = self.n
        self.scratch = [pltpu.SemaphoreType.DMA((3 * n,)) for _ in range(4 if kind == "gather" else 2)]
        self.scratch.append(pltpu.SemaphoreType.DMA((n,)))

    def _copies(self, ins, outs, sems):
        x, y, c = lax.axis_index("x"), lax.axis_index("y"), lax.axis_index("c")
        peers = [(1 - x, y), (x, 1 - y), (1 - x, 1 - y)]
        remote = lambda src, dst, send, recv, k, to: pltpu.make_async_remote_copy(
            src_ref=src, dst_ref=dst, send_sem=send.at[k], recv_sem=recv.at[k], device_id=to, device_id_type=MESH_ID)
        local, ici, d2d = [], [], []
        for a in range(self.n):
            place = self.places[a]
            if self.kind == "gather":
                local.append(pltpu.make_async_copy(ins[a], _slot(outs[a], place, 2 * x + y), sems[4].at[a]))
                for r, (px, py) in enumerate(peers):
                    ici.append(remote(_slot(ins[a], place, None, c), _slot(outs[a], place, 2 * x + y, c),
                                      sems[0], sems[1], 3 * a + r, (px, py, c)))
                    landed = _slot(outs[a], place, 2 * px + py, c)
                    d2d.append(remote(landed, landed, sems[2], sems[3], 3 * a + r, (x, y, 1 - c)))
            else:
                local.append(pltpu.make_async_copy(_slot(ins[a], place, 2 * x + y), outs[a].at[3], sems[2].at[a]))
                for r, (px, py) in enumerate(peers):
                    ici.append(remote(_slot(ins[a], place, 2 * px + py), outs[a].at[r], sems[0], sems[1], 3 * a + r, (px, py, c)))
        return local, ici, d2d

    def start(self, ins, outs, sems):
        local, ici, _ = self._copies(ins, outs, sems)
        for cp in local + ici:
            cp.start()

    def relay(self, ins, outs, sems):
        _, ici, d2d = self._copies(ins, outs, sems)
        for arrived, onward in zip(ici, d2d):
            arrived.wait_recv()
            onward.start()

    def finish(self, ins, outs, sems):
        local, ici, d2d = self._copies(ins, outs, sems)
        for cp in local + d2d:
            cp.wait()
        for cp in ici:
            cp.wait_send() if d2d else cp.wait()


def _exchange_call(job, name):
    n = job.n

    def body(*refs):
        ins, outs, sems = refs[:n], refs[n:2 * n], refs[2 * n:]
        job.start(ins, outs, sems)
        job.relay(ins, outs, sems)
        job.finish(ins, outs, sems)

    return pl.pallas_call(
        body, name=name, in_specs=[ANY_SPEC] * n, out_specs=[ANY_SPEC] * n, out_shape=job.out_shape,
        scratch_shapes=job.scratch, compiler_params=pltpu.CompilerParams(has_side_effects=True),
    )(*job.arrays)


def _carried_call(body, *, name, grid, in_specs, out_specs, out_shape, semantics, operands, scratch_shapes=(), job=None):
    scratch_shapes = list(scratch_shapes)
    if job is None:
        return pl.pallas_call(body, name=name, grid=grid, in_specs=in_specs, out_specs=out_specs, out_shape=out_shape,
                              scratch_shapes=scratch_shapes, compiler_params=_params(*semantics))(*operands), []
    n_in, n_out, n, n_scr = len(in_specs), len(out_specs), job.n, len(scratch_shapes)
    steps = math.prod(grid)

    def wrapped(*refs):
        ins, job_ins = refs[:n_in], refs[n_in:n_in + n]
        outs, job_outs = refs[n_in + n:n_in + n + n_out], refs[n_in + n + n_out:n_in + 2 * n + n_out]
        outs = outs + refs[n_in + 2 * n + n_out:n_in + 2 * n + n_out + n_scr]
        sems = refs[n_in + 2 * n + n_out + n_scr:]
        step = functools.reduce(lambda acc, d: acc * grid[d] + pl.program_id(d), range(len(grid)), 0)

        @pl.when(step == 0)
        def _():
            job.start(job_ins, job_outs, sems)

        @pl.when(step == (3 * steps) // 4)
        def _():
            job.relay(job_ins, job_outs, sems)

        body(*ins, *outs)

        @pl.when(step == steps - 1)
        def _():
            job.finish(job_ins, job_outs, sems)

    res = pl.pallas_call(
        wrapped, name=name, grid=grid, in_specs=list(in_specs) + [ANY_SPEC] * n, out_specs=list(out_specs) + [ANY_SPEC] * n,
        out_shape=list(out_shape) + job.out_shape, scratch_shapes=scratch_shapes + job.scratch,
        compiler_params=pltpu.CompilerParams(dimension_semantics=("arbitrary",) * len(grid), vmem_limit_bytes=VMEM_LIMIT,
                                             has_side_effects=True),
    )(*operands, *job.arrays)
    return res[:n_out], res[n_out:]


def _swap_with_sibling(parts):
    n = len(parts)

    def body(*refs):
        ins, outs = refs[:n], refs[n:2 * n]
        send, recv = refs[2 * n:]
        sibling = (lax.axis_index("x"), lax.axis_index("y"), 1 - lax.axis_index("c"))
        copies = [pltpu.make_async_remote_copy(src_ref=ins[a], dst_ref=outs[a], send_sem=send.at[a], recv_sem=recv.at[a],
                                               device_id=sibling, device_id_type=MESH_ID) for a in range(n)]
        for cp in copies:
            cp.start()
        for cp in copies:
            cp.wait()

    return pl.pallas_call(
        body, name="swap_with_sibling",
        in_specs=[ANY_SPEC] * n, out_specs=[ANY_SPEC] * n,
        out_shape=[jax.ShapeDtypeStruct(p.shape, p.dtype) for p in parts],
        scratch_shapes=[pltpu.SemaphoreType.DMA((n,)), pltpu.SemaphoreType.DMA((n,))],
        compiler_params=pltpu.CompilerParams(has_side_effects=True),
    )(*parts)


def _sum_all_devices(t):
    rows = t.shape[0]

    def body(t_ref, o_ref, slots, send, recv):
        x, y, c = lax.axis_index("x"), lax.axis_index("y"), lax.axis_index("c")
        me = 4 * x + 2 * y + c
        slots[me] = t_ref[...]
        copies = []
        for m in range(1, 8):
            peer = (x ^ (m >> 2), y ^ ((m >> 1) & 1), c ^ (m & 1))
            cp = pltpu.make_async_remote_copy(src_ref=t_ref, dst_ref=slots.at[me], send_sem=send.at[m - 1],
                                              recv_sem=recv.at[m - 1], device_id=peer, device_id_type=MESH_ID)
            cp.start()
            copies.append(cp)
        for cp in copies:
            cp.wait()
        acc = slots[0]
        for dev in range(1, 8):
            acc = acc + slots[dev]
        o_ref[...] = acc

    vmem = pl.BlockSpec(memory_space=pltpu.VMEM)
    return pl.pallas_call(
        body, name="sum_all_devices", in_specs=[vmem], out_specs=vmem,
        out_shape=jax.ShapeDtypeStruct(t.shape, F32),
        scratch_shapes=[pltpu.VMEM((8, rows, 128), F32), pltpu.SemaphoreType.DMA((7,)), pltpu.SemaphoreType.DMA((7,))],
        compiler_params=pltpu.CompilerParams(vmem_limit_bytes=VMEM_LIMIT, has_side_effects=True),
    )(t)


def _adamw(w, g, m, v):
    m = ADAM_B1 * m + (1.0 - ADAM_B1) * g
    v = ADAM_B2 * v + (1.0 - ADAM_B2) * jnp.square(g)
    m_hat = m / (1.0 - ADAM_B1 ** ADAM_STEP)
    v_hat = v / (1.0 - ADAM_B2 ** ADAM_STEP)
    return -ADAM_LR * (m_hat / (jnp.sqrt(v_hat) + ADAM_EPS) + ADAM_WD * w), m, v


def _chip_sum(received, name):
    rows, cols = received.shape[1:]
    tl = _row_block(rows, 512)

    def body(r_ref, o_ref):
        o_ref[...] = ((r_ref[0].astype(F32) + r_ref[1].astype(F32)) + r_ref[2].astype(F32)) + r_ref[3].astype(F32)

    return pl.pallas_call(body, name=name, grid=(rows // tl,),
                          in_specs=[pl.BlockSpec((N_CHIPS, tl, cols), lambda i: (0, i, 0))],
                          out_specs=pl.BlockSpec((tl, cols), lambda i: (i, 0)),
                          out_shape=jax.ShapeDtypeStruct((rows, cols), F32), compiler_params=_params("parallel"))(received)


def _adamw_layer(w, m, v, p_mine, p_other, layer, prev, name):
    _, rows, cols = w.shape
    tl = _row_block(rows, 512)

    def body(w_ref, m_ref, v_ref, pa_ref, pb_ref, *rest):
        g = pa_ref[:, :cols] + pb_ref[:, :cols]
        for o_ref, val in zip(rest[-4:], (g,) + _adamw(w_ref[...], g, m_ref[...], v_ref[...])):
            o_ref[...] = val

    stacked = pl.BlockSpec((None, tl, cols), lambda i: (layer, i, 0))
    part = pl.BlockSpec((tl, p_mine.shape[1]), lambda i: (i, 0))
    kept = list(prev) if prev else []
    return pl.pallas_call(
        body, name=name, grid=(rows // tl,),
        in_specs=[stacked] * 3 + [part] * 2 + [ANY_SPEC] * len(kept),
        out_specs=[stacked] * 4, out_shape=[jax.ShapeDtypeStruct(w.shape, F32)] * 4,
        input_output_aliases={5 + k: k for k in range(len(kept))},
        compiler_params=_params("parallel"),
    )(w, m, v, p_mine, p_other, *kept)


def _adamw_small(w, g, m, v):
    def fn(wv, gv, mv, vv):
        return _adamw(wv, gv, mv, vv)

    return _rowmap(fn, [w, g, m, v], "rrrr", [(w.shape, F32, "r")] * 3, name="adamw_small", tl=w.shape[0])


WEIGHTS = ['ffn1_norm', 'ffn1_w_gate', 'ffn1_w_up', 'ffn1_w_down', 'mix_norm', 'ffn2_norm', 'ffn2_w_gate', 'ffn2_w_up',
           'ffn2_w_down', 'ab_w_in', 's5_lambda_re', 's5_lambda_im', 's5_log_dt', 's5_b_re', 's5_b_im', 's5_c_re', 's5_c_im',
           's5_d', 's5_w_glu', 'ab_w_out', 'sc_w_in', 'sc_conv_w', 'sc_w_out', 'final_norm']
SHARDED = {'ffn1_w_gate': (0, FF_SLOT), 'ffn1_w_up': (0, FF_SLOT), 'ffn1_w_down': (0, FF_SLOT),
           'ffn2_w_gate': (0, FF_SLOT), 'ffn2_w_up': (0, FF_SLOT), 'ffn2_w_down': (0, FF_SLOT),
           'ab_w_in': (1, 512), 's5_w_glu': (0, 128), 'ab_w_out': (0, 256), 'sc_w_in': (1, 768), 'sc_conv_w': (1, 256),
           'sc_w_out': (0, 256)}
SWAPPED = ('ffn1_w_gate', 'ffn1_w_up', 'ffn2_w_gate', 'ffn2_w_up')
SMALL = [n for n in WEIGHTS if n not in SHARDED]


def _held(name, t):
    return jnp.swapaxes(t, 1, 2) if name in SWAPPED else t


def _pack(arrays):
    rows = []
    for t in arrays:
        flat = t.reshape(-1)
        rows.append(jnp.pad(flat, (0, (-flat.shape[0]) % 128)))
    flat = jnp.concatenate(rows)
    return jnp.pad(flat, (0, (-flat.shape[0]) % 1024)).reshape(-1, 128)


def _unpack(packed, like):
    flat, out, pos = packed.reshape(-1), [], 0
    for t in like:
        out.append(flat[pos:pos + t.size].reshape(t.shape))
        pos += t.size + (-t.size) % 128
    return out


def _local_grads(x, target, p, full, late=None):
    small, grads, saved = {}, {}, []
    for layer in range(2):
        x, s1 = _ffn_fwd(x, p["ffn1_norm"][layer:layer + 1], full, "ffn1", layer, late)
        if layer == 0:
            x, sm = _mix0_fwd(x, p["mix_norm"][0:1], p, full, late)
        else:
            x, sm = _mix1_fwd(x, p["mix_norm"][1:2], full, late)
        x, s2 = _ffn_fwd(x, p["ffn2_norm"][layer:layer + 1], full, "ffn2", layer, late)
        saved.append((s1, sm, s2))
    loss, dx, dg_final = _loss_head(x, p["final_norm"][None], target)
    small["final_norm"] = dg_final[0]
    gains = {n: [None, None] for n in ("ffn1_norm", "mix_norm", "ffn2_norm")}

    def ffn_bwd(which, layer, dx, s):
        dx, dg = _ffn_bwd(dx, s, p[f"{which}_norm"][layer:layer + 1], full, which, layer, grads, late,
                          inline=(which, layer) in (("ffn2", 1), ("ffn1", 0)))
        gains[f"{which}_norm"][layer] = dg[0]
        return dx

    for layer in (1, 0):
        s1, sm, s2 = saved[layer]
        dx = ffn_bwd("ffn2", layer, dx, s2)
        if layer == 0:
            dx, dg, s5_grads = _mix0_bwd(dx, sm, p["mix_norm"][0:1], p, full, grads, late)
            small.update(s5_grads)
        else:
            dx, dg, dw_in, dcw, dw_out = _mix1_bwd(dx, sm, p["mix_norm"][1:2], full[("sc_w_in", 0)], full[("sc_conv_w", 0)],
                                                   full[("sc_w_out", 0)])
            grads.update({("sc_w_in", 0): dw_in, ("sc_conv_w", 0): dcw.astype(WIRE_DTYPE), ("sc_w_out", 0): dw_out})
        gains["mix_norm"][layer] = dg[0]
        dx = ffn_bwd("ffn1", layer, dx, s1)
    small.update({n: jnp.stack(pair) for n, pair in gains.items()})
    return loss, dx, small, grads


_GATHER_PLAN = {
    "gather_early": [("ffn1_w_gate", 0), ("ffn1_w_up", 0)],
    "ffn1_0_up": [("ffn1_w_down", 0), ("ab_w_in", 0)],
    "sb_fwd": [("s5_w_glu", 0), ("ab_w_out", 0), ("ffn2_w_gate", 0), ("ffn2_w_up", 0), ("ffn2_w_down", 0),
               ("ffn1_w_gate", 1), ("ffn1_w_up", 1), ("ffn1_w_down", 1)],
    "ffn2_0_up": [("sc_w_in", 0), ("sc_conv_w", 0), ("sc_w_out", 0)],
    "ffn1_1_up": [("ffn2_w_gate", 1), ("ffn2_w_up", 1)],
    "mix1_in": [("ffn2_w_down", 1)],
}


class _Late:
    def __init__(self, shards, places):
        self.shards, self.places = shards, places
        self.sent, self.received = set(), {}

    def gather_job(self, carrier):
        keys = _GATHER_PLAN.get(carrier, [])
        if not keys:
            return None, []
        return _Exchange("gather", [self.shards[k] for k in keys], [self.places[k] for k in keys]), keys

    def scatter_job(self, grads):
        keys = [k for k in grads if k not in self.sent]
        if not keys:
            return None, []
        self.sent.update(keys)
        return _Exchange("scatter", [grads[k] for k in keys], [self.places[k] for k in keys]), keys


def kernel(x, ffn1_norm, ffn1_w_gate, ffn1_w_up, ffn1_w_down, mix_norm, ffn2_norm, ffn2_w_gate, ffn2_w_up, ffn2_w_down, ab_w_in, s5_lambda_re, s5_lambda_im, s5_log_dt, s5_b_re, s5_b_im, s5_c_re, s5_c_im, s5_d, s5_w_glu, ab_w_out, sc_w_in, sc_conv_w, sc_w_out, final_norm, loss_target, m_ffn1_norm, m_ffn1_w_gate, m_ffn1_w_up, m_ffn1_w_down, m_mix_norm, m_ffn2_norm, m_ffn2_w_gate, m_ffn2_w_up, m_ffn2_w_down, m_ab_w_in, m_s5_lambda_re, m_s5_lambda_im, m_s5_log_dt, m_s5_b_re, m_s5_b_im, m_s5_c_re, m_s5_c_im, m_s5_d, m_s5_w_glu, m_ab_w_out, m_sc_w_in, m_sc_conv_w, m_sc_w_out, m_final_norm, v_ffn1_norm, v_ffn1_w_gate, v_ffn1_w_up, v_ffn1_w_down, v_mix_norm, v_ffn2_norm, v_ffn2_w_gate, v_ffn2_w_up, v_ffn2_w_down, v_ab_w_in, v_s5_lambda_re, v_s5_lambda_im, v_s5_log_dt, v_s5_b_re, v_s5_b_im, v_s5_c_re, v_s5_c_im, v_s5_d, v_s5_w_glu, v_ab_w_out, v_sc_w_in, v_sc_conv_w, v_sc_w_out, v_final_norm):
    args = dict(locals())
    p = {n: _held(n, args[n]) for n in WEIGHTS}
    mom = {n: _held(n, args["m_" + n]) for n in WEIGHTS}
    var = {n: _held(n, args["v_" + n]) for n in WEIGHTS}

    keys = [(n, layer) for n in SHARDED for layer in range(p[n].shape[0])]
    shards, places = {}, {}
    for n, layer in keys:
        axis, width = SHARDED[n]
        t = p[n][layer] if n == "sc_conv_w" else p[n][layer].astype(MXU_DTYPE)
        pad = [(0, 0), (0, 0)]
        pad[axis] = (0, width - t.shape[axis])
        shards[(n, layer)], places[(n, layer)] = jnp.pad(t, pad), (axis, width)
    late = _Late(shards, places)
    job, first = late.gather_job("gather_early")
    full = dict(zip(first, _exchange_call(job, "gather_early")))

    loss, dx, small, grads = _local_grads(x[0], loss_target[0], p, full, late)
    loss = lax.psum(loss[0, 0], ("x", "y", "c"))
    assert set(late.received) == set(keys), "a gradient was left without a carrier"

    partial = [_chip_sum(late.received[(n, layer)], name=f"chip_sum_{n}_{layer}") for n, layer in keys]
    other = _swap_with_sibling(partial)
    out = {}
    for (n, layer), mine, theirs in zip(keys, partial, other):
        out[n] = _adamw_layer(p[n], mom[n], var[n], mine, theirs, layer, out.get(n), name=f"adamw_{n}_{layer}")
    out = {n: [_held(n, t) for t in res] for n, res in out.items()}

    like = [p[n] for n in SMALL]
    g_small = _sum_all_devices(_pack([small[n] for n in SMALL]))
    d_small, m_small, v_small = _adamw_small(_pack(like), g_small, _pack([mom[n] for n in SMALL]), _pack([var[n] for n in SMALL]))
    for k, packed in enumerate((g_small, d_small, m_small, v_small)):
        for n, t in zip(SMALL, _unpack(packed, like)):
            out.setdefault(n, [None] * 4)[k] = t

    return (loss, dx[None], *[out[n][0] for n in WEIGHTS], *[out[n][1] for n in WEIGHTS],
            *[out[n][2] for n in WEIGHTS], *[out[n][3] for n in WEIGHTS])
```

```python
import functools
import math

import jax
import jax.numpy as jnp
from jax import lax
from jax.experimental import pallas as pl
from jax.experimental.pallas import tpu as pltpu

F32 = jnp.float32
MXU_DTYPE = jnp.bfloat16
WIRE_DTYPE = jnp.bfloat16
MESH_ID = pl.DeviceIdType.MESH

D_MODEL = 1024
D_FF = 2752
N_CHIPS = 4
FF_SHARD = D_FF // N_CHIPS
FF_SLOT = 768
FF_PAD = N_CHIPS * FF_SLOT
S5_WIDTH = 512
S5_GROUP = 16
S5_GROUPS = 32
S5_STATE = 64
S5_LANES = S5_GROUPS * S5_STATE
S5_BLOCK = 512
SB_HEADS = 8
SB_DH = 64
SB_SCALE = 0.125
SB_PACK = 2
SB_QUERIES = 1024
SB_KEYS = 256
EPS = 1e-6
ADAM_LR, ADAM_B1, ADAM_B2, ADAM_EPS, ADAM_WD, ADAM_STEP = 0.001, 0.9, 0.999, 1e-08, 0.01, 10
VMEM_LIMIT = 56 * 1024 * 1024

ANY_SPEC = pl.BlockSpec(memory_space=pl.ANY)


def _params(*sem):
    return pltpu.CompilerParams(dimension_semantics=sem or None, vmem_limit_bytes=VMEM_LIMIT)


def _mm(a, b, *, name, ta=False, tb=False, out_dtype=F32, epilogue=None, extras=(), tm=1024, tn=1024, tk=1024, job=None):
    m, k = (a.shape[1], a.shape[0]) if ta else a.shape
    n = b.shape[0] if tb else b.shape[1]
    tm, tn, tk = min(tm, m), min(tn, n), min(tk, k)
    assert m % tm == 0 and n % tn == 0 and k % tk == 0, (name, m, n, k)
    nk = k // tk
    a_spec = pl.BlockSpec((tk, tm), lambda i, j, kk: (kk, i)) if ta else pl.BlockSpec((tm, tk), lambda i, j, kk: (i, kk))
    b_spec = pl.BlockSpec((tn, tk), lambda i, j, kk: (j, kk)) if tb else pl.BlockSpec((tk, tn), lambda i, j, kk: (kk, j))
    ex_specs = []
    for e in extras:
        if e.shape == (m, n):
            ex_specs.append(pl.BlockSpec((tm, tn), lambda i, j, kk: (i, j)))
        elif e.shape == (1, n):
            ex_specs.append(pl.BlockSpec((1, tn), lambda i, j, kk: (0, j)))
        else:
            assert e.shape == (m, 1), (name, e.shape)
            ex_specs.append(pl.BlockSpec((tm, 1), lambda i, j, kk: (i, 0)))
    dims = (((0 if ta else 1,), (1 if tb else 0,)), ((), ()))
    n_ex = len(extras)

    out_dtypes = list(out_dtype) if isinstance(out_dtype, (list, tuple)) else [out_dtype]
    n_out = len(out_dtypes)

    def body(a_ref, b_ref, *rest):
        ex_refs, o_refs = rest[:n_ex], rest[n_ex:n_ex + n_out]

        def product():
            return lax.dot_general(a_ref[...].astype(MXU_DTYPE), b_ref[...].astype(MXU_DTYPE), dims, preferred_element_type=F32)

        def finish(r):
            if epilogue is not None:
                r = epilogue(r, *[e[...] for e in ex_refs])
            for o_ref, val in zip(o_refs, r if isinstance(r, (tuple, list)) else (r,)):
                o_ref[...] = val.astype(o_ref.dtype)

        if nk == 1:
            finish(product())
            return
        acc_ref, kk = rest[n_ex + n_out], pl.program_id(2)

        @pl.when(kk == 0)
        def _():
            acc_ref[...] = jnp.zeros_like(acc_ref)

        acc_ref[...] += product()

        @pl.when(kk == nk - 1)
        def _():
            finish(acc_ref[...])

    res, got = _carried_call(
        body, name=name, grid=(m // tm, n // tn, nk),
        in_specs=[a_spec, b_spec, *ex_specs],
        out_specs=[pl.BlockSpec((tm, tn), lambda i, j, kk: (i, j))] * n_out,
        out_shape=[jax.ShapeDtypeStruct((m, n), dt) for dt in out_dtypes],
        scratch_shapes=[pltpu.VMEM((tm, tn), F32)] if nk > 1 else [],
        semantics=("parallel", "parallel", "arbitrary"), operands=(a, b, *extras), job=job)
    res = res if isinstance(out_dtype, (list, tuple)) else res[0]
    return res if job is None else (res, got)


def _row_block(rows, want):
    for tl in range(min(want, rows), 7, -1):
        if rows % tl == 0 and tl % 8 == 0:
            return tl
    return rows


def _rowmap(fn, ins, in_kinds, outs, *, name, tl):
    rows = next(x.shape[0] for x, kd in zip(ins, in_kinds) if kd == "r")
    tl = _row_block(rows, tl)
    n_in = len(ins)

    def spec(shape, kind):
        if kind == "r":
            return pl.BlockSpec((tl,) + tuple(shape[1:]), lambda i: (i,) + (0,) * (len(shape) - 1))
        return pl.BlockSpec(tuple(shape), lambda i: (0,) * len(shape))

    def body(*refs):
        in_refs, out_refs = refs[:n_in], refs[n_in:]
        res = fn(*[r[...] for r in in_refs])
        if not isinstance(res, (tuple, list)):
            res = (res,)
        for o_ref, val, (_, dt, kind) in zip(out_refs, res, outs):
            if kind == "r":
                o_ref[...] = val.astype(dt)
            else:
                @pl.when(pl.program_id(0) == 0)
                def _():
                    o_ref[...] = jnp.zeros_like(o_ref)

                o_ref[...] += val.astype(dt)

    has_acc = any(kd == "a" for _, _, kd in outs)
    res = pl.pallas_call(
        body, name=name, grid=(rows // tl,),
        in_specs=[spec(x.shape, kd) for x, kd in zip(ins, in_kinds)],
        out_specs=[spec(s, kd) for s, _, kd in outs],
        out_shape=[jax.ShapeDtypeStruct(s, dt) for s, dt, _ in outs],
        compiler_params=_params("arbitrary" if has_acc else "parallel"),
    )(*ins)
    return res[0] if len(outs) == 1 else res


def _rms_fwd(x, g):
    r = lax.rsqrt(jnp.mean(x * x, axis=-1, keepdims=True) + EPS)
    return x * r * g


def _rms_bwd(dh, x, g):
    r = lax.rsqrt(jnp.mean(x * x, axis=-1, keepdims=True) + EPS)
    xh = x * r
    dxh = dh * g
    dx = r * (dxh - xh * jnp.mean(dxh * xh, axis=-1, keepdims=True))
    return dx, jnp.sum(dh * xh, axis=0, keepdims=True)


def _norm(x, g, *, name):
    return _rowmap(lambda xv, gv: _rms_fwd(xv, gv), [x, g], "rc", [(x.shape, MXU_DTYPE, "r")], name=name, tl=256)


def _norm_bwd(dh, x, g, dres, *, name):
    def fn(dhv, xv, gv, drv):
        dx, dg = _rms_bwd(dhv, xv, gv)
        return dx + drv, dg
    return _rowmap(fn, [dh, x, g, dres], "rrcr", [(x.shape, F32, "r"), (g.shape, F32, "a")], name=name, tl=256)


def _swiglu_act(a, b):
    return jax.nn.silu(a) * b


def _ffn_up(x, g, wg, wu, *, name, tm=1024, tn=1024, job=None):
    m, d = x.shape
    n = wg.shape[0]
    tm, tn = min(tm, m), min(tn, n)
    assert m % tm == 0 and n % tn == 0, (name, m, n)

    def body(x_ref, g_ref, wg_ref, wu_ref, h_ref, a_ref, b_ref, s_ref):
        @pl.when(pl.program_id(1) == 0)
        def _():
            h_ref[...] = _rms_fwd(x_ref[...], g_ref[...]).astype(h_ref.dtype)

        hv = h_ref[...]
        av = lax.dot_general(hv, wg_ref[...], NT_DIMS, preferred_element_type=F32)
        bv = lax.dot_general(hv, wu_ref[...], NT_DIMS, preferred_element_type=F32)
        a_ref[...] = av.astype(a_ref.dtype)
        b_ref[...] = bv.astype(b_ref.dtype)
        s_ref[...] = _swiglu_act(av, bv).astype(s_ref.dtype)

    rows = pl.BlockSpec((tm, d), lambda i, j: (i, 0))
    wgt = pl.BlockSpec((tn, d), lambda i, j: (j, 0))
    tile = pl.BlockSpec((tm, tn), lambda i, j: (i, j))
    return _carried_call(
        body, name=name, grid=(m // tm, n // tn),
        in_specs=[rows, pl.BlockSpec((1, d), lambda i, j: (0, 0)), wgt, wgt],
        out_specs=[rows, tile, tile, tile],
        out_shape=[jax.ShapeDtypeStruct((m, d), MXU_DTYPE)] + [jax.ShapeDtypeStruct((m, n), MXU_DTYPE)] * 3,
        semantics=("parallel", "arbitrary"), operands=(x, g, wg, wu), job=job)


def _ffn_dx(da, db, wg, wu, x, g, dres, *, name, tm=1024, tk=1024, job=None):
    m, f = da.shape
    d = wg.shape[1]
    tm, tk = min(tm, m), min(tk, f)
    assert m % tm == 0 and f % tk == 0, (name, m, f)
    nk = f // tk

    def body(da_ref, db_ref, wg_ref, wu_ref, x_ref, g_ref, dr_ref, dx_ref, dg_ref, acc_ref):
        i, kk = pl.program_id(0), pl.program_id(1)

        @pl.when(kk == 0)
        def _():
            acc_ref[...] = jnp.zeros_like(acc_ref)

        acc_ref[...] += jnp.dot(da_ref[...], wg_ref[...], preferred_element_type=F32)
        acc_ref[...] += jnp.dot(db_ref[...], wu_ref[...], preferred_element_type=F32)

        @pl.when(jnp.logical_and(i == 0, kk == 0))
        def _():
            dg_ref[...] = jnp.zeros_like(dg_ref)

        @pl.when(kk == nk - 1)
        def _():
            dx, dg = _rms_bwd(acc_ref[...], x_ref[...], g_ref[...])
            dx_ref[...] = dx + dr_ref[...]
            dg_ref[...] += dg

    act = pl.BlockSpec((tm, tk), lambda i, kk: (i, kk))
    wgt = pl.BlockSpec((tk, d), lambda i, kk: (kk, 0))
    rows = pl.BlockSpec((tm, d), lambda i, kk: (i, 0))
    one = pl.BlockSpec((1, d), lambda i, kk: (0, 0))
    return _carried_call(
        body, name=name, grid=(m // tm, nk),
        in_specs=[act, act, wgt, wgt, rows, one, rows],
        out_specs=[rows, one],
        out_shape=[jax.ShapeDtypeStruct((m, d), F32), jax.ShapeDtypeStruct((1, d), F32)],
        scratch_shapes=[pltpu.VMEM((tm, d), F32)],
        semantics=("arbitrary", "arbitrary"), operands=(da, db, wg, wu, x, g, dres), job=job)


def _ffn_fwd(x, g, full, which, layer, late):
    tag = f"{which}_{layer}"
    job, keys = late.gather_job(f"{tag}_up") if late else (None, [])
    (h, a, b, s), got = _ffn_up(x, g, full[(f"{which}_w_gate", layer)], full[(f"{which}_w_up", layer)], name=f"{tag}_up", job=job)
    full.update(zip(keys, got))
    x2 = _mm(s, full[(f"{which}_w_down", layer)], name=f"{tag}_down", epilogue=lambda acc, xv: xv + 0.5 * acc, extras=[x])
    return x2, (x, h, a, b, s)


def _ffn_bwd(dx2, saved, g, full, which, layer, grads, late, inline):
    x, h, a, b, s = saved
    tag = f"{which}_{layer}"
    kg, ku, kd = [(f"{which}_w_{n}", layer) for n in ("gate", "up", "down")]
    wg, wu, wd = full[kg], full[ku], full[kd]
    send = (lambda: late.scatter_job(grads)) if (late and inline) else (lambda: (None, []))

    def act_bwd(ds, av, bv):
        _, vjp = jax.vjp(_swiglu_act, av.astype(F32), bv.astype(F32))
        return vjp(0.5 * ds)

    grads[kd] = _mm(s, dx2, ta=True, name=f"{tag}_dwd", out_dtype=WIRE_DTYPE, epilogue=lambda acc: 0.5 * acc, tk=2048)
    job, keys = send()
    (da, db), got = _carried(_mm, dx2, wd, tb=True, name=f"{tag}_dact", epilogue=act_bwd, extras=[a, b],
                             out_dtype=[MXU_DTYPE, MXU_DTYPE], job=job)
    _note(late, keys, got)
    grads[kg] = _mm(da, h, ta=True, name=f"{tag}_dwg", out_dtype=WIRE_DTYPE, tk=2048)
    job, keys = send()
    grads[ku], got = _carried(_mm, db, h, ta=True, name=f"{tag}_dwu", out_dtype=WIRE_DTYPE, tk=2048, job=job)
    _note(late, keys, got)
    job, keys = send()
    (dx, dg), got = _ffn_dx(da, db, wg, wu, x, g, dx2, name=f"{tag}_dx", job=job)
    _note(late, keys, got)
    return dx, dg


def _carried(fn, *args, job, **kwargs):
    return fn(*args, job=job, **kwargs) if job is not None else (fn(*args, **kwargs), [])


def _note(late, keys, got):
    if late:
        late.received.update(zip(keys, got))


def _softplus(z):
    return jnp.maximum(z, 0.0) + jnp.log(1.0 + jnp.exp(-jnp.abs(z)))


def _ones_dot(x, tri):
    if MXU_DTYPE == F32:
        return jnp.dot(x, tri, preferred_element_type=F32)
    hi = x.astype(MXU_DTYPE)
    lo = (x - hi.astype(F32)).astype(MXU_DTYPE)
    return jnp.dot(hi, tri, preferred_element_type=F32) + jnp.dot(lo, tri, preferred_element_type=F32)


NT_DIMS = (((1,), (1,)), ((), ()))
TN_DIMS = (((0,), (0,)), ((), ()))


def _sb_fwd(q, k, v, *, tq=SB_QUERIES, job=None):
    nh, seq, dh = q.shape
    tq = min(tq, seq)
    hp = SB_PACK

    tk = min(SB_KEYS, tq)
    per = tq // tk

    def body(q_ref, k_ref, v_ref, o_ref, ls_ref):
        i = pl.program_id(1)
        r_idx = lax.broadcasted_iota(jnp.int32, (tk, tk), 0)
        c_idx = lax.broadcasted_iota(jnp.int32, (tk, tk), 1)
        after = (r_idx > c_idx).astype(MXU_DTYPE)

        def block(hd, j, c, acc, straddles):
            off = pl.multiple_of(j * tk, tk)
            kv = k_ref[hd, pl.ds(off, tk), :]
            vv = v_ref[hd, pl.ds(off, tk), :]
            top = 0 if straddles is None else straddles * tk
            rows = tq - top
            z = lax.dot_general(q_ref[hd, pl.ds(top, rows), :], kv, NT_DIMS, preferred_element_type=F32)
            sp = _softplus(z)
            c_in = c[top:]
            if straddles is None:
                lk = -sp
                w = jnp.exp(z - sp + _ones_dot(lk, after) + c_in)
            else:
                before = lax.broadcasted_iota(jnp.int32, (rows, tk), 1) < lax.broadcasted_iota(jnp.int32, (rows, tk), 0)
                lk = jnp.where(before, -sp, 0.0)
                w = jnp.where(before, jnp.exp(z - sp + _ones_dot(lk, after) + c_in), 0.0)
            acc_new = acc[top:] + jnp.dot(w.astype(MXU_DTYPE), vv, preferred_element_type=F32)
            c_new = c_in + jnp.sum(lk, axis=1, keepdims=True)
            if top:
                c_new, acc_new = jnp.concatenate([c[:top], c_new], axis=0), jnp.concatenate([acc[:top], acc_new], axis=0)
            return c_new, acc_new

        def step(n, carry):
            return tuple(block(hd, i * per - 1 - n, *carry[hd], None) for hd in range(hp))

        carry = tuple((jnp.zeros((tq, 1), F32), jnp.zeros((tq, dh), F32)) for _ in range(hp))
        for s in reversed(range(per)):
            carry = tuple(block(hd, i * per + s, *carry[hd], s) for hd in range(hp))
        for hd, (c, acc) in enumerate(lax.fori_loop(0, i * per, step, carry)):
            o_ref[hd] = acc
            ls_ref[hd] = c

    whole = pl.BlockSpec((hp, seq, dh), lambda h, i: (h, 0, 0))
    return _carried_call(
        body, name="sb_fwd", grid=(nh // hp, seq // tq),
        in_specs=[pl.BlockSpec((hp, tq, dh), lambda h, i: (h, i, 0)), whole, whole],
        out_specs=[pl.BlockSpec((hp, tq, dh), lambda h, i: (h, i, 0)), pl.BlockSpec((hp, tq, 1), lambda h, i: (h, i, 0))],
        out_shape=[jax.ShapeDtypeStruct((nh, seq, dh), F32), jax.ShapeDtypeStruct((nh, seq, 1), F32)],
        semantics=("parallel", "parallel"), operands=(q, k, v), job=job)


def _sb_bwd(q, k, v, lsum, do, *, tq=SB_QUERIES, job=None):
    nh, seq, dh = q.shape
    tq = min(tq, seq)
    tk = min(SB_KEYS, tq)
    per = tq // tk
    hp = SB_PACK
    scale = SB_SCALE

    def body(q_ref, k_ref, v_ref, ls_ref, do_ref, dq_ref, dk_ref, dv_ref):
        i = pl.program_id(1)

        @pl.when(i == 0)
        def _():
            dk_ref[...] = jnp.zeros_like(dk_ref)
            dv_ref[...] = jnp.zeros_like(dv_ref)

        r_idx = lax.broadcasted_iota(jnp.int32, (tk, tk), 0)
        c_idx = lax.broadcasted_iota(jnp.int32, (tk, tk), 1)
        upto = (r_idx <= c_idx).astype(MXU_DTYPE)
        before = (r_idx < c_idx).astype(MXU_DTYPE)

        def block(hd, j, cp, ce, dq, straddles):
            off = pl.multiple_of(j * tk, tk)
            top = 0 if straddles is None else straddles * tk
            rows = tq - top
            part = pl.ds(top, rows)
            qv, dov = q_ref[hd, part, :], do_ref[hd, part, :].astype(MXU_DTYPE)
            kv = k_ref[hd, pl.ds(off, tk), :]
            vv = v_ref[hd, pl.ds(off, tk), :]
            z = lax.dot_general(qv, kv, NT_DIMS, preferred_element_type=F32)
            sp = _softplus(z)
            valid = None
            if straddles is not None:
                valid = lax.broadcasted_iota(jnp.int32, (rows, tk), 1) < lax.broadcasted_iota(jnp.int32, (rows, tk), 0)
            lk = -sp if valid is None else jnp.where(valid, -sp, 0.0)
            w = jnp.exp(z - sp + (ls_ref[hd, part, :] - cp[top:]) - _ones_dot(lk, upto))
            if valid is not None:
                w = jnp.where(valid, w, 0.0)
            e = w * lax.dot_general(dov, vv, NT_DIMS, preferred_element_type=F32)
            earlier = _ones_dot(e, before) + ce[top:]
            keep = jnp.exp(-sp)
            dz = e * keep - (1.0 - keep) * earlier
            if valid is not None:
                dz = jnp.where(valid, dz, 0.0)
            dzm = dz.astype(MXU_DTYPE)
            dk_ref[hd, pl.ds(off, tk), :] += lax.dot_general(dzm, qv, TN_DIMS, preferred_element_type=F32)
            dv_ref[hd, pl.ds(off, tk), :] += lax.dot_general(w.astype(MXU_DTYPE), dov, TN_DIMS, preferred_element_type=F32)
            new = (cp[top:] + jnp.sum(lk, axis=1, keepdims=True), ce[top:] + jnp.sum(e, axis=1, keepdims=True),
                   dq[top:] + jnp.dot(dzm, kv, preferred_element_type=F32))
            if top:
                new = tuple(jnp.concatenate([old[:top], val], axis=0) for old, val in zip((cp, ce, dq), new))
            return new

        def step(j, carry):
            return tuple(block(hd, j, *carry[hd], None) for hd in range(hp))

        zero = jnp.zeros((tq, 1), F32)
        carry = lax.fori_loop(0, i * per, step, tuple((zero, zero, jnp.zeros((tq, dh), F32)) for _ in range(hp)))
        for s in range(per):
            carry = tuple(block(hd, i * per + s, *carry[hd], s) for hd in range(hp))
        for hd in range(hp):
            dq_ref[hd] = carry[hd][2] * scale

    whole = pl.BlockSpec((hp, seq, dh), lambda h, i: (h, 0, 0))
    tile = pl.BlockSpec((hp, tq, dh), lambda h, i: (h, i, 0))
    return _carried_call(
        body, name="sb_bwd", grid=(nh // hp, seq // tq),
        in_specs=[tile, whole, whole, pl.BlockSpec((hp, tq, 1), lambda h, i: (h, i, 0)), tile],
        out_specs=[tile, whole, whole],
        out_shape=[jax.ShapeDtypeStruct((nh, seq, dh), F32)] * 3,
        semantics=("parallel", "arbitrary"), operands=(q, k, v, lsum, do), job=job)


SB_LANES = SB_PACK * SB_DH
Q_COL, K_COL, V_COL = (S5_WIDTH * n // SB_LANES for n in (1, 2, 3))


def _head_lanes(rows, hd):
    return lax.broadcasted_iota(jnp.int32, (rows, SB_LANES), 1) // SB_DH == hd


def _attend(proj, *, tq=SB_QUERIES, job=None):
    seq = proj.shape[0]
    tq = min(tq, seq)
    tk = min(SB_KEYS, tq)
    per, hp = tq // tk, SB_PACK

    def body(q_ref, k_ref, v_ref, o_ref, ls_ref):
        i = pl.program_id(1)
        r_idx = lax.broadcasted_iota(jnp.int32, (tk, tk), 0)
        c_idx = lax.broadcasted_iota(jnp.int32, (tk, tk), 1)
        after = (r_idx > c_idx).astype(MXU_DTYPE)
        lanes = [_head_lanes(tk, hd) for hd in range(hp)]

        def block(j, cs, acc, straddles):
            off = pl.multiple_of(j * tk, tk)
            k2, v2 = k_ref[pl.ds(off, tk), :], v_ref[pl.ds(off, tk), :]
            top = 0 if straddles is None else straddles * tk
            rows = tq - top
            q2 = (q_ref[pl.ds(top, rows), :] * SB_SCALE).astype(MXU_DTYPE)
            new_cs, out = [], acc[top:]
            for hd in range(hp):
                kv = jnp.where(lanes[hd], k2, 0.0).astype(MXU_DTYPE)
                vv = jnp.where(lanes[hd], v2, 0.0).astype(MXU_DTYPE)
                z = lax.dot_general(q2, kv, NT_DIMS, preferred_element_type=F32)
                sp = _softplus(z)
                c_in = cs[hd][top:]
                if straddles is None:
                    lk = -sp
                    w = jnp.exp(z - sp + _ones_dot(lk, after) + c_in)
                else:
                    before = lax.broadcasted_iota(jnp.int32, (rows, tk), 1) < lax.broadcasted_iota(jnp.int32, (rows, tk), 0)
                    lk = jnp.where(before, -sp, 0.0)
                    w = jnp.where(before, jnp.exp(z - sp + _ones_dot(lk, after) + c_in), 0.0)
                out = out + jnp.dot(w.astype(MXU_DTYPE), vv, preferred_element_type=F32)
                c_new = c_in + jnp.sum(lk, axis=1, keepdims=True)
                new_cs.append(jnp.concatenate([cs[hd][:top], c_new], axis=0) if top else c_new)
            return tuple(new_cs), (jnp.concatenate([acc[:top], out], axis=0) if top else out)

        carry = (tuple(jnp.zeros((tq, 1), F32) for _ in range(hp)), jnp.zeros((tq, SB_LANES), F32))
        for s in reversed(range(per)):
            carry = block(i * per + s, *carry, s)
        cs, acc = lax.fori_loop(0, i * per, lambda n, cr: block(i * per - 1 - n, *cr, None), carry)
        o_ref[...] = acc
        for hd in range(hp):
            ls_ref[hd] = cs[hd]

    whole = lambda col: pl.BlockSpec((seq, SB_LANES), lambda g, i: (0, col + g))
    return _carried_call(
        body, name="sb_fwd", grid=(SB_HEADS // hp, seq // tq),
        in_specs=[pl.BlockSpec((tq, SB_LANES), lambda g, i: (i, Q_COL + g)), whole(K_COL), whole(V_COL)],
        out_specs=[pl.BlockSpec((tq, SB_LANES), lambda g, i: (i, g)), pl.BlockSpec((hp, tq, 1), lambda g, i: (g, i, 0))],
        out_shape=[jax.ShapeDtypeStruct((seq, SB_HEADS * SB_DH), F32), jax.ShapeDtypeStruct((SB_HEADS, seq, 1), F32)],
        semantics=("parallel", "parallel"), operands=(proj, proj, proj), job=job)


def _attend_bwd(proj, lsum, dmix, *, tq=SB_QUERIES, job=None):
    seq = proj.shape[0]
    tq = min(tq, seq)
    tk = min(SB_KEYS, tq)
    per, hp = tq // tk, SB_PACK
    do_col = S5_WIDTH // SB_LANES

    def body(q_ref, k_ref, v_ref, ls_ref, do_ref, dq_ref, dk_ref, dv_ref):
        i = pl.program_id(1)

        @pl.when(i == 0)
        def _():
            dk_ref[...] = jnp.zeros_like(dk_ref)
            dv_ref[...] = jnp.zeros_like(dv_ref)

        r_idx = lax.broadcasted_iota(jnp.int32, (tk, tk), 0)
        c_idx = lax.broadcasted_iota(jnp.int32, (tk, tk), 1)
        upto = (r_idx <= c_idx).astype(MXU_DTYPE)
        before = (r_idx < c_idx).astype(MXU_DTYPE)
        lanes = [_head_lanes(tk, hd) for hd in range(hp)]

        def block(j, sums, dq, straddles):
            off = pl.multiple_of(j * tk, tk)
            k2, v2 = k_ref[pl.ds(off, tk), :], v_ref[pl.ds(off, tk), :]
            top = 0 if straddles is None else straddles * tk
            rows = tq - top
            part = pl.ds(top, rows)
            q2 = (q_ref[part, :] * SB_SCALE).astype(MXU_DTYPE)
            do2 = do_ref[part, :].astype(MXU_DTYPE)
            valid = None
            if straddles is not None:
                valid = lax.broadcasted_iota(jnp.int32, (rows, tk), 1) < lax.broadcasted_iota(jnp.int32, (rows, tk), 0)
            new_sums, out, dk, dv = [], dq[top:], jnp.zeros((tk, SB_LANES), F32), jnp.zeros((tk, SB_LANES), F32)
            for hd in range(hp):
                cp, ce = sums[hd]
                kv = jnp.where(lanes[hd], k2, 0.0).astype(MXU_DTYPE)
                vv = jnp.where(lanes[hd], v2, 0.0).astype(MXU_DTYPE)
                z = lax.dot_general(q2, kv, NT_DIMS, preferred_element_type=F32)
                sp = _softplus(z)
                lk = -sp if valid is None else jnp.where(valid, -sp, 0.0)
                w = jnp.exp(z - sp + (ls_ref[hd, part, :] - cp[top:]) - _ones_dot(lk, upto))
                if valid is not None:
                    w = jnp.where(valid, w, 0.0)
                e = w * lax.dot_general(do2, vv, NT_DIMS, preferred_element_type=F32)
                earlier = _ones_dot(e, before) + ce[top:]
                keep = jnp.exp(-sp)
                dz = e * keep - (1.0 - keep) * earlier
                if valid is not None:
                    dz = jnp.where(valid, dz, 0.0)
                dzm = dz.astype(MXU_DTYPE)
                out = out + jnp.dot(dzm, kv, preferred_element_type=F32)
                dk = dk + jnp.where(lanes[hd], lax.dot_general(dzm, q2, TN_DIMS, preferred_element_type=F32), 0.0)
                dv = dv + jnp.where(lanes[hd], lax.dot_general(w.astype(MXU_DTYPE), do2, TN_DIMS, preferred_element_type=F32), 0.0)
                new = (cp[top:] + jnp.sum(lk, axis=1, keepdims=True), ce[top:] + jnp.sum(e, axis=1, keepdims=True))
                new_sums.append(tuple(jnp.concatenate([old[:top], val], axis=0) for old, val in zip((cp, ce), new)) if top else new)
            dk_ref[pl.ds(off, tk), :] += dk
            dv_ref[pl.ds(off, tk), :] += dv
            return tuple(new_sums), (jnp.concatenate([dq[:top], out], axis=0) if top else out)

        zero = jnp.zeros((tq, 1), F32)
        carry = (tuple((zero, zero) for _ in range(hp)), jnp.zeros((tq, SB_LANES), F32))
        carry = lax.fori_loop(0, i * per, lambda j, cr: block(j, *cr, None), carry)
        for s in range(per):
            carry = block(i * per + s, *carry, s)
        dq_ref[...] = carry[1] * SB_SCALE

    whole = lambda col: pl.BlockSpec((seq, SB_LANES), lambda g, i: (0, col + g))
    tile = lambda col: pl.BlockSpec((tq, SB_LANES), lambda g, i: (i, col + g))
    acc = pl.BlockSpec((seq, SB_LANES), lambda g, i: (0, g))
    return _carried_call(
        body, name="sb_bwd", grid=(SB_HEADS // hp, seq // tq),
        in_specs=[tile(Q_COL), whole(K_COL), whole(V_COL), pl.BlockSpec((hp, tq, 1), lambda g, i: (g, i, 0)), tile(do_col)],
        out_specs=[tile(0), acc, acc],
        out_shape=[jax.ShapeDtypeStruct((seq, SB_HEADS * SB_DH), F32)] * 3,
        semantics=("parallel", "arbitrary"), operands=(proj, proj, proj, lsum, dmix), job=job)


def _s5_disc(lr, li, ldt, br, bi):
    dt = jnp.exp(ldt)
    mag = jnp.exp(lr * dt)
    ar = mag * jnp.cos(li * dt)
    ai = mag * jnp.sin(li * dt)
    den = lr * lr + li * li
    nr = ar - 1.0
    cr = (nr * lr + ai * li) / den
    ci = (ai * lr - nr * li) / den
    return ar, ai, cr[None] * br - ci[None] * bi, cr[None] * bi + ci[None] * br


def _s5_prep(lr, li, ldt, br, bi):
    shapes = [lr.shape, lr.shape, br.shape, br.shape]

    def body(lr_ref, li_ref, ldt_ref, br_ref, bi_ref, *outs):
        for o, val in zip(outs, _s5_disc(lr_ref[...], li_ref[...], ldt_ref[...], br_ref[...], bi_ref[...])):
            o[...] = val

    return pl.pallas_call(body, name="s5_prep", out_shape=[jax.ShapeDtypeStruct(s, F32) for s in shapes])(lr, li, ldt, br, bi)


def _s5_prep_bwd(lr, li, ldt, br, bi, cts):
    args = (lr, li, ldt, br, bi)

    def body(*refs):
        ins, ct_refs, outs = refs[:5], refs[5:9], refs[9:]
        _, vjp = jax.vjp(_s5_disc, *[r[...] for r in ins])
        for o, val in zip(outs, vjp(tuple(r[...] for r in ct_refs))):
            o[...] = val

    return pl.pallas_call(body, name="s5_prep_bwd", out_shape=[jax.ShapeDtypeStruct(a.shape, F32) for a in args])(*args, *cts)


def _s5_scan(bu, a, *, tc=512):
    seq, w2 = bu.shape
    tw = S5_BLOCK
    tc = min(tc, seq)
    assert seq % tc == 0 and w2 % (2 * tw) == 0

    def body(bu_ref, a_ref, h_ref, cr_ref, ci_ref):
        @pl.when(pl.program_id(1) == 0)
        def _():
            cr_ref[...] = jnp.zeros_like(cr_ref)
            ci_ref[...] = jnp.zeros_like(ci_ref)

        re, im = pl.ds(0, tw), pl.ds(tw, tw)
        ar, ai = a_ref[:, re], a_ref[:, im]

        def step(t, carry):
            hr, hi = carry
            row = pl.ds(t, 1)
            nr = ar * hr - ai * hi + bu_ref[row, re]
            ni = ar * hi + ai * hr + bu_ref[row, im]
            h_ref[row, re] = nr
            h_ref[row, im] = ni
            return nr, ni

        hr, hi = lax.fori_loop(0, tc, step, (cr_ref[...], ci_ref[...]), unroll=8)
        cr_ref[...] = hr
        ci_ref[...] = hi

    blk = pl.BlockSpec((tc, 2 * tw), lambda j, t: (t, j))
    return pl.pallas_call(
        body, name="s5_scan", grid=(w2 // (2 * tw), seq // tc),
        in_specs=[blk, pl.BlockSpec((1, 2 * tw), lambda j, t: (0, j))],
        out_specs=blk,
        out_shape=jax.ShapeDtypeStruct((seq, w2), F32),
        scratch_shapes=[pltpu.VMEM((1, tw), F32)] * 2,
        compiler_params=_params("parallel", "arbitrary"),
    )(bu, a)


def _s5_scan_bwd(d, h, a, *, tc=512):
    seq, w2 = d.shape
    tw = S5_BLOCK
    tc = min(tc, seq)
    assert seq % tc == 0 and w2 % (2 * tw) == 0
    nt = seq // tc

    def body(d_ref, h_ref, a_ref, g_ref, da_ref, cr_ref, ci_ref):
        @pl.when(pl.program_id(1) == 0)
        def _():
            cr_ref[...] = jnp.zeros_like(cr_ref)
            ci_ref[...] = jnp.zeros_like(ci_ref)
            da_ref[...] = jnp.zeros_like(da_ref)

        re, im = pl.ds(0, tw), pl.ds(tw, tw)
        ar, ai = a_ref[:, re], a_ref[:, im]

        def step(n, carry):
            gr, gi, sr, si = carry
            row = pl.ds(tc - 1 - n, 1)
            hrt, hit = h_ref[row, re], h_ref[row, im]
            sr = sr + gr * hrt + gi * hit
            si = si + gi * hrt - gr * hit
            ngr = d_ref[row, re] + ar * gr + ai * gi
            ngi = d_ref[row, im] + ar * gi - ai * gr
            g_ref[row, re] = ngr
            g_ref[row, im] = ngi
            return ngr, ngi, sr, si

        gr, gi, sr, si = lax.fori_loop(0, tc, step, (cr_ref[...], ci_ref[...], da_ref[:, re], da_ref[:, im]), unroll=8)
        cr_ref[...] = gr
        ci_ref[...] = gi
        da_ref[:, re] = sr
        da_ref[:, im] = si

    blk = pl.BlockSpec((tc, 2 * tw), lambda j, t: (nt - 1 - t, j))
    row = pl.BlockSpec((1, 2 * tw), lambda j, t: (0, j))
    return pl.pallas_call(
        body, name="s5_scan_bwd", grid=(w2 // (2 * tw), nt),
        in_specs=[blk, blk, row],
        out_specs=[blk, row],
        out_shape=[jax.ShapeDtypeStruct((seq, w2), F32), jax.ShapeDtypeStruct((1, w2), F32)],
        scratch_shapes=[pltpu.VMEM((1, tw), F32)] * 2,
        compiler_params=_params("parallel", "arbitrary"),
    )(d, h, a)


def _pair_columns(re, im, axis):
    shape = re.shape
    split = shape[:axis] + (shape[axis] // S5_BLOCK, S5_BLOCK) + shape[axis + 1:]
    both = jnp.stack([re.reshape(split), im.reshape(split)], axis=axis + 1)
    return both.reshape(shape[:axis] + (2 * shape[axis],) + shape[axis + 1:])


def _unpair_columns(t, axis):
    shape = t.shape
    both = t.reshape(shape[:axis] + (shape[axis] // (2 * S5_BLOCK), 2, S5_BLOCK) + shape[axis + 1:])
    half = shape[:axis] + (shape[axis] // 2,) + shape[axis + 1:]
    return (lax.index_in_dim(both, 0, axis + 1, keepdims=False).reshape(half),
            lax.index_in_dim(both, 1, axis + 1, keepdims=False).reshape(half))


def _block_diag(t):
    g, a, b = t.shape
    eye = jnp.eye(g, dtype=t.dtype)
    return (t[:, :, None, :] * eye[:, None, :, None]).reshape(g * a, g * b)


def _block_diag_part(m, g):
    a, b = m.shape[0] // g, m.shape[1] // g
    return jnp.moveaxis(jnp.diagonal(m.reshape(g, a, g, b), axis1=0, axis2=2), -1, 0)


def _gelu_glu(y, gate_pre):
    z = jax.nn.gelu(y)
    return z * jax.nn.sigmoid(gate_pre)


def _s5_fwd(u, p, w_glu):
    lr, li = p["s5_lambda_re"][0], p["s5_lambda_im"][0]
    ldt = p["s5_log_dt"][0][:, None]
    br = p["s5_b_re"][0].transpose(2, 0, 1)
    bi = p["s5_b_im"][0].transpose(2, 0, 1)
    ar, ai, bbr, bbi = _s5_prep(lr, li, ldt, br, bi)
    a = _pair_columns(ar.reshape(1, S5_LANES), ai.reshape(1, S5_LANES), 1)
    bmat = _pair_columns(_block_diag(bbr.transpose(1, 0, 2)), _block_diag(bbi.transpose(1, 0, 2)), 1)
    cmat = _pair_columns(_block_diag(p["s5_c_re"][0].transpose(0, 2, 1)),
                         -_block_diag(p["s5_c_im"][0].transpose(0, 2, 1)), 0)
    bmat, cmat = bmat.astype(MXU_DTYPE), cmat.astype(MXU_DTYPE)
    bu = _mm(u, bmat, name="s5_bu")
    h = _s5_scan(bu, a)
    d = p["s5_d"]
    y = _mm(h, cmat, name="s5_y", epilogue=lambda acc, uv, dv: acc + dv * uv, extras=[u, d])
    z = _rowmap(jax.nn.gelu, [y], "r", [(y.shape, MXU_DTYPE, "r")], name="s5_gelu", tl=512)
    gate_pre = _mm(z, w_glu, name="s5_glu")
    out = _rowmap(_gelu_glu, [y, gate_pre], "rr", [(y.shape, F32, "r")], name="s5_gate", tl=512)
    return out, (u, lr, li, ldt, br, bi, a, bmat, cmat, h, y, z, gate_pre)


def _s5_bwd(dout, saved, p, w_glu):
    u, lr, li, ldt, br, bi, a, bmat, cmat, h, y, z, gate_pre = saved
    d = p["s5_d"]

    def gate_bwd(dov, yv, gv):
        zv = jax.nn.gelu(yv)
        sg = jax.nn.sigmoid(gv)
        return dov * sg, dov * zv * sg * (1.0 - sg)

    dz_direct, dgate = _rowmap(gate_bwd, [dout, y, gate_pre], "rrr", [(y.shape, F32, "r"), (y.shape, MXU_DTYPE, "r")],
                               name="s5_gate_bwd", tl=512)
    dw_glu = _mm(z, dgate, ta=True, name="s5_dwglu", out_dtype=WIRE_DTYPE)
    dz = _mm(dgate, w_glu, tb=True, name="s5_dz", epilogue=lambda acc, prev: acc + prev, extras=[dz_direct])

    def gelu_bwd(dzv, yv, uv, dvv):
        _, vjp = jax.vjp(jax.nn.gelu, yv)
        dy = vjp(dzv)[0]
        return dy, dy * dvv, jnp.sum(dy * uv, axis=0, keepdims=True)

    dy, du_skip, dd = _rowmap(gelu_bwd, [dz, y, u, d], "rrrc",
                              [(y.shape, F32, "r"), (y.shape, F32, "r"), (d.shape, F32, "a")], name="s5_gelu_bwd", tl=512)
    dcmat = _mm(h, dy, ta=True, name="s5_dc")
    dstate = _mm(dy, cmat, tb=True, name="s5_dstate")
    g, da = _s5_scan_bwd(dstate, h, a)
    du = _mm(g, bmat, tb=True, name="s5_du", epilogue=lambda acc, prev: acc + prev, extras=[du_skip], out_dtype=MXU_DTYPE)
    dbmat = _mm(u, g, ta=True, name="s5_db")
    dbbr, dbbi = (_block_diag_part(t, S5_GROUPS).transpose(1, 0, 2) for t in _unpair_columns(dbmat, 1))
    dar, dai = _unpair_columns(da, 1)
    cts = (dar.reshape(S5_GROUPS, S5_STATE), dai.reshape(S5_GROUPS, S5_STATE), dbbr, dbbi)
    dlr, dli, dldt, dbr, dbi = _s5_prep_bwd(lr, li, ldt, br, bi, cts)
    dcr, dci = (_block_diag_part(t, S5_GROUPS).transpose(0, 2, 1) for t in _unpair_columns(dcmat, 0))
    grads = {
        "s5_lambda_re": dlr[None], "s5_lambda_im": dli[None], "s5_log_dt": dldt[:, 0][None],
        "s5_b_re": dbr.transpose(1, 2, 0)[None], "s5_b_im": dbi.transpose(1, 2, 0)[None],
        "s5_c_re": dcr[None], "s5_c_im": -dci[None], "s5_d": dd,
    }
    return du, dw_glu, grads


def _mix0_fwd(x, g, p, full, late):
    h = _norm(x, g, name="mix0_norm")
    proj = _mm(h, full[("ab_w_in", 0)], name="mix0_in")
    u = proj[:, :S5_WIDTH]
    job, keys = late.gather_job("sb_fwd") if late else (None, [])
    (o, lsum), got = _attend(proj, job=job)
    full.update(zip(keys, got))
    w_glu, w_out = full[("s5_w_glu", 0)], full[("ab_w_out", 0)]
    y_a, s5_saved = _s5_fwd(u, p, w_glu)
    mix = jnp.concatenate([y_a, o], axis=1).astype(MXU_DTYPE)
    x2 = _mm(mix, w_out, name="mix0_out", epilogue=lambda acc, xv: xv + acc, extras=[x])
    return x2, (x, h, proj, lsum, mix, s5_saved)


def _mix0_bwd(dx2, saved, g, p, full, grads, late):
    x, h, proj, lsum, mix, s5_saved = saved
    w_in, w_glu, w_out = full[("ab_w_in", 0)], full[("s5_w_glu", 0)], full[("ab_w_out", 0)]
    dmix = _mm(dx2, w_out, tb=True, name="mix0_dmix")
    grads[("ab_w_out", 0)] = _mm(mix, dx2, ta=True, name="mix0_dwout", out_dtype=WIRE_DTYPE)
    du, grads[("s5_w_glu", 0)], s5_grads = _s5_bwd(dmix[:, :S5_WIDTH], s5_saved, p, w_glu)
    job, keys = late.scatter_job(grads) if late else (None, [])
    (dq, dk, dv), got = _attend_bwd(proj, lsum, dmix, job=job)
    _note(late, keys, got)
    dproj = jnp.concatenate([du] + [t.astype(MXU_DTYPE) for t in (dq, dk, dv)], axis=1)
    grads[("ab_w_in", 0)] = _mm(h, dproj, ta=True, name="mix0_dwin", out_dtype=WIRE_DTYPE)
    job, keys = late.scatter_job(grads) if late else (None, [])
    dh, got = _carried(_mm, dproj, w_in, tb=True, name="mix0_dh", job=job)
    _note(late, keys, got)
    dx, dg = _norm_bwd(dh, x, g, dx2, name="mix0_norm_bwd")
    return dx, dg, s5_grads


def _shift_down(t, n):
    rows = lax.broadcasted_iota(jnp.int32, t.shape, 0)
    return jnp.where(rows >= n, pltpu.roll(t, n, 0), 0.0)


def _shift_up(t, n):
    rows = lax.broadcasted_iota(jnp.int32, t.shape, 0)
    return jnp.where(rows < t.shape[0] - n, pltpu.roll(t, t.shape[0] - n, 0), 0.0)


def _conv_fwd(proj, cw, *, tc=128):
    seq, c3 = proj.shape
    ch = c3 // 3
    nb = ch // tc

    def body(b_ref, c_ref, v_ref, w_ref, m_ref):
        pv = c_ref[...] * v_ref[...]
        w = w_ref[...]
        y = w[2:3] * pv + w[1:2] * _shift_down(pv, 1) + w[0:1] * _shift_down(pv, 2)
        m_ref[...] = (b_ref[...] * y).astype(m_ref.dtype)

    col = lambda part: pl.BlockSpec((seq, tc), lambda j: (0, part * nb + j))
    return pl.pallas_call(
        body, name="conv_fwd", grid=(nb,),
        in_specs=[col(0), col(1), col(2), pl.BlockSpec((3, tc), lambda j: (0, j))],
        out_specs=pl.BlockSpec((seq, tc), lambda j: (0, j)),
        out_shape=jax.ShapeDtypeStruct((seq, ch), MXU_DTYPE),
        compiler_params=_params("parallel"),
    )(proj, proj, proj, cw)


def _conv_bwd(proj, cw, dm, *, tc=128):
    seq, c3 = proj.shape
    ch = c3 // 3
    nb = ch // tc

    def body(b_ref, c_ref, v_ref, w_ref, dm_ref, dproj_ref, dw_ref, dc_ref, dv_ref):
        part = pl.program_id(1)

        @pl.when(part == 0)
        def _():
            cv, vv, dmv = c_ref[...], v_ref[...], dm_ref[...]
            pv = cv * vv
            w = w_ref[...]
            p1, p2 = _shift_down(pv, 1), _shift_down(pv, 2)
            y = w[2:3] * pv + w[1:2] * p1 + w[0:1] * p2
            dproj_ref[...] = (dmv * y).astype(dproj_ref.dtype)
            dy = dmv * b_ref[...]
            dp = w[2:3] * dy + w[1:2] * _shift_up(dy, 1) + w[0:1] * _shift_up(dy, 2)
            dc_ref[...] = (dp * vv).astype(dc_ref.dtype)
            dv_ref[...] = (dp * cv).astype(dv_ref.dtype)
            dw_ref[...] = jnp.concatenate([jnp.sum(dy * p2, axis=0, keepdims=True), jnp.sum(dy * p1, axis=0, keepdims=True),
                                           jnp.sum(dy * pv, axis=0, keepdims=True)], axis=0)

        @pl.when(part == 1)
        def _():
            dproj_ref[...] = dc_ref[...]

        @pl.when(part == 2)
        def _():
            dproj_ref[...] = dv_ref[...]

    col = lambda part: pl.BlockSpec((seq, tc), lambda j, t: (0, part * nb + j))
    small = pl.BlockSpec((3, tc), lambda j, t: (0, j))
    return pl.pallas_call(
        body, name="conv_bwd", grid=(nb, 3),
        in_specs=[col(0), col(1), col(2), small, pl.BlockSpec((seq, tc), lambda j, t: (0, j))],
        out_specs=[pl.BlockSpec((seq, tc), lambda j, t: (0, t * nb + j)), small],
        out_shape=[jax.ShapeDtypeStruct((seq, c3), MXU_DTYPE), jax.ShapeDtypeStruct((3, ch), F32)],
        scratch_shapes=[pltpu.VMEM((seq, tc), MXU_DTYPE)] * 2,
        compiler_params=_params("parallel", "arbitrary"),
    )(proj, proj, proj, cw, dm)


def _mix1_fwd(x, g, full, late):
    h = _norm(x, g, name="mix1_norm")
    job, keys = late.gather_job("mix1_in") if late else (None, [])
    proj, got = _carried(_mm, h, full[("sc_w_in", 0)], name="mix1_in", job=job)
    full.update(zip(keys, got))
    m = _conv_fwd(proj, full[("sc_conv_w", 0)])
    x2 = _mm(m, full[("sc_w_out", 0)], name="mix1_out", epilogue=lambda acc, xv: xv + acc, extras=[x])
    return x2, (x, h, proj, m)


def _mix1_bwd(dx2, saved, g, w_in, cw, w_out):
    x, h, proj, m = saved
    dm = _mm(dx2, w_out, tb=True, name="mix1_dm")
    dw_out = _mm(m, dx2, ta=True, name="mix1_dwout", out_dtype=WIRE_DTYPE)
    dproj, dcw = _conv_bwd(proj, cw, dm)
    dw_in = _mm(h, dproj, ta=True, name="mix1_dwin", out_dtype=WIRE_DTYPE)
    dh = _mm(dproj, w_in, tb=True, name="mix1_dh")
    dx, dg = _norm_bwd(dh, x, g, dx2, name="mix1_norm_bwd")
    return dx, dg, dw_in, dcw, dw_out


def _loss_head(x, g, target):
    feat = x.shape[1]

    def fn(xv, gv, tv):
        err = _rms_fwd(xv, gv) - tv
        dx, dg = _rms_bwd(err / feat, xv, gv)
        return jnp.sum(err * err, keepdims=True) * (0.5 / feat), dx, dg

    return _rowmap(fn, [x, g, target], "rcr", [((1, 1), F32, "a"), (x.shape, F32, "r"), (g.shape, F32, "a")],
                   name="loss_head", tl=256)


def _slot(ref, place, chip=None, half=None):
    axis, width = place
    shape = list(ref.shape)
    start = [0, 0]
    if chip is not None:
        start[axis], shape[axis] = chip * width, width
    if half is not None:
        h_axis = 0 if shape[0] % 32 == 0 else 1
        shape[h_axis] //= 2
        start[h_axis] = start[h_axis] + half * shape[h_axis]
    hint = lambda s, d: s if isinstance(s, int) else pl.multiple_of(s, 128 if d == 1 else 8)
    return ref.at[tuple(pl.ds(hint(s, d), n) for d, (s, n) in enumerate(zip(start, shape)))]


class _Exchange:
    def __init__(self, kind, arrays, places):
        self.kind, self.arrays, self.places, self.n = kind, list(arrays), list(places), len(arrays)
        self.out_shape = []
        for t, (axis, width) in zip(self.arrays, self.places):
            if kind == "gather":
                shape = list(t.shape)
                shape[axis] = N_CHIPS * width
            else:
                shape = [N_CHIPS] + list(t.shape)
                shape[1 + axis] = width
            self.out_shape.append(jax.ShapeDtypeStruct(tuple(shape), t.dtype))
        n = self.n
        self.scratch = [pltpu.SemaphoreType.DMA((3 * n,)) for _ in range(4 if kind == "gather" else 2)]
        self.scratch.append(pltpu.SemaphoreType.DMA((n,)))

    def _copies(self, ins, outs, sems):
        x, y, c = lax.axis_index("x"), lax.axis_index("y"), lax.axis_index("c")
        peers = [(1 - x, y), (x, 1 - y), (1 - x, 1 - y)]
        remote = lambda src, dst, send, recv, k, to: pltpu.make_async_remote_copy(
            src_ref=src, dst_ref=dst, send_sem=send.at[k], recv_sem=recv.at[k], device_id=to, device_id_type=MESH_ID)
        local, ici, d2d = [], [], []
        for a in range(self.n):
            place = self.places[a]
            if self.kind == "gather":
                local.append(pltpu.make_async_copy(ins[a], _slot(outs[a], place, 2 * x + y), sems[4].at[a]))
                for r, (px, py) in enumerate(peers):
                    ici.append(remote(_slot(ins[a], place, None, c), _slot(outs[a], place, 2 * x + y, c),
                                      sems[0], sems[1], 3 * a + r, (px, py, c)))
                    landed = _slot(outs[a], place, 2 * px + py, c)
                    d2d.append(remote(landed, landed, sems[2], sems[3], 3 * a + r, (x, y, 1 - c)))
            else:
                local.append(pltpu.make_async_copy(_slot(ins[a], place, 2 * x + y), outs[a].at[3], sems[2].at[a]))
                for r, (px, py) in enumerate(peers):
                    ici.append(remote(_slot(ins[a], place, 2 * px + py), outs[a].at[r], sems[0], sems[1], 3 * a + r, (px, py, c)))
        return local, ici, d2d

    def start(self, ins, outs, sems):
        local, ici, _ = self._copies(ins, outs, sems)
        for cp in local + ici:
            cp.start()

    def relay(self, ins, outs, sems):
        _, ici, d2d = self._copies(ins, outs, sems)
        for arrived, onward in zip(ici, d2d):
            arrived.wait_recv()
            onward.start()

    def finish(self, ins, outs, sems):
        local, ici, d2d = self._copies(ins, outs, sems)
        for cp in local + d2d:
            cp.wait()
        for cp in ici:
            cp.wait_send() if d2d else cp.wait()


def _exchange_call(job, name):
    n = job.n

    def body(*refs):
        ins, outs, sems = refs[:n], refs[n:2 * n], refs[2 * n:]
        job.start(ins, outs, sems)
        job.relay(ins, outs, sems)
        job.finish(ins, outs, sems)

    return pl.pallas_call(
        body, name=name, in_specs=[ANY_SPEC] * n, out_specs=[ANY_SPEC] * n, out_shape=job.out_shape,
        scratch_shapes=job.scratch, compiler_params=pltpu.CompilerParams(has_side_effects=True),
    )(*job.arrays)


def _carried_call(body, *, name, grid, in_specs, out_specs, out_shape, semantics, operands, scratch_shapes=(), job=None):
    scratch_shapes = list(scratch_shapes)
    if job is None:
        return pl.pallas_call(body, name=name, grid=grid, in_specs=in_specs, out_specs=out_specs, out_shape=out_shape,
                              scratch_shapes=scratch_shapes, compiler_params=_params(*semantics))(*operands), []
    n_in, n_out, n, n_scr = len(in_specs), len(out_specs), job.n, len(scratch_shapes)
    steps = math.prod(grid)

    def wrapped(*refs):
        ins, job_ins = refs[:n_in], refs[n_in:n_in + n]
        outs, job_outs = refs[n_in + n:n_in + n + n_out], refs[n_in + n + n_out:n_in + 2 * n + n_out]
        outs = outs + refs[n_in + 2 * n + n_out:n_in + 2 * n + n_out + n_scr]
        sems = refs[n_in + 2 * n + n_out + n_scr:]
        step = functools.reduce(lambda acc, d: acc * grid[d] + pl.program_id(d), range(len(grid)), 0)

        @pl.when(step == 0)
        def _():
            job.start(job_ins, job_outs, sems)

        @pl.when(step == (3 * steps) // 4)
        def _():
            job.relay(job_ins, job_outs, sems)

        body(*ins, *outs)

        @pl.when(step == steps - 1)
        def _():
            job.finish(job_ins, job_outs, sems)

    res = pl.pallas_call(
        wrapped, name=name, grid=grid, in_specs=list(in_specs) + [ANY_SPEC] * n, out_specs=list(out_specs) + [ANY_SPEC] * n,
        out_shape=list(out_shape) + job.out_shape, scratch_shapes=scratch_shapes + job.scratch,
        compiler_params=pltpu.CompilerParams(dimension_semantics=("arbitrary",) * len(grid), vmem_limit_bytes=VMEM_LIMIT,
                                             has_side_effects=True),
    )(*operands, *job.arrays)
    return res[:n_out], res[n_out:]


def _swap_with_sibling(parts):
    n = len(parts)

    def body(*refs):
        ins, outs = refs[:n], refs[n:2 * n]
        send, recv = refs[2 * n:]
        sibling = (lax.axis_index("x"), lax.axis_index("y"), 1 - lax.axis_index("c"))
        copies = [pltpu.make_async_remote_copy(src_ref=ins[a], dst_ref=outs[a], send_sem=send.at[a], recv_sem=recv.at[a],
                                               device_id=sibling, device_id_type=MESH_ID) for a in range(n)]
        for cp in copies:
            cp.start()
        for cp in copies:
            cp.wait()

    return pl.pallas_call(
        body, name="swap_with_sibling",
        in_specs=[ANY_SPEC] * n, out_specs=[ANY_SPEC] * n,
        out_shape=[jax.ShapeDtypeStruct(p.shape, p.dtype) for p in parts],
        scratch_shapes=[pltpu.SemaphoreType.DMA((n,)), pltpu.SemaphoreType.DMA((n,))],
        compiler_params=pltpu.CompilerParams(has_side_effects=True),
    )(*parts)


def _sum_all_devices(t):
    rows = t.shape[0]

    def body(t_ref, o_ref, slots, send, recv):
        x, y, c = lax.axis_index("x"), lax.axis_index("y"), lax.axis_index("c")
        me = 4 * x + 2 * y + c
        slots[me] = t_ref[...]
        copies = []
        for m in range(1, 8):
            peer = (x ^ (m >> 2), y ^ ((m >> 1) & 1), c ^ (m & 1))
            cp = pltpu.make_async_remote_copy(src_ref=t_ref, dst_ref=slots.at[me], send_sem=send.at[m - 1],
                                              recv_sem=recv.at[m - 1], device_id=peer, device_id_type=MESH_ID)
            cp.start()
            copies.append(cp)
        for cp in copies:
            cp.wait()
        acc = slots[0]
        for dev in range(1, 8):
            acc = acc + slots[dev]
        o_ref[...] = acc

    vmem = pl.BlockSpec(memory_space=pltpu.VMEM)
    return pl.pallas_call(
        body, name="sum_all_devices", in_specs=[vmem], out_specs=vmem,
        out_shape=jax.ShapeDtypeStruct(t.shape, F32),
        scratch_shapes=[pltpu.VMEM((8, rows, 128), F32), pltpu.SemaphoreType.DMA((7,)), pltpu.SemaphoreType.DMA((7,))],
        compiler_params=pltpu.CompilerParams(vmem_limit_bytes=VMEM_LIMIT, has_side_effects=True),
    )(t)


def _adamw(w, g, m, v):
    m = ADAM_B1 * m + (1.0 - ADAM_B1) * g
    v = ADAM_B2 * v + (1.0 - ADAM_B2) * jnp.square(g)
    m_hat = m / (1.0 - ADAM_B1 ** ADAM_STEP)
    v_hat = v / (1.0 - ADAM_B2 ** ADAM_STEP)
    return -ADAM_LR * (m_hat / (jnp.sqrt(v_hat) + ADAM_EPS) + ADAM_WD * w), m, v


def _chip_sum(received, name):
    rows, cols = received.shape[1:]
    tl = _row_block(rows, 512)

    def body(r_ref, o_ref):
        o_ref[...] = ((r_ref[0].astype(F32) + r_ref[1].astype(F32)) + r_ref[2].astype(F32)) + r_ref[3].astype(F32)

    return pl.pallas_call(body, name=name, grid=(rows // tl,),
                          in_specs=[pl.BlockSpec((N_CHIPS, tl, cols), lambda i: (0, i, 0))],
                          out_specs=pl.BlockSpec((tl, cols), lambda i: (i, 0)),
                          out_shape=jax.ShapeDtypeStruct((rows, cols), F32), compiler_params=_params("parallel"))(received)


def _adamw_layer(w, m, v, p_mine, p_other, layer, prev, name):
    _, rows, cols = w.shape
    tl = _row_block(rows, 512)

    def body(w_ref, m_ref, v_ref, pa_ref, pb_ref, *rest):
        g = pa_ref[:, :cols] + pb_ref[:, :cols]
        for o_ref, val in zip(rest[-4:], (g,) + _adamw(w_ref[...], g, m_ref[...], v_ref[...])):
            o_ref[...] = val

    stacked = pl.BlockSpec((None, tl, cols), lambda i: (layer, i, 0))
    part = pl.BlockSpec((tl, p_mine.shape[1]), lambda i: (i, 0))
    kept = list(prev) if prev else []
    return pl.pallas_call(
        body, name=name, grid=(rows // tl,),
        in_specs=[stacked] * 3 + [part] * 2 + [ANY_SPEC] * len(kept),
        out_specs=[stacked] * 4, out_shape=[jax.ShapeDtypeStruct(w.shape, F32)] * 4,
        input_output_aliases={5 + k: k for k in range(len(kept))},
        compiler_params=_params("parallel"),
    )(w, m, v, p_mine, p_other, *kept)


def _adamw_small(w, g, m, v):
    def fn(wv, gv, mv, vv):
        return _adamw(wv, gv, mv, vv)

    return _rowmap(fn, [w, g, m, v], "rrrr", [(w.shape, F32, "r")] * 3, name="adamw_small", tl=w.shape[0])


WEIGHTS = ['ffn1_norm', 'ffn1_w_gate', 'ffn1_w_up', 'ffn1_w_down', 'mix_norm', 'ffn2_norm', 'ffn2_w_gate', 'ffn2_w_up',
           'ffn2_w_down', 'ab_w_in', 's5_lambda_re', 's5_lambda_im', 's5_log_dt', 's5_b_re', 's5_b_im', 's5_c_re', 's5_c_im',
           's5_d', 's5_w_glu', 'ab_w_out', 'sc_w_in', 'sc_conv_w', 'sc_w_out', 'final_norm']
SHARDED = {'ffn1_w_gate': (0, FF_SLOT), 'ffn1_w_up': (0, FF_SLOT), 'ffn1_w_down': (0, FF_SLOT),
           'ffn2_w_gate': (0, FF_SLOT), 'ffn2_w_up': (0, FF_SLOT), 'ffn2_w_down': (0, FF_SLOT),
           'ab_w_in': (1, 512), 's5_w_glu': (0, 128), 'ab_w_out': (0, 256), 'sc_w_in': (1, 768), 'sc_conv_w': (1, 256),
           'sc_w_out': (0, 256)}
SWAPPED = ('ffn1_w_gate', 'ffn1_w_up', 'ffn2_w_gate', 'ffn2_w_up')
SMALL = [n for n in WEIGHTS if n not in SHARDED]


def _held(name, t):
    return jnp.swapaxes(t, 1, 2) if name in SWAPPED else t


def _pack(arrays):
    rows = []
    for t in arrays:
        flat = t.reshape(-1)
        rows.append(jnp.pad(flat, (0, (-flat.shape[0]) % 128)))
    flat = jnp.concatenate(rows)
    return jnp.pad(flat, (0, (-flat.shape[0]) % 1024)).reshape(-1, 128)


def _unpack(packed, like):
    flat, out, pos = packed.reshape(-1), [], 0
    for t in like:
        out.append(flat[pos:pos + t.size].reshape(t.shape))
        pos += t.size + (-t.size) % 128
    return out


def _local_grads(x, target, p, full, late=None):
    small, grads, saved = {}, {}, []
    for layer in range(2):
        x, s1 = _ffn_fwd(x, p["ffn1_norm"][layer:layer + 1], full, "ffn1", layer, late)
        if layer == 0:
            x, sm = _mix0_fwd(x, p["mix_norm"][0:1], p, full, late)
        else:
            x, sm = _mix1_fwd(x, p["mix_norm"][1:2], full, late)
        x, s2 = _ffn_fwd(x, p["ffn2_norm"][layer:layer + 1], full, "ffn2", layer, late)
        saved.append((s1, sm, s2))
    loss, dx, dg_final = _loss_head(x, p["final_norm"][None], target)
    small["final_norm"] = dg_final[0]
    gains = {n: [None, None] for n in ("ffn1_norm", "mix_norm", "ffn2_norm")}

    def ffn_bwd(which, layer, dx, s):
        dx, dg = _ffn_bwd(dx, s, p[f"{which}_norm"][layer:layer + 1], full, which, layer, grads, late,
                          inline=(which, layer) in (("ffn2", 1), ("ffn1", 0)))
        gains[f"{which}_norm"][layer] = dg[0]
        return dx

    for layer in (1, 0):
        s1, sm, s2 = saved[layer]
        dx = ffn_bwd("ffn2", layer, dx, s2)
        if layer == 0:
            dx, dg, s5_grads = _mix0_bwd(dx, sm, p["mix_norm"][0:1], p, full, grads, late)
            small.update(s5_grads)
        else:
            dx, dg, dw_in, dcw, dw_out = _mix1_bwd(dx, sm, p["mix_norm"][1:2], full[("sc_w_in", 0)], full[("sc_conv_w", 0)],
                                                   full[("sc_w_out", 0)])
            grads.update({("sc_w_in", 0): dw_in, ("sc_conv_w", 0): dcw.astype(WIRE_DTYPE), ("sc_w_out", 0): dw_out})
        gains["mix_norm"][layer] = dg[0]
        dx = ffn_bwd("ffn1", layer, dx, s1)
    small.update({n: jnp.stack(pair) for n, pair in gains.items()})
    return loss, dx, small, grads


_GATHER_PLAN = {
    "gather_early": [("ffn1_w_gate", 0), ("ffn1_w_up", 0)],
    "ffn1_0_up": [("ffn1_w_down", 0), ("ab_w_in", 0)],
    "sb_fwd": [("s5_w_glu", 0), ("ab_w_out", 0), ("ffn2_w_gate", 0), ("ffn2_w_up", 0), ("ffn2_w_down", 0),
               ("ffn1_w_gate", 1), ("ffn1_w_up", 1), ("ffn1_w_down", 1)],
    "ffn2_0_up": [("sc_w_in", 0), ("sc_conv_w", 0), ("sc_w_out", 0)],
    "ffn1_1_up": [("ffn2_w_gate", 1), ("ffn2_w_up", 1)],
    "mix1_in": [("ffn2_w_down", 1)],
}


class _Late:
    def __init__(self, shards, places):
        self.shards, self.places = shards, places
        self.sent, self.received = set(), {}

    def gather_job(self, carrier):
        keys = _GATHER_PLAN.get(carrier, [])
        if not keys:
            return None, []
        return _Exchange("gather", [self.shards[k] for k in keys], [self.places[k] for k in keys]), keys

    def scatter_job(self, grads):
        keys = [k for k in grads if k not in self.sent]
        if not keys:
            return None, []
        self.sent.update(keys)
        return _Exchange("scatter", [grads[k] for k in keys], [self.places[k] for k in keys]), keys


def kernel(x, ffn1_norm, ffn1_w_gate, ffn1_w_up, ffn1_w_down, mix_norm, ffn2_norm, ffn2_w_gate, ffn2_w_up, ffn2_w_down, ab_w_in, s5_lambda_re, s5_lambda_im, s5_log_dt, s5_b_re, s5_b_im, s5_c_re, s5_c_im, s5_d, s5_w_glu, ab_w_out, sc_w_in, sc_conv_w, sc_w_out, final_norm, loss_target, m_ffn1_norm, m_ffn1_w_gate, m_ffn1_w_up, m_ffn1_w_down, m_mix_norm, m_ffn2_norm, m_ffn2_w_gate, m_ffn2_w_up, m_ffn2_w_down, m_ab_w_in, m_s5_lambda_re, m_s5_lambda_im, m_s5_log_dt, m_s5_b_re, m_s5_b_im, m_s5_c_re, m_s5_c_im, m_s5_d, m_s5_w_glu, m_ab_w_out, m_sc_w_in, m_sc_conv_w, m_sc_w_out, m_final_norm, v_ffn1_norm, v_ffn1_w_gate, v_ffn1_w_up, v_ffn1_w_down, v_mix_norm, v_ffn2_norm, v_ffn2_w_gate, v_ffn2_w_up, v_ffn2_w_down, v_ab_w_in, v_s5_lambda_re, v_s5_lambda_im, v_s5_log_dt, v_s5_b_re, v_s5_b_im, v_s5_c_re, v_s5_c_im, v_s5_d, v_s5_w_glu, v_ab_w_out, v_sc_w_in, v_sc_conv_w, v_sc_w_out, v_final_norm):
    args = dict(locals())
    p = {n: _held(n, args[n]) for n in WEIGHTS}
    mom = {n: _held(n, args["m_" + n]) for n in WEIGHTS}
    var = {n: _held(n, args["v_" + n]) for n in WEIGHTS}

    keys = [(n, layer) for n in SHARDED for layer in range(p[n].shape[0])]
    shards, places = {}, {}
    for n, layer in keys:
        axis, width = SHARDED[n]
        t = p[n][layer] if n == "sc_conv_w" else p[n][layer].astype(MXU_DTYPE)
        pad = [(0, 0), (0, 0)]
        pad[axis] = (0, width - t.shape[axis])
        shards[(n, layer)], places[(n, layer)] = jnp.pad(t, pad), (axis, width)
    late = _Late(shards, places)
    job, first = late.gather_job("gather_early")
    full = dict(zip(first, _exchange_call(job, "gather_early")))

    loss, dx, small, grads = _local_grads(x[0], loss_target[0], p, full, late)
    loss = lax.psum(loss[0, 0], ("x", "y", "c"))
    assert set(late.received) == set(keys), "a gradient was left without a carrier"

    partial = [_chip_sum(late.received[(n, layer)], name=f"chip_sum_{n}_{layer}") for n, layer in keys]
    other = _swap_with_sibling(partial)
    out = {}
    for (n, layer), mine, theirs in zip(keys, partial, other):
        out[n] = _adamw_layer(p[n], mom[n], var[n], mine, theirs, layer, out.get(n), name=f"adamw_{n}_{layer}")
    out = {n: [_held(n, t) for t in res] for n, res in out.items()}

    like = [p[n] for n in SMALL]
    g_small = _sum_all_devices(_pack([small[n] for n in SMALL]))
    d_small, m_small, v_small = _adamw_small(_pack(like), g_small, _pack([mom[n] for n in SMALL]), _pack([var[n] for n in SMALL]))
    for k, packed in enumerate((g_small, d_small, m_small, v_small)):
        for n, t in zip(SMALL, _unpack(packed, like)):
            out.setdefault(n, [None] * 4)[k] = t

    return (loss, dx[None], *[out[n][0] for n in WEIGHTS], *[out[n][1] for n in WEIGHTS],
            *[out[n][2] for n in WEIGHTS], *[out[n][3] for n in WEIGHTS])
```

```python
import functools
import math

import jax
import jax.numpy as jnp
from jax import lax
from jax.experimental import pallas as pl
from jax.experimental.pallas import tpu as pltpu

F32 = jnp.float32
MXU_DTYPE = jnp.bfloat16
WIRE_DTYPE = jnp.bfloat16
MESH_ID = pl.DeviceIdType.MESH

D_MODEL = 1024
D_FF = 2752
N_CHIPS = 4
FF_SHARD = D_FF // N_CHIPS
FF_SLOT = 768
FF_PAD = N_CHIPS * FF_SLOT
S5_WIDTH = 512
S5_GROUP = 16
S5_GROUPS = 32
S5_STATE = 64
S5_LANES = S5_GROUPS * S5_STATE
S5_BLOCK = 512
SB_HEADS = 8
SB_DH = 64
SB_SCALE = 0.125
SB_PACK = 2
SB_QUERIES = 1024
SB_KEYS = 256
EPS = 1e-6
ADAM_LR, ADAM_B1, ADAM_B2, ADAM_EPS, ADAM_WD, ADAM_STEP = 0.001, 0.9, 0.999, 1e-08, 0.01, 10
VMEM_LIMIT = 56 * 1024 * 1024

ANY_SPEC = pl.BlockSpec(memory_space=pl.ANY)


def _params(*sem):
    return pltpu.CompilerParams(dimension_semantics=sem or None, vmem_limit_bytes=VMEM_LIMIT)


def _mm(a, b, *, name, ta=False, tb=False, out_dtype=F32, epilogue=None, extras=(), tm=1024, tn=1024, tk=1024, job=None):
    m, k = (a.shape[1], a.shape[0]) if ta else a.shape
    n = b.shape[0] if tb else b.shape[1]
    tm, tn, tk = min(tm, m), min(tn, n), min(tk, k)
    assert m % tm == 0 and n % tn == 0 and k % tk == 0, (name, m, n, k)
    nk = k // tk
    a_spec = pl.BlockSpec((tk, tm), lambda i, j, kk: (kk, i)) if ta else pl.BlockSpec((tm, tk), lambda i, j, kk: (i, kk))
    b_spec = pl.BlockSpec((tn, tk), lambda i, j, kk: (j, kk)) if tb else pl.BlockSpec((tk, tn), lambda i, j, kk: (kk, j))
    ex_specs = []
    for e in extras:
        if e.shape == (m, n):
            ex_specs.append(pl.BlockSpec((tm, tn), lambda i, j, kk: (i, j)))
        elif e.shape == (1, n):
            ex_specs.append(pl.BlockSpec((1, tn), lambda i, j, kk: (0, j)))
        else:
            assert e.shape == (m, 1), (name, e.shape)
            ex_specs.append(pl.BlockSpec((tm, 1), lambda i, j, kk: (i, 0)))
    dims = (((0 if ta else 1,), (1 if tb else 0,)), ((), ()))
    n_ex = len(extras)

    out_dtypes = list(out_dtype) if isinstance(out_dtype, (list, tuple)) else [out_dtype]
    n_out = len(out_dtypes)

    def body(a_ref, b_ref, *rest):
        ex_refs, o_refs = rest[:n_ex], rest[n_ex:n_ex + n_out]

        def product():
            return lax.dot_general(a_ref[...].astype(MXU_DTYPE), b_ref[...].astype(MXU_DTYPE), dims, preferred_element_type=F32)

        def finish(r):
            if epilogue is not None:
                r = epilogue(r, *[e[...] for e in ex_refs])
            for o_ref, val in zip(o_refs, r if isinstance(r, (tuple, list)) else (r,)):
                o_ref[...] = val.astype(o_ref.dtype)

        if nk == 1:
            finish(product())
            return
        acc_ref, kk = rest[n_ex + n_out], pl.program_id(2)

        @pl.when(kk == 0)
        def _():
            acc_ref[...] = jnp.zeros_like(acc_ref)

        acc_ref[...] += product()

        @pl.when(kk == nk - 1)
        def _():
            finish(acc_ref[...])

    res, got = _carried_call(
        body, name=name, grid=(m // tm, n // tn, nk),
        in_specs=[a_spec, b_spec, *ex_specs],
        out_specs=[pl.BlockSpec((tm, tn), lambda i, j, kk: (i, j))] * n_out,
        out_shape=[jax.ShapeDtypeStruct((m, n), dt) for dt in out_dtypes],
        scratch_shapes=[pltpu.VMEM((tm, tn), F32)] if nk > 1 else [],
        semantics=("parallel", "parallel", "arbitrary"), operands=(a, b, *extras), job=job)
    res = res if isinstance(out_dtype, (list, tuple)) else res[0]
    return res if job is None else (res, got)


def _row_block(rows, want):
    for tl in range(min(want, rows), 7, -1):
        if rows % tl == 0 and tl % 8 == 0:
            return tl
    return rows


def _rowmap(fn, ins, in_kinds, outs, *, name, tl):
    rows = next(x.shape[0] for x, kd in zip(ins, in_kinds) if kd == "r")
    tl = _row_block(rows, tl)
    n_in = len(ins)

    def spec(shape, kind):
        if kind == "r":
            return pl.BlockSpec((tl,) + tuple(shape[1:]), lambda i: (i,) + (0,) * (len(shape) - 1))
        return pl.BlockSpec(tuple(shape), lambda i: (0,) * len(shape))

    def body(*refs):
        in_refs, out_refs = refs[:n_in], refs[n_in:]
        res = fn(*[r[...] for r in in_refs])
        if not isinstance(res, (tuple, list)):
            res = (res,)
        for o_ref, val, (_, dt, kind) in zip(out_refs, res, outs):
            if kind == "r":
                o_ref[...] = val.astype(dt)
            else:
                @pl.when(pl.program_id(0) == 0)
                def _():
                    o_ref[...] = jnp.zeros_like(o_ref)

                o_ref[...] += val.astype(dt)

    has_acc = any(kd == "a" for _, _, kd in outs)
    res = pl.pallas_call(
        body, name=name, grid=(rows // tl,),
        in_specs=[spec(x.shape, kd) for x, kd in zip(ins, in_kinds)],
        out_specs=[spec(s, kd) for s, _, kd in outs],
        out_shape=[jax.ShapeDtypeStruct(s, dt) for s, dt, _ in outs],
        compiler_params=_params("arbitrary" if has_acc else "parallel"),
    )(*ins)
    return res[0] if len(outs) == 1 else res


def _rms_fwd(x, g):
    r = lax.rsqrt(jnp.mean(x * x, axis=-1, keepdims=True) + EPS)
    return x * r * g


def _rms_bwd(dh, x, g):
    r = lax.rsqrt(jnp.mean(x * x, axis=-1, keepdims=True) + EPS)
    xh = x * r
    dxh = dh * g
    dx = r * (dxh - xh * jnp.mean(dxh * xh, axis=-1, keepdims=True))
    return dx, jnp.sum(dh * xh, axis=0, keepdims=True)


def _norm(x, g, *, name):
    return _rowmap(lambda xv, gv: _rms_fwd(xv, gv), [x, g], "rc", [(x.shape, MXU_DTYPE, "r")], name=name, tl=256)


def _norm_bwd(dh, x, g, dres, *, name):
    def fn(dhv, xv, gv, drv):
        dx, dg = _rms_bwd(dhv, xv, gv)
        return dx + drv, dg
    return _rowmap(fn, [dh, x, g, dres], "rrcr", [(x.shape, F32, "r"), (g.shape, F32, "a")], name=name, tl=256)


def _swiglu_act(a, b):
    return jax.nn.silu(a) * b


def _ffn_up(x, g, wg, wu, *, name, tm=1024, tn=1024, job=None):
    m, d = x.shape
    n = wg.shape[0]
    tm, tn = min(tm, m), min(tn, n)
    assert m % tm == 0 and n % tn == 0, (name, m, n)

    def body(x_ref, g_ref, wg_ref, wu_ref, h_ref, a_ref, b_ref, s_ref):
        @pl.when(pl.program_id(1) == 0)
        def _():
            h_ref[...] = _rms_fwd(x_ref[...], g_ref[...]).astype(h_ref.dtype)

        hv = h_ref[...]
        av = lax.dot_general(hv, wg_ref[...], NT_DIMS, preferred_element_type=F32)
        bv = lax.dot_general(hv, wu_ref[...], NT_DIMS, preferred_element_type=F32)
        a_ref[...] = av.astype(a_ref.dtype)
        b_ref[...] = bv.astype(b_ref.dtype)
        s_ref[...] = _swiglu_act(av, bv).astype(s_ref.dtype)

    rows = pl.BlockSpec((tm, d), lambda i, j: (i, 0))
    wgt = pl.BlockSpec((tn, d), lambda i, j: (j, 0))
    tile = pl.BlockSpec((tm, tn), lambda i, j: (i, j))
    return _carried_call(
        body, name=name, grid=(m // tm, n // tn),
        in_specs=[rows, pl.BlockSpec((1, d), lambda i, j: (0, 0)), wgt, wgt],
        out_specs=[rows, tile, tile, tile],
        out_shape=[jax.ShapeDtypeStruct((m, d), MXU_DTYPE)] + [jax.ShapeDtypeStruct((m, n), MXU_DTYPE)] * 3,
        semantics=("parallel", "arbitrary"), operands=(x, g, wg, wu), job=job)


def _ffn_dx(da, db, wg, wu, x, g, dres, *, name, tm=1024, tk=1024, job=None):
    m, f = da.shape
    d = wg.shape[1]
    tm, tk = min(tm, m), min(tk, f)
    assert m % tm == 0 and f % tk == 0, (name, m, f)
    nk = f // tk

    def body(da_ref, db_ref, wg_ref, wu_ref, x_ref, g_ref, dr_ref, dx_ref, dg_ref, acc_ref):
        i, kk = pl.program_id(0), pl.program_id(1)

        @pl.when(kk == 0)
        def _():
            acc_ref[...] = jnp.zeros_like(acc_ref)

        acc_ref[...] += jnp.dot(da_ref[...], wg_ref[...], preferred_element_type=F32)
        acc_ref[...] += jnp.dot(db_ref[...], wu_ref[...], preferred_element_type=F32)

        @pl.when(jnp.logical_and(i == 0, kk == 0))
        def _():
            dg_ref[...] = jnp.zeros_like(dg_ref)

        @pl.when(kk == nk - 1)
        def _():
            dx, dg = _rms_bwd(acc_ref[...], x_ref[...], g_ref[...])
            dx_ref[...] = dx + dr_ref[...]
            dg_ref[...] += dg

    act = pl.BlockSpec((tm, tk), lambda i, kk: (i, kk))
    wgt = pl.BlockSpec((tk, d), lambda i, kk: (kk, 0))
    rows = pl.BlockSpec((tm, d), lambda i, kk: (i, 0))
    one = pl.BlockSpec((1, d), lambda i, kk: (0, 0))
    return _carried_call(
        body, name=name, grid=(m // tm, nk),
        in_specs=[act, act, wgt, wgt, rows, one, rows],
        out_specs=[rows, one],
        out_shape=[jax.ShapeDtypeStruct((m, d), F32), jax.ShapeDtypeStruct((1, d), F32)],
        scratch_shapes=[pltpu.VMEM((tm, d), F32)],
        semantics=("arbitrary", "arbitrary"), operands=(da, db, wg, wu, x, g, dres), job=job)


def _ffn_fwd(x, g, full, which, layer, late):
    tag = f"{which}_{layer}"
    job, keys = late.gather_job(f"{tag}_up") if late else (None, [])
    (h, a, b, s), got = _ffn_up(x, g, full[(f"{which}_w_gate", layer)], full[(f"{which}_w_up", layer)], name=f"{tag}_up", job=job)
    full.update(zip(keys, got))
    x2 = _mm(s, full[(f"{which}_w_down", layer)], name=f"{tag}_down", epilogue=lambda acc, xv: xv + 0.5 * acc, extras=[x])
    return x2, (x, h, a, b, s)


def _ffn_bwd(dx2, saved, g, full, which, layer, grads, late, inline):
    x, h, a, b, s = saved
    tag = f"{which}_{layer}"
    kg, ku, kd = [(f"{which}_w_{n}", layer) for n in ("gate", "up", "down")]
    wg, wu, wd = full[kg], full[ku], full[kd]
    send = (lambda: late.scatter_job(grads)) if (late and inline) else (lambda: (None, []))

    def act_bwd(ds, av, bv):
        _, vjp = jax.vjp(_swiglu_act, av.astype(F32), bv.astype(F32))
        return vjp(0.5 * ds)

    grads[kd] = _mm(s, dx2, ta=True, name=f"{tag}_dwd", out_dtype=WIRE_DTYPE, epilogue=lambda acc: 0.5 * acc, tk=2048)
    job, keys = send()
    (da, db), got = _carried(_mm, dx2, wd, tb=True, name=f"{tag}_dact", epilogue=act_bwd, extras=[a, b],
                             out_dtype=[MXU_DTYPE, MXU_DTYPE], job=job)
    _note(late, keys, got)
    grads[kg] = _mm(da, h, ta=True, name=f"{tag}_dwg", out_dtype=WIRE_DTYPE, tk=2048)
    job, keys = send()
    grads[ku], got = _carried(_mm, db, h, ta=True, name=f"{tag}_dwu", out_dtype=WIRE_DTYPE, tk=2048, job=job)
    _note(late, keys, got)
    job, keys = send()
    (dx, dg), got = _ffn_dx(da, db, wg, wu, x, g, dx2, name=f"{tag}_dx", job=job)
    _note(late, keys, got)
    return dx, dg


def _carried(fn, *args, job, **kwargs):
    return fn(*args, job=job, **kwargs) if job is not None else (fn(*args, **kwargs), [])


def _note(late, keys, got):
    if late:
        late.received.update(zip(keys, got))


def _softplus(z):
    return jnp.maximum(z, 0.0) + jnp.log(1.0 + jnp.exp(-jnp.abs(z)))


def _ones_dot(x, tri):
    if MXU_DTYPE == F32:
        return jnp.dot(x, tri, preferred_element_type=F32)
    hi = x.astype(MXU_DTYPE)
    lo = (x - hi.astype(F32)).astype(MXU_DTYPE)
    return jnp.dot(hi, tri, preferred_element_type=F32) + jnp.dot(lo, tri, preferred_element_type=F32)


NT_DIMS = (((1,), (1,)), ((), ()))
TN_DIMS = (((0,), (0,)), ((), ()))


SB_LANES = SB_PACK * SB_DH
Q_COL, K_COL, V_COL = (S5_WIDTH * n // SB_LANES for n in (1, 2, 3))


def _head_lanes(rows, hd):
    return lax.broadcasted_iota(jnp.int32, (rows, SB_LANES), 1) // SB_DH == hd


def _attend(proj, *, tq=SB_QUERIES, job=None):
    seq = proj.shape[0]
    tq = min(tq, seq)
    tk = min(SB_KEYS, tq)
    per, hp = tq // tk, SB_PACK

    def body(q_ref, k_ref, v_ref, o_ref, ls_ref):
        i = pl.program_id(1)
        r_idx = lax.broadcasted_iota(jnp.int32, (tk, tk), 0)
        c_idx = lax.broadcasted_iota(jnp.int32, (tk, tk), 1)
        after = (r_idx > c_idx).astype(MXU_DTYPE)
        lanes = [_head_lanes(tk, hd) for hd in range(hp)]

        def block(j, cs, acc, straddles):
            off = pl.multiple_of(j * tk, tk)
            k2, v2 = k_ref[pl.ds(off, tk), :], v_ref[pl.ds(off, tk), :]
            top = 0 if straddles is None else straddles * tk
            rows = tq - top
            q2 = (q_ref[pl.ds(top, rows), :] * SB_SCALE).astype(MXU_DTYPE)
            new_cs, out = [], acc[top:]
            for hd in range(hp):
                kv = jnp.where(lanes[hd], k2, 0.0).astype(MXU_DTYPE)
                vv = jnp.where(lanes[hd], v2, 0.0).astype(MXU_DTYPE)
                z = lax.dot_general(q2, kv, NT_DIMS, preferred_element_type=F32)
                sp = _softplus(z)
                c_in = cs[hd][top:]
                if straddles is None:
                    lk = -sp
                    w = jnp.exp(z - sp + _ones_dot(lk, after) + c_in)
                else:
                    before = lax.broadcasted_iota(jnp.int32, (rows, tk), 1) < lax.broadcasted_iota(jnp.int32, (rows, tk), 0)
                    lk = jnp.where(before, -sp, 0.0)
                    w = jnp.where(before, jnp.exp(z - sp + _ones_dot(lk, after) + c_in), 0.0)
                out = out + jnp.dot(w.astype(MXU_DTYPE), vv, preferred_element_type=F32)
                c_new = c_in + jnp.sum(lk, axis=1, keepdims=True)
                new_cs.append(jnp.concatenate([cs[hd][:top], c_new], axis=0) if top else c_new)
            return tuple(new_cs), (jnp.concatenate([acc[:top], out], axis=0) if top else out)

        carry = (tuple(jnp.zeros((tq, 1), F32) for _ in range(hp)), jnp.zeros((tq, SB_LANES), F32))
        for s in reversed(range(per)):
            carry = block(i * per + s, *carry, s)
        cs, acc = lax.fori_loop(0, i * per, lambda n, cr: block(i * per - 1 - n, *cr, None), carry)
        o_ref[...] = acc
        for hd in range(hp):
            ls_ref[hd] = cs[hd]

    whole = lambda col: pl.BlockSpec((seq, SB_LANES), lambda g, i: (0, col + g))
    return _carried_call(
        body, name="sb_fwd", grid=(SB_HEADS // hp, seq // tq),
        in_specs=[pl.BlockSpec((tq, SB_LANES), lambda g, i: (i, Q_COL + g)), whole(K_COL), whole(V_COL)],
        out_specs=[pl.BlockSpec((tq, SB_LANES), lambda g, i: (i, g)), pl.BlockSpec((hp, tq, 1), lambda g, i: (g, i, 0))],
        out_shape=[jax.ShapeDtypeStruct((seq, SB_HEADS * SB_DH), F32), jax.ShapeDtypeStruct((SB_HEADS, seq, 1), F32)],
        semantics=("parallel", "parallel"), operands=(proj, proj, proj), job=job)


def _attend_bwd(proj, lsum, dmix, *, tq=SB_QUERIES, job=None):
    seq = proj.shape[0]
    tq = min(tq, seq)
    tk = min(SB_KEYS, tq)
    per, hp = tq // tk, SB_PACK
    do_col = S5_WIDTH // SB_LANES

    def body(q_ref, k_ref, v_ref, ls_ref, do_ref, dq_ref, dk_ref, dv_ref):
        i = pl.program_id(1)

        @pl.when(i == 0)
        def _():
            dk_ref[...] = jnp.zeros_like(dk_ref)
            dv_ref[...] = jnp.zeros_like(dv_ref)

        r_idx = lax.broadcasted_iota(jnp.int32, (tk, tk), 0)
        c_idx = lax.broadcasted_iota(jnp.int32, (tk, tk), 1)
        upto = (r_idx <= c_idx).astype(MXU_DTYPE)
        before = (r_idx < c_idx).astype(MXU_DTYPE)
        lanes = [_head_lanes(tk, hd) for hd in range(hp)]

        def block(j, sums, dq, straddles):
            off = pl.multiple_of(j * tk, tk)
            k2, v2 = k_ref[pl.ds(off, tk), :], v_ref[pl.ds(off, tk), :]
            top = 0 if straddles is None else straddles * tk
            rows = tq - top
            part = pl.ds(top, rows)
            q2 = (q_ref[part, :] * SB_SCALE).astype(MXU_DTYPE)
            do2 = do_ref[part, :].astype(MXU_DTYPE)
            valid = None
            if straddles is not None:
                valid = lax.broadcasted_iota(jnp.int32, (rows, tk), 1) < lax.broadcasted_iota(jnp.int32, (rows, tk), 0)
            new_sums, out, dk, dv = [], dq[top:], jnp.zeros((tk, SB_LANES), F32), jnp.zeros((tk, SB_LANES), F32)
            for hd in range(hp):
                cp, ce = sums[hd]
                kv = jnp.where(lanes[hd], k2, 0.0).astype(MXU_DTYPE)
                vv = jnp.where(lanes[hd], v2, 0.0).astype(MXU_DTYPE)
                z = lax.dot_general(q2, kv, NT_DIMS, preferred_element_type=F32)
                sp = _softplus(z)
                lk = -sp if valid is None else jnp.where(valid, -sp, 0.0)
                w = jnp.exp(z - sp + (ls_ref[hd, part, :] - cp[top:]) - _ones_dot(lk, upto))
                if valid is not None:
                    w = jnp.where(valid, w, 0.0)
                e = w * lax.dot_general(do2, vv, NT_DIMS, preferred_element_type=F32)
                earlier = _ones_dot(e, before) + ce[top:]
                keep = jnp.exp(-sp)
                dz = e * keep - (1.0 - keep) * earlier
                if valid is not None:
                    dz = jnp.where(valid, dz, 0.0)
                dzm = dz.astype(MXU_DTYPE)
                out = out + jnp.dot(dzm, kv, preferred_element_type=F32)
                dk = dk + jnp.where(lanes[hd], lax.dot_general(dzm, q2, TN_DIMS, preferred_element_type=F32), 0.0)
                dv = dv + jnp.where(lanes[hd], lax.dot_general(w.astype(MXU_DTYPE), do2, TN_DIMS, preferred_element_type=F32), 0.0)
                new = (cp[top:] + jnp.sum(lk, axis=1, keepdims=True), ce[top:] + jnp.sum(e, axis=1, keepdims=True))
                new_sums.append(tuple(jnp.concatenate([old[:top], val], axis=0) for old, val in zip((cp, ce), new)) if top else new)
            dk_ref[pl.ds(off, tk), :] += dk
            dv_ref[pl.ds(off, tk), :] += dv
            return tuple(new_sums), (jnp.concatenate([dq[:top], out], axis=0) if top else out)

        zero = jnp.zeros((tq, 1), F32)
        carry = (tuple((zero, zero) for _ in range(hp)), jnp.zeros((tq, SB_LANES), F32))
        carry = lax.fori_loop(0, i * per, lambda j, cr: block(j, *cr, None), carry)
        for s in range(per):
            carry = block(i * per + s, *carry, s)
        dq_ref[...] = carry[1] * SB_SCALE

    whole = lambda col: pl.BlockSpec((seq, SB_LANES), lambda g, i: (0, col + g))
    tile = lambda col: pl.BlockSpec((tq, SB_LANES), lambda g, i: (i, col + g))
    acc = pl.BlockSpec((seq, SB_LANES), lambda g, i: (0, g))
    return _carried_call(
        body, name="sb_bwd", grid=(SB_HEADS // hp, seq // tq),
        in_specs=[tile(Q_COL), whole(K_COL), whole(V_COL), pl.BlockSpec((hp, tq, 1), lambda g, i: (g, i, 0)), tile(do_col)],
        out_specs=[tile(0), acc, acc],
        out_shape=[jax.ShapeDtypeStruct((seq, SB_HEADS * SB_DH), F32)] * 3,
        semantics=("parallel", "arbitrary"), operands=(proj, proj, proj, lsum, dmix), job=job)


def _s5_disc(lr, li, ldt, br, bi):
    dt = jnp.exp(ldt)
    mag = jnp.exp(lr * dt)
    ar = mag * jnp.cos(li * dt)
    ai = mag * jnp.sin(li * dt)
    den = lr * lr + li * li
    nr = ar - 1.0
    cr = (nr * lr + ai * li) / den
    ci = (ai * lr - nr * li) / den
    return ar, ai, cr[None] * br - ci[None] * bi, cr[None] * bi + ci[None] * br


def _s5_prep(lr, li, ldt, br, bi):
    shapes = [lr.shape, lr.shape, br.shape, br.shape]

    def body(lr_ref, li_ref, ldt_ref, br_ref, bi_ref, *outs):
        for o, val in zip(outs, _s5_disc(lr_ref[...], li_ref[...], ldt_ref[...], br_ref[...], bi_ref[...])):
            o[...] = val

    return pl.pallas_call(body, name="s5_prep", out_shape=[jax.ShapeDtypeStruct(s, F32) for s in shapes])(lr, li, ldt, br, bi)


def _s5_prep_bwd(lr, li, ldt, br, bi, cts):
    args = (lr, li, ldt, br, bi)

    def body(*refs):
        ins, ct_refs, outs = refs[:5], refs[5:9], refs[9:]
        _, vjp = jax.vjp(_s5_disc, *[r[...] for r in ins])
        for o, val in zip(outs, vjp(tuple(r[...] for r in ct_refs))):
            o[...] = val

    return pl.pallas_call(body, name="s5_prep_bwd", out_shape=[jax.ShapeDtypeStruct(a.shape, F32) for a in args])(*args, *cts)


SCAN_ROWS = 8


def _powers(ar, ai):
    out = [(ar, ai)]
    for _ in range(SCAN_ROWS - 1):
        pr, pi = out[-1]
        out.append((pr * ar - pi * ai, pr * ai + pi * ar))
    return out


def _s5_scan(bu, a, *, tc=512):
    seq, w2 = bu.shape
    tw = S5_BLOCK
    tc = min(tc, seq)
    assert seq % tc == 0 and w2 % (2 * tw) == 0

    def body(bu_ref, a_ref, h_ref, cr_ref, ci_ref):
        @pl.when(pl.program_id(1) == 0)
        def _():
            cr_ref[...] = jnp.zeros_like(cr_ref)
            ci_ref[...] = jnp.zeros_like(ci_ref)

        re, im = pl.ds(0, tw), pl.ds(tw, tw)
        powers = _powers(a_ref[:, re], a_ref[:, im])
        pr = jnp.concatenate([p[0] for p in powers], axis=0)
        pi = jnp.concatenate([p[1] for p in powers], axis=0)
        row_id = lax.broadcasted_iota(jnp.int32, (SCAN_ROWS, tw), 0)
        reach = {dist: tuple(jnp.where(row_id >= dist, part, 0.0) for part in powers[dist - 1]) for dist in (1, 2, 4)}

        def block(n, carry):
            hr, hi = carry
            rows = pl.ds(pl.multiple_of(n * SCAN_ROWS, SCAN_ROWS), SCAN_ROWS)
            yr, yi = bu_ref[rows, re], bu_ref[rows, im]
            for dist in (1, 2, 4):
                cr, ci = reach[dist]
                sr, si = pltpu.roll(yr, dist, 0), pltpu.roll(yi, dist, 0)
                yr, yi = yr + cr * sr - ci * si, yi + cr * si + ci * sr
            yr, yi = yr + pr * hr - pi * hi, yi + pr * hi + pi * hr
            h_ref[rows, re] = yr
            h_ref[rows, im] = yi
            return yr[SCAN_ROWS - 1:], yi[SCAN_ROWS - 1:]

        hr, hi = lax.fori_loop(0, tc // SCAN_ROWS, block, (cr_ref[...], ci_ref[...]), unroll=4)
        cr_ref[...] = hr
        ci_ref[...] = hi

    blk = pl.BlockSpec((tc, 2 * tw), lambda j, t: (t, j))
    return pl.pallas_call(
        body, name="s5_scan", grid=(w2 // (2 * tw), seq // tc),
        in_specs=[blk, pl.BlockSpec((1, 2 * tw), lambda j, t: (0, j))],
        out_specs=blk,
        out_shape=jax.ShapeDtypeStruct((seq, w2), F32),
        scratch_shapes=[pltpu.VMEM((1, tw), F32)] * 2,
        compiler_params=_params("parallel", "arbitrary"),
    )(bu, a)


def _s5_scan_bwd(d, h, a, *, tc=512):
    seq, w2 = d.shape
    tw = S5_BLOCK
    tc = min(tc, seq)
    assert seq % tc == 0 and w2 % (2 * tw) == 0
    nt = seq // tc

    def body(d_ref, h_ref, a_ref, g_ref, da_ref, cr_ref, ci_ref):
        @pl.when(pl.program_id(1) == 0)
        def _():
            cr_ref[...] = jnp.zeros_like(cr_ref)
            ci_ref[...] = jnp.zeros_like(ci_ref)
            da_ref[...] = jnp.zeros_like(da_ref)

        re, im = pl.ds(0, tw), pl.ds(tw, tw)
        powers = _powers(a_ref[:, re], a_ref[:, im])
        pr = jnp.concatenate([p[0] for p in reversed(powers)], axis=0)
        pi = jnp.concatenate([p[1] for p in reversed(powers)], axis=0)
        row_id = lax.broadcasted_iota(jnp.int32, (SCAN_ROWS, tw), 0)
        last = SCAN_ROWS - 1
        reach = {dist: tuple(jnp.where(row_id < SCAN_ROWS - dist, part, 0.0) for part in powers[dist - 1]) for dist in (1, 2, 4)}

        def block(n, carry):
            gr, gi, sr, si = carry
            rows = pl.ds(pl.multiple_of((tc // SCAN_ROWS - 1 - n) * SCAN_ROWS, SCAN_ROWS), SCAN_ROWS)
            yr, yi = d_ref[rows, re], d_ref[rows, im]
            for dist in (1, 2, 4):
                cr, ci = reach[dist]
                ur, ui = pltpu.roll(yr, SCAN_ROWS - dist, 0), pltpu.roll(yi, SCAN_ROWS - dist, 0)
                yr, yi = yr + cr * ur + ci * ui, yi + cr * ui - ci * ur
            yr, yi = yr + pr * gr + pi * gi, yi + pr * gi - pi * gr
            g_ref[rows, re] = yr
            g_ref[rows, im] = yi
            nr = jnp.where(row_id < last, pltpu.roll(yr, last, 0), gr)
            ni = jnp.where(row_id < last, pltpu.roll(yi, last, 0), gi)
            hr, hi = h_ref[rows, re], h_ref[rows, im]
            return yr[:1], yi[:1], sr + nr * hr + ni * hi, si + ni * hr - nr * hi

        zero = jnp.zeros((SCAN_ROWS, tw), F32)
        gr, gi, sr, si = lax.fori_loop(0, tc // SCAN_ROWS, block, (cr_ref[...], ci_ref[...], zero, zero), unroll=4)
        cr_ref[...] = gr
        ci_ref[...] = gi
        da_ref[:, re] += jnp.sum(sr, axis=0, keepdims=True)
        da_ref[:, im] += jnp.sum(si, axis=0, keepdims=True)

    blk = pl.BlockSpec((tc, 2 * tw), lambda j, t: (nt - 1 - t, j))
    row = pl.BlockSpec((1, 2 * tw), lambda j, t: (0, j))
    return pl.pallas_call(
        body, name="s5_scan_bwd", grid=(w2 // (2 * tw), nt),
        in_specs=[blk, blk, row],
        out_specs=[blk, row],
        out_shape=[jax.ShapeDtypeStruct((seq, w2), F32), jax.ShapeDtypeStruct((1, w2), F32)],
        scratch_shapes=[pltpu.VMEM((1, tw), F32)] * 2,
        compiler_params=_params("parallel", "arbitrary"),
    )(d, h, a)


def _pair_columns(re, im, axis):
    shape = re.shape
    split = shape[:axis] + (shape[axis] // S5_BLOCK, S5_BLOCK) + shape[axis + 1:]
    both = jnp.stack([re.reshape(split), im.reshape(split)], axis=axis + 1)
    return both.reshape(shape[:axis] + (2 * shape[axis],) + shape[axis + 1:])


def _unpair_columns(t, axis):
    shape = t.shape
    both = t.reshape(shape[:axis] + (shape[axis] // (2 * S5_BLOCK), 2, S5_BLOCK) + shape[axis + 1:])
    half = shape[:axis] + (shape[axis] // 2,) + shape[axis + 1:]
    return (lax.index_in_dim(both, 0, axis + 1, keepdims=False).reshape(half),
            lax.index_in_dim(both, 1, axis + 1, keepdims=False).reshape(half))


def _block_diag(t):
    g, a, b = t.shape
    eye = jnp.eye(g, dtype=t.dtype)
    return (t[:, :, None, :] * eye[:, None, :, None]).reshape(g * a, g * b)


def _block_diag_part(m, g):
    a, b = m.shape[0] // g, m.shape[1] // g
    return jnp.moveaxis(jnp.diagonal(m.reshape(g, a, g, b), axis1=0, axis2=2), -1, 0)


def _gelu_glu(y, gate_pre):
    z = jax.nn.gelu(y)
    return z * jax.nn.sigmoid(gate_pre)


def _s5_fwd(u, p, w_glu):
    lr, li = p["s5_lambda_re"][0], p["s5_lambda_im"][0]
    ldt = p["s5_log_dt"][0][:, None]
    br = p["s5_b_re"][0].transpose(2, 0, 1)
    bi = p["s5_b_im"][0].transpose(2, 0, 1)
    ar, ai, bbr, bbi = _s5_prep(lr, li, ldt, br, bi)
    a = _pair_columns(ar.reshape(1, S5_LANES), ai.reshape(1, S5_LANES), 1)
    bmat = _pair_columns(_block_diag(bbr.transpose(1, 0, 2)), _block_diag(bbi.transpose(1, 0, 2)), 1)
    cmat = _pair_columns(_block_diag(p["s5_c_re"][0].transpose(0, 2, 1)),
                         -_block_diag(p["s5_c_im"][0].transpose(0, 2, 1)), 0)
    bmat, cmat = bmat.astype(MXU_DTYPE), cmat.astype(MXU_DTYPE)
    bu = _mm(u, bmat, name="s5_bu")
    h = _s5_scan(bu, a)
    d = p["s5_d"]
    y = _mm(h, cmat, name="s5_y", epilogue=lambda acc, uv, dv: acc + dv * uv, extras=[u, d])
    z = _rowmap(jax.nn.gelu, [y], "r", [(y.shape, MXU_DTYPE, "r")], name="s5_gelu", tl=512)
    gate_pre = _mm(z, w_glu, name="s5_glu")
    out = _rowmap(_gelu_glu, [y, gate_pre], "rr", [(y.shape, F32, "r")], name="s5_gate", tl=512)
    return out, (u, lr, li, ldt, br, bi, a, bmat, cmat, h, y, z, gate_pre)


def _s5_bwd(dout, saved, p, w_glu):
    u, lr, li, ldt, br, bi, a, bmat, cmat, h, y, z, gate_pre = saved
    d = p["s5_d"]

    def gate_bwd(dov, yv, gv):
        zv = jax.nn.gelu(yv)
        sg = jax.nn.sigmoid(gv)
        return dov * sg, dov * zv * sg * (1.0 - sg)

    dz_direct, dgate = _rowmap(gate_bwd, [dout, y, gate_pre], "rrr", [(y.shape, F32, "r"), (y.shape, MXU_DTYPE, "r")],
                               name="s5_gate_bwd", tl=512)
    dw_glu = _mm(z, dgate, ta=True, name="s5_dwglu", out_dtype=WIRE_DTYPE)
    dz = _mm(dgate, w_glu, tb=True, name="s5_dz", epilogue=lambda acc, prev: acc + prev, extras=[dz_direct])

    def gelu_bwd(dzv, yv, uv, dvv):
        _, vjp = jax.vjp(jax.nn.gelu, yv)
        dy = vjp(dzv)[0]
        return dy, dy * dvv, jnp.sum(dy * uv, axis=0, keepdims=True)

    dy, du_skip, dd = _rowmap(gelu_bwd, [dz, y, u, d], "rrrc",
                              [(y.shape, F32, "r"), (y.shape, F32, "r"), (d.shape, F32, "a")], name="s5_gelu_bwd", tl=512)
    dcmat = _mm(h, dy, ta=True, name="s5_dc")
    dstate = _mm(dy, cmat, tb=True, name="s5_dstate")
    g, da = _s5_scan_bwd(dstate, h, a)
    du = _mm(g, bmat, tb=True, name="s5_du", epilogue=lambda acc, prev: acc + prev, extras=[du_skip], out_dtype=MXU_DTYPE)
    dbmat = _mm(u, g, ta=True, name="s5_db")
    dbbr, dbbi = (_block_diag_part(t, S5_GROUPS).transpose(1, 0, 2) for t in _unpair_columns(dbmat, 1))
    dar, dai = _unpair_columns(da, 1)
    cts = (dar.reshape(S5_GROUPS, S5_STATE), dai.reshape(S5_GROUPS, S5_STATE), dbbr, dbbi)
    dlr, dli, dldt, dbr, dbi = _s5_prep_bwd(lr, li, ldt, br, bi, cts)
    dcr, dci = (_block_diag_part(t, S5_GROUPS).transpose(0, 2, 1) for t in _unpair_columns(dcmat, 0))
    grads = {
        "s5_lambda_re": dlr[None], "s5_lambda_im": dli[None], "s5_log_dt": dldt[:, 0][None],
        "s5_b_re": dbr.transpose(1, 2, 0)[None], "s5_b_im": dbi.transpose(1, 2, 0)[None],
        "s5_c_re": dcr[None], "s5_c_im": -dci[None], "s5_d": dd,
    }
    return du, dw_glu, grads


def _mix0_fwd(x, g, p, full, late):
    h = _norm(x, g, name="mix0_norm")
    proj = _mm(h, full[("ab_w_in", 0)], name="mix0_in")
    u = proj[:, :S5_WIDTH]
    job, keys = late.gather_job("sb_fwd") if late else (None, [])
    (o, lsum), got = _attend(proj, job=job)
    full.update(zip(keys, got))
    w_glu, w_out = full[("s5_w_glu", 0)], full[("ab_w_out", 0)]
    y_a, s5_saved = _s5_fwd(u, p, w_glu)
    mix = jnp.concatenate([y_a, o], axis=1).astype(MXU_DTYPE)
    x2 = _mm(mix, w_out, name="mix0_out", epilogue=lambda acc, xv: xv + acc, extras=[x])
    return x2, (x, h, proj, lsum, mix, s5_saved)


def _mix0_bwd(dx2, saved, g, p, full, grads, late):
    x, h, proj, lsum, mix, s5_saved = saved
    w_in, w_glu, w_out = full[("ab_w_in", 0)], full[("s5_w_glu", 0)], full[("ab_w_out", 0)]
    dmix = _mm(dx2, w_out, tb=True, name="mix0_dmix")
    grads[("ab_w_out", 0)] = _mm(mix, dx2, ta=True, name="mix0_dwout", out_dtype=WIRE_DTYPE)
    du, grads[("s5_w_glu", 0)], s5_grads = _s5_bwd(dmix[:, :S5_WIDTH], s5_saved, p, w_glu)
    job, keys = late.scatter_job(grads) if late else (None, [])
    (dq, dk, dv), got = _attend_bwd(proj, lsum, dmix, job=job)
    _note(late, keys, got)
    dproj = jnp.concatenate([du] + [t.astype(MXU_DTYPE) for t in (dq, dk, dv)], axis=1)
    grads[("ab_w_in", 0)] = _mm(h, dproj, ta=True, name="mix0_dwin", out_dtype=WIRE_DTYPE)
    job, keys = late.scatter_job(grads) if late else (None, [])
    dh, got = _carried(_mm, dproj, w_in, tb=True, name="mix0_dh", job=job)
    _note(late, keys, got)
    dx, dg = _norm_bwd(dh, x, g, dx2, name="mix0_norm_bwd")
    return dx, dg, s5_grads


def _shift_down(t, n):
    rows = lax.broadcasted_iota(jnp.int32, t.shape, 0)
    return jnp.where(rows >= n, pltpu.roll(t, n, 0), 0.0)


def _shift_up(t, n):
    rows = lax.broadcasted_iota(jnp.int32, t.shape, 0)
    return jnp.where(rows < t.shape[0] - n, pltpu.roll(t, t.shape[0] - n, 0), 0.0)


def _conv_fwd(proj, cw, *, tc=128):
    seq, c3 = proj.shape
    ch = c3 // 3
    nb = ch // tc

    def body(b_ref, c_ref, v_ref, w_ref, m_ref):
        pv = c_ref[...] * v_ref[...]
        w = w_ref[...]
        y = w[2:3] * pv + w[1:2] * _shift_down(pv, 1) + w[0:1] * _shift_down(pv, 2)
        m_ref[...] = (b_ref[...] * y).astype(m_ref.dtype)

    col = lambda part: pl.BlockSpec((seq, tc), lambda j: (0, part * nb + j))
    return pl.pallas_call(
        body, name="conv_fwd", grid=(nb,),
        in_specs=[col(0), col(1), col(2), pl.BlockSpec((3, tc), lambda j: (0, j))],
        out_specs=pl.BlockSpec((seq, tc), lambda j: (0, j)),
        out_shape=jax.ShapeDtypeStruct((seq, ch), MXU_DTYPE),
        compiler_params=_params("parallel"),
    )(proj, proj, proj, cw)


def _conv_bwd(proj, cw, dm, *, tc=128):
    seq, c3 = proj.shape
    ch = c3 // 3
    nb = ch // tc

    def body(b_ref, c_ref, v_ref, w_ref, dm_ref, dproj_ref, dw_ref, dc_ref, dv_ref):
        part = pl.program_id(1)

        @pl.when(part == 0)
        def _():
            cv, vv, dmv = c_ref[...], v_ref[...], dm_ref[...]
            pv = cv * vv
            w = w_ref[...]
            p1, p2 = _shift_down(pv, 1), _shift_down(pv, 2)
            y = w[2:3] * pv + w[1:2] * p1 + w[0:1] * p2
            dproj_ref[...] = (dmv * y).astype(dproj_ref.dtype)
            dy = dmv * b_ref[...]
            dp = w[2:3] * dy + w[1:2] * _shift_up(dy, 1) + w[0:1] * _shift_up(dy, 2)
            dc_ref[...] = (dp * vv).astype(dc_ref.dtype)
            dv_ref[...] = (dp * cv).astype(dv_ref.dtype)
            dw_ref[...] = jnp.concatenate([jnp.sum(dy * p2, axis=0, keepdims=True), jnp.sum(dy * p1, axis=0, keepdims=True),
                                           jnp.sum(dy * pv, axis=0, keepdims=True)], axis=0)

        @pl.when(part == 1)
        def _():
            dproj_ref[...] = dc_ref[...]

        @pl.when(part == 2)
        def _():
            dproj_ref[...] = dv_ref[...]

    col = lambda part: pl.BlockSpec((seq, tc), lambda j, t: (0, part * nb + j))
    small = pl.BlockSpec((3, tc), lambda j, t: (0, j))
    return pl.pallas_call(
        body, name="conv_bwd", grid=(nb, 3),
        in_specs=[col(0), col(1), col(2), small, pl.BlockSpec((seq, tc), lambda j, t: (0, j))],
        out_specs=[pl.BlockSpec((seq, tc), lambda j, t: (0, t * nb + j)), small],
        out_shape=[jax.ShapeDtypeStruct((seq, c3), MXU_DTYPE), jax.ShapeDtypeStruct((3, ch), F32)],
        scratch_shapes=[pltpu.VMEM((seq, tc), MXU_DTYPE)] * 2,
        compiler_params=_params("parallel", "arbitrary"),
    )(proj, proj, proj, cw, dm)


def _mix1_fwd(x, g, full, late):
    h = _norm(x, g, name="mix1_norm")
    job, keys = late.gather_job("mix1_in") if late else (None, [])
    proj, got = _carried(_mm, h, full[("sc_w_in", 0)], name="mix1_in", job=job)
    full.update(zip(keys, got))
    m = _conv_fwd(proj, full[("sc_conv_w", 0)])
    x2 = _mm(m, full[("sc_w_out", 0)], name="mix1_out", epilogue=lambda acc, xv: xv + acc, extras=[x])
    return x2, (x, h, proj, m)


def _mix1_bwd(dx2, saved, g, w_in, cw, w_out):
    x, h, proj, m = saved
    dm = _mm(dx2, w_out, tb=True, name="mix1_dm")
    dw_out = _mm(m, dx2, ta=True, name="mix1_dwout", out_dtype=WIRE_DTYPE)
    dproj, dcw = _conv_bwd(proj, cw, dm)
    dw_in = _mm(h, dproj, ta=True, name="mix1_dwin", out_dtype=WIRE_DTYPE)
    dh = _mm(dproj, w_in, tb=True, name="mix1_dh")
    dx, dg = _norm_bwd(dh, x, g, dx2, name="mix1_norm_bwd")
    return dx, dg, dw_in, dcw, dw_out


def _loss_head(x, g, target):
    feat = x.shape[1]

    def fn(xv, gv, tv):
        err = _rms_fwd(xv, gv) - tv
        dx, dg = _rms_bwd(err / feat, xv, gv)
        return jnp.sum(err * err, keepdims=True) * (0.5 / feat), dx, dg

    return _rowmap(fn, [x, g, target], "rcr", [((1, 1), F32, "a"), (x.shape, F32, "r"), (g.shape, F32, "a")],
                   name="loss_head", tl=256)


def _slot(ref, place, chip=None, half=None):
    axis, width = place
    shape = list(ref.shape)
    start = [0, 0]
    if chip is not None:
        start[axis], shape[axis] = chip * width, width
    if half is not None:
        h_axis = 0 if shape[0] % 32 == 0 else 1
        shape[h_axis] //= 2
        start[h_axis] = start[h_axis] + half * shape[h_axis]
    hint = lambda s, d: s if isinstance(s, int) else pl.multiple_of(s, 128 if d == 1 else 8)
    return ref.at[tuple(pl.ds(hint(s, d), n) for d, (s, n) in enumerate(zip(start, shape)))]


class _Exchange:
    def __init__(self, kind, arrays, places):
        self.kind, self.arrays, self.places, self.n = kind, list(arrays), list(places), len(arrays)
        self.out_shape = []
        for t, (axis, width) in zip(self.arrays, self.places):
            if kind == "gather":
                shape = list(t.shape)
                shape[axis] = N_CHIPS * width
            else:
                shape = [N_CHIPS] + list(t.shape)
                shape[1 + axis] = width
            self.out_shape.append(jax.ShapeDtypeStruct(tuple(shape), t.dtype))
        n = self.n
        self.scratch = [pltpu.SemaphoreType.DMA((3 * n,)) for _ in range(4 if kind == "gather" else 2)]
        self.scratch.append(pltpu.SemaphoreType.DMA((n,)))

    def _copies(self, ins, outs, sems):
        x, y, c = lax.axis_index("x"), lax.axis_index("y"), lax.axis_index("c")
        peers = [(1 - x, y), (x, 1 - y), (1 - x, 1 - y)]
        remote = lambda src, dst, send, recv, k, to: pltpu.make_async_remote_copy(
            src_ref=src, dst_ref=dst, send_sem=send.at[k], recv_sem=recv.at[k], device_id=to, device_id_type=MESH_ID)
        local, ici, d2d = [], [], []
        for a in range(self.n):
            place = self.places[a]
            if self.kind == "gather":
                local.append(pltpu.make_async_copy(ins[a], _slot(outs[a], place, 2 * x + y), sems[4].at[a]))
                for r, (px, py) in enumerate(peers):
                    ici.append(remote(_slot(ins[a], place, None, c), _slot(outs[a], place, 2 * x + y, c),
                                      sems[0], sems[1], 3 * a + r, (px, py, c)))
                    landed = _slot(outs[a], place, 2 * px + py, c)
                    d2d.append(remote(landed, landed, sems[2], sems[3], 3 * a + r, (x, y, 1 - c)))
            else:
                local.append(pltpu.make_async_copy(_slot(ins[a], place, 2 * x + y), outs[a].at[3], sems[2].at[a]))
                for r, (px, py) in enumerate(peers):
                    ici.append(remote(_slot(ins[a], place, 2 * px + py), outs[a].at[r], sems[0], sems[1], 3 * a + r, (px, py, c)))
        return local, ici, d2d

    def start(self, ins, outs, sems):
        local, ici, _ = self._copies(ins, outs, sems)
        for cp in local + ici:
            cp.start()

    def relay(self, ins, outs, sems):
        _, ici, d2d = self._copies(ins, outs, sems)
        for arrived, onward in zip(ici, d2d):
            arrived.wait_recv()
            onward.start()

    def finish(self, ins, outs, sems):
        local, ici, d2d = self._copies(ins, outs, sems)
        for cp in local + d2d:
            cp.wait()
        for cp in ici:
            cp.wait_send() if d2d else cp.wait()


def _exchange_call(job, name):
    n = job.n

    def body(*refs):
        ins, outs, sems = refs[:n], refs[n:2 * n], refs[2 * n:]
        job.start(ins, outs, sems)
        job.relay(ins, outs, sems)
        job.finish(ins, outs, sems)

    return pl.pallas_call(
        body, name=name, in_specs=[ANY_SPEC] * n, out_specs=[ANY_SPEC] * n, out_shape=job.out_shape,
        scratch_shapes=job.scratch, compiler_params=pltpu.CompilerParams(has_side_effects=True),
    )(*job.arrays)


def _carried_call(body, *, name, grid, in_specs, out_specs, out_shape, semantics, operands, scratch_shapes=(), job=None):
    scratch_shapes = list(scratch_shapes)
    if job is None:
        return pl.pallas_call(body, name=name, grid=grid, in_specs=in_specs, out_specs=out_specs, out_shape=out_shape,
                              scratch_shapes=scratch_shapes, compiler_params=_params(*semantics))(*operands), []
    n_in, n_out, n, n_scr = len(in_specs), len(out_specs), job.n, len(scratch_shapes)
    steps = math.prod(grid)

    def wrapped(*refs):
        ins, job_ins = refs[:n_in], refs[n_in:n_in + n]
        outs, job_outs = refs[n_in + n:n_in + n + n_out], refs[n_in + n + n_out:n_in + 2 * n + n_out]
        outs = outs + refs[n_in + 2 * n + n_out:n_in + 2 * n + n_out + n_scr]
        sems = refs[n_in + 2 * n + n_out + n_scr:]
        step = functools.reduce(lambda acc, d: acc * grid[d] + pl.program_id(d), range(len(grid)), 0)

        @pl.when(step == 0)
        def _():
            job.start(job_ins, job_outs, sems)

        @pl.when(step == (3 * steps) // 4)
        def _():
            job.relay(job_ins, job_outs, sems)

        body(*ins, *outs)

        @pl.when(step == steps - 1)
        def _():
            job.finish(job_ins, job_outs, sems)

    res = pl.pallas_call(
        wrapped, name=name, grid=grid, in_specs=list(in_specs) + [ANY_SPEC] * n, out_specs=list(out_specs) + [ANY_SPEC] * n,
        out_shape=list(out_shape) + job.out_shape, scratch_shapes=scratch_shapes + job.scratch,
        compiler_params=pltpu.CompilerParams(dimension_semantics=("arbitrary",) * len(grid), vmem_limit_bytes=VMEM_LIMIT,
                                             has_side_effects=True),
    )(*operands, *job.arrays)
    return res[:n_out], res[n_out:]


def _swap_with_sibling(parts):
    n = len(parts)

    def body(*refs):
        ins, outs = refs[:n], refs[n:2 * n]
        send, recv = refs[2 * n:]
        sibling = (lax.axis_index("x"), lax.axis_index("y"), 1 - lax.axis_index("c"))
        copies = [pltpu.make_async_remote_copy(src_ref=ins[a], dst_ref=outs[a], send_sem=send.at[a], recv_sem=recv.at[a],
                                               device_id=sibling, device_id_type=MESH_ID) for a in range(n)]
        for cp in copies:
            cp.start()
        for cp in copies:
            cp.wait()

    return pl.pallas_call(
        body, name="swap_with_sibling",
        in_specs=[ANY_SPEC] * n, out_specs=[ANY_SPEC] * n,
        out_shape=[jax.ShapeDtypeStruct(p.shape, p.dtype) for p in parts],
        scratch_shapes=[pltpu.SemaphoreType.DMA((n,)), pltpu.SemaphoreType.DMA((n,))],
        compiler_params=pltpu.CompilerParams(has_side_effects=True),
    )(*parts)


def _sum_all_devices(t):
    rows = t.shape[0]

    def body(t_ref, o_ref, slots, send, recv):
        x, y, c = lax.axis_index("x"), lax.axis_index("y"), lax.axis_index("c")
        me = 4 * x + 2 * y + c
        slots[me] = t_ref[...]
        copies = []
        for m in range(1, 8):
            peer = (x ^ (m >> 2), y ^ ((m >> 1) & 1), c ^ (m & 1))
            cp = pltpu.make_async_remote_copy(src_ref=t_ref, dst_ref=slots.at[me], send_sem=send.at[m - 1],
                                              recv_sem=recv.at[m - 1], device_id=peer, device_id_type=MESH_ID)
            cp.start()
            copies.append(cp)
        for cp in copies:
            cp.wait()
        acc = slots[0]
        for dev in range(1, 8):
            acc = acc + slots[dev]
        o_ref[...] = acc

    vmem = pl.BlockSpec(memory_space=pltpu.VMEM)
    return pl.pallas_call(
        body, name="sum_all_devices", in_specs=[vmem], out_specs=vmem,
        out_shape=jax.ShapeDtypeStruct(t.shape, F32),
        scratch_shapes=[pltpu.VMEM((8, rows, 128), F32), pltpu.SemaphoreType.DMA((7,)), pltpu.SemaphoreType.DMA((7,))],
        compiler_params=pltpu.CompilerParams(vmem_limit_bytes=VMEM_LIMIT, has_side_effects=True),
    )(t)


def _adamw(w, g, m, v):
    m = ADAM_B1 * m + (1.0 - ADAM_B1) * g
    v = ADAM_B2 * v + (1.0 - ADAM_B2) * jnp.square(g)
    m_hat = m / (1.0 - ADAM_B1 ** ADAM_STEP)
    v_hat = v / (1.0 - ADAM_B2 ** ADAM_STEP)
    return -ADAM_LR * (m_hat / (jnp.sqrt(v_hat) + ADAM_EPS) + ADAM_WD * w), m, v


def _chip_sum(received, name):
    rows, cols = received.shape[1:]
    tl = _row_block(rows, 512)

    def body(r_ref, o_ref):
        o_ref[...] = ((r_ref[0].astype(F32) + r_ref[1].astype(F32)) + r_ref[2].astype(F32)) + r_ref[3].astype(F32)

    return pl.pallas_call(body, name=name, grid=(rows // tl,),
                          in_specs=[pl.BlockSpec((N_CHIPS, tl, cols), lambda i: (0, i, 0))],
                          out_specs=pl.BlockSpec((tl, cols), lambda i: (i, 0)),
                          out_shape=jax.ShapeDtypeStruct((rows, cols), F32), compiler_params=_params("parallel"))(received)


def _adamw_layer(w, m, v, p_mine, p_other, layer, prev, name):
    _, rows, cols = w.shape
    tl = _row_block(rows, 512)

    def body(w_ref, m_ref, v_ref, pa_ref, pb_ref, *rest):
        g = pa_ref[:, :cols] + pb_ref[:, :cols]
        for o_ref, val in zip(rest[-4:], (g,) + _adamw(w_ref[...], g, m_ref[...], v_ref[...])):
            o_ref[...] = val

    stacked = pl.BlockSpec((None, tl, cols), lambda i: (layer, i, 0))
    part = pl.BlockSpec((tl, p_mine.shape[1]), lambda i: (i, 0))
    kept = list(prev) if prev else []
    return pl.pallas_call(
        body, name=name, grid=(rows // tl,),
        in_specs=[stacked] * 3 + [part] * 2 + [ANY_SPEC] * len(kept),
        out_specs=[stacked] * 4, out_shape=[jax.ShapeDtypeStruct(w.shape, F32)] * 4,
        input_output_aliases={5 + k: k for k in range(len(kept))},
        compiler_params=_params("parallel"),
    )(w, m, v, p_mine, p_other, *kept)


def _adamw_small(w, g, m, v):
    def fn(wv, gv, mv, vv):
        return _adamw(wv, gv, mv, vv)

    return _rowmap(fn, [w, g, m, v], "rrrr", [(w.shape, F32, "r")] * 3, name="adamw_small", tl=w.shape[0])


WEIGHTS = ['ffn1_norm', 'ffn1_w_gate', 'ffn1_w_up', 'ffn1_w_down', 'mix_norm', 'ffn2_norm', 'ffn2_w_gate', 'ffn2_w_up',
           'ffn2_w_down', 'ab_w_in', 's5_lambda_re', 's5_lambda_im', 's5_log_dt', 's5_b_re', 's5_b_im', 's5_c_re', 's5_c_im',
           's5_d', 's5_w_glu', 'ab_w_out', 'sc_w_in', 'sc_conv_w', 'sc_w_out', 'final_norm']
SHARDED = {'ffn1_w_gate': (0, FF_SLOT), 'ffn1_w_up': (0, FF_SLOT), 'ffn1_w_down': (0, FF_SLOT),
           'ffn2_w_gate': (0, FF_SLOT), 'ffn2_w_up': (0, FF_SLOT), 'ffn2_w_down': (0, FF_SLOT),
           'ab_w_in': (1, 512), 's5_w_glu': (0, 128), 'ab_w_out': (0, 256), 'sc_w_in': (1, 768), 'sc_conv_w': (1, 256),
           'sc_w_out': (0, 256)}
SWAPPED = ('ffn1_w_gate', 'ffn1_w_up', 'ffn2_w_gate', 'ffn2_w_up')
SMALL = [n for n in WEIGHTS if n not in SHARDED]


def _held(name, t):
    return jnp.swapaxes(t, 1, 2) if name in SWAPPED else t


def _pack(arrays):
    rows = []
    for t in arrays:
        flat = t.reshape(-1)
        rows.append(jnp.pad(flat, (0, (-flat.shape[0]) % 128)))
    flat = jnp.concatenate(rows)
    return jnp.pad(flat, (0, (-flat.shape[0]) % 1024)).reshape(-1, 128)


def _unpack(packed, like):
    flat, out, pos = packed.reshape(-1), [], 0
    for t in like:
        out.append(flat[pos:pos + t.size].reshape(t.shape))
        pos += t.size + (-t.size) % 128
    return out


def _local_grads(x, target, p, full, late=None):
    small, grads, saved = {}, {}, []
    for layer in range(2):
        x, s1 = _ffn_fwd(x, p["ffn1_norm"][layer:layer + 1], full, "ffn1", layer, late)
        if layer == 0:
            x, sm = _mix0_fwd(x, p["mix_norm"][0:1], p, full, late)
        else:
            x, sm = _mix1_fwd(x, p["mix_norm"][1:2], full, late)
        x, s2 = _ffn_fwd(x, p["ffn2_norm"][layer:layer + 1], full, "ffn2", layer, late)
        saved.append((s1, sm, s2))
    loss, dx, dg_final = _loss_head(x, p["final_norm"][None], target)
    small["final_norm"] = dg_final[0]
    gains = {n: [None, None] for n in ("ffn1_norm", "mix_norm", "ffn2_norm")}

    def ffn_bwd(which, layer, dx, s):
        dx, dg = _ffn_bwd(dx, s, p[f"{which}_norm"][layer:layer + 1], full, which, layer, grads, late,
                          inline=(which, layer) in (("ffn2", 1), ("ffn1", 0)))
        gains[f"{which}_norm"][layer] = dg[0]
        return dx

    for layer in (1, 0):
        s1, sm, s2 = saved[layer]
        dx = ffn_bwd("ffn2", layer, dx, s2)
        if layer == 0:
            dx, dg, s5_grads = _mix0_bwd(dx, sm, p["mix_norm"][0:1], p, full, grads, late)
            small.update(s5_grads)
        else:
            dx, dg, dw_in, dcw, dw_out = _mix1_bwd(dx, sm, p["mix_norm"][1:2], full[("sc_w_in", 0)], full[("sc_conv_w", 0)],
                                                   full[("sc_w_out", 0)])
            grads.update({("sc_w_in", 0): dw_in, ("sc_conv_w", 0): dcw.astype(WIRE_DTYPE), ("sc_w_out", 0): dw_out})
        gains["mix_norm"][layer] = dg[0]
        dx = ffn_bwd("ffn1", layer, dx, s1)
    small.update({n: jnp.stack(pair) for n, pair in gains.items()})
    return loss, dx, small, grads


_GATHER_PLAN = {
    "gather_early": [("ffn1_w_gate", 0), ("ffn1_w_up", 0)],
    "ffn1_0_up": [("ffn1_w_down", 0), ("ab_w_in", 0)],
    "sb_fwd": [("s5_w_glu", 0), ("ab_w_out", 0), ("ffn2_w_gate", 0), ("ffn2_w_up", 0), ("ffn2_w_down", 0),
               ("ffn1_w_gate", 1), ("ffn1_w_up", 1), ("ffn1_w_down", 1)],
    "ffn2_0_up": [("sc_w_in", 0), ("sc_conv_w", 0), ("sc_w_out", 0)],
    "ffn1_1_up": [("ffn2_w_gate", 1), ("ffn2_w_up", 1)],
    "mix1_in": [("ffn2_w_down", 1)],
}


class _Late:
    def __init__(self, shards, places):
        self.shards, self.places = shards, places
        self.sent, self.received = set(), {}

    def gather_job(self, carrier):
        keys = _GATHER_PLAN.get(carrier, [])
        if not keys:
            return None, []
        return _Exchange("gather", [self.shards[k] for k in keys], [self.places[k] for k in keys]), keys

    def scatter_job(self, grads):
        keys = [k for k in grads if k not in self.sent]
        if not keys:
            return None, []
        self.sent.update(keys)
        return _Exchange("scatter", [grads[k] for k in keys], [self.places[k] for k in keys]), keys


def kernel(x, ffn1_norm, ffn1_w_gate, ffn1_w_up, ffn1_w_down, mix_norm, ffn2_norm, ffn2_w_gate, ffn2_w_up, ffn2_w_down, ab_w_in, s5_lambda_re, s5_lambda_im, s5_log_dt, s5_b_re, s5_b_im, s5_c_re, s5_c_im, s5_d, s5_w_glu, ab_w_out, sc_w_in, sc_conv_w, sc_w_out, final_norm, loss_target, m_ffn1_norm, m_ffn1_w_gate, m_ffn1_w_up, m_ffn1_w_down, m_mix_norm, m_ffn2_norm, m_ffn2_w_gate, m_ffn2_w_up, m_ffn2_w_down, m_ab_w_in, m_s5_lambda_re, m_s5_lambda_im, m_s5_log_dt, m_s5_b_re, m_s5_b_im, m_s5_c_re, m_s5_c_im, m_s5_d, m_s5_w_glu, m_ab_w_out, m_sc_w_in, m_sc_conv_w, m_sc_w_out, m_final_norm, v_ffn1_norm, v_ffn1_w_gate, v_ffn1_w_up, v_ffn1_w_down, v_mix_norm, v_ffn2_norm, v_ffn2_w_gate, v_ffn2_w_up, v_ffn2_w_down, v_ab_w_in, v_s5_lambda_re, v_s5_lambda_im, v_s5_log_dt, v_s5_b_re, v_s5_b_im, v_s5_c_re, v_s5_c_im, v_s5_d, v_s5_w_glu, v_ab_w_out, v_sc_w_in, v_sc_conv_w, v_sc_w_out, v_final_norm):
    args = dict(locals())
    p = {n: _held(n, args[n]) for n in WEIGHTS}
    mom = {n: _held(n, args["m_" + n]) for n in WEIGHTS}
    var = {n: _held(n, args["v_" + n]) for n in WEIGHTS}

    keys = [(n, layer) for n in SHARDED for layer in range(p[n].shape[0])]
    shards, places = {}, {}
    for n, layer in keys:
        axis, width = SHARDED[n]
        t = p[n][layer] if n == "sc_conv_w" else p[n][layer].astype(MXU_DTYPE)
        pad = [(0, 0), (0, 0)]
        pad[axis] = (0, width - t.shape[axis])
        shards[(n, layer)], places[(n, layer)] = jnp.pad(t, pad), (axis, width)
    late = _Late(shards, places)
    job, first = late.gather_job("gather_early")
    full = dict(zip(first, _exchange_call(job, "gather_early")))

    loss, dx, small, grads = _local_grads(x[0], loss_target[0], p, full, late)
    loss = lax.psum(loss[0, 0], ("x", "y", "c"))
    assert set(late.received) == set(keys), "a gradient was left without a carrier"

    partial = [_chip_sum(late.received[(n, layer)], name=f"chip_sum_{n}_{layer}") for n, layer in keys]
    other = _swap_with_sibling(partial)
    out = {}
    for (n, layer), mine, theirs in zip(keys, partial, other):
        out[n] = _adamw_layer(p[n], mom[n], var[n], mine, theirs, layer, out.get(n), name=f"adamw_{n}_{layer}")
    out = {n: [_held(n, t) for t in res] for n, res in out.items()}

    like = [p[n] for n in SMALL]
    g_small = _sum_all_devices(_pack([small[n] for n in SMALL]))
    d_small, m_small, v_small = _adamw_small(_pack(like), g_small, _pack([mom[n] for n in SMALL]), _pack([var[n] for n in SMALL]))
    for k, packed in enumerate((g_small, d_small, m_small, v_small)):
        for n, t in zip(SMALL, _unpack(packed, like)):
            out.setdefault(n, [None] * 4)[k] = t

    return (loss, dx[None], *[out[n][0] for n in WEIGHTS], *[out[n][1] for n in WEIGHTS],
            *[out[n][2] for n in WEIGHTS], *[out[n][3] for n in WEIGHTS])
```

```python
import functools
import math

import jax
import jax.numpy as jnp
from jax import lax
from jax.experimental import pallas as pl
from jax.experimental.pallas import tpu as pltpu

F32 = jnp.float32
MXU_DTYPE = jnp.bfloat16
WIRE_DTYPE = jnp.bfloat16
MESH_ID = pl.DeviceIdType.MESH

D_MODEL = 1024
D_FF = 2752
N_CHIPS = 4
FF_SHARD = D_FF // N_CHIPS
FF_SLOT = 768
FF_PAD = N_CHIPS * FF_SLOT
S5_WIDTH = 512
S5_GROUP = 16
S5_GROUPS = 32
S5_STATE = 64
S5_LANES = S5_GROUPS * S5_STATE
S5_BLOCK = 512
S5_DIAG = S5_LANES // S5_BLOCK
SB_HEADS = 8
SB_DH = 64
SB_SCALE = 0.125
SB_PACK = 2
SB_QUERIES = 1024
SB_KEYS = 256
EPS = 1e-6
ADAM_LR, ADAM_B1, ADAM_B2, ADAM_EPS, ADAM_WD, ADAM_STEP = 0.001, 0.9, 0.999, 1e-08, 0.01, 10
VMEM_LIMIT = 56 * 1024 * 1024

ANY_SPEC = pl.BlockSpec(memory_space=pl.ANY)


def _params(*sem):
    return pltpu.CompilerParams(dimension_semantics=sem or None, vmem_limit_bytes=VMEM_LIMIT)


def _mm(a, b, *, name, ta=False, tb=False, out_dtype=F32, epilogue=None, extras=(), tm=1024, tn=1024, tk=1024,
        diagonal=None, job=None):
    m, k = (a.shape[1], a.shape[0]) if ta else a.shape
    n = b.shape[0] if tb else b.shape[1]
    out_block = lambda i, j, kk: (i, j)
    if diagonal is None:
        tm, tn, tk = min(tm, m), min(tn, n), min(tk, k)
        assert m % tm == 0 and n % tn == 0 and k % tk == 0, (name, m, n, k)
        grid = (m // tm, n // tn, k // tk)
        a_spec = pl.BlockSpec((tk, tm), lambda i, j, kk: (kk, i)) if ta else pl.BlockSpec((tm, tk), lambda i, j, kk: (i, kk))
        b_spec = pl.BlockSpec((tn, tk), lambda i, j, kk: (j, kk)) if tb else pl.BlockSpec((tk, tn), lambda i, j, kk: (kk, j))
    elif ta:
        tm, tn, tk = m // diagonal, n // diagonal, min(tk, k)
        grid = (diagonal, 1, k // tk)
        a_spec = pl.BlockSpec((tk, tm), lambda i, j, kk: (kk, i))
        b_spec = pl.BlockSpec((tk, tn), lambda i, j, kk: (kk, i))
        out_block = lambda i, j, kk: (i, i)
    else:
        tm, tn, tk = min(tm, m), n // diagonal, k // diagonal
        grid = (m // tm, diagonal, 1)
        a_spec = pl.BlockSpec((tm, tk), lambda i, j, kk: (i, j))
        b_spec = pl.BlockSpec((tn, tk) if tb else (tk, tn), lambda i, j, kk: (j, j))
    nk = grid[2]
    ex_specs = []
    for e in extras:
        if e.shape == (m, n):
            ex_specs.append(pl.BlockSpec((tm, tn), lambda i, j, kk: (i, j)))
        elif e.shape == (1, n):
            ex_specs.append(pl.BlockSpec((1, tn), lambda i, j, kk: (0, j)))
        else:
            assert e.shape == (m, 1), (name, e.shape)
            ex_specs.append(pl.BlockSpec((tm, 1), lambda i, j, kk: (i, 0)))
    dims = (((0 if ta else 1,), (1 if tb else 0,)), ((), ()))
    n_ex = len(extras)

    out_dtypes = list(out_dtype) if isinstance(out_dtype, (list, tuple)) else [out_dtype]
    n_out = len(out_dtypes)

    def body(a_ref, b_ref, *rest):
        ex_refs, o_refs = rest[:n_ex], rest[n_ex:n_ex + n_out]

        def product():
            return lax.dot_general(a_ref[...].astype(MXU_DTYPE), b_ref[...].astype(MXU_DTYPE), dims, preferred_element_type=F32)

        def finish(r):
            if epilogue is not None:
                r = epilogue(r, *[e[...] for e in ex_refs])
            for o_ref, val in zip(o_refs, r if isinstance(r, (tuple, list)) else (r,)):
                o_ref[...] = val.astype(o_ref.dtype)

        if nk == 1:
            finish(product())
            return
        acc_ref, kk = rest[n_ex + n_out], pl.program_id(2)

        @pl.when(kk == 0)
        def _():
            acc_ref[...] = jnp.zeros_like(acc_ref)

        acc_ref[...] += product()

        @pl.when(kk == nk - 1)
        def _():
            finish(acc_ref[...])

    res, got = _carried_call(
        body, name=name, grid=grid,
        in_specs=[a_spec, b_spec, *ex_specs],
        out_specs=[pl.BlockSpec((tm, tn), out_block)] * n_out,
        out_shape=[jax.ShapeDtypeStruct((m, n), dt) for dt in out_dtypes],
        scratch_shapes=[pltpu.VMEM((tm, tn), F32)] if nk > 1 else [],
        semantics=("parallel", "parallel", "arbitrary"), operands=(a, b, *extras), job=job)
    res = res if isinstance(out_dtype, (list, tuple)) else res[0]
    return res if job is None else (res, got)


def _row_block(rows, want):
    for tl in range(min(want, rows), 7, -1):
        if rows % tl == 0 and tl % 8 == 0:
            return tl
    return rows


def _rowmap(fn, ins, in_kinds, outs, *, name, tl):
    rows = next(x.shape[0] for x, kd in zip(ins, in_kinds) if kd == "r")
    tl = _row_block(rows, tl)
    n_in = len(ins)

    def spec(shape, kind):
        if kind == "r":
            return pl.BlockSpec((tl,) + tuple(shape[1:]), lambda i: (i,) + (0,) * (len(shape) - 1))
        return pl.BlockSpec(tuple(shape), lambda i: (0,) * len(shape))

    def body(*refs):
        in_refs, out_refs = refs[:n_in], refs[n_in:]
        res = fn(*[r[...] for r in in_refs])
        if not isinstance(res, (tuple, list)):
            res = (res,)
        for o_ref, val, (_, dt, kind) in zip(out_refs, res, outs):
            if kind == "r":
                o_ref[...] = val.astype(dt)
            else:
                @pl.when(pl.program_id(0) == 0)
                def _():
                    o_ref[...] = jnp.zeros_like(o_ref)

                o_ref[...] += val.astype(dt)

    has_acc = any(kd == "a" for _, _, kd in outs)
    res = pl.pallas_call(
        body, name=name, grid=(rows // tl,),
        in_specs=[spec(x.shape, kd) for x, kd in zip(ins, in_kinds)],
        out_specs=[spec(s, kd) for s, _, kd in outs],
        out_shape=[jax.ShapeDtypeStruct(s, dt) for s, dt, _ in outs],
        compiler_params=_params("arbitrary" if has_acc else "parallel"),
    )(*ins)
    return res[0] if len(outs) == 1 else res


def _rms_fwd(x, g):
    r = lax.rsqrt(jnp.mean(x * x, axis=-1, keepdims=True) + EPS)
    return x * r * g


def _rms_bwd(dh, x, g):
    r = lax.rsqrt(jnp.mean(x * x, axis=-1, keepdims=True) + EPS)
    xh = x * r
    dxh = dh * g
    dx = r * (dxh - xh * jnp.mean(dxh * xh, axis=-1, keepdims=True))
    return dx, jnp.sum(dh * xh, axis=0, keepdims=True)


def _swiglu_act(a, b):
    return jax.nn.silu(a) * b


def _ffn_up(x, g, wg, wu, *, name, tm=1024, tn=1024, job=None):
    m, d = x.shape
    n = wg.shape[0]
    tm, tn = min(tm, m), min(tn, n)
    assert m % tm == 0 and n % tn == 0, (name, m, n)

    def body(x_ref, g_ref, wg_ref, wu_ref, h_ref, a_ref, b_ref, s_ref):
        @pl.when(pl.program_id(1) == 0)
        def _():
            h_ref[...] = _rms_fwd(x_ref[...], g_ref[...]).astype(h_ref.dtype)

        hv = h_ref[...]
        av = lax.dot_general(hv, wg_ref[...], NT_DIMS, preferred_element_type=F32)
        bv = lax.dot_general(hv, wu_ref[...], NT_DIMS, preferred_element_type=F32)
        a_ref[...] = av.astype(a_ref.dtype)
        b_ref[...] = bv.astype(b_ref.dtype)
        s_ref[...] = _swiglu_act(av, bv).astype(s_ref.dtype)

    rows = pl.BlockSpec((tm, d), lambda i, j: (i, 0))
    wgt = pl.BlockSpec((tn, d), lambda i, j: (j, 0))
    tile = pl.BlockSpec((tm, tn), lambda i, j: (i, j))
    return _carried_call(
        body, name=name, grid=(m // tm, n // tn),
        in_specs=[rows, pl.BlockSpec((1, d), lambda i, j: (0, 0)), wgt, wgt],
        out_specs=[rows, tile, tile, tile],
        out_shape=[jax.ShapeDtypeStruct((m, d), MXU_DTYPE)] + [jax.ShapeDtypeStruct((m, n), MXU_DTYPE)] * 3,
        semantics=("parallel", "arbitrary"), operands=(x, g, wg, wu), job=job)


def _norm_proj(x, g, w, *, name, tm=1024, tn=1024, job=None):
    m, d = x.shape
    n = w.shape[1]
    tm, tn = min(tm, m), min(tn, n)
    assert m % tm == 0 and n % tn == 0, (name, m, n)

    def body(x_ref, g_ref, w_ref, h_ref, o_ref):
        @pl.when(pl.program_id(1) == 0)
        def _():
            h_ref[...] = _rms_fwd(x_ref[...], g_ref[...]).astype(h_ref.dtype)

        o_ref[...] = jnp.dot(h_ref[...], w_ref[...], preferred_element_type=F32)

    rows = pl.BlockSpec((tm, d), lambda i, j: (i, 0))
    return _carried_call(
        body, name=name, grid=(m // tm, n // tn),
        in_specs=[rows, pl.BlockSpec((1, d), lambda i, j: (0, 0)), pl.BlockSpec((d, tn), lambda i, j: (0, j))],
        out_specs=[rows, pl.BlockSpec((tm, tn), lambda i, j: (i, j))],
        out_shape=[jax.ShapeDtypeStruct((m, d), MXU_DTYPE), jax.ShapeDtypeStruct((m, n), F32)],
        semantics=("parallel", "arbitrary"), operands=(x, g, w), job=job)


def _proj_norm_bwd(pairs, x, g, dres, *, name, tm=1024, tk=1024, job=None):
    m, f = pairs[0][0].shape
    d = x.shape[1]
    tm, tk = min(tm, m), min(tk, f)
    assert m % tm == 0 and f % tk == 0, (name, m, f)
    nk, n_pairs = f // tk, len(pairs)
    swapped = [kept == "FD" for _, _, kept in pairs]

    def body(*refs):
        dy_refs, w_refs = refs[:n_pairs], refs[n_pairs:2 * n_pairs]
        x_ref, g_ref, dr_ref, dx_ref, dg_ref, acc_ref = refs[2 * n_pairs:]
        i, kk = pl.program_id(0), pl.program_id(1)

        @pl.when(kk == 0)
        def _():
            acc_ref[...] = jnp.zeros_like(acc_ref)

        for dy_ref, w_ref, rows_are_f in zip(dy_refs, w_refs, swapped):
            dims = (((1,), (0,)), ((), ())) if rows_are_f else NT_DIMS
            acc_ref[...] += lax.dot_general(dy_ref[...].astype(MXU_DTYPE), w_ref[...], dims, preferred_element_type=F32)

        @pl.when(jnp.logical_and(i == 0, kk == 0))
        def _():
            dg_ref[...] = jnp.zeros_like(dg_ref)

        @pl.when(kk == nk - 1)
        def _():
            dx, dg = _rms_bwd(acc_ref[...], x_ref[...], g_ref[...])
            dx_ref[...] = dx + dr_ref[...]
            dg_ref[...] += dg

    act = pl.BlockSpec((tm, tk), lambda i, kk: (i, kk))
    w_specs = [pl.BlockSpec((tk, d), lambda i, kk: (kk, 0)) if s else pl.BlockSpec((d, tk), lambda i, kk: (0, kk)) for s in swapped]
    rows = pl.BlockSpec((tm, d), lambda i, kk: (i, 0))
    one = pl.BlockSpec((1, d), lambda i, kk: (0, 0))
    return _carried_call(
        body, name=name, grid=(m // tm, nk),
        in_specs=[act] * n_pairs + w_specs + [rows, one, rows],
        out_specs=[rows, one],
        out_shape=[jax.ShapeDtypeStruct((m, d), F32), jax.ShapeDtypeStruct((1, d), F32)],
        scratch_shapes=[pltpu.VMEM((tm, d), F32)],
        semantics=("arbitrary", "arbitrary"), operands=(*[p[0] for p in pairs], *[p[1] for p in pairs], x, g, dres), job=job)


def _ffn_dx(da, db, wg, wu, x, g, dres, *, name, job=None):
    return _proj_norm_bwd([(da, wg, "FD"), (db, wu, "FD")], x, g, dres, name=name, job=job)


def _ffn_fwd(x, g, full, which, layer, late):
    tag = f"{which}_{layer}"
    job, keys = late.gather_job(f"{tag}_up") if late else (None, [])
    (h, a, b, s), got = _ffn_up(x, g, full[(f"{which}_w_gate", layer)], full[(f"{which}_w_up", layer)], name=f"{tag}_up", job=job)
    full.update(zip(keys, got))
    x2 = _mm(s, full[(f"{which}_w_down", layer)], name=f"{tag}_down", epilogue=lambda acc, xv: xv + 0.5 * acc, extras=[x])
    return x2, (x, h, a, b, s)


def _ffn_bwd(dx2, saved, g, full, which, layer, grads, late, inline):
    x, h, a, b, s = saved
    tag = f"{which}_{layer}"
    kg, ku, kd = [(f"{which}_w_{n}", layer) for n in ("gate", "up", "down")]
    wg, wu, wd = full[kg], full[ku], full[kd]
    send = (lambda: late.scatter_job(grads)) if (late and inline) else (lambda: (None, []))

    def act_bwd(ds, av, bv):
        _, vjp = jax.vjp(_swiglu_act, av.astype(F32), bv.astype(F32))
        return vjp(0.5 * ds)

    grads[kd] = _mm(s, dx2, ta=True, name=f"{tag}_dwd", out_dtype=WIRE_DTYPE, epilogue=lambda acc: 0.5 * acc, tk=2048)
    job, keys = send()
    (da, db), got = _carried(_mm, dx2, wd, tb=True, name=f"{tag}_dact", epilogue=act_bwd, extras=[a, b],
                             out_dtype=[MXU_DTYPE, MXU_DTYPE], job=job)
    _note(late, keys, got)
    grads[kg] = _mm(da, h, ta=True, name=f"{tag}_dwg", out_dtype=WIRE_DTYPE, tk=2048)
    job, keys = send()
    grads[ku], got = _carried(_mm, db, h, ta=True, name=f"{tag}_dwu", out_dtype=WIRE_DTYPE, tk=2048, job=job)
    _note(late, keys, got)
    job, keys = send()
    (dx, dg), got = _ffn_dx(da, db, wg, wu, x, g, dx2, name=f"{tag}_dx", job=job)
    _note(late, keys, got)
    return dx, dg


def _carried(fn, *args, job, **kwargs):
    return fn(*args, job=job, **kwargs) if job is not None else (fn(*args, **kwargs), [])


def _note(late, keys, got):
    if late:
        late.received.update(zip(keys, got))


def _softplus(z):
    return jnp.maximum(z, 0.0) + jnp.log(1.0 + jnp.exp(-jnp.abs(z)))


def _ones_dot(x, tri):
    if MXU_DTYPE == F32:
        return jnp.dot(x, tri, preferred_element_type=F32)
    hi = x.astype(MXU_DTYPE)
    lo = (x - hi.astype(F32)).astype(MXU_DTYPE)
    return jnp.dot(hi, tri, preferred_element_type=F32) + jnp.dot(lo, tri, preferred_element_type=F32)


NT_DIMS = (((1,), (1,)), ((), ()))
TN_DIMS = (((0,), (0,)), ((), ()))


SB_LANES = SB_PACK * SB_DH
Q_COL, K_COL, V_COL = (S5_WIDTH * n // SB_LANES for n in (1, 2, 3))


def _head_lanes(rows, hd):
    return lax.broadcasted_iota(jnp.int32, (rows, SB_LANES), 1) // SB_DH == hd


def _attend(proj, *, tq=SB_QUERIES, job=None):
    seq = proj.shape[0]
    tq = min(tq, seq)
    tk = min(SB_KEYS, tq)
    per, hp = tq // tk, SB_PACK

    def body(q_ref, k_ref, v_ref, o_ref, ls_ref):
        i = pl.program_id(1)
        r_idx = lax.broadcasted_iota(jnp.int32, (tk, tk), 0)
        c_idx = lax.broadcasted_iota(jnp.int32, (tk, tk), 1)
        after = (r_idx > c_idx).astype(MXU_DTYPE)
        lanes = [_head_lanes(tk, hd) for hd in range(hp)]

        def block(j, cs, acc, straddles):
            off = pl.multiple_of(j * tk, tk)
            k2, v2 = k_ref[pl.ds(off, tk), :], v_ref[pl.ds(off, tk), :]
            top = 0 if straddles is None else straddles * tk
            rows = tq - top
            q2 = (q_ref[pl.ds(top, rows), :] * SB_SCALE).astype(MXU_DTYPE)
            new_cs, out = [], acc[top:]
            for hd in range(hp):
                kv = jnp.where(lanes[hd], k2, 0.0).astype(MXU_DTYPE)
                vv = jnp.where(lanes[hd], v2, 0.0).astype(MXU_DTYPE)
                z = lax.dot_general(q2, kv, NT_DIMS, preferred_element_type=F32)
                sp = _softplus(z)
                c_in = cs[hd][top:]
                if straddles is None:
                    lk = -sp
                    w = jnp.exp(z - sp + _ones_dot(lk, after) + c_in)
                else:
                    before = lax.broadcasted_iota(jnp.int32, (rows, tk), 1) < lax.broadcasted_iota(jnp.int32, (rows, tk), 0)
                    lk = jnp.where(before, -sp, 0.0)
                    w = jnp.where(before, jnp.exp(z - sp + _ones_dot(lk, after) + c_in), 0.0)
                out = out + jnp.dot(w.astype(MXU_DTYPE), vv, preferred_element_type=F32)
                c_new = c_in + jnp.sum(lk, axis=1, keepdims=True)
                new_cs.append(jnp.concatenate([cs[hd][:top], c_new], axis=0) if top else c_new)
            return tuple(new_cs), (jnp.concatenate([acc[:top], out], axis=0) if top else out)

        carry = (tuple(jnp.zeros((tq, 1), F32) for _ in range(hp)), jnp.zeros((tq, SB_LANES), F32))
        for s in reversed(range(per)):
            carry = block(i * per + s, *carry, s)
        cs, acc = lax.fori_loop(0, i * per, lambda n, cr: block(i * per - 1 - n, *cr, None), carry)
        o_ref[...] = acc
        for hd in range(hp):
            ls_ref[hd] = cs[hd]

    whole = lambda col: pl.BlockSpec((seq, SB_LANES), lambda g, i: (0, col + g))
    return _carried_call(
        body, name="sb_fwd", grid=(SB_HEADS // hp, seq // tq),
        in_specs=[pl.BlockSpec((tq, SB_LANES), lambda g, i: (i, Q_COL + g)), whole(K_COL), whole(V_COL)],
        out_specs=[pl.BlockSpec((tq, SB_LANES), lambda g, i: (i, g)), pl.BlockSpec((hp, tq, 1), lambda g, i: (g, i, 0))],
        out_shape=[jax.ShapeDtypeStruct((seq, SB_HEADS * SB_DH), F32), jax.ShapeDtypeStruct((SB_HEADS, seq, 1), F32)],
        semantics=("parallel", "parallel"), operands=(proj, proj, proj), job=job)


def _attend_bwd(proj, lsum, dmix, *, tq=SB_QUERIES, job=None):
    seq = proj.shape[0]
    tq = min(tq, seq)
    tk = min(SB_KEYS, tq)
    per, hp = tq // tk, SB_PACK
    do_col = S5_WIDTH // SB_LANES

    def body(q_ref, k_ref, v_ref, ls_ref, do_ref, dq_ref, dk_ref, dv_ref):
        i = pl.program_id(1)

        @pl.when(i == 0)
        def _():
            dk_ref[...] = jnp.zeros_like(dk_ref)
            dv_ref[...] = jnp.zeros_like(dv_ref)

        r_idx = lax.broadcasted_iota(jnp.int32, (tk, tk), 0)
        c_idx = lax.broadcasted_iota(jnp.int32, (tk, tk), 1)
        upto = (r_idx <= c_idx).astype(MXU_DTYPE)
        before = (r_idx < c_idx).astype(MXU_DTYPE)
        lanes = [_head_lanes(tk, hd) for hd in range(hp)]

        def block(j, sums, dq, straddles):
            off = pl.multiple_of(j * tk, tk)
            k2, v2 = k_ref[pl.ds(off, tk), :], v_ref[pl.ds(off, tk), :]
            top = 0 if straddles is None else straddles * tk
            rows = tq - top
            part = pl.ds(top, rows)
            q2 = (q_ref[part, :] * SB_SCALE).astype(MXU_DTYPE)
            do2 = do_ref[part, :].astype(MXU_DTYPE)
            valid = None
            if straddles is not None:
                valid = lax.broadcasted_iota(jnp.int32, (rows, tk), 1) < lax.broadcasted_iota(jnp.int32, (rows, tk), 0)
            new_sums, out, dk, dv = [], dq[top:], jnp.zeros((tk, SB_LANES), F32), jnp.zeros((tk, SB_LANES), F32)
            for hd in range(hp):
                cp, ce = sums[hd]
                kv = jnp.where(lanes[hd], k2, 0.0).astype(MXU_DTYPE)
                vv = jnp.where(lanes[hd], v2, 0.0).astype(MXU_DTYPE)
                z = lax.dot_general(q2, kv, NT_DIMS, preferred_element_type=F32)
                sp = _softplus(z)
                lk = -sp if valid is None else jnp.where(valid, -sp, 0.0)
                w = jnp.exp(z - sp + (ls_ref[hd, part, :] - cp[top:]) - _ones_dot(lk, upto))
                if valid is not None:
                    w = jnp.where(valid, w, 0.0)
                e = w * lax.dot_general(do2, vv, NT_DIMS, preferred_element_type=F32)
                earlier = _ones_dot(e, before) + ce[top:]
                keep = jnp.exp(-sp)
                dz = e * keep - (1.0 - keep) * earlier
                if valid is not None:
                    dz = jnp.where(valid, dz, 0.0)
                dzm = dz.astype(MXU_DTYPE)
                out = out + jnp.dot(dzm, kv, preferred_element_type=F32)
                dk = dk + jnp.where(lanes[hd], lax.dot_general(dzm, q2, TN_DIMS, preferred_element_type=F32), 0.0)
                dv = dv + jnp.where(lanes[hd], lax.dot_general(w.astype(MXU_DTYPE), do2, TN_DIMS, preferred_element_type=F32), 0.0)
                new = (cp[top:] + jnp.sum(lk, axis=1, keepdims=True), ce[top:] + jnp.sum(e, axis=1, keepdims=True))
                new_sums.append(tuple(jnp.concatenate([old[:top], val], axis=0) for old, val in zip((cp, ce), new)) if top else new)
            dk_ref[pl.ds(off, tk), :] += dk
            dv_ref[pl.ds(off, tk), :] += dv
            return tuple(new_sums), (jnp.concatenate([dq[:top], out], axis=0) if top else out)

        zero = jnp.zeros((tq, 1), F32)
        carry = (tuple((zero, zero) for _ in range(hp)), jnp.zeros((tq, SB_LANES), F32))
        carry = lax.fori_loop(0, i * per, lambda j, cr: block(j, *cr, None), carry)
        for s in range(per):
            carry = block(i * per + s, *carry, s)
        dq_ref[...] = carry[1] * SB_SCALE

    whole = lambda col: pl.BlockSpec((seq, SB_LANES), lambda g, i: (0, col + g))
    tile = lambda col: pl.BlockSpec((tq, SB_LANES), lambda g, i: (i, col + g))
    acc = pl.BlockSpec((seq, SB_LANES), lambda g, i: (0, g))
    return _carried_call(
        body, name="sb_bwd", grid=(SB_HEADS // hp, seq // tq),
        in_specs=[tile(Q_COL), whole(K_COL), whole(V_COL), pl.BlockSpec((hp, tq, 1), lambda g, i: (g, i, 0)), tile(do_col)],
        out_specs=[tile(0), acc, acc],
        out_shape=[jax.ShapeDtypeStruct((seq, SB_HEADS * SB_DH), F32)] * 3,
        semantics=("parallel", "arbitrary"), operands=(proj, proj, proj, lsum, dmix), job=job)


def _s5_disc(lr, li, ldt, br, bi):
    dt = jnp.exp(ldt)
    mag = jnp.exp(lr * dt)
    ar = mag * jnp.cos(li * dt)
    ai = mag * jnp.sin(li * dt)
    den = lr * lr + li * li
    nr = ar - 1.0
    cr = (nr * lr + ai * li) / den
    ci = (ai * lr - nr * li) / den
    return ar, ai, cr[None] * br - ci[None] * bi, cr[None] * bi + ci[None] * br


def _s5_prep(lr, li, ldt, br, bi):
    shapes = [lr.shape, lr.shape, br.shape, br.shape]

    def body(lr_ref, li_ref, ldt_ref, br_ref, bi_ref, *outs):
        for o, val in zip(outs, _s5_disc(lr_ref[...], li_ref[...], ldt_ref[...], br_ref[...], bi_ref[...])):
            o[...] = val

    return pl.pallas_call(body, name="s5_prep", out_shape=[jax.ShapeDtypeStruct(s, F32) for s in shapes])(lr, li, ldt, br, bi)


def _s5_prep_bwd(lr, li, ldt, br, bi, cts):
    args = (lr, li, ldt, br, bi)

    def body(*refs):
        ins, ct_refs, outs = refs[:5], refs[5:9], refs[9:]
        _, vjp = jax.vjp(_s5_disc, *[r[...] for r in ins])
        for o, val in zip(outs, vjp(tuple(r[...] for r in ct_refs))):
            o[...] = val

    return pl.pallas_call(body, name="s5_prep_bwd", out_shape=[jax.ShapeDtypeStruct(a.shape, F32) for a in args])(*args, *cts)


SCAN_ROWS = 8


def _powers(ar, ai):
    out = [(ar, ai)]
    for _ in range(SCAN_ROWS - 1):
        pr, pi = out[-1]
        out.append((pr * ar - pi * ai, pr * ai + pi * ar))
    return out


def _s5_scan(bu, a, *, tc=512):
    seq, w2 = bu.shape
    tw = S5_BLOCK
    tc = min(tc, seq)
    assert seq % tc == 0 and w2 % (2 * tw) == 0

    def body(bu_ref, a_ref, h_ref, cr_ref, ci_ref):
        @pl.when(pl.program_id(1) == 0)
        def _():
            cr_ref[...] = jnp.zeros_like(cr_ref)
            ci_ref[...] = jnp.zeros_like(ci_ref)

        re, im = pl.ds(0, tw), pl.ds(tw, tw)
        powers = _powers(a_ref[:, re], a_ref[:, im])
        pr = jnp.concatenate([p[0] for p in powers], axis=0)
        pi = jnp.concatenate([p[1] for p in powers], axis=0)
        row_id = lax.broadcasted_iota(jnp.int32, (SCAN_ROWS, tw), 0)
        reach = {dist: tuple(jnp.where(row_id >= dist, part, 0.0) for part in powers[dist - 1]) for dist in (1, 2, 4)}

        def block(n, carry):
            hr, hi = carry
            rows = pl.ds(pl.multiple_of(n * SCAN_ROWS, SCAN_ROWS), SCAN_ROWS)
            yr, yi = bu_ref[rows, re], bu_ref[rows, im]
            for dist in (1, 2, 4):
                cr, ci = reach[dist]
                sr, si = pltpu.roll(yr, dist, 0), pltpu.roll(yi, dist, 0)
                yr, yi = yr + cr * sr - ci * si, yi + cr * si + ci * sr
            yr, yi = yr + pr * hr - pi * hi, yi + pr * hi + pi * hr
            h_ref[rows, re] = yr
            h_ref[rows, im] = yi
            return yr[SCAN_ROWS - 1:], yi[SCAN_ROWS - 1:]

        hr, hi = lax.fori_loop(0, tc // SCAN_ROWS, block, (cr_ref[...], ci_ref[...]), unroll=4)
        cr_ref[...] = hr
        ci_ref[...] = hi

    blk = pl.BlockSpec((tc, 2 * tw), lambda j, t: (t, j))
    return pl.pallas_call(
        body, name="s5_scan", grid=(w2 // (2 * tw), seq // tc),
        in_specs=[blk, pl.BlockSpec((1, 2 * tw), lambda j, t: (0, j))],
        out_specs=blk,
        out_shape=jax.ShapeDtypeStruct((seq, w2), F32),
        scratch_shapes=[pltpu.VMEM((1, tw), F32)] * 2,
        compiler_params=_params("parallel", "arbitrary"),
    )(bu, a)


def _s5_scan_bwd(d, h, a, *, tc=512):
    seq, w2 = d.shape
    tw = S5_BLOCK
    tc = min(tc, seq)
    assert seq % tc == 0 and w2 % (2 * tw) == 0
    nt = seq // tc

    def body(d_ref, h_ref, a_ref, g_ref, da_ref, cr_ref, ci_ref):
        @pl.when(pl.program_id(1) == 0)
        def _():
            cr_ref[...] = jnp.zeros_like(cr_ref)
            ci_ref[...] = jnp.zeros_like(ci_ref)
            da_ref[...] = jnp.zeros_like(da_ref)

        re, im = pl.ds(0, tw), pl.ds(tw, tw)
        powers = _powers(a_ref[:, re], a_ref[:, im])
        pr = jnp.concatenate([p[0] for p in reversed(powers)], axis=0)
        pi = jnp.concatenate([p[1] for p in reversed(powers)], axis=0)
        row_id = lax.broadcasted_iota(jnp.int32, (SCAN_ROWS, tw), 0)
        last = SCAN_ROWS - 1
        reach = {dist: tuple(jnp.where(row_id < SCAN_ROWS - dist, part, 0.0) for part in powers[dist - 1]) for dist in (1, 2, 4)}

        def block(n, carry):
            gr, gi, sr, si = carry
            rows = pl.ds(pl.multiple_of((tc // SCAN_ROWS - 1 - n) * SCAN_ROWS, SCAN_ROWS), SCAN_ROWS)
            yr, yi = d_ref[rows, re], d_ref[rows, im]
            for dist in (1, 2, 4):
                cr, ci = reach[dist]
                ur, ui = pltpu.roll(yr, SCAN_ROWS - dist, 0), pltpu.roll(yi, SCAN_ROWS - dist, 0)
                yr, yi = yr + cr * ur + ci * ui, yi + cr * ui - ci * ur
            yr, yi = yr + pr * gr + pi * gi, yi + pr * gi - pi * gr
            g_ref[rows, re] = yr
            g_ref[rows, im] = yi
            nr = jnp.where(row_id < last, pltpu.roll(yr, last, 0), gr)
            ni = jnp.where(row_id < last, pltpu.roll(yi, last, 0), gi)
            hr, hi = h_ref[rows, re], h_ref[rows, im]
            return yr[:1], yi[:1], sr + nr * hr + ni * hi, si + ni * hr - nr * hi

        zero = jnp.zeros((SCAN_ROWS, tw), F32)
        gr, gi, sr, si = lax.fori_loop(0, tc // SCAN_ROWS, block, (cr_ref[...], ci_ref[...], zero, zero), unroll=4)
        cr_ref[...] = gr
        ci_ref[...] = gi
        da_ref[:, re] += jnp.sum(sr, axis=0, keepdims=True)
        da_ref[:, im] += jnp.sum(si, axis=0, keepdims=True)

    blk = pl.BlockSpec((tc, 2 * tw), lambda j, t: (nt - 1 - t, j))
    row = pl.BlockSpec((1, 2 * tw), lambda j, t: (0, j))
    return pl.pallas_call(
        body, name="s5_scan_bwd", grid=(w2 // (2 * tw), nt),
        in_specs=[blk, blk, row],
        out_specs=[blk, row],
        out_shape=[jax.ShapeDtypeStruct((seq, w2), F32), jax.ShapeDtypeStruct((1, w2), F32)],
        scratch_shapes=[pltpu.VMEM((1, tw), F32)] * 2,
        compiler_params=_params("parallel", "arbitrary"),
    )(d, h, a)


def _pair_columns(re, im, axis):
    shape = re.shape
    split = shape[:axis] + (shape[axis] // S5_BLOCK, S5_BLOCK) + shape[axis + 1:]
    both = jnp.stack([re.reshape(split), im.reshape(split)], axis=axis + 1)
    return both.reshape(shape[:axis] + (2 * shape[axis],) + shape[axis + 1:])


def _unpair_columns(t, axis):
    shape = t.shape
    both = t.reshape(shape[:axis] + (shape[axis] // (2 * S5_BLOCK), 2, S5_BLOCK) + shape[axis + 1:])
    half = shape[:axis] + (shape[axis] // 2,) + shape[axis + 1:]
    return (lax.index_in_dim(both, 0, axis + 1, keepdims=False).reshape(half),
            lax.index_in_dim(both, 1, axis + 1, keepdims=False).reshape(half))


def _block_diag(t):
    g, a, b = t.shape
    eye = jnp.eye(g, dtype=t.dtype)
    return (t[:, :, None, :] * eye[:, None, :, None]).reshape(g * a, g * b)


def _block_diag_part(m, g):
    a, b = m.shape[0] // g, m.shape[1] // g
    return jnp.moveaxis(jnp.diagonal(m.reshape(g, a, g, b), axis1=0, axis2=2), -1, 0)


def _gelu_glu(y, gate_pre):
    z = jax.nn.gelu(y)
    return z * jax.nn.sigmoid(gate_pre)


def _s5_fwd(u, p, w_glu):
    lr, li = p["s5_lambda_re"][0], p["s5_lambda_im"][0]
    ldt = p["s5_log_dt"][0][:, None]
    br = p["s5_b_re"][0].transpose(2, 0, 1)
    bi = p["s5_b_im"][0].transpose(2, 0, 1)
    ar, ai, bbr, bbi = _s5_prep(lr, li, ldt, br, bi)
    a = _pair_columns(ar.reshape(1, S5_LANES), ai.reshape(1, S5_LANES), 1)
    bmat = _pair_columns(_block_diag(bbr.transpose(1, 0, 2)), _block_diag(bbi.transpose(1, 0, 2)), 1)
    cmat = _pair_columns(_block_diag(p["s5_c_re"][0].transpose(0, 2, 1)),
                         -_block_diag(p["s5_c_im"][0].transpose(0, 2, 1)), 0)
    bmat, cmat = bmat.astype(MXU_DTYPE), cmat.astype(MXU_DTYPE)
    bu = _mm(u, bmat, name="s5_bu", diagonal=S5_DIAG)
    h = _s5_scan(bu, a)
    d = p["s5_d"]
    y = _mm(h, cmat, name="s5_y", epilogue=lambda acc, uv, dv: acc + dv * uv, extras=[u, d], diagonal=S5_DIAG)
    z = _rowmap(jax.nn.gelu, [y], "r", [(y.shape, MXU_DTYPE, "r")], name="s5_gelu", tl=512)
    gate_pre = _mm(z, w_glu, name="s5_glu")
    out = _rowmap(_gelu_glu, [y, gate_pre], "rr", [(y.shape, F32, "r")], name="s5_gate", tl=512)
    return out, (u, lr, li, ldt, br, bi, a, bmat, cmat, h, y, z, gate_pre)


def _s5_bwd(dout, saved, p, w_glu):
    u, lr, li, ldt, br, bi, a, bmat, cmat, h, y, z, gate_pre = saved
    d = p["s5_d"]

    def gate_bwd(dov, yv, gv):
        zv = jax.nn.gelu(yv)
        sg = jax.nn.sigmoid(gv)
        return dov * sg, dov * zv * sg * (1.0 - sg)

    dz_direct, dgate = _rowmap(gate_bwd, [dout, y, gate_pre], "rrr", [(y.shape, F32, "r"), (y.shape, MXU_DTYPE, "r")],
                               name="s5_gate_bwd", tl=512)
    dw_glu = _mm(z, dgate, ta=True, name="s5_dwglu", out_dtype=WIRE_DTYPE)
    dz = _mm(dgate, w_glu, tb=True, name="s5_dz", epilogue=lambda acc, prev: acc + prev, extras=[dz_direct])

    def gelu_bwd(dzv, yv, uv, dvv):
        _, vjp = jax.vjp(jax.nn.gelu, yv)
        dy = vjp(dzv)[0]
        return dy, dy * dvv, jnp.sum(dy * uv, axis=0, keepdims=True)

    dy, du_skip, dd = _rowmap(gelu_bwd, [dz, y, u, d], "rrrc",
                              [(y.shape, F32, "r"), (y.shape, F32, "r"), (d.shape, F32, "a")], name="s5_gelu_bwd", tl=512)
    dcmat = _mm(h, dy, ta=True, name="s5_dc", diagonal=S5_DIAG)
    dstate = _mm(dy, cmat, tb=True, name="s5_dstate", diagonal=S5_DIAG)
    g, da = _s5_scan_bwd(dstate, h, a)
    du = _mm(g, bmat, tb=True, name="s5_du", epilogue=lambda acc, prev: acc + prev, extras=[du_skip], out_dtype=MXU_DTYPE,
             diagonal=S5_DIAG)
    dbmat = _mm(u, g, ta=True, name="s5_db", diagonal=S5_DIAG)
    dbbr, dbbi = (_block_diag_part(t, S5_GROUPS).transpose(1, 0, 2) for t in _unpair_columns(dbmat, 1))
    dar, dai = _unpair_columns(da, 1)
    cts = (dar.reshape(S5_GROUPS, S5_STATE), dai.reshape(S5_GROUPS, S5_STATE), dbbr, dbbi)
    dlr, dli, dldt, dbr, dbi = _s5_prep_bwd(lr, li, ldt, br, bi, cts)
    dcr, dci = (_block_diag_part(t, S5_GROUPS).transpose(0, 2, 1) for t in _unpair_columns(dcmat, 0))
    grads = {
        "s5_lambda_re": dlr[None], "s5_lambda_im": dli[None], "s5_log_dt": dldt[:, 0][None],
        "s5_b_re": dbr.transpose(1, 2, 0)[None], "s5_b_im": dbi.transpose(1, 2, 0)[None],
        "s5_c_re": dcr[None], "s5_c_im": -dci[None], "s5_d": dd,
    }
    return du, dw_glu, grads


def _mix0_fwd(x, g, p, full, late):
    (h, proj), _ = _norm_proj(x, g, full[("ab_w_in", 0)], name="mix0_in")
    u = proj[:, :S5_WIDTH]
    job, keys = late.gather_job("sb_fwd") if late else (None, [])
    (o, lsum), got = _attend(proj, job=job)
    full.update(zip(keys, got))
    w_glu, w_out = full[("s5_w_glu", 0)], full[("ab_w_out", 0)]
    y_a, s5_saved = _s5_fwd(u, p, w_glu)
    mix = jnp.concatenate([y_a, o], axis=1).astype(MXU_DTYPE)
    x2 = _mm(mix, w_out, name="mix0_out", epilogue=lambda acc, xv: xv + acc, extras=[x])
    return x2, (x, h, proj, lsum, mix, s5_saved)


def _mix0_bwd(dx2, saved, g, p, full, grads, late):
    x, h, proj, lsum, mix, s5_saved = saved
    w_in, w_glu, w_out = full[("ab_w_in", 0)], full[("s5_w_glu", 0)], full[("ab_w_out", 0)]
    dmix = _mm(dx2, w_out, tb=True, name="mix0_dmix")
    grads[("ab_w_out", 0)] = _mm(mix, dx2, ta=True, name="mix0_dwout", out_dtype=WIRE_DTYPE)
    du, grads[("s5_w_glu", 0)], s5_grads = _s5_bwd(dmix[:, :S5_WIDTH], s5_saved, p, w_glu)
    job, keys = late.scatter_job(grads) if late else (None, [])
    (dq, dk, dv), got = _attend_bwd(proj, lsum, dmix, job=job)
    _note(late, keys, got)
    dproj = jnp.concatenate([du] + [t.astype(MXU_DTYPE) for t in (dq, dk, dv)], axis=1)
    grads[("ab_w_in", 0)] = _mm(h, dproj, ta=True, name="mix0_dwin", out_dtype=WIRE_DTYPE)
    job, keys = late.scatter_job(grads) if late else (None, [])
    (dx, dg), got = _proj_norm_bwd([(dproj, w_in, "DF")], x, g, dx2, name="mix0_dh", job=job)
    _note(late, keys, got)
    return dx, dg, s5_grads


def _shift_down(t, n):
    rows = lax.broadcasted_iota(jnp.int32, t.shape, 0)
    return jnp.where(rows >= n, pltpu.roll(t, n, 0), 0.0)


def _shift_up(t, n):
    rows = lax.broadcasted_iota(jnp.int32, t.shape, 0)
    return jnp.where(rows < t.shape[0] - n, pltpu.roll(t, t.shape[0] - n, 0), 0.0)


def _conv_fwd(proj, cw, *, tc=128):
    seq, c3 = proj.shape
    ch = c3 // 3
    nb = ch // tc

    def body(b_ref, c_ref, v_ref, w_ref, m_ref):
        pv = c_ref[...] * v_ref[...]
        w = w_ref[...]
        y = w[2:3] * pv + w[1:2] * _shift_down(pv, 1) + w[0:1] * _shift_down(pv, 2)
        m_ref[...] = (b_ref[...] * y).astype(m_ref.dtype)

    col = lambda part: pl.BlockSpec((seq, tc), lambda j: (0, part * nb + j))
    return pl.pallas_call(
        body, name="conv_fwd", grid=(nb,),
        in_specs=[col(0), col(1), col(2), pl.BlockSpec((3, tc), lambda j: (0, j))],
        out_specs=pl.BlockSpec((seq, tc), lambda j: (0, j)),
        out_shape=jax.ShapeDtypeStruct((seq, ch), MXU_DTYPE),
        compiler_params=_params("parallel"),
    )(proj, proj, proj, cw)


def _conv_bwd(proj, cw, dm, *, tc=128):
    seq, c3 = proj.shape
    ch = c3 // 3
    nb = ch // tc

    def body(b_ref, c_ref, v_ref, w_ref, dm_ref, dproj_ref, dw_ref, dc_ref, dv_ref):
        part = pl.program_id(1)

        @pl.when(part == 0)
        def _():
            cv, vv, dmv = c_ref[...], v_ref[...], dm_ref[...]
            pv = cv * vv
            w = w_ref[...]
            p1, p2 = _shift_down(pv, 1), _shift_down(pv, 2)
            y = w[2:3] * pv + w[1:2] * p1 + w[0:1] * p2
            dproj_ref[...] = (dmv * y).astype(dproj_ref.dtype)
            dy = dmv * b_ref[...]
            dp = w[2:3] * dy + w[1:2] * _shift_up(dy, 1) + w[0:1] * _shift_up(dy, 2)
            dc_ref[...] = (dp * vv).astype(dc_ref.dtype)
            dv_ref[...] = (dp * cv).astype(dv_ref.dtype)
            dw_ref[...] = jnp.concatenate([jnp.sum(dy * p2, axis=0, keepdims=True), jnp.sum(dy * p1, axis=0, keepdims=True),
                                           jnp.sum(dy * pv, axis=0, keepdims=True)], axis=0)

        @pl.when(part == 1)
        def _():
            dproj_ref[...] = dc_ref[...]

        @pl.when(part == 2)
        def _():
            dproj_ref[...] = dv_ref[...]

    col = lambda part: pl.BlockSpec((seq, tc), lambda j, t: (0, part * nb + j))
    small = pl.BlockSpec((3, tc), lambda j, t: (0, j))
    return pl.pallas_call(
        body, name="conv_bwd", grid=(nb, 3),
        in_specs=[col(0), col(1), col(2), small, pl.BlockSpec((seq, tc), lambda j, t: (0, j))],
        out_specs=[pl.BlockSpec((seq, tc), lambda j, t: (0, t * nb + j)), small],
        out_shape=[jax.ShapeDtypeStruct((seq, c3), MXU_DTYPE), jax.ShapeDtypeStruct((3, ch), F32)],
        scratch_shapes=[pltpu.VMEM((seq, tc), MXU_DTYPE)] * 2,
        compiler_params=_params("parallel", "arbitrary"),
    )(proj, proj, proj, cw, dm)


def _mix1_fwd(x, g, full, late):
    job, keys = late.gather_job("mix1_in") if late else (None, [])
    (h, proj), got = _norm_proj(x, g, full[("sc_w_in", 0)], name="mix1_in", job=job)
    full.update(zip(keys, got))
    m = _conv_fwd(proj, full[("sc_conv_w", 0)])
    x2 = _mm(m, full[("sc_w_out", 0)], name="mix1_out", epilogue=lambda acc, xv: xv + acc, extras=[x])
    return x2, (x, h, proj, m)


def _mix1_bwd(dx2, saved, g, w_in, cw, w_out):
    x, h, proj, m = saved
    dm = _mm(dx2, w_out, tb=True, name="mix1_dm")
    dw_out = _mm(m, dx2, ta=True, name="mix1_dwout", out_dtype=WIRE_DTYPE)
    dproj, dcw = _conv_bwd(proj, cw, dm)
    dw_in = _mm(h, dproj, ta=True, name="mix1_dwin", out_dtype=WIRE_DTYPE)
    (dx, dg), _ = _proj_norm_bwd([(dproj, w_in, "DF")], x, g, dx2, name="mix1_dh")
    return dx, dg, dw_in, dcw, dw_out


def _loss_head(x, g, target):
    feat = x.shape[1]

    def fn(xv, gv, tv):
        err = _rms_fwd(xv, gv) - tv
        dx, dg = _rms_bwd(err / feat, xv, gv)
        return jnp.sum(err * err, keepdims=True) * (0.5 / feat), dx, dg

    return _rowmap(fn, [x, g, target], "rcr", [((1, 1), F32, "a"), (x.shape, F32, "r"), (g.shape, F32, "a")],
                   name="loss_head", tl=256)


def _slot(ref, place, chip=None, half=None):
    axis, width = place
    shape = list(ref.shape)
    start = [0, 0]
    if chip is not None:
        start[axis], shape[axis] = chip * width, width
    if half is not None:
        h_axis = 0 if shape[0] % 32 == 0 else 1
        shape[h_axis] //= 2
        start[h_axis] = start[h_axis] + half * shape[h_axis]
    hint = lambda s, d: s if isinstance(s, int) else pl.multiple_of(s, 128 if d == 1 else 8)
    return ref.at[tuple(pl.ds(hint(s, d), n) for d, (s, n) in enumerate(zip(start, shape)))]


class _Exchange:
    def __init__(self, kind, arrays, places):
        self.kind, self.arrays, self.places, self.n = kind, list(arrays), list(places), len(arrays)
        self.out_shape = []
        for t, (axis, width) in zip(self.arrays, self.places):
            if kind == "gather":
                shape = list(t.shape)
                shape[axis] = N_CHIPS * width
            else:
                shape = [N_CHIPS] + list(t.shape)
                shape[1 + axis] = width
            self.out_shape.append(jax.ShapeDtypeStruct(tuple(shape), t.dtype))
        n = self.n
        self.scratch = [pltpu.SemaphoreType.DMA((3 * n,)) for _ in range(4 if kind == "gather" else 2)]
        self.scratch.append(pltpu.SemaphoreType.DMA((n,)))

    def _copies(self, ins, outs, sems):
        x, y, c = lax.axis_index("x"), lax.axis_index("y"), lax.axis_index("c")
        peers = [(1 - x, y), (x, 1 - y), (1 - x, 1 - y)]
        remote = lambda src, dst, send, recv, k, to: pltpu.make_async_remote_copy(
            src_ref=src, dst_ref=dst, send_sem=send.at[k], recv_sem=recv.at[k], device_id=to, device_id_type=MESH_ID)
        local, ici, d2d = [], [], []
        for a in range(self.n):
            place = self.places[a]
            if self.kind == "gather":
                local.append(pltpu.make_async_copy(ins[a], _slot(outs[a], place, 2 * x + y), sems[4].at[a]))
                for r, (px, py) in enumerate(peers):
                    ici.append(remote(_slot(ins[a], place, None, c), _slot(outs[a], place, 2 * x + y, c),
                                      sems[0], sems[1], 3 * a + r, (px, py, c)))
                    landed = _slot(outs[a], place, 2 * px + py, c)
                    d2d.append(remote(landed, landed, sems[2], sems[3], 3 * a + r, (x, y, 1 - c)))
            else:
                local.append(pltpu.make_async_copy(_slot(ins[a], place, 2 * x + y), outs[a].at[3], sems[2].at[a]))
                for r, (px, py) in enumerate(peers):
                    ici.append(remote(_slot(ins[a], place, 2 * px + py), outs[a].at[r], sems[0], sems[1], 3 * a + r, (px, py, c)))
        return local, ici, d2d

    def start(self, ins, outs, sems):
        local, ici, _ = self._copies(ins, outs, sems)
        for cp in local + ici:
            cp.start()

    def relay(self, ins, outs, sems):
        _, ici, d2d = self._copies(ins, outs, sems)
        for arrived, onward in zip(ici, d2d):
            arrived.wait_recv()
            onward.start()

    def finish(self, ins, outs, sems):
        local, ici, d2d = self._copies(ins, outs, sems)
        for cp in local + d2d:
            cp.wait()
        for cp in ici:
            cp.wait_send() if d2d else cp.wait()


def _exchange_call(job, name):
    n = job.n

    def body(*refs):
        ins, outs, sems = refs[:n], refs[n:2 * n], refs[2 * n:]
        job.start(ins, outs, sems)
        job.relay(ins, outs, sems)
        job.finish(ins, outs, sems)

    return pl.pallas_call(
        body, name=name, in_specs=[ANY_SPEC] * n, out_specs=[ANY_SPEC] * n, out_shape=job.out_shape,
        scratch_shapes=job.scratch, compiler_params=pltpu.CompilerParams(has_side_effects=True),
    )(*job.arrays)


def _carried_call(body, *, name, grid, in_specs, out_specs, out_shape, semantics, operands, scratch_shapes=(), job=None):
    scratch_shapes = list(scratch_shapes)
    if job is None:
        return pl.pallas_call(body, name=name, grid=grid, in_specs=in_specs, out_specs=out_specs, out_shape=out_shape,
                              scratch_shapes=scratch_shapes, compiler_params=_params(*semantics))(*operands), []
    n_in, n_out, n, n_scr = len(in_specs), len(out_specs), job.n, len(scratch_shapes)
    steps = math.prod(grid)

    def wrapped(*refs):
        ins, job_ins = refs[:n_in], refs[n_in:n_in + n]
        outs, job_outs = refs[n_in + n:n_in + n + n_out], refs[n_in + n + n_out:n_in + 2 * n + n_out]
        outs = outs + refs[n_in + 2 * n + n_out:n_in + 2 * n + n_out + n_scr]
        sems = refs[n_in + 2 * n + n_out + n_scr:]
        step = functools.reduce(lambda acc, d: acc * grid[d] + pl.program_id(d), range(len(grid)), 0)

        @pl.when(step == 0)
        def _():
            job.start(job_ins, job_outs, sems)

        @pl.when(step == (3 * steps) // 4)
        def _():
            job.relay(job_ins, job_outs, sems)

        body(*ins, *outs)

        @pl.when(step == steps - 1)
        def _():
            job.finish(job_ins, job_outs, sems)

    res = pl.pallas_call(
        wrapped, name=name, grid=grid, in_specs=list(in_specs) + [ANY_SPEC] * n, out_specs=list(out_specs) + [ANY_SPEC] * n,
        out_shape=list(out_shape) + job.out_shape, scratch_shapes=scratch_shapes + job.scratch,
        compiler_params=pltpu.CompilerParams(dimension_semantics=("arbitrary",) * len(grid), vmem_limit_bytes=VMEM_LIMIT,
                                             has_side_effects=True),
    )(*operands, *job.arrays)
    return res[:n_out], res[n_out:]


def _swap_with_sibling(parts):
    n = len(parts)

    def body(*refs):
        ins, outs = refs[:n], refs[n:2 * n]
        send, recv = refs[2 * n:]
        sibling = (lax.axis_index("x"), lax.axis_index("y"), 1 - lax.axis_index("c"))
        copies = [pltpu.make_async_remote_copy(src_ref=ins[a], dst_ref=outs[a], send_sem=send.at[a], recv_sem=recv.at[a],
                                               device_id=sibling, device_id_type=MESH_ID) for a in range(n)]
        for cp in copies:
            cp.start()
        for cp in copies:
            cp.wait()

    return pl.pallas_call(
        body, name="swap_with_sibling",
        in_specs=[ANY_SPEC] * n, out_specs=[ANY_SPEC] * n,
        out_shape=[jax.ShapeDtypeStruct(p.shape, p.dtype) for p in parts],
        scratch_shapes=[pltpu.SemaphoreType.DMA((n,)), pltpu.SemaphoreType.DMA((n,))],
        compiler_params=pltpu.CompilerParams(has_side_effects=True),
    )(*parts)


def _sum_all_devices(t):
    rows = t.shape[0]

    def body(t_ref, o_ref, slots, send, recv):
        x, y, c = lax.axis_index("x"), lax.axis_index("y"), lax.axis_index("c")
        me = 4 * x + 2 * y + c
        slots[me] = t_ref[...]
        copies = []
        for m in range(1, 8):
            peer = (x ^ (m >> 2), y ^ ((m >> 1) & 1), c ^ (m & 1))
            cp = pltpu.make_async_remote_copy(src_ref=t_ref, dst_ref=slots.at[me], send_sem=send.at[m - 1],
                                              recv_sem=recv.at[m - 1], device_id=peer, device_id_type=MESH_ID)
            cp.start()
            copies.append(cp)
        for cp in copies:
            cp.wait()
        acc = slots[0]
        for dev in range(1, 8):
            acc = acc + slots[dev]
        o_ref[...] = acc

    vmem = pl.BlockSpec(memory_space=pltpu.VMEM)
    return pl.pallas_call(
        body, name="sum_all_devices", in_specs=[vmem], out_specs=vmem,
        out_shape=jax.ShapeDtypeStruct(t.shape, F32),
        scratch_shapes=[pltpu.VMEM((8, rows, 128), F32), pltpu.SemaphoreType.DMA((7,)), pltpu.SemaphoreType.DMA((7,))],
        compiler_params=pltpu.CompilerParams(vmem_limit_bytes=VMEM_LIMIT, has_side_effects=True),
    )(t)


def _adamw(w, g, m, v):
    m = ADAM_B1 * m + (1.0 - ADAM_B1) * g
    v = ADAM_B2 * v + (1.0 - ADAM_B2) * jnp.square(g)
    m_hat = m / (1.0 - ADAM_B1 ** ADAM_STEP)
    v_hat = v / (1.0 - ADAM_B2 ** ADAM_STEP)
    return -ADAM_LR * (m_hat / (jnp.sqrt(v_hat) + ADAM_EPS) + ADAM_WD * w), m, v


def _chip_sum(received, name):
    rows, cols = received.shape[1:]
    tl = _row_block(rows, 512)

    def body(r_ref, o_ref):
        o_ref[...] = ((r_ref[0].astype(F32) + r_ref[1].astype(F32)) + r_ref[2].astype(F32)) + r_ref[3].astype(F32)

    return pl.pallas_call(body, name=name, grid=(rows // tl,),
                          in_specs=[pl.BlockSpec((N_CHIPS, tl, cols), lambda i: (0, i, 0))],
                          out_specs=pl.BlockSpec((tl, cols), lambda i: (i, 0)),
                          out_shape=jax.ShapeDtypeStruct((rows, cols), F32), compiler_params=_params("parallel"))(received)


def _adamw_layer(w, m, v, p_mine, p_other, layer, prev, name):
    _, rows, cols = w.shape
    tl = _row_block(rows, 512)

    def body(w_ref, m_ref, v_ref, pa_ref, pb_ref, *rest):
        g = pa_ref[:, :cols] + pb_ref[:, :cols]
        for o_ref, val in zip(rest[-4:], (g,) + _adamw(w_ref[...], g, m_ref[...], v_ref[...])):
            o_ref[...] = val

    stacked = pl.BlockSpec((None, tl, cols), lambda i: (layer, i, 0))
    part = pl.BlockSpec((tl, p_mine.shape[1]), lambda i: (i, 0))
    kept = list(prev) if prev else []
    return pl.pallas_call(
        body, name=name, grid=(rows // tl,),
        in_specs=[stacked] * 3 + [part] * 2 + [ANY_SPEC] * len(kept),
        out_specs=[stacked] * 4, out_shape=[jax.ShapeDtypeStruct(w.shape, F32)] * 4,
        input_output_aliases={5 + k: k for k in range(len(kept))},
        compiler_params=_params("parallel"),
    )(w, m, v, p_mine, p_other, *kept)


def _adamw_small(w, g, m, v):
    def fn(wv, gv, mv, vv):
        return _adamw(wv, gv, mv, vv)

    return _rowmap(fn, [w, g, m, v], "rrrr", [(w.shape, F32, "r")] * 3, name="adamw_small", tl=w.shape[0])


WEIGHTS = ['ffn1_norm', 'ffn1_w_gate', 'ffn1_w_up', 'ffn1_w_down', 'mix_norm', 'ffn2_norm', 'ffn2_w_gate', 'ffn2_w_up',
           'ffn2_w_down', 'ab_w_in', 's5_lambda_re', 's5_lambda_im', 's5_log_dt', 's5_b_re', 's5_b_im', 's5_c_re', 's5_c_im',
           's5_d', 's5_w_glu', 'ab_w_out', 'sc_w_in', 'sc_conv_w', 'sc_w_out', 'final_norm']
SHARDED = {'ffn1_w_gate': (0, FF_SLOT), 'ffn1_w_up': (0, FF_SLOT), 'ffn1_w_down': (0, FF_SLOT),
           'ffn2_w_gate': (0, FF_SLOT), 'ffn2_w_up': (0, FF_SLOT), 'ffn2_w_down': (0, FF_SLOT),
           'ab_w_in': (1, 512), 's5_w_glu': (0, 128), 'ab_w_out': (0, 256), 'sc_w_in': (1, 768), 'sc_conv_w': (1, 256),
           'sc_w_out': (0, 256)}
SWAPPED = ('ffn1_w_gate', 'ffn1_w_up', 'ffn2_w_gate', 'ffn2_w_up')
SMALL = [n for n in WEIGHTS if n not in SHARDED]


def _held(name, t):
    return jnp.swapaxes(t, 1, 2) if name in SWAPPED else t


def _pack(arrays):
    rows = []
    for t in arrays:
        flat = t.reshape(-1)
        rows.append(jnp.pad(flat, (0, (-flat.shape[0]) % 128)))
    flat = jnp.concatenate(rows)
    return jnp.pad(flat, (0, (-flat.shape[0]) % 1024)).reshape(-1, 128)


def _unpack(packed, like):
    flat, out, pos = packed.reshape(-1), [], 0
    for t in like:
        out.append(flat[pos:pos + t.size].reshape(t.shape))
        pos += t.size + (-t.size) % 128
    return out


def _local_grads(x, target, p, full, late=None):
    small, grads, saved = {}, {}, []
    for layer in range(2):
        x, s1 = _ffn_fwd(x, p["ffn1_norm"][layer:layer + 1], full, "ffn1", layer, late)
        if layer == 0:
            x, sm = _mix0_fwd(x, p["mix_norm"][0:1], p, full, late)
        else:
            x, sm = _mix1_fwd(x, p["mix_norm"][1:2], full, late)
        x, s2 = _ffn_fwd(x, p["ffn2_norm"][layer:layer + 1], full, "ffn2", layer, late)
        saved.append((s1, sm, s2))
    loss, dx, dg_final = _loss_head(x, p["final_norm"][None], target)
    small["final_norm"] = dg_final[0]
    gains = {n: [None, None] for n in ("ffn1_norm", "mix_norm", "ffn2_norm")}

    def ffn_bwd(which, layer, dx, s):
        dx, dg = _ffn_bwd(dx, s, p[f"{which}_norm"][layer:layer + 1], full, which, layer, grads, late,
                          inline=(which, layer) in (("ffn2", 1), ("ffn1", 0)))
        gains[f"{which}_norm"][layer] = dg[0]
        return dx

    for layer in (1, 0):
        s1, sm, s2 = saved[layer]
        dx = ffn_bwd("ffn2", layer, dx, s2)
        if layer == 0:
            dx, dg, s5_grads = _mix0_bwd(dx, sm, p["mix_norm"][0:1], p, full, grads, late)
            small.update(s5_grads)
        else:
            dx, dg, dw_in, dcw, dw_out = _mix1_bwd(dx, sm, p["mix_norm"][1:2], full[("sc_w_in", 0)], full[("sc_conv_w", 0)],
                                                   full[("sc_w_out", 0)])
            grads.update({("sc_w_in", 0): dw_in, ("sc_conv_w", 0): dcw.astype(WIRE_DTYPE), ("sc_w_out", 0): dw_out})
        gains["mix_norm"][layer] = dg[0]
        dx = ffn_bwd("ffn1", layer, dx, s1)
    small.update({n: jnp.stack(pair) for n, pair in gains.items()})
    return loss, dx, small, grads


_GATHER_PLAN = {
    "gather_early": [("ffn1_w_gate", 0), ("ffn1_w_up", 0)],
    "ffn1_0_up": [("ffn1_w_down", 0), ("ab_w_in", 0)],
    "sb_fwd": [("s5_w_glu", 0), ("ab_w_out", 0), ("ffn2_w_gate", 0), ("ffn2_w_up", 0), ("ffn2_w_down", 0),
               ("ffn1_w_gate", 1), ("ffn1_w_up", 1), ("ffn1_w_down", 1)],
    "ffn2_0_up": [("sc_w_in", 0), ("sc_conv_w", 0), ("sc_w_out", 0)],
    "ffn1_1_up": [("ffn2_w_gate", 1), ("ffn2_w_up", 1)],
    "mix1_in": [("ffn2_w_down", 1)],
}


class _Late:
    def __init__(self, shards, places):
        self.shards, self.places = shards, places
        self.sent, self.received = set(), {}

    def gather_job(self, carrier):
        keys = _GATHER_PLAN.get(carrier, [])
        if not keys:
            return None, []
        return _Exchange("gather", [self.shards[k] for k in keys], [self.places[k] for k in keys]), keys

    def scatter_job(self, grads):
        keys = [k for k in grads if k not in self.sent]
        if not keys:
            return None, []
        self.sent.update(keys)
        return _Exchange("scatter", [grads[k] for k in keys], [self.places[k] for k in keys]), keys


def kernel(x, ffn1_norm, ffn1_w_gate, ffn1_w_up, ffn1_w_down, mix_norm, ffn2_norm, ffn2_w_gate, ffn2_w_up, ffn2_w_down, ab_w_in, s5_lambda_re, s5_lambda_im, s5_log_dt, s5_b_re, s5_b_im, s5_c_re, s5_c_im, s5_d, s5_w_glu, ab_w_out, sc_w_in, sc_conv_w, sc_w_out, final_norm, loss_target, m_ffn1_norm, m_ffn1_w_gate, m_ffn1_w_up, m_ffn1_w_down, m_mix_norm, m_ffn2_norm, m_ffn2_w_gate, m_ffn2_w_up, m_ffn2_w_down, m_ab_w_in, m_s5_lambda_re, m_s5_lambda_im, m_s5_log_dt, m_s5_b_re, m_s5_b_im, m_s5_c_re, m_s5_c_im, m_s5_d, m_s5_w_glu, m_ab_w_out, m_sc_w_in, m_sc_conv_w, m_sc_w_out, m_final_norm, v_ffn1_norm, v_ffn1_w_gate, v_ffn1_w_up, v_ffn1_w_down, v_mix_norm, v_ffn2_norm, v_ffn2_w_gate, v_ffn2_w_up, v_ffn2_w_down, v_ab_w_in, v_s5_lambda_re, v_s5_lambda_im, v_s5_log_dt, v_s5_b_re, v_s5_b_im, v_s5_c_re, v_s5_c_im, v_s5_d, v_s5_w_glu, v_ab_w_out, v_sc_w_in, v_sc_conv_w, v_sc_w_out, v_final_norm):
    args = dict(locals())
    p = {n: _held(n, args[n]) for n in WEIGHTS}
    mom = {n: _held(n, args["m_" + n]) for n in WEIGHTS}
    var = {n: _held(n, args["v_" + n]) for n in WEIGHTS}

    keys = [(n, layer) for n in SHARDED for layer in range(p[n].shape[0])]
    shards, places = {}, {}
    for n, layer in keys:
        axis, width = SHARDED[n]
        t = p[n][layer] if n == "sc_conv_w" else p[n][layer].astype(MXU_DTYPE)
        pad = [(0, 0), (0, 0)]
        pad[axis] = (0, width - t.shape[axis])
        shards[(n, layer)], places[(n, layer)] = jnp.pad(t, pad), (axis, width)
    late = _Late(shards, places)
    job, first = late.gather_job("gather_early")
    full = dict(zip(first, _exchange_call(job, "gather_early")))

    loss, dx, small, grads = _local_grads(x[0], loss_target[0], p, full, late)
    loss = lax.psum(loss[0, 0], ("x", "y", "c"))
    assert set(late.received) == set(keys), "a gradient was left without a carrier"

    partial = [_chip_sum(late.received[(n, layer)], name=f"chip_sum_{n}_{layer}") for n, layer in keys]
    other = _swap_with_sibling(partial)
    out = {}
    for (n, layer), mine, theirs in zip(keys, partial, other):
        out[n] = _adamw_layer(p[n], mom[n], var[n], mine, theirs, layer, out.get(n), name=f"adamw_{n}_{layer}")
    out = {n: [_held(n, t) for t in res] for n, res in out.items()}

    like = [p[n] for n in SMALL]
    g_small = _sum_all_devices(_pack([small[n] for n in SMALL]))
    d_small, m_small, v_small = _adamw_small(_pack(like), g_small, _pack([mom[n] for n in SMALL]), _pack([var[n] for n in SMALL]))
    for k, packed in enumerate((g_small, d_small, m_small, v_small)):
        for n, t in zip(SMALL, _unpack(packed, like)):
            out.setdefault(n, [None] * 4)[k] = t

    return (loss, dx[None], *[out[n][0] for n in WEIGHTS], *[out[n][1] for n in WEIGHTS],
            *[out[n][2] for n in WEIGHTS], *[out[n][3] for n in WEIGHTS])
```

```python
import functools
import math

import jax
import jax.numpy as jnp
from jax import lax
from jax.experimental import pallas as pl
from jax.experimental.pallas import tpu as pltpu

F32 = jnp.float32
MXU_DTYPE = jnp.bfloat16
WIRE_DTYPE = jnp.bfloat16
MESH_ID = pl.DeviceIdType.MESH

D_MODEL = 1024
D_FF = 2752
N_CHIPS = 4
FF_SHARD = D_FF // N_CHIPS
FF_SLOT = 768
FF_PAD = N_CHIPS * FF_SLOT
S5_WIDTH = 512
S5_GROUP = 16
S5_GROUPS = 32
S5_STATE = 64
S5_LANES = S5_GROUPS * S5_STATE
S5_BLOCK = 512
S5_DIAG = S5_LANES // S5_BLOCK
SB_HEADS = 8
SB_DH = 64
SB_SCALE = 0.125
SB_PACK = 2
SB_QUERIES = 1024
SB_KEYS = 256
EPS = 1e-6
ADAM_LR, ADAM_B1, ADAM_B2, ADAM_EPS, ADAM_WD, ADAM_STEP = 0.001, 0.9, 0.999, 1e-08, 0.01, 10
VMEM_LIMIT = 56 * 1024 * 1024

ANY_SPEC = pl.BlockSpec(memory_space=pl.ANY)


def _params(*sem):
    return pltpu.CompilerParams(dimension_semantics=sem or None, vmem_limit_bytes=VMEM_LIMIT)


def _mm(a, b, *, name, ta=False, tb=False, out_dtype=F32, epilogue=None, extras=(), tm=1024, tn=1024, tk=1024, job=None):
    m, k = (a.shape[1], a.shape[0]) if ta else a.shape
    n = b.shape[0] if tb else b.shape[1]
    tm, tn, tk = min(tm, m), min(tn, n), min(tk, k)
    assert m % tm == 0 and n % tn == 0 and k % tk == 0, (name, m, n, k)
    grid = (m // tm, n // tn, k // tk)
    a_spec = pl.BlockSpec((tk, tm), lambda i, j, kk: (kk, i)) if ta else pl.BlockSpec((tm, tk), lambda i, j, kk: (i, kk))
    b_spec = pl.BlockSpec((tn, tk), lambda i, j, kk: (j, kk)) if tb else pl.BlockSpec((tk, tn), lambda i, j, kk: (kk, j))
    nk = grid[2]
    ex_specs = []
    for e in extras:
        if e.shape == (m, n):
            ex_specs.append(pl.BlockSpec((tm, tn), lambda i, j, kk: (i, j)))
        elif e.shape == (1, n):
            ex_specs.append(pl.BlockSpec((1, tn), lambda i, j, kk: (0, j)))
        else:
            assert e.shape == (m, 1), (name, e.shape)
            ex_specs.append(pl.BlockSpec((tm, 1), lambda i, j, kk: (i, 0)))
    dims = (((0 if ta else 1,), (1 if tb else 0,)), ((), ()))
    n_ex = len(extras)

    out_dtypes = list(out_dtype) if isinstance(out_dtype, (list, tuple)) else [out_dtype]
    n_out = len(out_dtypes)

    def body(a_ref, b_ref, *rest):
        ex_refs, o_refs = rest[:n_ex], rest[n_ex:n_ex + n_out]

        def product():
            return lax.dot_general(a_ref[...].astype(MXU_DTYPE), b_ref[...].astype(MXU_DTYPE), dims, preferred_element_type=F32)

        def finish(r):
            if epilogue is not None:
                r = epilogue(r, *[e[...] for e in ex_refs])
            for o_ref, val in zip(o_refs, r if isinstance(r, (tuple, list)) else (r,)):
                o_ref[...] = val.astype(o_ref.dtype)

        if nk == 1:
            finish(product())
            return
        acc_ref, kk = rest[n_ex + n_out], pl.program_id(2)

        @pl.when(kk == 0)
        def _():
            acc_ref[...] = jnp.zeros_like(acc_ref)

        acc_ref[...] += product()

        @pl.when(kk == nk - 1)
        def _():
            finish(acc_ref[...])

    res, got = _carried_call(
        body, name=name, grid=grid,
        in_specs=[a_spec, b_spec, *ex_specs],
        out_specs=[pl.BlockSpec((tm, tn), lambda i, j, kk: (i, j))] * n_out,
        out_shape=[jax.ShapeDtypeStruct((m, n), dt) for dt in out_dtypes],
        scratch_shapes=[pltpu.VMEM((tm, tn), F32)] if nk > 1 else [],
        semantics=("parallel", "parallel", "arbitrary"), operands=(a, b, *extras), job=job)
    res = res if isinstance(out_dtype, (list, tuple)) else res[0]
    return res if job is None else (res, got)


def _row_block(rows, want):
    for tl in range(min(want, rows), 7, -1):
        if rows % tl == 0 and tl % 8 == 0:
            return tl
    return rows


def _rowmap(fn, ins, in_kinds, outs, *, name, tl):
    rows = next(x.shape[0] for x, kd in zip(ins, in_kinds) if kd == "r")
    tl = _row_block(rows, tl)
    n_in = len(ins)

    def spec(shape, kind):
        if kind == "r":
            return pl.BlockSpec((tl,) + tuple(shape[1:]), lambda i: (i,) + (0,) * (len(shape) - 1))
        return pl.BlockSpec(tuple(shape), lambda i: (0,) * len(shape))

    def body(*refs):
        in_refs, out_refs = refs[:n_in], refs[n_in:]
        res = fn(*[r[...] for r in in_refs])
        if not isinstance(res, (tuple, list)):
            res = (res,)
        for o_ref, val, (_, dt, kind) in zip(out_refs, res, outs):
            if kind == "r":
                o_ref[...] = val.astype(dt)
            else:
                @pl.when(pl.program_id(0) == 0)
                def _():
                    o_ref[...] = jnp.zeros_like(o_ref)

                o_ref[...] += val.astype(dt)

    has_acc = any(kd == "a" for _, _, kd in outs)
    res = pl.pallas_call(
        body, name=name, grid=(rows // tl,),
        in_specs=[spec(x.shape, kd) for x, kd in zip(ins, in_kinds)],
        out_specs=[spec(s, kd) for s, _, kd in outs],
        out_shape=[jax.ShapeDtypeStruct(s, dt) for s, dt, _ in outs],
        compiler_params=_params("arbitrary" if has_acc else "parallel"),
    )(*ins)
    return res[0] if len(outs) == 1 else res


def _rms_fwd(x, g):
    r = lax.rsqrt(jnp.mean(x * x, axis=-1, keepdims=True) + EPS)
    return x * r * g


def _rms_bwd(dh, x, g):
    r = lax.rsqrt(jnp.mean(x * x, axis=-1, keepdims=True) + EPS)
    xh = x * r
    dxh = dh * g
    dx = r * (dxh - xh * jnp.mean(dxh * xh, axis=-1, keepdims=True))
    return dx, jnp.sum(dh * xh, axis=0, keepdims=True)


def _swiglu_act(a, b):
    return jax.nn.silu(a) * b


def _ffn_up(x, g, wg, wu, *, name, tm=1024, tn=1024, job=None):
    m, d = x.shape
    n = wg.shape[0]
    tm, tn = min(tm, m), min(tn, n)
    assert m % tm == 0 and n % tn == 0, (name, m, n)

    def body(x_ref, g_ref, wg_ref, wu_ref, h_ref, a_ref, b_ref, s_ref):
        @pl.when(pl.program_id(1) == 0)
        def _():
            h_ref[...] = _rms_fwd(x_ref[...], g_ref[...]).astype(h_ref.dtype)

        hv = h_ref[...]
        av = lax.dot_general(hv, wg_ref[...], NT_DIMS, preferred_element_type=F32)
        bv = lax.dot_general(hv, wu_ref[...], NT_DIMS, preferred_element_type=F32)
        a_ref[...] = av.astype(a_ref.dtype)
        b_ref[...] = bv.astype(b_ref.dtype)
        s_ref[...] = _swiglu_act(av, bv).astype(s_ref.dtype)

    rows = pl.BlockSpec((tm, d), lambda i, j: (i, 0))
    wgt = pl.BlockSpec((tn, d), lambda i, j: (j, 0))
    tile = pl.BlockSpec((tm, tn), lambda i, j: (i, j))
    return _carried_call(
        body, name=name, grid=(m // tm, n // tn),
        in_specs=[rows, pl.BlockSpec((1, d), lambda i, j: (0, 0)), wgt, wgt],
        out_specs=[rows, tile, tile, tile],
        out_shape=[jax.ShapeDtypeStruct((m, d), MXU_DTYPE)] + [jax.ShapeDtypeStruct((m, n), MXU_DTYPE)] * 3,
        semantics=("parallel", "arbitrary"), operands=(x, g, wg, wu), job=job)


def _norm_proj(x, g, w, *, name, tm=1024, tn=1024, job=None):
    m, d = x.shape
    n = w.shape[1]
    tm, tn = min(tm, m), min(tn, n)
    assert m % tm == 0 and n % tn == 0, (name, m, n)

    def body(x_ref, g_ref, w_ref, h_ref, o_ref):
        @pl.when(pl.program_id(1) == 0)
        def _():
            h_ref[...] = _rms_fwd(x_ref[...], g_ref[...]).astype(h_ref.dtype)

        o_ref[...] = jnp.dot(h_ref[...], w_ref[...], preferred_element_type=F32)

    rows = pl.BlockSpec((tm, d), lambda i, j: (i, 0))
    return _carried_call(
        body, name=name, grid=(m // tm, n // tn),
        in_specs=[rows, pl.BlockSpec((1, d), lambda i, j: (0, 0)), pl.BlockSpec((d, tn), lambda i, j: (0, j))],
        out_specs=[rows, pl.BlockSpec((tm, tn), lambda i, j: (i, j))],
        out_shape=[jax.ShapeDtypeStruct((m, d), MXU_DTYPE), jax.ShapeDtypeStruct((m, n), F32)],
        semantics=("parallel", "arbitrary"), operands=(x, g, w), job=job)


def _proj_norm_bwd(pairs, x, g, dres, *, name, tm=1024, tk=1024, job=None):
    m, f = pairs[0][0].shape
    d = x.shape[1]
    tm, tk = min(tm, m), min(tk, f)
    assert m % tm == 0 and f % tk == 0, (name, m, f)
    nk, n_pairs = f // tk, len(pairs)
    swapped = [kept == "FD" for _, _, kept in pairs]

    def body(*refs):
        dy_refs, w_refs = refs[:n_pairs], refs[n_pairs:2 * n_pairs]
        x_ref, g_ref, dr_ref, dx_ref, dg_ref, acc_ref = refs[2 * n_pairs:]
        i, kk = pl.program_id(0), pl.program_id(1)

        @pl.when(kk == 0)
        def _():
            acc_ref[...] = jnp.zeros_like(acc_ref)

        for dy_ref, w_ref, rows_are_f in zip(dy_refs, w_refs, swapped):
            dims = (((1,), (0,)), ((), ())) if rows_are_f else NT_DIMS
            acc_ref[...] += lax.dot_general(dy_ref[...].astype(MXU_DTYPE), w_ref[...], dims, preferred_element_type=F32)

        @pl.when(jnp.logical_and(i == 0, kk == 0))
        def _():
            dg_ref[...] = jnp.zeros_like(dg_ref)

        @pl.when(kk == nk - 1)
        def _():
            dx, dg = _rms_bwd(acc_ref[...], x_ref[...], g_ref[...])
            dx_ref[...] = dx + dr_ref[...]
            dg_ref[...] += dg

    act = pl.BlockSpec((tm, tk), lambda i, kk: (i, kk))
    w_specs = [pl.BlockSpec((tk, d), lambda i, kk: (kk, 0)) if s else pl.BlockSpec((d, tk), lambda i, kk: (0, kk)) for s in swapped]
    rows = pl.BlockSpec((tm, d), lambda i, kk: (i, 0))
    one = pl.BlockSpec((1, d), lambda i, kk: (0, 0))
    return _carried_call(
        body, name=name, grid=(m // tm, nk),
        in_specs=[act] * n_pairs + w_specs + [rows, one, rows],
        out_specs=[rows, one],
        out_shape=[jax.ShapeDtypeStruct((m, d), F32), jax.ShapeDtypeStruct((1, d), F32)],
        scratch_shapes=[pltpu.VMEM((tm, d), F32)],
        semantics=("arbitrary", "arbitrary"), operands=(*[p[0] for p in pairs], *[p[1] for p in pairs], x, g, dres), job=job)


def _ffn_dx(da, db, wg, wu, x, g, dres, *, name, job=None):
    return _proj_norm_bwd([(da, wg, "FD"), (db, wu, "FD")], x, g, dres, name=name, job=job)


def _ffn_fwd(x, g, full, which, layer, late):
    tag = f"{which}_{layer}"
    job, keys = late.gather_job(f"{tag}_up") if late else (None, [])
    (h, a, b, s), got = _ffn_up(x, g, full[(f"{which}_w_gate", layer)], full[(f"{which}_w_up", layer)], name=f"{tag}_up", job=job)
    full.update(zip(keys, got))
    x2 = _mm(s, full[(f"{which}_w_down", layer)], name=f"{tag}_down", epilogue=lambda acc, xv: xv + 0.5 * acc, extras=[x])
    return x2, (x, h, a, b, s)


def _ffn_bwd(dx2, saved, g, full, which, layer, grads, late, inline):
    x, h, a, b, s = saved
    tag = f"{which}_{layer}"
    kg, ku, kd = [(f"{which}_w_{n}", layer) for n in ("gate", "up", "down")]
    wg, wu, wd = full[kg], full[ku], full[kd]
    send = (lambda: late.scatter_job(grads)) if (late and inline) else (lambda: (None, []))

    def act_bwd(ds, av, bv):
        _, vjp = jax.vjp(_swiglu_act, av.astype(F32), bv.astype(F32))
        return vjp(0.5 * ds)

    grads[kd] = _mm(s, dx2, ta=True, name=f"{tag}_dwd", out_dtype=WIRE_DTYPE, epilogue=lambda acc: 0.5 * acc, tk=2048)
    job, keys = send()
    (da, db), got = _carried(_mm, dx2, wd, tb=True, name=f"{tag}_dact", epilogue=act_bwd, extras=[a, b],
                             out_dtype=[MXU_DTYPE, MXU_DTYPE], job=job)
    _note(late, keys, got)
    grads[kg] = _mm(da, h, ta=True, name=f"{tag}_dwg", out_dtype=WIRE_DTYPE, tk=2048)
    job, keys = send()
    grads[ku], got = _carried(_mm, db, h, ta=True, name=f"{tag}_dwu", out_dtype=WIRE_DTYPE, tk=2048, job=job)
    _note(late, keys, got)
    job, keys = send()
    (dx, dg), got = _ffn_dx(da, db, wg, wu, x, g, dx2, name=f"{tag}_dx", job=job)
    _note(late, keys, got)
    return dx, dg


def _carried(fn, *args, job, **kwargs):
    return fn(*args, job=job, **kwargs) if job is not None else (fn(*args, **kwargs), [])


def _note(late, keys, got):
    if late:
        late.received.update(zip(keys, got))


def _softplus(z):
    return jnp.maximum(z, 0.0) + jnp.log(1.0 + jnp.exp(-jnp.abs(z)))


def _ones_dot(x, tri):
    if MXU_DTYPE == F32:
        return jnp.dot(x, tri, preferred_element_type=F32)
    hi = x.astype(MXU_DTYPE)
    lo = (x - hi.astype(F32)).astype(MXU_DTYPE)
    return jnp.dot(hi, tri, preferred_element_type=F32) + jnp.dot(lo, tri, preferred_element_type=F32)


NT_DIMS = (((1,), (1,)), ((), ()))
TN_DIMS = (((0,), (0,)), ((), ()))


SB_LANES = SB_PACK * SB_DH
Q_COL, K_COL, V_COL = (S5_WIDTH * n // SB_LANES for n in (1, 2, 3))


def _head_lanes(rows, hd):
    return lax.broadcasted_iota(jnp.int32, (rows, SB_LANES), 1) // SB_DH == hd


def _attend(proj, *, tq=SB_QUERIES, job=None):
    seq = proj.shape[0]
    tq = min(tq, seq)
    tk = min(SB_KEYS, tq)
    per, hp = tq // tk, SB_PACK

    def body(q_ref, k_ref, v_ref, o_ref, ls_ref):
        i = pl.program_id(1)
        r_idx = lax.broadcasted_iota(jnp.int32, (tk, tk), 0)
        c_idx = lax.broadcasted_iota(jnp.int32, (tk, tk), 1)
        after = (r_idx > c_idx).astype(MXU_DTYPE)
        lanes = [_head_lanes(tk, hd) for hd in range(hp)]

        def block(j, cs, acc, straddles):
            off = pl.multiple_of(j * tk, tk)
            k2, v2 = k_ref[pl.ds(off, tk), :], v_ref[pl.ds(off, tk), :]
            top = 0 if straddles is None else straddles * tk
            rows = tq - top
            q2 = (q_ref[pl.ds(top, rows), :] * SB_SCALE).astype(MXU_DTYPE)
            new_cs, out = [], acc[top:]
            for hd in range(hp):
                kv = jnp.where(lanes[hd], k2, 0.0).astype(MXU_DTYPE)
                vv = jnp.where(lanes[hd], v2, 0.0).astype(MXU_DTYPE)
                z = lax.dot_general(q2, kv, NT_DIMS, preferred_element_type=F32)
                sp = _softplus(z)
                c_in = cs[hd][top:]
                if straddles is None:
                    lk = -sp
                    w = jnp.exp(z - sp + _ones_dot(lk, after) + c_in)
                else:
                    before = lax.broadcasted_iota(jnp.int32, (rows, tk), 1) < lax.broadcasted_iota(jnp.int32, (rows, tk), 0)
                    lk = jnp.where(before, -sp, 0.0)
                    w = jnp.where(before, jnp.exp(z - sp + _ones_dot(lk, after) + c_in), 0.0)
                out = out + jnp.dot(w.astype(MXU_DTYPE), vv, preferred_element_type=F32)
                c_new = c_in + jnp.sum(lk, axis=1, keepdims=True)
                new_cs.append(jnp.concatenate([cs[hd][:top], c_new], axis=0) if top else c_new)
            return tuple(new_cs), (jnp.concatenate([acc[:top], out], axis=0) if top else out)

        carry = (tuple(jnp.zeros((tq, 1), F32) for _ in range(hp)), jnp.zeros((tq, SB_LANES), F32))
        for s in reversed(range(per)):
            carry = block(i * per + s, *carry, s)
        cs, acc = lax.fori_loop(0, i * per, lambda n, cr: block(i * per - 1 - n, *cr, None), carry)
        o_ref[...] = acc
        for hd in range(hp):
            ls_ref[hd] = cs[hd]

    whole = lambda col: pl.BlockSpec((seq, SB_LANES), lambda g, i: (0, col + g))
    return _carried_call(
        body, name="sb_fwd", grid=(SB_HEADS // hp, seq // tq),
        in_specs=[pl.BlockSpec((tq, SB_LANES), lambda g, i: (i, Q_COL + g)), whole(K_COL), whole(V_COL)],
        out_specs=[pl.BlockSpec((tq, SB_LANES), lambda g, i: (i, g)), pl.BlockSpec((hp, tq, 1), lambda g, i: (g, i, 0))],
        out_shape=[jax.ShapeDtypeStruct((seq, SB_HEADS * SB_DH), F32), jax.ShapeDtypeStruct((SB_HEADS, seq, 1), F32)],
        semantics=("parallel", "parallel"), operands=(proj, proj, proj), job=job)


def _attend_bwd(proj, lsum, dmix, *, tq=SB_QUERIES, job=None):
    seq = proj.shape[0]
    tq = min(tq, seq)
    tk = min(SB_KEYS, tq)
    per, hp = tq // tk, SB_PACK
    do_col = S5_WIDTH // SB_LANES

    def body(q_ref, k_ref, v_ref, ls_ref, do_ref, dq_ref, dk_ref, dv_ref):
        i = pl.program_id(1)

        @pl.when(i == 0)
        def _():
            dk_ref[...] = jnp.zeros_like(dk_ref)
            dv_ref[...] = jnp.zeros_like(dv_ref)

        r_idx = lax.broadcasted_iota(jnp.int32, (tk, tk), 0)
        c_idx = lax.broadcasted_iota(jnp.int32, (tk, tk), 1)
        upto = (r_idx <= c_idx).astype(MXU_DTYPE)
        before = (r_idx < c_idx).astype(MXU_DTYPE)
        lanes = [_head_lanes(tk, hd) for hd in range(hp)]

        def block(j, sums, dq, straddles):
            off = pl.multiple_of(j * tk, tk)
            k2, v2 = k_ref[pl.ds(off, tk), :], v_ref[pl.ds(off, tk), :]
            top = 0 if straddles is None else straddles * tk
            rows = tq - top
            part = pl.ds(top, rows)
            q2 = (q_ref[part, :] * SB_SCALE).astype(MXU_DTYPE)
            do2 = do_ref[part, :].astype(MXU_DTYPE)
            valid = None
            if straddles is not None:
                valid = lax.broadcasted_iota(jnp.int32, (rows, tk), 1) < lax.broadcasted_iota(jnp.int32, (rows, tk), 0)
            new_sums, out, dk, dv = [], dq[top:], jnp.zeros((tk, SB_LANES), F32), jnp.zeros((tk, SB_LANES), F32)
            for hd in range(hp):
                cp, ce = sums[hd]
                kv = jnp.where(lanes[hd], k2, 0.0).astype(MXU_DTYPE)
                vv = jnp.where(lanes[hd], v2, 0.0).astype(MXU_DTYPE)
                z = lax.dot_general(q2, kv, NT_DIMS, preferred_element_type=F32)
                sp = _softplus(z)
                lk = -sp if valid is None else jnp.where(valid, -sp, 0.0)
                w = jnp.exp(z - sp + (ls_ref[hd, part, :] - cp[top:]) - _ones_dot(lk, upto))
                if valid is not None:
                    w = jnp.where(valid, w, 0.0)
                e = w * lax.dot_general(do2, vv, NT_DIMS, preferred_element_type=F32)
                earlier = _ones_dot(e, before) + ce[top:]
                keep = jnp.exp(-sp)
                dz = e * keep - (1.0 - keep) * earlier
                if valid is not None:
                    dz = jnp.where(valid, dz, 0.0)
                dzm = dz.astype(MXU_DTYPE)
                out = out + jnp.dot(dzm, kv, preferred_element_type=F32)
                dk = dk + jnp.where(lanes[hd], lax.dot_general(dzm, q2, TN_DIMS, preferred_element_type=F32), 0.0)
                dv = dv + jnp.where(lanes[hd], lax.dot_general(w.astype(MXU_DTYPE), do2, TN_DIMS, preferred_element_type=F32), 0.0)
                new = (cp[top:] + jnp.sum(lk, axis=1, keepdims=True), ce[top:] + jnp.sum(e, axis=1, keepdims=True))
                new_sums.append(tuple(jnp.concatenate([old[:top], val], axis=0) for old, val in zip((cp, ce), new)) if top else new)
            dk_ref[pl.ds(off, tk), :] += dk
            dv_ref[pl.ds(off, tk), :] += dv
            return tuple(new_sums), (jnp.concatenate([dq[:top], out], axis=0) if top else out)

        zero = jnp.zeros((tq, 1), F32)
        carry = (tuple((zero, zero) for _ in range(hp)), jnp.zeros((tq, SB_LANES), F32))
        carry = lax.fori_loop(0, i * per, lambda j, cr: block(j, *cr, None), carry)
        for s in range(per):
            carry = block(i * per + s, *carry, s)
        dq_ref[...] = carry[1] * SB_SCALE

    whole = lambda col: pl.BlockSpec((seq, SB_LANES), lambda g, i: (0, col + g))
    tile = lambda col: pl.BlockSpec((tq, SB_LANES), lambda g, i: (i, col + g))
    acc = pl.BlockSpec((seq, SB_LANES), lambda g, i: (0, g))
    return _carried_call(
        body, name="sb_bwd", grid=(SB_HEADS // hp, seq // tq),
        in_specs=[tile(Q_COL), whole(K_COL), whole(V_COL), pl.BlockSpec((hp, tq, 1), lambda g, i: (g, i, 0)), tile(do_col)],
        out_specs=[tile(0), acc, acc],
        out_shape=[jax.ShapeDtypeStruct((seq, SB_HEADS * SB_DH), F32)] * 3,
        semantics=("parallel", "arbitrary"), operands=(proj, proj, proj, lsum, dmix), job=job)


def _s5_disc(lr, li, ldt, br, bi):
    dt = jnp.exp(ldt)
    mag = jnp.exp(lr * dt)
    ar = mag * jnp.cos(li * dt)
    ai = mag * jnp.sin(li * dt)
    den = lr * lr + li * li
    nr = ar - 1.0
    cr = (nr * lr + ai * li) / den
    ci = (ai * lr - nr * li) / den
    return ar, ai, cr[None] * br - ci[None] * bi, cr[None] * bi + ci[None] * br


def _s5_prep(lr, li, ldt, br, bi):
    shapes = [lr.shape, lr.shape, br.shape, br.shape]

    def body(lr_ref, li_ref, ldt_ref, br_ref, bi_ref, *outs):
        for o, val in zip(outs, _s5_disc(lr_ref[...], li_ref[...], ldt_ref[...], br_ref[...], bi_ref[...])):
            o[...] = val

    return pl.pallas_call(body, name="s5_prep", out_shape=[jax.ShapeDtypeStruct(s, F32) for s in shapes])(lr, li, ldt, br, bi)


def _s5_prep_bwd(lr, li, ldt, br, bi, cts):
    args = (lr, li, ldt, br, bi)

    def body(*refs):
        ins, ct_refs, outs = refs[:5], refs[5:9], refs[9:]
        _, vjp = jax.vjp(_s5_disc, *[r[...] for r in ins])
        for o, val in zip(outs, vjp(tuple(r[...] for r in ct_refs))):
            o[...] = val

    return pl.pallas_call(body, name="s5_prep_bwd", out_shape=[jax.ShapeDtypeStruct(a.shape, F32) for a in args])(*args, *cts)


SCAN_ROWS = 8


def _powers(ar, ai):
    out = [(ar, ai)]
    for _ in range(SCAN_ROWS - 1):
        pr, pi = out[-1]
        out.append((pr * ar - pi * ai, pr * ai + pi * ar))
    return out


def _s5_states(u, bmat, cmat, a, d, *, tc=512):
    seq, width = u.shape
    nj, cols, w2 = bmat.shape
    tw = w2 // 2
    tc = min(tc, seq)
    assert seq % tc == 0 and nj * cols == width and tw == S5_BLOCK

    def body(u_ref, bm_ref, cm_ref, a_ref, d_ref, h_ref, y_ref, cr_ref, ci_ref):
        @pl.when(pl.program_id(1) == 0)
        def _():
            cr_ref[...] = jnp.zeros_like(cr_ref)
            ci_ref[...] = jnp.zeros_like(ci_ref)

        uv = u_ref[...]
        h_ref[...] = jnp.dot(uv.astype(MXU_DTYPE), bm_ref[0], preferred_element_type=F32)
        re, im = pl.ds(0, tw), pl.ds(tw, tw)
        powers = _powers(a_ref[:, re], a_ref[:, im])
        pr = jnp.concatenate([p[0] for p in powers], axis=0)
        pi = jnp.concatenate([p[1] for p in powers], axis=0)
        row_id = lax.broadcasted_iota(jnp.int32, (SCAN_ROWS, tw), 0)
        reach = {dist: tuple(jnp.where(row_id >= dist, part, 0.0) for part in powers[dist - 1]) for dist in (1, 2, 4)}

        def block(n, carry):
            hr, hi = carry
            rows = pl.ds(pl.multiple_of(n * SCAN_ROWS, SCAN_ROWS), SCAN_ROWS)
            yr, yi = h_ref[rows, re], h_ref[rows, im]
            for dist in (1, 2, 4):
                cr, ci = reach[dist]
                sr, si = pltpu.roll(yr, dist, 0), pltpu.roll(yi, dist, 0)
                yr, yi = yr + cr * sr - ci * si, yi + cr * si + ci * sr
            yr, yi = yr + pr * hr - pi * hi, yi + pr * hi + pi * hr
            h_ref[rows, re] = yr
            h_ref[rows, im] = yi
            return yr[SCAN_ROWS - 1:], yi[SCAN_ROWS - 1:]

        hr, hi = lax.fori_loop(0, tc // SCAN_ROWS, block, (cr_ref[...], ci_ref[...]), unroll=4)
        cr_ref[...] = hr
        ci_ref[...] = hi
        y_ref[...] = jnp.dot(h_ref[...].astype(MXU_DTYPE), cm_ref[0], preferred_element_type=F32) + d_ref[...] * uv

    io = pl.BlockSpec((tc, cols), lambda j, t: (t, j))
    return pl.pallas_call(
        body, name="s5_states", grid=(nj, seq // tc),
        in_specs=[io, pl.BlockSpec((1, cols, w2), lambda j, t: (j, 0, 0)), pl.BlockSpec((1, w2, cols), lambda j, t: (j, 0, 0)),
                  pl.BlockSpec((1, w2), lambda j, t: (0, j)), pl.BlockSpec((1, cols), lambda j, t: (0, j))],
        out_specs=[pl.BlockSpec((tc, w2), lambda j, t: (t, j)), io],
        out_shape=[jax.ShapeDtypeStruct((seq, nj * w2), F32), jax.ShapeDtypeStruct((seq, width), F32)],
        scratch_shapes=[pltpu.VMEM((1, tw), F32)] * 2,
        compiler_params=_params("parallel", "arbitrary"),
    )(u, bmat, cmat, a, d)


def _s5_states_bwd(dy, h, u, bmat, cmat, a, du_skip, *, tc=512):
    seq, width = u.shape
    nj, cols, w2 = bmat.shape
    tw = w2 // 2
    tc = min(tc, seq)
    assert seq % tc == 0
    nt = seq // tc

    def body(dy_ref, h_ref, u_ref, bm_ref, cm_ref, a_ref, sk_ref, du_ref, da_ref, db_ref, dc_ref, g_ref, cr_ref, ci_ref):
        @pl.when(pl.program_id(1) == 0)
        def _():
            cr_ref[...] = jnp.zeros_like(cr_ref)
            ci_ref[...] = jnp.zeros_like(ci_ref)
            da_ref[...] = jnp.zeros_like(da_ref)
            db_ref[...] = jnp.zeros_like(db_ref)
            dc_ref[...] = jnp.zeros_like(dc_ref)

        dyv = dy_ref[...].astype(MXU_DTYPE)
        g_ref[...] = lax.dot_general(dyv, cm_ref[0], NT_DIMS, preferred_element_type=F32)
        re, im = pl.ds(0, tw), pl.ds(tw, tw)
        powers = _powers(a_ref[:, re], a_ref[:, im])
        pr = jnp.concatenate([p[0] for p in reversed(powers)], axis=0)
        pi = jnp.concatenate([p[1] for p in reversed(powers)], axis=0)
        row_id = lax.broadcasted_iota(jnp.int32, (SCAN_ROWS, tw), 0)
        last = SCAN_ROWS - 1
        reach = {dist: tuple(jnp.where(row_id < SCAN_ROWS - dist, part, 0.0) for part in powers[dist - 1]) for dist in (1, 2, 4)}

        def block(n, carry):
            gr, gi, sr, si = carry
            rows = pl.ds(pl.multiple_of((tc // SCAN_ROWS - 1 - n) * SCAN_ROWS, SCAN_ROWS), SCAN_ROWS)
            yr, yi = g_ref[rows, re], g_ref[rows, im]
            for dist in (1, 2, 4):
                cr, ci = reach[dist]
                ur, ui = pltpu.roll(yr, SCAN_ROWS - dist, 0), pltpu.roll(yi, SCAN_ROWS - dist, 0)
                yr, yi = yr + cr * ur + ci * ui, yi + cr * ui - ci * ur
            yr, yi = yr + pr * gr + pi * gi, yi + pr * gi - pi * gr
            g_ref[rows, re] = yr
            g_ref[rows, im] = yi
            nr = jnp.where(row_id < last, pltpu.roll(yr, last, 0), gr)
            ni = jnp.where(row_id < last, pltpu.roll(yi, last, 0), gi)
            hr, hi = h_ref[rows, re], h_ref[rows, im]
            return yr[:1], yi[:1], sr + nr * hr + ni * hi, si + ni * hr - nr * hi

        zero = jnp.zeros((SCAN_ROWS, tw), F32)
        gr, gi, sr, si = lax.fori_loop(0, tc // SCAN_ROWS, block, (cr_ref[...], ci_ref[...], zero, zero), unroll=4)
        cr_ref[...] = gr
        ci_ref[...] = gi
        da_ref[:, re] += jnp.sum(sr, axis=0, keepdims=True)
        da_ref[:, im] += jnp.sum(si, axis=0, keepdims=True)
        gv = g_ref[...].astype(MXU_DTYPE)
        du_ref[...] = (lax.dot_general(gv, bm_ref[0], NT_DIMS, preferred_element_type=F32) + sk_ref[...]).astype(du_ref.dtype)
        db_ref[0] += lax.dot_general(u_ref[...].astype(MXU_DTYPE), gv, TN_DIMS, preferred_element_type=F32)
        dc_ref[0] += lax.dot_general(h_ref[...].astype(MXU_DTYPE), dyv, TN_DIMS, preferred_element_type=F32)

    io = pl.BlockSpec((tc, cols), lambda j, t: (nt - 1 - t, j))
    bm = pl.BlockSpec((1, cols, w2), lambda j, t: (j, 0, 0))
    cm = pl.BlockSpec((1, w2, cols), lambda j, t: (j, 0, 0))
    row = pl.BlockSpec((1, w2), lambda j, t: (0, j))
    return pl.pallas_call(
        body, name="s5_states_bwd", grid=(nj, nt),
        in_specs=[io, pl.BlockSpec((tc, w2), lambda j, t: (nt - 1 - t, j)), io, bm, cm, row, io],
        out_specs=[io, row, bm, cm],
        out_shape=[jax.ShapeDtypeStruct((seq, width), MXU_DTYPE), jax.ShapeDtypeStruct((1, nj * w2), F32),
                   jax.ShapeDtypeStruct(bmat.shape, F32), jax.ShapeDtypeStruct(cmat.shape, F32)],
        scratch_shapes=[pltpu.VMEM((tc, w2), F32)] + [pltpu.VMEM((1, tw), F32)] * 2,
        compiler_params=_params("parallel", "arbitrary"),
    )(dy, h, u, bmat, cmat, a, du_skip)


def _pair_columns(re, im, axis):
    shape = re.shape
    split = shape[:axis] + (shape[axis] // S5_BLOCK, S5_BLOCK) + shape[axis + 1:]
    both = jnp.stack([re.reshape(split), im.reshape(split)], axis=axis + 1)
    return both.reshape(shape[:axis] + (2 * shape[axis],) + shape[axis + 1:])


def _unpair_columns(t, axis):
    shape = t.shape
    both = t.reshape(shape[:axis] + (shape[axis] // (2 * S5_BLOCK), 2, S5_BLOCK) + shape[axis + 1:])
    half = shape[:axis] + (shape[axis] // 2,) + shape[axis + 1:]
    return (lax.index_in_dim(both, 0, axis + 1, keepdims=False).reshape(half),
            lax.index_in_dim(both, 1, axis + 1, keepdims=False).reshape(half))


S5_PER_BLOCK = S5_GROUPS // S5_DIAG


def _block_diag(t):
    g, a, b = t.shape
    n = S5_PER_BLOCK
    eye = jnp.eye(n, dtype=t.dtype)
    return (t.reshape(g // n, n, a, 1, b) * eye[None, :, None, :, None]).reshape(g // n, n * a, n * b)


def _block_diag_part(m):
    j, n = m.shape[0], S5_PER_BLOCK
    a, b = m.shape[1] // n, m.shape[2] // n
    return jnp.moveaxis(jnp.diagonal(m.reshape(j, n, a, n, b), axis1=1, axis2=3), -1, 1).reshape(j * n, a, b)


def _gelu_glu(y, gate_pre):
    z = jax.nn.gelu(y)
    return z * jax.nn.sigmoid(gate_pre)


def _s5_fwd(u, p, w_glu):
    lr, li = p["s5_lambda_re"][0], p["s5_lambda_im"][0]
    ldt = p["s5_log_dt"][0][:, None]
    br = p["s5_b_re"][0].transpose(2, 0, 1)
    bi = p["s5_b_im"][0].transpose(2, 0, 1)
    ar, ai, bbr, bbi = _s5_prep(lr, li, ldt, br, bi)
    a = _pair_columns(ar.reshape(1, S5_LANES), ai.reshape(1, S5_LANES), 1)
    bmat = jnp.concatenate([_block_diag(bbr.transpose(1, 0, 2)), _block_diag(bbi.transpose(1, 0, 2))], axis=2)
    cmat = jnp.concatenate([_block_diag(p["s5_c_re"][0].transpose(0, 2, 1)),
                            -_block_diag(p["s5_c_im"][0].transpose(0, 2, 1))], axis=1)
    bmat, cmat = bmat.astype(MXU_DTYPE), cmat.astype(MXU_DTYPE)
    d = p["s5_d"]
    h, y = _s5_states(u, bmat, cmat, a, d)
    z = _rowmap(jax.nn.gelu, [y], "r", [(y.shape, MXU_DTYPE, "r")], name="s5_gelu", tl=512)
    gate_pre = _mm(z, w_glu, name="s5_glu")
    out = _rowmap(_gelu_glu, [y, gate_pre], "rr", [(y.shape, F32, "r")], name="s5_gate", tl=512)
    return out, (u, lr, li, ldt, br, bi, a, bmat, cmat, h, y, z, gate_pre)


def _s5_bwd(dout, saved, p, w_glu):
    u, lr, li, ldt, br, bi, a, bmat, cmat, h, y, z, gate_pre = saved
    d = p["s5_d"]

    def gate_bwd(dov, yv, gv):
        zv = jax.nn.gelu(yv)
        sg = jax.nn.sigmoid(gv)
        return dov * sg, dov * zv * sg * (1.0 - sg)

    dz_direct, dgate = _rowmap(gate_bwd, [dout, y, gate_pre], "rrr", [(y.shape, F32, "r"), (y.shape, MXU_DTYPE, "r")],
                               name="s5_gate_bwd", tl=512)
    dw_glu = _mm(z, dgate, ta=True, name="s5_dwglu", out_dtype=WIRE_DTYPE)
    dz = _mm(dgate, w_glu, tb=True, name="s5_dz", epilogue=lambda acc, prev: acc + prev, extras=[dz_direct])

    def gelu_bwd(dzv, yv, uv, dvv):
        _, vjp = jax.vjp(jax.nn.gelu, yv)
        dy = vjp(dzv)[0]
        return dy, dy * dvv, jnp.sum(dy * uv, axis=0, keepdims=True)

    dy, du_skip, dd = _rowmap(gelu_bwd, [dz, y, u, d], "rrrc",
                              [(y.shape, F32, "r"), (y.shape, F32, "r"), (d.shape, F32, "a")], name="s5_gelu_bwd", tl=512)
    du, da, dbmat, dcmat = _s5_states_bwd(dy, h, u, bmat, cmat, a, du_skip)
    dbbr, dbbi = (_block_diag_part(t).transpose(1, 0, 2) for t in (dbmat[:, :, :S5_BLOCK], dbmat[:, :, S5_BLOCK:]))
    dar, dai = _unpair_columns(da, 1)
    cts = (dar.reshape(S5_GROUPS, S5_STATE), dai.reshape(S5_GROUPS, S5_STATE), dbbr, dbbi)
    dlr, dli, dldt, dbr, dbi = _s5_prep_bwd(lr, li, ldt, br, bi, cts)
    dcr, dci = (_block_diag_part(t).transpose(0, 2, 1) for t in (dcmat[:, :S5_BLOCK], dcmat[:, S5_BLOCK:]))
    grads = {
        "s5_lambda_re": dlr[None], "s5_lambda_im": dli[None], "s5_log_dt": dldt[:, 0][None],
        "s5_b_re": dbr.transpose(1, 2, 0)[None], "s5_b_im": dbi.transpose(1, 2, 0)[None],
        "s5_c_re": dcr[None], "s5_c_im": -dci[None], "s5_d": dd,
    }
    return du, dw_glu, grads


def _mix0_fwd(x, g, p, full, late):
    (h, proj), _ = _norm_proj(x, g, full[("ab_w_in", 0)], name="mix0_in")
    u = proj[:, :S5_WIDTH]
    job, keys = late.gather_job("sb_fwd") if late else (None, [])
    (o, lsum), got = _attend(proj, job=job)
    full.update(zip(keys, got))
    w_glu, w_out = full[("s5_w_glu", 0)], full[("ab_w_out", 0)]
    y_a, s5_saved = _s5_fwd(u, p, w_glu)
    mix = jnp.concatenate([y_a, o], axis=1).astype(MXU_DTYPE)
    x2 = _mm(mix, w_out, name="mix0_out", epilogue=lambda acc, xv: xv + acc, extras=[x])
    return x2, (x, h, proj, lsum, mix, s5_saved)


def _mix0_bwd(dx2, saved, g, p, full, grads, late):
    x, h, proj, lsum, mix, s5_saved = saved
    w_in, w_glu, w_out = full[("ab_w_in", 0)], full[("s5_w_glu", 0)], full[("ab_w_out", 0)]
    dmix = _mm(dx2, w_out, tb=True, name="mix0_dmix")
    grads[("ab_w_out", 0)] = _mm(mix, dx2, ta=True, name="mix0_dwout", out_dtype=WIRE_DTYPE)
    du, grads[("s5_w_glu", 0)], s5_grads = _s5_bwd(dmix[:, :S5_WIDTH], s5_saved, p, w_glu)
    job, keys = late.scatter_job(grads) if late else (None, [])
    (dq, dk, dv), got = _attend_bwd(proj, lsum, dmix, job=job)
    _note(late, keys, got)
    dproj = jnp.concatenate([du] + [t.astype(MXU_DTYPE) for t in (dq, dk, dv)], axis=1)
    grads[("ab_w_in", 0)] = _mm(h, dproj, ta=True, name="mix0_dwin", out_dtype=WIRE_DTYPE)
    job, keys = late.scatter_job(grads) if late else (None, [])
    (dx, dg), got = _proj_norm_bwd([(dproj, w_in, "DF")], x, g, dx2, name="mix0_dh", job=job)
    _note(late, keys, got)
    return dx, dg, s5_grads


def _shift_down(t, n):
    rows = lax.broadcasted_iota(jnp.int32, t.shape, 0)
    return jnp.where(rows >= n, pltpu.roll(t, n, 0), 0.0)


def _shift_up(t, n):
    rows = lax.broadcasted_iota(jnp.int32, t.shape, 0)
    return jnp.where(rows < t.shape[0] - n, pltpu.roll(t, t.shape[0] - n, 0), 0.0)


def _conv_fwd(proj, cw, *, tc=128):
    seq, c3 = proj.shape
    ch = c3 // 3
    nb = ch // tc

    def body(b_ref, c_ref, v_ref, w_ref, m_ref):
        pv = c_ref[...] * v_ref[...]
        w = w_ref[...]
        y = w[2:3] * pv + w[1:2] * _shift_down(pv, 1) + w[0:1] * _shift_down(pv, 2)
        m_ref[...] = (b_ref[...] * y).astype(m_ref.dtype)

    col = lambda part: pl.BlockSpec((seq, tc), lambda j: (0, part * nb + j))
    return pl.pallas_call(
        body, name="conv_fwd", grid=(nb,),
        in_specs=[col(0), col(1), col(2), pl.BlockSpec((3, tc), lambda j: (0, j))],
        out_specs=pl.BlockSpec((seq, tc), lambda j: (0, j)),
        out_shape=jax.ShapeDtypeStruct((seq, ch), MXU_DTYPE),
        compiler_params=_params("parallel"),
    )(proj, proj, proj, cw)


def _conv_bwd(proj, cw, dm, *, tc=128):
    seq, c3 = proj.shape
    ch = c3 // 3
    nb = ch // tc

    def body(b_ref, c_ref, v_ref, w_ref, dm_ref, dproj_ref, dw_ref, dc_ref, dv_ref):
        part = pl.program_id(1)

        @pl.when(part == 0)
        def _():
            cv, vv, dmv = c_ref[...], v_ref[...], dm_ref[...]
            pv = cv * vv
            w = w_ref[...]
            p1, p2 = _shift_down(pv, 1), _shift_down(pv, 2)
            y = w[2:3] * pv + w[1:2] * p1 + w[0:1] * p2
            dproj_ref[...] = (dmv * y).astype(dproj_ref.dtype)
            dy = dmv * b_ref[...]
            dp = w[2:3] * dy + w[1:2] * _shift_up(dy, 1) + w[0:1] * _shift_up(dy, 2)
            dc_ref[...] = (dp * vv).astype(dc_ref.dtype)
            dv_ref[...] = (dp * cv).astype(dv_ref.dtype)
            dw_ref[...] = jnp.concatenate([jnp.sum(dy * p2, axis=0, keepdims=True), jnp.sum(dy * p1, axis=0, keepdims=True),
                                           jnp.sum(dy * pv, axis=0, keepdims=True)], axis=0)

        @pl.when(part == 1)
        def _():
            dproj_ref[...] = dc_ref[...]

        @pl.when(part == 2)
        def _():
            dproj_ref[...] = dv_ref[...]

    col = lambda part: pl.BlockSpec((seq, tc), lambda j, t: (0, part * nb + j))
    small = pl.BlockSpec((3, tc), lambda j, t: (0, j))
    return pl.pallas_call(
        body, name="conv_bwd", grid=(nb, 3),
        in_specs=[col(0), col(1), col(2), small, pl.BlockSpec((seq, tc), lambda j, t: (0, j))],
        out_specs=[pl.BlockSpec((seq, tc), lambda j, t: (0, t * nb + j)), small],
        out_shape=[jax.ShapeDtypeStruct((seq, c3), MXU_DTYPE), jax.ShapeDtypeStruct((3, ch), F32)],
        scratch_shapes=[pltpu.VMEM((seq, tc), MXU_DTYPE)] * 2,
        compiler_params=_params("parallel", "arbitrary"),
    )(proj, proj, proj, cw, dm)


def _mix1_fwd(x, g, full, late):
    job, keys = late.gather_job("mix1_in") if late else (None, [])
    (h, proj), got = _norm_proj(x, g, full[("sc_w_in", 0)], name="mix1_in", job=job)
    full.update(zip(keys, got))
    m = _conv_fwd(proj, full[("sc_conv_w", 0)])
    x2 = _mm(m, full[("sc_w_out", 0)], name="mix1_out", epilogue=lambda acc, xv: xv + acc, extras=[x])
    return x2, (x, h, proj, m)


def _mix1_bwd(dx2, saved, g, w_in, cw, w_out):
    x, h, proj, m = saved
    dm = _mm(dx2, w_out, tb=True, name="mix1_dm")
    dw_out = _mm(m, dx2, ta=True, name="mix1_dwout", out_dtype=WIRE_DTYPE)
    dproj, dcw = _conv_bwd(proj, cw, dm)
    dw_in = _mm(h, dproj, ta=True, name="mix1_dwin", out_dtype=WIRE_DTYPE)
    (dx, dg), _ = _proj_norm_bwd([(dproj, w_in, "DF")], x, g, dx2, name="mix1_dh")
    return dx, dg, dw_in, dcw, dw_out


def _loss_head(x, g, target):
    feat = x.shape[1]

    def fn(xv, gv, tv):
        err = _rms_fwd(xv, gv) - tv
        dx, dg = _rms_bwd(err / feat, xv, gv)
        return jnp.sum(err * err, keepdims=True) * (0.5 / feat), dx, dg

    return _rowmap(fn, [x, g, target], "rcr", [((1, 1), F32, "a"), (x.shape, F32, "r"), (g.shape, F32, "a")],
                   name="loss_head", tl=256)


def _slot(ref, place, chip=None, half=None):
    axis, width = place
    shape = list(ref.shape)
    start = [0, 0]
    if chip is not None:
        start[axis], shape[axis] = chip * width, width
    if half is not None:
        h_axis = 0 if shape[0] % 32 == 0 else 1
        shape[h_axis] //= 2
        start[h_axis] = start[h_axis] + half * shape[h_axis]
    hint = lambda s, d: s if isinstance(s, int) else pl.multiple_of(s, 128 if d == 1 else 8)
    return ref.at[tuple(pl.ds(hint(s, d), n) for d, (s, n) in enumerate(zip(start, shape)))]


class _Exchange:
    def __init__(self, kind, arrays, places):
        self.kind, self.arrays, self.places, self.n = kind, list(arrays), list(places), len(arrays)
        self.out_shape = []
        for t, (axis, width) in zip(self.arrays, self.places):
            if kind == "gather":
                shape = list(t.shape)
                shape[axis] = N_CHIPS * width
            else:
                shape = [N_CHIPS] + list(t.shape)
                shape[1 + axis] = width
            self.out_shape.append(jax.ShapeDtypeStruct(tuple(shape), t.dtype))
        n = self.n
        self.scratch = [pltpu.SemaphoreType.DMA((3 * n,)) for _ in range(4 if kind == "gather" else 2)]
        self.scratch.append(pltpu.SemaphoreType.DMA((n,)))

    def _copies(self, ins, outs, sems):
        x, y, c = lax.axis_index("x"), lax.axis_index("y"), lax.axis_index("c")
        peers = [(1 - x, y), (x, 1 - y), (1 - x, 1 - y)]
        remote = lambda src, dst, send, recv, k, to: pltpu.make_async_remote_copy(
            src_ref=src, dst_ref=dst, send_sem=send.at[k], recv_sem=recv.at[k], device_id=to, device_id_type=MESH_ID)
        local, ici, d2d = [], [], []
        for a in range(self.n):
            place = self.places[a]
            if self.kind == "gather":
                local.append(pltpu.make_async_copy(ins[a], _slot(outs[a], place, 2 * x + y), sems[4].at[a]))
                for r, (px, py) in enumerate(peers):
                    ici.append(remote(_slot(ins[a], place, None, c), _slot(outs[a], place, 2 * x + y, c),
                                      sems[0], sems[1], 3 * a + r, (px, py, c)))
                    landed = _slot(outs[a], place, 2 * px + py, c)
                    d2d.append(remote(landed, landed, sems[2], sems[3], 3 * a + r, (x, y, 1 - c)))
            else:
                local.append(pltpu.make_async_copy(_slot(ins[a], place, 2 * x + y), outs[a].at[3], sems[2].at[a]))
                for r, (px, py) in enumerate(peers):
                    ici.append(remote(_slot(ins[a], place, 2 * px + py), outs[a].at[r], sems[0], sems[1], 3 * a + r, (px, py, c)))
        return local, ici, d2d

    def start(self, ins, outs, sems):
        local, ici, _ = self._copies(ins, outs, sems)
        for cp in local + ici:
            cp.start()

    def relay(self, ins, outs, sems):
        _, ici, d2d = self._copies(ins, outs, sems)
        for arrived, onward in zip(ici, d2d):
            arrived.wait_recv()
            onward.start()

    def finish(self, ins, outs, sems):
        local, ici, d2d = self._copies(ins, outs, sems)
        for cp in local + d2d:
            cp.wait()
        for cp in ici:
            cp.wait_send() if d2d else cp.wait()


def _exchange_call(job, name):
    n = job.n

    def body(*refs):
        ins, outs, sems = refs[:n], refs[n:2 * n], refs[2 * n:]
        job.start(ins, outs, sems)
        job.relay(ins, outs, sems)
        job.finish(ins, outs, sems)

    return pl.pallas_call(
        body, name=name, in_specs=[ANY_SPEC] * n, out_specs=[ANY_SPEC] * n, out_shape=job.out_shape,
        scratch_shapes=job.scratch, compiler_params=pltpu.CompilerParams(has_side_effects=True),
    )(*job.arrays)


def _carried_call(body, *, name, grid, in_specs, out_specs, out_shape, semantics, operands, scratch_shapes=(), job=None):
    scratch_shapes = list(scratch_shapes)
    if job is None:
        return pl.pallas_call(body, name=name, grid=grid, in_specs=in_specs, out_specs=out_specs, out_shape=out_shape,
                              scratch_shapes=scratch_shapes, compiler_params=_params(*semantics))(*operands), []
    n_in, n_out, n, n_scr = len(in_specs), len(out_specs), job.n, len(scratch_shapes)
    steps = math.prod(grid)

    def wrapped(*refs):
        ins, job_ins = refs[:n_in], refs[n_in:n_in + n]
        outs, job_outs = refs[n_in + n:n_in + n + n_out], refs[n_in + n + n_out:n_in + 2 * n + n_out]
        outs = outs + refs[n_in + 2 * n + n_out:n_in + 2 * n + n_out + n_scr]
        sems = refs[n_in + 2 * n + n_out + n_scr:]
        step = functools.reduce(lambda acc, d: acc * grid[d] + pl.program_id(d), range(len(grid)), 0)

        @pl.when(step == 0)
        def _():
            job.start(job_ins, job_outs, sems)

        @pl.when(step == (3 * steps) // 4)
        def _():
            job.relay(job_ins, job_outs, sems)

        body(*ins, *outs)

        @pl.when(step == steps - 1)
        def _():
            job.finish(job_ins, job_outs, sems)

    res = pl.pallas_call(
        wrapped, name=name, grid=grid, in_specs=list(in_specs) + [ANY_SPEC] * n, out_specs=list(out_specs) + [ANY_SPEC] * n,
        out_shape=list(out_shape) + job.out_shape, scratch_shapes=scratch_shapes + job.scratch,
        compiler_params=pltpu.CompilerParams(dimension_semantics=("arbitrary",) * len(grid), vmem_limit_bytes=VMEM_LIMIT,
                                             has_side_effects=True),
    )(*operands, *job.arrays)
    return res[:n_out], res[n_out:]


def _swap_with_sibling(parts):
    n = len(parts)

    def body(*refs):
        ins, outs = refs[:n], refs[n:2 * n]
        send, recv = refs[2 * n:]
        sibling = (lax.axis_index("x"), lax.axis_index("y"), 1 - lax.axis_index("c"))
        copies = [pltpu.make_async_remote_copy(src_ref=ins[a], dst_ref=outs[a], send_sem=send.at[a], recv_sem=recv.at[a],
                                               device_id=sibling, device_id_type=MESH_ID) for a in range(n)]
        for cp in copies:
            cp.start()
        for cp in copies:
            cp.wait()

    return pl.pallas_call(
        body, name="swap_with_sibling",
        in_specs=[ANY_SPEC] * n, out_specs=[ANY_SPEC] * n,
        out_shape=[jax.ShapeDtypeStruct(p.shape, p.dtype) for p in parts],
        scratch_shapes=[pltpu.SemaphoreType.DMA((n,)), pltpu.SemaphoreType.DMA((n,))],
        compiler_params=pltpu.CompilerParams(has_side_effects=True),
    )(*parts)


def _sum_all_devices(t):
    rows = t.shape[0]

    def body(t_ref, o_ref, slots, send, recv):
        x, y, c = lax.axis_index("x"), lax.axis_index("y"), lax.axis_index("c")
        me = 4 * x + 2 * y + c
        slots[me] = t_ref[...]
        copies = []
        for m in range(1, 8):
            peer = (x ^ (m >> 2), y ^ ((m >> 1) & 1), c ^ (m & 1))
            cp = pltpu.make_async_remote_copy(src_ref=t_ref, dst_ref=slots.at[me], send_sem=send.at[m - 1],
                                              recv_sem=recv.at[m - 1], device_id=peer, device_id_type=MESH_ID)
            cp.start()
            copies.append(cp)
        for cp in copies:
            cp.wait()
        acc = slots[0]
        for dev in range(1, 8):
            acc = acc + slots[dev]
        o_ref[...] = acc

    vmem = pl.BlockSpec(memory_space=pltpu.VMEM)
    return pl.pallas_call(
        body, name="sum_all_devices", in_specs=[vmem], out_specs=vmem,
        out_shape=jax.ShapeDtypeStruct(t.shape, F32),
        scratch_shapes=[pltpu.VMEM((8, rows, 128), F32), pltpu.SemaphoreType.DMA((7,)), pltpu.SemaphoreType.DMA((7,))],
        compiler_params=pltpu.CompilerParams(vmem_limit_bytes=VMEM_LIMIT, has_side_effects=True),
    )(t)


def _adamw(w, g, m, v):
    m = ADAM_B1 * m + (1.0 - ADAM_B1) * g
    v = ADAM_B2 * v + (1.0 - ADAM_B2) * jnp.square(g)
    m_hat = m / (1.0 - ADAM_B1 ** ADAM_STEP)
    v_hat = v / (1.0 - ADAM_B2 ** ADAM_STEP)
    return -ADAM_LR * (m_hat / (jnp.sqrt(v_hat) + ADAM_EPS) + ADAM_WD * w), m, v


def _chip_sum(received, name):
    rows, cols = received.shape[1:]
    tl = _row_block(rows, 512)

    def body(r_ref, o_ref):
        o_ref[...] = ((r_ref[0].astype(F32) + r_ref[1].astype(F32)) + r_ref[2].astype(F32)) + r_ref[3].astype(F32)

    return pl.pallas_call(body, name=name, grid=(rows // tl,),
                          in_specs=[pl.BlockSpec((N_CHIPS, tl, cols), lambda i: (0, i, 0))],
                          out_specs=pl.BlockSpec((tl, cols), lambda i: (i, 0)),
                          out_shape=jax.ShapeDtypeStruct((rows, cols), F32), compiler_params=_params("parallel"))(received)


def _adamw_layer(w, m, v, p_mine, p_other, layer, prev, name):
    _, rows, cols = w.shape
    tl = _row_block(rows, 512)

    def body(w_ref, m_ref, v_ref, pa_ref, pb_ref, *rest):
        g = pa_ref[:, :cols] + pb_ref[:, :cols]
        for o_ref, val in zip(rest[-4:], (g,) + _adamw(w_ref[...], g, m_ref[...], v_ref[...])):
            o_ref[...] = val

    stacked = pl.BlockSpec((None, tl, cols), lambda i: (layer, i, 0))
    part = pl.BlockSpec((tl, p_mine.shape[1]), lambda i: (i, 0))
    kept = list(prev) if prev else []
    return pl.pallas_call(
        body, name=name, grid=(rows // tl,),
        in_specs=[stacked] * 3 + [part] * 2 + [ANY_SPEC] * len(kept),
        out_specs=[stacked] * 4, out_shape=[jax.ShapeDtypeStruct(w.shape, F32)] * 4,
        input_output_aliases={5 + k: k for k in range(len(kept))},
        compiler_params=_params("parallel"),
    )(w, m, v, p_mine, p_other, *kept)


def _adamw_small(w, g, m, v):
    def fn(wv, gv, mv, vv):
        return _adamw(wv, gv, mv, vv)

    return _rowmap(fn, [w, g, m, v], "rrrr", [(w.shape, F32, "r")] * 3, name="adamw_small", tl=w.shape[0])


WEIGHTS = ['ffn1_norm', 'ffn1_w_gate', 'ffn1_w_up', 'ffn1_w_down', 'mix_norm', 'ffn2_norm', 'ffn2_w_gate', 'ffn2_w_up',
           'ffn2_w_down', 'ab_w_in', 's5_lambda_re', 's5_lambda_im', 's5_log_dt', 's5_b_re', 's5_b_im', 's5_c_re', 's5_c_im',
           's5_d', 's5_w_glu', 'ab_w_out', 'sc_w_in', 'sc_conv_w', 'sc_w_out', 'final_norm']
SHARDED = {'ffn1_w_gate': (0, FF_SLOT), 'ffn1_w_up': (0, FF_SLOT), 'ffn1_w_down': (0, FF_SLOT),
           'ffn2_w_gate': (0, FF_SLOT), 'ffn2_w_up': (0, FF_SLOT), 'ffn2_w_down': (0, FF_SLOT),
           'ab_w_in': (1, 512), 's5_w_glu': (0, 128), 'ab_w_out': (0, 256), 'sc_w_in': (1, 768), 'sc_conv_w': (1, 256),
           'sc_w_out': (0, 256)}
SWAPPED = ('ffn1_w_gate', 'ffn1_w_up', 'ffn2_w_gate', 'ffn2_w_up')
SMALL = [n for n in WEIGHTS if n not in SHARDED]


def _held(name, t):
    return jnp.swapaxes(t, 1, 2) if name in SWAPPED else t


def _pack(arrays):
    rows = []
    for t in arrays:
        flat = t.reshape(-1)
        rows.append(jnp.pad(flat, (0, (-flat.shape[0]) % 128)))
    flat = jnp.concatenate(rows)
    return jnp.pad(flat, (0, (-flat.shape[0]) % 1024)).reshape(-1, 128)


def _unpack(packed, like):
    flat, out, pos = packed.reshape(-1), [], 0
    for t in like:
        out.append(flat[pos:pos + t.size].reshape(t.shape))
        pos += t.size + (-t.size) % 128
    return out


def _local_grads(x, target, p, full, late=None):
    small, grads, saved = {}, {}, []
    for layer in range(2):
        x, s1 = _ffn_fwd(x, p["ffn1_norm"][layer:layer + 1], full, "ffn1", layer, late)
        if layer == 0:
            x, sm = _mix0_fwd(x, p["mix_norm"][0:1], p, full, late)
        else:
            x, sm = _mix1_fwd(x, p["mix_norm"][1:2], full, late)
        x, s2 = _ffn_fwd(x, p["ffn2_norm"][layer:layer + 1], full, "ffn2", layer, late)
        saved.append((s1, sm, s2))
    loss, dx, dg_final = _loss_head(x, p["final_norm"][None], target)
    small["final_norm"] = dg_final[0]
    gains = {n: [None, None] for n in ("ffn1_norm", "mix_norm", "ffn2_norm")}

    def ffn_bwd(which, layer, dx, s):
        dx, dg = _ffn_bwd(dx, s, p[f"{which}_norm"][layer:layer + 1], full, which, layer, grads, late,
                          inline=(which, layer) in (("ffn2", 1), ("ffn1", 0)))
        gains[f"{which}_norm"][layer] = dg[0]
        return dx

    for layer in (1, 0):
        s1, sm, s2 = saved[layer]
        dx = ffn_bwd("ffn2", layer, dx, s2)
        if layer == 0:
            dx, dg, s5_grads = _mix0_bwd(dx, sm, p["mix_norm"][0:1], p, full, grads, late)
            small.update(s5_grads)
        else:
            dx, dg, dw_in, dcw, dw_out = _mix1_bwd(dx, sm, p["mix_norm"][1:2], full[("sc_w_in", 0)], full[("sc_conv_w", 0)],
                                                   full[("sc_w_out", 0)])
            grads.update({("sc_w_in", 0): dw_in, ("sc_conv_w", 0): dcw.astype(WIRE_DTYPE), ("sc_w_out", 0): dw_out})
        gains["mix_norm"][layer] = dg[0]
        dx = ffn_bwd("ffn1", layer, dx, s1)
    small.update({n: jnp.stack(pair) for n, pair in gains.items()})
    return loss, dx, small, grads


_GATHER_PLAN = {
    "gather_early": [("ffn1_w_gate", 0), ("ffn1_w_up", 0)],
    "ffn1_0_up": [("ffn1_w_down", 0), ("ab_w_in", 0)],
    "sb_fwd": [("s5_w_glu", 0), ("ab_w_out", 0), ("ffn2_w_gate", 0), ("ffn2_w_up", 0), ("ffn2_w_down", 0),
               ("ffn1_w_gate", 1), ("ffn1_w_up", 1), ("ffn1_w_down", 1)],
    "ffn2_0_up": [("sc_w_in", 0), ("sc_conv_w", 0), ("sc_w_out", 0)],
    "ffn1_1_up": [("ffn2_w_gate", 1), ("ffn2_w_up", 1)],
    "mix1_in": [("ffn2_w_down", 1)],
}


class _Late:
    def __init__(self, shards, places):
        self.shards, self.places = shards, places
        self.sent, self.received = set(), {}

    def gather_job(self, carrier):
        keys = _GATHER_PLAN.get(carrier, [])
        if not keys:
            return None, []
        return _Exchange("gather", [self.shards[k] for k in keys], [self.places[k] for k in keys]), keys

    def scatter_job(self, grads):
        keys = [k for k in grads if k not in self.sent]
        if not keys:
            return None, []
        self.sent.update(keys)
        return _Exchange("scatter", [grads[k] for k in keys], [self.places[k] for k in keys]), keys


def kernel(x, ffn1_norm, ffn1_w_gate, ffn1_w_up, ffn1_w_down, mix_norm, ffn2_norm, ffn2_w_gate, ffn2_w_up, ffn2_w_down, ab_w_in, s5_lambda_re, s5_lambda_im, s5_log_dt, s5_b_re, s5_b_im, s5_c_re, s5_c_im, s5_d, s5_w_glu, ab_w_out, sc_w_in, sc_conv_w, sc_w_out, final_norm, loss_target, m_ffn1_norm, m_ffn1_w_gate, m_ffn1_w_up, m_ffn1_w_down, m_mix_norm, m_ffn2_norm, m_ffn2_w_gate, m_ffn2_w_up, m_ffn2_w_down, m_ab_w_in, m_s5_lambda_re, m_s5_lambda_im, m_s5_log_dt, m_s5_b_re, m_s5_b_im, m_s5_c_re, m_s5_c_im, m_s5_d, m_s5_w_glu, m_ab_w_out, m_sc_w_in, m_sc_conv_w, m_sc_w_out, m_final_norm, v_ffn1_norm, v_ffn1_w_gate, v_ffn1_w_up, v_ffn1_w_down, v_mix_norm, v_ffn2_norm, v_ffn2_w_gate, v_ffn2_w_up, v_ffn2_w_down, v_ab_w_in, v_s5_lambda_re, v_s5_lambda_im, v_s5_log_dt, v_s5_b_re, v_s5_b_im, v_s5_c_re, v_s5_c_im, v_s5_d, v_s5_w_glu, v_ab_w_out, v_sc_w_in, v_sc_conv_w, v_sc_w_out, v_final_norm):
    args = dict(locals())
    p = {n: _held(n, args[n]) for n in WEIGHTS}
    mom = {n: _held(n, args["m_" + n]) for n in WEIGHTS}
    var = {n: _held(n, args["v_" + n]) for n in WEIGHTS}

    keys = [(n, layer) for n in SHARDED for layer in range(p[n].shape[0])]
    shards, places = {}, {}
    for n, layer in keys:
        axis, width = SHARDED[n]
        t = p[n][layer] if n == "sc_conv_w" else p[n][layer].astype(MXU_DTYPE)
        pad = [(0, 0), (0, 0)]
        pad[axis] = (0, width - t.shape[axis])
        shards[(n, layer)], places[(n, layer)] = jnp.pad(t, pad), (axis, width)
    late = _Late(shards, places)
    job, first = late.gather_job("gather_early")
    full = dict(zip(first, _exchange_call(job, "gather_early")))

    loss, dx, small, grads = _local_grads(x[0], loss_target[0], p, full, late)
    loss = lax.psum(loss[0, 0], ("x", "y", "c"))
    assert set(late.received) == set(keys), "a gradient was left without a carrier"

    partial = [_chip_sum(late.received[(n, layer)], name=f"chip_sum_{n}_{layer}") for n, layer in keys]
    other = _swap_with_sibling(partial)
    out = {}
    for (n, layer), mine, theirs in zip(keys, partial, other):
        out[n] = _adamw_layer(p[n], mom[n], var[n], mine, theirs, layer, out.get(n), name=f"adamw_{n}_{layer}")
    out = {n: [_held(n, t) for t in res] for n, res in out.items()}

    like = [p[n] for n in SMALL]
    g_small = _sum_all_devices(_pack([small[n] for n in SMALL]))
    d_small, m_small, v_small = _adamw_small(_pack(like), g_small, _pack([mom[n] for n in SMALL]), _pack([var[n] for n in SMALL]))
    for k, packed in enumerate((g_small, d_small, m_small, v_small)):
        for n, t in zip(SMALL, _unpack(packed, like)):
            out.setdefault(n, [None] * 4)[k] = t

    return (loss, dx[None], *[out[n][0] for n in WEIGHTS], *[out[n][1] for n in WEIGHTS],
            *[out[n][2] for n in WEIGHTS], *[out[n][3] for n in WEIGHTS])
```

```python
import functools
import math

import jax
import jax.numpy as jnp
from jax import lax
from jax.experimental import pallas as pl
from jax.experimental.pallas import tpu as pltpu

F32 = jnp.float32
MXU_DTYPE = jnp.bfloat16
WIRE_DTYPE = jnp.bfloat16
MESH_ID = pl.DeviceIdType.MESH

D_MODEL = 1024
D_FF = 2752
N_CHIPS = 4
FF_SHARD = D_FF // N_CHIPS
FF_SLOT = 768
FF_PAD = N_CHIPS * FF_SLOT
S5_WIDTH = 512
S5_GROUP = 16
S5_GROUPS = 32
S5_STATE = 64
S5_LANES = S5_GROUPS * S5_STATE
S5_BLOCK = 512
S5_DIAG = S5_LANES // S5_BLOCK
SB_HEADS = 8
SB_DH = 64
SB_SCALE = 0.125
SB_PACK = 2
SB_QUERIES = 1024
SB_KEYS = 256
EPS = 1e-6
ADAM_LR, ADAM_B1, ADAM_B2, ADAM_EPS, ADAM_WD, ADAM_STEP = 0.001, 0.9, 0.999, 1e-08, 0.01, 10
VMEM_LIMIT = 56 * 1024 * 1024

ANY_SPEC = pl.BlockSpec(memory_space=pl.ANY)


def _params(*sem):
    return pltpu.CompilerParams(dimension_semantics=sem or None, vmem_limit_bytes=VMEM_LIMIT)


def _mm(a, b, *, name, ta=False, tb=False, out_dtype=F32, epilogue=None, extras=(), tm=1024, tn=1024, tk=1024, job=None):
    m, k = (a.shape[1], a.shape[0]) if ta else a.shape
    n = b.shape[0] if tb else b.shape[1]
    tm, tn, tk = min(tm, m), min(tn, n), min(tk, k)
    assert m % tm == 0 and n % tn == 0 and k % tk == 0, (name, m, n, k)
    grid = (m // tm, n // tn, k // tk)
    a_spec = pl.BlockSpec((tk, tm), lambda i, j, kk: (kk, i)) if ta else pl.BlockSpec((tm, tk), lambda i, j, kk: (i, kk))
    b_spec = pl.BlockSpec((tn, tk), lambda i, j, kk: (j, kk)) if tb else pl.BlockSpec((tk, tn), lambda i, j, kk: (kk, j))
    nk = grid[2]
    ex_specs = []
    for e in extras:
        if e.shape == (m, n):
            ex_specs.append(pl.BlockSpec((tm, tn), lambda i, j, kk: (i, j)))
        elif e.shape == (1, n):
            ex_specs.append(pl.BlockSpec((1, tn), lambda i, j, kk: (0, j)))
        else:
            assert e.shape == (m, 1), (name, e.shape)
            ex_specs.append(pl.BlockSpec((tm, 1), lambda i, j, kk: (i, 0)))
    dims = (((0 if ta else 1,), (1 if tb else 0,)), ((), ()))
    n_ex = len(extras)

    out_dtypes = list(out_dtype) if isinstance(out_dtype, (list, tuple)) else [out_dtype]
    n_out = len(out_dtypes)

    def body(a_ref, b_ref, *rest):
        ex_refs, o_refs = rest[:n_ex], rest[n_ex:n_ex + n_out]

        def product():
            return lax.dot_general(a_ref[...].astype(MXU_DTYPE), b_ref[...].astype(MXU_DTYPE), dims, preferred_element_type=F32)

        def finish(r):
            if epilogue is not None:
                r = epilogue(r, *[e[...] for e in ex_refs])
            for o_ref, val in zip(o_refs, r if isinstance(r, (tuple, list)) else (r,)):
                o_ref[...] = val.astype(o_ref.dtype)

        if nk == 1:
            finish(product())
            return
        acc_ref, kk = rest[n_ex + n_out], pl.program_id(2)

        @pl.when(kk == 0)
        def _():
            acc_ref[...] = jnp.zeros_like(acc_ref)

        acc_ref[...] += product()

        @pl.when(kk == nk - 1)
        def _():
            finish(acc_ref[...])

    res, got = _carried_call(
        body, name=name, grid=grid,
        in_specs=[a_spec, b_spec, *ex_specs],
        out_specs=[pl.BlockSpec((tm, tn), lambda i, j, kk: (i, j))] * n_out,
        out_shape=[jax.ShapeDtypeStruct((m, n), dt) for dt in out_dtypes],
        scratch_shapes=[pltpu.VMEM((tm, tn), F32)] if nk > 1 else [],
        semantics=("parallel", "parallel", "arbitrary"), operands=(a, b, *extras), job=job)
    res = res if isinstance(out_dtype, (list, tuple)) else res[0]
    return res if job is None else (res, got)


def _row_block(rows, want, tile=8):
    for tl in range(min(want, rows), tile - 1, -1):
        if rows % tl == 0 and tl % tile == 0:
            return tl
    return rows


def _rowmap(fn, ins, in_kinds, outs, *, name, tl):
    rows = next(x.shape[0] for x, kd in zip(ins, in_kinds) if kd == "r")
    tl = _row_block(rows, tl)
    n_in = len(ins)

    def spec(shape, kind):
        if kind == "r":
            return pl.BlockSpec((tl,) + tuple(shape[1:]), lambda i: (i,) + (0,) * (len(shape) - 1))
        return pl.BlockSpec(tuple(shape), lambda i: (0,) * len(shape))

    def body(*refs):
        in_refs, out_refs = refs[:n_in], refs[n_in:]
        res = fn(*[r[...] for r in in_refs])
        if not isinstance(res, (tuple, list)):
            res = (res,)
        for o_ref, val, (_, dt, kind) in zip(out_refs, res, outs):
            if kind == "r":
                o_ref[...] = val.astype(dt)
            else:
                @pl.when(pl.program_id(0) == 0)
                def _():
                    o_ref[...] = jnp.zeros_like(o_ref)

                o_ref[...] += val.astype(dt)

    has_acc = any(kd == "a" for _, _, kd in outs)
    res = pl.pallas_call(
        body, name=name, grid=(rows // tl,),
        in_specs=[spec(x.shape, kd) for x, kd in zip(ins, in_kinds)],
        out_specs=[spec(s, kd) for s, _, kd in outs],
        out_shape=[jax.ShapeDtypeStruct(s, dt) for s, dt, _ in outs],
        compiler_params=_params("arbitrary" if has_acc else "parallel"),
    )(*ins)
    return res[0] if len(outs) == 1 else res


def _rms_fwd(x, g):
    r = lax.rsqrt(jnp.mean(x * x, axis=-1, keepdims=True) + EPS)
    return x * r * g


def _rms_bwd(dh, x, g):
    r = lax.rsqrt(jnp.mean(x * x, axis=-1, keepdims=True) + EPS)
    xh = x * r
    dxh = dh * g
    dx = r * (dxh - xh * jnp.mean(dxh * xh, axis=-1, keepdims=True))
    return dx, jnp.sum(dh * xh, axis=0, keepdims=True)


def _swiglu_act(a, b):
    return jax.nn.silu(a) * b


def _ffn_up(x, g, wg, wu, *, name, tm=1024, tn=1024, job=None):
    m, d = x.shape
    n = wg.shape[0]
    tm, tn = min(tm, m), min(tn, n)
    assert m % tm == 0 and n % tn == 0, (name, m, n)

    def body(x_ref, g_ref, wg_ref, wu_ref, h_ref, a_ref, b_ref, s_ref):
        @pl.when(pl.program_id(1) == 0)
        def _():
            h_ref[...] = _rms_fwd(x_ref[...], g_ref[...]).astype(h_ref.dtype)

        hv = h_ref[...]
        av = lax.dot_general(hv, wg_ref[...], NT_DIMS, preferred_element_type=F32)
        bv = lax.dot_general(hv, wu_ref[...], NT_DIMS, preferred_element_type=F32)
        a_ref[...] = av.astype(a_ref.dtype)
        b_ref[...] = bv.astype(b_ref.dtype)
        s_ref[...] = _swiglu_act(av, bv).astype(s_ref.dtype)

    rows = pl.BlockSpec((tm, d), lambda i, j: (i, 0))
    wgt = pl.BlockSpec((tn, d), lambda i, j: (j, 0))
    tile = pl.BlockSpec((tm, tn), lambda i, j: (i, j))
    return _carried_call(
        body, name=name, grid=(m // tm, n // tn),
        in_specs=[rows, pl.BlockSpec((1, d), lambda i, j: (0, 0)), wgt, wgt],
        out_specs=[rows, tile, tile, tile],
        out_shape=[jax.ShapeDtypeStruct((m, d), MXU_DTYPE)] + [jax.ShapeDtypeStruct((m, n), MXU_DTYPE)] * 3,
        semantics=("parallel", "arbitrary"), operands=(x, g, wg, wu), job=job)


def _norm_proj(x, g, w, *, name, tm=1024, tn=1024, job=None):
    m, d = x.shape
    n = w.shape[1]
    tm, tn = min(tm, m), min(tn, n)
    assert m % tm == 0 and n % tn == 0, (name, m, n)

    def body(x_ref, g_ref, w_ref, h_ref, o_ref):
        @pl.when(pl.program_id(1) == 0)
        def _():
            h_ref[...] = _rms_fwd(x_ref[...], g_ref[...]).astype(h_ref.dtype)

        o_ref[...] = jnp.dot(h_ref[...], w_ref[...], preferred_element_type=F32)

    rows = pl.BlockSpec((tm, d), lambda i, j: (i, 0))
    return _carried_call(
        body, name=name, grid=(m // tm, n // tn),
        in_specs=[rows, pl.BlockSpec((1, d), lambda i, j: (0, 0)), pl.BlockSpec((d, tn), lambda i, j: (0, j))],
        out_specs=[rows, pl.BlockSpec((tm, tn), lambda i, j: (i, j))],
        out_shape=[jax.ShapeDtypeStruct((m, d), MXU_DTYPE), jax.ShapeDtypeStruct((m, n), F32)],
        semantics=("parallel", "arbitrary"), operands=(x, g, w), job=job)


def _proj_norm_bwd(pairs, x, g, dres, *, name, tm=1024, tk=1024, job=None):
    m, f = pairs[0][0].shape
    d = x.shape[1]
    tm, tk = min(tm, m), min(tk, f)
    assert m % tm == 0 and f % tk == 0, (name, m, f)
    nk, n_pairs = f // tk, len(pairs)
    swapped = [kept == "FD" for _, _, kept in pairs]

    def body(*refs):
        dy_refs, w_refs = refs[:n_pairs], refs[n_pairs:2 * n_pairs]
        x_ref, g_ref, dr_ref, dx_ref, dg_ref, acc_ref = refs[2 * n_pairs:]
        i, kk = pl.program_id(0), pl.program_id(1)

        @pl.when(kk == 0)
        def _():
            acc_ref[...] = jnp.zeros_like(acc_ref)

        for dy_ref, w_ref, rows_are_f in zip(dy_refs, w_refs, swapped):
            dims = (((1,), (0,)), ((), ())) if rows_are_f else NT_DIMS
            acc_ref[...] += lax.dot_general(dy_ref[...].astype(MXU_DTYPE), w_ref[...], dims, preferred_element_type=F32)

        @pl.when(jnp.logical_and(i == 0, kk == 0))
        def _():
            dg_ref[...] = jnp.zeros_like(dg_ref)

        @pl.when(kk == nk - 1)
        def _():
            dx, dg = _rms_bwd(acc_ref[...], x_ref[...], g_ref[...])
            dx_ref[...] = dx + dr_ref[...]
            dg_ref[...] += dg

    act = pl.BlockSpec((tm, tk), lambda i, kk: (i, kk))
    w_specs = [pl.BlockSpec((tk, d), lambda i, kk: (kk, 0)) if s else pl.BlockSpec((d, tk), lambda i, kk: (0, kk)) for s in swapped]
    rows = pl.BlockSpec((tm, d), lambda i, kk: (i, 0))
    one = pl.BlockSpec((1, d), lambda i, kk: (0, 0))
    return _carried_call(
        body, name=name, grid=(m // tm, nk),
        in_specs=[act] * n_pairs + w_specs + [rows, one, rows],
        out_specs=[rows, one],
        out_shape=[jax.ShapeDtypeStruct((m, d), F32), jax.ShapeDtypeStruct((1, d), F32)],
        scratch_shapes=[pltpu.VMEM((tm, d), F32)],
        semantics=("arbitrary", "arbitrary"), operands=(*[p[0] for p in pairs], *[p[1] for p in pairs], x, g, dres), job=job)


def _ffn_dx(da, db, wg, wu, x, g, dres, *, name, job=None):
    return _proj_norm_bwd([(da, wg, "FD"), (db, wu, "FD")], x, g, dres, name=name, job=job)


def _ffn_fwd(x, g, full, which, layer, late):
    tag = f"{which}_{layer}"
    job, keys = late.gather_job(f"{tag}_up") if late else (None, [])
    (h, a, b, s), got = _ffn_up(x, g, full[(f"{which}_w_gate", layer)], full[(f"{which}_w_up", layer)], name=f"{tag}_up", job=job)
    full.update(zip(keys, got))
    x2 = _mm(s, full[(f"{which}_w_down", layer)], name=f"{tag}_down", epilogue=lambda acc, xv: xv + 0.5 * acc, extras=[x],
             tk=FF_PAD)
    return x2, (x, h, a, b, s)


def _ffn_bwd(dx2, saved, g, full, which, layer, grads, late, inline):
    x, h, a, b, s = saved
    tag = f"{which}_{layer}"
    kg, ku, kd = [(f"{which}_w_{n}", layer) for n in ("gate", "up", "down")]
    wg, wu, wd = full[kg], full[ku], full[kd]
    send = (lambda: late.scatter_job(grads)) if (late and inline) else (lambda: (None, []))

    def act_bwd(ds, av, bv):
        _, vjp = jax.vjp(_swiglu_act, av.astype(F32), bv.astype(F32))
        return vjp(0.5 * ds)

    grads[kd] = _mm(s, dx2, ta=True, name=f"{tag}_dwd", out_dtype=WIRE_DTYPE, epilogue=lambda acc: 0.5 * acc, tk=2048)
    job, keys = send()
    (da, db), got = _carried(_mm, dx2, wd, tb=True, name=f"{tag}_dact", epilogue=act_bwd, extras=[a, b],
                             out_dtype=[MXU_DTYPE, MXU_DTYPE], job=job)
    _note(late, keys, got)
    grads[kg] = _mm(da, h, ta=True, name=f"{tag}_dwg", out_dtype=WIRE_DTYPE, tk=2048)
    job, keys = send()
    grads[ku], got = _carried(_mm, db, h, ta=True, name=f"{tag}_dwu", out_dtype=WIRE_DTYPE, tk=2048, job=job)
    _note(late, keys, got)
    job, keys = send()
    (dx, dg), got = _ffn_dx(da, db, wg, wu, x, g, dx2, name=f"{tag}_dx", job=job)
    _note(late, keys, got)
    return dx, dg


def _carried(fn, *args, job, **kwargs):
    return fn(*args, job=job, **kwargs) if job is not None else (fn(*args, **kwargs), [])


def _note(late, keys, got):
    if late:
        late.received.update(zip(keys, got))


def _softplus(z):
    return jnp.maximum(z, 0.0) + jnp.log(1.0 + jnp.exp(-jnp.abs(z)))


def _ones_dot(x, tri):
    if MXU_DTYPE == F32:
        return jnp.dot(x, tri, preferred_element_type=F32)
    hi = x.astype(MXU_DTYPE)
    lo = (x - hi.astype(F32)).astype(MXU_DTYPE)
    return jnp.dot(hi, tri, preferred_element_type=F32) + jnp.dot(lo, tri, preferred_element_type=F32)


NT_DIMS = (((1,), (1,)), ((), ()))
TN_DIMS = (((0,), (0,)), ((), ()))


SB_LANES = SB_PACK * SB_DH
Q_COL, K_COL, V_COL = (S5_WIDTH * n // SB_LANES for n in (1, 2, 3))


def _head_lanes(rows, hd):
    return lax.broadcasted_iota(jnp.int32, (rows, SB_LANES), 1) // SB_DH == hd


def _attend(proj, *, tq=SB_QUERIES, job=None):
    seq = proj.shape[0]
    tq = min(tq, seq)
    tk = min(SB_KEYS, tq)
    per, hp = tq // tk, SB_PACK

    def body(q_ref, k_ref, v_ref, o_ref, ls_ref):
        i = pl.program_id(1)
        r_idx = lax.broadcasted_iota(jnp.int32, (tk, tk), 0)
        c_idx = lax.broadcasted_iota(jnp.int32, (tk, tk), 1)
        after = (r_idx > c_idx).astype(MXU_DTYPE)
        lanes = [_head_lanes(tk, hd) for hd in range(hp)]

        def block(j, cs, acc, straddles):
            off = pl.multiple_of(j * tk, tk)
            k2, v2 = k_ref[pl.ds(off, tk), :], v_ref[pl.ds(off, tk), :]
            top = 0 if straddles is None else straddles * tk
            rows = tq - top
            q2 = (q_ref[pl.ds(top, rows), :] * SB_SCALE).astype(MXU_DTYPE)
            new_cs, out = [], acc[top:]
            for hd in range(hp):
                kv = jnp.where(lanes[hd], k2, 0.0).astype(MXU_DTYPE)
                vv = jnp.where(lanes[hd], v2, 0.0).astype(MXU_DTYPE)
                z = lax.dot_general(q2, kv, NT_DIMS, preferred_element_type=F32)
                sp = _softplus(z)
                c_in = cs[hd][top:]
                if straddles is None:
                    lk = -sp
                    w = jnp.exp(z - sp + _ones_dot(lk, after) + c_in)
                else:
                    before = lax.broadcasted_iota(jnp.int32, (rows, tk), 1) < lax.broadcasted_iota(jnp.int32, (rows, tk), 0)
                    lk = jnp.where(before, -sp, 0.0)
                    w = jnp.where(before, jnp.exp(z - sp + _ones_dot(lk, after) + c_in), 0.0)
                out = out + jnp.dot(w.astype(MXU_DTYPE), vv, preferred_element_type=F32)
                c_new = c_in + jnp.sum(lk, axis=1, keepdims=True)
                new_cs.append(jnp.concatenate([cs[hd][:top], c_new], axis=0) if top else c_new)
            return tuple(new_cs), (jnp.concatenate([acc[:top], out], axis=0) if top else out)

        carry = (tuple(jnp.zeros((tq, 1), F32) for _ in range(hp)), jnp.zeros((tq, SB_LANES), F32))
        for s in reversed(range(per)):
            carry = block(i * per + s, *carry, s)
        cs, acc = lax.fori_loop(0, i * per, lambda n, cr: block(i * per - 1 - n, *cr, None), carry)
        o_ref[...] = acc
        for hd in range(hp):
            ls_ref[hd] = cs[hd]

    whole = lambda col: pl.BlockSpec((seq, SB_LANES), lambda g, i: (0, col + g))
    return _carried_call(
        body, name="sb_fwd", grid=(SB_HEADS // hp, seq // tq),
        in_specs=[pl.BlockSpec((tq, SB_LANES), lambda g, i: (i, Q_COL + g)), whole(K_COL), whole(V_COL)],
        out_specs=[pl.BlockSpec((tq, SB_LANES), lambda g, i: (i, g)), pl.BlockSpec((hp, tq, 1), lambda g, i: (g, i, 0))],
        out_shape=[jax.ShapeDtypeStruct((seq, SB_HEADS * SB_DH), F32), jax.ShapeDtypeStruct((SB_HEADS, seq, 1), F32)],
        semantics=("parallel", "parallel"), operands=(proj, proj, proj), job=job)


def _attend_bwd(proj, lsum, dmix, *, tq=SB_QUERIES, job=None):
    seq = proj.shape[0]
    tq = min(tq, seq)
    tk = min(SB_KEYS, tq)
    per, hp = tq // tk, SB_PACK
    do_col = S5_WIDTH // SB_LANES

    def body(q_ref, k_ref, v_ref, ls_ref, do_ref, dq_ref, dk_ref, dv_ref):
        i = pl.program_id(1)

        @pl.when(i == 0)
        def _():
            dk_ref[...] = jnp.zeros_like(dk_ref)
            dv_ref[...] = jnp.zeros_like(dv_ref)

        r_idx = lax.broadcasted_iota(jnp.int32, (tk, tk), 0)
        c_idx = lax.broadcasted_iota(jnp.int32, (tk, tk), 1)
        upto = (r_idx <= c_idx).astype(MXU_DTYPE)
        before = (r_idx < c_idx).astype(MXU_DTYPE)
        lanes = [_head_lanes(tk, hd) for hd in range(hp)]

        def block(j, sums, dq, straddles):
            off = pl.multiple_of(j * tk, tk)
            k2, v2 = k_ref[pl.ds(off, tk), :], v_ref[pl.ds(off, tk), :]
            top = 0 if straddles is None else straddles * tk
            rows = tq - top
            part = pl.ds(top, rows)
            q2 = (q_ref[part, :] * SB_SCALE).astype(MXU_DTYPE)
            do2 = do_ref[part, :].astype(MXU_DTYPE)
            valid = None
            if straddles is not None:
                valid = lax.broadcasted_iota(jnp.int32, (rows, tk), 1) < lax.broadcasted_iota(jnp.int32, (rows, tk), 0)
            new_sums, out, dk, dv = [], dq[top:], jnp.zeros((tk, SB_LANES), F32), jnp.zeros((tk, SB_LANES), F32)
            for hd in range(hp):
                cp, ce = sums[hd]
                kv = jnp.where(lanes[hd], k2, 0.0).astype(MXU_DTYPE)
                vv = jnp.where(lanes[hd], v2, 0.0).astype(MXU_DTYPE)
                z = lax.dot_general(q2, kv, NT_DIMS, preferred_element_type=F32)
                sp = _softplus(z)
                lk = -sp if valid is None else jnp.where(valid, -sp, 0.0)
                w = jnp.exp(z - sp + (ls_ref[hd, part, :] - cp[top:]) - _ones_dot(lk, upto))
                if valid is not None:
                    w = jnp.where(valid, w, 0.0)
                e = w * lax.dot_general(do2, vv, NT_DIMS, preferred_element_type=F32)
                earlier = jnp.dot(e.astype(MXU_DTYPE), before, preferred_element_type=F32) + ce[top:]
                keep = jnp.exp(-sp)
                dz = e * keep - (1.0 - keep) * earlier
                if valid is not None:
                    dz = jnp.where(valid, dz, 0.0)
                dzm = dz.astype(MXU_DTYPE)
                out = out + jnp.dot(dzm, kv, preferred_element_type=F32)
                dk = dk + jnp.where(lanes[hd], lax.dot_general(dzm, q2, TN_DIMS, preferred_element_type=F32), 0.0)
                dv = dv + jnp.where(lanes[hd], lax.dot_general(w.astype(MXU_DTYPE), do2, TN_DIMS, preferred_element_type=F32), 0.0)
                new = (cp[top:] + jnp.sum(lk, axis=1, keepdims=True), ce[top:] + jnp.sum(e, axis=1, keepdims=True))
                new_sums.append(tuple(jnp.concatenate([old[:top], val], axis=0) for old, val in zip((cp, ce), new)) if top else new)
            dk_ref[pl.ds(off, tk), :] += dk
            dv_ref[pl.ds(off, tk), :] += dv
            return tuple(new_sums), (jnp.concatenate([dq[:top], out], axis=0) if top else out)

        zero = jnp.zeros((tq, 1), F32)
        carry = (tuple((zero, zero) for _ in range(hp)), jnp.zeros((tq, SB_LANES), F32))
        carry = lax.fori_loop(0, i * per, lambda j, cr: block(j, *cr, None), carry)
        for s in range(per):
            carry = block(i * per + s, *carry, s)
        dq_ref[...] = carry[1] * SB_SCALE

    whole = lambda col: pl.BlockSpec((seq, SB_LANES), lambda g, i: (0, col + g))
    tile = lambda col: pl.BlockSpec((tq, SB_LANES), lambda g, i: (i, col + g))
    acc = pl.BlockSpec((seq, SB_LANES), lambda g, i: (0, g))
    return _carried_call(
        body, name="sb_bwd", grid=(SB_HEADS // hp, seq // tq),
        in_specs=[tile(Q_COL), whole(K_COL), whole(V_COL), pl.BlockSpec((hp, tq, 1), lambda g, i: (g, i, 0)), tile(do_col)],
        out_specs=[tile(0), acc, acc],
        out_shape=[jax.ShapeDtypeStruct((seq, SB_HEADS * SB_DH), F32)] * 3,
        semantics=("parallel", "arbitrary"), operands=(proj, proj, proj, lsum, dmix), job=job)


def _s5_disc(lr, li, ldt, br, bi):
    dt = jnp.exp(ldt)
    mag = jnp.exp(lr * dt)
    ar = mag * jnp.cos(li * dt)
    ai = mag * jnp.sin(li * dt)
    den = lr * lr + li * li
    nr = ar - 1.0
    cr = (nr * lr + ai * li) / den
    ci = (ai * lr - nr * li) / den
    return ar, ai, cr[None] * br - ci[None] * bi, cr[None] * bi + ci[None] * br


def _s5_prep(lr, li, ldt, br, bi):
    shapes = [lr.shape, lr.shape, br.shape, br.shape]

    def body(lr_ref, li_ref, ldt_ref, br_ref, bi_ref, *outs):
        for o, val in zip(outs, _s5_disc(lr_ref[...], li_ref[...], ldt_ref[...], br_ref[...], bi_ref[...])):
            o[...] = val

    return pl.pallas_call(body, name="s5_prep", out_shape=[jax.ShapeDtypeStruct(s, F32) for s in shapes])(lr, li, ldt, br, bi)


def _s5_prep_bwd(lr, li, ldt, br, bi, cts):
    args = (lr, li, ldt, br, bi)

    def body(*refs):
        ins, ct_refs, outs = refs[:5], refs[5:9], refs[9:]
        _, vjp = jax.vjp(_s5_disc, *[r[...] for r in ins])
        for o, val in zip(outs, vjp(tuple(r[...] for r in ct_refs))):
            o[...] = val

    return pl.pallas_call(body, name="s5_prep_bwd", out_shape=[jax.ShapeDtypeStruct(a.shape, F32) for a in args])(*args, *cts)


SCAN_ROWS = 8


def _powers(ar, ai):
    out = [(ar, ai)]
    for _ in range(SCAN_ROWS - 1):
        pr, pi = out[-1]
        out.append((pr * ar - pi * ai, pr * ai + pi * ar))
    return out


def _s5_states(u, bmat, cmat, a, d, *, tc=512):
    seq, width = u.shape
    nj, cols, w2 = bmat.shape
    tw = w2 // 2
    tc = min(tc, seq)
    assert seq % tc == 0 and nj * cols == width and tw == S5_BLOCK

    def body(u_ref, bm_ref, cm_ref, a_ref, d_ref, h_ref, y_ref, cr_ref, ci_ref):
        @pl.when(pl.program_id(1) == 0)
        def _():
            cr_ref[...] = jnp.zeros_like(cr_ref)
            ci_ref[...] = jnp.zeros_like(ci_ref)

        uv = u_ref[...]
        h_ref[...] = jnp.dot(uv.astype(MXU_DTYPE), bm_ref[0], preferred_element_type=F32)
        re, im = pl.ds(0, tw), pl.ds(tw, tw)
        powers = _powers(a_ref[:, re], a_ref[:, im])
        pr = jnp.concatenate([p[0] for p in powers], axis=0)
        pi = jnp.concatenate([p[1] for p in powers], axis=0)
        row_id = lax.broadcasted_iota(jnp.int32, (SCAN_ROWS, tw), 0)
        reach = {dist: tuple(jnp.where(row_id >= dist, part, 0.0) for part in powers[dist - 1]) for dist in (1, 2, 4)}

        def block(n, carry):
            hr, hi = carry
            rows = pl.ds(pl.multiple_of(n * SCAN_ROWS, SCAN_ROWS), SCAN_ROWS)
            yr, yi = h_ref[rows, re], h_ref[rows, im]
            for dist in (1, 2, 4):
                cr, ci = reach[dist]
                sr, si = pltpu.roll(yr, dist, 0), pltpu.roll(yi, dist, 0)
                yr, yi = yr + cr * sr - ci * si, yi + cr * si + ci * sr
            yr, yi = yr + pr * hr - pi * hi, yi + pr * hi + pi * hr
            h_ref[rows, re] = yr
            h_ref[rows, im] = yi
            return yr[SCAN_ROWS - 1:], yi[SCAN_ROWS - 1:]

        hr, hi = lax.fori_loop(0, tc // SCAN_ROWS, block, (cr_ref[...], ci_ref[...]), unroll=4)
        cr_ref[...] = hr
        ci_ref[...] = hi
        y_ref[...] = jnp.dot(h_ref[...].astype(MXU_DTYPE), cm_ref[0], preferred_element_type=F32) + d_ref[...] * uv

    io = pl.BlockSpec((tc, cols), lambda j, t: (t, j))
    return pl.pallas_call(
        body, name="s5_states", grid=(nj, seq // tc),
        in_specs=[io, pl.BlockSpec((1, cols, w2), lambda j, t: (j, 0, 0)), pl.BlockSpec((1, w2, cols), lambda j, t: (j, 0, 0)),
                  pl.BlockSpec((1, w2), lambda j, t: (0, j)), pl.BlockSpec((1, cols), lambda j, t: (0, j))],
        out_specs=[pl.BlockSpec((tc, w2), lambda j, t: (t, j)), io],
        out_shape=[jax.ShapeDtypeStruct((seq, nj * w2), F32), jax.ShapeDtypeStruct((seq, width), F32)],
        scratch_shapes=[pltpu.VMEM((1, tw), F32)] * 2,
        compiler_params=_params("parallel", "arbitrary"),
    )(u, bmat, cmat, a, d)


def _s5_states_bwd(dy, h, u, bmat, cmat, a, du_skip, *, tc=512):
    seq, width = u.shape
    nj, cols, w2 = bmat.shape
    tw = w2 // 2
    tc = min(tc, seq)
    assert seq % tc == 0
    nt = seq // tc

    def body(dy_ref, h_ref, u_ref, bm_ref, cm_ref, a_ref, sk_ref, du_ref, da_ref, db_ref, dc_ref, g_ref, cr_ref, ci_ref):
        @pl.when(pl.program_id(1) == 0)
        def _():
            cr_ref[...] = jnp.zeros_like(cr_ref)
            ci_ref[...] = jnp.zeros_like(ci_ref)
            da_ref[...] = jnp.zeros_like(da_ref)
            db_ref[...] = jnp.zeros_like(db_ref)
            dc_ref[...] = jnp.zeros_like(dc_ref)

        dyv = dy_ref[...].astype(MXU_DTYPE)
        g_ref[...] = lax.dot_general(dyv, cm_ref[0], NT_DIMS, preferred_element_type=F32)
        re, im = pl.ds(0, tw), pl.ds(tw, tw)
        powers = _powers(a_ref[:, re], a_ref[:, im])
        pr = jnp.concatenate([p[0] for p in reversed(powers)], axis=0)
        pi = jnp.concatenate([p[1] for p in reversed(powers)], axis=0)
        row_id = lax.broadcasted_iota(jnp.int32, (SCAN_ROWS, tw), 0)
        last = SCAN_ROWS - 1
        reach = {dist: tuple(jnp.where(row_id < SCAN_ROWS - dist, part, 0.0) for part in powers[dist - 1]) for dist in (1, 2, 4)}

        def block(n, carry):
            gr, gi, sr, si = carry
            rows = pl.ds(pl.multiple_of((tc // SCAN_ROWS - 1 - n) * SCAN_ROWS, SCAN_ROWS), SCAN_ROWS)
            yr, yi = g_ref[rows, re], g_ref[rows, im]
            for dist in (1, 2, 4):
                cr, ci = reach[dist]
                ur, ui = pltpu.roll(yr, SCAN_ROWS - dist, 0), pltpu.roll(yi, SCAN_ROWS - dist, 0)
                yr, yi = yr + cr * ur + ci * ui, yi + cr * ui - ci * ur
            yr, yi = yr + pr * gr + pi * gi, yi + pr * gi - pi * gr
            g_ref[rows, re] = yr
            g_ref[rows, im] = yi
            nr = jnp.where(row_id < last, pltpu.roll(yr, last, 0), gr)
            ni = jnp.where(row_id < last, pltpu.roll(yi, last, 0), gi)
            hr, hi = h_ref[rows, re], h_ref[rows, im]
            return yr[:1], yi[:1], sr + nr * hr + ni * hi, si + ni * hr - nr * hi

        zero = jnp.zeros((SCAN_ROWS, tw), F32)
        gr, gi, sr, si = lax.fori_loop(0, tc // SCAN_ROWS, block, (cr_ref[...], ci_ref[...], zero, zero), unroll=4)
        cr_ref[...] = gr
        ci_ref[...] = gi
        da_ref[:, re] += jnp.sum(sr, axis=0, keepdims=True)
        da_ref[:, im] += jnp.sum(si, axis=0, keepdims=True)
        gv = g_ref[...].astype(MXU_DTYPE)
        du_ref[...] = (lax.dot_general(gv, bm_ref[0], NT_DIMS, preferred_element_type=F32) + sk_ref[...]).astype(du_ref.dtype)
        db_ref[0] += lax.dot_general(u_ref[...].astype(MXU_DTYPE), gv, TN_DIMS, preferred_element_type=F32)
        dc_ref[0] += lax.dot_general(h_ref[...].astype(MXU_DTYPE), dyv, TN_DIMS, preferred_element_type=F32)

    io = pl.BlockSpec((tc, cols), lambda j, t: (nt - 1 - t, j))
    bm = pl.BlockSpec((1, cols, w2), lambda j, t: (j, 0, 0))
    cm = pl.BlockSpec((1, w2, cols), lambda j, t: (j, 0, 0))
    row = pl.BlockSpec((1, w2), lambda j, t: (0, j))
    return pl.pallas_call(
        body, name="s5_states_bwd", grid=(nj, nt),
        in_specs=[io, pl.BlockSpec((tc, w2), lambda j, t: (nt - 1 - t, j)), io, bm, cm, row, io],
        out_specs=[io, row, bm, cm],
        out_shape=[jax.ShapeDtypeStruct((seq, width), MXU_DTYPE), jax.ShapeDtypeStruct((1, nj * w2), F32),
                   jax.ShapeDtypeStruct(bmat.shape, F32), jax.ShapeDtypeStruct(cmat.shape, F32)],
        scratch_shapes=[pltpu.VMEM((tc, w2), F32)] + [pltpu.VMEM((1, tw), F32)] * 2,
        compiler_params=_params("parallel", "arbitrary"),
    )(dy, h, u, bmat, cmat, a, du_skip)


def _pair_columns(re, im, axis):
    shape = re.shape
    split = shape[:axis] + (shape[axis] // S5_BLOCK, S5_BLOCK) + shape[axis + 1:]
    both = jnp.stack([re.reshape(split), im.reshape(split)], axis=axis + 1)
    return both.reshape(shape[:axis] + (2 * shape[axis],) + shape[axis + 1:])


def _unpair_columns(t, axis):
    shape = t.shape
    both = t.reshape(shape[:axis] + (shape[axis] // (2 * S5_BLOCK), 2, S5_BLOCK) + shape[axis + 1:])
    half = shape[:axis] + (shape[axis] // 2,) + shape[axis + 1:]
    return (lax.index_in_dim(both, 0, axis + 1, keepdims=False).reshape(half),
            lax.index_in_dim(both, 1, axis + 1, keepdims=False).reshape(half))


S5_PER_BLOCK = S5_GROUPS // S5_DIAG


def _block_diag(t):
    g, a, b = t.shape
    n = S5_PER_BLOCK
    eye = jnp.eye(n, dtype=t.dtype)
    return (t.reshape(g // n, n, a, 1, b) * eye[None, :, None, :, None]).reshape(g // n, n * a, n * b)


def _block_diag_part(m):
    j, n = m.shape[0], S5_PER_BLOCK
    a, b = m.shape[1] // n, m.shape[2] // n
    return jnp.moveaxis(jnp.diagonal(m.reshape(j, n, a, n, b), axis1=1, axis2=3), -1, 1).reshape(j * n, a, b)


def _gelu_glu(y, gate_pre):
    z = jax.nn.gelu(y)
    return z * jax.nn.sigmoid(gate_pre)


def _s5_fwd(u, p, w_glu):
    lr, li = p["s5_lambda_re"][0], p["s5_lambda_im"][0]
    ldt = p["s5_log_dt"][0][:, None]
    br = p["s5_b_re"][0].transpose(2, 0, 1)
    bi = p["s5_b_im"][0].transpose(2, 0, 1)
    ar, ai, bbr, bbi = _s5_prep(lr, li, ldt, br, bi)
    a = _pair_columns(ar.reshape(1, S5_LANES), ai.reshape(1, S5_LANES), 1)
    bmat = jnp.concatenate([_block_diag(bbr.transpose(1, 0, 2)), _block_diag(bbi.transpose(1, 0, 2))], axis=2)
    cmat = jnp.concatenate([_block_diag(p["s5_c_re"][0].transpose(0, 2, 1)),
                            -_block_diag(p["s5_c_im"][0].transpose(0, 2, 1))], axis=1)
    bmat, cmat = bmat.astype(MXU_DTYPE), cmat.astype(MXU_DTYPE)
    d = p["s5_d"]
    h, y = _s5_states(u, bmat, cmat, a, d)
    z = _rowmap(jax.nn.gelu, [y], "r", [(y.shape, MXU_DTYPE, "r")], name="s5_gelu", tl=512)
    gate_pre = _mm(z, w_glu, name="s5_glu")
    out = _rowmap(_gelu_glu, [y, gate_pre], "rr", [(y.shape, F32, "r")], name="s5_gate", tl=512)
    return out, (u, lr, li, ldt, br, bi, a, bmat, cmat, h, y, z, gate_pre)


def _s5_bwd(dout, saved, p, w_glu):
    u, lr, li, ldt, br, bi, a, bmat, cmat, h, y, z, gate_pre = saved
    d = p["s5_d"]

    def gate_bwd(dov, yv, gv):
        zv = jax.nn.gelu(yv)
        sg = jax.nn.sigmoid(gv)
        return dov * sg, dov * zv * sg * (1.0 - sg)

    dz_direct, dgate = _rowmap(gate_bwd, [dout, y, gate_pre], "rrr", [(y.shape, F32, "r"), (y.shape, MXU_DTYPE, "r")],
                               name="s5_gate_bwd", tl=512)
    dw_glu = _mm(z, dgate, ta=True, name="s5_dwglu", out_dtype=WIRE_DTYPE)
    dz = _mm(dgate, w_glu, tb=True, name="s5_dz", epilogue=lambda acc, prev: acc + prev, extras=[dz_direct])

    def gelu_bwd(dzv, yv, uv, dvv):
        _, vjp = jax.vjp(jax.nn.gelu, yv)
        dy = vjp(dzv)[0]
        return dy, dy * dvv, jnp.sum(dy * uv, axis=0, keepdims=True)

    dy, du_skip, dd = _rowmap(gelu_bwd, [dz, y, u, d], "rrrc",
                              [(y.shape, F32, "r"), (y.shape, F32, "r"), (d.shape, F32, "a")], name="s5_gelu_bwd", tl=512)
    du, da, dbmat, dcmat = _s5_states_bwd(dy, h, u, bmat, cmat, a, du_skip)
    dbbr, dbbi = (_block_diag_part(t).transpose(1, 0, 2) for t in (dbmat[:, :, :S5_BLOCK], dbmat[:, :, S5_BLOCK:]))
    dar, dai = _unpair_columns(da, 1)
    cts = (dar.reshape(S5_GROUPS, S5_STATE), dai.reshape(S5_GROUPS, S5_STATE), dbbr, dbbi)
    dlr, dli, dldt, dbr, dbi = _s5_prep_bwd(lr, li, ldt, br, bi, cts)
    dcr, dci = (_block_diag_part(t).transpose(0, 2, 1) for t in (dcmat[:, :S5_BLOCK], dcmat[:, S5_BLOCK:]))
    grads = {
        "s5_lambda_re": dlr[None], "s5_lambda_im": dli[None], "s5_log_dt": dldt[:, 0][None],
        "s5_b_re": dbr.transpose(1, 2, 0)[None], "s5_b_im": dbi.transpose(1, 2, 0)[None],
        "s5_c_re": dcr[None], "s5_c_im": -dci[None], "s5_d": dd,
    }
    return du, dw_glu, grads


def _mix0_fwd(x, g, p, full, late):
    (h, proj), _ = _norm_proj(x, g, full[("ab_w_in", 0)], name="mix0_in")
    u = proj[:, :S5_WIDTH]
    job, keys = late.gather_job("sb_fwd") if late else (None, [])
    (o, lsum), got = _attend(proj, job=job)
    full.update(zip(keys, got))
    w_glu, w_out = full[("s5_w_glu", 0)], full[("ab_w_out", 0)]
    y_a, s5_saved = _s5_fwd(u, p, w_glu)
    mix = jnp.concatenate([y_a, o], axis=1).astype(MXU_DTYPE)
    x2 = _mm(mix, w_out, name="mix0_out", epilogue=lambda acc, xv: xv + acc, extras=[x])
    return x2, (x, h, proj, lsum, mix, s5_saved)


def _mix0_bwd(dx2, saved, g, p, full, grads, late):
    x, h, proj, lsum, mix, s5_saved = saved
    w_in, w_glu, w_out = full[("ab_w_in", 0)], full[("s5_w_glu", 0)], full[("ab_w_out", 0)]
    dmix = _mm(dx2, w_out, tb=True, name="mix0_dmix")
    grads[("ab_w_out", 0)] = _mm(mix, dx2, ta=True, name="mix0_dwout", out_dtype=WIRE_DTYPE)
    du, grads[("s5_w_glu", 0)], s5_grads = _s5_bwd(dmix[:, :S5_WIDTH], s5_saved, p, w_glu)
    job, keys = late.scatter_job(grads) if late else (None, [])
    (dq, dk, dv), got = _attend_bwd(proj, lsum, dmix, job=job)
    _note(late, keys, got)
    dproj = jnp.concatenate([du] + [t.astype(MXU_DTYPE) for t in (dq, dk, dv)], axis=1)
    grads[("ab_w_in", 0)] = _mm(h, dproj, ta=True, name="mix0_dwin", out_dtype=WIRE_DTYPE)
    job, keys = late.scatter_job(grads) if late else (None, [])
    (dx, dg), got = _proj_norm_bwd([(dproj, w_in, "DF")], x, g, dx2, name="mix0_dh", job=job)
    _note(late, keys, got)
    return dx, dg, s5_grads


def _shift_down(t, n):
    rows = lax.broadcasted_iota(jnp.int32, t.shape, 0)
    return jnp.where(rows >= n, pltpu.roll(t, n, 0), 0.0)


def _shift_up(t, n):
    rows = lax.broadcasted_iota(jnp.int32, t.shape, 0)
    return jnp.where(rows < t.shape[0] - n, pltpu.roll(t, t.shape[0] - n, 0), 0.0)


def _conv_fwd(proj, cw, *, tc=128):
    seq, c3 = proj.shape
    ch = c3 // 3
    nb = ch // tc

    def body(b_ref, c_ref, v_ref, w_ref, m_ref):
        pv = c_ref[...] * v_ref[...]
        w = w_ref[...]
        y = w[2:3] * pv + w[1:2] * _shift_down(pv, 1) + w[0:1] * _shift_down(pv, 2)
        m_ref[...] = (b_ref[...] * y).astype(m_ref.dtype)

    col = lambda part: pl.BlockSpec((seq, tc), lambda j: (0, part * nb + j))
    return pl.pallas_call(
        body, name="conv_fwd", grid=(nb,),
        in_specs=[col(0), col(1), col(2), pl.BlockSpec((3, tc), lambda j: (0, j))],
        out_specs=pl.BlockSpec((seq, tc), lambda j: (0, j)),
        out_shape=jax.ShapeDtypeStruct((seq, ch), MXU_DTYPE),
        compiler_params=_params("parallel"),
    )(proj, proj, proj, cw)


def _conv_bwd(proj, cw, dm, *, tc=128):
    seq, c3 = proj.shape
    ch = c3 // 3
    nb = ch // tc

    def body(b_ref, c_ref, v_ref, w_ref, dm_ref, dproj_ref, dw_ref, dc_ref, dv_ref):
        part = pl.program_id(1)

        @pl.when(part == 0)
        def _():
            cv, vv, dmv = c_ref[...], v_ref[...], dm_ref[...]
            pv = cv * vv
            w = w_ref[...]
            p1, p2 = _shift_down(pv, 1), _shift_down(pv, 2)
            y = w[2:3] * pv + w[1:2] * p1 + w[0:1] * p2
            dproj_ref[...] = (dmv * y).astype(dproj_ref.dtype)
            dy = dmv * b_ref[...]
            dp = w[2:3] * dy + w[1:2] * _shift_up(dy, 1) + w[0:1] * _shift_up(dy, 2)
            dc_ref[...] = (dp * vv).astype(dc_ref.dtype)
            dv_ref[...] = (dp * cv).astype(dv_ref.dtype)
            dw_ref[...] = jnp.concatenate([jnp.sum(dy * p2, axis=0, keepdims=True), jnp.sum(dy * p1, axis=0, keepdims=True),
                                           jnp.sum(dy * pv, axis=0, keepdims=True)], axis=0)

        @pl.when(part == 1)
        def _():
            dproj_ref[...] = dc_ref[...]

        @pl.when(part == 2)
        def _():
            dproj_ref[...] = dv_ref[...]

    col = lambda part: pl.BlockSpec((seq, tc), lambda j, t: (0, part * nb + j))
    small = pl.BlockSpec((3, tc), lambda j, t: (0, j))
    return pl.pallas_call(
        body, name="conv_bwd", grid=(nb, 3),
        in_specs=[col(0), col(1), col(2), small, pl.BlockSpec((seq, tc), lambda j, t: (0, j))],
        out_specs=[pl.BlockSpec((seq, tc), lambda j, t: (0, t * nb + j)), small],
        out_shape=[jax.ShapeDtypeStruct((seq, c3), MXU_DTYPE), jax.ShapeDtypeStruct((3, ch), F32)],
        scratch_shapes=[pltpu.VMEM((seq, tc), MXU_DTYPE)] * 2,
        compiler_params=_params("parallel", "arbitrary"),
    )(proj, proj, proj, cw, dm)


def _mix1_fwd(x, g, full, late):
    job, keys = late.gather_job("mix1_in") if late else (None, [])
    (h, proj), got = _norm_proj(x, g, full[("sc_w_in", 0)], name="mix1_in", job=job)
    full.update(zip(keys, got))
    m = _conv_fwd(proj, full[("sc_conv_w", 0)])
    x2 = _mm(m, full[("sc_w_out", 0)], name="mix1_out", epilogue=lambda acc, xv: xv + acc, extras=[x])
    return x2, (x, h, proj, m)


def _mix1_bwd(dx2, saved, g, w_in, cw, w_out):
    x, h, proj, m = saved
    dm = _mm(dx2, w_out, tb=True, name="mix1_dm")
    dw_out = _mm(m, dx2, ta=True, name="mix1_dwout", out_dtype=WIRE_DTYPE)
    dproj, dcw = _conv_bwd(proj, cw, dm)
    dw_in = _mm(h, dproj, ta=True, name="mix1_dwin", out_dtype=WIRE_DTYPE)
    (dx, dg), _ = _proj_norm_bwd([(dproj, w_in, "DF")], x, g, dx2, name="mix1_dh")
    return dx, dg, dw_in, dcw, dw_out


def _loss_head(x, g, target):
    feat = x.shape[1]

    def fn(xv, gv, tv):
        err = _rms_fwd(xv, gv) - tv
        dx, dg = _rms_bwd(err / feat, xv, gv)
        return jnp.sum(err * err, keepdims=True) * (0.5 / feat), dx, dg

    return _rowmap(fn, [x, g, target], "rcr", [((1, 1), F32, "a"), (x.shape, F32, "r"), (g.shape, F32, "a")],
                   name="loss_head", tl=256)


def _slot(ref, place, chip=None, half=None):
    axis, width = place
    shape = list(ref.shape)
    start = [0, 0]
    if chip is not None:
        start[axis], shape[axis] = chip * width, width
    if half is not None:
        h_axis = 0 if shape[0] % 32 == 0 else 1
        shape[h_axis] //= 2
        start[h_axis] = start[h_axis] + half * shape[h_axis]
    hint = lambda s, d: s if isinstance(s, int) else pl.multiple_of(s, 128 if d == 1 else 8)
    return ref.at[tuple(pl.ds(hint(s, d), n) for d, (s, n) in enumerate(zip(start, shape)))]


class _Exchange:
    def __init__(self, kind, arrays, places):
        self.kind, self.arrays, self.places, self.n = kind, list(arrays), list(places), len(arrays)
        self.out_shape = []
        for t, (axis, width) in zip(self.arrays, self.places):
            if kind == "gather":
                shape = list(t.shape)
                shape[axis] = N_CHIPS * width
            else:
                shape = [N_CHIPS] + list(t.shape)
                shape[1 + axis] = width
            self.out_shape.append(jax.ShapeDtypeStruct(tuple(shape), t.dtype))
        n = self.n
        self.scratch = [pltpu.SemaphoreType.DMA((3 * n,)) for _ in range(4 if kind == "gather" else 2)]
        self.scratch.append(pltpu.SemaphoreType.DMA((n,)))

    def _copies(self, ins, outs, sems):
        x, y, c = lax.axis_index("x"), lax.axis_index("y"), lax.axis_index("c")
        peers = [(1 - x, y), (x, 1 - y), (1 - x, 1 - y)]
        remote = lambda src, dst, send, recv, k, to: pltpu.make_async_remote_copy(
            src_ref=src, dst_ref=dst, send_sem=send.at[k], recv_sem=recv.at[k], device_id=to, device_id_type=MESH_ID)
        local, ici, d2d = [], [], []
        for a in range(self.n):
            place = self.places[a]
            if self.kind == "gather":
                local.append(pltpu.make_async_copy(ins[a], _slot(outs[a], place, 2 * x + y), sems[4].at[a]))
                for r, (px, py) in enumerate(peers):
                    ici.append(remote(_slot(ins[a], place, None, c), _slot(outs[a], place, 2 * x + y, c),
                                      sems[0], sems[1], 3 * a + r, (px, py, c)))
                    landed = _slot(outs[a], place, 2 * px + py, c)
                    d2d.append(remote(landed, landed, sems[2], sems[3], 3 * a + r, (x, y, 1 - c)))
            else:
                local.append(pltpu.make_async_copy(_slot(ins[a], place, 2 * x + y), outs[a].at[3], sems[2].at[a]))
                for r, (px, py) in enumerate(peers):
                    ici.append(remote(_slot(ins[a], place, 2 * px + py), outs[a].at[r], sems[0], sems[1], 3 * a + r, (px, py, c)))
        return local, ici, d2d

    def start(self, ins, outs, sems):
        local, ici, _ = self._copies(ins, outs, sems)
        for cp in local + ici:
            cp.start()

    def relay(self, ins, outs, sems):
        _, ici, d2d = self._copies(ins, outs, sems)
        for arrived, onward in zip(ici, d2d):
            arrived.wait_recv()
            onward.start()

    def finish(self, ins, outs, sems):
        local, ici, d2d = self._copies(ins, outs, sems)
        for cp in local + d2d:
            cp.wait()
        for cp in ici:
            cp.wait_send() if d2d else cp.wait()


def _exchange_call(job, name):
    n = job.n

    def body(*refs):
        ins, outs, sems = refs[:n], refs[n:2 * n], refs[2 * n:]
        job.start(ins, outs, sems)
        job.relay(ins, outs, sems)
        job.finish(ins, outs, sems)

    return pl.pallas_call(
        body, name=name, in_specs=[ANY_SPEC] * n, out_specs=[ANY_SPEC] * n, out_shape=job.out_shape,
        scratch_shapes=job.scratch, compiler_params=pltpu.CompilerParams(has_side_effects=True),
    )(*job.arrays)


def _carried_call(body, *, name, grid, in_specs, out_specs, out_shape, semantics, operands, scratch_shapes=(), job=None):
    scratch_shapes = list(scratch_shapes)
    if job is None:
        return pl.pallas_call(body, name=name, grid=grid, in_specs=in_specs, out_specs=out_specs, out_shape=out_shape,
                              scratch_shapes=scratch_shapes, compiler_params=_params(*semantics))(*operands), []
    n_in, n_out, n, n_scr = len(in_specs), len(out_specs), job.n, len(scratch_shapes)
    steps = math.prod(grid)

    def wrapped(*refs):
        ins, job_ins = refs[:n_in], refs[n_in:n_in + n]
        outs, job_outs = refs[n_in + n:n_in + n + n_out], refs[n_in + n + n_out:n_in + 2 * n + n_out]
        outs = outs + refs[n_in + 2 * n + n_out:n_in + 2 * n + n_out + n_scr]
        sems = refs[n_in + 2 * n + n_out + n_scr:]
        step = functools.reduce(lambda acc, d: acc * grid[d] + pl.program_id(d), range(len(grid)), 0)

        @pl.when(step == 0)
        def _():
            job.start(job_ins, job_outs, sems)

        @pl.when(step == (3 * steps) // 4)
        def _():
            job.relay(job_ins, job_outs, sems)

        body(*ins, *outs)

        @pl.when(step == steps - 1)
        def _():
            job.finish(job_ins, job_outs, sems)

    res = pl.pallas_call(
        wrapped, name=name, grid=grid, in_specs=list(in_specs) + [ANY_SPEC] * n, out_specs=list(out_specs) + [ANY_SPEC] * n,
        out_shape=list(out_shape) + job.out_shape, scratch_shapes=scratch_shapes + job.scratch,
        compiler_params=pltpu.CompilerParams(dimension_semantics=("arbitrary",) * len(grid), vmem_limit_bytes=VMEM_LIMIT,
                                             has_side_effects=True),
    )(*operands, *job.arrays)
    return res[:n_out], res[n_out:]


def _swap_with_sibling(parts):
    n = len(parts)

    def body(*refs):
        ins, outs = refs[:n], refs[n:2 * n]
        send, recv = refs[2 * n:]
        sibling = (lax.axis_index("x"), lax.axis_index("y"), 1 - lax.axis_index("c"))
        copies = [pltpu.make_async_remote_copy(src_ref=ins[a], dst_ref=outs[a], send_sem=send.at[a], recv_sem=recv.at[a],
                                               device_id=sibling, device_id_type=MESH_ID) for a in range(n)]
        for cp in copies:
            cp.start()
        for cp in copies:
            cp.wait()

    return pl.pallas_call(
        body, name="swap_with_sibling",
        in_specs=[ANY_SPEC] * n, out_specs=[ANY_SPEC] * n,
        out_shape=[jax.ShapeDtypeStruct(p.shape, p.dtype) for p in parts],
        scratch_shapes=[pltpu.SemaphoreType.DMA((n,)), pltpu.SemaphoreType.DMA((n,))],
        compiler_params=pltpu.CompilerParams(has_side_effects=True),
    )(*parts)


def _sum_all_devices(t):
    rows = t.shape[0]

    def body(t_ref, o_ref, slots, send, recv):
        x, y, c = lax.axis_index("x"), lax.axis_index("y"), lax.axis_index("c")
        me = 4 * x + 2 * y + c
        slots[me] = t_ref[...]
        copies = []
        for m in range(1, 8):
            peer = (x ^ (m >> 2), y ^ ((m >> 1) & 1), c ^ (m & 1))
            cp = pltpu.make_async_remote_copy(src_ref=t_ref, dst_ref=slots.at[me], send_sem=send.at[m - 1],
                                              recv_sem=recv.at[m - 1], device_id=peer, device_id_type=MESH_ID)
            cp.start()
            copies.append(cp)
        for cp in copies:
            cp.wait()
        acc = slots[0]
        for dev in range(1, 8):
            acc = acc + slots[dev]
        o_ref[...] = acc

    vmem = pl.BlockSpec(memory_space=pltpu.VMEM)
    return pl.pallas_call(
        body, name="sum_all_devices", in_specs=[vmem], out_specs=vmem,
        out_shape=jax.ShapeDtypeStruct(t.shape, F32),
        scratch_shapes=[pltpu.VMEM((8, rows, 128), F32), pltpu.SemaphoreType.DMA((7,)), pltpu.SemaphoreType.DMA((7,))],
        compiler_params=pltpu.CompilerParams(vmem_limit_bytes=VMEM_LIMIT, has_side_effects=True),
    )(t)


def _adamw(w, g, m, v):
    m = ADAM_B1 * m + (1.0 - ADAM_B1) * g
    v = ADAM_B2 * v + (1.0 - ADAM_B2) * jnp.square(g)
    m_hat = m / (1.0 - ADAM_B1 ** ADAM_STEP)
    v_hat = v / (1.0 - ADAM_B2 ** ADAM_STEP)
    return -ADAM_LR * (m_hat / (jnp.sqrt(v_hat) + ADAM_EPS) + ADAM_WD * w), m, v


def _chip_sum(received, name):
    rows, cols = received.shape[1:]
    tl = _row_block(rows, 512, tile=32 // received.dtype.itemsize)

    def body(r_ref, o_ref):
        total = ((r_ref[0].astype(F32) + r_ref[1].astype(F32)) + r_ref[2].astype(F32)) + r_ref[3].astype(F32)
        o_ref[...] = total.astype(o_ref.dtype)

    return pl.pallas_call(body, name=name, grid=(rows // tl,),
                          in_specs=[pl.BlockSpec((N_CHIPS, tl, cols), lambda i: (0, i, 0))],
                          out_specs=pl.BlockSpec((tl, cols), lambda i: (i, 0)),
                          out_shape=jax.ShapeDtypeStruct((rows, cols), received.dtype),
                          compiler_params=_params("parallel"))(received)


def _adamw_layer(w, m, v, p_mine, p_other, layer, prev, name):
    _, rows, cols = w.shape
    tl = _row_block(rows, 512, tile=32 // p_mine.dtype.itemsize)

    def body(w_ref, m_ref, v_ref, pa_ref, pb_ref, *rest):
        g = pa_ref[:, :cols].astype(F32) + pb_ref[:, :cols].astype(F32)
        for o_ref, val in zip(rest[-4:], (g,) + _adamw(w_ref[...], g, m_ref[...], v_ref[...])):
            o_ref[...] = val

    stacked = pl.BlockSpec((None, tl, cols), lambda i: (layer, i, 0))
    part = pl.BlockSpec((tl, p_mine.shape[1]), lambda i: (i, 0))
    kept = list(prev) if prev else []
    return pl.pallas_call(
        body, name=name, grid=(rows // tl,),
        in_specs=[stacked] * 3 + [part] * 2 + [ANY_SPEC] * len(kept),
        out_specs=[stacked] * 4, out_shape=[jax.ShapeDtypeStruct(w.shape, F32)] * 4,
        input_output_aliases={5 + k: k for k in range(len(kept))},
        compiler_params=_params("parallel"),
    )(w, m, v, p_mine, p_other, *kept)


def _adamw_small(w, g, m, v):
    def fn(wv, gv, mv, vv):
        return _adamw(wv, gv, mv, vv)

    return _rowmap(fn, [w, g, m, v], "rrrr", [(w.shape, F32, "r")] * 3, name="adamw_small", tl=w.shape[0])


WEIGHTS = ['ffn1_norm', 'ffn1_w_gate', 'ffn1_w_up', 'ffn1_w_down', 'mix_norm', 'ffn2_norm', 'ffn2_w_gate', 'ffn2_w_up',
           'ffn2_w_down', 'ab_w_in', 's5_lambda_re', 's5_lambda_im', 's5_log_dt', 's5_b_re', 's5_b_im', 's5_c_re', 's5_c_im',
           's5_d', 's5_w_glu', 'ab_w_out', 'sc_w_in', 'sc_conv_w', 'sc_w_out', 'final_norm']
SHARDED = {'ffn1_w_gate': (0, FF_SLOT), 'ffn1_w_up': (0, FF_SLOT), 'ffn1_w_down': (0, FF_SLOT),
           'ffn2_w_gate': (0, FF_SLOT), 'ffn2_w_up': (0, FF_SLOT), 'ffn2_w_down': (0, FF_SLOT),
           'ab_w_in': (1, 512), 's5_w_glu': (0, 128), 'ab_w_out': (0, 256), 'sc_w_in': (1, 768), 'sc_conv_w': (1, 256),
           'sc_w_out': (0, 256)}
SWAPPED = ('ffn1_w_gate', 'ffn1_w_up', 'ffn2_w_gate', 'ffn2_w_up')
SMALL = [n for n in WEIGHTS if n not in SHARDED]


def _held(name, t):
    return jnp.swapaxes(t, 1, 2) if name in SWAPPED else t


def _pack(arrays):
    rows = []
    for t in arrays:
        flat = t.reshape(-1)
        rows.append(jnp.pad(flat, (0, (-flat.shape[0]) % 128)))
    flat = jnp.concatenate(rows)
    return jnp.pad(flat, (0, (-flat.shape[0]) % 1024)).reshape(-1, 128)


def _unpack(packed, like):
    flat, out, pos = packed.reshape(-1), [], 0
    for t in like:
        out.append(flat[pos:pos + t.size].reshape(t.shape))
        pos += t.size + (-t.size) % 128
    return out


def _local_grads(x, target, p, full, late=None):
    small, grads, saved = {}, {}, []
    for layer in range(2):
        x, s1 = _ffn_fwd(x, p["ffn1_norm"][layer:layer + 1], full, "ffn1", layer, late)
        if layer == 0:
            x, sm = _mix0_fwd(x, p["mix_norm"][0:1], p, full, late)
        else:
            x, sm = _mix1_fwd(x, p["mix_norm"][1:2], full, late)
        x, s2 = _ffn_fwd(x, p["ffn2_norm"][layer:layer + 1], full, "ffn2", layer, late)
        saved.append((s1, sm, s2))
    loss, dx, dg_final = _loss_head(x, p["final_norm"][None], target)
    small["final_norm"] = dg_final[0]
    gains = {n: [None, None] for n in ("ffn1_norm", "mix_norm", "ffn2_norm")}

    def ffn_bwd(which, layer, dx, s):
        dx, dg = _ffn_bwd(dx, s, p[f"{which}_norm"][layer:layer + 1], full, which, layer, grads, late,
                          inline=(which, layer) in (("ffn2", 1), ("ffn1", 0)))
        gains[f"{which}_norm"][layer] = dg[0]
        return dx

    for layer in (1, 0):
        s1, sm, s2 = saved[layer]
        dx = ffn_bwd("ffn2", layer, dx, s2)
        if layer == 0:
            dx, dg, s5_grads = _mix0_bwd(dx, sm, p["mix_norm"][0:1], p, full, grads, late)
            small.update(s5_grads)
        else:
            dx, dg, dw_in, dcw, dw_out = _mix1_bwd(dx, sm, p["mix_norm"][1:2], full[("sc_w_in", 0)], full[("sc_conv_w", 0)],
                                                   full[("sc_w_out", 0)])
            grads.update({("sc_w_in", 0): dw_in, ("sc_conv_w", 0): dcw.astype(WIRE_DTYPE), ("sc_w_out", 0): dw_out})
        gains["mix_norm"][layer] = dg[0]
        dx = ffn_bwd("ffn1", layer, dx, s1)
    small.update({n: jnp.stack(pair) for n, pair in gains.items()})
    return loss, dx, small, grads


_GATHER_PLAN = {
    "gather_early": [("ffn1_w_gate", 0), ("ffn1_w_up", 0)],
    "ffn1_0_up": [("ffn1_w_down", 0), ("ab_w_in", 0)],
    "sb_fwd": [("s5_w_glu", 0), ("ab_w_out", 0), ("ffn2_w_gate", 0), ("ffn2_w_up", 0), ("ffn2_w_down", 0),
               ("ffn1_w_gate", 1), ("ffn1_w_up", 1), ("ffn1_w_down", 1)],
    "ffn2_0_up": [("sc_w_in", 0), ("sc_conv_w", 0), ("sc_w_out", 0)],
    "ffn1_1_up": [("ffn2_w_gate", 1), ("ffn2_w_up", 1)],
    "mix1_in": [("ffn2_w_down", 1)],
}


class _Late:
    def __init__(self, shards, places):
        self.shards, self.places = shards, places
        self.sent, self.received = set(), {}

    def gather_job(self, carrier):
        keys = _GATHER_PLAN.get(carrier, [])
        if not keys:
            return None, []
        return _Exchange("gather", [self.shards[k] for k in keys], [self.places[k] for k in keys]), keys

    def scatter_job(self, grads):
        keys = [k for k in grads if k not in self.sent]
        if not keys:
            return None, []
        self.sent.update(keys)
        return _Exchange("scatter", [grads[k] for k in keys], [self.places[k] for k in keys]), keys


def kernel(x, ffn1_norm, ffn1_w_gate, ffn1_w_up, ffn1_w_down, mix_norm, ffn2_norm, ffn2_w_gate, ffn2_w_up, ffn2_w_down, ab_w_in, s5_lambda_re, s5_lambda_im, s5_log_dt, s5_b_re, s5_b_im, s5_c_re, s5_c_im, s5_d, s5_w_glu, ab_w_out, sc_w_in, sc_conv_w, sc_w_out, final_norm, loss_target, m_ffn1_norm, m_ffn1_w_gate, m_ffn1_w_up, m_ffn1_w_down, m_mix_norm, m_ffn2_norm, m_ffn2_w_gate, m_ffn2_w_up, m_ffn2_w_down, m_ab_w_in, m_s5_lambda_re, m_s5_lambda_im, m_s5_log_dt, m_s5_b_re, m_s5_b_im, m_s5_c_re, m_s5_c_im, m_s5_d, m_s5_w_glu, m_ab_w_out, m_sc_w_in, m_sc_conv_w, m_sc_w_out, m_final_norm, v_ffn1_norm, v_ffn1_w_gate, v_ffn1_w_up, v_ffn1_w_down, v_mix_norm, v_ffn2_norm, v_ffn2_w_gate, v_ffn2_w_up, v_ffn2_w_down, v_ab_w_in, v_s5_lambda_re, v_s5_lambda_im, v_s5_log_dt, v_s5_b_re, v_s5_b_im, v_s5_c_re, v_s5_c_im, v_s5_d, v_s5_w_glu, v_ab_w_out, v_sc_w_in, v_sc_conv_w, v_sc_w_out, v_final_norm):
    args = dict(locals())
    p = {n: _held(n, args[n]) for n in WEIGHTS}
    mom = {n: _held(n, args["m_" + n]) for n in WEIGHTS}
    var = {n: _held(n, args["v_" + n]) for n in WEIGHTS}

    keys = [(n, layer) for n in SHARDED for layer in range(p[n].shape[0])]
    shards, places = {}, {}
    for n, layer in keys:
        axis, width = SHARDED[n]
        t = p[n][layer] if n == "sc_conv_w" else p[n][layer].astype(MXU_DTYPE)
        pad = [(0, 0), (0, 0)]
        pad[axis] = (0, width - t.shape[axis])
        shards[(n, layer)], places[(n, layer)] = jnp.pad(t, pad), (axis, width)
    late = _Late(shards, places)
    job, first = late.gather_job("gather_early")
    full = dict(zip(first, _exchange_call(job, "gather_early")))

    loss, dx, small, grads = _local_grads(x[0], loss_target[0], p, full, late)
    loss = lax.psum(loss[0, 0], ("x", "y", "c"))
    assert set(late.received) == set(keys), "a gradient was left without a carrier"

    partial = [_chip_sum(late.received[(n, layer)], name=f"chip_sum_{n}_{layer}") for n, layer in keys]
    other = _swap_with_sibling(partial)
    out = {}
    for (n, layer), mine, theirs in zip(keys, partial, other):
        out[n] = _adamw_layer(p[n], mom[n], var[n], mine, theirs, layer, out.get(n), name=f"adamw_{n}_{layer}")
    out = {n: [_held(n, t) for t in res] for n, res in out.items()}

    like = [p[n] for n in SMALL]
    g_small = _sum_all_devices(_pack([small[n] for n in SMALL]))
    d_small, m_small, v_small = _adamw_small(_pack(like), g_small, _pack([mom[n] for n in SMALL]), _pack([var[n] for n in SMALL]))
    for k, packed in enumerate((g_small, d_small, m_small, v_small)):
        for n, t in zip(SMALL, _unpack(packed, like)):
            out.setdefault(n, [None] * 4)[k] = t

    return (loss, dx[None], *[out[n][0] for n in WEIGHTS], *[out[n][1] for n in WEIGHTS],
            *[out[n][2] for n in WEIGHTS], *[out[n][3] for n in WEIGHTS])
```

```python
import functools
import math

import jax
import jax.numpy as jnp
from jax import lax
from jax.experimental import pallas as pl
from jax.experimental.pallas import tpu as pltpu

F32 = jnp.float32
MXU_DTYPE = jnp.bfloat16
WIRE_DTYPE = jnp.bfloat16
MESH_ID = pl.DeviceIdType.MESH

D_MODEL = 1024
D_FF = 2752
N_CHIPS = 4
FF_SHARD = D_FF // N_CHIPS
FF_SLOT = 768
FF_PAD = N_CHIPS * FF_SLOT
S5_WIDTH = 512
S5_GROUP = 16
S5_GROUPS = 32
S5_STATE = 64
S5_LANES = S5_GROUPS * S5_STATE
S5_BLOCK = 512
S5_DIAG = S5_LANES // S5_BLOCK
SB_HEADS = 8
SB_DH = 64
SB_SCALE = 0.125
SB_PACK = 2
SB_QUERIES = 1024
SB_KEYS = 256
EPS = 1e-6
ADAM_LR, ADAM_B1, ADAM_B2, ADAM_EPS, ADAM_WD, ADAM_STEP = 0.001, 0.9, 0.999, 1e-08, 0.01, 10
VMEM_LIMIT = 56 * 1024 * 1024

ANY_SPEC = pl.BlockSpec(memory_space=pl.ANY)


def _params(*sem):
    return pltpu.CompilerParams(dimension_semantics=sem or None, vmem_limit_bytes=VMEM_LIMIT)


def _mm(a, b, *, name, ta=False, tb=False, out_dtype=F32, epilogue=None, extras=(), tm=1024, tn=1024, tk=1024, job=None):
    m, k = (a.shape[1], a.shape[0]) if ta else a.shape
    n = b.shape[0] if tb else b.shape[1]
    tm, tn, tk = min(tm, m), min(tn, n), min(tk, k)
    assert m % tm == 0 and n % tn == 0 and k % tk == 0, (name, m, n, k)
    grid = (m // tm, n // tn, k // tk)
    a_spec = pl.BlockSpec((tk, tm), lambda i, j, kk: (kk, i)) if ta else pl.BlockSpec((tm, tk), lambda i, j, kk: (i, kk))
    b_spec = pl.BlockSpec((tn, tk), lambda i, j, kk: (j, kk)) if tb else pl.BlockSpec((tk, tn), lambda i, j, kk: (kk, j))
    nk = grid[2]
    ex_specs = []
    for e in extras:
        if e.shape == (m, n):
            ex_specs.append(pl.BlockSpec((tm, tn), lambda i, j, kk: (i, j)))
        elif e.shape == (1, n):
            ex_specs.append(pl.BlockSpec((1, tn), lambda i, j, kk: (0, j)))
        else:
            assert e.shape == (m, 1), (name, e.shape)
            ex_specs.append(pl.BlockSpec((tm, 1), lambda i, j, kk: (i, 0)))
    dims = (((0 if ta else 1,), (1 if tb else 0,)), ((), ()))
    n_ex = len(extras)

    out_dtypes = list(out_dtype) if isinstance(out_dtype, (list, tuple)) else [out_dtype]
    n_out = len(out_dtypes)

    def body(a_ref, b_ref, *rest):
        ex_refs, o_refs = rest[:n_ex], rest[n_ex:n_ex + n_out]

        def product():
            return lax.dot_general(a_ref[...].astype(MXU_DTYPE), b_ref[...].astype(MXU_DTYPE), dims, preferred_element_type=F32)

        def finish(r):
            if epilogue is not None:
                r = epilogue(r, *[e[...] for e in ex_refs])
            for o_ref, val in zip(o_refs, r if isinstance(r, (tuple, list)) else (r,)):
                o_ref[...] = val.astype(o_ref.dtype)

        if nk == 1:
            finish(product())
            return
        acc_ref, kk = rest[n_ex + n_out], pl.program_id(2)

        @pl.when(kk == 0)
        def _():
            acc_ref[...] = jnp.zeros_like(acc_ref)

        acc_ref[...] += product()

        @pl.when(kk == nk - 1)
        def _():
            finish(acc_ref[...])

    res, got = _carried_call(
        body, name=name, grid=grid,
        in_specs=[a_spec, b_spec, *ex_specs],
        out_specs=[pl.BlockSpec((tm, tn), lambda i, j, kk: (i, j))] * n_out,
        out_shape=[jax.ShapeDtypeStruct((m, n), dt) for dt in out_dtypes],
        scratch_shapes=[pltpu.VMEM((tm, tn), F32)] if nk > 1 else [],
        semantics=("parallel", "parallel", "arbitrary"), operands=(a, b, *extras), job=job)
    res = res if isinstance(out_dtype, (list, tuple)) else res[0]
    return res if job is None else (res, got)


def _row_block(rows, want, tile=8):
    for tl in range(min(want, rows), tile - 1, -1):
        if rows % tl == 0 and tl % tile == 0:
            return tl
    return rows


def _rowmap(fn, ins, in_kinds, outs, *, name, tl):
    rows = next(x.shape[0] for x, kd in zip(ins, in_kinds) if kd == "r")
    tl = _row_block(rows, tl)
    n_in = len(ins)

    def spec(shape, kind):
        if kind == "r":
            return pl.BlockSpec((tl,) + tuple(shape[1:]), lambda i: (i,) + (0,) * (len(shape) - 1))
        return pl.BlockSpec(tuple(shape), lambda i: (0,) * len(shape))

    def body(*refs):
        in_refs, out_refs = refs[:n_in], refs[n_in:]
        res = fn(*[r[...] for r in in_refs])
        if not isinstance(res, (tuple, list)):
            res = (res,)
        for o_ref, val, (_, dt, kind) in zip(out_refs, res, outs):
            if kind == "r":
                o_ref[...] = val.astype(dt)
            else:
                @pl.when(pl.program_id(0) == 0)
                def _():
                    o_ref[...] = jnp.zeros_like(o_ref)

                o_ref[...] += val.astype(dt)

    has_acc = any(kd == "a" for _, _, kd in outs)
    res = pl.pallas_call(
        body, name=name, grid=(rows // tl,),
        in_specs=[spec(x.shape, kd) for x, kd in zip(ins, in_kinds)],
        out_specs=[spec(s, kd) for s, _, kd in outs],
        out_shape=[jax.ShapeDtypeStruct(s, dt) for s, dt, _ in outs],
        compiler_params=_params("arbitrary" if has_acc else "parallel"),
    )(*ins)
    return res[0] if len(outs) == 1 else res


def _rms_fwd(x, g):
    r = lax.rsqrt(jnp.mean(x * x, axis=-1, keepdims=True) + EPS)
    return x * r * g


def _rms_bwd(dh, x, g):
    r = lax.rsqrt(jnp.mean(x * x, axis=-1, keepdims=True) + EPS)
    xh = x * r
    dxh = dh * g
    dx = r * (dxh - xh * jnp.mean(dxh * xh, axis=-1, keepdims=True))
    return dx, jnp.sum(dh * xh, axis=0, keepdims=True)


def _swiglu_act(a, b):
    return jax.nn.silu(a) * b


def _ffn_up(x, g, wg, wu, *, name, tm=1024, tn=1024, job=None):
    m, d = x.shape
    n = wg.shape[0]
    tm, tn = min(tm, m), min(tn, n)
    assert m % tm == 0 and n % tn == 0, (name, m, n)

    def body(x_ref, g_ref, wg_ref, wu_ref, h_ref, a_ref, b_ref, s_ref):
        @pl.when(pl.program_id(1) == 0)
        def _():
            h_ref[...] = _rms_fwd(x_ref[...], g_ref[...]).astype(h_ref.dtype)

        hv = h_ref[...]
        av = lax.dot_general(hv, wg_ref[...], NT_DIMS, preferred_element_type=F32)
        bv = lax.dot_general(hv, wu_ref[...], NT_DIMS, preferred_element_type=F32)
        a_ref[...] = av.astype(a_ref.dtype)
        b_ref[...] = bv.astype(b_ref.dtype)
        s_ref[...] = _swiglu_act(av, bv).astype(s_ref.dtype)

    rows = pl.BlockSpec((tm, d), lambda i, j: (i, 0))
    wgt = pl.BlockSpec((tn, d), lambda i, j: (j, 0))
    tile = pl.BlockSpec((tm, tn), lambda i, j: (i, j))
    return _carried_call(
        body, name=name, grid=(m // tm, n // tn),
        in_specs=[rows, pl.BlockSpec((1, d), lambda i, j: (0, 0)), wgt, wgt],
        out_specs=[rows, tile, tile, tile],
        out_shape=[jax.ShapeDtypeStruct((m, d), MXU_DTYPE)] + [jax.ShapeDtypeStruct((m, n), MXU_DTYPE)] * 3,
        semantics=("parallel", "arbitrary"), operands=(x, g, wg, wu), job=job)


def _norm_proj(x, g, w, *, name, tm=1024, tn=1024, job=None):
    m, d = x.shape
    n = w.shape[1]
    tm, tn = min(tm, m), min(tn, n)
    assert m % tm == 0 and n % tn == 0, (name, m, n)

    def body(x_ref, g_ref, w_ref, h_ref, o_ref):
        @pl.when(pl.program_id(1) == 0)
        def _():
            h_ref[...] = _rms_fwd(x_ref[...], g_ref[...]).astype(h_ref.dtype)

        o_ref[...] = jnp.dot(h_ref[...], w_ref[...], preferred_element_type=F32)

    rows = pl.BlockSpec((tm, d), lambda i, j: (i, 0))
    return _carried_call(
        body, name=name, grid=(m // tm, n // tn),
        in_specs=[rows, pl.BlockSpec((1, d), lambda i, j: (0, 0)), pl.BlockSpec((d, tn), lambda i, j: (0, j))],
        out_specs=[rows, pl.BlockSpec((tm, tn), lambda i, j: (i, j))],
        out_shape=[jax.ShapeDtypeStruct((m, d), MXU_DTYPE), jax.ShapeDtypeStruct((m, n), F32)],
        semantics=("parallel", "arbitrary"), operands=(x, g, w), job=job)


def _proj_norm_bwd(pairs, x, g, dres, *, name, tm=1024, tk=1024, job=None):
    m, f = pairs[0][0].shape
    d = x.shape[1]
    tm, tk = min(tm, m), min(tk, f)
    assert m % tm == 0 and f % tk == 0, (name, m, f)
    nk, n_pairs = f // tk, len(pairs)
    swapped = [kept == "FD" for _, _, kept in pairs]

    def body(*refs):
        dy_refs, w_refs = refs[:n_pairs], refs[n_pairs:2 * n_pairs]
        x_ref, g_ref, dr_ref, dx_ref, dg_ref, acc_ref = refs[2 * n_pairs:]
        i, kk = pl.program_id(0), pl.program_id(1)

        @pl.when(kk == 0)
        def _():
            acc_ref[...] = jnp.zeros_like(acc_ref)

        for dy_ref, w_ref, rows_are_f in zip(dy_refs, w_refs, swapped):
            dims = (((1,), (0,)), ((), ())) if rows_are_f else NT_DIMS
            acc_ref[...] += lax.dot_general(dy_ref[...].astype(MXU_DTYPE), w_ref[...], dims, preferred_element_type=F32)

        @pl.when(jnp.logical_and(i == 0, kk == 0))
        def _():
            dg_ref[...] = jnp.zeros_like(dg_ref)

        @pl.when(kk == nk - 1)
        def _():
            dx, dg = _rms_bwd(acc_ref[...], x_ref[...], g_ref[...])
            dx_ref[...] = dx + dr_ref[...]
            dg_ref[...] += dg

    act = pl.BlockSpec((tm, tk), lambda i, kk: (i, kk))
    w_specs = [pl.BlockSpec((tk, d), lambda i, kk: (kk, 0)) if s else pl.BlockSpec((d, tk), lambda i, kk: (0, kk)) for s in swapped]
    rows = pl.BlockSpec((tm, d), lambda i, kk: (i, 0))
    one = pl.BlockSpec((1, d), lambda i, kk: (0, 0))
    return _carried_call(
        body, name=name, grid=(m // tm, nk),
        in_specs=[act] * n_pairs + w_specs + [rows, one, rows],
        out_specs=[rows, one],
        out_shape=[jax.ShapeDtypeStruct((m, d), F32), jax.ShapeDtypeStruct((1, d), F32)],
        scratch_shapes=[pltpu.VMEM((tm, d), F32)],
        semantics=("arbitrary", "arbitrary"), operands=(*[p[0] for p in pairs], *[p[1] for p in pairs], x, g, dres), job=job)


def _ffn_dx(da, db, wg, wu, x, g, dres, *, name, job=None):
    return _proj_norm_bwd([(da, wg, "FD"), (db, wu, "FD")], x, g, dres, name=name, job=job)


def _ffn_fwd(x, g, full, which, layer, late):
    tag = f"{which}_{layer}"
    job, keys = late.gather_job(f"{tag}_up") if late else (None, [])
    (h, a, b, s), got = _ffn_up(x, g, full[(f"{which}_w_gate", layer)], full[(f"{which}_w_up", layer)], name=f"{tag}_up", job=job)
    full.update(zip(keys, got))
    x2 = _mm(s, full[(f"{which}_w_down", layer)], name=f"{tag}_down", epilogue=lambda acc, xv: xv + 0.5 * acc, extras=[x],
             tk=FF_PAD)
    return x2, (x, h, a, b, s)


def _ffn_bwd(dx2, saved, g, full, which, layer, grads, late, inline):
    x, h, a, b, s = saved
    tag = f"{which}_{layer}"
    kg, ku, kd = [(f"{which}_w_{n}", layer) for n in ("gate", "up", "down")]
    wg, wu, wd = full[kg], full[ku], full[kd]
    send = (lambda: late.scatter_job(grads)) if (late and inline) else (lambda: (None, []))

    def act_bwd(ds, av, bv):
        _, vjp = jax.vjp(_swiglu_act, av.astype(F32), bv.astype(F32))
        return vjp(0.5 * ds)

    grads[kd] = _mm(s, dx2, ta=True, name=f"{tag}_dwd", out_dtype=WIRE_DTYPE, epilogue=lambda acc: 0.5 * acc, tk=2048)
    job, keys = send()
    (da, db), got = _carried(_mm, dx2, wd, tb=True, name=f"{tag}_dact", epilogue=act_bwd, extras=[a, b],
                             out_dtype=[MXU_DTYPE, MXU_DTYPE], job=job)
    _note(late, keys, got)
    grads[kg] = _mm(da, h, ta=True, name=f"{tag}_dwg", out_dtype=WIRE_DTYPE, tk=2048)
    job, keys = send()
    grads[ku], got = _carried(_mm, db, h, ta=True, name=f"{tag}_dwu", out_dtype=WIRE_DTYPE, tk=2048, job=job)
    _note(late, keys, got)
    job, keys = send()
    (dx, dg), got = _ffn_dx(da, db, wg, wu, x, g, dx2, name=f"{tag}_dx", job=job)
    _note(late, keys, got)
    return dx, dg


def _carried(fn, *args, job, **kwargs):
    return fn(*args, job=job, **kwargs) if job is not None else (fn(*args, **kwargs), [])


def _note(late, keys, got):
    if late:
        late.received.update(zip(keys, got))


def _softplus(z):
    return jnp.maximum(z, 0.0) + jnp.log(1.0 + jnp.exp(-jnp.abs(z)))


def _ones_dot(x, tri):
    if MXU_DTYPE == F32:
        return jnp.dot(x, tri, preferred_element_type=F32)
    hi = x.astype(MXU_DTYPE)
    lo = (x - hi.astype(F32)).astype(MXU_DTYPE)
    return jnp.dot(hi, tri, preferred_element_type=F32) + jnp.dot(lo, tri, preferred_element_type=F32)


NT_DIMS = (((1,), (1,)), ((), ()))
TN_DIMS = (((0,), (0,)), ((), ()))


SB_LANES = SB_PACK * SB_DH
Q_COL, K_COL, V_COL = (S5_WIDTH * n // SB_LANES for n in (1, 2, 3))


def _head_lanes(rows, hd):
    return lax.broadcasted_iota(jnp.int32, (rows, SB_LANES), 1) // SB_DH == hd


def _attend(proj, *, tq=SB_QUERIES, job=None):
    seq = proj.shape[0]
    tq = min(tq, seq)
    tk = min(SB_KEYS, tq)
    per, hp = tq // tk, SB_PACK

    def body(q_ref, k_ref, v_ref, o_ref, ls_ref):
        i = pl.program_id(1)
        r_idx = lax.broadcasted_iota(jnp.int32, (tk, tk), 0)
        c_idx = lax.broadcasted_iota(jnp.int32, (tk, tk), 1)
        after = (r_idx > c_idx).astype(MXU_DTYPE)
        lanes = [_head_lanes(tk, hd) for hd in range(hp)]

        def block(j, cs, acc, straddles):
            off = pl.multiple_of(j * tk, tk)
            k2, v2 = k_ref[pl.ds(off, tk), :], v_ref[pl.ds(off, tk), :]
            top = 0 if straddles is None else straddles * tk
            rows = tq - top
            q2 = (q_ref[pl.ds(top, rows), :] * SB_SCALE).astype(MXU_DTYPE)
            new_cs, out = [], acc[top:]
            for hd in range(hp):
                kv = jnp.where(lanes[hd], k2, 0.0).astype(MXU_DTYPE)
                vv = jnp.where(lanes[hd], v2, 0.0).astype(MXU_DTYPE)
                z = lax.dot_general(q2, kv, NT_DIMS, preferred_element_type=F32)
                sp = _softplus(z)
                c_in = cs[hd][top:]
                if straddles is None:
                    lk = -sp
                    w = jnp.exp(z - sp + _ones_dot(lk, after) + c_in)
                else:
                    before = lax.broadcasted_iota(jnp.int32, (rows, tk), 1) < lax.broadcasted_iota(jnp.int32, (rows, tk), 0)
                    lk = jnp.where(before, -sp, 0.0)
                    w = jnp.where(before, jnp.exp(z - sp + _ones_dot(lk, after) + c_in), 0.0)
                out = out + jnp.dot(w.astype(MXU_DTYPE), vv, preferred_element_type=F32)
                c_new = c_in + jnp.sum(lk, axis=1, keepdims=True)
                new_cs.append(jnp.concatenate([cs[hd][:top], c_new], axis=0) if top else c_new)
            return tuple(new_cs), (jnp.concatenate([acc[:top], out], axis=0) if top else out)

        carry = (tuple(jnp.zeros((tq, 1), F32) for _ in range(hp)), jnp.zeros((tq, SB_LANES), F32))
        for s in reversed(range(per)):
            carry = block(i * per + s, *carry, s)
        cs, acc = lax.fori_loop(0, i * per, lambda n, cr: block(i * per - 1 - n, *cr, None), carry)
        o_ref[...] = acc
        for hd in range(hp):
            ls_ref[hd] = cs[hd]

    whole = lambda col: pl.BlockSpec((seq, SB_LANES), lambda g, i: (0, col + g))
    return _carried_call(
        body, name="sb_fwd", grid=(SB_HEADS // hp, seq // tq),
        in_specs=[pl.BlockSpec((tq, SB_LANES), lambda g, i: (i, Q_COL + g)), whole(K_COL), whole(V_COL)],
        out_specs=[pl.BlockSpec((tq, SB_LANES), lambda g, i: (i, g)), pl.BlockSpec((hp, tq, 1), lambda g, i: (g, i, 0))],
        out_shape=[jax.ShapeDtypeStruct((seq, SB_HEADS * SB_DH), F32), jax.ShapeDtypeStruct((SB_HEADS, seq, 1), F32)],
        semantics=("parallel", "parallel"), operands=(proj, proj, proj), job=job)


def _attend_bwd(proj, lsum, dmix, *, tq=SB_QUERIES, job=None):
    seq = proj.shape[0]
    tq = min(tq, seq)
    tk = min(SB_KEYS, tq)
    per, hp = tq // tk, SB_PACK
    do_col = S5_WIDTH // SB_LANES

    def body(q_ref, k_ref, v_ref, ls_ref, do_ref, dq_ref, dk_ref, dv_ref):
        i = pl.program_id(1)

        @pl.when(i == 0)
        def _():
            dk_ref[...] = jnp.zeros_like(dk_ref)
            dv_ref[...] = jnp.zeros_like(dv_ref)

        r_idx = lax.broadcasted_iota(jnp.int32, (tk, tk), 0)
        c_idx = lax.broadcasted_iota(jnp.int32, (tk, tk), 1)
        upto = (r_idx <= c_idx).astype(MXU_DTYPE)
        before = (r_idx < c_idx).astype(MXU_DTYPE)
        lanes = [_head_lanes(tk, hd) for hd in range(hp)]

        def block(j, sums, dq, straddles):
            off = pl.multiple_of(j * tk, tk)
            k2, v2 = k_ref[pl.ds(off, tk), :], v_ref[pl.ds(off, tk), :]
            top = 0 if straddles is None else straddles * tk
            rows = tq - top
            part = pl.ds(top, rows)
            q2 = (q_ref[part, :] * SB_SCALE).astype(MXU_DTYPE)
            do2 = do_ref[part, :].astype(MXU_DTYPE)
            valid = None
            if straddles is not None:
                valid = lax.broadcasted_iota(jnp.int32, (rows, tk), 1) < lax.broadcasted_iota(jnp.int32, (rows, tk), 0)
            new_sums, out, dk, dv = [], dq[top:], jnp.zeros((tk, SB_LANES), F32), jnp.zeros((tk, SB_LANES), F32)
            for hd in range(hp):
                cp, ce = sums[hd]
                kv = jnp.where(lanes[hd], k2, 0.0).astype(MXU_DTYPE)
                vv = jnp.where(lanes[hd], v2, 0.0).astype(MXU_DTYPE)
                z = lax.dot_general(q2, kv, NT_DIMS, preferred_element_type=F32)
                sp = _softplus(z)
                lk = -sp if valid is None else jnp.where(valid, -sp, 0.0)
                w = jnp.exp(z - sp + (ls_ref[hd, part, :] - cp[top:]) - _ones_dot(lk, upto))
                if valid is not None:
                    w = jnp.where(valid, w, 0.0)
                e = w * lax.dot_general(do2, vv, NT_DIMS, preferred_element_type=F32)
                earlier = jnp.dot(e.astype(MXU_DTYPE), before, preferred_element_type=F32) + ce[top:]
                keep = jnp.exp(-sp)
                dz = e * keep - (1.0 - keep) * earlier
                if valid is not None:
                    dz = jnp.where(valid, dz, 0.0)
                dzm = dz.astype(MXU_DTYPE)
                out = out + jnp.dot(dzm, kv, preferred_element_type=F32)
                dk = dk + jnp.where(lanes[hd], lax.dot_general(dzm, q2, TN_DIMS, preferred_element_type=F32), 0.0)
                dv = dv + jnp.where(lanes[hd], lax.dot_general(w.astype(MXU_DTYPE), do2, TN_DIMS, preferred_element_type=F32), 0.0)
                new = (cp[top:] + jnp.sum(lk, axis=1, keepdims=True), ce[top:] + jnp.sum(e, axis=1, keepdims=True))
                new_sums.append(tuple(jnp.concatenate([old[:top], val], axis=0) for old, val in zip((cp, ce), new)) if top else new)
            dk_ref[pl.ds(off, tk), :] += dk
            dv_ref[pl.ds(off, tk), :] += dv
            return tuple(new_sums), (jnp.concatenate([dq[:top], out], axis=0) if top else out)

        zero = jnp.zeros((tq, 1), F32)
        carry = (tuple((zero, zero) for _ in range(hp)), jnp.zeros((tq, SB_LANES), F32))
        carry = lax.fori_loop(0, i * per, lambda j, cr: block(j, *cr, None), carry)
        for s in range(per):
            carry = block(i * per + s, *carry, s)
        dq_ref[...] = carry[1] * SB_SCALE

    whole = lambda col: pl.BlockSpec((seq, SB_LANES), lambda g, i: (0, col + g))
    tile = lambda col: pl.BlockSpec((tq, SB_LANES), lambda g, i: (i, col + g))
    acc = pl.BlockSpec((seq, SB_LANES), lambda g, i: (0, g))
    return _carried_call(
        body, name="sb_bwd", grid=(SB_HEADS // hp, seq // tq),
        in_specs=[tile(Q_COL), whole(K_COL), whole(V_COL), pl.BlockSpec((hp, tq, 1), lambda g, i: (g, i, 0)), tile(do_col)],
        out_specs=[tile(0), acc, acc],
        out_shape=[jax.ShapeDtypeStruct((seq, SB_HEADS * SB_DH), F32)] * 3,
        semantics=("parallel", "arbitrary"), operands=(proj, proj, proj, lsum, dmix), job=job)


def _s5_disc(lr, li, ldt, br, bi):
    dt = jnp.exp(ldt)
    mag = jnp.exp(lr * dt)
    ar = mag * jnp.cos(li * dt)
    ai = mag * jnp.sin(li * dt)
    den = lr * lr + li * li
    nr = ar - 1.0
    cr = (nr * lr + ai * li) / den
    ci = (ai * lr - nr * li) / den
    return ar, ai, cr[None] * br - ci[None] * bi, cr[None] * bi + ci[None] * br


def _s5_prep(lr, li, ldt, br, bi):
    shapes = [lr.shape, lr.shape, br.shape, br.shape]

    def body(lr_ref, li_ref, ldt_ref, br_ref, bi_ref, *outs):
        for o, val in zip(outs, _s5_disc(lr_ref[...], li_ref[...], ldt_ref[...], br_ref[...], bi_ref[...])):
            o[...] = val

    return pl.pallas_call(body, name="s5_prep", out_shape=[jax.ShapeDtypeStruct(s, F32) for s in shapes])(lr, li, ldt, br, bi)


def _s5_prep_bwd(lr, li, ldt, br, bi, cts):
    args = (lr, li, ldt, br, bi)

    def body(*refs):
        ins, ct_refs, outs = refs[:5], refs[5:9], refs[9:]
        _, vjp = jax.vjp(_s5_disc, *[r[...] for r in ins])
        for o, val in zip(outs, vjp(tuple(r[...] for r in ct_refs))):
            o[...] = val

    return pl.pallas_call(body, name="s5_prep_bwd", out_shape=[jax.ShapeDtypeStruct(a.shape, F32) for a in args])(*args, *cts)


SCAN_ROWS = 8


def _powers(ar, ai):
    out = [(ar, ai)]
    for _ in range(SCAN_ROWS - 1):
        pr, pi = out[-1]
        out.append((pr * ar - pi * ai, pr * ai + pi * ar))
    return out


def _s5_states(u, bmat, cmat, a, d, *, tc=512):
    seq, width = u.shape
    nj, cols, w2 = bmat.shape
    tw = w2 // 2
    tc = min(tc, seq)
    assert seq % tc == 0 and nj * cols == width and tw == S5_BLOCK

    def body(u_ref, bm_ref, cm_ref, a_ref, d_ref, h_ref, y_ref, cr_ref, ci_ref):
        @pl.when(pl.program_id(1) == 0)
        def _():
            cr_ref[...] = jnp.zeros_like(cr_ref)
            ci_ref[...] = jnp.zeros_like(ci_ref)

        uv = u_ref[...]
        h_ref[...] = jnp.dot(uv.astype(MXU_DTYPE), bm_ref[0], preferred_element_type=F32)
        re, im = pl.ds(0, tw), pl.ds(tw, tw)
        powers = _powers(a_ref[:, re], a_ref[:, im])
        pr = jnp.concatenate([p[0] for p in powers], axis=0)
        pi = jnp.concatenate([p[1] for p in powers], axis=0)
        row_id = lax.broadcasted_iota(jnp.int32, (SCAN_ROWS, tw), 0)
        reach = {dist: tuple(jnp.where(row_id >= dist, part, 0.0) for part in powers[dist - 1]) for dist in (1, 2, 4)}

        def block(n, carry):
            hr, hi = carry
            rows = pl.ds(pl.multiple_of(n * SCAN_ROWS, SCAN_ROWS), SCAN_ROWS)
            yr, yi = h_ref[rows, re], h_ref[rows, im]
            for dist in (1, 2, 4):
                cr, ci = reach[dist]
                sr, si = pltpu.roll(yr, dist, 0), pltpu.roll(yi, dist, 0)
                yr, yi = yr + cr * sr - ci * si, yi + cr * si + ci * sr
            yr, yi = yr + pr * hr - pi * hi, yi + pr * hi + pi * hr
            h_ref[rows, re] = yr
            h_ref[rows, im] = yi
            return yr[SCAN_ROWS - 1:], yi[SCAN_ROWS - 1:]

        hr, hi = lax.fori_loop(0, tc // SCAN_ROWS, block, (cr_ref[...], ci_ref[...]), unroll=4)
        cr_ref[...] = hr
        ci_ref[...] = hi
        y_ref[...] = jnp.dot(h_ref[...].astype(MXU_DTYPE), cm_ref[0], preferred_element_type=F32) + d_ref[...] * uv

    io = pl.BlockSpec((tc, cols), lambda j, t: (t, j))
    return pl.pallas_call(
        body, name="s5_states", grid=(nj, seq // tc),
        in_specs=[io, pl.BlockSpec((1, cols, w2), lambda j, t: (j, 0, 0)), pl.BlockSpec((1, w2, cols), lambda j, t: (j, 0, 0)),
                  pl.BlockSpec((1, w2), lambda j, t: (0, j)), pl.BlockSpec((1, cols), lambda j, t: (0, j))],
        out_specs=[pl.BlockSpec((tc, w2), lambda j, t: (t, j)), io],
        out_shape=[jax.ShapeDtypeStruct((seq, nj * w2), F32), jax.ShapeDtypeStruct((seq, width), F32)],
        scratch_shapes=[pltpu.VMEM((1, tw), F32)] * 2,
        compiler_params=_params("parallel", "arbitrary"),
    )(u, bmat, cmat, a, d)


def _s5_states_bwd(dy, h, u, bmat, cmat, a, du_skip, *, tc=512):
    seq, width = u.shape
    nj, cols, w2 = bmat.shape
    tw = w2 // 2
    tc = min(tc, seq)
    assert seq % tc == 0
    nt = seq // tc

    def body(dy_ref, h_ref, u_ref, bm_ref, cm_ref, a_ref, sk_ref, du_ref, da_ref, db_ref, dc_ref, g_ref, cr_ref, ci_ref):
        @pl.when(pl.program_id(1) == 0)
        def _():
            cr_ref[...] = jnp.zeros_like(cr_ref)
            ci_ref[...] = jnp.zeros_like(ci_ref)
            da_ref[...] = jnp.zeros_like(da_ref)
            db_ref[...] = jnp.zeros_like(db_ref)
            dc_ref[...] = jnp.zeros_like(dc_ref)

        dyv = dy_ref[...].astype(MXU_DTYPE)
        g_ref[...] = lax.dot_general(dyv, cm_ref[0], NT_DIMS, preferred_element_type=F32)
        re, im = pl.ds(0, tw), pl.ds(tw, tw)
        powers = _powers(a_ref[:, re], a_ref[:, im])
        pr = jnp.concatenate([p[0] for p in reversed(powers)], axis=0)
        pi = jnp.concatenate([p[1] for p in reversed(powers)], axis=0)
        row_id = lax.broadcasted_iota(jnp.int32, (SCAN_ROWS, tw), 0)
        last = SCAN_ROWS - 1
        reach = {dist: tuple(jnp.where(row_id < SCAN_ROWS - dist, part, 0.0) for part in powers[dist - 1]) for dist in (1, 2, 4)}

        def block(n, carry):
            gr, gi, sr, si = carry
            rows = pl.ds(pl.multiple_of((tc // SCAN_ROWS - 1 - n) * SCAN_ROWS, SCAN_ROWS), SCAN_ROWS)
            yr, yi = g_ref[rows, re], g_ref[rows, im]
            for dist in (1, 2, 4):
                cr, ci = reach[dist]
                ur, ui = pltpu.roll(yr, SCAN_ROWS - dist, 0), pltpu.roll(yi, SCAN_ROWS - dist, 0)
                yr, yi = yr + cr * ur + ci * ui, yi + cr * ui - ci * ur
            yr, yi = yr + pr * gr + pi * gi, yi + pr * gi - pi * gr
            g_ref[rows, re] = yr
            g_ref[rows, im] = yi
            nr = jnp.where(row_id < last, pltpu.roll(yr, last, 0), gr)
            ni = jnp.where(row_id < last, pltpu.roll(yi, last, 0), gi)
            hr, hi = h_ref[rows, re], h_ref[rows, im]
            return yr[:1], yi[:1], sr + nr * hr + ni * hi, si + ni * hr - nr * hi

        zero = jnp.zeros((SCAN_ROWS, tw), F32)
        gr, gi, sr, si = lax.fori_loop(0, tc // SCAN_ROWS, block, (cr_ref[...], ci_ref[...], zero, zero), unroll=4)
        cr_ref[...] = gr
        ci_ref[...] = gi
        da_ref[:, re] += jnp.sum(sr, axis=0, keepdims=True)
        da_ref[:, im] += jnp.sum(si, axis=0, keepdims=True)
        gv = g_ref[...].astype(MXU_DTYPE)
        du_ref[...] = (lax.dot_general(gv, bm_ref[0], NT_DIMS, preferred_element_type=F32) + sk_ref[...]).astype(du_ref.dtype)
        db_ref[0] += lax.dot_general(u_ref[...].astype(MXU_DTYPE), gv, TN_DIMS, preferred_element_type=F32)
        dc_ref[0] += lax.dot_general(h_ref[...].astype(MXU_DTYPE), dyv, TN_DIMS, preferred_element_type=F32)

    io = pl.BlockSpec((tc, cols), lambda j, t: (nt - 1 - t, j))
    bm = pl.BlockSpec((1, cols, w2), lambda j, t: (j, 0, 0))
    cm = pl.BlockSpec((1, w2, cols), lambda j, t: (j, 0, 0))
    row = pl.BlockSpec((1, w2), lambda j, t: (0, j))
    return pl.pallas_call(
        body, name="s5_states_bwd", grid=(nj, nt),
        in_specs=[io, pl.BlockSpec((tc, w2), lambda j, t: (nt - 1 - t, j)), io, bm, cm, row, io],
        out_specs=[io, row, bm, cm],
        out_shape=[jax.ShapeDtypeStruct((seq, width), MXU_DTYPE), jax.ShapeDtypeStruct((1, nj * w2), F32),
                   jax.ShapeDtypeStruct(bmat.shape, F32), jax.ShapeDtypeStruct(cmat.shape, F32)],
        scratch_shapes=[pltpu.VMEM((tc, w2), F32)] + [pltpu.VMEM((1, tw), F32)] * 2,
        compiler_params=_params("parallel", "arbitrary"),
    )(dy, h, u, bmat, cmat, a, du_skip)


def _pair_columns(re, im, axis):
    shape = re.shape
    split = shape[:axis] + (shape[axis] // S5_BLOCK, S5_BLOCK) + shape[axis + 1:]
    both = jnp.stack([re.reshape(split), im.reshape(split)], axis=axis + 1)
    return both.reshape(shape[:axis] + (2 * shape[axis],) + shape[axis + 1:])


def _unpair_columns(t, axis):
    shape = t.shape
    both = t.reshape(shape[:axis] + (shape[axis] // (2 * S5_BLOCK), 2, S5_BLOCK) + shape[axis + 1:])
    half = shape[:axis] + (shape[axis] // 2,) + shape[axis + 1:]
    return (lax.index_in_dim(both, 0, axis + 1, keepdims=False).reshape(half),
            lax.index_in_dim(both, 1, axis + 1, keepdims=False).reshape(half))


S5_PER_BLOCK = S5_GROUPS // S5_DIAG


def _block_diag(t):
    g, a, b = t.shape
    n = S5_PER_BLOCK
    eye = jnp.eye(n, dtype=t.dtype)
    return (t.reshape(g // n, n, a, 1, b) * eye[None, :, None, :, None]).reshape(g // n, n * a, n * b)


def _block_diag_part(m):
    j, n = m.shape[0], S5_PER_BLOCK
    a, b = m.shape[1] // n, m.shape[2] // n
    return jnp.moveaxis(jnp.diagonal(m.reshape(j, n, a, n, b), axis1=1, axis2=3), -1, 1).reshape(j * n, a, b)


def _gelu_glu(y, gate_pre):
    z = jax.nn.gelu(y)
    return z * jax.nn.sigmoid(gate_pre)


def _s5_fwd(u, p, w_glu):
    lr, li = p["s5_lambda_re"][0], p["s5_lambda_im"][0]
    ldt = p["s5_log_dt"][0][:, None]
    br = p["s5_b_re"][0].transpose(2, 0, 1)
    bi = p["s5_b_im"][0].transpose(2, 0, 1)
    ar, ai, bbr, bbi = _s5_prep(lr, li, ldt, br, bi)
    a = _pair_columns(ar.reshape(1, S5_LANES), ai.reshape(1, S5_LANES), 1)
    bmat = jnp.concatenate([_block_diag(bbr.transpose(1, 0, 2)), _block_diag(bbi.transpose(1, 0, 2))], axis=2)
    cmat = jnp.concatenate([_block_diag(p["s5_c_re"][0].transpose(0, 2, 1)),
                            -_block_diag(p["s5_c_im"][0].transpose(0, 2, 1))], axis=1)
    bmat, cmat = bmat.astype(MXU_DTYPE), cmat.astype(MXU_DTYPE)
    d = p["s5_d"]
    h, y = _s5_states(u, bmat, cmat, a, d)
    z = _rowmap(jax.nn.gelu, [y], "r", [(y.shape, MXU_DTYPE, "r")], name="s5_gelu", tl=512)
    gate_pre = _mm(z, w_glu, name="s5_glu")
    out = _rowmap(_gelu_glu, [y, gate_pre], "rr", [(y.shape, F32, "r")], name="s5_gate", tl=512)
    return out, (u, lr, li, ldt, br, bi, a, bmat, cmat, h, y, z, gate_pre)


def _s5_bwd(dout, saved, p, w_glu):
    u, lr, li, ldt, br, bi, a, bmat, cmat, h, y, z, gate_pre = saved
    d = p["s5_d"]

    def gate_bwd(dov, yv, gv):
        zv = jax.nn.gelu(yv)
        sg = jax.nn.sigmoid(gv)
        return dov * sg, dov * zv * sg * (1.0 - sg)

    dz_direct, dgate = _rowmap(gate_bwd, [dout, y, gate_pre], "rrr", [(y.shape, F32, "r"), (y.shape, MXU_DTYPE, "r")],
                               name="s5_gate_bwd", tl=512)
    dw_glu = _mm(z, dgate, ta=True, name="s5_dwglu", out_dtype=WIRE_DTYPE)
    dz = _mm(dgate, w_glu, tb=True, name="s5_dz", epilogue=lambda acc, prev: acc + prev, extras=[dz_direct])

    def gelu_bwd(dzv, yv, uv, dvv):
        _, vjp = jax.vjp(jax.nn.gelu, yv)
        dy = vjp(dzv)[0]
        return dy, dy * dvv, jnp.sum(dy * uv, axis=0, keepdims=True)

    dy, du_skip, dd = _rowmap(gelu_bwd, [dz, y, u, d], "rrrc",
                              [(y.shape, F32, "r"), (y.shape, F32, "r"), (d.shape, F32, "a")], name="s5_gelu_bwd", tl=512)
    du, da, dbmat, dcmat = _s5_states_bwd(dy, h, u, bmat, cmat, a, du_skip)
    dbbr, dbbi = (_block_diag_part(t).transpose(1, 0, 2) for t in (dbmat[:, :, :S5_BLOCK], dbmat[:, :, S5_BLOCK:]))
    dar, dai = _unpair_columns(da, 1)
    cts = (dar.reshape(S5_GROUPS, S5_STATE), dai.reshape(S5_GROUPS, S5_STATE), dbbr, dbbi)
    dlr, dli, dldt, dbr, dbi = _s5_prep_bwd(lr, li, ldt, br, bi, cts)
    dcr, dci = (_block_diag_part(t).transpose(0, 2, 1) for t in (dcmat[:, :S5_BLOCK], dcmat[:, S5_BLOCK:]))
    grads = {
        "s5_lambda_re": dlr[None], "s5_lambda_im": dli[None], "s5_log_dt": dldt[:, 0][None],
        "s5_b_re": dbr.transpose(1, 2, 0)[None], "s5_b_im": dbi.transpose(1, 2, 0)[None],
        "s5_c_re": dcr[None], "s5_c_im": -dci[None], "s5_d": dd,
    }
    return du, dw_glu, grads


def _mix0_fwd(x, g, p, full, late):
    (h, proj), _ = _norm_proj(x, g, full[("ab_w_in", 0)], name="mix0_in")
    u = proj[:, :S5_WIDTH]
    job, keys = late.gather_job("sb_fwd") if late else (None, [])
    (o, lsum), got = _attend(proj, job=job)
    full.update(zip(keys, got))
    w_glu, w_out = full[("s5_w_glu", 0)], full[("ab_w_out", 0)]
    y_a, s5_saved = _s5_fwd(u, p, w_glu)
    mix = jnp.concatenate([y_a, o], axis=1).astype(MXU_DTYPE)
    x2 = _mm(mix, w_out, name="mix0_out", epilogue=lambda acc, xv: xv + acc, extras=[x])
    return x2, (x, h, proj, lsum, mix, s5_saved)


def _mix0_bwd(dx2, saved, g, p, full, grads, late):
    x, h, proj, lsum, mix, s5_saved = saved
    w_in, w_glu, w_out = full[("ab_w_in", 0)], full[("s5_w_glu", 0)], full[("ab_w_out", 0)]
    dmix = _mm(dx2, w_out, tb=True, name="mix0_dmix")
    grads[("ab_w_out", 0)] = _mm(mix, dx2, ta=True, name="mix0_dwout", out_dtype=WIRE_DTYPE)
    du, grads[("s5_w_glu", 0)], s5_grads = _s5_bwd(dmix[:, :S5_WIDTH], s5_saved, p, w_glu)
    job, keys = late.scatter_job(grads) if late else (None, [])
    (dq, dk, dv), got = _attend_bwd(proj, lsum, dmix, job=job)
    _note(late, keys, got)
    dproj = jnp.concatenate([du] + [t.astype(MXU_DTYPE) for t in (dq, dk, dv)], axis=1)
    grads[("ab_w_in", 0)] = _mm(h, dproj, ta=True, name="mix0_dwin", out_dtype=WIRE_DTYPE)
    job, keys = late.scatter_job(grads) if late else (None, [])
    (dx, dg), got = _proj_norm_bwd([(dproj, w_in, "DF")], x, g, dx2, name="mix0_dh", job=job)
    _note(late, keys, got)
    return dx, dg, s5_grads


def _shift_down(t, n):
    rows = lax.broadcasted_iota(jnp.int32, t.shape, 0)
    return jnp.where(rows >= n, pltpu.roll(t, n, 0), 0.0)


def _shift_up(t, n):
    rows = lax.broadcasted_iota(jnp.int32, t.shape, 0)
    return jnp.where(rows < t.shape[0] - n, pltpu.roll(t, t.shape[0] - n, 0), 0.0)


def _conv_fwd(proj, cw, *, tc=128):
    seq, c3 = proj.shape
    ch = c3 // 3
    nb = ch // tc

    def body(b_ref, c_ref, v_ref, w_ref, m_ref):
        pv = c_ref[...] * v_ref[...]
        w = w_ref[...]
        y = w[2:3] * pv + w[1:2] * _shift_down(pv, 1) + w[0:1] * _shift_down(pv, 2)
        m_ref[...] = (b_ref[...] * y).astype(m_ref.dtype)

    col = lambda part: pl.BlockSpec((seq, tc), lambda j: (0, part * nb + j))
    return pl.pallas_call(
        body, name="conv_fwd", grid=(nb,),
        in_specs=[col(0), col(1), col(2), pl.BlockSpec((3, tc), lambda j: (0, j))],
        out_specs=pl.BlockSpec((seq, tc), lambda j: (0, j)),
        out_shape=jax.ShapeDtypeStruct((seq, ch), MXU_DTYPE),
        compiler_params=_params("parallel"),
    )(proj, proj, proj, cw)


def _conv_bwd(proj, cw, dm, *, tc=128):
    seq, c3 = proj.shape
    ch = c3 // 3
    nb = ch // tc

    def body(b_ref, c_ref, v_ref, w_ref, dm_ref, dproj_ref, dw_ref, dc_ref, dv_ref):
        part = pl.program_id(1)

        @pl.when(part == 0)
        def _():
            cv, vv, dmv = c_ref[...], v_ref[...], dm_ref[...]
            pv = cv * vv
            w = w_ref[...]
            p1, p2 = _shift_down(pv, 1), _shift_down(pv, 2)
            y = w[2:3] * pv + w[1:2] * p1 + w[0:1] * p2
            dproj_ref[...] = (dmv * y).astype(dproj_ref.dtype)
            dy = dmv * b_ref[...]
            dp = w[2:3] * dy + w[1:2] * _shift_up(dy, 1) + w[0:1] * _shift_up(dy, 2)
            dc_ref[...] = (dp * vv).astype(dc_ref.dtype)
            dv_ref[...] = (dp * cv).astype(dv_ref.dtype)
            dw_ref[...] = jnp.concatenate([jnp.sum(dy * p2, axis=0, keepdims=True), jnp.sum(dy * p1, axis=0, keepdims=True),
                                           jnp.sum(dy * pv, axis=0, keepdims=True)], axis=0)

        @pl.when(part == 1)
        def _():
            dproj_ref[...] = dc_ref[...]

        @pl.when(part == 2)
        def _():
            dproj_ref[...] = dv_ref[...]

    col = lambda part: pl.BlockSpec((seq, tc), lambda j, t: (0, part * nb + j))
    small = pl.BlockSpec((3, tc), lambda j, t: (0, j))
    return pl.pallas_call(
        body, name="conv_bwd", grid=(nb, 3),
        in_specs=[col(0), col(1), col(2), small, pl.BlockSpec((seq, tc), lambda j, t: (0, j))],
        out_specs=[pl.BlockSpec((seq, tc), lambda j, t: (0, t * nb + j)), small],
        out_shape=[jax.ShapeDtypeStruct((seq, c3), MXU_DTYPE), jax.ShapeDtypeStruct((3, ch), F32)],
        scratch_shapes=[pltpu.VMEM((seq, tc), MXU_DTYPE)] * 2,
        compiler_params=_params("parallel", "arbitrary"),
    )(proj, proj, proj, cw, dm)


def _mix1_fwd(x, g, full, late):
    job, keys = late.gather_job("mix1_in") if late else (None, [])
    (h, proj), got = _norm_proj(x, g, full[("sc_w_in", 0)], name="mix1_in", job=job)
    full.update(zip(keys, got))
    m = _conv_fwd(proj, full[("sc_conv_w", 0)])
    x2 = _mm(m, full[("sc_w_out", 0)], name="mix1_out", epilogue=lambda acc, xv: xv + acc, extras=[x])
    return x2, (x, h, proj, m)


def _mix1_bwd(dx2, saved, g, w_in, cw, w_out):
    x, h, proj, m = saved
    dm = _mm(dx2, w_out, tb=True, name="mix1_dm")
    dw_out = _mm(m, dx2, ta=True, name="mix1_dwout", out_dtype=WIRE_DTYPE)
    dproj, dcw = _conv_bwd(proj, cw, dm)
    dw_in = _mm(h, dproj, ta=True, name="mix1_dwin", out_dtype=WIRE_DTYPE)
    (dx, dg), _ = _proj_norm_bwd([(dproj, w_in, "DF")], x, g, dx2, name="mix1_dh")
    return dx, dg, dw_in, dcw, dw_out


def _loss_head(x, g, target):
    feat = x.shape[1]

    def fn(xv, gv, tv):
        err = _rms_fwd(xv, gv) - tv
        dx, dg = _rms_bwd(err / feat, xv, gv)
        return jnp.sum(err * err, keepdims=True) * (0.5 / feat), dx, dg

    return _rowmap(fn, [x, g, target], "rcr", [((1, 1), F32, "a"), (x.shape, F32, "r"), (g.shape, F32, "a")],
                   name="loss_head", tl=256)


def _slot(ref, place, chip=None, half=None):
    axis, width = place
    shape = list(ref.shape)
    start = [0, 0]
    if chip is not None:
        start[axis], shape[axis] = chip * width, width
    if half is not None:
        h_axis = 0 if shape[0] % 32 == 0 else 1
        shape[h_axis] //= 2
        start[h_axis] = start[h_axis] + half * shape[h_axis]
    hint = lambda s, d: s if isinstance(s, int) else pl.multiple_of(s, 128 if d == 1 else 8)
    return ref.at[tuple(pl.ds(hint(s, d), n) for d, (s, n) in enumerate(zip(start, shape)))]


class _Exchange:
    def __init__(self, kind, arrays, places):
        self.kind, self.arrays, self.places, self.n = kind, list(arrays), list(places), len(arrays)
        self.out_shape = []
        for t, (axis, width) in zip(self.arrays, self.places):
            if kind == "gather":
                shape = list(t.shape)
                shape[axis] = N_CHIPS * width
            else:
                shape = [N_CHIPS] + list(t.shape)
                shape[1 + axis] = width
            self.out_shape.append(jax.ShapeDtypeStruct(tuple(shape), t.dtype))
        n = self.n
        self.scratch = [pltpu.SemaphoreType.DMA((3 * n,)) for _ in range(4 if kind == "gather" else 2)]
        self.scratch.append(pltpu.SemaphoreType.DMA((n,)))

    def _copies(self, ins, outs, sems):
        x, y, c = lax.axis_index("x"), lax.axis_index("y"), lax.axis_index("c")
        peers = [(1 - x, y), (x, 1 - y), (1 - x, 1 - y)]
        remote = lambda src, dst, send, recv, k, to: pltpu.make_async_remote_copy(
            src_ref=src, dst_ref=dst, send_sem=send.at[k], recv_sem=recv.at[k], device_id=to, device_id_type=MESH_ID)
        local, ici, d2d = [], [], []
        for a in range(self.n):
            place = self.places[a]
            if self.kind == "gather":
                local.append(pltpu.make_async_copy(ins[a], _slot(outs[a], place, 2 * x + y), sems[4].at[a]))
                for r, (px, py) in enumerate(peers):
                    ici.append(remote(_slot(ins[a], place, None, c), _slot(outs[a], place, 2 * x + y, c),
                                      sems[0], sems[1], 3 * a + r, (px, py, c)))
                    landed = _slot(outs[a], place, 2 * px + py, c)
                    d2d.append(remote(landed, landed, sems[2], sems[3], 3 * a + r, (x, y, 1 - c)))
            else:
                local.append(pltpu.make_async_copy(_slot(ins[a], place, 2 * x + y), outs[a].at[3], sems[2].at[a]))
                for r, (px, py) in enumerate(peers):
                    ici.append(remote(_slot(ins[a], place, 2 * px + py), outs[a].at[r], sems[0], sems[1], 3 * a + r, (px, py, c)))
        return local, ici, d2d

    def start(self, ins, outs, sems):
        local, ici, _ = self._copies(ins, outs, sems)
        for cp in local + ici:
            cp.start()

    def relay(self, ins, outs, sems):
        _, ici, d2d = self._copies(ins, outs, sems)
        for arrived, onward in zip(ici, d2d):
            arrived.wait_recv()
            onward.start()

    def finish(self, ins, outs, sems):
        local, ici, d2d = self._copies(ins, outs, sems)
        for cp in local + d2d:
            cp.wait()
        for cp in ici:
            cp.wait_send() if d2d else cp.wait()


def _exchange_call(job, name):
    n = job.n

    def body(*refs):
        ins, outs, sems = refs[:n], refs[n:2 * n], refs[2 * n:]
        job.start(ins, outs, sems)
        job.relay(ins, outs, sems)
        job.finish(ins, outs, sems)

    return pl.pallas_call(
        body, name=name, in_specs=[ANY_SPEC] * n, out_specs=[ANY_SPEC] * n, out_shape=job.out_shape,
        scratch_shapes=job.scratch, compiler_params=pltpu.CompilerParams(has_side_effects=True),
    )(*job.arrays)


def _carried_call(body, *, name, grid, in_specs, out_specs, out_shape, semantics, operands, scratch_shapes=(), job=None):
    scratch_shapes = list(scratch_shapes)
    if job is None:
        return pl.pallas_call(body, name=name, grid=grid, in_specs=in_specs, out_specs=out_specs, out_shape=out_shape,
                              scratch_shapes=scratch_shapes, compiler_params=_params(*semantics))(*operands), []
    n_in, n_out, n, n_scr = len(in_specs), len(out_specs), job.n, len(scratch_shapes)
    steps = math.prod(grid)

    def wrapped(*refs):
        ins, job_ins = refs[:n_in], refs[n_in:n_in + n]
        outs, job_outs = refs[n_in + n:n_in + n + n_out], refs[n_in + n + n_out:n_in + 2 * n + n_out]
        outs = outs + refs[n_in + 2 * n + n_out:n_in + 2 * n + n_out + n_scr]
        sems = refs[n_in + 2 * n + n_out + n_scr:]
        step = functools.reduce(lambda acc, d: acc * grid[d] + pl.program_id(d), range(len(grid)), 0)

        @pl.when(step == 0)
        def _():
            job.start(job_ins, job_outs, sems)

        @pl.when(step == (3 * steps) // 4)
        def _():
            job.relay(job_ins, job_outs, sems)

        body(*ins, *outs)

        @pl.when(step == steps - 1)
        def _():
            job.finish(job_ins, job_outs, sems)

    res = pl.pallas_call(
        wrapped, name=name, grid=grid, in_specs=list(in_specs) + [ANY_SPEC] * n, out_specs=list(out_specs) + [ANY_SPEC] * n,
        out_shape=list(out_shape) + job.out_shape, scratch_shapes=scratch_shapes + job.scratch,
        compiler_params=pltpu.CompilerParams(dimension_semantics=("arbitrary",) * len(grid), vmem_limit_bytes=VMEM_LIMIT,
                                             has_side_effects=True),
    )(*operands, *job.arrays)
    return res[:n_out], res[n_out:]


def _swap_with_sibling(parts):
    n = len(parts)

    def body(*refs):
        ins, outs = refs[:n], refs[n:2 * n]
        send, recv = refs[2 * n:]
        sibling = (lax.axis_index("x"), lax.axis_index("y"), 1 - lax.axis_index("c"))
        copies = [pltpu.make_async_remote_copy(src_ref=ins[a], dst_ref=outs[a], send_sem=send.at[a], recv_sem=recv.at[a],
                                               device_id=sibling, device_id_type=MESH_ID) for a in range(n)]
        for cp in copies:
            cp.start()
        for cp in copies:
            cp.wait()

    return pl.pallas_call(
        body, name="swap_with_sibling",
        in_specs=[ANY_SPEC] * n, out_specs=[ANY_SPEC] * n,
        out_shape=[jax.ShapeDtypeStruct(p.shape, p.dtype) for p in parts],
        scratch_shapes=[pltpu.SemaphoreType.DMA((n,)), pltpu.SemaphoreType.DMA((n,))],
        compiler_params=pltpu.CompilerParams(has_side_effects=True),
    )(*parts)


def _sum_all_devices(t):
    rows = t.shape[0]

    def body(t_ref, o_ref, slots, send, recv):
        x, y, c = lax.axis_index("x"), lax.axis_index("y"), lax.axis_index("c")
        me = 4 * x + 2 * y + c
        slots[me] = t_ref[...]
        copies = []
        for m in range(1, 8):
            peer = (x ^ (m >> 2), y ^ ((m >> 1) & 1), c ^ (m & 1))
            cp = pltpu.make_async_remote_copy(src_ref=t_ref, dst_ref=slots.at[me], send_sem=send.at[m - 1],
                                              recv_sem=recv.at[m - 1], device_id=peer, device_id_type=MESH_ID)
            cp.start()
            copies.append(cp)
        for cp in copies:
            cp.wait()
        acc = slots[0]
        for dev in range(1, 8):
            acc = acc + slots[dev]
        o_ref[...] = acc

    vmem = pl.BlockSpec(memory_space=pltpu.VMEM)
    return pl.pallas_call(
        body, name="sum_all_devices", in_specs=[vmem], out_specs=vmem,
        out_shape=jax.ShapeDtypeStruct(t.shape, F32),
        scratch_shapes=[pltpu.VMEM((8, rows, 128), F32), pltpu.SemaphoreType.DMA((7,)), pltpu.SemaphoreType.DMA((7,))],
        compiler_params=pltpu.CompilerParams(vmem_limit_bytes=VMEM_LIMIT, has_side_effects=True),
    )(t)


def _adamw(w, g, m, v):
    m = ADAM_B1 * m + (1.0 - ADAM_B1) * g
    v = ADAM_B2 * v + (1.0 - ADAM_B2) * jnp.square(g)
    m_hat = m / (1.0 - ADAM_B1 ** ADAM_STEP)
    v_hat = v / (1.0 - ADAM_B2 ** ADAM_STEP)
    return -ADAM_LR * (m_hat / (jnp.sqrt(v_hat) + ADAM_EPS) + ADAM_WD * w), m, v


def _chip_sum(received, name):
    rows, cols = received.shape[1:]
    tl = _row_block(rows, 512, tile=32 // received.dtype.itemsize)

    def body(r_ref, o_ref):
        total = ((r_ref[0].astype(F32) + r_ref[1].astype(F32)) + r_ref[2].astype(F32)) + r_ref[3].astype(F32)
        o_ref[...] = total.astype(o_ref.dtype)

    return pl.pallas_call(body, name=name, grid=(rows // tl,),
                          in_specs=[pl.BlockSpec((N_CHIPS, tl, cols), lambda i: (0, i, 0))],
                          out_specs=pl.BlockSpec((tl, cols), lambda i: (i, 0)),
                          out_shape=jax.ShapeDtypeStruct((rows, cols), received.dtype),
                          compiler_params=_params("parallel"))(received)


def _adamw_layer(w, m, v, p_mine, p_other, layer, prev, name):
    _, rows, cols = w.shape
    assert p_mine.shape[1] == cols and p_mine.shape[0] >= rows
    tl = _row_block(rows, 512, tile=32 // p_mine.dtype.itemsize)
    tc = cols
    if tl == rows and rows > 512 and cols % 256 == 0:
        tc = 256

    def body(w_ref, m_ref, v_ref, pa_ref, pb_ref, *rest):
        g = pa_ref[...].astype(F32) + pb_ref[...].astype(F32)
        for o_ref, val in zip(rest[-4:], (g,) + _adamw(w_ref[...], g, m_ref[...], v_ref[...])):
            o_ref[...] = val

    stacked = pl.BlockSpec((None, tl, tc), lambda i, j: (layer, i, j))
    part = pl.BlockSpec((tl, tc), lambda i, j: (i, j))
    kept = list(prev) if prev else []
    return pl.pallas_call(
        body, name=name, grid=(rows // tl, cols // tc),
        in_specs=[stacked] * 3 + [part] * 2 + [ANY_SPEC] * len(kept),
        out_specs=[stacked] * 4, out_shape=[jax.ShapeDtypeStruct(w.shape, F32)] * 4,
        input_output_aliases={5 + k: k for k in range(len(kept))},
        compiler_params=_params("parallel", "parallel"),
    )(w, m, v, p_mine, p_other, *kept)


def _adamw_small(w, g, m, v):
    def fn(wv, gv, mv, vv):
        return _adamw(wv, gv, mv, vv)

    return _rowmap(fn, [w, g, m, v], "rrrr", [(w.shape, F32, "r")] * 3, name="adamw_small", tl=w.shape[0])


WEIGHTS = ['ffn1_norm', 'ffn1_w_gate', 'ffn1_w_up', 'ffn1_w_down', 'mix_norm', 'ffn2_norm', 'ffn2_w_gate', 'ffn2_w_up',
           'ffn2_w_down', 'ab_w_in', 's5_lambda_re', 's5_lambda_im', 's5_log_dt', 's5_b_re', 's5_b_im', 's5_c_re', 's5_c_im',
           's5_d', 's5_w_glu', 'ab_w_out', 'sc_w_in', 'sc_conv_w', 'sc_w_out', 'final_norm']
SHARDED = {'ffn1_w_gate': (0, FF_SLOT), 'ffn1_w_up': (0, FF_SLOT), 'ffn1_w_down': (0, FF_SLOT),
           'ffn2_w_gate': (0, FF_SLOT), 'ffn2_w_up': (0, FF_SLOT), 'ffn2_w_down': (0, FF_SLOT),
           'ab_w_in': (1, 512), 's5_w_glu': (0, 128), 'ab_w_out': (0, 256), 'sc_w_in': (1, 768), 'sc_conv_w': (1, 256),
           'sc_w_out': (0, 256)}
SWAPPED = ('ffn1_w_gate', 'ffn1_w_up', 'ffn2_w_gate', 'ffn2_w_up')
SMALL = [n for n in WEIGHTS if n not in SHARDED]


def _held(name, t):
    return jnp.swapaxes(t, 1, 2) if name in SWAPPED else t


def _pack(arrays):
    rows = []
    for t in arrays:
        flat = t.reshape(-1)
        rows.append(jnp.pad(flat, (0, (-flat.shape[0]) % 128)))
    flat = jnp.concatenate(rows)
    return jnp.pad(flat, (0, (-flat.shape[0]) % 1024)).reshape(-1, 128)


def _unpack(packed, like):
    flat, out, pos = packed.reshape(-1), [], 0
    for t in like:
        out.append(flat[pos:pos + t.size].reshape(t.shape))
        pos += t.size + (-t.size) % 128
    return out


def _local_grads(x, target, p, full, late=None):
    small, grads, saved = {}, {}, []
    for layer in range(2):
        x, s1 = _ffn_fwd(x, p["ffn1_norm"][layer:layer + 1], full, "ffn1", layer, late)
        if layer == 0:
            x, sm = _mix0_fwd(x, p["mix_norm"][0:1], p, full, late)
        else:
            x, sm = _mix1_fwd(x, p["mix_norm"][1:2], full, late)
        x, s2 = _ffn_fwd(x, p["ffn2_norm"][layer:layer + 1], full, "ffn2", layer, late)
        saved.append((s1, sm, s2))
    loss, dx, dg_final = _loss_head(x, p["final_norm"][None], target)
    small["final_norm"] = dg_final[0]
    gains = {n: [None, None] for n in ("ffn1_norm", "mix_norm", "ffn2_norm")}

    def ffn_bwd(which, layer, dx, s):
        dx, dg = _ffn_bwd(dx, s, p[f"{which}_norm"][layer:layer + 1], full, which, layer, grads, late,
                          inline=(which, layer) in (("ffn2", 1), ("ffn1", 0)))
        gains[f"{which}_norm"][layer] = dg[0]
        return dx

    for layer in (1, 0):
        s1, sm, s2 = saved[layer]
        dx = ffn_bwd("ffn2", layer, dx, s2)
        if layer == 0:
            dx, dg, s5_grads = _mix0_bwd(dx, sm, p["mix_norm"][0:1], p, full, grads, late)
            small.update(s5_grads)
        else:
            dx, dg, dw_in, dcw, dw_out = _mix1_bwd(dx, sm, p["mix_norm"][1:2], full[("sc_w_in", 0)], full[("sc_conv_w", 0)],
                                                   full[("sc_w_out", 0)])
            grads.update({("sc_w_in", 0): dw_in, ("sc_conv_w", 0): dcw.astype(WIRE_DTYPE), ("sc_w_out", 0): dw_out})
        gains["mix_norm"][layer] = dg[0]
        dx = ffn_bwd("ffn1", layer, dx, s1)
    small.update({n: jnp.stack(pair) for n, pair in gains.items()})
    return loss, dx, small, grads


_GATHER_PLAN = {
    "gather_early": [("ffn1_w_gate", 0), ("ffn1_w_up", 0)],
    "ffn1_0_up": [("ffn1_w_down", 0), ("ab_w_in", 0)],
    "sb_fwd": [("s5_w_glu", 0), ("ab_w_out", 0), ("ffn2_w_gate", 0), ("ffn2_w_up", 0), ("ffn2_w_down", 0),
               ("ffn1_w_gate", 1), ("ffn1_w_up", 1), ("ffn1_w_down", 1)],
    "ffn2_0_up": [("sc_w_in", 0), ("sc_conv_w", 0), ("sc_w_out", 0)],
    "ffn1_1_up": [("ffn2_w_gate", 1), ("ffn2_w_up", 1)],
    "mix1_in": [("ffn2_w_down", 1)],
}


class _Late:
    def __init__(self, shards, places):
        self.shards, self.places = shards, places
        self.sent, self.received = set(), {}

    def gather_job(self, carrier):
        keys = _GATHER_PLAN.get(carrier, [])
        if not keys:
            return None, []
        return _Exchange("gather", [self.shards[k] for k in keys], [self.places[k] for k in keys]), keys

    def scatter_job(self, grads):
        keys = [k for k in grads if k not in self.sent]
        if not keys:
            return None, []
        self.sent.update(keys)
        return _Exchange("scatter", [grads[k] for k in keys], [self.places[k] for k in keys]), keys


def kernel(x, ffn1_norm, ffn1_w_gate, ffn1_w_up, ffn1_w_down, mix_norm, ffn2_norm, ffn2_w_gate, ffn2_w_up, ffn2_w_down, ab_w_in, s5_lambda_re, s5_lambda_im, s5_log_dt, s5_b_re, s5_b_im, s5_c_re, s5_c_im, s5_d, s5_w_glu, ab_w_out, sc_w_in, sc_conv_w, sc_w_out, final_norm, loss_target, m_ffn1_norm, m_ffn1_w_gate, m_ffn1_w_up, m_ffn1_w_down, m_mix_norm, m_ffn2_norm, m_ffn2_w_gate, m_ffn2_w_up, m_ffn2_w_down, m_ab_w_in, m_s5_lambda_re, m_s5_lambda_im, m_s5_log_dt, m_s5_b_re, m_s5_b_im, m_s5_c_re, m_s5_c_im, m_s5_d, m_s5_w_glu, m_ab_w_out, m_sc_w_in, m_sc_conv_w, m_sc_w_out, m_final_norm, v_ffn1_norm, v_ffn1_w_gate, v_ffn1_w_up, v_ffn1_w_down, v_mix_norm, v_ffn2_norm, v_ffn2_w_gate, v_ffn2_w_up, v_ffn2_w_down, v_ab_w_in, v_s5_lambda_re, v_s5_lambda_im, v_s5_log_dt, v_s5_b_re, v_s5_b_im, v_s5_c_re, v_s5_c_im, v_s5_d, v_s5_w_glu, v_ab_w_out, v_sc_w_in, v_sc_conv_w, v_sc_w_out, v_final_norm):
    args = dict(locals())
    p = {n: _held(n, args[n]) for n in WEIGHTS}
    mom = {n: _held(n, args["m_" + n]) for n in WEIGHTS}
    var = {n: _held(n, args["v_" + n]) for n in WEIGHTS}

    keys = [(n, layer) for n in SHARDED for layer in range(p[n].shape[0])]
    shards, places = {}, {}
    for n, layer in keys:
        axis, width = SHARDED[n]
        t = p[n][layer] if n == "sc_conv_w" else p[n][layer].astype(MXU_DTYPE)
        pad = [(0, 0), (0, 0)]
        pad[axis] = (0, width - t.shape[axis])
        shards[(n, layer)], places[(n, layer)] = jnp.pad(t, pad), (axis, width)
    late = _Late(shards, places)
    job, first = late.gather_job("gather_early")
    full = dict(zip(first, _exchange_call(job, "gather_early")))

    loss, dx, small, grads = _local_grads(x[0], loss_target[0], p, full, late)
    loss = lax.psum(loss[0, 0], ("x", "y", "c"))
    assert set(late.received) == set(keys), "a gradient was left without a carrier"

    partial = [_chip_sum(late.received[(n, layer)], name=f"chip_sum_{n}_{layer}") for n, layer in keys]
    other = _swap_with_sibling(partial)
    out = {}
    for (n, layer), mine, theirs in zip(keys, partial, other):
        out[n] = _adamw_layer(p[n], mom[n], var[n], mine, theirs, layer, out.get(n), name=f"adamw_{n}_{layer}")
    out = {n: [_held(n, t) for t in res] for n, res in out.items()}

    like = [p[n] for n in SMALL]
    g_small = _sum_all_devices(_pack([small[n] for n in SMALL]))
    d_small, m_small, v_small = _adamw_small(_pack(like), g_small, _pack([mom[n] for n in SMALL]), _pack([var[n] for n in SMALL]))
    for k, packed in enumerate((g_small, d_small, m_small, v_small)):
        for n, t in zip(SMALL, _unpack(packed, like)):
            out.setdefault(n, [None] * 4)[k] = t

    return (loss, dx[None], *[out[n][0] for n in WEIGHTS], *[out[n][1] for n in WEIGHTS],
            *[out[n][2] for n in WEIGHTS], *[out[n][3] for n in WEIGHTS])
```

```python
import functools
import math

import jax
import jax.numpy as jnp
from jax import lax
from jax.experimental import pallas as pl
from jax.experimental.pallas import tpu as pltpu

F32 = jnp.float32
MXU_DTYPE = jnp.bfloat16
WIRE_DTYPE = jnp.bfloat16
MESH_ID = pl.DeviceIdType.MESH

D_MODEL = 1024
D_FF = 2752
N_CHIPS = 4
FF_SHARD = D_FF // N_CHIPS
FF_SLOT = 768
FF_PAD = N_CHIPS * FF_SLOT
S5_WIDTH = 512
S5_GROUP = 16
S5_GROUPS = 32
S5_STATE = 64
S5_LANES = S5_GROUPS * S5_STATE
S5_BLOCK = 512
S5_DIAG = S5_LANES // S5_BLOCK
SB_HEADS = 8
SB_DH = 64
SB_SCALE = 0.125
SB_PACK = 2
SB_QUERIES = 1024
SB_KEYS = 256
EPS = 1e-6
ADAM_LR, ADAM_B1, ADAM_B2, ADAM_EPS, ADAM_WD, ADAM_STEP = 0.001, 0.9, 0.999, 1e-08, 0.01, 10
VMEM_LIMIT = 56 * 1024 * 1024

ANY_SPEC = pl.BlockSpec(memory_space=pl.ANY)


def _params(*sem):
    return pltpu.CompilerParams(dimension_semantics=sem or None, vmem_limit_bytes=VMEM_LIMIT)


def _mm(a, b, *, name, ta=False, tb=False, out_dtype=F32, epilogue=None, extras=(), tm=1024, tn=1024, tk=1024, job=None):
    m, k = (a.shape[1], a.shape[0]) if ta else a.shape
    n = b.shape[0] if tb else b.shape[1]
    tm, tn, tk = min(tm, m), min(tn, n), min(tk, k)
    assert m % tm == 0 and n % tn == 0 and k % tk == 0, (name, m, n, k)
    grid = (m // tm, n // tn, k // tk)
    a_spec = pl.BlockSpec((tk, tm), lambda i, j, kk: (kk, i)) if ta else pl.BlockSpec((tm, tk), lambda i, j, kk: (i, kk))
    b_spec = pl.BlockSpec((tn, tk), lambda i, j, kk: (j, kk)) if tb else pl.BlockSpec((tk, tn), lambda i, j, kk: (kk, j))
    nk = grid[2]
    ex_specs = []
    for e in extras:
        if e.shape == (m, n):
            ex_specs.append(pl.BlockSpec((tm, tn), lambda i, j, kk: (i, j)))
        elif e.shape == (1, n):
            ex_specs.append(pl.BlockSpec((1, tn), lambda i, j, kk: (0, j)))
        else:
            assert e.shape == (m, 1), (name, e.shape)
            ex_specs.append(pl.BlockSpec((tm, 1), lambda i, j, kk: (i, 0)))
    dims = (((0 if ta else 1,), (1 if tb else 0,)), ((), ()))
    n_ex = len(extras)

    out_dtypes = list(out_dtype) if isinstance(out_dtype, (list, tuple)) else [out_dtype]
    n_out = len(out_dtypes)

    def body(a_ref, b_ref, *rest):
        ex_refs, o_refs = rest[:n_ex], rest[n_ex:n_ex + n_out]

        def product():
            return lax.dot_general(a_ref[...].astype(MXU_DTYPE), b_ref[...].astype(MXU_DTYPE), dims, preferred_element_type=F32)

        def finish(r):
            if epilogue is not None:
                r = epilogue(r, *[e[...] for e in ex_refs])
            for o_ref, val in zip(o_refs, r if isinstance(r, (tuple, list)) else (r,)):
                o_ref[...] = val.astype(o_ref.dtype)

        if nk == 1:
            finish(product())
            return
        acc_ref, kk = rest[n_ex + n_out], pl.program_id(2)

        @pl.when(kk == 0)
        def _():
            acc_ref[...] = jnp.zeros_like(acc_ref)

        acc_ref[...] += product()

        @pl.when(kk == nk - 1)
        def _():
            finish(acc_ref[...])

    res, got = _carried_call(
        body, name=name, grid=grid,
        in_specs=[a_spec, b_spec, *ex_specs],
        out_specs=[pl.BlockSpec((tm, tn), lambda i, j, kk: (i, j))] * n_out,
        out_shape=[jax.ShapeDtypeStruct((m, n), dt) for dt in out_dtypes],
        scratch_shapes=[pltpu.VMEM((tm, tn), F32)] if nk > 1 else [],
        semantics=("parallel", "parallel", "arbitrary"), operands=(a, b, *extras), job=job)
    res = res if isinstance(out_dtype, (list, tuple)) else res[0]
    return res if job is None else (res, got)


def _row_block(rows, want, tile=8):
    for tl in range(min(want, rows), tile - 1, -1):
        if rows % tl == 0 and tl % tile == 0:
            return tl
    return rows


def _rowmap(fn, ins, in_kinds, outs, *, name, tl):
    rows = next(x.shape[0] for x, kd in zip(ins, in_kinds) if kd == "r")
    tl = _row_block(rows, tl)
    n_in = len(ins)

    def spec(shape, kind):
        if kind == "r":
            return pl.BlockSpec((tl,) + tuple(shape[1:]), lambda i: (i,) + (0,) * (len(shape) - 1))
        return pl.BlockSpec(tuple(shape), lambda i: (0,) * len(shape))

    def body(*refs):
        in_refs, out_refs = refs[:n_in], refs[n_in:]
        res = fn(*[r[...] for r in in_refs])
        if not isinstance(res, (tuple, list)):
            res = (res,)
        for o_ref, val, (_, dt, kind) in zip(out_refs, res, outs):
            if kind == "r":
                o_ref[...] = val.astype(dt)
            else:
                @pl.when(pl.program_id(0) == 0)
                def _():
                    o_ref[...] = jnp.zeros_like(o_ref)

                o_ref[...] += val.astype(dt)

    has_acc = any(kd == "a" for _, _, kd in outs)
    res = pl.pallas_call(
        body, name=name, grid=(rows // tl,),
        in_specs=[spec(x.shape, kd) for x, kd in zip(ins, in_kinds)],
        out_specs=[spec(s, kd) for s, _, kd in outs],
        out_shape=[jax.ShapeDtypeStruct(s, dt) for s, dt, _ in outs],
        compiler_params=_params("arbitrary" if has_acc else "parallel"),
    )(*ins)
    return res[0] if len(outs) == 1 else res


def _rms_fwd(x, g):
    r = lax.rsqrt(jnp.mean(x * x, axis=-1, keepdims=True) + EPS)
    return x * r * g


def _rms_bwd(dh, x, g):
    r = lax.rsqrt(jnp.mean(x * x, axis=-1, keepdims=True) + EPS)
    xh = x * r
    dxh = dh * g
    dx = r * (dxh - xh * jnp.mean(dxh * xh, axis=-1, keepdims=True))
    return dx, jnp.sum(dh * xh, axis=0, keepdims=True)


def _swiglu_act(a, b):
    return jax.nn.silu(a) * b


def _ffn_up(x, g, wg, wu, *, name, tm=1024, tn=1024, job=None):
    m, d = x.shape
    n = wg.shape[0]
    tm, tn = min(tm, m), min(tn, n)
    assert m % tm == 0 and n % tn == 0, (name, m, n)

    def body(x_ref, g_ref, wg_ref, wu_ref, h_ref, a_ref, b_ref, s_ref):
        @pl.when(pl.program_id(1) == 0)
        def _():
            h_ref[...] = _rms_fwd(x_ref[...], g_ref[...]).astype(h_ref.dtype)

        hv = h_ref[...]
        av = lax.dot_general(hv, wg_ref[...], NT_DIMS, preferred_element_type=F32)
        bv = lax.dot_general(hv, wu_ref[...], NT_DIMS, preferred_element_type=F32)
        a_ref[...] = av.astype(a_ref.dtype)
        b_ref[...] = bv.astype(b_ref.dtype)
        s_ref[...] = _swiglu_act(av, bv).astype(s_ref.dtype)

    rows = pl.BlockSpec((tm, d), lambda i, j: (i, 0))
    wgt = pl.BlockSpec((tn, d), lambda i, j: (j, 0))
    tile = pl.BlockSpec((tm, tn), lambda i, j: (i, j))
    return _carried_call(
        body, name=name, grid=(m // tm, n // tn),
        in_specs=[rows, pl.BlockSpec((1, d), lambda i, j: (0, 0)), wgt, wgt],
        out_specs=[rows, tile, tile, tile],
        out_shape=[jax.ShapeDtypeStruct((m, d), MXU_DTYPE)] + [jax.ShapeDtypeStruct((m, n), MXU_DTYPE)] * 3,
        semantics=("parallel", "arbitrary"), operands=(x, g, wg, wu), job=job)


def _norm_proj(x, g, w, *, name, tm=1024, tn=1024, job=None):
    m, d = x.shape
    n = w.shape[1]
    tm, tn = min(tm, m), min(tn, n)
    assert m % tm == 0 and n % tn == 0, (name, m, n)

    def body(x_ref, g_ref, w_ref, h_ref, o_ref):
        @pl.when(pl.program_id(1) == 0)
        def _():
            h_ref[...] = _rms_fwd(x_ref[...], g_ref[...]).astype(h_ref.dtype)

        o_ref[...] = jnp.dot(h_ref[...], w_ref[...], preferred_element_type=F32)

    rows = pl.BlockSpec((tm, d), lambda i, j: (i, 0))
    return _carried_call(
        body, name=name, grid=(m // tm, n // tn),
        in_specs=[rows, pl.BlockSpec((1, d), lambda i, j: (0, 0)), pl.BlockSpec((d, tn), lambda i, j: (0, j))],
        out_specs=[rows, pl.BlockSpec((tm, tn), lambda i, j: (i, j))],
        out_shape=[jax.ShapeDtypeStruct((m, d), MXU_DTYPE), jax.ShapeDtypeStruct((m, n), F32)],
        semantics=("parallel", "arbitrary"), operands=(x, g, w), job=job)


def _proj_norm_bwd(pairs, x, g, dres, *, name, tm=1024, tk=1024, job=None):
    m, f = pairs[0][0].shape
    d = x.shape[1]
    tm, tk = min(tm, m), min(tk, f)
    assert m % tm == 0 and f % tk == 0, (name, m, f)
    nk, n_pairs = f // tk, len(pairs)
    swapped = [kept == "FD" for _, _, kept in pairs]

    def body(*refs):
        dy_refs, w_refs = refs[:n_pairs], refs[n_pairs:2 * n_pairs]
        x_ref, g_ref, dr_ref, dx_ref, dg_ref, acc_ref = refs[2 * n_pairs:]
        i, kk = pl.program_id(0), pl.program_id(1)

        @pl.when(kk == 0)
        def _():
            acc_ref[...] = jnp.zeros_like(acc_ref)

        for dy_ref, w_ref, rows_are_f in zip(dy_refs, w_refs, swapped):
            dims = (((1,), (0,)), ((), ())) if rows_are_f else NT_DIMS
            acc_ref[...] += lax.dot_general(dy_ref[...].astype(MXU_DTYPE), w_ref[...], dims, preferred_element_type=F32)

        @pl.when(jnp.logical_and(i == 0, kk == 0))
        def _():
            dg_ref[...] = jnp.zeros_like(dg_ref)

        @pl.when(kk == nk - 1)
        def _():
            dx, dg = _rms_bwd(acc_ref[...], x_ref[...], g_ref[...])
            dx_ref[...] = dx + dr_ref[...]
            dg_ref[...] += dg

    act = pl.BlockSpec((tm, tk), lambda i, kk: (i, kk))
    w_specs = [pl.BlockSpec((tk, d), lambda i, kk: (kk, 0)) if s else pl.BlockSpec((d, tk), lambda i, kk: (0, kk)) for s in swapped]
    rows = pl.BlockSpec((tm, d), lambda i, kk: (i, 0))
    one = pl.BlockSpec((1, d), lambda i, kk: (0, 0))
    return _carried_call(
        body, name=name, grid=(m // tm, nk),
        in_specs=[act] * n_pairs + w_specs + [rows, one, rows],
        out_specs=[rows, one],
        out_shape=[jax.ShapeDtypeStruct((m, d), F32), jax.ShapeDtypeStruct((1, d), F32)],
        scratch_shapes=[pltpu.VMEM((tm, d), F32)],
        semantics=("arbitrary", "arbitrary"), operands=(*[p[0] for p in pairs], *[p[1] for p in pairs], x, g, dres), job=job)


def _ffn_dx(da, db, wg, wu, x, g, dres, *, name, job=None):
    return _proj_norm_bwd([(da, wg, "FD"), (db, wu, "FD")], x, g, dres, name=name, job=job)


def _ffn_fwd(x, g, full, which, layer, late):
    tag = f"{which}_{layer}"
    job, keys = late.gather_job(f"{tag}_up") if late else (None, [])
    (h, a, b, s), got = _ffn_up(x, g, full[(f"{which}_w_gate", layer)], full[(f"{which}_w_up", layer)], name=f"{tag}_up", job=job)
    full.update(zip(keys, got))
    x2 = _mm(s, full[(f"{which}_w_down", layer)], name=f"{tag}_down", epilogue=lambda acc, xv: xv + 0.5 * acc, extras=[x],
             tk=FF_PAD)
    return x2, (x, h, a, b, s)


def _ffn_bwd(dx2, saved, g, full, which, layer, grads, late, inline):
    x, h, a, b, s = saved
    tag = f"{which}_{layer}"
    kg, ku, kd = [(f"{which}_w_{n}", layer) for n in ("gate", "up", "down")]
    wg, wu, wd = full[kg], full[ku], full[kd]
    send = (lambda: late.scatter_job(grads)) if (late and inline) else (lambda: (None, []))

    def act_bwd(ds, av, bv):
        _, vjp = jax.vjp(_swiglu_act, av.astype(F32), bv.astype(F32))
        return vjp(0.5 * ds)

    grads[kd] = _mm(s, dx2, ta=True, name=f"{tag}_dwd", out_dtype=WIRE_DTYPE, epilogue=lambda acc: 0.5 * acc, tk=2048)
    job, keys = send()
    (da, db), got = _carried(_mm, dx2, wd, tb=True, name=f"{tag}_dact", epilogue=act_bwd, extras=[a, b],
                             out_dtype=[MXU_DTYPE, MXU_DTYPE], job=job)
    _note(late, keys, got)
    grads[kg] = _mm(da, h, ta=True, name=f"{tag}_dwg", out_dtype=WIRE_DTYPE, tk=2048)
    job, keys = send()
    grads[ku], got = _carried(_mm, db, h, ta=True, name=f"{tag}_dwu", out_dtype=WIRE_DTYPE, tk=2048, job=job)
    _note(late, keys, got)
    job, keys = send()
    (dx, dg), got = _ffn_dx(da, db, wg, wu, x, g, dx2, name=f"{tag}_dx", job=job)
    _note(late, keys, got)
    return dx, dg


def _carried(fn, *args, job, **kwargs):
    return fn(*args, job=job, **kwargs) if job is not None else (fn(*args, **kwargs), [])


def _note(late, keys, got):
    if late:
        late.received.update(zip(keys, got))


def _softplus(z):
    return jnp.maximum(z, 0.0) + jnp.log(1.0 + jnp.exp(-jnp.abs(z)))


def _ones_dot(x, tri):
    if MXU_DTYPE == F32:
        return jnp.dot(x, tri, preferred_element_type=F32)
    hi = x.astype(MXU_DTYPE)
    lo = (x - hi.astype(F32)).astype(MXU_DTYPE)
    return jnp.dot(hi, tri, preferred_element_type=F32) + jnp.dot(lo, tri, preferred_element_type=F32)


NT_DIMS = (((1,), (1,)), ((), ()))
TN_DIMS = (((0,), (0,)), ((), ()))


SB_LANES = SB_PACK * SB_DH
Q_COL, K_COL, V_COL = (S5_WIDTH * n // SB_LANES for n in (1, 2, 3))


def _head_lanes(rows, hd):
    return lax.broadcasted_iota(jnp.int32, (rows, SB_LANES), 1) // SB_DH == hd


def _attend(proj, *, tq=SB_QUERIES, job=None):
    seq = proj.shape[0]
    tq = min(tq, seq)
    tk = min(SB_KEYS, tq)
    per, hp = tq // tk, SB_PACK

    def body(q_ref, k_ref, v_ref, o_ref, ls_ref):
        i = pl.program_id(1)
        r_idx = lax.broadcasted_iota(jnp.int32, (tk, tk), 0)
        c_idx = lax.broadcasted_iota(jnp.int32, (tk, tk), 1)
        after = (r_idx > c_idx).astype(MXU_DTYPE)
        lanes = [_head_lanes(tk, hd) for hd in range(hp)]

        def block(j, cs, acc, straddles):
            off = pl.multiple_of(j * tk, tk)
            k2, v2 = k_ref[pl.ds(off, tk), :], v_ref[pl.ds(off, tk), :]
            top = 0 if straddles is None else straddles * tk
            rows = tq - top
            q2 = (q_ref[pl.ds(top, rows), :] * SB_SCALE).astype(MXU_DTYPE)
            new_cs, out = [], acc[top:]
            for hd in range(hp):
                kv = jnp.where(lanes[hd], k2, 0.0).astype(MXU_DTYPE)
                vv = jnp.where(lanes[hd], v2, 0.0).astype(MXU_DTYPE)
                z = lax.dot_general(q2, kv, NT_DIMS, preferred_element_type=F32)
                sp = _softplus(z)
                c_in = cs[hd][top:]
                if straddles is None:
                    lk = -sp
                    w = jnp.exp(z - sp + _ones_dot(lk, after) + c_in)
                else:
                    before = lax.broadcasted_iota(jnp.int32, (rows, tk), 1) < lax.broadcasted_iota(jnp.int32, (rows, tk), 0)
                    lk = jnp.where(before, -sp, 0.0)
                    w = jnp.where(before, jnp.exp(z - sp + _ones_dot(lk, after) + c_in), 0.0)
                out = out + jnp.dot(w.astype(MXU_DTYPE), vv, preferred_element_type=F32)
                c_new = c_in + jnp.sum(lk, axis=1, keepdims=True)
                new_cs.append(jnp.concatenate([cs[hd][:top], c_new], axis=0) if top else c_new)
            return tuple(new_cs), (jnp.concatenate([acc[:top], out], axis=0) if top else out)

        carry = (tuple(jnp.zeros((tq, 1), F32) for _ in range(hp)), jnp.zeros((tq, SB_LANES), F32))
        for s in reversed(range(per)):
            carry = block(i * per + s, *carry, s)
        cs, acc = lax.fori_loop(0, i * per, lambda n, cr: block(i * per - 1 - n, *cr, None), carry)
        o_ref[...] = acc
        for hd in range(hp):
            ls_ref[hd] = cs[hd]

    whole = lambda col: pl.BlockSpec((seq, SB_LANES), lambda g, i: (0, col + g))
    return _carried_call(
        body, name="sb_fwd", grid=(SB_HEADS // hp, seq // tq),
        in_specs=[pl.BlockSpec((tq, SB_LANES), lambda g, i: (i, Q_COL + g)), whole(K_COL), whole(V_COL)],
        out_specs=[pl.BlockSpec((tq, SB_LANES), lambda g, i: (i, g)), pl.BlockSpec((hp, tq, 1), lambda g, i: (g, i, 0))],
        out_shape=[jax.ShapeDtypeStruct((seq, SB_HEADS * SB_DH), F32), jax.ShapeDtypeStruct((SB_HEADS, seq, 1), F32)],
        semantics=("parallel", "parallel"), operands=(proj, proj, proj), job=job)


def _attend_bwd(proj, lsum, dmix, *, tq=SB_QUERIES, job=None):
    seq = proj.shape[0]
    tq = min(tq, seq)
    tk = min(SB_KEYS, tq)
    per, hp = tq // tk, SB_PACK
    do_col = S5_WIDTH // SB_LANES

    def body(q_ref, k_ref, v_ref, ls_ref, do_ref, dq_ref, dk_ref, dv_ref):
        i = pl.program_id(1)

        @pl.when(i == 0)
        def _():
            dk_ref[...] = jnp.zeros_like(dk_ref)
            dv_ref[...] = jnp.zeros_like(dv_ref)

        r_idx = lax.broadcasted_iota(jnp.int32, (tk, tk), 0)
        c_idx = lax.broadcasted_iota(jnp.int32, (tk, tk), 1)
        upto = (r_idx <= c_idx).astype(MXU_DTYPE)
        before = (r_idx < c_idx).astype(MXU_DTYPE)
        lanes = [_head_lanes(tk, hd) for hd in range(hp)]

        def block(j, sums, dq, straddles):
            off = pl.multiple_of(j * tk, tk)
            k2, v2 = k_ref[pl.ds(off, tk), :], v_ref[pl.ds(off, tk), :]
            top = 0 if straddles is None else straddles * tk
            rows = tq - top
            part = pl.ds(top, rows)
            q2 = (q_ref[part, :] * SB_SCALE).astype(MXU_DTYPE)
            do2 = do_ref[part, :].astype(MXU_DTYPE)
            valid = None
            if straddles is not None:
                valid = lax.broadcasted_iota(jnp.int32, (rows, tk), 1) < lax.broadcasted_iota(jnp.int32, (rows, tk), 0)
            new_sums, out, dk, dv = [], dq[top:], jnp.zeros((tk, SB_LANES), F32), jnp.zeros((tk, SB_LANES), F32)
            for hd in range(hp):
                cp, ce = sums[hd]
                kv = jnp.where(lanes[hd], k2, 0.0).astype(MXU_DTYPE)
                vv = jnp.where(lanes[hd], v2, 0.0).astype(MXU_DTYPE)
                z = lax.dot_general(q2, kv, NT_DIMS, preferred_element_type=F32)
                sp = _softplus(z)
                lk = -sp if valid is None else jnp.where(valid, -sp, 0.0)
                w = jnp.exp(z - sp + (ls_ref[hd, part, :] - cp[top:]) - _ones_dot(lk, upto))
                if valid is not None:
                    w = jnp.where(valid, w, 0.0)
                e = w * lax.dot_general(do2, vv, NT_DIMS, preferred_element_type=F32)
                earlier = jnp.dot(e.astype(MXU_DTYPE), before, preferred_element_type=F32) + ce[top:]
                keep = jnp.exp(-sp)
                dz = e * keep - (1.0 - keep) * earlier
                if valid is not None:
                    dz = jnp.where(valid, dz, 0.0)
                dzm = dz.astype(MXU_DTYPE)
                out = out + jnp.dot(dzm, kv, preferred_element_type=F32)
                dk = dk + jnp.where(lanes[hd], lax.dot_general(dzm, q2, TN_DIMS, preferred_element_type=F32), 0.0)
                dv = dv + jnp.where(lanes[hd], lax.dot_general(w.astype(MXU_DTYPE), do2, TN_DIMS, preferred_element_type=F32), 0.0)
                new = (cp[top:] + jnp.sum(lk, axis=1, keepdims=True), ce[top:] + jnp.sum(e, axis=1, keepdims=True))
                new_sums.append(tuple(jnp.concatenate([old[:top], val], axis=0) for old, val in zip((cp, ce), new)) if top else new)
            dk_ref[pl.ds(off, tk), :] += dk
            dv_ref[pl.ds(off, tk), :] += dv
            return tuple(new_sums), (jnp.concatenate([dq[:top], out], axis=0) if top else out)

        zero = jnp.zeros((tq, 1), F32)
        carry = (tuple((zero, zero) for _ in range(hp)), jnp.zeros((tq, SB_LANES), F32))
        carry = lax.fori_loop(0, i * per, lambda j, cr: block(j, *cr, None), carry)
        for s in range(per):
            carry = block(i * per + s, *carry, s)
        dq_ref[...] = carry[1] * SB_SCALE

    whole = lambda col: pl.BlockSpec((seq, SB_LANES), lambda g, i: (0, col + g))
    tile = lambda col: pl.BlockSpec((tq, SB_LANES), lambda g, i: (i, col + g))
    acc = pl.BlockSpec((seq, SB_LANES), lambda g, i: (0, g))
    return _carried_call(
        body, name="sb_bwd", grid=(SB_HEADS // hp, seq // tq),
        in_specs=[tile(Q_COL), whole(K_COL), whole(V_COL), pl.BlockSpec((hp, tq, 1), lambda g, i: (g, i, 0)), tile(do_col)],
        out_specs=[tile(0), acc, acc],
        out_shape=[jax.ShapeDtypeStruct((seq, SB_HEADS * SB_DH), F32)] * 3,
        semantics=("parallel", "arbitrary"), operands=(proj, proj, proj, lsum, dmix), job=job)


def _s5_disc(lr, li, ldt, br, bi):
    dt = jnp.exp(ldt)
    mag = jnp.exp(lr * dt)
    ar = mag * jnp.cos(li * dt)
    ai = mag * jnp.sin(li * dt)
    den = lr * lr + li * li
    nr = ar - 1.0
    cr = (nr * lr + ai * li) / den
    ci = (ai * lr - nr * li) / den
    return ar, ai, cr[None] * br - ci[None] * bi, cr[None] * bi + ci[None] * br


def _s5_prep(lr, li, ldt, br, bi):
    shapes = [lr.shape, lr.shape, br.shape, br.shape]

    def body(lr_ref, li_ref, ldt_ref, br_ref, bi_ref, *outs):
        for o, val in zip(outs, _s5_disc(lr_ref[...], li_ref[...], ldt_ref[...], br_ref[...], bi_ref[...])):
            o[...] = val

    return pl.pallas_call(body, name="s5_prep", out_shape=[jax.ShapeDtypeStruct(s, F32) for s in shapes])(lr, li, ldt, br, bi)


def _s5_prep_bwd(lr, li, ldt, br, bi, cts):
    args = (lr, li, ldt, br, bi)

    def body(*refs):
        ins, ct_refs, outs = refs[:5], refs[5:9], refs[9:]
        _, vjp = jax.vjp(_s5_disc, *[r[...] for r in ins])
        for o, val in zip(outs, vjp(tuple(r[...] for r in ct_refs))):
            o[...] = val

    return pl.pallas_call(body, name="s5_prep_bwd", out_shape=[jax.ShapeDtypeStruct(a.shape, F32) for a in args])(*args, *cts)


SCAN_ROWS = 8


def _powers(ar, ai):
    out = [(ar, ai)]
    for _ in range(SCAN_ROWS - 1):
        pr, pi = out[-1]
        out.append((pr * ar - pi * ai, pr * ai + pi * ar))
    return out


def _s5_states(u, bmat, cmat, a, d, *, tc=512):
    seq, width = u.shape
    nj, cols, w2 = bmat.shape
    tw = w2 // 2
    tc = min(tc, seq)
    assert seq % tc == 0 and nj * cols == width and tw == S5_BLOCK

    def body(u_ref, bm_ref, cm_ref, a_ref, d_ref, h_ref, y_ref, cr_ref, ci_ref):
        @pl.when(pl.program_id(1) == 0)
        def _():
            cr_ref[...] = jnp.zeros_like(cr_ref)
            ci_ref[...] = jnp.zeros_like(ci_ref)

        uv = u_ref[...]
        h_ref[...] = jnp.dot(uv.astype(MXU_DTYPE), bm_ref[0], preferred_element_type=F32)
        re, im = pl.ds(0, tw), pl.ds(tw, tw)
        powers = _powers(a_ref[:, re], a_ref[:, im])
        pr = jnp.concatenate([p[0] for p in powers], axis=0)
        pi = jnp.concatenate([p[1] for p in powers], axis=0)
        row_id = lax.broadcasted_iota(jnp.int32, (SCAN_ROWS, tw), 0)
        reach = {dist: tuple(jnp.where(row_id >= dist, part, 0.0) for part in powers[dist - 1]) for dist in (1, 2, 4)}

        def block(n, carry):
            hr, hi = carry
            rows = pl.ds(pl.multiple_of(n * SCAN_ROWS, SCAN_ROWS), SCAN_ROWS)
            yr, yi = h_ref[rows, re], h_ref[rows, im]
            for dist in (1, 2, 4):
                cr, ci = reach[dist]
                sr, si = pltpu.roll(yr, dist, 0), pltpu.roll(yi, dist, 0)
                yr, yi = yr + cr * sr - ci * si, yi + cr * si + ci * sr
            yr, yi = yr + pr * hr - pi * hi, yi + pr * hi + pi * hr
            h_ref[rows, re] = yr
            h_ref[rows, im] = yi
            return yr[SCAN_ROWS - 1:], yi[SCAN_ROWS - 1:]

        hr, hi = lax.fori_loop(0, tc // SCAN_ROWS, block, (cr_ref[...], ci_ref[...]), unroll=4)
        cr_ref[...] = hr
        ci_ref[...] = hi
        y_ref[...] = jnp.dot(h_ref[...].astype(MXU_DTYPE), cm_ref[0], preferred_element_type=F32) + d_ref[...] * uv

    io = pl.BlockSpec((tc, cols), lambda j, t: (t, j))
    return pl.pallas_call(
        body, name="s5_states", grid=(nj, seq // tc),
        in_specs=[io, pl.BlockSpec((1, cols, w2), lambda j, t: (j, 0, 0)), pl.BlockSpec((1, w2, cols), lambda j, t: (j, 0, 0)),
                  pl.BlockSpec((1, w2), lambda j, t: (0, j)), pl.BlockSpec((1, cols), lambda j, t: (0, j))],
        out_specs=[pl.BlockSpec((tc, w2), lambda j, t: (t, j)), io],
        out_shape=[jax.ShapeDtypeStruct((seq, nj * w2), F32), jax.ShapeDtypeStruct((seq, width), F32)],
        scratch_shapes=[pltpu.VMEM((1, tw), F32)] * 2,
        compiler_params=_params("parallel", "arbitrary"),
    )(u, bmat, cmat, a, d)


def _s5_states_bwd(dy, h, u, bmat, cmat, a, du_skip, *, tc=512):
    seq, width = u.shape
    nj, cols, w2 = bmat.shape
    tw = w2 // 2
    tc = min(tc, seq)
    assert seq % tc == 0
    nt = seq // tc

    def body(dy_ref, h_ref, u_ref, bm_ref, cm_ref, a_ref, sk_ref, du_ref, da_ref, db_ref, dc_ref, g_ref, cr_ref, ci_ref):
        @pl.when(pl.program_id(1) == 0)
        def _():
            cr_ref[...] = jnp.zeros_like(cr_ref)
            ci_ref[...] = jnp.zeros_like(ci_ref)
            da_ref[...] = jnp.zeros_like(da_ref)
            db_ref[...] = jnp.zeros_like(db_ref)
            dc_ref[...] = jnp.zeros_like(dc_ref)

        dyv = dy_ref[...].astype(MXU_DTYPE)
        g_ref[...] = lax.dot_general(dyv, cm_ref[0], NT_DIMS, preferred_element_type=F32)
        re, im = pl.ds(0, tw), pl.ds(tw, tw)
        powers = _powers(a_ref[:, re], a_ref[:, im])
        pr = jnp.concatenate([p[0] for p in reversed(powers)], axis=0)
        pi = jnp.concatenate([p[1] for p in reversed(powers)], axis=0)
        row_id = lax.broadcasted_iota(jnp.int32, (SCAN_ROWS, tw), 0)
        last = SCAN_ROWS - 1
        reach = {dist: tuple(jnp.where(row_id < SCAN_ROWS - dist, part, 0.0) for part in powers[dist - 1]) for dist in (1, 2, 4)}

        def block(n, carry):
            gr, gi, sr, si = carry
            rows = pl.ds(pl.multiple_of((tc // SCAN_ROWS - 1 - n) * SCAN_ROWS, SCAN_ROWS), SCAN_ROWS)
            yr, yi = g_ref[rows, re], g_ref[rows, im]
            for dist in (1, 2, 4):
                cr, ci = reach[dist]
                ur, ui = pltpu.roll(yr, SCAN_ROWS - dist, 0), pltpu.roll(yi, SCAN_ROWS - dist, 0)
                yr, yi = yr + cr * ur + ci * ui, yi + cr * ui - ci * ur
            yr, yi = yr + pr * gr + pi * gi, yi + pr * gi - pi * gr
            g_ref[rows, re] = yr
            g_ref[rows, im] = yi
            nr = jnp.where(row_id < last, pltpu.roll(yr, last, 0), gr)
            ni = jnp.where(row_id < last, pltpu.roll(yi, last, 0), gi)
            hr, hi = h_ref[rows, re], h_ref[rows, im]
            return yr[:1], yi[:1], sr + nr * hr + ni * hi, si + ni * hr - nr * hi

        zero = jnp.zeros((SCAN_ROWS, tw), F32)
        gr, gi, sr, si = lax.fori_loop(0, tc // SCAN_ROWS, block, (cr_ref[...], ci_ref[...], zero, zero), unroll=4)
        cr_ref[...] = gr
        ci_ref[...] = gi
        da_ref[:, re] += jnp.sum(sr, axis=0, keepdims=True)
        da_ref[:, im] += jnp.sum(si, axis=0, keepdims=True)
        gv = g_ref[...].astype(MXU_DTYPE)
        du_ref[...] = (lax.dot_general(gv, bm_ref[0], NT_DIMS, preferred_element_type=F32) + sk_ref[...]).astype(du_ref.dtype)
        db_ref[0] += lax.dot_general(u_ref[...].astype(MXU_DTYPE), gv, TN_DIMS, preferred_element_type=F32)
        dc_ref[0] += lax.dot_general(h_ref[...].astype(MXU_DTYPE), dyv, TN_DIMS, preferred_element_type=F32)

    io = pl.BlockSpec((tc, cols), lambda j, t: (nt - 1 - t, j))
    bm = pl.BlockSpec((1, cols, w2), lambda j, t: (j, 0, 0))
    cm = pl.BlockSpec((1, w2, cols), lambda j, t: (j, 0, 0))
    row = pl.BlockSpec((1, w2), lambda j, t: (0, j))
    return pl.pallas_call(
        body, name="s5_states_bwd", grid=(nj, nt),
        in_specs=[io, pl.BlockSpec((tc, w2), lambda j, t: (nt - 1 - t, j)), io, bm, cm, row, io],
        out_specs=[io, row, bm, cm],
        out_shape=[jax.ShapeDtypeStruct((seq, width), MXU_DTYPE), jax.ShapeDtypeStruct((1, nj * w2), F32),
                   jax.ShapeDtypeStruct(bmat.shape, F32), jax.ShapeDtypeStruct(cmat.shape, F32)],
        scratch_shapes=[pltpu.VMEM((tc, w2), F32)] + [pltpu.VMEM((1, tw), F32)] * 2,
        compiler_params=_params("parallel", "arbitrary"),
    )(dy, h, u, bmat, cmat, a, du_skip)


def _pair_columns(re, im, axis):
    shape = re.shape
    split = shape[:axis] + (shape[axis] // S5_BLOCK, S5_BLOCK) + shape[axis + 1:]
    both = jnp.stack([re.reshape(split), im.reshape(split)], axis=axis + 1)
    return both.reshape(shape[:axis] + (2 * shape[axis],) + shape[axis + 1:])


def _unpair_columns(t, axis):
    shape = t.shape
    both = t.reshape(shape[:axis] + (shape[axis] // (2 * S5_BLOCK), 2, S5_BLOCK) + shape[axis + 1:])
    half = shape[:axis] + (shape[axis] // 2,) + shape[axis + 1:]
    return (lax.index_in_dim(both, 0, axis + 1, keepdims=False).reshape(half),
            lax.index_in_dim(both, 1, axis + 1, keepdims=False).reshape(half))


S5_PER_BLOCK = S5_GROUPS // S5_DIAG


def _block_diag(t):
    g, a, b = t.shape
    n = S5_PER_BLOCK
    eye = jnp.eye(n, dtype=t.dtype)
    return (t.reshape(g // n, n, a, 1, b) * eye[None, :, None, :, None]).reshape(g // n, n * a, n * b)


def _block_diag_part(m):
    j, n = m.shape[0], S5_PER_BLOCK
    a, b = m.shape[1] // n, m.shape[2] // n
    return jnp.moveaxis(jnp.diagonal(m.reshape(j, n, a, n, b), axis1=1, axis2=3), -1, 1).reshape(j * n, a, b)


def _gelu_glu(y, gate_pre):
    z = jax.nn.gelu(y)
    return z * jax.nn.sigmoid(gate_pre)


def _s5_fwd(u, p, w_glu):
    lr, li = p["s5_lambda_re"][0], p["s5_lambda_im"][0]
    ldt = p["s5_log_dt"][0][:, None]
    br = p["s5_b_re"][0].transpose(2, 0, 1)
    bi = p["s5_b_im"][0].transpose(2, 0, 1)
    ar, ai, bbr, bbi = _s5_prep(lr, li, ldt, br, bi)
    a = _pair_columns(ar.reshape(1, S5_LANES), ai.reshape(1, S5_LANES), 1)
    bmat = jnp.concatenate([_block_diag(bbr.transpose(1, 0, 2)), _block_diag(bbi.transpose(1, 0, 2))], axis=2)
    cmat = jnp.concatenate([_block_diag(p["s5_c_re"][0].transpose(0, 2, 1)),
                            -_block_diag(p["s5_c_im"][0].transpose(0, 2, 1))], axis=1)
    bmat, cmat = bmat.astype(MXU_DTYPE), cmat.astype(MXU_DTYPE)
    d = p["s5_d"]
    h, y = _s5_states(u, bmat, cmat, a, d)
    z = _rowmap(jax.nn.gelu, [y], "r", [(y.shape, MXU_DTYPE, "r")], name="s5_gelu", tl=512)
    gate_pre = _mm(z, w_glu, name="s5_glu")
    out = _rowmap(_gelu_glu, [y, gate_pre], "rr", [(y.shape, F32, "r")], name="s5_gate", tl=512)
    return out, (u, lr, li, ldt, br, bi, a, bmat, cmat, h, y, z, gate_pre)


def _s5_bwd(dout, saved, p, w_glu):
    u, lr, li, ldt, br, bi, a, bmat, cmat, h, y, z, gate_pre = saved
    d = p["s5_d"]

    def gate_bwd(dov, yv, gv):
        zv = jax.nn.gelu(yv)
        sg = jax.nn.sigmoid(gv)
        return dov * sg, dov * zv * sg * (1.0 - sg)

    dz_direct, dgate = _rowmap(gate_bwd, [dout, y, gate_pre], "rrr", [(y.shape, F32, "r"), (y.shape, MXU_DTYPE, "r")],
                               name="s5_gate_bwd", tl=512)
    dw_glu = _mm(z, dgate, ta=True, name="s5_dwglu", out_dtype=WIRE_DTYPE)
    dz = _mm(dgate, w_glu, tb=True, name="s5_dz", epilogue=lambda acc, prev: acc + prev, extras=[dz_direct])

    def gelu_bwd(dzv, yv, uv, dvv):
        _, vjp = jax.vjp(jax.nn.gelu, yv)
        dy = vjp(dzv)[0]
        return dy, dy * dvv, jnp.sum(dy * uv, axis=0, keepdims=True)

    dy, du_skip, dd = _rowmap(gelu_bwd, [dz, y, u, d], "rrrc",
                              [(y.shape, F32, "r"), (y.shape, F32, "r"), (d.shape, F32, "a")], name="s5_gelu_bwd", tl=512)
    du, da, dbmat, dcmat = _s5_states_bwd(dy, h, u, bmat, cmat, a, du_skip)
    dbbr, dbbi = (_block_diag_part(t).transpose(1, 0, 2) for t in (dbmat[:, :, :S5_BLOCK], dbmat[:, :, S5_BLOCK:]))
    dar, dai = _unpair_columns(da, 1)
    cts = (dar.reshape(S5_GROUPS, S5_STATE), dai.reshape(S5_GROUPS, S5_STATE), dbbr, dbbi)
    dlr, dli, dldt, dbr, dbi = _s5_prep_bwd(lr, li, ldt, br, bi, cts)
    dcr, dci = (_block_diag_part(t).transpose(0, 2, 1) for t in (dcmat[:, :S5_BLOCK], dcmat[:, S5_BLOCK:]))
    grads = {
        "s5_lambda_re": dlr[None], "s5_lambda_im": dli[None], "s5_log_dt": dldt[:, 0][None],
        "s5_b_re": dbr.transpose(1, 2, 0)[None], "s5_b_im": dbi.transpose(1, 2, 0)[None],
        "s5_c_re": dcr[None], "s5_c_im": -dci[None], "s5_d": dd,
    }
    return du, dw_glu, grads


def _mix0_fwd(x, g, p, full, late):
    (h, proj), _ = _norm_proj(x, g, full[("ab_w_in", 0)], name="mix0_in")
    u = proj[:, :S5_WIDTH]
    job, keys = late.gather_job("sb_fwd") if late else (None, [])
    (o, lsum), got = _attend(proj, job=job)
    full.update(zip(keys, got))
    w_glu, w_out = full[("s5_w_glu", 0)], full[("ab_w_out", 0)]
    y_a, s5_saved = _s5_fwd(u, p, w_glu)
    mix = jnp.concatenate([y_a, o], axis=1).astype(MXU_DTYPE)
    x2 = _mm(mix, w_out, name="mix0_out", epilogue=lambda acc, xv: xv + acc, extras=[x])
    return x2, (x, h, proj, lsum, mix, s5_saved)


def _mix0_bwd(dx2, saved, g, p, full, grads, late):
    x, h, proj, lsum, mix, s5_saved = saved
    w_in, w_glu, w_out = full[("ab_w_in", 0)], full[("s5_w_glu", 0)], full[("ab_w_out", 0)]
    dmix = _mm(dx2, w_out, tb=True, name="mix0_dmix")
    grads[("ab_w_out", 0)] = _mm(mix, dx2, ta=True, name="mix0_dwout", out_dtype=WIRE_DTYPE)
    du, grads[("s5_w_glu", 0)], s5_grads = _s5_bwd(dmix[:, :S5_WIDTH], s5_saved, p, w_glu)
    job, keys = late.scatter_job(grads) if late else (None, [])
    (dq, dk, dv), got = _attend_bwd(proj, lsum, dmix, job=job)
    _note(late, keys, got)
    dproj = jnp.concatenate([du] + [t.astype(MXU_DTYPE) for t in (dq, dk, dv)], axis=1)
    grads[("ab_w_in", 0)] = _mm(h, dproj, ta=True, name="mix0_dwin", out_dtype=WIRE_DTYPE)
    job, keys = late.scatter_job(grads) if late else (None, [])
    (dx, dg), got = _proj_norm_bwd([(dproj, w_in, "DF")], x, g, dx2, name="mix0_dh", job=job)
    _note(late, keys, got)
    return dx, dg, s5_grads


def _shift_down(t, n):
    rows = lax.broadcasted_iota(jnp.int32, t.shape, 0)
    return jnp.where(rows >= n, pltpu.roll(t, n, 0), 0.0)


def _shift_up(t, n):
    rows = lax.broadcasted_iota(jnp.int32, t.shape, 0)
    return jnp.where(rows < t.shape[0] - n, pltpu.roll(t, t.shape[0] - n, 0), 0.0)


def _conv_fwd(proj, cw, *, tc=128):
    seq, c3 = proj.shape
    ch = c3 // 3
    nb = ch // tc

    def body(b_ref, c_ref, v_ref, w_ref, m_ref):
        pv = c_ref[...] * v_ref[...]
        w = w_ref[...]
        y = w[2:3] * pv + w[1:2] * _shift_down(pv, 1) + w[0:1] * _shift_down(pv, 2)
        m_ref[...] = (b_ref[...] * y).astype(m_ref.dtype)

    col = lambda part: pl.BlockSpec((seq, tc), lambda j: (0, part * nb + j))
    return pl.pallas_call(
        body, name="conv_fwd", grid=(nb,),
        in_specs=[col(0), col(1), col(2), pl.BlockSpec((3, tc), lambda j: (0, j))],
        out_specs=pl.BlockSpec((seq, tc), lambda j: (0, j)),
        out_shape=jax.ShapeDtypeStruct((seq, ch), MXU_DTYPE),
        compiler_params=_params("parallel"),
    )(proj, proj, proj, cw)


def _conv_bwd(proj, cw, dm, *, tc=128):
    seq, c3 = proj.shape
    ch = c3 // 3
    nb = ch // tc

    def body(b_ref, c_ref, v_ref, w_ref, dm_ref, dproj_ref, dw_ref, dc_ref, dv_ref):
        part = pl.program_id(1)

        @pl.when(part == 0)
        def _():
            cv, vv, dmv = c_ref[...], v_ref[...], dm_ref[...]
            pv = cv * vv
            w = w_ref[...]
            p1, p2 = _shift_down(pv, 1), _shift_down(pv, 2)
            y = w[2:3] * pv + w[1:2] * p1 + w[0:1] * p2
            dproj_ref[...] = (dmv * y).astype(dproj_ref.dtype)
            dy = dmv * b_ref[...]
            dp = w[2:3] * dy + w[1:2] * _shift_up(dy, 1) + w[0:1] * _shift_up(dy, 2)
            dc_ref[...] = (dp * vv).astype(dc_ref.dtype)
            dv_ref[...] = (dp * cv).astype(dv_ref.dtype)
            dw_ref[...] = jnp.concatenate([jnp.sum(dy * p2, axis=0, keepdims=True), jnp.sum(dy * p1, axis=0, keepdims=True),
                                           jnp.sum(dy * pv, axis=0, keepdims=True)], axis=0)

        @pl.when(part == 1)
        def _():
            dproj_ref[...] = dc_ref[...]

        @pl.when(part == 2)
        def _():
            dproj_ref[...] = dv_ref[...]

    col = lambda part: pl.BlockSpec((seq, tc), lambda j, t: (0, part * nb + j))
    small = pl.BlockSpec((3, tc), lambda j, t: (0, j))
    return pl.pallas_call(
        body, name="conv_bwd", grid=(nb, 3),
        in_specs=[col(0), col(1), col(2), small, pl.BlockSpec((seq, tc), lambda j, t: (0, j))],
        out_specs=[pl.BlockSpec((seq, tc), lambda j, t: (0, t * nb + j)), small],
        out_shape=[jax.ShapeDtypeStruct((seq, c3), MXU_DTYPE), jax.ShapeDtypeStruct((3, ch), F32)],
        scratch_shapes=[pltpu.VMEM((seq, tc), MXU_DTYPE)] * 2,
        compiler_params=_params("parallel", "arbitrary"),
    )(proj, proj, proj, cw, dm)


def _mix1_fwd(x, g, full, late):
    job, keys = late.gather_job("mix1_in") if late else (None, [])
    (h, proj), got = _norm_proj(x, g, full[("sc_w_in", 0)], name="mix1_in", job=job)
    full.update(zip(keys, got))
    m = _conv_fwd(proj, full[("sc_conv_w", 0)])
    x2 = _mm(m, full[("sc_w_out", 0)], name="mix1_out", epilogue=lambda acc, xv: xv + acc, extras=[x])
    return x2, (x, h, proj, m)


def _mix1_bwd(dx2, saved, g, w_in, cw, w_out):
    x, h, proj, m = saved
    dm = _mm(dx2, w_out, tb=True, name="mix1_dm")
    dw_out = _mm(m, dx2, ta=True, name="mix1_dwout", out_dtype=WIRE_DTYPE)
    dproj, dcw = _conv_bwd(proj, cw, dm)
    dw_in = _mm(h, dproj, ta=True, name="mix1_dwin", out_dtype=WIRE_DTYPE)
    (dx, dg), _ = _proj_norm_bwd([(dproj, w_in, "DF")], x, g, dx2, name="mix1_dh")
    return dx, dg, dw_in, dcw, dw_out


def _loss_head(x, g, target):
    feat = x.shape[1]

    def fn(xv, gv, tv):
        err = _rms_fwd(xv, gv) - tv
        dx, dg = _rms_bwd(err / feat, xv, gv)
        return jnp.sum(err * err, keepdims=True) * (0.5 / feat), dx, dg

    return _rowmap(fn, [x, g, target], "rcr", [((1, 1), F32, "a"), (x.shape, F32, "r"), (g.shape, F32, "a")],
                   name="loss_head", tl=256)


def _slot(ref, place, chip=None, half=None):
    axis, width = place
    shape = list(ref.shape)
    start = [0, 0]
    if chip is not None:
        start[axis], shape[axis] = chip * width, width
    if half is not None:
        h_axis = 0 if shape[0] % 32 == 0 else 1
        shape[h_axis] //= 2
        start[h_axis] = start[h_axis] + half * shape[h_axis]
    hint = lambda s, d: s if isinstance(s, int) else pl.multiple_of(s, 128 if d == 1 else 8)
    return ref.at[tuple(pl.ds(hint(s, d), n) for d, (s, n) in enumerate(zip(start, shape)))]


class _Exchange:
    def __init__(self, kind, arrays, places):
        self.kind, self.arrays, self.places, self.n = kind, list(arrays), list(places), len(arrays)
        self.out_shape = []
        for t, (axis, width) in zip(self.arrays, self.places):
            if kind == "gather":
                shape = list(t.shape)
                shape[axis] = N_CHIPS * width
            else:
                shape = [N_CHIPS] + list(t.shape)
                shape[1 + axis] = width
            self.out_shape.append(jax.ShapeDtypeStruct(tuple(shape), t.dtype))
        n = self.n
        self.scratch = [pltpu.SemaphoreType.DMA((3 * n,)) for _ in range(4 if kind == "gather" else 2)]
        self.scratch.append(pltpu.SemaphoreType.DMA((n,)))

    def _copies(self, ins, outs, sems):
        x, y, c = lax.axis_index("x"), lax.axis_index("y"), lax.axis_index("c")
        peers = [(1 - x, y), (x, 1 - y), (1 - x, 1 - y)]
        remote = lambda src, dst, send, recv, k, to: pltpu.make_async_remote_copy(
            src_ref=src, dst_ref=dst, send_sem=send.at[k], recv_sem=recv.at[k], device_id=to, device_id_type=MESH_ID)
        local, ici, d2d = [], [], []
        for a in range(self.n):
            place = self.places[a]
            if self.kind == "gather":
                local.append(pltpu.make_async_copy(ins[a], _slot(outs[a], place, 2 * x + y), sems[4].at[a]))
                for r, (px, py) in enumerate(peers):
                    ici.append(remote(_slot(ins[a], place, None, c), _slot(outs[a], place, 2 * x + y, c),
                                      sems[0], sems[1], 3 * a + r, (px, py, c)))
                    landed = _slot(outs[a], place, 2 * px + py, c)
                    d2d.append(remote(landed, landed, sems[2], sems[3], 3 * a + r, (x, y, 1 - c)))
            else:
                local.append(pltpu.make_async_copy(_slot(ins[a], place, 2 * x + y), outs[a].at[3], sems[2].at[a]))
                for r, (px, py) in enumerate(peers):
                    ici.append(remote(_slot(ins[a], place, 2 * px + py), outs[a].at[r], sems[0], sems[1], 3 * a + r, (px, py, c)))
        return local, ici, d2d

    def start(self, ins, outs, sems):
        local, ici, _ = self._copies(ins, outs, sems)
        for cp in local + ici:
            cp.start()

    def relay(self, ins, outs, sems):
        _, ici, d2d = self._copies(ins, outs, sems)
        for arrived, onward in zip(ici, d2d):
            arrived.wait_recv()
            onward.start()

    def finish(self, ins, outs, sems):
        local, ici, d2d = self._copies(ins, outs, sems)
        for cp in local + d2d:
            cp.wait()
        for cp in ici:
            cp.wait_send() if d2d else cp.wait()


def _exchange_call(job, name):
    n = job.n

    def body(*refs):
        ins, outs, sems = refs[:n], refs[n:2 * n], refs[2 * n:]
        job.start(ins, outs, sems)
        job.relay(ins, outs, sems)
        job.finish(ins, outs, sems)

    return pl.pallas_call(
        body, name=name, in_specs=[ANY_SPEC] * n, out_specs=[ANY_SPEC] * n, out_shape=job.out_shape,
        scratch_shapes=job.scratch, compiler_params=pltpu.CompilerParams(has_side_effects=True),
    )(*job.arrays)


def _carried_call(body, *, name, grid, in_specs, out_specs, out_shape, semantics, operands, scratch_shapes=(), job=None):
    scratch_shapes = list(scratch_shapes)
    if job is None:
        return pl.pallas_call(body, name=name, grid=grid, in_specs=in_specs, out_specs=out_specs, out_shape=out_shape,
                              scratch_shapes=scratch_shapes, compiler_params=_params(*semantics))(*operands), []
    n_in, n_out, n, n_scr = len(in_specs), len(out_specs), job.n, len(scratch_shapes)
    steps = math.prod(grid)

    def wrapped(*refs):
        ins, job_ins = refs[:n_in], refs[n_in:n_in + n]
        outs, job_outs = refs[n_in + n:n_in + n + n_out], refs[n_in + n + n_out:n_in + 2 * n + n_out]
        outs = outs + refs[n_in + 2 * n + n_out:n_in + 2 * n + n_out + n_scr]
        sems = refs[n_in + 2 * n + n_out + n_scr:]
        step = functools.reduce(lambda acc, d: acc * grid[d] + pl.program_id(d), range(len(grid)), 0)

        @pl.when(step == 0)
        def _():
            job.start(job_ins, job_outs, sems)

        @pl.when(step == (3 * steps) // 4)
        def _():
            job.relay(job_ins, job_outs, sems)

        body(*ins, *outs)

        @pl.when(step == steps - 1)
        def _():
            job.finish(job_ins, job_outs, sems)

    res = pl.pallas_call(
        wrapped, name=name, grid=grid, in_specs=list(in_specs) + [ANY_SPEC] * n, out_specs=list(out_specs) + [ANY_SPEC] * n,
        out_shape=list(out_shape) + job.out_shape, scratch_shapes=scratch_shapes + job.scratch,
        compiler_params=pltpu.CompilerParams(dimension_semantics=("arbitrary",) * len(grid), vmem_limit_bytes=VMEM_LIMIT,
                                             has_side_effects=True),
    )(*operands, *job.arrays)
    return res[:n_out], res[n_out:]


def _swap_with_sibling(parts):
    n = len(parts)

    def body(*refs):
        ins, outs = refs[:n], refs[n:2 * n]
        send, recv = refs[2 * n:]
        sibling = (lax.axis_index("x"), lax.axis_index("y"), 1 - lax.axis_index("c"))
        copies = [pltpu.make_async_remote_copy(src_ref=ins[a], dst_ref=outs[a], send_sem=send.at[a], recv_sem=recv.at[a],
                                               device_id=sibling, device_id_type=MESH_ID) for a in range(n)]
        for cp in copies:
            cp.start()
        for cp in copies:
            cp.wait()

    return pl.pallas_call(
        body, name="swap_with_sibling",
        in_specs=[ANY_SPEC] * n, out_specs=[ANY_SPEC] * n,
        out_shape=[jax.ShapeDtypeStruct(p.shape, p.dtype) for p in parts],
        scratch_shapes=[pltpu.SemaphoreType.DMA((n,)), pltpu.SemaphoreType.DMA((n,))],
        compiler_params=pltpu.CompilerParams(has_side_effects=True),
    )(*parts)


def _sum_all_devices(t):
    rows = t.shape[0]

    def body(t_ref, o_ref, slots, send, recv):
        x, y, c = lax.axis_index("x"), lax.axis_index("y"), lax.axis_index("c")
        me = 4 * x + 2 * y + c
        slots[me] = t_ref[...]
        copies = []
        for m in range(1, 8):
            peer = (x ^ (m >> 2), y ^ ((m >> 1) & 1), c ^ (m & 1))
            cp = pltpu.make_async_remote_copy(src_ref=t_ref, dst_ref=slots.at[me], send_sem=send.at[m - 1],
                                              recv_sem=recv.at[m - 1], device_id=peer, device_id_type=MESH_ID)
            cp.start()
            copies.append(cp)
        for cp in copies:
            cp.wait()
        acc = slots[0]
        for dev in range(1, 8):
            acc = acc + slots[dev]
        o_ref[...] = acc

    vmem = pl.BlockSpec(memory_space=pltpu.VMEM)
    return pl.pallas_call(
        body, name="sum_all_devices", in_specs=[vmem], out_specs=vmem,
        out_shape=jax.ShapeDtypeStruct(t.shape, F32),
        scratch_shapes=[pltpu.VMEM((8, rows, 128), F32), pltpu.SemaphoreType.DMA((7,)), pltpu.SemaphoreType.DMA((7,))],
        compiler_params=pltpu.CompilerParams(vmem_limit_bytes=VMEM_LIMIT, has_side_effects=True),
    )(t)


def _adamw(w, g, m, v):
    m = ADAM_B1 * m + (1.0 - ADAM_B1) * g
    v = ADAM_B2 * v + (1.0 - ADAM_B2) * jnp.square(g)
    m_hat = m / (1.0 - ADAM_B1 ** ADAM_STEP)
    v_hat = v / (1.0 - ADAM_B2 ** ADAM_STEP)
    return -ADAM_LR * (m_hat / (jnp.sqrt(v_hat) + ADAM_EPS) + ADAM_WD * w), m, v


def _chip_sum(received, name):
    rows, cols = received.shape[1:]
    tl = _row_block(rows, 512, tile=32 // received.dtype.itemsize)

    def body(r_ref, o_ref):
        total = ((r_ref[0].astype(F32) + r_ref[1].astype(F32)) + r_ref[2].astype(F32)) + r_ref[3].astype(F32)
        o_ref[...] = total.astype(o_ref.dtype)

    return pl.pallas_call(body, name=name, grid=(rows // tl,),
                          in_specs=[pl.BlockSpec((N_CHIPS, tl, cols), lambda i: (0, i, 0))],
                          out_specs=pl.BlockSpec((tl, cols), lambda i: (i, 0)),
                          out_shape=jax.ShapeDtypeStruct((rows, cols), received.dtype),
                          compiler_params=_params("parallel"))(received)


def _adamw_layer(w, m, v, p_mine, p_other, layer, prev, name):
    _, rows, cols = w.shape
    assert p_mine.shape[1] == cols and p_mine.shape[0] >= rows
    tl = _row_block(rows, 512, tile=32 // p_mine.dtype.itemsize)
    tc = cols
    if tl < 128 < rows and cols % 256 == 0:
        tl, tc = rows, 256

    def body(w_ref, m_ref, v_ref, pa_ref, pb_ref, *rest):
        g = pa_ref[...].astype(F32) + pb_ref[...].astype(F32)
        for o_ref, val in zip(rest[-4:], (g,) + _adamw(w_ref[...], g, m_ref[...], v_ref[...])):
            o_ref[...] = val

    stacked = pl.BlockSpec((None, tl, tc), lambda i, j: (layer, i, j))
    part = pl.BlockSpec((tl, tc), lambda i, j: (i, j))
    kept = list(prev) if prev else []
    return pl.pallas_call(
        body, name=name, grid=(rows // tl, cols // tc),
        in_specs=[stacked] * 3 + [part] * 2 + [ANY_SPEC] * len(kept),
        out_specs=[stacked] * 4, out_shape=[jax.ShapeDtypeStruct(w.shape, F32)] * 4,
        input_output_aliases={5 + k: k for k in range(len(kept))},
        compiler_params=_params("parallel", "parallel"),
    )(w, m, v, p_mine, p_other, *kept)


def _adamw_small(w, g, m, v):
    def fn(wv, gv, mv, vv):
        return _adamw(wv, gv, mv, vv)

    return _rowmap(fn, [w, g, m, v], "rrrr", [(w.shape, F32, "r")] * 3, name="adamw_small", tl=w.shape[0])


WEIGHTS = ['ffn1_norm', 'ffn1_w_gate', 'ffn1_w_up', 'ffn1_w_down', 'mix_norm', 'ffn2_norm', 'ffn2_w_gate', 'ffn2_w_up',
           'ffn2_w_down', 'ab_w_in', 's5_lambda_re', 's5_lambda_im', 's5_log_dt', 's5_b_re', 's5_b_im', 's5_c_re', 's5_c_im',
           's5_d', 's5_w_glu', 'ab_w_out', 'sc_w_in', 'sc_conv_w', 'sc_w_out', 'final_norm']
SHARDED = {'ffn1_w_gate': (0, FF_SLOT), 'ffn1_w_up': (0, FF_SLOT), 'ffn1_w_down': (0, FF_SLOT),
           'ffn2_w_gate': (0, FF_SLOT), 'ffn2_w_up': (0, FF_SLOT), 'ffn2_w_down': (0, FF_SLOT),
           'ab_w_in': (1, 512), 's5_w_glu': (0, 128), 'ab_w_out': (0, 256), 'sc_w_in': (1, 768), 'sc_conv_w': (1, 256),
           'sc_w_out': (0, 256)}
SWAPPED = ('ffn1_w_gate', 'ffn1_w_up', 'ffn2_w_gate', 'ffn2_w_up')
SMALL = [n for n in WEIGHTS if n not in SHARDED]


def _held(name, t):
    return jnp.swapaxes(t, 1, 2) if name in SWAPPED else t


def _pack(arrays):
    rows = []
    for t in arrays:
        flat = t.reshape(-1)
        rows.append(jnp.pad(flat, (0, (-flat.shape[0]) % 128)))
    flat = jnp.concatenate(rows)
    return jnp.pad(flat, (0, (-flat.shape[0]) % 1024)).reshape(-1, 128)


def _unpack(packed, like):
    flat, out, pos = packed.reshape(-1), [], 0
    for t in like:
        out.append(flat[pos:pos + t.size].reshape(t.shape))
        pos += t.size + (-t.size) % 128
    return out


def _local_grads(x, target, p, full, late=None):
    small, grads, saved = {}, {}, []
    for layer in range(2):
        x, s1 = _ffn_fwd(x, p["ffn1_norm"][layer:layer + 1], full, "ffn1", layer, late)
        if layer == 0:
            x, sm = _mix0_fwd(x, p["mix_norm"][0:1], p, full, late)
        else:
            x, sm = _mix1_fwd(x, p["mix_norm"][1:2], full, late)
        x, s2 = _ffn_fwd(x, p["ffn2_norm"][layer:layer + 1], full, "ffn2", layer, late)
        saved.append((s1, sm, s2))
    loss, dx, dg_final = _loss_head(x, p["final_norm"][None], target)
    small["final_norm"] = dg_final[0]
    gains = {n: [None, None] for n in ("ffn1_norm", "mix_norm", "ffn2_norm")}

    def ffn_bwd(which, layer, dx, s):
        dx, dg = _ffn_bwd(dx, s, p[f"{which}_norm"][layer:layer + 1], full, which, layer, grads, late,
                          inline=(which, layer) in (("ffn2", 1), ("ffn1", 0)))
        gains[f"{which}_norm"][layer] = dg[0]
        return dx

    for layer in (1, 0):
        s1, sm, s2 = saved[layer]
        dx = ffn_bwd("ffn2", layer, dx, s2)
        if layer == 0:
            dx, dg, s5_grads = _mix0_bwd(dx, sm, p["mix_norm"][0:1], p, full, grads, late)
            small.update(s5_grads)
        else:
            dx, dg, dw_in, dcw, dw_out = _mix1_bwd(dx, sm, p["mix_norm"][1:2], full[("sc_w_in", 0)], full[("sc_conv_w", 0)],
                                                   full[("sc_w_out", 0)])
            grads.update({("sc_w_in", 0): dw_in, ("sc_conv_w", 0): dcw.astype(WIRE_DTYPE), ("sc_w_out", 0): dw_out})
        gains["mix_norm"][layer] = dg[0]
        dx = ffn_bwd("ffn1", layer, dx, s1)
    small.update({n: jnp.stack(pair) for n, pair in gains.items()})
    return loss, dx, small, grads


_GATHER_PLAN = {
    "gather_early": [("ffn1_w_gate", 0), ("ffn1_w_up", 0)],
    "ffn1_0_up": [("ffn1_w_down", 0), ("ab_w_in", 0)],
    "sb_fwd": [("s5_w_glu", 0), ("ab_w_out", 0), ("ffn2_w_gate", 0), ("ffn2_w_up", 0), ("ffn2_w_down", 0),
               ("ffn1_w_gate", 1), ("ffn1_w_up", 1), ("ffn1_w_down", 1)],
    "ffn2_0_up": [("sc_w_in", 0), ("sc_conv_w", 0), ("sc_w_out", 0)],
    "ffn1_1_up": [("ffn2_w_gate", 1), ("ffn2_w_up", 1)],
    "mix1_in": [("ffn2_w_down", 1)],
}


class _Late:
    def __init__(self, shards, places):
        self.shards, self.places = shards, places
        self.sent, self.received = set(), {}

    def gather_job(self, carrier):
        keys = _GATHER_PLAN.get(carrier, [])
        if not keys:
            return None, []
        return _Exchange("gather", [self.shards[k] for k in keys], [self.places[k] for k in keys]), keys

    def scatter_job(self, grads):
        keys = [k for k in grads if k not in self.sent]
        if not keys:
            return None, []
        self.sent.update(keys)
        return _Exchange("scatter", [grads[k] for k in keys], [self.places[k] for k in keys]), keys


def kernel(x, ffn1_norm, ffn1_w_gate, ffn1_w_up, ffn1_w_down, mix_norm, ffn2_norm, ffn2_w_gate, ffn2_w_up, ffn2_w_down, ab_w_in, s5_lambda_re, s5_lambda_im, s5_log_dt, s5_b_re, s5_b_im, s5_c_re, s5_c_im, s5_d, s5_w_glu, ab_w_out, sc_w_in, sc_conv_w, sc_w_out, final_norm, loss_target, m_ffn1_norm, m_ffn1_w_gate, m_ffn1_w_up, m_ffn1_w_down, m_mix_norm, m_ffn2_norm, m_ffn2_w_gate, m_ffn2_w_up, m_ffn2_w_down, m_ab_w_in, m_s5_lambda_re, m_s5_lambda_im, m_s5_log_dt, m_s5_b_re, m_s5_b_im, m_s5_c_re, m_s5_c_im, m_s5_d, m_s5_w_glu, m_ab_w_out, m_sc_w_in, m_sc_conv_w, m_sc_w_out, m_final_norm, v_ffn1_norm, v_ffn1_w_gate, v_ffn1_w_up, v_ffn1_w_down, v_mix_norm, v_ffn2_norm, v_ffn2_w_gate, v_ffn2_w_up, v_ffn2_w_down, v_ab_w_in, v_s5_lambda_re, v_s5_lambda_im, v_s5_log_dt, v_s5_b_re, v_s5_b_im, v_s5_c_re, v_s5_c_im, v_s5_d, v_s5_w_glu, v_ab_w_out, v_sc_w_in, v_sc_conv_w, v_sc_w_out, v_final_norm):
    args = dict(locals())
    p = {n: _held(n, args[n]) for n in WEIGHTS}
    mom = {n: _held(n, args["m_" + n]) for n in WEIGHTS}
    var = {n: _held(n, args["v_" + n]) for n in WEIGHTS}

    keys = [(n, layer) for n in SHARDED for layer in range(p[n].shape[0])]
    shards, places = {}, {}
    for n, layer in keys:
        axis, width = SHARDED[n]
        t = p[n][layer] if n == "sc_conv_w" else p[n][layer].astype(MXU_DTYPE)
        pad = [(0, 0), (0, 0)]
        pad[axis] = (0, width - t.shape[axis])
        shards[(n, layer)], places[(n, layer)] = jnp.pad(t, pad), (axis, width)
    late = _Late(shards, places)
    job, first = late.gather_job("gather_early")
    full = dict(zip(first, _exchange_call(job, "gather_early")))

    loss, dx, small, grads = _local_grads(x[0], loss_target[0], p, full, late)
    loss = lax.psum(loss[0, 0], ("x", "y", "c"))
    assert set(late.received) == set(keys), "a gradient was left without a carrier"

    partial = [_chip_sum(late.received[(n, layer)], name=f"chip_sum_{n}_{layer}") for n, layer in keys]
    other = _swap_with_sibling(partial)
    out = {}
    for (n, layer), mine, theirs in zip(keys, partial, other):
        out[n] = _adamw_layer(p[n], mom[n], var[n], mine, theirs, layer, out.get(n), name=f"adamw_{n}_{layer}")
    out = {n: [_held(n, t) for t in res] for n, res in out.items()}

    like = [p[n] for n in SMALL]
    g_small = _sum_all_devices(_pack([small[n] for n in SMALL]))
    d_small, m_small, v_small = _adamw_small(_pack(like), g_small, _pack([mom[n] for n in SMALL]), _pack([var[n] for n in SMALL]))
    for k, packed in enumerate((g_small, d_small, m_small, v_small)):
        for n, t in zip(SMALL, _unpack(packed, like)):
            out.setdefault(n, [None] * 4)[k] = t

    return (loss, dx[None], *[out[n][0] for n in WEIGHTS], *[out[n][1] for n in WEIGHTS],
            *[out[n][2] for n in WEIGHTS], *[out[n][3] for n in WEIGHTS])
```

```python
import functools
import math

import jax
import jax.numpy as jnp
from jax import lax
from jax.experimental import pallas as pl
from jax.experimental.pallas import tpu as pltpu

F32 = jnp.float32
MXU_DTYPE = jnp.bfloat16
WIRE_DTYPE = jnp.bfloat16
MESH_ID = pl.DeviceIdType.MESH

D_MODEL = 1024
D_FF = 2752
N_CHIPS = 4
FF_SHARD = D_FF // N_CHIPS
FF_SLOT = 768
FF_PAD = N_CHIPS * FF_SLOT
S5_WIDTH = 512
S5_GROUP = 16
S5_GROUPS = 32
S5_STATE = 64
S5_LANES = S5_GROUPS * S5_STATE
S5_BLOCK = 512
S5_DIAG = S5_LANES // S5_BLOCK
SB_HEADS = 8
SB_DH = 64
SB_SCALE = 0.125
SB_PACK = 2
SB_QUERIES = 1024
SB_KEYS = 256
EPS = 1e-6
ADAM_LR, ADAM_B1, ADAM_B2, ADAM_EPS, ADAM_WD, ADAM_STEP = 0.001, 0.9, 0.999, 1e-08, 0.01, 10
VMEM_LIMIT = 56 * 1024 * 1024

ANY_SPEC = pl.BlockSpec(memory_space=pl.ANY)


def _params(*sem):
    return pltpu.CompilerParams(dimension_semantics=sem or None, vmem_limit_bytes=VMEM_LIMIT)


def _mm(a, b, *, name, ta=False, tb=False, out_dtype=F32, epilogue=None, extras=(), tm=1024, tn=1024, tk=1024, job=None):
    m, k = (a.shape[1], a.shape[0]) if ta else a.shape
    n = b.shape[0] if tb else b.shape[1]
    tm, tn, tk = min(tm, m), min(tn, n), min(tk, k)
    assert m % tm == 0 and n % tn == 0 and k % tk == 0, (name, m, n, k)
    grid = (m // tm, n // tn, k // tk)
    a_spec = pl.BlockSpec((tk, tm), lambda i, j, kk: (kk, i)) if ta else pl.BlockSpec((tm, tk), lambda i, j, kk: (i, kk))
    b_spec = pl.BlockSpec((tn, tk), lambda i, j, kk: (j, kk)) if tb else pl.BlockSpec((tk, tn), lambda i, j, kk: (kk, j))
    nk = grid[2]
    ex_specs = []
    for e in extras:
        if e.shape == (m, n):
            ex_specs.append(pl.BlockSpec((tm, tn), lambda i, j, kk: (i, j)))
        elif e.shape == (1, n):
            ex_specs.append(pl.BlockSpec((1, tn), lambda i, j, kk: (0, j)))
        else:
            assert e.shape == (m, 1), (name, e.shape)
            ex_specs.append(pl.BlockSpec((tm, 1), lambda i, j, kk: (i, 0)))
    dims = (((0 if ta else 1,), (1 if tb else 0,)), ((), ()))
    n_ex = len(extras)

    out_dtypes = list(out_dtype) if isinstance(out_dtype, (list, tuple)) else [out_dtype]
    n_out = len(out_dtypes)

    def body(a_ref, b_ref, *rest):
        ex_refs, o_refs = rest[:n_ex], rest[n_ex:n_ex + n_out]

        def product():
            return lax.dot_general(a_ref[...].astype(MXU_DTYPE), b_ref[...].astype(MXU_DTYPE), dims, preferred_element_type=F32)

        def finish(r):
            if epilogue is not None:
                r = epilogue(r, *[e[...] for e in ex_refs])
            for o_ref, val in zip(o_refs, r if isinstance(r, (tuple, list)) else (r,)):
                o_ref[...] = val.astype(o_ref.dtype)

        if nk == 1:
            finish(product())
            return
        acc_ref, kk = rest[n_ex + n_out], pl.program_id(2)

        @pl.when(kk == 0)
        def _():
            acc_ref[...] = jnp.zeros_like(acc_ref)

        acc_ref[...] += product()

        @pl.when(kk == nk - 1)
        def _():
            finish(acc_ref[...])

    res, got = _carried_call(
        body, name=name, grid=grid,
        in_specs=[a_spec, b_spec, *ex_specs],
        out_specs=[pl.BlockSpec((tm, tn), lambda i, j, kk: (i, j))] * n_out,
        out_shape=[jax.ShapeDtypeStruct((m, n), dt) for dt in out_dtypes],
        scratch_shapes=[pltpu.VMEM((tm, tn), F32)] if nk > 1 else [],
        semantics=("parallel", "parallel", "arbitrary"), operands=(a, b, *extras), job=job)
    res = res if isinstance(out_dtype, (list, tuple)) else res[0]
    return res if job is None else (res, got)


def _row_block(rows, want, tile=8):
    for tl in range(min(want, rows), tile - 1, -1):
        if rows % tl == 0 and tl % tile == 0:
            return tl
    return rows


def _rowmap(fn, ins, in_kinds, outs, *, name, tl):
    rows = next(x.shape[0] for x, kd in zip(ins, in_kinds) if kd == "r")
    tl = _row_block(rows, tl)
    n_in = len(ins)

    def spec(shape, kind):
        if kind == "r":
            return pl.BlockSpec((tl,) + tuple(shape[1:]), lambda i: (i,) + (0,) * (len(shape) - 1))
        return pl.BlockSpec(tuple(shape), lambda i: (0,) * len(shape))

    def body(*refs):
        in_refs, out_refs = refs[:n_in], refs[n_in:]
        res = fn(*[r[...] for r in in_refs])
        if not isinstance(res, (tuple, list)):
            res = (res,)
        for o_ref, val, (_, dt, kind) in zip(out_refs, res, outs):
            if kind == "r":
                o_ref[...] = val.astype(dt)
            else:
                @pl.when(pl.program_id(0) == 0)
                def _():
                    o_ref[...] = jnp.zeros_like(o_ref)

                o_ref[...] += val.astype(dt)

    has_acc = any(kd == "a" for _, _, kd in outs)
    res = pl.pallas_call(
        body, name=name, grid=(rows // tl,),
        in_specs=[spec(x.shape, kd) for x, kd in zip(ins, in_kinds)],
        out_specs=[spec(s, kd) for s, _, kd in outs],
        out_shape=[jax.ShapeDtypeStruct(s, dt) for s, dt, _ in outs],
        compiler_params=_params("arbitrary" if has_acc else "parallel"),
    )(*ins)
    return res[0] if len(outs) == 1 else res


def _rms_fwd(x, g):
    r = lax.rsqrt(jnp.mean(x * x, axis=-1, keepdims=True) + EPS)
    return x * r * g


def _rms_bwd(dh, x, g):
    r = lax.rsqrt(jnp.mean(x * x, axis=-1, keepdims=True) + EPS)
    xh = x * r
    dxh = dh * g
    dx = r * (dxh - xh * jnp.mean(dxh * xh, axis=-1, keepdims=True))
    return dx, jnp.sum(dh * xh, axis=0, keepdims=True)


def _swiglu_act(a, b):
    return jax.nn.silu(a) * b


def _ffn_up(x, g, wg, wu, *, name, tm=1024, tn=1024, job=None):
    m, d = x.shape
    n = wg.shape[0]
    tm, tn = min(tm, m), min(tn, n)
    assert m % tm == 0 and n % tn == 0, (name, m, n)

    def body(x_ref, g_ref, wg_ref, wu_ref, h_ref, a_ref, b_ref, s_ref):
        @pl.when(pl.program_id(1) == 0)
        def _():
            h_ref[...] = _rms_fwd(x_ref[...], g_ref[...]).astype(h_ref.dtype)

        hv = h_ref[...]
        av = lax.dot_general(hv, wg_ref[...], NT_DIMS, preferred_element_type=F32)
        bv = lax.dot_general(hv, wu_ref[...], NT_DIMS, preferred_element_type=F32)
        a_ref[...] = av.astype(a_ref.dtype)
        b_ref[...] = bv.astype(b_ref.dtype)
        s_ref[...] = _swiglu_act(av, bv).astype(s_ref.dtype)

    rows = pl.BlockSpec((tm, d), lambda i, j: (i, 0))
    wgt = pl.BlockSpec((tn, d), lambda i, j: (j, 0))
    tile = pl.BlockSpec((tm, tn), lambda i, j: (i, j))
    return _carried_call(
        body, name=name, grid=(m // tm, n // tn),
        in_specs=[rows, pl.BlockSpec((1, d), lambda i, j: (0, 0)), wgt, wgt],
        out_specs=[rows, tile, tile, tile],
        out_shape=[jax.ShapeDtypeStruct((m, d), MXU_DTYPE)] + [jax.ShapeDtypeStruct((m, n), MXU_DTYPE)] * 3,
        semantics=("parallel", "arbitrary"), operands=(x, g, wg, wu), job=job)


def _norm_proj(x, g, w, *, name, tm=1024, tn=1024, job=None):
    m, d = x.shape
    n = w.shape[1]
    tm, tn = min(tm, m), min(tn, n)
    assert m % tm == 0 and n % tn == 0, (name, m, n)

    def body(x_ref, g_ref, w_ref, h_ref, o_ref):
        @pl.when(pl.program_id(1) == 0)
        def _():
            h_ref[...] = _rms_fwd(x_ref[...], g_ref[...]).astype(h_ref.dtype)

        o_ref[...] = jnp.dot(h_ref[...], w_ref[...], preferred_element_type=F32)

    rows = pl.BlockSpec((tm, d), lambda i, j: (i, 0))
    return _carried_call(
        body, name=name, grid=(m // tm, n // tn),
        in_specs=[rows, pl.BlockSpec((1, d), lambda i, j: (0, 0)), pl.BlockSpec((d, tn), lambda i, j: (0, j))],
        out_specs=[rows, pl.BlockSpec((tm, tn), lambda i, j: (i, j))],
        out_shape=[jax.ShapeDtypeStruct((m, d), MXU_DTYPE), jax.ShapeDtypeStruct((m, n), F32)],
        semantics=("parallel", "arbitrary"), operands=(x, g, w), job=job)


def _proj_norm_bwd(pairs, x, g, dres, *, name, tm=1024, tk=1024, job=None):
    m, f = pairs[0][0].shape
    d = x.shape[1]
    tm, tk = min(tm, m), min(tk, f)
    assert m % tm == 0 and f % tk == 0, (name, m, f)
    nk, n_pairs = f // tk, len(pairs)
    swapped = [kept == "FD" for _, _, kept in pairs]

    def body(*refs):
        dy_refs, w_refs = refs[:n_pairs], refs[n_pairs:2 * n_pairs]
        x_ref, g_ref, dr_ref, dx_ref, dg_ref = refs[2 * n_pairs:2 * n_pairs + 5]
        i, kk = pl.program_id(0), pl.program_id(1)

        part = None
        for dy_ref, w_ref, rows_are_f in zip(dy_refs, w_refs, swapped):
            dims = (((1,), (0,)), ((), ())) if rows_are_f else NT_DIMS
            term = lax.dot_general(dy_ref[...].astype(MXU_DTYPE), w_ref[...], dims, preferred_element_type=F32)
            part = term if part is None else part + term

        @pl.when(jnp.logical_and(i == 0, kk == 0))
        def _():
            dg_ref[...] = jnp.zeros_like(dg_ref)

        def finish(dh):
            dx, dg = _rms_bwd(dh, x_ref[...], g_ref[...])
            dx_ref[...] = dx + dr_ref[...]
            dg_ref[...] += dg

        if nk == 1:
            finish(part)
            return
        acc_ref = refs[2 * n_pairs + 5]

        @pl.when(kk == 0)
        def _():
            acc_ref[...] = jnp.zeros_like(acc_ref)

        acc_ref[...] += part

        @pl.when(kk == nk - 1)
        def _():
            finish(acc_ref[...])

    act = pl.BlockSpec((tm, tk), lambda i, kk: (i, kk))
    w_specs = [pl.BlockSpec((tk, d), lambda i, kk: (kk, 0)) if s else pl.BlockSpec((d, tk), lambda i, kk: (0, kk)) for s in swapped]
    rows = pl.BlockSpec((tm, d), lambda i, kk: (i, 0))
    one = pl.BlockSpec((1, d), lambda i, kk: (0, 0))
    return _carried_call(
        body, name=name, grid=(m // tm, nk),
        in_specs=[act] * n_pairs + w_specs + [rows, one, rows],
        out_specs=[rows, one],
        out_shape=[jax.ShapeDtypeStruct((m, d), F32), jax.ShapeDtypeStruct((1, d), F32)],
        scratch_shapes=[pltpu.VMEM((tm, d), F32)] if nk > 1 else [],
        semantics=("arbitrary", "arbitrary"), operands=(*[p[0] for p in pairs], *[p[1] for p in pairs], x, g, dres), job=job)


def _ffn_dx(da, db, wg, wu, x, g, dres, *, name, job=None):
    return _proj_norm_bwd([(da, wg, "FD"), (db, wu, "FD")], x, g, dres, name=name, tm=512, tk=FF_PAD, job=job)


def _ffn_fwd(x, g, full, which, layer, late):
    tag = f"{which}_{layer}"
    job, keys = late.gather_job(f"{tag}_up") if late else (None, [])
    (h, a, b, s), got = _ffn_up(x, g, full[(f"{which}_w_gate", layer)], full[(f"{which}_w_up", layer)], name=f"{tag}_up", job=job)
    full.update(zip(keys, got))
    x2 = _mm(s, full[(f"{which}_w_down", layer)], name=f"{tag}_down", epilogue=lambda acc, xv: xv + 0.5 * acc, extras=[x],
             tk=FF_PAD)
    return x2, (x, h, a, b, s)


def _ffn_bwd(dx2, saved, g, full, which, layer, grads, late, inline):
    x, h, a, b, s = saved
    tag = f"{which}_{layer}"
    kg, ku, kd = [(f"{which}_w_{n}", layer) for n in ("gate", "up", "down")]
    wg, wu, wd = full[kg], full[ku], full[kd]
    send = (lambda: late.scatter_job(grads)) if (late and inline) else (lambda: (None, []))

    def act_bwd(ds, av, bv):
        a32, b32, half = av.astype(F32), bv.astype(F32), 0.5 * ds
        sig = jax.nn.sigmoid(a32)
        return half * b32 * (sig * (1.0 + a32 * (1.0 - sig))), half * (a32 * sig)

    grads[kd] = _mm(s, dx2, ta=True, name=f"{tag}_dwd", out_dtype=WIRE_DTYPE, epilogue=lambda acc: 0.5 * acc, tk=2048)
    job, keys = send()
    (da, db), got = _carried(_mm, dx2, wd, tb=True, name=f"{tag}_dact", epilogue=act_bwd, extras=[a, b],
                             out_dtype=[MXU_DTYPE, MXU_DTYPE], job=job)
    _note(late, keys, got)
    grads[kg] = _mm(da, h, ta=True, name=f"{tag}_dwg", out_dtype=WIRE_DTYPE, tk=4096)
    job, keys = send()
    grads[ku], got = _carried(_mm, db, h, ta=True, name=f"{tag}_dwu", out_dtype=WIRE_DTYPE, tk=4096, job=job)
    _note(late, keys, got)
    job, keys = send()
    (dx, dg), got = _ffn_dx(da, db, wg, wu, x, g, dx2, name=f"{tag}_dx", job=job)
    _note(late, keys, got)
    return dx, dg


def _carried(fn, *args, job, **kwargs):
    return fn(*args, job=job, **kwargs) if job is not None else (fn(*args, **kwargs), [])


def _note(late, keys, got):
    if late:
        late.received.update(zip(keys, got))


def _softplus(z):
    return jnp.maximum(z, 0.0) + jnp.log(1.0 + jnp.exp(-jnp.abs(z)))


def _ones_dot(x, tri):
    if MXU_DTYPE == F32:
        return jnp.dot(x, tri, preferred_element_type=F32)
    hi = x.astype(MXU_DTYPE)
    lo = (x - hi.astype(F32)).astype(MXU_DTYPE)
    return jnp.dot(hi, tri, preferred_element_type=F32) + jnp.dot(lo, tri, preferred_element_type=F32)


NT_DIMS = (((1,), (1,)), ((), ()))
TN_DIMS = (((0,), (0,)), ((), ()))


SB_LANES = SB_PACK * SB_DH
Q_COL, K_COL, V_COL = (S5_WIDTH * n // SB_LANES for n in (1, 2, 3))


def _head_lanes(rows, hd):
    return lax.broadcasted_iota(jnp.int32, (rows, SB_LANES), 1) // SB_DH == hd


def _attend(proj, *, tq=SB_QUERIES, job=None):
    seq = proj.shape[0]
    tq = min(tq, seq)
    tk = min(SB_KEYS, tq)
    per, hp = tq // tk, SB_PACK

    def body(q_ref, k_ref, v_ref, o_ref, ls_ref):
        i = pl.program_id(1)
        r_idx = lax.broadcasted_iota(jnp.int32, (tk, tk), 0)
        c_idx = lax.broadcasted_iota(jnp.int32, (tk, tk), 1)
        after = (r_idx > c_idx).astype(MXU_DTYPE)
        lanes = [_head_lanes(tk, hd) for hd in range(hp)]

        def block(j, cs, acc, straddles):
            off = pl.multiple_of(j * tk, tk)
            k2, v2 = k_ref[pl.ds(off, tk), :], v_ref[pl.ds(off, tk), :]
            top = 0 if straddles is None else straddles * tk
            rows = tq - top
            q2 = (q_ref[pl.ds(top, rows), :] * SB_SCALE).astype(MXU_DTYPE)
            new_cs, out = [], acc[top:]
            for hd in range(hp):
                kv = jnp.where(lanes[hd], k2, 0.0).astype(MXU_DTYPE)
                vv = jnp.where(lanes[hd], v2, 0.0).astype(MXU_DTYPE)
                z = lax.dot_general(q2, kv, NT_DIMS, preferred_element_type=F32)
                sp = _softplus(z)
                c_in = cs[hd][top:]
                if straddles is None:
                    lk = -sp
                    w = jnp.exp(z - sp + _ones_dot(lk, after) + c_in)
                else:
                    before = lax.broadcasted_iota(jnp.int32, (rows, tk), 1) < lax.broadcasted_iota(jnp.int32, (rows, tk), 0)
                    lk = jnp.where(before, -sp, 0.0)
                    w = jnp.where(before, jnp.exp(z - sp + _ones_dot(lk, after) + c_in), 0.0)
                out = out + jnp.dot(w.astype(MXU_DTYPE), vv, preferred_element_type=F32)
                c_new = c_in + jnp.sum(lk, axis=1, keepdims=True)
                new_cs.append(jnp.concatenate([cs[hd][:top], c_new], axis=0) if top else c_new)
            return tuple(new_cs), (jnp.concatenate([acc[:top], out], axis=0) if top else out)

        carry = (tuple(jnp.zeros((tq, 1), F32) for _ in range(hp)), jnp.zeros((tq, SB_LANES), F32))
        for s in reversed(range(per)):
            carry = block(i * per + s, *carry, s)
        cs, acc = lax.fori_loop(0, i * per, lambda n, cr: block(i * per - 1 - n, *cr, None), carry)
        o_ref[...] = acc
        for hd in range(hp):
            ls_ref[hd] = cs[hd]

    whole = lambda col: pl.BlockSpec((seq, SB_LANES), lambda g, i: (0, col + g))
    return _carried_call(
        body, name="sb_fwd", grid=(SB_HEADS // hp, seq // tq),
        in_specs=[pl.BlockSpec((tq, SB_LANES), lambda g, i: (i, Q_COL + g)), whole(K_COL), whole(V_COL)],
        out_specs=[pl.BlockSpec((tq, SB_LANES), lambda g, i: (i, g)), pl.BlockSpec((hp, tq, 1), lambda g, i: (g, i, 0))],
        out_shape=[jax.ShapeDtypeStruct((seq, SB_HEADS * SB_DH), F32), jax.ShapeDtypeStruct((SB_HEADS, seq, 1), F32)],
        semantics=("parallel", "parallel"), operands=(proj, proj, proj), job=job)


def _attend_bwd(proj, lsum, dmix, *, tq=SB_QUERIES, job=None):
    seq = proj.shape[0]
    tq = min(tq, seq)
    tk = min(SB_KEYS, tq)
    per, hp = tq // tk, SB_PACK
    do_col = S5_WIDTH // SB_LANES

    def body(q_ref, k_ref, v_ref, ls_ref, do_ref, dq_ref, dk_ref, dv_ref):
        i = pl.program_id(1)

        @pl.when(i == 0)
        def _():
            dk_ref[...] = jnp.zeros_like(dk_ref)
            dv_ref[...] = jnp.zeros_like(dv_ref)

        r_idx = lax.broadcasted_iota(jnp.int32, (tk, tk), 0)
        c_idx = lax.broadcasted_iota(jnp.int32, (tk, tk), 1)
        upto = (r_idx <= c_idx).astype(MXU_DTYPE)
        before = (r_idx < c_idx).astype(MXU_DTYPE)
        lanes = [_head_lanes(tk, hd) for hd in range(hp)]

        def block(j, sums, dq, straddles):
            off = pl.multiple_of(j * tk, tk)
            k2, v2 = k_ref[pl.ds(off, tk), :], v_ref[pl.ds(off, tk), :]
            top = 0 if straddles is None else straddles * tk
            rows = tq - top
            part = pl.ds(top, rows)
            q2 = (q_ref[part, :] * SB_SCALE).astype(MXU_DTYPE)
            do2 = do_ref[part, :].astype(MXU_DTYPE)
            valid = None
            if straddles is not None:
                valid = lax.broadcasted_iota(jnp.int32, (rows, tk), 1) < lax.broadcasted_iota(jnp.int32, (rows, tk), 0)
            new_sums, out, dk, dv = [], dq[top:], jnp.zeros((tk, SB_LANES), F32), jnp.zeros((tk, SB_LANES), F32)
            for hd in range(hp):
                cp, ce = sums[hd]
                kv = jnp.where(lanes[hd], k2, 0.0).astype(MXU_DTYPE)
                vv = jnp.where(lanes[hd], v2, 0.0).astype(MXU_DTYPE)
                z = lax.dot_general(q2, kv, NT_DIMS, preferred_element_type=F32)
                sp = _softplus(z)
                lk = -sp if valid is None else jnp.where(valid, -sp, 0.0)
                w = jnp.exp(z - sp + (ls_ref[hd, part, :] - cp[top:]) - _ones_dot(lk, upto))
                if valid is not None:
                    w = jnp.where(valid, w, 0.0)
                e = w * lax.dot_general(do2, vv, NT_DIMS, preferred_element_type=F32)
                earlier = jnp.dot(e.astype(MXU_DTYPE), before, preferred_element_type=F32) + ce[top:]
                keep = jnp.exp(-sp)
                dz = e * keep - (1.0 - keep) * earlier
                if valid is not None:
                    dz = jnp.where(valid, dz, 0.0)
                dzm = dz.astype(MXU_DTYPE)
                out = out + jnp.dot(dzm, kv, preferred_element_type=F32)
                dk = dk + jnp.where(lanes[hd], lax.dot_general(dzm, q2, TN_DIMS, preferred_element_type=F32), 0.0)
                dv = dv + jnp.where(lanes[hd], lax.dot_general(w.astype(MXU_DTYPE), do2, TN_DIMS, preferred_element_type=F32), 0.0)
                new = (cp[top:] + jnp.sum(lk, axis=1, keepdims=True), ce[top:] + jnp.sum(e, axis=1, keepdims=True))
                new_sums.append(tuple(jnp.concatenate([old[:top], val], axis=0) for old, val in zip((cp, ce), new)) if top else new)
            dk_ref[pl.ds(off, tk), :] += dk
            dv_ref[pl.ds(off, tk), :] += dv
            return tuple(new_sums), (jnp.concatenate([dq[:top], out], axis=0) if top else out)

        zero = jnp.zeros((tq, 1), F32)
        carry = (tuple((zero, zero) for _ in range(hp)), jnp.zeros((tq, SB_LANES), F32))
        carry = lax.fori_loop(0, i * per, lambda j, cr: block(j, *cr, None), carry)
        for s in range(per):
            carry = block(i * per + s, *carry, s)
        dq_ref[...] = carry[1] * SB_SCALE

    whole = lambda col: pl.BlockSpec((seq, SB_LANES), lambda g, i: (0, col + g))
    tile = lambda col: pl.BlockSpec((tq, SB_LANES), lambda g, i: (i, col + g))
    acc = pl.BlockSpec((seq, SB_LANES), lambda g, i: (0, g))
    return _carried_call(
        body, name="sb_bwd", grid=(SB_HEADS // hp, seq // tq),
        in_specs=[tile(Q_COL), whole(K_COL), whole(V_COL), pl.BlockSpec((hp, tq, 1), lambda g, i: (g, i, 0)), tile(do_col)],
        out_specs=[tile(0), acc, acc],
        out_shape=[jax.ShapeDtypeStruct((seq, SB_HEADS * SB_DH), F32)] * 3,
        semantics=("parallel", "arbitrary"), operands=(proj, proj, proj, lsum, dmix), job=job)


def _s5_disc(lr, li, ldt, br, bi):
    dt = jnp.exp(ldt)
    mag = jnp.exp(lr * dt)
    ar = mag * jnp.cos(li * dt)
    ai = mag * jnp.sin(li * dt)
    den = lr * lr + li * li
    nr = ar - 1.0
    cr = (nr * lr + ai * li) / den
    ci = (ai * lr - nr * li) / den
    return ar, ai, cr[None] * br - ci[None] * bi, cr[None] * bi + ci[None] * br


def _s5_prep(lr, li, ldt, br, bi):
    shapes = [lr.shape, lr.shape, br.shape, br.shape]

    def body(lr_ref, li_ref, ldt_ref, br_ref, bi_ref, *outs):
        for o, val in zip(outs, _s5_disc(lr_ref[...], li_ref[...], ldt_ref[...], br_ref[...], bi_ref[...])):
            o[...] = val

    return pl.pallas_call(body, name="s5_prep", out_shape=[jax.ShapeDtypeStruct(s, F32) for s in shapes])(lr, li, ldt, br, bi)


def _s5_prep_bwd(lr, li, ldt, br, bi, cts):
    args = (lr, li, ldt, br, bi)

    def body(*refs):
        ins, ct_refs, outs = refs[:5], refs[5:9], refs[9:]
        _, vjp = jax.vjp(_s5_disc, *[r[...] for r in ins])
        for o, val in zip(outs, vjp(tuple(r[...] for r in ct_refs))):
            o[...] = val

    return pl.pallas_call(body, name="s5_prep_bwd", out_shape=[jax.ShapeDtypeStruct(a.shape, F32) for a in args])(*args, *cts)


SCAN_ROWS = 8


def _powers(ar, ai):
    out = [(ar, ai)]
    for _ in range(SCAN_ROWS - 1):
        pr, pi = out[-1]
        out.append((pr * ar - pi * ai, pr * ai + pi * ar))
    return out


def _s5_states(u, bmat, cmat, a, d, *, tc=512):
    seq, width = u.shape
    nj, cols, w2 = bmat.shape
    tw = w2 // 2
    tc = min(tc, seq)
    assert seq % tc == 0 and nj * cols == width and tw == S5_BLOCK

    def body(u_ref, bm_ref, cm_ref, a_ref, d_ref, h_ref, y_ref, cr_ref, ci_ref):
        @pl.when(pl.program_id(1) == 0)
        def _():
            cr_ref[...] = jnp.zeros_like(cr_ref)
            ci_ref[...] = jnp.zeros_like(ci_ref)

        uv = u_ref[...]
        h_ref[...] = jnp.dot(uv.astype(MXU_DTYPE), bm_ref[0], preferred_element_type=F32)
        re, im = pl.ds(0, tw), pl.ds(tw, tw)
        powers = _powers(a_ref[:, re], a_ref[:, im])
        pr = jnp.concatenate([p[0] for p in powers], axis=0)
        pi = jnp.concatenate([p[1] for p in powers], axis=0)
        row_id = lax.broadcasted_iota(jnp.int32, (SCAN_ROWS, tw), 0)
        reach = {dist: tuple(jnp.where(row_id >= dist, part, 0.0) for part in powers[dist - 1]) for dist in (1, 2, 4)}

        def block(n, carry):
            hr, hi = carry
            rows = pl.ds(pl.multiple_of(n * SCAN_ROWS, SCAN_ROWS), SCAN_ROWS)
            yr, yi = h_ref[rows, re], h_ref[rows, im]
            for dist in (1, 2, 4):
                cr, ci = reach[dist]
                sr, si = pltpu.roll(yr, dist, 0), pltpu.roll(yi, dist, 0)
                yr, yi = yr + cr * sr - ci * si, yi + cr * si + ci * sr
            yr, yi = yr + pr * hr - pi * hi, yi + pr * hi + pi * hr
            h_ref[rows, re] = yr
            h_ref[rows, im] = yi
            return yr[SCAN_ROWS - 1:], yi[SCAN_ROWS - 1:]

        hr, hi = lax.fori_loop(0, tc // SCAN_ROWS, block, (cr_ref[...], ci_ref[...]), unroll=4)
        cr_ref[...] = hr
        ci_ref[...] = hi
        y_ref[...] = jnp.dot(h_ref[...].astype(MXU_DTYPE), cm_ref[0], preferred_element_type=F32) + d_ref[...] * uv

    io = pl.BlockSpec((tc, cols), lambda j, t: (t, j))
    return pl.pallas_call(
        body, name="s5_states", grid=(nj, seq // tc),
        in_specs=[io, pl.BlockSpec((1, cols, w2), lambda j, t: (j, 0, 0)), pl.BlockSpec((1, w2, cols), lambda j, t: (j, 0, 0)),
                  pl.BlockSpec((1, w2), lambda j, t: (0, j)), pl.BlockSpec((1, cols), lambda j, t: (0, j))],
        out_specs=[pl.BlockSpec((tc, w2), lambda j, t: (t, j)), io],
        out_shape=[jax.ShapeDtypeStruct((seq, nj * w2), F32), jax.ShapeDtypeStruct((seq, width), F32)],
        scratch_shapes=[pltpu.VMEM((1, tw), F32)] * 2,
        compiler_params=_params("parallel", "arbitrary"),
    )(u, bmat, cmat, a, d)


def _s5_states_bwd(dy, h, u, bmat, cmat, a, du_skip, *, tc=512):
    seq, width = u.shape
    nj, cols, w2 = bmat.shape
    tw = w2 // 2
    tc = min(tc, seq)
    assert seq % tc == 0
    nt = seq // tc

    def body(dy_ref, h_ref, u_ref, bm_ref, cm_ref, a_ref, sk_ref, du_ref, da_ref, db_ref, dc_ref, g_ref, cr_ref, ci_ref):
        @pl.when(pl.program_id(1) == 0)
        def _():
            cr_ref[...] = jnp.zeros_like(cr_ref)
            ci_ref[...] = jnp.zeros_like(ci_ref)
            da_ref[...] = jnp.zeros_like(da_ref)
            db_ref[...] = jnp.zeros_like(db_ref)
            dc_ref[...] = jnp.zeros_like(dc_ref)

        dyv = dy_ref[...].astype(MXU_DTYPE)
        g_ref[...] = lax.dot_general(dyv, cm_ref[0], NT_DIMS, preferred_element_type=F32)
        re, im = pl.ds(0, tw), pl.ds(tw, tw)
        powers = _powers(a_ref[:, re], a_ref[:, im])
        pr = jnp.concatenate([p[0] for p in reversed(powers)], axis=0)
        pi = jnp.concatenate([p[1] for p in reversed(powers)], axis=0)
        row_id = lax.broadcasted_iota(jnp.int32, (SCAN_ROWS, tw), 0)
        last = SCAN_ROWS - 1
        reach = {dist: tuple(jnp.where(row_id < SCAN_ROWS - dist, part, 0.0) for part in powers[dist - 1]) for dist in (1, 2, 4)}

        def block(n, carry):
            gr, gi, sr, si = carry
            rows = pl.ds(pl.multiple_of((tc // SCAN_ROWS - 1 - n) * SCAN_ROWS, SCAN_ROWS), SCAN_ROWS)
            yr, yi = g_ref[rows, re], g_ref[rows, im]
            for dist in (1, 2, 4):
                cr, ci = reach[dist]
                ur, ui = pltpu.roll(yr, SCAN_ROWS - dist, 0), pltpu.roll(yi, SCAN_ROWS - dist, 0)
                yr, yi = yr + cr * ur + ci * ui, yi + cr * ui - ci * ur
            yr, yi = yr + pr * gr + pi * gi, yi + pr * gi - pi * gr
            g_ref[rows, re] = yr
            g_ref[rows, im] = yi
            nr = jnp.where(row_id < last, pltpu.roll(yr, last, 0), gr)
            ni = jnp.where(row_id < last, pltpu.roll(yi, last, 0), gi)
            hr, hi = h_ref[rows, re], h_ref[rows, im]
            return yr[:1], yi[:1], sr + nr * hr + ni * hi, si + ni * hr - nr * hi

        zero = jnp.zeros((SCAN_ROWS, tw), F32)
        gr, gi, sr, si = lax.fori_loop(0, tc // SCAN_ROWS, block, (cr_ref[...], ci_ref[...], zero, zero), unroll=4)
        cr_ref[...] = gr
        ci_ref[...] = gi
        da_ref[:, re] += jnp.sum(sr, axis=0, keepdims=True)
        da_ref[:, im] += jnp.sum(si, axis=0, keepdims=True)
        gv = g_ref[...].astype(MXU_DTYPE)
        du_ref[...] = (lax.dot_general(gv, bm_ref[0], NT_DIMS, preferred_element_type=F32) + sk_ref[...]).astype(du_ref.dtype)
        db_ref[0] += lax.dot_general(u_ref[...].astype(MXU_DTYPE), gv, TN_DIMS, preferred_element_type=F32)
        dc_ref[0] += lax.dot_general(h_ref[...].astype(MXU_DTYPE), dyv, TN_DIMS, preferred_element_type=F32)

    io = pl.BlockSpec((tc, cols), lambda j, t: (nt - 1 - t, j))
    bm = pl.BlockSpec((1, cols, w2), lambda j, t: (j, 0, 0))
    cm = pl.BlockSpec((1, w2, cols), lambda j, t: (j, 0, 0))
    row = pl.BlockSpec((1, w2), lambda j, t: (0, j))
    return pl.pallas_call(
        body, name="s5_states_bwd", grid=(nj, nt),
        in_specs=[io, pl.BlockSpec((tc, w2), lambda j, t: (nt - 1 - t, j)), io, bm, cm, row, io],
        out_specs=[io, row, bm, cm],
        out_shape=[jax.ShapeDtypeStruct((seq, width), MXU_DTYPE), jax.ShapeDtypeStruct((1, nj * w2), F32),
                   jax.ShapeDtypeStruct(bmat.shape, F32), jax.ShapeDtypeStruct(cmat.shape, F32)],
        scratch_shapes=[pltpu.VMEM((tc, w2), F32)] + [pltpu.VMEM((1, tw), F32)] * 2,
        compiler_params=_params("parallel", "arbitrary"),
    )(dy, h, u, bmat, cmat, a, du_skip)


def _pair_columns(re, im, axis):
    shape = re.shape
    split = shape[:axis] + (shape[axis] // S5_BLOCK, S5_BLOCK) + shape[axis + 1:]
    both = jnp.stack([re.reshape(split), im.reshape(split)], axis=axis + 1)
    return both.reshape(shape[:axis] + (2 * shape[axis],) + shape[axis + 1:])


def _unpair_columns(t, axis):
    shape = t.shape
    both = t.reshape(shape[:axis] + (shape[axis] // (2 * S5_BLOCK), 2, S5_BLOCK) + shape[axis + 1:])
    half = shape[:axis] + (shape[axis] // 2,) + shape[axis + 1:]
    return (lax.index_in_dim(both, 0, axis + 1, keepdims=False).reshape(half),
            lax.index_in_dim(both, 1, axis + 1, keepdims=False).reshape(half))


S5_PER_BLOCK = S5_GROUPS // S5_DIAG


def _block_diag(t):
    g, a, b = t.shape
    n = S5_PER_BLOCK
    eye = jnp.eye(n, dtype=t.dtype)
    return (t.reshape(g // n, n, a, 1, b) * eye[None, :, None, :, None]).reshape(g // n, n * a, n * b)


def _block_diag_part(m):
    j, n = m.shape[0], S5_PER_BLOCK
    a, b = m.shape[1] // n, m.shape[2] // n
    return jnp.moveaxis(jnp.diagonal(m.reshape(j, n, a, n, b), axis1=1, axis2=3), -1, 1).reshape(j * n, a, b)


def _gelu_glu(y, gate_pre):
    z = jax.nn.gelu(y)
    return z * jax.nn.sigmoid(gate_pre)


def _s5_fwd(u, p, w_glu):
    lr, li = p["s5_lambda_re"][0], p["s5_lambda_im"][0]
    ldt = p["s5_log_dt"][0][:, None]
    br = p["s5_b_re"][0].transpose(2, 0, 1)
    bi = p["s5_b_im"][0].transpose(2, 0, 1)
    ar, ai, bbr, bbi = _s5_prep(lr, li, ldt, br, bi)
    a = _pair_columns(ar.reshape(1, S5_LANES), ai.reshape(1, S5_LANES), 1)
    bmat = jnp.concatenate([_block_diag(bbr.transpose(1, 0, 2)), _block_diag(bbi.transpose(1, 0, 2))], axis=2)
    cmat = jnp.concatenate([_block_diag(p["s5_c_re"][0].transpose(0, 2, 1)),
                            -_block_diag(p["s5_c_im"][0].transpose(0, 2, 1))], axis=1)
    bmat, cmat = bmat.astype(MXU_DTYPE), cmat.astype(MXU_DTYPE)
    d = p["s5_d"]
    h, y = _s5_states(u, bmat, cmat, a, d)
    z = _rowmap(jax.nn.gelu, [y], "r", [(y.shape, MXU_DTYPE, "r")], name="s5_gelu", tl=512)
    gate_pre = _mm(z, w_glu, name="s5_glu")
    out = _rowmap(_gelu_glu, [y, gate_pre], "rr", [(y.shape, F32, "r")], name="s5_gate", tl=512)
    return out, (u, lr, li, ldt, br, bi, a, bmat, cmat, h, y, z, gate_pre)


def _s5_bwd(dout, saved, p, w_glu):
    u, lr, li, ldt, br, bi, a, bmat, cmat, h, y, z, gate_pre = saved
    d = p["s5_d"]

    def gate_bwd(dov, yv, gv):
        zv = jax.nn.gelu(yv)
        sg = jax.nn.sigmoid(gv)
        return dov * sg, dov * zv * sg * (1.0 - sg)

    dz_direct, dgate = _rowmap(gate_bwd, [dout, y, gate_pre], "rrr", [(y.shape, F32, "r"), (y.shape, MXU_DTYPE, "r")],
                               name="s5_gate_bwd", tl=512)
    dw_glu = _mm(z, dgate, ta=True, name="s5_dwglu", out_dtype=WIRE_DTYPE)
    dz = _mm(dgate, w_glu, tb=True, name="s5_dz", epilogue=lambda acc, prev: acc + prev, extras=[dz_direct])

    def gelu_bwd(dzv, yv, uv, dvv):
        _, vjp = jax.vjp(jax.nn.gelu, yv)
        dy = vjp(dzv)[0]
        return dy, dy * dvv, jnp.sum(dy * uv, axis=0, keepdims=True)

    dy, du_skip, dd = _rowmap(gelu_bwd, [dz, y, u, d], "rrrc",
                              [(y.shape, F32, "r"), (y.shape, F32, "r"), (d.shape, F32, "a")], name="s5_gelu_bwd", tl=512)
    du, da, dbmat, dcmat = _s5_states_bwd(dy, h, u, bmat, cmat, a, du_skip)
    dbbr, dbbi = (_block_diag_part(t).transpose(1, 0, 2) for t in (dbmat[:, :, :S5_BLOCK], dbmat[:, :, S5_BLOCK:]))
    dar, dai = _unpair_columns(da, 1)
    cts = (dar.reshape(S5_GROUPS, S5_STATE), dai.reshape(S5_GROUPS, S5_STATE), dbbr, dbbi)
    dlr, dli, dldt, dbr, dbi = _s5_prep_bwd(lr, li, ldt, br, bi, cts)
    dcr, dci = (_block_diag_part(t).transpose(0, 2, 1) for t in (dcmat[:, :S5_BLOCK], dcmat[:, S5_BLOCK:]))
    grads = {
        "s5_lambda_re": dlr[None], "s5_lambda_im": dli[None], "s5_log_dt": dldt[:, 0][None],
        "s5_b_re": dbr.transpose(1, 2, 0)[None], "s5_b_im": dbi.transpose(1, 2, 0)[None],
        "s5_c_re": dcr[None], "s5_c_im": -dci[None], "s5_d": dd,
    }
    return du, dw_glu, grads


def _mix0_fwd(x, g, p, full, late):
    (h, proj), _ = _norm_proj(x, g, full[("ab_w_in", 0)], name="mix0_in")
    u = proj[:, :S5_WIDTH]
    job, keys = late.gather_job("sb_fwd") if late else (None, [])
    (o, lsum), got = _attend(proj, job=job)
    full.update(zip(keys, got))
    w_glu, w_out = full[("s5_w_glu", 0)], full[("ab_w_out", 0)]
    y_a, s5_saved = _s5_fwd(u, p, w_glu)
    mix = jnp.concatenate([y_a, o], axis=1).astype(MXU_DTYPE)
    x2 = _mm(mix, w_out, name="mix0_out", epilogue=lambda acc, xv: xv + acc, extras=[x])
    return x2, (x, h, proj, lsum, mix, s5_saved)


def _mix0_bwd(dx2, saved, g, p, full, grads, late):
    x, h, proj, lsum, mix, s5_saved = saved
    w_in, w_glu, w_out = full[("ab_w_in", 0)], full[("s5_w_glu", 0)], full[("ab_w_out", 0)]
    dmix = _mm(dx2, w_out, tb=True, name="mix0_dmix")
    grads[("ab_w_out", 0)] = _mm(mix, dx2, ta=True, name="mix0_dwout", out_dtype=WIRE_DTYPE)
    du, grads[("s5_w_glu", 0)], s5_grads = _s5_bwd(dmix[:, :S5_WIDTH], s5_saved, p, w_glu)
    job, keys = late.scatter_job(grads) if late else (None, [])
    (dq, dk, dv), got = _attend_bwd(proj, lsum, dmix, job=job)
    _note(late, keys, got)
    dproj = jnp.concatenate([du] + [t.astype(MXU_DTYPE) for t in (dq, dk, dv)], axis=1)
    grads[("ab_w_in", 0)] = _mm(h, dproj, ta=True, name="mix0_dwin", out_dtype=WIRE_DTYPE)
    job, keys = late.scatter_job(grads) if late else (None, [])
    (dx, dg), got = _proj_norm_bwd([(dproj, w_in, "DF")], x, g, dx2, name="mix0_dh", job=job)
    _note(late, keys, got)
    return dx, dg, s5_grads


def _shift_down(t, n):
    rows = lax.broadcasted_iota(jnp.int32, t.shape, 0)
    return jnp.where(rows >= n, pltpu.roll(t, n, 0), 0.0)


def _shift_up(t, n):
    rows = lax.broadcasted_iota(jnp.int32, t.shape, 0)
    return jnp.where(rows < t.shape[0] - n, pltpu.roll(t, t.shape[0] - n, 0), 0.0)


def _conv_fwd(proj, cw, *, tc=128):
    seq, c3 = proj.shape
    ch = c3 // 3
    nb = ch // tc

    def body(b_ref, c_ref, v_ref, w_ref, m_ref):
        pv = c_ref[...] * v_ref[...]
        w = w_ref[...]
        y = w[2:3] * pv + w[1:2] * _shift_down(pv, 1) + w[0:1] * _shift_down(pv, 2)
        m_ref[...] = (b_ref[...] * y).astype(m_ref.dtype)

    col = lambda part: pl.BlockSpec((seq, tc), lambda j: (0, part * nb + j))
    return pl.pallas_call(
        body, name="conv_fwd", grid=(nb,),
        in_specs=[col(0), col(1), col(2), pl.BlockSpec((3, tc), lambda j: (0, j))],
        out_specs=pl.BlockSpec((seq, tc), lambda j: (0, j)),
        out_shape=jax.ShapeDtypeStruct((seq, ch), MXU_DTYPE),
        compiler_params=_params("parallel"),
    )(proj, proj, proj, cw)


def _conv_bwd(proj, cw, dm, *, tc=128):
    seq, c3 = proj.shape
    ch = c3 // 3
    nb = ch // tc

    def body(b_ref, c_ref, v_ref, w_ref, dm_ref, dproj_ref, dw_ref, dc_ref, dv_ref):
        part = pl.program_id(1)

        @pl.when(part == 0)
        def _():
            cv, vv, dmv = c_ref[...], v_ref[...], dm_ref[...]
            pv = cv * vv
            w = w_ref[...]
            p1, p2 = _shift_down(pv, 1), _shift_down(pv, 2)
            y = w[2:3] * pv + w[1:2] * p1 + w[0:1] * p2
            dproj_ref[...] = (dmv * y).astype(dproj_ref.dtype)
            dy = dmv * b_ref[...]
            dp = w[2:3] * dy + w[1:2] * _shift_up(dy, 1) + w[0:1] * _shift_up(dy, 2)
            dc_ref[...] = (dp * vv).astype(dc_ref.dtype)
            dv_ref[...] = (dp * cv).astype(dv_ref.dtype)
            dw_ref[...] = jnp.concatenate([jnp.sum(dy * p2, axis=0, keepdims=True), jnp.sum(dy * p1, axis=0, keepdims=True),
                                           jnp.sum(dy * pv, axis=0, keepdims=True)], axis=0)

        @pl.when(part == 1)
        def _():
            dproj_ref[...] = dc_ref[...]

        @pl.when(part == 2)
        def _():
            dproj_ref[...] = dv_ref[...]

    col = lambda part: pl.BlockSpec((seq, tc), lambda j, t: (0, part * nb + j))
    small = pl.BlockSpec((3, tc), lambda j, t: (0, j))
    return pl.pallas_call(
        body, name="conv_bwd", grid=(nb, 3),
        in_specs=[col(0), col(1), col(2), small, pl.BlockSpec((seq, tc), lambda j, t: (0, j))],
        out_specs=[pl.BlockSpec((seq, tc), lambda j, t: (0, t * nb + j)), small],
        out_shape=[jax.ShapeDtypeStruct((seq, c3), MXU_DTYPE), jax.ShapeDtypeStruct((3, ch), F32)],
        scratch_shapes=[pltpu.VMEM((seq, tc), MXU_DTYPE)] * 2,
        compiler_params=_params("parallel", "arbitrary"),
    )(proj, proj, proj, cw, dm)


def _mix1_fwd(x, g, full, late):
    job, keys = late.gather_job("mix1_in") if late else (None, [])
    (h, proj), got = _norm_proj(x, g, full[("sc_w_in", 0)], name="mix1_in", job=job)
    full.update(zip(keys, got))
    m = _conv_fwd(proj, full[("sc_conv_w", 0)])
    x2 = _mm(m, full[("sc_w_out", 0)], name="mix1_out", epilogue=lambda acc, xv: xv + acc, extras=[x])
    return x2, (x, h, proj, m)


def _mix1_bwd(dx2, saved, g, w_in, cw, w_out):
    x, h, proj, m = saved
    dm = _mm(dx2, w_out, tb=True, name="mix1_dm")
    dw_out = _mm(m, dx2, ta=True, name="mix1_dwout", out_dtype=WIRE_DTYPE)
    dproj, dcw = _conv_bwd(proj, cw, dm)
    dw_in = _mm(h, dproj, ta=True, name="mix1_dwin", out_dtype=WIRE_DTYPE)
    (dx, dg), _ = _proj_norm_bwd([(dproj, w_in, "DF")], x, g, dx2, name="mix1_dh")
    return dx, dg, dw_in, dcw, dw_out


def _loss_head(x, g, target):
    feat = x.shape[1]

    def fn(xv, gv, tv):
        err = _rms_fwd(xv, gv) - tv
        dx, dg = _rms_bwd(err / feat, xv, gv)
        return jnp.sum(err * err, keepdims=True) * (0.5 / feat), dx, dg

    return _rowmap(fn, [x, g, target], "rcr", [((1, 1), F32, "a"), (x.shape, F32, "r"), (g.shape, F32, "a")],
                   name="loss_head", tl=256)


def _slot(ref, place, chip=None, half=None, piece=(0, 1)):
    axis, width = place
    shape = list(ref.shape)
    start = [0, 0]
    if chip is not None:
        start[axis], shape[axis] = chip * width, width
    if half is not None:
        h_axis = 0 if shape[0] % 32 == 0 else 1
        shape[h_axis] //= 2 * piece[1]
        start[h_axis] = start[h_axis] + (half * piece[1] + piece[0]) * shape[h_axis]
    hint = lambda s, d: s if isinstance(s, int) else pl.multiple_of(s, 128 if d == 1 else 8)
    return ref.at[tuple(pl.ds(hint(s, d), n) for d, (s, n) in enumerate(zip(start, shape)))]


class _Exchange:
    def __init__(self, kind, arrays, places, pieces=1):
        self.kind, self.arrays, self.places, self.n, self.pieces = kind, list(arrays), list(places), len(arrays), pieces
        self.out_shape = []
        for t, (axis, width) in zip(self.arrays, self.places):
            if kind == "gather":
                shape = list(t.shape)
                shape[axis] = N_CHIPS * width
            else:
                shape = [N_CHIPS] + list(t.shape)
                shape[1 + axis] = width
            self.out_shape.append(jax.ShapeDtypeStruct(tuple(shape), t.dtype))
        n = self.n
        self.scratch = [pltpu.SemaphoreType.DMA((3 * n * pieces,)) for _ in range(4 if kind == "gather" else 2)]
        self.scratch.append(pltpu.SemaphoreType.DMA((n,)))

    def _copies(self, ins, outs, sems):
        x, y, c = lax.axis_index("x"), lax.axis_index("y"), lax.axis_index("c")
        peers = [(1 - x, y), (x, 1 - y), (1 - x, 1 - y)]
        remote = lambda src, dst, send, recv, k, to: pltpu.make_async_remote_copy(
            src_ref=src, dst_ref=dst, send_sem=send.at[k], recv_sem=recv.at[k], device_id=to, device_id_type=MESH_ID)
        local, ici, d2d = [], [], []
        for a in range(self.n):
            place = self.places[a]
            if self.kind == "gather":
                local.append(pltpu.make_async_copy(ins[a], _slot(outs[a], place, 2 * x + y), sems[4].at[a]))
                for q in range(self.pieces):
                    for r, (px, py) in enumerate(peers):
                        k, part = (3 * a + r) * self.pieces + q, (q, self.pieces)
                        ici.append(remote(_slot(ins[a], place, None, c, part), _slot(outs[a], place, 2 * x + y, c, part),
                                          sems[0], sems[1], k, (px, py, c)))
                        landed = _slot(outs[a], place, 2 * px + py, c, part)
                        d2d.append(remote(landed, landed, sems[2], sems[3], k, (x, y, 1 - c)))
            else:
                local.append(pltpu.make_async_copy(_slot(ins[a], place, 2 * x + y), outs[a].at[3], sems[2].at[a]))
                for r, (px, py) in enumerate(peers):
                    ici.append(remote(_slot(ins[a], place, 2 * px + py), outs[a].at[r], sems[0], sems[1], 3 * a + r, (px, py, c)))
        return local, ici, d2d

    def start(self, ins, outs, sems):
        local, ici, _ = self._copies(ins, outs, sems)
        for cp in local + ici:
            cp.start()

    def relay(self, ins, outs, sems):
        _, ici, d2d = self._copies(ins, outs, sems)
        for arrived, onward in zip(ici, d2d):
            arrived.wait_recv()
            onward.start()

    def finish(self, ins, outs, sems):
        local, ici, d2d = self._copies(ins, outs, sems)
        for cp in local + d2d:
            cp.wait()
        for cp in ici:
            cp.wait_send() if d2d else cp.wait()


def _exchange_call(job, name):
    n = job.n

    def body(*refs):
        ins, outs, sems = refs[:n], refs[n:2 * n], refs[2 * n:]
        job.start(ins, outs, sems)
        job.relay(ins, outs, sems)
        job.finish(ins, outs, sems)

    return pl.pallas_call(
        body, name=name, in_specs=[ANY_SPEC] * n, out_specs=[ANY_SPEC] * n, out_shape=job.out_shape,
        scratch_shapes=job.scratch, compiler_params=pltpu.CompilerParams(has_side_effects=True),
    )(*job.arrays)


def _carried_call(body, *, name, grid, in_specs, out_specs, out_shape, semantics, operands, scratch_shapes=(), job=None):
    scratch_shapes = list(scratch_shapes)
    if job is None:
        return pl.pallas_call(body, name=name, grid=grid, in_specs=in_specs, out_specs=out_specs, out_shape=out_shape,
                              scratch_shapes=scratch_shapes, compiler_params=_params(*semantics))(*operands), []
    n_in, n_out, n, n_scr = len(in_specs), len(out_specs), job.n, len(scratch_shapes)
    steps = math.prod(grid)

    def wrapped(*refs):
        ins, job_ins = refs[:n_in], refs[n_in:n_in + n]
        outs, job_outs = refs[n_in + n:n_in + n + n_out], refs[n_in + n + n_out:n_in + 2 * n + n_out]
        outs = outs + refs[n_in + 2 * n + n_out:n_in + 2 * n + n_out + n_scr]
        sems = refs[n_in + 2 * n + n_out + n_scr:]
        step = functools.reduce(lambda acc, d: acc * grid[d] + pl.program_id(d), range(len(grid)), 0)

        @pl.when(step == 0)
        def _():
            job.start(job_ins, job_outs, sems)

        @pl.when(step == (3 * steps) // 4)
        def _():
            job.relay(job_ins, job_outs, sems)

        body(*ins, *outs)

        @pl.when(step == steps - 1)
        def _():
            job.finish(job_ins, job_outs, sems)

    res = pl.pallas_call(
        wrapped, name=name, grid=grid, in_specs=list(in_specs) + [ANY_SPEC] * n, out_specs=list(out_specs) + [ANY_SPEC] * n,
        out_shape=list(out_shape) + job.out_shape, scratch_shapes=scratch_shapes + job.scratch,
        compiler_params=pltpu.CompilerParams(dimension_semantics=("arbitrary",) * len(grid), vmem_limit_bytes=VMEM_LIMIT,
                                             has_side_effects=True),
    )(*operands, *job.arrays)
    return res[:n_out], res[n_out:]


def _swap_with_sibling(parts):
    n = len(parts)

    def body(*refs):
        ins, outs = refs[:n], refs[n:2 * n]
        send, recv = refs[2 * n:]
        sibling = (lax.axis_index("x"), lax.axis_index("y"), 1 - lax.axis_index("c"))
        copies = [pltpu.make_async_remote_copy(src_ref=ins[a], dst_ref=outs[a], send_sem=send.at[a], recv_sem=recv.at[a],
                                               device_id=sibling, device_id_type=MESH_ID) for a in range(n)]
        for cp in copies:
            cp.start()
        for cp in copies:
            cp.wait()

    return pl.pallas_call(
        body, name="swap_with_sibling",
        in_specs=[ANY_SPEC] * n, out_specs=[ANY_SPEC] * n,
        out_shape=[jax.ShapeDtypeStruct(p.shape, p.dtype) for p in parts],
        scratch_shapes=[pltpu.SemaphoreType.DMA((n,)), pltpu.SemaphoreType.DMA((n,))],
        compiler_params=pltpu.CompilerParams(has_side_effects=True),
    )(*parts)


def _sum_all_devices(t):
    rows = t.shape[0]

    def body(t_ref, o_ref, slots, send, recv):
        x, y, c = lax.axis_index("x"), lax.axis_index("y"), lax.axis_index("c")
        me = 4 * x + 2 * y + c
        slots[me] = t_ref[...]
        copies = []
        for m in range(1, 8):
            peer = (x ^ (m >> 2), y ^ ((m >> 1) & 1), c ^ (m & 1))
            cp = pltpu.make_async_remote_copy(src_ref=t_ref, dst_ref=slots.at[me], send_sem=send.at[m - 1],
                                              recv_sem=recv.at[m - 1], device_id=peer, device_id_type=MESH_ID)
            cp.start()
            copies.append(cp)
        for cp in copies:
            cp.wait()
        acc = slots[0]
        for dev in range(1, 8):
            acc = acc + slots[dev]
        o_ref[...] = acc

    vmem = pl.BlockSpec(memory_space=pltpu.VMEM)
    return pl.pallas_call(
        body, name="sum_all_devices", in_specs=[vmem], out_specs=vmem,
        out_shape=jax.ShapeDtypeStruct(t.shape, F32),
        scratch_shapes=[pltpu.VMEM((8, rows, 128), F32), pltpu.SemaphoreType.DMA((7,)), pltpu.SemaphoreType.DMA((7,))],
        compiler_params=pltpu.CompilerParams(vmem_limit_bytes=VMEM_LIMIT, has_side_effects=True),
    )(t)


def _adamw(w, g, m, v):
    m = ADAM_B1 * m + (1.0 - ADAM_B1) * g
    v = ADAM_B2 * v + (1.0 - ADAM_B2) * jnp.square(g)
    m_hat = m / (1.0 - ADAM_B1 ** ADAM_STEP)
    v_hat = v / (1.0 - ADAM_B2 ** ADAM_STEP)
    return -ADAM_LR * (m_hat / (jnp.sqrt(v_hat) + ADAM_EPS) + ADAM_WD * w), m, v


def _chip_sum(received, name):
    rows, cols = received.shape[1:]
    tl = _row_block(rows, 512, tile=32 // received.dtype.itemsize)

    def body(r_ref, o_ref):
        total = ((r_ref[0].astype(F32) + r_ref[1].astype(F32)) + r_ref[2].astype(F32)) + r_ref[3].astype(F32)
        o_ref[...] = total.astype(o_ref.dtype)

    return pl.pallas_call(body, name=name, grid=(rows // tl,),
                          in_specs=[pl.BlockSpec((N_CHIPS, tl, cols), lambda i: (0, i, 0))],
                          out_specs=pl.BlockSpec((tl, cols), lambda i: (i, 0)),
                          out_shape=jax.ShapeDtypeStruct((rows, cols), received.dtype),
                          compiler_params=_params("parallel"))(received)


def _adamw_layer(w, m, v, p_mine, p_other, layer, prev, name):
    _, rows, cols = w.shape
    assert p_mine.shape[1] == cols and p_mine.shape[0] >= rows
    tl = _row_block(rows, 512, tile=32 // p_mine.dtype.itemsize)
    tc = cols
    if tl < 128 < rows and cols % 256 == 0:
        tl, tc = rows, 256

    def body(w_ref, m_ref, v_ref, pa_ref, pb_ref, *rest):
        g = pa_ref[...].astype(F32) + pb_ref[...].astype(F32)
        for o_ref, val in zip(rest[-4:], (g,) + _adamw(w_ref[...], g, m_ref[...], v_ref[...])):
            o_ref[...] = val

    stacked = pl.BlockSpec((None, tl, tc), lambda i, j: (layer, i, j))
    part = pl.BlockSpec((tl, tc), lambda i, j: (i, j))
    kept = list(prev) if prev else []
    return pl.pallas_call(
        body, name=name, grid=(rows // tl, cols // tc),
        in_specs=[stacked] * 3 + [part] * 2 + [ANY_SPEC] * len(kept),
        out_specs=[stacked] * 4, out_shape=[jax.ShapeDtypeStruct(w.shape, F32)] * 4,
        input_output_aliases={5 + k: k for k in range(len(kept))},
        compiler_params=_params("parallel", "parallel"),
    )(w, m, v, p_mine, p_other, *kept)


def _adamw_small(w, g, m, v):
    def fn(wv, gv, mv, vv):
        return _adamw(wv, gv, mv, vv)

    return _rowmap(fn, [w, g, m, v], "rrrr", [(w.shape, F32, "r")] * 3, name="adamw_small", tl=w.shape[0])


WEIGHTS = ['ffn1_norm', 'ffn1_w_gate', 'ffn1_w_up', 'ffn1_w_down', 'mix_norm', 'ffn2_norm', 'ffn2_w_gate', 'ffn2_w_up',
           'ffn2_w_down', 'ab_w_in', 's5_lambda_re', 's5_lambda_im', 's5_log_dt', 's5_b_re', 's5_b_im', 's5_c_re', 's5_c_im',
           's5_d', 's5_w_glu', 'ab_w_out', 'sc_w_in', 'sc_conv_w', 'sc_w_out', 'final_norm']
SHARDED = {'ffn1_w_gate': (0, FF_SLOT), 'ffn1_w_up': (0, FF_SLOT), 'ffn1_w_down': (0, FF_SLOT),
           'ffn2_w_gate': (0, FF_SLOT), 'ffn2_w_up': (0, FF_SLOT), 'ffn2_w_down': (0, FF_SLOT),
           'ab_w_in': (1, 512), 's5_w_glu': (0, 128), 'ab_w_out': (0, 256), 'sc_w_in': (1, 768), 'sc_conv_w': (1, 256),
           'sc_w_out': (0, 256)}
SWAPPED = ('ffn1_w_gate', 'ffn1_w_up', 'ffn2_w_gate', 'ffn2_w_up')
SMALL = [n for n in WEIGHTS if n not in SHARDED]


def _held(name, t):
    return jnp.swapaxes(t, 1, 2) if name in SWAPPED else t


def _pack(arrays):
    rows = []
    for t in arrays:
        flat = t.reshape(-1)
        rows.append(jnp.pad(flat, (0, (-flat.shape[0]) % 128)))
    flat = jnp.concatenate(rows)
    return jnp.pad(flat, (0, (-flat.shape[0]) % 1024)).reshape(-1, 128)


def _unpack(packed, like):
    flat, out, pos = packed.reshape(-1), [], 0
    for t in like:
        out.append(flat[pos:pos + t.size].reshape(t.shape))
        pos += t.size + (-t.size) % 128
    return out


def _local_grads(x, target, p, full, late=None):
    small, grads, saved = {}, {}, []
    for layer in range(2):
        x, s1 = _ffn_fwd(x, p["ffn1_norm"][layer:layer + 1], full, "ffn1", layer, late)
        if layer == 0:
            x, sm = _mix0_fwd(x, p["mix_norm"][0:1], p, full, late)
        else:
            x, sm = _mix1_fwd(x, p["mix_norm"][1:2], full, late)
        x, s2 = _ffn_fwd(x, p["ffn2_norm"][layer:layer + 1], full, "ffn2", layer, late)
        saved.append((s1, sm, s2))
    loss, dx, dg_final = _loss_head(x, p["final_norm"][None], target)
    small["final_norm"] = dg_final[0]
    gains = {n: [None, None] for n in ("ffn1_norm", "mix_norm", "ffn2_norm")}

    def ffn_bwd(which, layer, dx, s):
        dx, dg = _ffn_bwd(dx, s, p[f"{which}_norm"][layer:layer + 1], full, which, layer, grads, late,
                          inline=(which, layer) in (("ffn2", 1), ("ffn1", 0)))
        gains[f"{which}_norm"][layer] = dg[0]
        return dx

    for layer in (1, 0):
        s1, sm, s2 = saved[layer]
        dx = ffn_bwd("ffn2", layer, dx, s2)
        if layer == 0:
            dx, dg, s5_grads = _mix0_bwd(dx, sm, p["mix_norm"][0:1], p, full, grads, late)
            small.update(s5_grads)
        else:
            dx, dg, dw_in, dcw, dw_out = _mix1_bwd(dx, sm, p["mix_norm"][1:2], full[("sc_w_in", 0)], full[("sc_conv_w", 0)],
                                                   full[("sc_w_out", 0)])
            grads.update({("sc_w_in", 0): dw_in, ("sc_conv_w", 0): dcw.astype(WIRE_DTYPE), ("sc_w_out", 0): dw_out})
        gains["mix_norm"][layer] = dg[0]
        dx = ffn_bwd("ffn1", layer, dx, s1)
    small.update({n: jnp.stack(pair) for n, pair in gains.items()})
    return loss, dx, small, grads


_GATHER_PLAN = {
    "gather_early": [("ffn1_w_gate", 0), ("ffn1_w_up", 0)],
    "ffn1_0_up": [("ffn1_w_down", 0), ("ab_w_in", 0)],
    "sb_fwd": [("s5_w_glu", 0), ("ab_w_out", 0), ("ffn2_w_gate", 0), ("ffn2_w_up", 0), ("ffn2_w_down", 0),
               ("ffn1_w_gate", 1), ("ffn1_w_up", 1), ("ffn1_w_down", 1)],
    "ffn2_0_up": [("sc_w_in", 0), ("sc_conv_w", 0), ("sc_w_out", 0)],
    "ffn1_1_up": [("ffn2_w_gate", 1), ("ffn2_w_up", 1)],
    "mix1_in": [("ffn2_w_down", 1)],
}


class _Late:
    def __init__(self, shards, places):
        self.shards, self.places = shards, places
        self.sent, self.received = set(), {}

    def gather_job(self, carrier):
        keys = _GATHER_PLAN.get(carrier, [])
        if not keys:
            return None, []
        pieces = 4 if carrier == "gather_early" else 1
        return _Exchange("gather", [self.shards[k] for k in keys], [self.places[k] for k in keys], pieces), keys

    def scatter_job(self, grads):
        keys = [k for k in grads if k not in self.sent]
        if not keys:
            return None, []
        self.sent.update(keys)
        return _Exchange("scatter", [grads[k] for k in keys], [self.places[k] for k in keys]), keys


def kernel(x, ffn1_norm, ffn1_w_gate, ffn1_w_up, ffn1_w_down, mix_norm, ffn2_norm, ffn2_w_gate, ffn2_w_up, ffn2_w_down, ab_w_in, s5_lambda_re, s5_lambda_im, s5_log_dt, s5_b_re, s5_b_im, s5_c_re, s5_c_im, s5_d, s5_w_glu, ab_w_out, sc_w_in, sc_conv_w, sc_w_out, final_norm, loss_target, m_ffn1_norm, m_ffn1_w_gate, m_ffn1_w_up, m_ffn1_w_down, m_mix_norm, m_ffn2_norm, m_ffn2_w_gate, m_ffn2_w_up, m_ffn2_w_down, m_ab_w_in, m_s5_lambda_re, m_s5_lambda_im, m_s5_log_dt, m_s5_b_re, m_s5_b_im, m_s5_c_re, m_s5_c_im, m_s5_d, m_s5_w_glu, m_ab_w_out, m_sc_w_in, m_sc_conv_w, m_sc_w_out, m_final_norm, v_ffn1_norm, v_ffn1_w_gate, v_ffn1_w_up, v_ffn1_w_down, v_mix_norm, v_ffn2_norm, v_ffn2_w_gate, v_ffn2_w_up, v_ffn2_w_down, v_ab_w_in, v_s5_lambda_re, v_s5_lambda_im, v_s5_log_dt, v_s5_b_re, v_s5_b_im, v_s5_c_re, v_s5_c_im, v_s5_d, v_s5_w_glu, v_ab_w_out, v_sc_w_in, v_sc_conv_w, v_sc_w_out, v_final_norm):
    args = dict(locals())
    p = {n: _held(n, args[n]) for n in WEIGHTS}
    mom = {n: _held(n, args["m_" + n]) for n in WEIGHTS}
    var = {n: _held(n, args["v_" + n]) for n in WEIGHTS}

    keys = [(n, layer) for n in SHARDED for layer in range(p[n].shape[0])]
    shards, places = {}, {}
    for n, layer in keys:
        axis, width = SHARDED[n]
        t = p[n][layer] if n == "sc_conv_w" else p[n][layer].astype(MXU_DTYPE)
        pad = [(0, 0), (0, 0)]
        pad[axis] = (0, width - t.shape[axis])
        shards[(n, layer)], places[(n, layer)] = jnp.pad(t, pad), (axis, width)
    late = _Late(shards, places)
    job, first = late.gather_job("gather_early")
    full = dict(zip(first, _exchange_call(job, "gather_early")))

    loss, dx, small, grads = _local_grads(x[0], loss_target[0], p, full, late)
    loss = lax.psum(loss[0, 0], ("x", "y", "c"))
    assert set(late.received) == set(keys), "a gradient was left without a carrier"

    partial = [_chip_sum(late.received[(n, layer)], name=f"chip_sum_{n}_{layer}") for n, layer in keys]
    other = _swap_with_sibling(partial)
    out = {}
    for (n, layer), mine, theirs in zip(keys, partial, other):
        out[n] = _adamw_layer(p[n], mom[n], var[n], mine, theirs, layer, out.get(n), name=f"adamw_{n}_{layer}")
    out = {n: [_held(n, t) for t in res] for n, res in out.items()}

    like = [p[n] for n in SMALL]
    g_small = _sum_all_devices(_pack([small[n] for n in SMALL]))
    d_small, m_small, v_small = _adamw_small(_pack(like), g_small, _pack([mom[n] for n in SMALL]), _pack([var[n] for n in SMALL]))
    for k, packed in enumerate((g_small, d_small, m_small, v_small)):
        for n, t in zip(SMALL, _unpack(packed, like)):
            out.setdefault(n, [None] * 4)[k] = t

    return (loss, dx[None], *[out[n][0] for n in WEIGHTS], *[out[n][1] for n in WEIGHTS],
            *[out[n][2] for n in WEIGHTS], *[out[n][3] for n in WEIGHTS])
```

```python
import functools
import math

import jax
import jax.numpy as jnp
from jax import lax
from jax.experimental import pallas as pl
from jax.experimental.pallas import tpu as pltpu

F32 = jnp.float32
MXU_DTYPE = jnp.bfloat16
WIRE_DTYPE = jnp.bfloat16
MESH_ID = pl.DeviceIdType.MESH

D_MODEL = 1024
D_FF = 2752
N_CHIPS = 4
FF_SHARD = D_FF // N_CHIPS
FF_SLOT = 768
FF_PAD = N_CHIPS * FF_SLOT
S5_WIDTH = 512
S5_GROUP = 16
S5_GROUPS = 32
S5_STATE = 64
S5_LANES = S5_GROUPS * S5_STATE
S5_BLOCK = 512
S5_DIAG = S5_LANES // S5_BLOCK
SB_HEADS = 8
SB_DH = 64
SB_SCALE = 0.125
SB_PACK = 2
SB_QUERIES = 1024
SB_KEYS = 256
EPS = 1e-6
ADAM_LR, ADAM_B1, ADAM_B2, ADAM_EPS, ADAM_WD, ADAM_STEP = 0.001, 0.9, 0.999, 1e-08, 0.01, 10
VMEM_LIMIT = 56 * 1024 * 1024

ANY_SPEC = pl.BlockSpec(memory_space=pl.ANY)


def _params(*sem):
    return pltpu.CompilerParams(dimension_semantics=sem or None, vmem_limit_bytes=VMEM_LIMIT)


def _mm(a, b, *, name, ta=False, tb=False, out_dtype=F32, epilogue=None, extras=(), tm=1024, tn=1024, tk=1024, job=None):
    m, k = (a.shape[1], a.shape[0]) if ta else a.shape
    n = b.shape[0] if tb else b.shape[1]
    tm, tn, tk = min(tm, m), min(tn, n), min(tk, k)
    assert m % tm == 0 and n % tn == 0 and k % tk == 0, (name, m, n, k)
    grid = (m // tm, n // tn, k // tk)
    a_spec = pl.BlockSpec((tk, tm), lambda i, j, kk: (kk, i)) if ta else pl.BlockSpec((tm, tk), lambda i, j, kk: (i, kk))
    b_spec = pl.BlockSpec((tn, tk), lambda i, j, kk: (j, kk)) if tb else pl.BlockSpec((tk, tn), lambda i, j, kk: (kk, j))
    nk = grid[2]
    ex_specs = []
    for e in extras:
        if e.shape == (m, n):
            ex_specs.append(pl.BlockSpec((tm, tn), lambda i, j, kk: (i, j)))
        elif e.shape == (1, n):
            ex_specs.append(pl.BlockSpec((1, tn), lambda i, j, kk: (0, j)))
        else:
            assert e.shape == (m, 1), (name, e.shape)
            ex_specs.append(pl.BlockSpec((tm, 1), lambda i, j, kk: (i, 0)))
    dims = (((0 if ta else 1,), (1 if tb else 0,)), ((), ()))
    n_ex = len(extras)

    out_dtypes = list(out_dtype) if isinstance(out_dtype, (list, tuple)) else [out_dtype]
    n_out = len(out_dtypes)

    def body(a_ref, b_ref, *rest):
        ex_refs, o_refs = rest[:n_ex], rest[n_ex:n_ex + n_out]

        def product():
            return lax.dot_general(a_ref[...].astype(MXU_DTYPE), b_ref[...].astype(MXU_DTYPE), dims, preferred_element_type=F32)

        def finish(r):
            if epilogue is not None:
                r = epilogue(r, *[e[...] for e in ex_refs])
            for o_ref, val in zip(o_refs, r if isinstance(r, (tuple, list)) else (r,)):
                o_ref[...] = val.astype(o_ref.dtype)

        if nk == 1:
            finish(product())
            return
        acc_ref, kk = rest[n_ex + n_out], pl.program_id(2)

        @pl.when(kk == 0)
        def _():
            acc_ref[...] = jnp.zeros_like(acc_ref)

        acc_ref[...] += product()

        @pl.when(kk == nk - 1)
        def _():
            finish(acc_ref[...])

    res, got = _carried_call(
        body, name=name, grid=grid,
        in_specs=[a_spec, b_spec, *ex_specs],
        out_specs=[pl.BlockSpec((tm, tn), lambda i, j, kk: (i, j))] * n_out,
        out_shape=[jax.ShapeDtypeStruct((m, n), dt) for dt in out_dtypes],
        scratch_shapes=[pltpu.VMEM((tm, tn), F32)] if nk > 1 else [],
        semantics=("parallel", "parallel", "arbitrary"), operands=(a, b, *extras), job=job)
    res = res if isinstance(out_dtype, (list, tuple)) else res[0]
    return res if job is None else (res, got)


def _row_block(rows, want, tile=8):
    for tl in range(min(want, rows), tile - 1, -1):
        if rows % tl == 0 and tl % tile == 0:
            return tl
    return rows


def _rowmap(fn, ins, in_kinds, outs, *, name, tl):
    rows = next(x.shape[0] for x, kd in zip(ins, in_kinds) if kd == "r")
    tl = _row_block(rows, tl)
    n_in = len(ins)

    def spec(shape, kind):
        if kind == "r":
            return pl.BlockSpec((tl,) + tuple(shape[1:]), lambda i: (i,) + (0,) * (len(shape) - 1))
        return pl.BlockSpec(tuple(shape), lambda i: (0,) * len(shape))

    def body(*refs):
        in_refs, out_refs = refs[:n_in], refs[n_in:]
        res = fn(*[r[...] for r in in_refs])
        if not isinstance(res, (tuple, list)):
            res = (res,)
        for o_ref, val, (_, dt, kind) in zip(out_refs, res, outs):
            if kind == "r":
                o_ref[...] = val.astype(dt)
            else:
                @pl.when(pl.program_id(0) == 0)
                def _():
                    o_ref[...] = jnp.zeros_like(o_ref)

                o_ref[...] += val.astype(dt)

    has_acc = any(kd == "a" for _, _, kd in outs)
    res = pl.pallas_call(
        body, name=name, grid=(rows // tl,),
        in_specs=[spec(x.shape, kd) for x, kd in zip(ins, in_kinds)],
        out_specs=[spec(s, kd) for s, _, kd in outs],
        out_shape=[jax.ShapeDtypeStruct(s, dt) for s, dt, _ in outs],
        compiler_params=_params("arbitrary" if has_acc else "parallel"),
    )(*ins)
    return res[0] if len(outs) == 1 else res


def _rms_fwd(x, g):
    r = lax.rsqrt(jnp.mean(x * x, axis=-1, keepdims=True) + EPS)
    return x * r * g


def _rms_bwd(dh, x, g):
    r = lax.rsqrt(jnp.mean(x * x, axis=-1, keepdims=True) + EPS)
    xh = x * r
    dxh = dh * g
    dx = r * (dxh - xh * jnp.mean(dxh * xh, axis=-1, keepdims=True))
    return dx, jnp.sum(dh * xh, axis=0, keepdims=True)


def _swiglu_act(a, b):
    return jax.nn.silu(a) * b


def _ffn_up(x, g, wg, wu, *, name, tm=1024, tn=1024, job=None):
    m, d = x.shape
    n = wg.shape[0]
    tm, tn = min(tm, m), min(tn, n)
    assert m % tm == 0 and n % tn == 0, (name, m, n)

    def body(x_ref, g_ref, wg_ref, wu_ref, h_ref, a_ref, b_ref, s_ref):
        @pl.when(pl.program_id(1) == 0)
        def _():
            h_ref[...] = _rms_fwd(x_ref[...], g_ref[...]).astype(h_ref.dtype)

        hv = h_ref[...]
        av = lax.dot_general(hv, wg_ref[...], NT_DIMS, preferred_element_type=F32)
        bv = lax.dot_general(hv, wu_ref[...], NT_DIMS, preferred_element_type=F32)
        a_ref[...] = av.astype(a_ref.dtype)
        b_ref[...] = bv.astype(b_ref.dtype)
        s_ref[...] = _swiglu_act(av, bv).astype(s_ref.dtype)

    rows = pl.BlockSpec((tm, d), lambda i, j: (i, 0))
    wgt = pl.BlockSpec((tn, d), lambda i, j: (j, 0))
    tile = pl.BlockSpec((tm, tn), lambda i, j: (i, j))
    return _carried_call(
        body, name=name, grid=(m // tm, n // tn),
        in_specs=[rows, pl.BlockSpec((1, d), lambda i, j: (0, 0)), wgt, wgt],
        out_specs=[rows, tile, tile, tile],
        out_shape=[jax.ShapeDtypeStruct((m, d), MXU_DTYPE)] + [jax.ShapeDtypeStruct((m, n), MXU_DTYPE)] * 3,
        semantics=("parallel", "arbitrary"), operands=(x, g, wg, wu), job=job)


def _norm_proj(x, g, w, *, name, tm=1024, tn=1024, job=None):
    m, d = x.shape
    n = w.shape[1]
    tm, tn = min(tm, m), min(tn, n)
    assert m % tm == 0 and n % tn == 0, (name, m, n)

    def body(x_ref, g_ref, w_ref, h_ref, o_ref):
        @pl.when(pl.program_id(1) == 0)
        def _():
            h_ref[...] = _rms_fwd(x_ref[...], g_ref[...]).astype(h_ref.dtype)

        o_ref[...] = jnp.dot(h_ref[...], w_ref[...], preferred_element_type=F32)

    rows = pl.BlockSpec((tm, d), lambda i, j: (i, 0))
    return _carried_call(
        body, name=name, grid=(m // tm, n // tn),
        in_specs=[rows, pl.BlockSpec((1, d), lambda i, j: (0, 0)), pl.BlockSpec((d, tn), lambda i, j: (0, j))],
        out_specs=[rows, pl.BlockSpec((tm, tn), lambda i, j: (i, j))],
        out_shape=[jax.ShapeDtypeStruct((m, d), MXU_DTYPE), jax.ShapeDtypeStruct((m, n), F32)],
        semantics=("parallel", "arbitrary"), operands=(x, g, w), job=job)


def _proj_norm_bwd(pairs, x, g, dres, *, name, tm=1024, tk=1024, job=None):
    m, f = pairs[0][0].shape
    d = x.shape[1]
    tm, tk = min(tm, m), min(tk, f)
    assert m % tm == 0 and f % tk == 0, (name, m, f)
    nk, n_pairs = f // tk, len(pairs)
    swapped = [kept == "FD" for _, _, kept in pairs]

    def body(*refs):
        dy_refs, w_refs = refs[:n_pairs], refs[n_pairs:2 * n_pairs]
        x_ref, g_ref, dr_ref, dx_ref, dg_ref = refs[2 * n_pairs:2 * n_pairs + 5]
        i, kk = pl.program_id(0), pl.program_id(1)

        part = None
        for dy_ref, w_ref, rows_are_f in zip(dy_refs, w_refs, swapped):
            dims = (((1,), (0,)), ((), ())) if rows_are_f else NT_DIMS
            term = lax.dot_general(dy_ref[...].astype(MXU_DTYPE), w_ref[...], dims, preferred_element_type=F32)
            part = term if part is None else part + term

        @pl.when(jnp.logical_and(i == 0, kk == 0))
        def _():
            dg_ref[...] = jnp.zeros_like(dg_ref)

        def finish(dh):
            dx, dg = _rms_bwd(dh, x_ref[...], g_ref[...])
            dx_ref[...] = dx + dr_ref[...]
            dg_ref[...] += dg

        if nk == 1:
            finish(part)
            return
        acc_ref = refs[2 * n_pairs + 5]

        @pl.when(kk == 0)
        def _():
            acc_ref[...] = jnp.zeros_like(acc_ref)

        acc_ref[...] += part

        @pl.when(kk == nk - 1)
        def _():
            finish(acc_ref[...])

    act = pl.BlockSpec((tm, tk), lambda i, kk: (i, kk))
    w_specs = [pl.BlockSpec((tk, d), lambda i, kk: (kk, 0)) if s else pl.BlockSpec((d, tk), lambda i, kk: (0, kk)) for s in swapped]
    rows = pl.BlockSpec((tm, d), lambda i, kk: (i, 0))
    one = pl.BlockSpec((1, d), lambda i, kk: (0, 0))
    return _carried_call(
        body, name=name, grid=(m // tm, nk),
        in_specs=[act] * n_pairs + w_specs + [rows, one, rows],
        out_specs=[rows, one],
        out_shape=[jax.ShapeDtypeStruct((m, d), F32), jax.ShapeDtypeStruct((1, d), F32)],
        scratch_shapes=[pltpu.VMEM((tm, d), F32)] if nk > 1 else [],
        semantics=("arbitrary", "arbitrary"), operands=(*[p[0] for p in pairs], *[p[1] for p in pairs], x, g, dres), job=job)


def _ffn_dx(da, db, wg, wu, x, g, dres, *, name, job=None):
    return _proj_norm_bwd([(da, wg, "FD"), (db, wu, "FD")], x, g, dres, name=name, tm=512, tk=FF_PAD, job=job)


def _ffn_fwd(x, g, full, which, layer, late):
    tag = f"{which}_{layer}"
    job, keys = late.gather_job(f"{tag}_up") if late else (None, [])
    (h, a, b, s), got = _ffn_up(x, g, full[(f"{which}_w_gate", layer)], full[(f"{which}_w_up", layer)], name=f"{tag}_up", job=job)
    full.update(zip(keys, got))
    x2 = _mm(s, full[(f"{which}_w_down", layer)], name=f"{tag}_down", epilogue=lambda acc, xv: xv + 0.5 * acc, extras=[x],
             tk=FF_PAD)
    return x2, (x, h, a, b, s)


def _ffn_bwd(dx2, saved, g, full, which, layer, grads, late, inline):
    x, h, a, b, s = saved
    tag = f"{which}_{layer}"
    kg, ku, kd = [(f"{which}_w_{n}", layer) for n in ("gate", "up", "down")]
    wg, wu, wd = full[kg], full[ku], full[kd]
    send = (lambda: late.scatter_job(grads)) if (late and inline) else (lambda: (None, []))

    def act_bwd(ds, av, bv):
        a32, b32, half = av.astype(F32), bv.astype(F32), 0.5 * ds
        sig = jax.nn.sigmoid(a32)
        return half * b32 * (sig * (1.0 + a32 * (1.0 - sig))), half * (a32 * sig)

    grads[kd] = _mm(s, dx2, ta=True, name=f"{tag}_dwd", out_dtype=WIRE_DTYPE, epilogue=lambda acc: 0.5 * acc, tk=2048)
    job, keys = send()
    (da, db), got = _carried(_mm, dx2, wd, tb=True, name=f"{tag}_dact", epilogue=act_bwd, extras=[a, b],
                             out_dtype=[MXU_DTYPE, MXU_DTYPE], job=job)
    _note(late, keys, got)
    grads[kg] = _mm(da, h, ta=True, name=f"{tag}_dwg", out_dtype=WIRE_DTYPE, tk=4096)
    job, keys = send()
    grads[ku], got = _carried(_mm, db, h, ta=True, name=f"{tag}_dwu", out_dtype=WIRE_DTYPE, tk=4096, job=job)
    _note(late, keys, got)
    job, keys = send()
    (dx, dg), got = _ffn_dx(da, db, wg, wu, x, g, dx2, name=f"{tag}_dx", job=job)
    _note(late, keys, got)
    return dx, dg


def _carried(fn, *args, job, **kwargs):
    return fn(*args, job=job, **kwargs) if job is not None else (fn(*args, **kwargs), [])


def _note(late, keys, got):
    if late:
        late.received.update(zip(keys, got))


def _softplus(z):
    return jnp.maximum(z, 0.0) + jnp.log(1.0 + jnp.exp(-jnp.abs(z)))


def _ones_dot(x, tri):
    if MXU_DTYPE == F32:
        return jnp.dot(x, tri, preferred_element_type=F32)
    hi = x.astype(MXU_DTYPE)
    lo = (x - hi.astype(F32)).astype(MXU_DTYPE)
    return jnp.dot(hi, tri, preferred_element_type=F32) + jnp.dot(lo, tri, preferred_element_type=F32)


NT_DIMS = (((1,), (1,)), ((), ()))
TN_DIMS = (((0,), (0,)), ((), ()))


SB_LANES = SB_PACK * SB_DH
Q_COL, K_COL, V_COL = (S5_WIDTH * n // SB_LANES for n in (1, 2, 3))


def _head_lanes(rows, hd):
    return lax.broadcasted_iota(jnp.int32, (rows, SB_LANES), 1) // SB_DH == hd


def _attend(proj, *, tq=SB_QUERIES, job=None):
    seq = proj.shape[0]
    tq = min(tq, seq)
    tk = min(SB_KEYS, tq)
    per, hp = tq // tk, SB_PACK

    def body(q_ref, k_ref, v_ref, o_ref, ls_ref):
        i = pl.program_id(1)
        r_idx = lax.broadcasted_iota(jnp.int32, (tk, tk), 0)
        c_idx = lax.broadcasted_iota(jnp.int32, (tk, tk), 1)
        after = (r_idx > c_idx).astype(MXU_DTYPE)
        lanes = [_head_lanes(tk, hd) for hd in range(hp)]

        def block(j, cs, acc, straddles):
            off = pl.multiple_of(j * tk, tk)
            k2, v2 = k_ref[pl.ds(off, tk), :], v_ref[pl.ds(off, tk), :]
            top = 0 if straddles is None else straddles * tk
            rows = tq - top
            q2 = (q_ref[pl.ds(top, rows), :] * SB_SCALE).astype(MXU_DTYPE)
            new_cs, out = [], acc[top:]
            for hd in range(hp):
                kv = jnp.where(lanes[hd], k2, 0.0).astype(MXU_DTYPE)
                vv = jnp.where(lanes[hd], v2, 0.0).astype(MXU_DTYPE)
                z = lax.dot_general(q2, kv, NT_DIMS, preferred_element_type=F32)
                sp = _softplus(z)
                c_in = cs[hd][top:]
                if straddles is None:
                    lk = -sp
                    w = jnp.exp(z - sp + _ones_dot(lk, after) + c_in)
                else:
                    before = lax.broadcasted_iota(jnp.int32, (rows, tk), 1) < lax.broadcasted_iota(jnp.int32, (rows, tk), 0)
                    lk = jnp.where(before, -sp, 0.0)
                    w = jnp.where(before, jnp.exp(z - sp + _ones_dot(lk, after) + c_in), 0.0)
                out = out + jnp.dot(w.astype(MXU_DTYPE), vv, preferred_element_type=F32)
                c_new = c_in + jnp.sum(lk, axis=1, keepdims=True)
                new_cs.append(jnp.concatenate([cs[hd][:top], c_new], axis=0) if top else c_new)
            return tuple(new_cs), (jnp.concatenate([acc[:top], out], axis=0) if top else out)

        carry = (tuple(jnp.zeros((tq, 1), F32) for _ in range(hp)), jnp.zeros((tq, SB_LANES), F32))
        for s in reversed(range(per)):
            carry = block(i * per + s, *carry, s)
        cs, acc = lax.fori_loop(0, i * per, lambda n, cr: block(i * per - 1 - n, *cr, None), carry)
        o_ref[...] = acc
        for hd in range(hp):
            ls_ref[hd] = cs[hd]

    whole = lambda col: pl.BlockSpec((seq, SB_LANES), lambda g, i: (0, col + g))
    return _carried_call(
        body, name="sb_fwd", grid=(SB_HEADS // hp, seq // tq),
        in_specs=[pl.BlockSpec((tq, SB_LANES), lambda g, i: (i, Q_COL + g)), whole(K_COL), whole(V_COL)],
        out_specs=[pl.BlockSpec((tq, SB_LANES), lambda g, i: (i, g)), pl.BlockSpec((hp, tq, 1), lambda g, i: (g, i, 0))],
        out_shape=[jax.ShapeDtypeStruct((seq, SB_HEADS * SB_DH), F32), jax.ShapeDtypeStruct((SB_HEADS, seq, 1), F32)],
        semantics=("parallel", "parallel"), operands=(proj, proj, proj), job=job)


def _attend_bwd(proj, lsum, dmix, *, tq=SB_QUERIES, job=None):
    seq = proj.shape[0]
    tq = min(tq, seq)
    tk = min(SB_KEYS, tq)
    per, hp = tq // tk, SB_PACK
    do_col = S5_WIDTH // SB_LANES

    def body(q_ref, k_ref, v_ref, ls_ref, do_ref, dq_ref, dk_ref, dv_ref):
        i = pl.program_id(1)

        @pl.when(i == 0)
        def _():
            dk_ref[...] = jnp.zeros_like(dk_ref)
            dv_ref[...] = jnp.zeros_like(dv_ref)

        r_idx = lax.broadcasted_iota(jnp.int32, (tk, tk), 0)
        c_idx = lax.broadcasted_iota(jnp.int32, (tk, tk), 1)
        upto = (r_idx <= c_idx).astype(MXU_DTYPE)
        before = (r_idx < c_idx).astype(MXU_DTYPE)
        lanes = [_head_lanes(tk, hd) for hd in range(hp)]

        def block(j, sums, dq, straddles):
            off = pl.multiple_of(j * tk, tk)
            k2, v2 = k_ref[pl.ds(off, tk), :], v_ref[pl.ds(off, tk), :]
            top = 0 if straddles is None else straddles * tk
            rows = tq - top
            part = pl.ds(top, rows)
            q2 = (q_ref[part, :] * SB_SCALE).astype(MXU_DTYPE)
            do2 = do_ref[part, :].astype(MXU_DTYPE)
            valid = None
            if straddles is not None:
                valid = lax.broadcasted_iota(jnp.int32, (rows, tk), 1) < lax.broadcasted_iota(jnp.int32, (rows, tk), 0)
            new_sums, out, dk, dv = [], dq[top:], jnp.zeros((tk, SB_LANES), F32), jnp.zeros((tk, SB_LANES), F32)
            for hd in range(hp):
                cp, ce = sums[hd]
                kv = jnp.where(lanes[hd], k2, 0.0).astype(MXU_DTYPE)
                vv = jnp.where(lanes[hd], v2, 0.0).astype(MXU_DTYPE)
                z = lax.dot_general(q2, kv, NT_DIMS, preferred_element_type=F32)
                sp = _softplus(z)
                lk = -sp if valid is None else jnp.where(valid, -sp, 0.0)
                w = jnp.exp(z - sp + (ls_ref[hd, part, :] - cp[top:]) - _ones_dot(lk, upto))
                if valid is not None:
                    w = jnp.where(valid, w, 0.0)
                e = w * lax.dot_general(do2, vv, NT_DIMS, preferred_element_type=F32)
                earlier = jnp.dot(e.astype(MXU_DTYPE), before, preferred_element_type=F32) + ce[top:]
                keep = jnp.exp(-sp)
                dz = e * keep - (1.0 - keep) * earlier
                if valid is not None:
                    dz = jnp.where(valid, dz, 0.0)
                dzm = dz.astype(MXU_DTYPE)
                out = out + jnp.dot(dzm, kv, preferred_element_type=F32)
                dk = dk + jnp.where(lanes[hd], lax.dot_general(dzm, q2, TN_DIMS, preferred_element_type=F32), 0.0)
                dv = dv + jnp.where(lanes[hd], lax.dot_general(w.astype(MXU_DTYPE), do2, TN_DIMS, preferred_element_type=F32), 0.0)
                new = (cp[top:] + jnp.sum(lk, axis=1, keepdims=True), ce[top:] + jnp.sum(e, axis=1, keepdims=True))
                new_sums.append(tuple(jnp.concatenate([old[:top], val], axis=0) for old, val in zip((cp, ce), new)) if top else new)
            dk_ref[pl.ds(off, tk), :] += dk
            dv_ref[pl.ds(off, tk), :] += dv
            return tuple(new_sums), (jnp.concatenate([dq[:top], out], axis=0) if top else out)

        zero = jnp.zeros((tq, 1), F32)
        carry = (tuple((zero, zero) for _ in range(hp)), jnp.zeros((tq, SB_LANES), F32))
        carry = lax.fori_loop(0, i * per, lambda j, cr: block(j, *cr, None), carry)
        for s in range(per):
            carry = block(i * per + s, *carry, s)
        dq_ref[...] = carry[1] * SB_SCALE

    whole = lambda col: pl.BlockSpec((seq, SB_LANES), lambda g, i: (0, col + g))
    tile = lambda col: pl.BlockSpec((tq, SB_LANES), lambda g, i: (i, col + g))
    acc = pl.BlockSpec((seq, SB_LANES), lambda g, i: (0, g))
    return _carried_call(
        body, name="sb_bwd", grid=(SB_HEADS // hp, seq // tq),
        in_specs=[tile(Q_COL), whole(K_COL), whole(V_COL), pl.BlockSpec((hp, tq, 1), lambda g, i: (g, i, 0)), tile(do_col)],
        out_specs=[tile(0), acc, acc],
        out_shape=[jax.ShapeDtypeStruct((seq, SB_HEADS * SB_DH), F32)] * 3,
        semantics=("parallel", "arbitrary"), operands=(proj, proj, proj, lsum, dmix), job=job)


def _s5_disc(lr, li, ldt, br, bi):
    dt = jnp.exp(ldt)
    mag = jnp.exp(lr * dt)
    ar = mag * jnp.cos(li * dt)
    ai = mag * jnp.sin(li * dt)
    den = lr * lr + li * li
    nr = ar - 1.0
    cr = (nr * lr + ai * li) / den
    ci = (ai * lr - nr * li) / den
    return ar, ai, cr[None] * br - ci[None] * bi, cr[None] * bi + ci[None] * br


def _s5_prep(lr, li, ldt, br, bi):
    shapes = [lr.shape, lr.shape, br.shape, br.shape]

    def body(lr_ref, li_ref, ldt_ref, br_ref, bi_ref, *outs):
        for o, val in zip(outs, _s5_disc(lr_ref[...], li_ref[...], ldt_ref[...], br_ref[...], bi_ref[...])):
            o[...] = val

    return pl.pallas_call(body, name="s5_prep", out_shape=[jax.ShapeDtypeStruct(s, F32) for s in shapes])(lr, li, ldt, br, bi)


def _s5_prep_bwd(lr, li, ldt, br, bi, cts):
    args = (lr, li, ldt, br, bi)

    def body(*refs):
        ins, ct_refs, outs = refs[:5], refs[5:9], refs[9:]
        _, vjp = jax.vjp(_s5_disc, *[r[...] for r in ins])
        for o, val in zip(outs, vjp(tuple(r[...] for r in ct_refs))):
            o[...] = val

    return pl.pallas_call(body, name="s5_prep_bwd", out_shape=[jax.ShapeDtypeStruct(a.shape, F32) for a in args])(*args, *cts)


SCAN_ROWS = 8


def _powers(ar, ai):
    out = [(ar, ai)]
    for _ in range(SCAN_ROWS - 1):
        pr, pi = out[-1]
        out.append((pr * ar - pi * ai, pr * ai + pi * ar))
    return out


def _s5_states(u, bmat, cmat, a, d, *, tc=512):
    seq, width = u.shape
    nj, cols, w2 = bmat.shape
    tw = w2 // 2
    tc = min(tc, seq)
    assert seq % tc == 0 and nj * cols == width and tw == S5_BLOCK

    def body(u_ref, bm_ref, cm_ref, a_ref, d_ref, h_ref, y_ref, cr_ref, ci_ref):
        @pl.when(pl.program_id(1) == 0)
        def _():
            cr_ref[...] = jnp.zeros_like(cr_ref)
            ci_ref[...] = jnp.zeros_like(ci_ref)

        uv = u_ref[...]
        h_ref[...] = jnp.dot(uv.astype(MXU_DTYPE), bm_ref[0], preferred_element_type=F32)
        re, im = pl.ds(0, tw), pl.ds(tw, tw)
        powers = _powers(a_ref[:, re], a_ref[:, im])
        pr = jnp.concatenate([p[0] for p in powers], axis=0)
        pi = jnp.concatenate([p[1] for p in powers], axis=0)
        row_id = lax.broadcasted_iota(jnp.int32, (SCAN_ROWS, tw), 0)
        reach = {dist: tuple(jnp.where(row_id >= dist, part, 0.0) for part in powers[dist - 1]) for dist in (1, 2, 4)}

        def block(n, carry):
            hr, hi = carry
            rows = pl.ds(pl.multiple_of(n * SCAN_ROWS, SCAN_ROWS), SCAN_ROWS)
            yr, yi = h_ref[rows, re], h_ref[rows, im]
            for dist in (1, 2, 4):
                cr, ci = reach[dist]
                sr, si = pltpu.roll(yr, dist, 0), pltpu.roll(yi, dist, 0)
                yr, yi = yr + cr * sr - ci * si, yi + cr * si + ci * sr
            yr, yi = yr + pr * hr - pi * hi, yi + pr * hi + pi * hr
            h_ref[rows, re] = yr
            h_ref[rows, im] = yi
            return yr[SCAN_ROWS - 1:], yi[SCAN_ROWS - 1:]

        hr, hi = lax.fori_loop(0, tc // SCAN_ROWS, block, (cr_ref[...], ci_ref[...]), unroll=4)
        cr_ref[...] = hr
        ci_ref[...] = hi
        y_ref[...] = jnp.dot(h_ref[...].astype(MXU_DTYPE), cm_ref[0], preferred_element_type=F32) + d_ref[...] * uv

    io = pl.BlockSpec((tc, cols), lambda j, t: (t, j))
    return pl.pallas_call(
        body, name="s5_states", grid=(nj, seq // tc),
        in_specs=[io, pl.BlockSpec((1, cols, w2), lambda j, t: (j, 0, 0)), pl.BlockSpec((1, w2, cols), lambda j, t: (j, 0, 0)),
                  pl.BlockSpec((1, w2), lambda j, t: (0, j)), pl.BlockSpec((1, cols), lambda j, t: (0, j))],
        out_specs=[pl.BlockSpec((tc, w2), lambda j, t: (t, j)), io],
        out_shape=[jax.ShapeDtypeStruct((seq, nj * w2), F32), jax.ShapeDtypeStruct((seq, width), F32)],
        scratch_shapes=[pltpu.VMEM((1, tw), F32)] * 2,
        compiler_params=_params("parallel", "arbitrary"),
    )(u, bmat, cmat, a, d)


def _s5_states_bwd(dy, h, u, bmat, cmat, a, du_skip, *, tc=512):
    seq, width = u.shape
    nj, cols, w2 = bmat.shape
    tw = w2 // 2
    tc = min(tc, seq)
    assert seq % tc == 0
    nt = seq // tc

    def body(dy_ref, h_ref, u_ref, bm_ref, cm_ref, a_ref, sk_ref, du_ref, da_ref, db_ref, dc_ref, g_ref, cr_ref, ci_ref):
        @pl.when(pl.program_id(1) == 0)
        def _():
            cr_ref[...] = jnp.zeros_like(cr_ref)
            ci_ref[...] = jnp.zeros_like(ci_ref)
            da_ref[...] = jnp.zeros_like(da_ref)
            db_ref[...] = jnp.zeros_like(db_ref)
            dc_ref[...] = jnp.zeros_like(dc_ref)

        dyv = dy_ref[...].astype(MXU_DTYPE)
        g_ref[...] = lax.dot_general(dyv, cm_ref[0], NT_DIMS, preferred_element_type=F32)
        re, im = pl.ds(0, tw), pl.ds(tw, tw)
        powers = _powers(a_ref[:, re], a_ref[:, im])
        pr = jnp.concatenate([p[0] for p in reversed(powers)], axis=0)
        pi = jnp.concatenate([p[1] for p in reversed(powers)], axis=0)
        row_id = lax.broadcasted_iota(jnp.int32, (SCAN_ROWS, tw), 0)
        last = SCAN_ROWS - 1
        reach = {dist: tuple(jnp.where(row_id < SCAN_ROWS - dist, part, 0.0) for part in powers[dist - 1]) for dist in (1, 2, 4)}

        def block(n, carry):
            gr, gi, sr, si = carry
            rows = pl.ds(pl.multiple_of((tc // SCAN_ROWS - 1 - n) * SCAN_ROWS, SCAN_ROWS), SCAN_ROWS)
            yr, yi = g_ref[rows, re], g_ref[rows, im]
            for dist in (1, 2, 4):
                cr, ci = reach[dist]
                ur, ui = pltpu.roll(yr, SCAN_ROWS - dist, 0), pltpu.roll(yi, SCAN_ROWS - dist, 0)
                yr, yi = yr + cr * ur + ci * ui, yi + cr * ui - ci * ur
            yr, yi = yr + pr * gr + pi * gi, yi + pr * gi - pi * gr
            g_ref[rows, re] = yr
            g_ref[rows, im] = yi
            nr = jnp.where(row_id < last, pltpu.roll(yr, last, 0), gr)
            ni = jnp.where(row_id < last, pltpu.roll(yi, last, 0), gi)
            hr, hi = h_ref[rows, re], h_ref[rows, im]
            return yr[:1], yi[:1], sr + nr * hr + ni * hi, si + ni * hr - nr * hi

        zero = jnp.zeros((SCAN_ROWS, tw), F32)
        gr, gi, sr, si = lax.fori_loop(0, tc // SCAN_ROWS, block, (cr_ref[...], ci_ref[...], zero, zero), unroll=4)
        cr_ref[...] = gr
        ci_ref[...] = gi
        da_ref[:, re] += jnp.sum(sr, axis=0, keepdims=True)
        da_ref[:, im] += jnp.sum(si, axis=0, keepdims=True)
        gv = g_ref[...].astype(MXU_DTYPE)
        du_ref[...] = (lax.dot_general(gv, bm_ref[0], NT_DIMS, preferred_element_type=F32) + sk_ref[...]).astype(du_ref.dtype)
        db_ref[0] += lax.dot_general(u_ref[...].astype(MXU_DTYPE), gv, TN_DIMS, preferred_element_type=F32)
        dc_ref[0] += lax.dot_general(h_ref[...].astype(MXU_DTYPE), dyv, TN_DIMS, preferred_element_type=F32)

    io = pl.BlockSpec((tc, cols), lambda j, t: (nt - 1 - t, j))
    bm = pl.BlockSpec((1, cols, w2), lambda j, t: (j, 0, 0))
    cm = pl.BlockSpec((1, w2, cols), lambda j, t: (j, 0, 0))
    row = pl.BlockSpec((1, w2), lambda j, t: (0, j))
    return pl.pallas_call(
        body, name="s5_states_bwd", grid=(nj, nt),
        in_specs=[io, pl.BlockSpec((tc, w2), lambda j, t: (nt - 1 - t, j)), io, bm, cm, row, io],
        out_specs=[io, row, bm, cm],
        out_shape=[jax.ShapeDtypeStruct((seq, width), MXU_DTYPE), jax.ShapeDtypeStruct((1, nj * w2), F32),
                   jax.ShapeDtypeStruct(bmat.shape, F32), jax.ShapeDtypeStruct(cmat.shape, F32)],
        scratch_shapes=[pltpu.VMEM((tc, w2), F32)] + [pltpu.VMEM((1, tw), F32)] * 2,
        compiler_params=_params("parallel", "arbitrary"),
    )(dy, h, u, bmat, cmat, a, du_skip)


def _pair_columns(re, im, axis):
    shape = re.shape
    split = shape[:axis] + (shape[axis] // S5_BLOCK, S5_BLOCK) + shape[axis + 1:]
    both = jnp.stack([re.reshape(split), im.reshape(split)], axis=axis + 1)
    return both.reshape(shape[:axis] + (2 * shape[axis],) + shape[axis + 1:])


def _unpair_columns(t, axis):
    shape = t.shape
    both = t.reshape(shape[:axis] + (shape[axis] // (2 * S5_BLOCK), 2, S5_BLOCK) + shape[axis + 1:])
    half = shape[:axis] + (shape[axis] // 2,) + shape[axis + 1:]
    return (lax.index_in_dim(both, 0, axis + 1, keepdims=False).reshape(half),
            lax.index_in_dim(both, 1, axis + 1, keepdims=False).reshape(half))


S5_PER_BLOCK = S5_GROUPS // S5_DIAG


def _block_diag(t):
    g, a, b = t.shape
    n = S5_PER_BLOCK
    eye = jnp.eye(n, dtype=t.dtype)
    return (t.reshape(g // n, n, a, 1, b) * eye[None, :, None, :, None]).reshape(g // n, n * a, n * b)


def _block_diag_part(m):
    j, n = m.shape[0], S5_PER_BLOCK
    a, b = m.shape[1] // n, m.shape[2] // n
    return jnp.moveaxis(jnp.diagonal(m.reshape(j, n, a, n, b), axis1=1, axis2=3), -1, 1).reshape(j * n, a, b)


def _gelu_glu(y, gate_pre):
    z = jax.nn.gelu(y)
    return z * jax.nn.sigmoid(gate_pre)


def _s5_fwd(u, p, w_glu):
    lr, li = p["s5_lambda_re"][0], p["s5_lambda_im"][0]
    ldt = p["s5_log_dt"][0][:, None]
    br = p["s5_b_re"][0].transpose(2, 0, 1)
    bi = p["s5_b_im"][0].transpose(2, 0, 1)
    ar, ai, bbr, bbi = _s5_prep(lr, li, ldt, br, bi)
    a = _pair_columns(ar.reshape(1, S5_LANES), ai.reshape(1, S5_LANES), 1)
    bmat = jnp.concatenate([_block_diag(bbr.transpose(1, 0, 2)), _block_diag(bbi.transpose(1, 0, 2))], axis=2)
    cmat = jnp.concatenate([_block_diag(p["s5_c_re"][0].transpose(0, 2, 1)),
                            -_block_diag(p["s5_c_im"][0].transpose(0, 2, 1))], axis=1)
    bmat, cmat = bmat.astype(MXU_DTYPE), cmat.astype(MXU_DTYPE)
    d = p["s5_d"]
    h, y = _s5_states(u, bmat, cmat, a, d)
    z = _rowmap(jax.nn.gelu, [y], "r", [(y.shape, MXU_DTYPE, "r")], name="s5_gelu", tl=512)
    gate_pre = _mm(z, w_glu, name="s5_glu")
    out = _rowmap(_gelu_glu, [y, gate_pre], "rr", [(y.shape, F32, "r")], name="s5_gate", tl=512)
    return out, (u, lr, li, ldt, br, bi, a, bmat, cmat, h, y, z, gate_pre)


def _s5_bwd(dout, saved, p, w_glu):
    u, lr, li, ldt, br, bi, a, bmat, cmat, h, y, z, gate_pre = saved
    d = p["s5_d"]

    def gate_bwd(dov, yv, gv):
        zv = jax.nn.gelu(yv)
        sg = jax.nn.sigmoid(gv)
        return dov * sg, dov * zv * sg * (1.0 - sg)

    dz_direct, dgate = _rowmap(gate_bwd, [dout, y, gate_pre], "rrr", [(y.shape, F32, "r"), (y.shape, MXU_DTYPE, "r")],
                               name="s5_gate_bwd", tl=512)
    dw_glu = _mm(z, dgate, ta=True, name="s5_dwglu", out_dtype=WIRE_DTYPE)
    dz = _mm(dgate, w_glu, tb=True, name="s5_dz", epilogue=lambda acc, prev: acc + prev, extras=[dz_direct])

    def gelu_bwd(dzv, yv, uv, dvv):
        _, vjp = jax.vjp(jax.nn.gelu, yv)
        dy = vjp(dzv)[0]
        return dy, dy * dvv, jnp.sum(dy * uv, axis=0, keepdims=True)

    dy, du_skip, dd = _rowmap(gelu_bwd, [dz, y, u, d], "rrrc",
                              [(y.shape, F32, "r"), (y.shape, F32, "r"), (d.shape, F32, "a")], name="s5_gelu_bwd", tl=512)
    du, da, dbmat, dcmat = _s5_states_bwd(dy, h, u, bmat, cmat, a, du_skip)
    dbbr, dbbi = (_block_diag_part(t).transpose(1, 0, 2) for t in (dbmat[:, :, :S5_BLOCK], dbmat[:, :, S5_BLOCK:]))
    dar, dai = _unpair_columns(da, 1)
    cts = (dar.reshape(S5_GROUPS, S5_STATE), dai.reshape(S5_GROUPS, S5_STATE), dbbr, dbbi)
    dlr, dli, dldt, dbr, dbi = _s5_prep_bwd(lr, li, ldt, br, bi, cts)
    dcr, dci = (_block_diag_part(t).transpose(0, 2, 1) for t in (dcmat[:, :S5_BLOCK], dcmat[:, S5_BLOCK:]))
    grads = {
        "s5_lambda_re": dlr[None], "s5_lambda_im": dli[None], "s5_log_dt": dldt[:, 0][None],
        "s5_b_re": dbr.transpose(1, 2, 0)[None], "s5_b_im": dbi.transpose(1, 2, 0)[None],
        "s5_c_re": dcr[None], "s5_c_im": -dci[None], "s5_d": dd,
    }
    return du, dw_glu, grads


def _mix0_fwd(x, g, p, full, late):
    (h, proj), _ = _norm_proj(x, g, full[("ab_w_in", 0)], name="mix0_in")
    u = proj[:, :S5_WIDTH]
    job, keys = late.gather_job("sb_fwd") if late else (None, [])
    (o, lsum), got = _attend(proj, job=job)
    full.update(zip(keys, got))
    w_glu, w_out = full[("s5_w_glu", 0)], full[("ab_w_out", 0)]
    y_a, s5_saved = _s5_fwd(u, p, w_glu)
    mix = jnp.concatenate([y_a, o], axis=1).astype(MXU_DTYPE)
    x2 = _mm(mix, w_out, name="mix0_out", epilogue=lambda acc, xv: xv + acc, extras=[x])
    return x2, (x, h, proj, lsum, mix, s5_saved)


def _mix0_bwd(dx2, saved, g, p, full, grads, late):
    x, h, proj, lsum, mix, s5_saved = saved
    w_in, w_glu, w_out = full[("ab_w_in", 0)], full[("s5_w_glu", 0)], full[("ab_w_out", 0)]
    dmix = _mm(dx2, w_out, tb=True, name="mix0_dmix")
    grads[("ab_w_out", 0)] = _mm(mix, dx2, ta=True, name="mix0_dwout", out_dtype=WIRE_DTYPE)
    du, grads[("s5_w_glu", 0)], s5_grads = _s5_bwd(dmix[:, :S5_WIDTH], s5_saved, p, w_glu)
    job, keys = late.scatter_job(grads) if late else (None, [])
    (dq, dk, dv), got = _attend_bwd(proj, lsum, dmix, job=job)
    _note(late, keys, got)
    dproj = jnp.concatenate([du] + [t.astype(MXU_DTYPE) for t in (dq, dk, dv)], axis=1)
    grads[("ab_w_in", 0)] = _mm(h, dproj, ta=True, name="mix0_dwin", out_dtype=WIRE_DTYPE)
    job, keys = late.scatter_job(grads) if late else (None, [])
    (dx, dg), got = _proj_norm_bwd([(dproj, w_in, "DF")], x, g, dx2, name="mix0_dh", job=job)
    _note(late, keys, got)
    return dx, dg, s5_grads


def _shift_down(t, n):
    rows = lax.broadcasted_iota(jnp.int32, t.shape, 0)
    return jnp.where(rows >= n, pltpu.roll(t, n, 0), 0.0)


def _shift_up(t, n):
    rows = lax.broadcasted_iota(jnp.int32, t.shape, 0)
    return jnp.where(rows < t.shape[0] - n, pltpu.roll(t, t.shape[0] - n, 0), 0.0)


def _conv_fwd(proj, cw, *, tc=128):
    seq, c3 = proj.shape
    ch = c3 // 3
    nb = ch // tc

    def body(b_ref, c_ref, v_ref, w_ref, m_ref):
        pv = c_ref[...] * v_ref[...]
        w = w_ref[...]
        y = w[2:3] * pv + w[1:2] * _shift_down(pv, 1) + w[0:1] * _shift_down(pv, 2)
        m_ref[...] = (b_ref[...] * y).astype(m_ref.dtype)

    col = lambda part: pl.BlockSpec((seq, tc), lambda j: (0, part * nb + j))
    return pl.pallas_call(
        body, name="conv_fwd", grid=(nb,),
        in_specs=[col(0), col(1), col(2), pl.BlockSpec((3, tc), lambda j: (0, j))],
        out_specs=pl.BlockSpec((seq, tc), lambda j: (0, j)),
        out_shape=jax.ShapeDtypeStruct((seq, ch), MXU_DTYPE),
        compiler_params=_params("parallel"),
    )(proj, proj, proj, cw)


def _conv_bwd(proj, cw, dm, *, tc=128):
    seq, c3 = proj.shape
    ch = c3 // 3
    nb = ch // tc

    def body(b_ref, c_ref, v_ref, w_ref, dm_ref, dproj_ref, dw_ref, dc_ref, dv_ref):
        part = pl.program_id(1)

        @pl.when(part == 0)
        def _():
            cv, vv, dmv = c_ref[...], v_ref[...], dm_ref[...]
            pv = cv * vv
            w = w_ref[...]
            p1, p2 = _shift_down(pv, 1), _shift_down(pv, 2)
            y = w[2:3] * pv + w[1:2] * p1 + w[0:1] * p2
            dproj_ref[...] = (dmv * y).astype(dproj_ref.dtype)
            dy = dmv * b_ref[...]
            dp = w[2:3] * dy + w[1:2] * _shift_up(dy, 1) + w[0:1] * _shift_up(dy, 2)
            dc_ref[...] = (dp * vv).astype(dc_ref.dtype)
            dv_ref[...] = (dp * cv).astype(dv_ref.dtype)
            dw_ref[...] = jnp.concatenate([jnp.sum(dy * p2, axis=0, keepdims=True), jnp.sum(dy * p1, axis=0, keepdims=True),
                                           jnp.sum(dy * pv, axis=0, keepdims=True)], axis=0)

        @pl.when(part == 1)
        def _():
            dproj_ref[...] = dc_ref[...]

        @pl.when(part == 2)
        def _():
            dproj_ref[...] = dv_ref[...]

    col = lambda part: pl.BlockSpec((seq, tc), lambda j, t: (0, part * nb + j))
    small = pl.BlockSpec((3, tc), lambda j, t: (0, j))
    return pl.pallas_call(
        body, name="conv_bwd", grid=(nb, 3),
        in_specs=[col(0), col(1), col(2), small, pl.BlockSpec((seq, tc), lambda j, t: (0, j))],
        out_specs=[pl.BlockSpec((seq, tc), lambda j, t: (0, t * nb + j)), small],
        out_shape=[jax.ShapeDtypeStruct((seq, c3), MXU_DTYPE), jax.ShapeDtypeStruct((3, ch), F32)],
        scratch_shapes=[pltpu.VMEM((seq, tc), MXU_DTYPE)] * 2,
        compiler_params=_params("parallel", "arbitrary"),
    )(proj, proj, proj, cw, dm)


def _mix1_fwd(x, g, full, late):
    job, keys = late.gather_job("mix1_in") if late else (None, [])
    (h, proj), got = _norm_proj(x, g, full[("sc_w_in", 0)], name="mix1_in", job=job)
    full.update(zip(keys, got))
    m = _conv_fwd(proj, full[("sc_conv_w", 0)])
    x2 = _mm(m, full[("sc_w_out", 0)], name="mix1_out", epilogue=lambda acc, xv: xv + acc, extras=[x])
    return x2, (x, h, proj, m)


def _mix1_bwd(dx2, saved, g, w_in, cw, w_out):
    x, h, proj, m = saved
    dm = _mm(dx2, w_out, tb=True, name="mix1_dm")
    dw_out = _mm(m, dx2, ta=True, name="mix1_dwout", out_dtype=WIRE_DTYPE)
    dproj, dcw = _conv_bwd(proj, cw, dm)
    dw_in = _mm(h, dproj, ta=True, name="mix1_dwin", out_dtype=WIRE_DTYPE)
    (dx, dg), _ = _proj_norm_bwd([(dproj, w_in, "DF")], x, g, dx2, name="mix1_dh")
    return dx, dg, dw_in, dcw, dw_out


def _loss_head(x, g, target):
    feat = x.shape[1]

    def fn(xv, gv, tv):
        err = _rms_fwd(xv, gv) - tv
        dx, dg = _rms_bwd(err / feat, xv, gv)
        return jnp.sum(err * err, keepdims=True) * (0.5 / feat), dx, dg

    return _rowmap(fn, [x, g, target], "rcr", [((1, 1), F32, "a"), (x.shape, F32, "r"), (g.shape, F32, "a")],
                   name="loss_head", tl=256)


def _slot(ref, place, chip=None, half=None, piece=(0, 1)):
    axis, width = place
    shape = list(ref.shape)
    start = [0, 0]
    if chip is not None:
        start[axis], shape[axis] = chip * width, width
    if half is not None:
        h_axis = 0 if shape[0] % 32 == 0 else 1
        shape[h_axis] //= 2 * piece[1]
        start[h_axis] = start[h_axis] + (half * piece[1] + piece[0]) * shape[h_axis]
    hint = lambda s, d: s if isinstance(s, int) else pl.multiple_of(s, 128 if d == 1 else 8)
    return ref.at[tuple(pl.ds(hint(s, d), n) for d, (s, n) in enumerate(zip(start, shape)))]


class _Exchange:
    def __init__(self, kind, arrays, places, pieces=1):
        self.kind, self.arrays, self.places, self.n, self.pieces = kind, list(arrays), list(places), len(arrays), pieces
        self.out_shape = []
        for t, (axis, width) in zip(self.arrays, self.places):
            if kind == "gather":
                shape = list(t.shape)
                shape[axis] = N_CHIPS * width
            else:
                shape = [N_CHIPS] + list(t.shape)
                shape[1 + axis] = width
            self.out_shape.append(jax.ShapeDtypeStruct(tuple(shape), t.dtype))
        n = self.n
        self.scratch = [pltpu.SemaphoreType.DMA((3 * n * pieces,)) for _ in range(4 if kind == "gather" else 2)]
        self.scratch.append(pltpu.SemaphoreType.DMA((n,)))

    def _copies(self, ins, outs, sems):
        x, y, c = lax.axis_index("x"), lax.axis_index("y"), lax.axis_index("c")
        peers = [(1 - x, y), (x, 1 - y), (1 - x, 1 - y)]
        remote = lambda src, dst, send, recv, k, to: pltpu.make_async_remote_copy(
            src_ref=src, dst_ref=dst, send_sem=send.at[k], recv_sem=recv.at[k], device_id=to, device_id_type=MESH_ID)
        local, ici, d2d = [], [], []
        for a in range(self.n):
            place = self.places[a]
            if self.kind == "gather":
                local.append(pltpu.make_async_copy(ins[a], _slot(outs[a], place, 2 * x + y), sems[4].at[a]))
                for q in range(self.pieces):
                    for r, (px, py) in enumerate(peers):
                        k, part = (3 * a + r) * self.pieces + q, (q, self.pieces)
                        ici.append(remote(_slot(ins[a], place, None, c, part), _slot(outs[a], place, 2 * x + y, c, part),
                                          sems[0], sems[1], k, (px, py, c)))
                        landed = _slot(outs[a], place, 2 * px + py, c, part)
                        d2d.append(remote(landed, landed, sems[2], sems[3], k, (x, y, 1 - c)))
            else:
                local.append(pltpu.make_async_copy(_slot(ins[a], place, 2 * x + y), outs[a].at[3], sems[2].at[a]))
                for r, (px, py) in enumerate(peers):
                    ici.append(remote(_slot(ins[a], place, 2 * px + py), outs[a].at[r], sems[0], sems[1], 3 * a + r, (px, py, c)))
        return local, ici, d2d

    def start(self, ins, outs, sems):
        local, ici, _ = self._copies(ins, outs, sems)
        for cp in local + ici:
            cp.start()

    def relay(self, ins, outs, sems):
        _, ici, d2d = self._copies(ins, outs, sems)
        for arrived, onward in zip(ici, d2d):
            arrived.wait_recv()
            onward.start()

    def finish(self, ins, outs, sems):
        local, ici, d2d = self._copies(ins, outs, sems)
        for cp in local + d2d:
            cp.wait()
        for cp in ici:
            cp.wait_send() if d2d else cp.wait()


def _exchange_call(job, name):
    n = job.n

    def body(*refs):
        ins, outs, sems = refs[:n], refs[n:2 * n], refs[2 * n:]
        job.start(ins, outs, sems)
        job.relay(ins, outs, sems)
        job.finish(ins, outs, sems)

    return pl.pallas_call(
        body, name=name, in_specs=[ANY_SPEC] * n, out_specs=[ANY_SPEC] * n, out_shape=job.out_shape,
        scratch_shapes=job.scratch, compiler_params=pltpu.CompilerParams(has_side_effects=True),
    )(*job.arrays)


def _carried_call(body, *, name, grid, in_specs, out_specs, out_shape, semantics, operands, scratch_shapes=(), job=None):
    scratch_shapes = list(scratch_shapes)
    if job is None:
        return pl.pallas_call(body, name=name, grid=grid, in_specs=in_specs, out_specs=out_specs, out_shape=out_shape,
                              scratch_shapes=scratch_shapes, compiler_params=_params(*semantics))(*operands), []
    n_in, n_out, n, n_scr = len(in_specs), len(out_specs), job.n, len(scratch_shapes)
    steps = math.prod(grid)

    def wrapped(*refs):
        ins, job_ins = refs[:n_in], refs[n_in:n_in + n]
        outs, job_outs = refs[n_in + n:n_in + n + n_out], refs[n_in + n + n_out:n_in + 2 * n + n_out]
        outs = outs + refs[n_in + 2 * n + n_out:n_in + 2 * n + n_out + n_scr]
        sems = refs[n_in + 2 * n + n_out + n_scr:]
        step = functools.reduce(lambda acc, d: acc * grid[d] + pl.program_id(d), range(len(grid)), 0)

        @pl.when(step == 0)
        def _():
            job.start(job_ins, job_outs, sems)

        @pl.when(step == (3 * steps) // 4)
        def _():
            job.relay(job_ins, job_outs, sems)

        body(*ins, *outs)

        @pl.when(step == steps - 1)
        def _():
            job.finish(job_ins, job_outs, sems)

    res = pl.pallas_call(
        wrapped, name=name, grid=grid, in_specs=list(in_specs) + [ANY_SPEC] * n, out_specs=list(out_specs) + [ANY_SPEC] * n,
        out_shape=list(out_shape) + job.out_shape, scratch_shapes=scratch_shapes + job.scratch,
        compiler_params=pltpu.CompilerParams(dimension_semantics=("arbitrary",) * len(grid), vmem_limit_bytes=VMEM_LIMIT,
                                             has_side_effects=True),
    )(*operands, *job.arrays)
    return res[:n_out], res[n_out:]


def _swap_and_spread(parts, t):
    n = len(parts)

    def body(*refs):
        ins, t_ref, outs, slots = refs[:n], refs[n], refs[n + 1:2 * n + 1], refs[2 * n + 1]
        send, recv, all_send, all_recv, own = refs[2 * n + 2:]
        x, y, c = lax.axis_index("x"), lax.axis_index("y"), lax.axis_index("c")
        mine = slots.at[4 * x + 2 * y + c]
        copies = [pltpu.make_async_copy(t_ref, mine, own)]
        copies += [pltpu.make_async_remote_copy(src_ref=ins[a], dst_ref=outs[a], send_sem=send.at[a], recv_sem=recv.at[a],
                                                device_id=(x, y, 1 - c), device_id_type=MESH_ID) for a in range(n)]
        for m in range(1, 8):
            peer = (x ^ (m >> 2), y ^ ((m >> 1) & 1), c ^ (m & 1))
            copies.append(pltpu.make_async_remote_copy(src_ref=t_ref, dst_ref=mine, send_sem=all_send.at[m - 1],
                                                       recv_sem=all_recv.at[m - 1], device_id=peer, device_id_type=MESH_ID))
        for cp in copies:
            cp.start()
        for cp in copies:
            cp.wait()

    res = pl.pallas_call(
        body, name="swap_and_spread",
        in_specs=[ANY_SPEC] * (n + 1), out_specs=[ANY_SPEC] * (n + 1),
        out_shape=[jax.ShapeDtypeStruct(p.shape, p.dtype) for p in parts] + [jax.ShapeDtypeStruct((8,) + t.shape, t.dtype)],
        scratch_shapes=[pltpu.SemaphoreType.DMA((n,)), pltpu.SemaphoreType.DMA((n,)), pltpu.SemaphoreType.DMA((7,)),
                        pltpu.SemaphoreType.DMA((7,)), pltpu.SemaphoreType.DMA(())],
        compiler_params=pltpu.CompilerParams(has_side_effects=True),
    )(*parts, t)
    return res[:n], res[n]


def _adamw(w, g, m, v):
    m = ADAM_B1 * m + (1.0 - ADAM_B1) * g
    v = ADAM_B2 * v + (1.0 - ADAM_B2) * jnp.square(g)
    m_hat = m / (1.0 - ADAM_B1 ** ADAM_STEP)
    v_hat = v / (1.0 - ADAM_B2 ** ADAM_STEP)
    return -ADAM_LR * (m_hat / (jnp.sqrt(v_hat) + ADAM_EPS) + ADAM_WD * w), m, v


def _chip_sums(group, name):
    rows, cols = group[0].shape[1:]
    count = len(group)
    tl = _row_block(rows, 512 if count == 1 else 128, tile=32 // group[0].dtype.itemsize)

    def body(*refs):
        for r_ref, o_ref in zip(refs[:count], refs[count:]):
            total = ((r_ref[0].astype(F32) + r_ref[1].astype(F32)) + r_ref[2].astype(F32)) + r_ref[3].astype(F32)
            o_ref[...] = total.astype(o_ref.dtype)

    return pl.pallas_call(body, name=name, grid=(rows // tl,),
                          in_specs=[pl.BlockSpec((N_CHIPS, tl, cols), lambda i: (0, i, 0))] * count,
                          out_specs=[pl.BlockSpec((tl, cols), lambda i: (i, 0))] * count,
                          out_shape=[jax.ShapeDtypeStruct((rows, cols), group[0].dtype)] * count,
                          compiler_params=_params("parallel"))(*group)


def _adamw_layer(w, m, v, p_mine, p_other, layer, prev, name):
    _, rows, cols = w.shape
    assert p_mine.shape[1] == cols and p_mine.shape[0] >= rows
    tl = _row_block(rows, 512, tile=32 // p_mine.dtype.itemsize)
    tc = cols
    if tl < 128 < rows and cols % 256 == 0:
        tl, tc = rows, 256

    def body(w_ref, m_ref, v_ref, pa_ref, pb_ref, *rest):
        g = pa_ref[...].astype(F32) + pb_ref[...].astype(F32)
        for o_ref, val in zip(rest[-4:], (g,) + _adamw(w_ref[...], g, m_ref[...], v_ref[...])):
            o_ref[...] = val

    stacked = pl.BlockSpec((None, tl, tc), lambda i, j: (layer, i, j))
    part = pl.BlockSpec((tl, tc), lambda i, j: (i, j))
    kept = list(prev) if prev else []
    return pl.pallas_call(
        body, name=name, grid=(rows // tl, cols // tc),
        in_specs=[stacked] * 3 + [part] * 2 + [ANY_SPEC] * len(kept),
        out_specs=[stacked] * 4, out_shape=[jax.ShapeDtypeStruct(w.shape, F32)] * 4,
        input_output_aliases={5 + k: k for k in range(len(kept))},
        compiler_params=_params("parallel", "parallel"),
    )(w, m, v, p_mine, p_other, *kept)


def _adamw_small(w, slots, m, v):
    def fn(wv, sv, mv, vv):
        g = sv[0]
        for dev in range(1, 8):
            g = g + sv[dev]
        return (g,) + _adamw(wv, g, mv, vv)

    return _rowmap(fn, [w, slots, m, v], "rcrr", [(w.shape, F32, "r")] * 4, name="adamw_small", tl=w.shape[0])


WEIGHTS = ['ffn1_norm', 'ffn1_w_gate', 'ffn1_w_up', 'ffn1_w_down', 'mix_norm', 'ffn2_norm', 'ffn2_w_gate', 'ffn2_w_up',
           'ffn2_w_down', 'ab_w_in', 's5_lambda_re', 's5_lambda_im', 's5_log_dt', 's5_b_re', 's5_b_im', 's5_c_re', 's5_c_im',
           's5_d', 's5_w_glu', 'ab_w_out', 'sc_w_in', 'sc_conv_w', 'sc_w_out', 'final_norm']
SHARDED = {'ffn1_w_gate': (0, FF_SLOT), 'ffn1_w_up': (0, FF_SLOT), 'ffn1_w_down': (0, FF_SLOT),
           'ffn2_w_gate': (0, FF_SLOT), 'ffn2_w_up': (0, FF_SLOT), 'ffn2_w_down': (0, FF_SLOT),
           'ab_w_in': (1, 512), 's5_w_glu': (0, 128), 'ab_w_out': (0, 256), 'sc_w_in': (1, 768), 'sc_conv_w': (1, 256),
           'sc_w_out': (0, 256)}
SWAPPED = ('ffn1_w_gate', 'ffn1_w_up', 'ffn2_w_gate', 'ffn2_w_up')
SMALL = [n for n in WEIGHTS if n not in SHARDED]


def _held(name, t):
    return jnp.swapaxes(t, 1, 2) if name in SWAPPED else t


def _pack(arrays):
    rows = []
    for t in arrays:
        flat = t.reshape(-1)
        rows.append(jnp.pad(flat, (0, (-flat.shape[0]) % 128)))
    flat = jnp.concatenate(rows)
    return jnp.pad(flat, (0, (-flat.shape[0]) % 1024)).reshape(-1, 128)


def _unpack(packed, like):
    flat, out, pos = packed.reshape(-1), [], 0
    for t in like:
        out.append(flat[pos:pos + t.size].reshape(t.shape))
        pos += t.size + (-t.size) % 128
    return out


def _local_grads(x, target, p, full, late=None):
    small, grads, saved = {}, {}, []
    for layer in range(2):
        x, s1 = _ffn_fwd(x, p["ffn1_norm"][layer:layer + 1], full, "ffn1", layer, late)
        if layer == 0:
            x, sm = _mix0_fwd(x, p["mix_norm"][0:1], p, full, late)
        else:
            x, sm = _mix1_fwd(x, p["mix_norm"][1:2], full, late)
        x, s2 = _ffn_fwd(x, p["ffn2_norm"][layer:layer + 1], full, "ffn2", layer, late)
        saved.append((s1, sm, s2))
    loss, dx, dg_final = _loss_head(x, p["final_norm"][None], target)
    small["final_norm"] = dg_final[0]
    gains = {n: [None, None] for n in ("ffn1_norm", "mix_norm", "ffn2_norm")}

    def ffn_bwd(which, layer, dx, s):
        dx, dg = _ffn_bwd(dx, s, p[f"{which}_norm"][layer:layer + 1], full, which, layer, grads, late,
                          inline=(which, layer) in (("ffn2", 1), ("ffn1", 0)))
        gains[f"{which}_norm"][layer] = dg[0]
        return dx

    for layer in (1, 0):
        s1, sm, s2 = saved[layer]
        dx = ffn_bwd("ffn2", layer, dx, s2)
        if layer == 0:
            dx, dg, s5_grads = _mix0_bwd(dx, sm, p["mix_norm"][0:1], p, full, grads, late)
            small.update(s5_grads)
        else:
            dx, dg, dw_in, dcw, dw_out = _mix1_bwd(dx, sm, p["mix_norm"][1:2], full[("sc_w_in", 0)], full[("sc_conv_w", 0)],
                                                   full[("sc_w_out", 0)])
            grads.update({("sc_w_in", 0): dw_in, ("sc_conv_w", 0): dcw.astype(WIRE_DTYPE), ("sc_w_out", 0): dw_out})
        gains["mix_norm"][layer] = dg[0]
        dx = ffn_bwd("ffn1", layer, dx, s1)
    small.update({n: jnp.stack(pair) for n, pair in gains.items()})
    return loss, dx, small, grads


_GATHER_PLAN = {
    "gather_early": [("ffn1_w_gate", 0), ("ffn1_w_up", 0)],
    "ffn1_0_up": [("ffn1_w_down", 0), ("ab_w_in", 0)],
    "sb_fwd": [("s5_w_glu", 0), ("ab_w_out", 0), ("ffn2_w_gate", 0), ("ffn2_w_up", 0), ("ffn2_w_down", 0),
               ("ffn1_w_gate", 1), ("ffn1_w_up", 1), ("ffn1_w_down", 1)],
    "ffn2_0_up": [("sc_w_in", 0), ("sc_conv_w", 0), ("sc_w_out", 0)],
    "ffn1_1_up": [("ffn2_w_gate", 1), ("ffn2_w_up", 1)],
    "mix1_in": [("ffn2_w_down", 1)],
}


class _Late:
    def __init__(self, shards, places):
        self.shards, self.places = shards, places
        self.sent, self.received = set(), {}

    def gather_job(self, carrier):
        keys = _GATHER_PLAN.get(carrier, [])
        if not keys:
            return None, []
        pieces = 4 if carrier == "gather_early" else 1
        return _Exchange("gather", [self.shards[k] for k in keys], [self.places[k] for k in keys], pieces), keys

    def scatter_job(self, grads):
        keys = [k for k in grads if k not in self.sent]
        if not keys:
            return None, []
        self.sent.update(keys)
        return _Exchange("scatter", [grads[k] for k in keys], [self.places[k] for k in keys]), keys


def kernel(x, ffn1_norm, ffn1_w_gate, ffn1_w_up, ffn1_w_down, mix_norm, ffn2_norm, ffn2_w_gate, ffn2_w_up, ffn2_w_down, ab_w_in, s5_lambda_re, s5_lambda_im, s5_log_dt, s5_b_re, s5_b_im, s5_c_re, s5_c_im, s5_d, s5_w_glu, ab_w_out, sc_w_in, sc_conv_w, sc_w_out, final_norm, loss_target, m_ffn1_norm, m_ffn1_w_gate, m_ffn1_w_up, m_ffn1_w_down, m_mix_norm, m_ffn2_norm, m_ffn2_w_gate, m_ffn2_w_up, m_ffn2_w_down, m_ab_w_in, m_s5_lambda_re, m_s5_lambda_im, m_s5_log_dt, m_s5_b_re, m_s5_b_im, m_s5_c_re, m_s5_c_im, m_s5_d, m_s5_w_glu, m_ab_w_out, m_sc_w_in, m_sc_conv_w, m_sc_w_out, m_final_norm, v_ffn1_norm, v_ffn1_w_gate, v_ffn1_w_up, v_ffn1_w_down, v_mix_norm, v_ffn2_norm, v_ffn2_w_gate, v_ffn2_w_up, v_ffn2_w_down, v_ab_w_in, v_s5_lambda_re, v_s5_lambda_im, v_s5_log_dt, v_s5_b_re, v_s5_b_im, v_s5_c_re, v_s5_c_im, v_s5_d, v_s5_w_glu, v_ab_w_out, v_sc_w_in, v_sc_conv_w, v_sc_w_out, v_final_norm):
    args = dict(locals())
    p = {n: _held(n, args[n]) for n in WEIGHTS}
    mom = {n: _held(n, args["m_" + n]) for n in WEIGHTS}
    var = {n: _held(n, args["v_" + n]) for n in WEIGHTS}

    keys = [(n, layer) for n in SHARDED for layer in range(p[n].shape[0])]
    shards, places = {}, {}
    for n, layer in keys:
        axis, width = SHARDED[n]
        t = p[n][layer] if n == "sc_conv_w" else p[n][layer].astype(MXU_DTYPE)
        pad = [(0, 0), (0, 0)]
        pad[axis] = (0, width - t.shape[axis])
        shards[(n, layer)], places[(n, layer)] = jnp.pad(t, pad), (axis, width)
    late = _Late(shards, places)
    job, first = late.gather_job("gather_early")
    full = dict(zip(first, _exchange_call(job, "gather_early")))

    loss, dx, small, grads = _local_grads(x[0], loss_target[0], p, full, late)
    loss = lax.psum(loss[0, 0], ("x", "y", "c"))
    assert set(late.received) == set(keys), "a gradient was left without a carrier"

    alike = {}
    for key in keys:
        alike.setdefault((late.received[key].shape, late.received[key].dtype), []).append(key)
    summed = {}
    for group in alike.values():
        summed.update(zip(group, _chip_sums([late.received[k] for k in group], name=f"chip_sum_{group[0][0]}_x{len(group)}")))
    partial = [summed[key] for key in keys]
    other, small_slots = _swap_and_spread(partial, _pack([small[n] for n in SMALL]))
    out = {}
    for (n, layer), mine, theirs in zip(keys, partial, other):
        out[n] = _adamw_layer(p[n], mom[n], var[n], mine, theirs, layer, out.get(n), name=f"adamw_{n}_{layer}")
    out = {n: [_held(n, t) for t in res] for n, res in out.items()}

    like = [p[n] for n in SMALL]
    results = _adamw_small(_pack(like), small_slots, _pack([mom[n] for n in SMALL]), _pack([var[n] for n in SMALL]))
    for k, packed in enumerate(results):
        for n, t in zip(SMALL, _unpack(packed, like)):
            out.setdefault(n, [None] * 4)[k] = t

    return (loss, dx[None], *[out[n][0] for n in WEIGHTS], *[out[n][1] for n in WEIGHTS],
            *[out[n][2] for n in WEIGHTS], *[out[n][3] for n in WEIGHTS])
```

```python
import functools
import math

import jax
import jax.numpy as jnp
from jax import lax
from jax.experimental import pallas as pl
from jax.experimental.pallas import tpu as pltpu

F32 = jnp.float32
MXU_DTYPE = jnp.bfloat16
WIRE_DTYPE = jnp.bfloat16
MESH_ID = pl.DeviceIdType.MESH

N_CHIPS = 4
FF_SLOT = 768
FF_PAD = N_CHIPS * FF_SLOT
S5_WIDTH = 512
S5_GROUPS = 32
S5_STATE = 64
S5_LANES = S5_GROUPS * S5_STATE
S5_BLOCK = 512
S5_DIAG = S5_LANES // S5_BLOCK
SB_HEADS = 8
SB_DH = 64
SB_SCALE = 0.125
SB_PACK = 2
SB_QUERIES = 1024
SB_KEYS = 256
EPS = 1e-6
ADAM_LR, ADAM_B1, ADAM_B2, ADAM_EPS, ADAM_WD, ADAM_STEP = 0.001, 0.9, 0.999, 1e-08, 0.01, 10
VMEM_LIMIT = 56 * 1024 * 1024

ANY_SPEC = pl.BlockSpec(memory_space=pl.ANY)


def _params(*sem):
    return pltpu.CompilerParams(dimension_semantics=sem or None, vmem_limit_bytes=VMEM_LIMIT)


def _mm(a, b, *, name, ta=False, tb=False, out_dtype=F32, epilogue=None, extras=(), tm=1024, tn=1024, tk=1024, job=None):
    m, k = (a.shape[1], a.shape[0]) if ta else a.shape
    n = b.shape[0] if tb else b.shape[1]
    tm, tn, tk = min(tm, m), min(tn, n), min(tk, k)
    assert m % tm == 0 and n % tn == 0 and k % tk == 0, (name, m, n, k)
    grid = (m // tm, n // tn, k // tk)
    a_spec = pl.BlockSpec((tk, tm), lambda i, j, kk: (kk, i)) if ta else pl.BlockSpec((tm, tk), lambda i, j, kk: (i, kk))
    b_spec = pl.BlockSpec((tn, tk), lambda i, j, kk: (j, kk)) if tb else pl.BlockSpec((tk, tn), lambda i, j, kk: (kk, j))
    nk = grid[2]
    ex_specs = []
    for e in extras:
        if e.shape == (m, n):
            ex_specs.append(pl.BlockSpec((tm, tn), lambda i, j, kk: (i, j)))
        elif e.shape == (1, n):
            ex_specs.append(pl.BlockSpec((1, tn), lambda i, j, kk: (0, j)))
        else:
            assert e.shape == (m, 1), (name, e.shape)
            ex_specs.append(pl.BlockSpec((tm, 1), lambda i, j, kk: (i, 0)))
    dims = (((0 if ta else 1,), (1 if tb else 0,)), ((), ()))
    n_ex = len(extras)

    out_dtypes = list(out_dtype) if isinstance(out_dtype, (list, tuple)) else [out_dtype]
    n_out = len(out_dtypes)

    def body(a_ref, b_ref, *rest):
        ex_refs, o_refs = rest[:n_ex], rest[n_ex:n_ex + n_out]

        def product():
            return lax.dot_general(a_ref[...].astype(MXU_DTYPE), b_ref[...].astype(MXU_DTYPE), dims, preferred_element_type=F32)

        def finish(r):
            if epilogue is not None:
                r = epilogue(r, *[e[...] for e in ex_refs])
            for o_ref, val in zip(o_refs, r if isinstance(r, (tuple, list)) else (r,)):
                o_ref[...] = val.astype(o_ref.dtype)

        if nk == 1:
            finish(product())
            return
        acc_ref, kk = rest[n_ex + n_out], pl.program_id(2)

        @pl.when(kk == 0)
        def _():
            acc_ref[...] = jnp.zeros_like(acc_ref)

        acc_ref[...] += product()

        @pl.when(kk == nk - 1)
        def _():
            finish(acc_ref[...])

    res, got = _carried_call(
        body, name=name, grid=grid,
        in_specs=[a_spec, b_spec, *ex_specs],
        out_specs=[pl.BlockSpec((tm, tn), lambda i, j, kk: (i, j))] * n_out,
        out_shape=[jax.ShapeDtypeStruct((m, n), dt) for dt in out_dtypes],
        scratch_shapes=[pltpu.VMEM((tm, tn), F32)] if nk > 1 else [],
        semantics=("parallel", "parallel", "arbitrary"), operands=(a, b, *extras), job=job)
    res = res if isinstance(out_dtype, (list, tuple)) else res[0]
    return res if job is None else (res, got)


def _row_block(rows, want, tile=8):
    for tl in range(min(want, rows), tile - 1, -1):
        if rows % tl == 0 and tl % tile == 0:
            return tl
    return rows


def _rowmap(fn, ins, in_kinds, outs, *, name, tl):
    rows = next(x.shape[0] for x, kd in zip(ins, in_kinds) if kd == "r")
    tl = _row_block(rows, tl)
    n_in = len(ins)

    def spec(shape, kind):
        if kind == "r":
            return pl.BlockSpec((tl,) + tuple(shape[1:]), lambda i: (i,) + (0,) * (len(shape) - 1))
        return pl.BlockSpec(tuple(shape), lambda i: (0,) * len(shape))

    def body(*refs):
        in_refs, out_refs = refs[:n_in], refs[n_in:]
        res = fn(*[r[...] for r in in_refs])
        if not isinstance(res, (tuple, list)):
            res = (res,)
        for o_ref, val, (_, dt, kind) in zip(out_refs, res, outs):
            if kind == "r":
                o_ref[...] = val.astype(dt)
            else:
                @pl.when(pl.program_id(0) == 0)
                def _():
                    o_ref[...] = jnp.zeros_like(o_ref)

                o_ref[...] += val.astype(dt)

    has_acc = any(kd == "a" for _, _, kd in outs)
    res = pl.pallas_call(
        body, name=name, grid=(rows // tl,),
        in_specs=[spec(x.shape, kd) for x, kd in zip(ins, in_kinds)],
        out_specs=[spec(s, kd) for s, _, kd in outs],
        out_shape=[jax.ShapeDtypeStruct(s, dt) for s, dt, _ in outs],
        compiler_params=_params("arbitrary" if has_acc else "parallel"),
    )(*ins)
    return res[0] if len(outs) == 1 else res


def _rms_fwd(x, g):
    r = lax.rsqrt(jnp.mean(x * x, axis=-1, keepdims=True) + EPS)
    return x * r * g


def _rms_bwd(dh, x, g):
    r = lax.rsqrt(jnp.mean(x * x, axis=-1, keepdims=True) + EPS)
    xh = x * r
    dxh = dh * g
    dx = r * (dxh - xh * jnp.mean(dxh * xh, axis=-1, keepdims=True))
    return dx, jnp.sum(dh * xh, axis=0, keepdims=True)


def _swiglu_act(a, b):
    return jax.nn.silu(a) * b


def _ffn_up(x, g, wg, wu, *, name, tm=1024, tn=1024, job=None):
    m, d = x.shape
    n = wg.shape[0]
    tm, tn = min(tm, m), min(tn, n)
    assert m % tm == 0 and n % tn == 0, (name, m, n)

    def body(x_ref, g_ref, wg_ref, wu_ref, h_ref, a_ref, b_ref, s_ref):
        @pl.when(pl.program_id(1) == 0)
        def _():
            h_ref[...] = _rms_fwd(x_ref[...], g_ref[...]).astype(h_ref.dtype)

        hv = h_ref[...]
        av = lax.dot_general(hv, wg_ref[...], NT_DIMS, preferred_element_type=F32)
        bv = lax.dot_general(hv, wu_ref[...], NT_DIMS, preferred_element_type=F32)
        a_ref[...] = av.astype(a_ref.dtype)
        b_ref[...] = bv.astype(b_ref.dtype)
        s_ref[...] = _swiglu_act(av, bv).astype(s_ref.dtype)

    rows = pl.BlockSpec((tm, d), lambda i, j: (i, 0))
    wgt = pl.BlockSpec((tn, d), lambda i, j: (j, 0))
    tile = pl.BlockSpec((tm, tn), lambda i, j: (i, j))
    return _carried_call(
        body, name=name, grid=(m // tm, n // tn),
        in_specs=[rows, pl.BlockSpec((1, d), lambda i, j: (0, 0)), wgt, wgt],
        out_specs=[rows, tile, tile, tile],
        out_shape=[jax.ShapeDtypeStruct((m, d), MXU_DTYPE)] + [jax.ShapeDtypeStruct((m, n), MXU_DTYPE)] * 3,
        semantics=("parallel", "arbitrary"), operands=(x, g, wg, wu), job=job)


def _norm_proj(x, g, w, *, name, tm=1024, tn=1024, job=None):
    m, d = x.shape
    n = w.shape[1]
    tm, tn = min(tm, m), min(tn, n)
    assert m % tm == 0 and n % tn == 0, (name, m, n)

    def body(x_ref, g_ref, w_ref, h_ref, o_ref):
        @pl.when(pl.program_id(1) == 0)
        def _():
            h_ref[...] = _rms_fwd(x_ref[...], g_ref[...]).astype(h_ref.dtype)

        o_ref[...] = jnp.dot(h_ref[...], w_ref[...], preferred_element_type=F32)

    rows = pl.BlockSpec((tm, d), lambda i, j: (i, 0))
    return _carried_call(
        body, name=name, grid=(m // tm, n // tn),
        in_specs=[rows, pl.BlockSpec((1, d), lambda i, j: (0, 0)), pl.BlockSpec((d, tn), lambda i, j: (0, j))],
        out_specs=[rows, pl.BlockSpec((tm, tn), lambda i, j: (i, j))],
        out_shape=[jax.ShapeDtypeStruct((m, d), MXU_DTYPE), jax.ShapeDtypeStruct((m, n), F32)],
        semantics=("parallel", "arbitrary"), operands=(x, g, w), job=job)


def _proj_norm_bwd(pairs, x, g, dres, *, name, tm=1024, tk=1024, job=None):
    m, f = pairs[0][0].shape
    d = x.shape[1]
    tm, tk = min(tm, m), min(tk, f)
    assert m % tm == 0 and f % tk == 0, (name, m, f)
    nk, n_pairs = f // tk, len(pairs)
    swapped = [kept == "FD" for _, _, kept in pairs]

    def body(*refs):
        dy_refs, w_refs = refs[:n_pairs], refs[n_pairs:2 * n_pairs]
        x_ref, g_ref, dr_ref, dx_ref, dg_ref = refs[2 * n_pairs:2 * n_pairs + 5]
        i, kk = pl.program_id(0), pl.program_id(1)

        part = None
        for dy_ref, w_ref, rows_are_f in zip(dy_refs, w_refs, swapped):
            dims = (((1,), (0,)), ((), ())) if rows_are_f else NT_DIMS
            term = lax.dot_general(dy_ref[...].astype(MXU_DTYPE), w_ref[...], dims, preferred_element_type=F32)
            part = term if part is None else part + term

        @pl.when(jnp.logical_and(i == 0, kk == 0))
        def _():
            dg_ref[...] = jnp.zeros_like(dg_ref)

        def finish(dh):
            dx, dg = _rms_bwd(dh, x_ref[...], g_ref[...])
            dx_ref[...] = dx + dr_ref[...]
            dg_ref[...] += dg

        if nk == 1:
            finish(part)
            return
        acc_ref = refs[2 * n_pairs + 5]

        @pl.when(kk == 0)
        def _():
            acc_ref[...] = jnp.zeros_like(acc_ref)

        acc_ref[...] += part

        @pl.when(kk == nk - 1)
        def _():
            finish(acc_ref[...])

    act = pl.BlockSpec((tm, tk), lambda i, kk: (i, kk))
    w_specs = [pl.BlockSpec((tk, d), lambda i, kk: (kk, 0)) if s else pl.BlockSpec((d, tk), lambda i, kk: (0, kk)) for s in swapped]
    rows = pl.BlockSpec((tm, d), lambda i, kk: (i, 0))
    one = pl.BlockSpec((1, d), lambda i, kk: (0, 0))
    return _carried_call(
        body, name=name, grid=(m // tm, nk),
        in_specs=[act] * n_pairs + w_specs + [rows, one, rows],
        out_specs=[rows, one],
        out_shape=[jax.ShapeDtypeStruct((m, d), F32), jax.ShapeDtypeStruct((1, d), F32)],
        scratch_shapes=[pltpu.VMEM((tm, d), F32)] if nk > 1 else [],
        semantics=("arbitrary", "arbitrary"), operands=(*[p[0] for p in pairs], *[p[1] for p in pairs], x, g, dres), job=job)


def _ffn_dx(da, db, wg, wu, x, g, dres, *, name, job=None):
    return _proj_norm_bwd([(da, wg, "FD"), (db, wu, "FD")], x, g, dres, name=name, tm=512, tk=FF_PAD, job=job)


def _ffn_fwd(x, g, full, which, layer, late):
    tag = f"{which}_{layer}"
    job, keys = late.gather_job(f"{tag}_up") if late else (None, [])
    (h, a, b, s), got = _ffn_up(x, g, full[(f"{which}_w_gate", layer)], full[(f"{which}_w_up", layer)], name=f"{tag}_up", job=job)
    full.update(zip(keys, got))
    x2 = _mm(s, full[(f"{which}_w_down", layer)], name=f"{tag}_down", epilogue=lambda acc, xv: xv + 0.5 * acc, extras=[x],
             tk=FF_PAD)
    return x2, (x, h, a, b, s)


def _ffn_bwd(dx2, saved, g, full, which, layer, grads, late, inline):
    x, h, a, b, s = saved
    tag = f"{which}_{layer}"
    kg, ku, kd = [(f"{which}_w_{n}", layer) for n in ("gate", "up", "down")]
    wg, wu, wd = full[kg], full[ku], full[kd]
    send = (lambda only: late.scatter_job(grads, only)) if (late and inline) else (lambda only: (None, []))

    def act_bwd(ds, av, bv):
        a32, b32, half = av.astype(F32), bv.astype(F32), 0.5 * ds
        sig = jax.nn.sigmoid(a32)
        return half * b32 * (sig * (1.0 + a32 * (1.0 - sig))), half * (a32 * sig)

    grads[kd] = _mm(s, dx2, ta=True, name=f"{tag}_dwd", out_dtype=WIRE_DTYPE, epilogue=lambda acc: 0.5 * acc, tk=2048)
    job, keys = send([kd])
    (da, db), got = _carried(_mm, dx2, wd, tb=True, name=f"{tag}_dact", epilogue=act_bwd, extras=[a, b],
                             out_dtype=[MXU_DTYPE, MXU_DTYPE], job=job)
    _note(late, keys, got)
    grads[kg] = _mm(da, h, ta=True, name=f"{tag}_dwg", out_dtype=WIRE_DTYPE, tk=4096)
    job, keys = send([kg]) if inline == "chain" else (None, [])
    grads[ku], got = _carried(_mm, db, h, ta=True, name=f"{tag}_dwu", out_dtype=WIRE_DTYPE, tk=4096, job=job)
    _note(late, keys, got)
    job, keys = send([ku] if inline == "chain" else [kg])
    (dx, dg), got = _ffn_dx(da, db, wg, wu, x, g, dx2, name=f"{tag}_dx", job=job)
    _note(late, keys, got)
    return dx, dg


def _carried(fn, *args, job, **kwargs):
    return fn(*args, job=job, **kwargs) if job is not None else (fn(*args, **kwargs), [])


def _note(late, keys, got):
    if late:
        late.received.update(zip(keys, got))


def _softplus(z):
    return jnp.maximum(z, 0.0) + jnp.log(1.0 + jnp.exp(-jnp.abs(z)))


def _ones_dot(x, tri):
    if MXU_DTYPE == F32:
        return jnp.dot(x, tri, preferred_element_type=F32)
    hi = x.astype(MXU_DTYPE)
    lo = (x - hi.astype(F32)).astype(MXU_DTYPE)
    return jnp.dot(hi, tri, preferred_element_type=F32) + jnp.dot(lo, tri, preferred_element_type=F32)


NT_DIMS = (((1,), (1,)), ((), ()))
TN_DIMS = (((0,), (0,)), ((), ()))


SB_LANES = SB_PACK * SB_DH
Q_COL, K_COL, V_COL = (S5_WIDTH * n // SB_LANES for n in (1, 2, 3))


def _head_lanes(rows, hd):
    return lax.broadcasted_iota(jnp.int32, (rows, SB_LANES), 1) // SB_DH == hd


def _attend(proj, *, tq=SB_QUERIES, job=None):
    seq = proj.shape[0]
    tq = min(tq, seq)
    tk = min(SB_KEYS, tq)
    per, hp = tq // tk, SB_PACK

    def body(q_ref, k_ref, v_ref, o_ref, ls_ref):
        i = pl.program_id(1)
        r_idx = lax.broadcasted_iota(jnp.int32, (tk, tk), 0)
        c_idx = lax.broadcasted_iota(jnp.int32, (tk, tk), 1)
        after = (r_idx > c_idx).astype(MXU_DTYPE)
        lanes = [_head_lanes(tk, hd) for hd in range(hp)]

        def block(j, cs, acc, straddles):
            off = pl.multiple_of(j * tk, tk)
            k2, v2 = k_ref[pl.ds(off, tk), :], v_ref[pl.ds(off, tk), :]
            top = 0 if straddles is None else straddles * tk
            rows = tq - top
            q2 = (q_ref[pl.ds(top, rows), :] * SB_SCALE).astype(MXU_DTYPE)
            new_cs, out = [], acc[top:]
            for hd in range(hp):
                kv = jnp.where(lanes[hd], k2, 0.0).astype(MXU_DTYPE)
                vv = jnp.where(lanes[hd], v2, 0.0).astype(MXU_DTYPE)
                z = lax.dot_general(q2, kv, NT_DIMS, preferred_element_type=F32)
                sp = _softplus(z)
                c_in = cs[hd][top:]
                if straddles is None:
                    lk = -sp
                    w = jnp.exp(z - sp + _ones_dot(lk, after) + c_in)
                else:
                    before = lax.broadcasted_iota(jnp.int32, (rows, tk), 1) < lax.broadcasted_iota(jnp.int32, (rows, tk), 0)
                    lk = jnp.where(before, -sp, 0.0)
                    w = jnp.where(before, jnp.exp(z - sp + _ones_dot(lk, after) + c_in), 0.0)
                out = out + jnp.dot(w.astype(MXU_DTYPE), vv, preferred_element_type=F32)
                c_new = c_in + jnp.sum(lk, axis=1, keepdims=True)
                new_cs.append(jnp.concatenate([cs[hd][:top], c_new], axis=0) if top else c_new)
            return tuple(new_cs), (jnp.concatenate([acc[:top], out], axis=0) if top else out)

        carry = (tuple(jnp.zeros((tq, 1), F32) for _ in range(hp)), jnp.zeros((tq, SB_LANES), F32))
        for s in reversed(range(per)):
            carry = block(i * per + s, *carry, s)
        cs, acc = lax.fori_loop(0, i * per, lambda n, cr: block(i * per - 1 - n, *cr, None), carry)
        o_ref[...] = acc
        for hd in range(hp):
            ls_ref[hd] = cs[hd]

    whole = lambda col: pl.BlockSpec((seq, SB_LANES), lambda g, i: (0, col + g))
    return _carried_call(
        body, name="sb_fwd", grid=(SB_HEADS // hp, seq // tq),
        in_specs=[pl.BlockSpec((tq, SB_LANES), lambda g, i: (i, Q_COL + g)), whole(K_COL), whole(V_COL)],
        out_specs=[pl.BlockSpec((tq, SB_LANES), lambda g, i: (i, g)), pl.BlockSpec((hp, tq, 1), lambda g, i: (g, i, 0))],
        out_shape=[jax.ShapeDtypeStruct((seq, SB_HEADS * SB_DH), F32), jax.ShapeDtypeStruct((SB_HEADS, seq, 1), F32)],
        semantics=("parallel", "parallel"), operands=(proj, proj, proj), job=job)


def _attend_bwd(proj, lsum, dmix, *, tq=SB_QUERIES, job=None):
    seq = proj.shape[0]
    tq = min(tq, seq)
    tk = min(SB_KEYS, tq)
    per, hp = tq // tk, SB_PACK
    do_col = S5_WIDTH // SB_LANES

    def body(q_ref, k_ref, v_ref, ls_ref, do_ref, dq_ref, dk_ref, dv_ref):
        i = pl.program_id(1)

        @pl.when(i == 0)
        def _():
            dk_ref[...] = jnp.zeros_like(dk_ref)
            dv_ref[...] = jnp.zeros_like(dv_ref)

        r_idx = lax.broadcasted_iota(jnp.int32, (tk, tk), 0)
        c_idx = lax.broadcasted_iota(jnp.int32, (tk, tk), 1)
        upto = (r_idx <= c_idx).astype(MXU_DTYPE)
        before = (r_idx < c_idx).astype(MXU_DTYPE)
        lanes = [_head_lanes(tk, hd) for hd in range(hp)]

        def block(j, sums, dq, straddles):
            off = pl.multiple_of(j * tk, tk)
            k2, v2 = k_ref[pl.ds(off, tk), :], v_ref[pl.ds(off, tk), :]
            top = 0 if straddles is None else straddles * tk
            rows = tq - top
            part = pl.ds(top, rows)
            q2 = (q_ref[part, :] * SB_SCALE).astype(MXU_DTYPE)
            do2 = do_ref[part, :].astype(MXU_DTYPE)
            valid = None
            if straddles is not None:
                valid = lax.broadcasted_iota(jnp.int32, (rows, tk), 1) < lax.broadcasted_iota(jnp.int32, (rows, tk), 0)
            new_sums, out, dk, dv = [], dq[top:], jnp.zeros((tk, SB_LANES), F32), jnp.zeros((tk, SB_LANES), F32)
            for hd in range(hp):
                cp, ce = sums[hd]
                kv = jnp.where(lanes[hd], k2, 0.0).astype(MXU_DTYPE)
                vv = jnp.where(lanes[hd], v2, 0.0).astype(MXU_DTYPE)
                z = lax.dot_general(q2, kv, NT_DIMS, preferred_element_type=F32)
                sp = _softplus(z)
                lk = -sp if valid is None else jnp.where(valid, -sp, 0.0)
                w = jnp.exp(z - sp + (ls_ref[hd, part, :] - cp[top:]) - _ones_dot(lk, upto))
                if valid is not None:
                    w = jnp.where(valid, w, 0.0)
                e = w * lax.dot_general(do2, vv, NT_DIMS, preferred_element_type=F32)
                earlier = jnp.dot(e.astype(MXU_DTYPE), before, preferred_element_type=F32) + ce[top:]
                keep = jnp.exp(-sp)
                dz = e * keep - (1.0 - keep) * earlier
                if valid is not None:
                    dz = jnp.where(valid, dz, 0.0)
                dzm = dz.astype(MXU_DTYPE)
                out = out + jnp.dot(dzm, kv, preferred_element_type=F32)
                dk = dk + jnp.where(lanes[hd], lax.dot_general(dzm, q2, TN_DIMS, preferred_element_type=F32), 0.0)
                dv = dv + jnp.where(lanes[hd], lax.dot_general(w.astype(MXU_DTYPE), do2, TN_DIMS, preferred_element_type=F32), 0.0)
                new = (cp[top:] + jnp.sum(lk, axis=1, keepdims=True), ce[top:] + jnp.sum(e, axis=1, keepdims=True))
                new_sums.append(tuple(jnp.concatenate([old[:top], val], axis=0) for old, val in zip((cp, ce), new)) if top else new)
            dk_ref[pl.ds(off, tk), :] += dk
            dv_ref[pl.ds(off, tk), :] += dv
            return tuple(new_sums), (jnp.concatenate([dq[:top], out], axis=0) if top else out)

        zero = jnp.zeros((tq, 1), F32)
        carry = (tuple((zero, zero) for _ in range(hp)), jnp.zeros((tq, SB_LANES), F32))
        carry = lax.fori_loop(0, i * per, lambda j, cr: block(j, *cr, None), carry)
        for s in range(per):
            carry = block(i * per + s, *carry, s)
        dq_ref[...] = carry[1] * SB_SCALE

    whole = lambda col: pl.BlockSpec((seq, SB_LANES), lambda g, i: (0, col + g))
    tile = lambda col: pl.BlockSpec((tq, SB_LANES), lambda g, i: (i, col + g))
    acc = pl.BlockSpec((seq, SB_LANES), lambda g, i: (0, g))
    return _carried_call(
        body, name="sb_bwd", grid=(SB_HEADS // hp, seq // tq),
        in_specs=[tile(Q_COL), whole(K_COL), whole(V_COL), pl.BlockSpec((hp, tq, 1), lambda g, i: (g, i, 0)), tile(do_col)],
        out_specs=[tile(0), acc, acc],
        out_shape=[jax.ShapeDtypeStruct((seq, SB_HEADS * SB_DH), F32)] * 3,
        semantics=("parallel", "arbitrary"), operands=(proj, proj, proj, lsum, dmix), job=job)


def _s5_disc(lr, li, ldt, br, bi):
    dt = jnp.exp(ldt)
    mag = jnp.exp(lr * dt)
    ar = mag * jnp.cos(li * dt)
    ai = mag * jnp.sin(li * dt)
    den = lr * lr + li * li
    nr = ar - 1.0
    cr = (nr * lr + ai * li) / den
    ci = (ai * lr - nr * li) / den
    return ar, ai, cr[None] * br - ci[None] * bi, cr[None] * bi + ci[None] * br


def _s5_prep(lr, li, ldt, br, bi):
    shapes = [lr.shape, lr.shape, br.shape, br.shape]

    def body(lr_ref, li_ref, ldt_ref, br_ref, bi_ref, *outs):
        for o, val in zip(outs, _s5_disc(lr_ref[...], li_ref[...], ldt_ref[...], br_ref[...], bi_ref[...])):
            o[...] = val

    return pl.pallas_call(body, name="s5_prep", out_shape=[jax.ShapeDtypeStruct(s, F32) for s in shapes])(lr, li, ldt, br, bi)


def _s5_prep_bwd(lr, li, ldt, br, bi, cts):
    args = (lr, li, ldt, br, bi)

    def body(*refs):
        ins, ct_refs, outs = refs[:5], refs[5:9], refs[9:]
        _, vjp = jax.vjp(_s5_disc, *[r[...] for r in ins])
        for o, val in zip(outs, vjp(tuple(r[...] for r in ct_refs))):
            o[...] = val

    return pl.pallas_call(body, name="s5_prep_bwd", out_shape=[jax.ShapeDtypeStruct(a.shape, F32) for a in args])(*args, *cts)


SCAN_ROWS = 8


def _powers(ar, ai):
    out = [(ar, ai)]
    for _ in range(SCAN_ROWS - 1):
        pr, pi = out[-1]
        out.append((pr * ar - pi * ai, pr * ai + pi * ar))
    return out


def _s5_states(u, bmat, cmat, a, d, *, tc=512):
    seq, width = u.shape
    nj, cols, w2 = bmat.shape
    tw = w2 // 2
    tc = min(tc, seq)
    assert seq % tc == 0 and nj * cols == width and tw == S5_BLOCK

    def body(u_ref, bm_ref, cm_ref, a_ref, d_ref, h_ref, y_ref, cr_ref, ci_ref):
        @pl.when(pl.program_id(1) == 0)
        def _():
            cr_ref[...] = jnp.zeros_like(cr_ref)
            ci_ref[...] = jnp.zeros_like(ci_ref)

        uv = u_ref[...]
        h_ref[...] = jnp.dot(uv.astype(MXU_DTYPE), bm_ref[0], preferred_element_type=F32)
        re, im = pl.ds(0, tw), pl.ds(tw, tw)
        powers = _powers(a_ref[:, re], a_ref[:, im])
        pr = jnp.concatenate([p[0] for p in powers], axis=0)
        pi = jnp.concatenate([p[1] for p in powers], axis=0)
        row_id = lax.broadcasted_iota(jnp.int32, (SCAN_ROWS, tw), 0)
        reach = {dist: tuple(jnp.where(row_id >= dist, part, 0.0) for part in powers[dist - 1]) for dist in (1, 2, 4)}

        def block(n, carry):
            hr, hi = carry
            rows = pl.ds(pl.multiple_of(n * SCAN_ROWS, SCAN_ROWS), SCAN_ROWS)
            yr, yi = h_ref[rows, re], h_ref[rows, im]
            for dist in (1, 2, 4):
                cr, ci = reach[dist]
                sr, si = pltpu.roll(yr, dist, 0), pltpu.roll(yi, dist, 0)
                yr, yi = yr + cr * sr - ci * si, yi + cr * si + ci * sr
            yr, yi = yr + pr * hr - pi * hi, yi + pr * hi + pi * hr
            h_ref[rows, re] = yr
            h_ref[rows, im] = yi
            return yr[SCAN_ROWS - 1:], yi[SCAN_ROWS - 1:]

        hr, hi = lax.fori_loop(0, tc // SCAN_ROWS, block, (cr_ref[...], ci_ref[...]), unroll=4)
        cr_ref[...] = hr
        ci_ref[...] = hi
        y_ref[...] = jnp.dot(h_ref[...].astype(MXU_DTYPE), cm_ref[0], preferred_element_type=F32) + d_ref[...] * uv

    io = pl.BlockSpec((tc, cols), lambda j, t: (t, j))
    return pl.pallas_call(
        body, name="s5_states", grid=(nj, seq // tc),
        in_specs=[io, pl.BlockSpec((1, cols, w2), lambda j, t: (j, 0, 0)), pl.BlockSpec((1, w2, cols), lambda j, t: (j, 0, 0)),
                  pl.BlockSpec((1, w2), lambda j, t: (0, j)), pl.BlockSpec((1, cols), lambda j, t: (0, j))],
        out_specs=[pl.BlockSpec((tc, w2), lambda j, t: (t, j)), io],
        out_shape=[jax.ShapeDtypeStruct((seq, nj * w2), F32), jax.ShapeDtypeStruct((seq, width), F32)],
        scratch_shapes=[pltpu.VMEM((1, tw), F32)] * 2,
        compiler_params=_params("parallel", "arbitrary"),
    )(u, bmat, cmat, a, d)


def _s5_states_bwd(dy, h, u, bmat, cmat, a, du_skip, *, tc=512):
    seq, width = u.shape
    nj, cols, w2 = bmat.shape
    tw = w2 // 2
    tc = min(tc, seq)
    assert seq % tc == 0
    nt = seq // tc

    def body(dy_ref, h_ref, u_ref, bm_ref, cm_ref, a_ref, sk_ref, du_ref, da_ref, db_ref, dc_ref, g_ref, cr_ref, ci_ref):
        @pl.when(pl.program_id(1) == 0)
        def _():
            cr_ref[...] = jnp.zeros_like(cr_ref)
            ci_ref[...] = jnp.zeros_like(ci_ref)
            da_ref[...] = jnp.zeros_like(da_ref)
            db_ref[...] = jnp.zeros_like(db_ref)
            dc_ref[...] = jnp.zeros_like(dc_ref)

        dyv = dy_ref[...].astype(MXU_DTYPE)
        g_ref[...] = lax.dot_general(dyv, cm_ref[0], NT_DIMS, preferred_element_type=F32)
        re, im = pl.ds(0, tw), pl.ds(tw, tw)
        powers = _powers(a_ref[:, re], a_ref[:, im])
        pr = jnp.concatenate([p[0] for p in reversed(powers)], axis=0)
        pi = jnp.concatenate([p[1] for p in reversed(powers)], axis=0)
        row_id = lax.broadcasted_iota(jnp.int32, (SCAN_ROWS, tw), 0)
        last = SCAN_ROWS - 1
        reach = {dist: tuple(jnp.where(row_id < SCAN_ROWS - dist, part, 0.0) for part in powers[dist - 1]) for dist in (1, 2, 4)}

        def block(n, carry):
            gr, gi, sr, si = carry
            rows = pl.ds(pl.multiple_of((tc // SCAN_ROWS - 1 - n) * SCAN_ROWS, SCAN_ROWS), SCAN_ROWS)
            yr, yi = g_ref[rows, re], g_ref[rows, im]
            for dist in (1, 2, 4):
                cr, ci = reach[dist]
                ur, ui = pltpu.roll(yr, SCAN_ROWS - dist, 0), pltpu.roll(yi, SCAN_ROWS - dist, 0)
                yr, yi = yr + cr * ur + ci * ui, yi + cr * ui - ci * ur
            yr, yi = yr + pr * gr + pi * gi, yi + pr * gi - pi * gr
            g_ref[rows, re] = yr
            g_ref[rows, im] = yi
            nr = jnp.where(row_id < last, pltpu.roll(yr, last, 0), gr)
            ni = jnp.where(row_id < last, pltpu.roll(yi, last, 0), gi)
            hr, hi = h_ref[rows, re], h_ref[rows, im]
            return yr[:1], yi[:1], sr + nr * hr + ni * hi, si + ni * hr - nr * hi

        zero = jnp.zeros((SCAN_ROWS, tw), F32)
        gr, gi, sr, si = lax.fori_loop(0, tc // SCAN_ROWS, block, (cr_ref[...], ci_ref[...], zero, zero), unroll=4)
        cr_ref[...] = gr
        ci_ref[...] = gi
        da_ref[:, re] += jnp.sum(sr, axis=0, keepdims=True)
        da_ref[:, im] += jnp.sum(si, axis=0, keepdims=True)
        gv = g_ref[...].astype(MXU_DTYPE)
        du_ref[...] = (lax.dot_general(gv, bm_ref[0], NT_DIMS, preferred_element_type=F32) + sk_ref[...]).astype(du_ref.dtype)
        db_ref[0] += lax.dot_general(u_ref[...].astype(MXU_DTYPE), gv, TN_DIMS, preferred_element_type=F32)
        dc_ref[0] += lax.dot_general(h_ref[...].astype(MXU_DTYPE), dyv, TN_DIMS, preferred_element_type=F32)

    io = pl.BlockSpec((tc, cols), lambda j, t: (nt - 1 - t, j))
    bm = pl.BlockSpec((1, cols, w2), lambda j, t: (j, 0, 0))
    cm = pl.BlockSpec((1, w2, cols), lambda j, t: (j, 0, 0))
    row = pl.BlockSpec((1, w2), lambda j, t: (0, j))
    return pl.pallas_call(
        body, name="s5_states_bwd", grid=(nj, nt),
        in_specs=[io, pl.BlockSpec((tc, w2), lambda j, t: (nt - 1 - t, j)), io, bm, cm, row, io],
        out_specs=[io, row, bm, cm],
        out_shape=[jax.ShapeDtypeStruct((seq, width), MXU_DTYPE), jax.ShapeDtypeStruct((1, nj * w2), F32),
                   jax.ShapeDtypeStruct(bmat.shape, F32), jax.ShapeDtypeStruct(cmat.shape, F32)],
        scratch_shapes=[pltpu.VMEM((tc, w2), F32)] + [pltpu.VMEM((1, tw), F32)] * 2,
        compiler_params=_params("parallel", "arbitrary"),
    )(dy, h, u, bmat, cmat, a, du_skip)


def _pair_columns(re, im, axis):
    shape = re.shape
    split = shape[:axis] + (shape[axis] // S5_BLOCK, S5_BLOCK) + shape[axis + 1:]
    both = jnp.stack([re.reshape(split), im.reshape(split)], axis=axis + 1)
    return both.reshape(shape[:axis] + (2 * shape[axis],) + shape[axis + 1:])


def _unpair_columns(t, axis):
    shape = t.shape
    both = t.reshape(shape[:axis] + (shape[axis] // (2 * S5_BLOCK), 2, S5_BLOCK) + shape[axis + 1:])
    half = shape[:axis] + (shape[axis] // 2,) + shape[axis + 1:]
    return (lax.index_in_dim(both, 0, axis + 1, keepdims=False).reshape(half),
            lax.index_in_dim(both, 1, axis + 1, keepdims=False).reshape(half))


S5_PER_BLOCK = S5_GROUPS // S5_DIAG


def _block_diag(t):
    g, a, b = t.shape
    n = S5_PER_BLOCK
    eye = jnp.eye(n, dtype=t.dtype)
    return (t.reshape(g // n, n, a, 1, b) * eye[None, :, None, :, None]).reshape(g // n, n * a, n * b)


def _block_diag_part(m):
    j, n = m.shape[0], S5_PER_BLOCK
    a, b = m.shape[1] // n, m.shape[2] // n
    return jnp.moveaxis(jnp.diagonal(m.reshape(j, n, a, n, b), axis1=1, axis2=3), -1, 1).reshape(j * n, a, b)


def _gelu_glu(y, gate_pre):
    z = jax.nn.gelu(y)
    return z * jax.nn.sigmoid(gate_pre)


def _s5_fwd(u, p, w_glu):
    lr, li = p["s5_lambda_re"][0], p["s5_lambda_im"][0]
    ldt = p["s5_log_dt"][0][:, None]
    br = p["s5_b_re"][0].transpose(2, 0, 1)
    bi = p["s5_b_im"][0].transpose(2, 0, 1)
    ar, ai, bbr, bbi = _s5_prep(lr, li, ldt, br, bi)
    a = _pair_columns(ar.reshape(1, S5_LANES), ai.reshape(1, S5_LANES), 1)
    bmat = jnp.concatenate([_block_diag(bbr.transpose(1, 0, 2)), _block_diag(bbi.transpose(1, 0, 2))], axis=2)
    cmat = jnp.concatenate([_block_diag(p["s5_c_re"][0].transpose(0, 2, 1)),
                            -_block_diag(p["s5_c_im"][0].transpose(0, 2, 1))], axis=1)
    bmat, cmat = bmat.astype(MXU_DTYPE), cmat.astype(MXU_DTYPE)
    d = p["s5_d"]
    h, y = _s5_states(u, bmat, cmat, a, d)
    z = _rowmap(jax.nn.gelu, [y], "r", [(y.shape, MXU_DTYPE, "r")], name="s5_gelu", tl=512)
    gate_pre = _mm(z, w_glu, name="s5_glu")
    out = _rowmap(_gelu_glu, [y, gate_pre], "rr", [(y.shape, F32, "r")], name="s5_gate", tl=512)
    return out, (u, lr, li, ldt, br, bi, a, bmat, cmat, h, y, z, gate_pre)


def _s5_bwd(dout, saved, p, w_glu):
    u, lr, li, ldt, br, bi, a, bmat, cmat, h, y, z, gate_pre = saved
    d = p["s5_d"]

    def gate_bwd(dov, yv, gv):
        zv = jax.nn.gelu(yv)
        sg = jax.nn.sigmoid(gv)
        return dov * sg, dov * zv * sg * (1.0 - sg)

    dz_direct, dgate = _rowmap(gate_bwd, [dout, y, gate_pre], "rrr", [(y.shape, F32, "r"), (y.shape, MXU_DTYPE, "r")],
                               name="s5_gate_bwd", tl=512)
    dw_glu = _mm(z, dgate, ta=True, name="s5_dwglu", out_dtype=WIRE_DTYPE)
    dz = _mm(dgate, w_glu, tb=True, name="s5_dz", epilogue=lambda acc, prev: acc + prev, extras=[dz_direct])

    def gelu_bwd(dzv, yv, uv, dvv):
        _, vjp = jax.vjp(jax.nn.gelu, yv)
        dy = vjp(dzv)[0]
        return dy, dy * dvv, jnp.sum(dy * uv, axis=0, keepdims=True)

    dy, du_skip, dd = _rowmap(gelu_bwd, [dz, y, u, d], "rrrc",
                              [(y.shape, F32, "r"), (y.shape, F32, "r"), (d.shape, F32, "a")], name="s5_gelu_bwd", tl=512)
    du, da, dbmat, dcmat = _s5_states_bwd(dy, h, u, bmat, cmat, a, du_skip)
    dbbr, dbbi = (_block_diag_part(t).transpose(1, 0, 2) for t in (dbmat[:, :, :S5_BLOCK], dbmat[:, :, S5_BLOCK:]))
    dar, dai = _unpair_columns(da, 1)
    cts = (dar.reshape(S5_GROUPS, S5_STATE), dai.reshape(S5_GROUPS, S5_STATE), dbbr, dbbi)
    dlr, dli, dldt, dbr, dbi = _s5_prep_bwd(lr, li, ldt, br, bi, cts)
    dcr, dci = (_block_diag_part(t).transpose(0, 2, 1) for t in (dcmat[:, :S5_BLOCK], dcmat[:, S5_BLOCK:]))
    grads = {
        "s5_lambda_re": dlr[None], "s5_lambda_im": dli[None], "s5_log_dt": dldt[:, 0][None],
        "s5_b_re": dbr.transpose(1, 2, 0)[None], "s5_b_im": dbi.transpose(1, 2, 0)[None],
        "s5_c_re": dcr[None], "s5_c_im": -dci[None], "s5_d": dd,
    }
    return du, dw_glu, grads


def _mix0_fwd(x, g, p, full, late):
    (h, proj), _ = _norm_proj(x, g, full[("ab_w_in", 0)], name="mix0_in")
    u = proj[:, :S5_WIDTH]
    job, keys = late.gather_job("sb_fwd") if late else (None, [])
    (o, lsum), got = _attend(proj, job=job)
    full.update(zip(keys, got))
    w_glu, w_out = full[("s5_w_glu", 0)], full[("ab_w_out", 0)]
    y_a, s5_saved = _s5_fwd(u, p, w_glu)
    mix = jnp.concatenate([y_a, o], axis=1).astype(MXU_DTYPE)
    x2 = _mm(mix, w_out, name="mix0_out", epilogue=lambda acc, xv: xv + acc, extras=[x])
    return x2, (x, h, proj, lsum, mix, s5_saved)


def _mix0_bwd(dx2, saved, g, p, full, grads, late):
    x, h, proj, lsum, mix, s5_saved = saved
    w_in, w_glu, w_out = full[("ab_w_in", 0)], full[("s5_w_glu", 0)], full[("ab_w_out", 0)]
    dmix = _mm(dx2, w_out, tb=True, name="mix0_dmix")
    grads[("ab_w_out", 0)] = _mm(mix, dx2, ta=True, name="mix0_dwout", out_dtype=WIRE_DTYPE)
    du, grads[("s5_w_glu", 0)], s5_grads = _s5_bwd(dmix[:, :S5_WIDTH], s5_saved, p, w_glu)
    job, keys = late.scatter_job(grads) if late else (None, [])
    (dq, dk, dv), got = _attend_bwd(proj, lsum, dmix, job=job)
    _note(late, keys, got)
    dproj = jnp.concatenate([du] + [t.astype(MXU_DTYPE) for t in (dq, dk, dv)], axis=1)
    grads[("ab_w_in", 0)] = _mm(h, dproj, ta=True, name="mix0_dwin", out_dtype=WIRE_DTYPE)
    job, keys = late.scatter_job(grads) if late else (None, [])
    (dx, dg), got = _proj_norm_bwd([(dproj, w_in, "DF")], x, g, dx2, name="mix0_dh", job=job)
    _note(late, keys, got)
    return dx, dg, s5_grads


def _shift_down(t, n):
    rows = lax.broadcasted_iota(jnp.int32, t.shape, 0)
    return jnp.where(rows >= n, pltpu.roll(t, n, 0), 0.0)


def _shift_up(t, n):
    rows = lax.broadcasted_iota(jnp.int32, t.shape, 0)
    return jnp.where(rows < t.shape[0] - n, pltpu.roll(t, t.shape[0] - n, 0), 0.0)


def _conv_fwd(proj, cw, *, tc=128):
    seq, c3 = proj.shape
    ch = c3 // 3
    nb = ch // tc

    def body(b_ref, c_ref, v_ref, w_ref, m_ref):
        pv = c_ref[...] * v_ref[...]
        w = w_ref[...]
        y = w[2:3] * pv + w[1:2] * _shift_down(pv, 1) + w[0:1] * _shift_down(pv, 2)
        m_ref[...] = (b_ref[...] * y).astype(m_ref.dtype)

    col = lambda part: pl.BlockSpec((seq, tc), lambda j: (0, part * nb + j))
    return pl.pallas_call(
        body, name="conv_fwd", grid=(nb,),
        in_specs=[col(0), col(1), col(2), pl.BlockSpec((3, tc), lambda j: (0, j))],
        out_specs=pl.BlockSpec((seq, tc), lambda j: (0, j)),
        out_shape=jax.ShapeDtypeStruct((seq, ch), MXU_DTYPE),
        compiler_params=_params("parallel"),
    )(proj, proj, proj, cw)


def _conv_bwd(proj, cw, dm, *, tc=128):
    seq, c3 = proj.shape
    ch = c3 // 3
    nb = ch // tc

    def body(b_ref, c_ref, v_ref, w_ref, dm_ref, dproj_ref, dw_ref, dc_ref, dv_ref):
        part = pl.program_id(1)

        @pl.when(part == 0)
        def _():
            cv, vv, dmv = c_ref[...], v_ref[...], dm_ref[...]
            pv = cv * vv
            w = w_ref[...]
            p1, p2 = _shift_down(pv, 1), _shift_down(pv, 2)
            y = w[2:3] * pv + w[1:2] * p1 + w[0:1] * p2
            dproj_ref[...] = (dmv * y).astype(dproj_ref.dtype)
            dy = dmv * b_ref[...]
            dp = w[2:3] * dy + w[1:2] * _shift_up(dy, 1) + w[0:1] * _shift_up(dy, 2)
            dc_ref[...] = (dp * vv).astype(dc_ref.dtype)
            dv_ref[...] = (dp * cv).astype(dv_ref.dtype)
            dw_ref[...] = jnp.concatenate([jnp.sum(dy * p2, axis=0, keepdims=True), jnp.sum(dy * p1, axis=0, keepdims=True),
                                           jnp.sum(dy * pv, axis=0, keepdims=True)], axis=0)

        @pl.when(part == 1)
        def _():
            dproj_ref[...] = dc_ref[...]

        @pl.when(part == 2)
        def _():
            dproj_ref[...] = dv_ref[...]

    col = lambda part: pl.BlockSpec((seq, tc), lambda j, t: (0, part * nb + j))
    small = pl.BlockSpec((3, tc), lambda j, t: (0, j))
    return pl.pallas_call(
        body, name="conv_bwd", grid=(nb, 3),
        in_specs=[col(0), col(1), col(2), small, pl.BlockSpec((seq, tc), lambda j, t: (0, j))],
        out_specs=[pl.BlockSpec((seq, tc), lambda j, t: (0, t * nb + j)), small],
        out_shape=[jax.ShapeDtypeStruct((seq, c3), MXU_DTYPE), jax.ShapeDtypeStruct((3, ch), F32)],
        scratch_shapes=[pltpu.VMEM((seq, tc), MXU_DTYPE)] * 2,
        compiler_params=_params("parallel", "arbitrary"),
    )(proj, proj, proj, cw, dm)


def _mix1_fwd(x, g, full, late):
    job, keys = late.gather_job("mix1_in") if late else (None, [])
    (h, proj), got = _norm_proj(x, g, full[("sc_w_in", 0)], name="mix1_in", job=job)
    full.update(zip(keys, got))
    m = _conv_fwd(proj, full[("sc_conv_w", 0)])
    x2 = _mm(m, full[("sc_w_out", 0)], name="mix1_out", epilogue=lambda acc, xv: xv + acc, extras=[x])
    return x2, (x, h, proj, m)


def _mix1_bwd(dx2, saved, g, full, grads, late):
    x, h, proj, m = saved
    w_in, cw, w_out = full[("sc_w_in", 0)], full[("sc_conv_w", 0)], full[("sc_w_out", 0)]
    dm = _mm(dx2, w_out, tb=True, name="mix1_dm")
    grads[("sc_w_out", 0)] = _mm(m, dx2, ta=True, name="mix1_dwout", out_dtype=WIRE_DTYPE)
    dproj, dcw = _conv_bwd(proj, cw, dm)
    grads[("sc_conv_w", 0)] = dcw.astype(WIRE_DTYPE)
    grads[("sc_w_in", 0)] = _mm(h, dproj, ta=True, name="mix1_dwin", out_dtype=WIRE_DTYPE)
    job, keys = late.scatter_job(grads, [("ffn2_w_up", 1)]) if late else (None, [])
    (dx, dg), got = _proj_norm_bwd([(dproj, w_in, "DF")], x, g, dx2, name="mix1_dh", job=job)
    _note(late, keys, got)
    return dx, dg


def _loss_head(x, g, target):
    feat = x.shape[1]

    def fn(xv, gv, tv):
        err = _rms_fwd(xv, gv) - tv
        dx, dg = _rms_bwd(err / feat, xv, gv)
        return jnp.sum(err * err, keepdims=True) * (0.5 / feat), dx, dg

    return _rowmap(fn, [x, g, target], "rcr", [((1, 1), F32, "a"), (x.shape, F32, "r"), (g.shape, F32, "a")],
                   name="loss_head", tl=256)


def _slot(ref, place, chip=None, half=None, piece=(0, 1)):
    axis, width = place
    shape = list(ref.shape)
    start = [0, 0]
    if chip is not None:
        start[axis], shape[axis] = chip * width, width
    if half is not None:
        h_axis = 0 if shape[0] % 32 == 0 else 1
        shape[h_axis] //= 2 * piece[1]
        start[h_axis] = start[h_axis] + (half * piece[1] + piece[0]) * shape[h_axis]
    hint = lambda s, d: s if isinstance(s, int) else pl.multiple_of(s, 128 if d == 1 else 8)
    return ref.at[tuple(pl.ds(hint(s, d), n) for d, (s, n) in enumerate(zip(start, shape)))]


class _Exchange:
    def __init__(self, kind, arrays, places, pieces=1):
        self.kind, self.arrays, self.places, self.n, self.pieces = kind, list(arrays), list(places), len(arrays), pieces
        self.out_shape = []
        for t, (axis, width) in zip(self.arrays, self.places):
            if kind == "gather":
                shape = list(t.shape)
                shape[axis] = N_CHIPS * width
            else:
                shape = [N_CHIPS] + list(t.shape)
                shape[1 + axis] = width
            self.out_shape.append(jax.ShapeDtypeStruct(tuple(shape), t.dtype))
        n = self.n
        self.scratch = [pltpu.SemaphoreType.DMA((3 * n * pieces,)) for _ in range(4 if kind == "gather" else 2)]
        self.scratch.append(pltpu.SemaphoreType.DMA((n,)))

    def _copies(self, ins, outs, sems):
        x, y, c = lax.axis_index("x"), lax.axis_index("y"), lax.axis_index("c")
        peers = [(1 - x, y), (x, 1 - y), (1 - x, 1 - y)]
        remote = lambda src, dst, send, recv, k, to: pltpu.make_async_remote_copy(
            src_ref=src, dst_ref=dst, send_sem=send.at[k], recv_sem=recv.at[k], device_id=to, device_id_type=MESH_ID)
        local, ici, d2d = [], [], []
        for a in range(self.n):
            place = self.places[a]
            if self.kind == "gather":
                local.append(pltpu.make_async_copy(ins[a], _slot(outs[a], place, 2 * x + y), sems[4].at[a]))
                for q in range(self.pieces):
                    for r, (px, py) in enumerate(peers):
                        k, part = (3 * a + r) * self.pieces + q, (q, self.pieces)
                        ici.append(remote(_slot(ins[a], place, None, c, part), _slot(outs[a], place, 2 * x + y, c, part),
                                          sems[0], sems[1], k, (px, py, c)))
                        landed = _slot(outs[a], place, 2 * px + py, c, part)
                        d2d.append(remote(landed, landed, sems[2], sems[3], k, (x, y, 1 - c)))
            else:
                local.append(pltpu.make_async_copy(_slot(ins[a], place, 2 * x + y), outs[a].at[3], sems[2].at[a]))
                for r, (px, py) in enumerate(peers):
                    ici.append(remote(_slot(ins[a], place, 2 * px + py), outs[a].at[r], sems[0], sems[1], 3 * a + r, (px, py, c)))
        return local, ici, d2d

    def start(self, ins, outs, sems):
        local, ici, _ = self._copies(ins, outs, sems)
        for cp in local + ici:
            cp.start()

    def relay(self, ins, outs, sems):
        _, ici, d2d = self._copies(ins, outs, sems)
        for arrived, onward in zip(ici, d2d):
            arrived.wait_recv()
            onward.start()

    def finish(self, ins, outs, sems):
        local, ici, d2d = self._copies(ins, outs, sems)
        for cp in local + d2d:
            cp.wait()
        for cp in ici:
            cp.wait_send() if d2d else cp.wait()


def _exchange_call(job, name):
    n = job.n

    def body(*refs):
        ins, outs, sems = refs[:n], refs[n:2 * n], refs[2 * n:]
        job.start(ins, outs, sems)
        job.relay(ins, outs, sems)
        job.finish(ins, outs, sems)

    return pl.pallas_call(
        body, name=name, in_specs=[ANY_SPEC] * n, out_specs=[ANY_SPEC] * n, out_shape=job.out_shape,
        scratch_shapes=job.scratch, compiler_params=pltpu.CompilerParams(has_side_effects=True),
    )(*job.arrays)


def _carried_call(body, *, name, grid, in_specs, out_specs, out_shape, semantics, operands, scratch_shapes=(), job=None):
    scratch_shapes = list(scratch_shapes)
    if job is None:
        return pl.pallas_call(body, name=name, grid=grid, in_specs=in_specs, out_specs=out_specs, out_shape=out_shape,
                              scratch_shapes=scratch_shapes, compiler_params=_params(*semantics))(*operands), []
    n_in, n_out, n, n_scr = len(in_specs), len(out_specs), job.n, len(scratch_shapes)
    steps = math.prod(grid)

    def wrapped(*refs):
        ins, job_ins = refs[:n_in], refs[n_in:n_in + n]
        outs, job_outs = refs[n_in + n:n_in + n + n_out], refs[n_in + n + n_out:n_in + 2 * n + n_out]
        outs = outs + refs[n_in + 2 * n + n_out:n_in + 2 * n + n_out + n_scr]
        sems = refs[n_in + 2 * n + n_out + n_scr:]
        step = functools.reduce(lambda acc, d: acc * grid[d] + pl.program_id(d), range(len(grid)), 0)

        @pl.when(step == 0)
        def _():
            job.start(job_ins, job_outs, sems)

        @pl.when(step == (3 * steps) // 4)
        def _():
            job.relay(job_ins, job_outs, sems)

        body(*ins, *outs)

        @pl.when(step == steps - 1)
        def _():
            job.finish(job_ins, job_outs, sems)

    res = pl.pallas_call(
        wrapped, name=name, grid=grid, in_specs=list(in_specs) + [ANY_SPEC] * n, out_specs=list(out_specs) + [ANY_SPEC] * n,
        out_shape=list(out_shape) + job.out_shape, scratch_shapes=scratch_shapes + job.scratch,
        compiler_params=pltpu.CompilerParams(dimension_semantics=("arbitrary",) * len(grid), vmem_limit_bytes=VMEM_LIMIT,
                                             has_side_effects=True),
    )(*operands, *job.arrays)
    return res[:n_out], res[n_out:]


def _swap_and_spread(parts, t):
    n = len(parts)

    def body(*refs):
        ins, t_ref, outs, slots = refs[:n], refs[n], refs[n + 1:2 * n + 1], refs[2 * n + 1]
        send, recv, all_send, all_recv, own = refs[2 * n + 2:]
        x, y, c = lax.axis_index("x"), lax.axis_index("y"), lax.axis_index("c")
        mine = slots.at[4 * x + 2 * y + c]
        copies = [pltpu.make_async_copy(t_ref, mine, own)]
        copies += [pltpu.make_async_remote_copy(src_ref=ins[a], dst_ref=outs[a], send_sem=send.at[a], recv_sem=recv.at[a],
                                                device_id=(x, y, 1 - c), device_id_type=MESH_ID) for a in range(n)]
        for m in range(1, 8):
            peer = (x ^ (m >> 2), y ^ ((m >> 1) & 1), c ^ (m & 1))
            copies.append(pltpu.make_async_remote_copy(src_ref=t_ref, dst_ref=mine, send_sem=all_send.at[m - 1],
                                                       recv_sem=all_recv.at[m - 1], device_id=peer, device_id_type=MESH_ID))
        for cp in copies:
            cp.start()
        for cp in copies:
            cp.wait()

    res = pl.pallas_call(
        body, name="swap_and_spread",
        in_specs=[ANY_SPEC] * (n + 1), out_specs=[ANY_SPEC] * (n + 1),
        out_shape=[jax.ShapeDtypeStruct(p.shape, p.dtype) for p in parts] + [jax.ShapeDtypeStruct((8,) + t.shape, t.dtype)],
        scratch_shapes=[pltpu.SemaphoreType.DMA((n,)), pltpu.SemaphoreType.DMA((n,)), pltpu.SemaphoreType.DMA((7,)),
                        pltpu.SemaphoreType.DMA((7,)), pltpu.SemaphoreType.DMA(())],
        compiler_params=pltpu.CompilerParams(has_side_effects=True),
    )(*parts, t)
    return res[:n], res[n]


def _adamw(w, g, m, v):
    m = ADAM_B1 * m + (1.0 - ADAM_B1) * g
    v = ADAM_B2 * v + (1.0 - ADAM_B2) * jnp.square(g)
    m_hat = m / (1.0 - ADAM_B1 ** ADAM_STEP)
    v_hat = v / (1.0 - ADAM_B2 ** ADAM_STEP)
    return -ADAM_LR * (m_hat / (jnp.sqrt(v_hat) + ADAM_EPS) + ADAM_WD * w), m, v


def _chip_sums(group, name):
    rows, cols = group[0].shape[1:]
    count = len(group)
    tl = _row_block(rows, 512 if count == 1 else 128, tile=32 // group[0].dtype.itemsize)

    def body(*refs):
        for r_ref, o_ref in zip(refs[:count], refs[count:]):
            total = ((r_ref[0].astype(F32) + r_ref[1].astype(F32)) + r_ref[2].astype(F32)) + r_ref[3].astype(F32)
            o_ref[...] = total.astype(o_ref.dtype)

    return pl.pallas_call(body, name=name, grid=(rows // tl,),
                          in_specs=[pl.BlockSpec((N_CHIPS, tl, cols), lambda i: (0, i, 0))] * count,
                          out_specs=[pl.BlockSpec((tl, cols), lambda i: (i, 0))] * count,
                          out_shape=[jax.ShapeDtypeStruct((rows, cols), group[0].dtype)] * count,
                          compiler_params=_params("parallel"))(*group)


def _adamw_layer(w, m, v, p_mine, p_other, layer, prev, name):
    _, rows, cols = w.shape
    assert p_mine.shape[1] == cols and p_mine.shape[0] >= rows
    tl = _row_block(rows, 512, tile=32 // p_mine.dtype.itemsize)
    tc = cols
    if tl < 128 < rows and cols % 256 == 0:
        tl, tc = rows, 256

    def body(w_ref, m_ref, v_ref, pa_ref, pb_ref, *rest):
        g = pa_ref[...].astype(F32) + pb_ref[...].astype(F32)
        for o_ref, val in zip(rest[-4:], (g,) + _adamw(w_ref[...], g, m_ref[...], v_ref[...])):
            o_ref[...] = val

    stacked = pl.BlockSpec((None, tl, tc), lambda i, j: (layer, i, j))
    part = pl.BlockSpec((tl, tc), lambda i, j: (i, j))
    kept = list(prev) if prev else []
    return pl.pallas_call(
        body, name=name, grid=(rows // tl, cols // tc),
        in_specs=[stacked] * 3 + [part] * 2 + [ANY_SPEC] * len(kept),
        out_specs=[stacked] * 4, out_shape=[jax.ShapeDtypeStruct(w.shape, F32)] * 4,
        input_output_aliases={5 + k: k for k in range(len(kept))},
        compiler_params=_params("parallel", "parallel"),
    )(w, m, v, p_mine, p_other, *kept)


def _adamw_small(w, slots, m, v):
    def fn(wv, sv, mv, vv):
        g = sv[0]
        for dev in range(1, 8):
            g = g + sv[dev]
        return (g,) + _adamw(wv, g, mv, vv)

    return _rowmap(fn, [w, slots, m, v], "rcrr", [(w.shape, F32, "r")] * 4, name="adamw_small", tl=w.shape[0])


WEIGHTS = ['ffn1_norm', 'ffn1_w_gate', 'ffn1_w_up', 'ffn1_w_down', 'mix_norm', 'ffn2_norm', 'ffn2_w_gate', 'ffn2_w_up',
           'ffn2_w_down', 'ab_w_in', 's5_lambda_re', 's5_lambda_im', 's5_log_dt', 's5_b_re', 's5_b_im', 's5_c_re', 's5_c_im',
           's5_d', 's5_w_glu', 'ab_w_out', 'sc_w_in', 'sc_conv_w', 'sc_w_out', 'final_norm']
SHARDED = {'ffn1_w_gate': (0, FF_SLOT), 'ffn1_w_up': (0, FF_SLOT), 'ffn1_w_down': (0, FF_SLOT),
           'ffn2_w_gate': (0, FF_SLOT), 'ffn2_w_up': (0, FF_SLOT), 'ffn2_w_down': (0, FF_SLOT),
           'ab_w_in': (1, 512), 's5_w_glu': (0, 128), 'ab_w_out': (0, 256), 'sc_w_in': (1, 768), 'sc_conv_w': (1, 256),
           'sc_w_out': (0, 256)}
SWAPPED = ('ffn1_w_gate', 'ffn1_w_up', 'ffn2_w_gate', 'ffn2_w_up')
SMALL = [n for n in WEIGHTS if n not in SHARDED]


def _held(name, t):
    return jnp.swapaxes(t, 1, 2) if name in SWAPPED else t


def _pack(arrays):
    rows = []
    for t in arrays:
        flat = t.reshape(-1)
        rows.append(jnp.pad(flat, (0, (-flat.shape[0]) % 128)))
    flat = jnp.concatenate(rows)
    return jnp.pad(flat, (0, (-flat.shape[0]) % 1024)).reshape(-1, 128)


def _unpack(packed, like):
    flat, out, pos = packed.reshape(-1), [], 0
    for t in like:
        out.append(flat[pos:pos + t.size].reshape(t.shape))
        pos += t.size + (-t.size) % 128
    return out


def _local_grads(x, target, p, full, late=None):
    small, grads, saved = {}, {}, []
    for layer in range(2):
        x, s1 = _ffn_fwd(x, p["ffn1_norm"][layer:layer + 1], full, "ffn1", layer, late)
        if layer == 0:
            x, sm = _mix0_fwd(x, p["mix_norm"][0:1], p, full, late)
        else:
            x, sm = _mix1_fwd(x, p["mix_norm"][1:2], full, late)
        x, s2 = _ffn_fwd(x, p["ffn2_norm"][layer:layer + 1], full, "ffn2", layer, late)
        saved.append((s1, sm, s2))
    loss, dx, dg_final = _loss_head(x, p["final_norm"][None], target)
    small["final_norm"] = dg_final[0]
    gains = {n: [None, None] for n in ("ffn1_norm", "mix_norm", "ffn2_norm")}

    def ffn_bwd(which, layer, dx, s):
        dx, dg = _ffn_bwd(dx, s, p[f"{which}_norm"][layer:layer + 1], full, which, layer, grads, late,
                          inline={("ffn2", 1): "defer", ("ffn1", 0): "chain"}.get((which, layer)))
        gains[f"{which}_norm"][layer] = dg[0]
        return dx

    for layer in (1, 0):
        s1, sm, s2 = saved[layer]
        dx = ffn_bwd("ffn2", layer, dx, s2)
        if layer == 0:
            dx, dg, s5_grads = _mix0_bwd(dx, sm, p["mix_norm"][0:1], p, full, grads, late)
            small.update(s5_grads)
        else:
            dx, dg = _mix1_bwd(dx, sm, p["mix_norm"][1:2], full, grads, late)
        gains["mix_norm"][layer] = dg[0]
        dx = ffn_bwd("ffn1", layer, dx, s1)
    small.update({n: jnp.stack(pair) for n, pair in gains.items()})
    return loss, dx, small, grads


_GATHER_PLAN = {
    "gather_early": [("ffn1_w_gate", 0), ("ffn1_w_up", 0)],
    "ffn1_0_up": [("ffn1_w_down", 0), ("ab_w_in", 0)],
    "sb_fwd": [("s5_w_glu", 0), ("ab_w_out", 0), ("ffn2_w_gate", 0), ("ffn2_w_up", 0), ("ffn2_w_down", 0),
               ("ffn1_w_gate", 1), ("ffn1_w_up", 1), ("ffn1_w_down", 1)],
    "ffn2_0_up": [("sc_w_in", 0), ("sc_conv_w", 0), ("sc_w_out", 0)],
    "ffn1_1_up": [("ffn2_w_gate", 1), ("ffn2_w_up", 1)],
    "mix1_in": [("ffn2_w_down", 1)],
}


class _Late:
    def __init__(self, shards, places):
        self.shards, self.places = shards, places
        self.sent, self.received = set(), {}

    def gather_job(self, carrier):
        keys = _GATHER_PLAN.get(carrier, [])
        if not keys:
            return None, []
        pieces = 4 if carrier == "gather_early" else 1
        return _Exchange("gather", [self.shards[k] for k in keys], [self.places[k] for k in keys], pieces), keys

    def scatter_job(self, grads, only=None):
        keys = [k for k in grads if k not in self.sent and (only is None or k in only)]
        if not keys:
            return None, []
        self.sent.update(keys)
        return _Exchange("scatter", [grads[k] for k in keys], [self.places[k] for k in keys]), keys


def kernel(x, ffn1_norm, ffn1_w_gate, ffn1_w_up, ffn1_w_down, mix_norm, ffn2_norm, ffn2_w_gate, ffn2_w_up, ffn2_w_down, ab_w_in, s5_lambda_re, s5_lambda_im, s5_log_dt, s5_b_re, s5_b_im, s5_c_re, s5_c_im, s5_d, s5_w_glu, ab_w_out, sc_w_in, sc_conv_w, sc_w_out, final_norm, loss_target, m_ffn1_norm, m_ffn1_w_gate, m_ffn1_w_up, m_ffn1_w_down, m_mix_norm, m_ffn2_norm, m_ffn2_w_gate, m_ffn2_w_up, m_ffn2_w_down, m_ab_w_in, m_s5_lambda_re, m_s5_lambda_im, m_s5_log_dt, m_s5_b_re, m_s5_b_im, m_s5_c_re, m_s5_c_im, m_s5_d, m_s5_w_glu, m_ab_w_out, m_sc_w_in, m_sc_conv_w, m_sc_w_out, m_final_norm, v_ffn1_norm, v_ffn1_w_gate, v_ffn1_w_up, v_ffn1_w_down, v_mix_norm, v_ffn2_norm, v_ffn2_w_gate, v_ffn2_w_up, v_ffn2_w_down, v_ab_w_in, v_s5_lambda_re, v_s5_lambda_im, v_s5_log_dt, v_s5_b_re, v_s5_b_im, v_s5_c_re, v_s5_c_im, v_s5_d, v_s5_w_glu, v_ab_w_out, v_sc_w_in, v_sc_conv_w, v_sc_w_out, v_final_norm):
    args = dict(locals())
    p = {n: _held(n, args[n]) for n in WEIGHTS}
    mom = {n: _held(n, args["m_" + n]) for n in WEIGHTS}
    var = {n: _held(n, args["v_" + n]) for n in WEIGHTS}

    keys = [(n, layer) for n in SHARDED for layer in range(p[n].shape[0])]
    shards, places = {}, {}
    for n, layer in keys:
        axis, width = SHARDED[n]
        t = p[n][layer] if n == "sc_conv_w" else p[n][layer].astype(MXU_DTYPE)
        pad = [(0, 0), (0, 0)]
        pad[axis] = (0, width - t.shape[axis])
        shards[(n, layer)], places[(n, layer)] = jnp.pad(t, pad), (axis, width)
    late = _Late(shards, places)
    job, first = late.gather_job("gather_early")
    full = dict(zip(first, _exchange_call(job, "gather_early")))

    loss, dx, small, grads = _local_grads(x[0], loss_target[0], p, full, late)
    loss = lax.psum(loss[0, 0], ("x", "y", "c"))
    assert set(late.received) == set(keys), "a gradient was left without a carrier"

    alike = {}
    for key in keys:
        alike.setdefault((late.received[key].shape, late.received[key].dtype), []).append(key)
    summed = {}
    for group in alike.values():
        summed.update(zip(group, _chip_sums([late.received[k] for k in group], name=f"chip_sum_{group[0][0]}_x{len(group)}")))
    partial = [summed[key] for key in keys]
    other, small_slots = _swap_and_spread(partial, _pack([small[n] for n in SMALL]))
    out = {}
    for (n, layer), mine, theirs in zip(keys, partial, other):
        out[n] = _adamw_layer(p[n], mom[n], var[n], mine, theirs, layer, out.get(n), name=f"adamw_{n}_{layer}")
    out = {n: [_held(n, t) for t in res] for n, res in out.items()}

    like = [p[n] for n in SMALL]
    results = _adamw_small(_pack(like), small_slots, _pack([mom[n] for n in SMALL]), _pack([var[n] for n in SMALL]))
    for k, packed in enumerate(results):
        for n, t in zip(SMALL, _unpack(packed, like)):
            out.setdefault(n, [None] * 4)[k] = t

    return (loss, dx[None], *[out[n][0] for n in WEIGHTS], *[out[n][1] for n in WEIGHTS],
            *[out[n][2] for n in WEIGHTS], *[out[n][3] for n in WEIGHTS])
```

```python
import functools
import math

import jax
import jax.numpy as jnp
from jax import lax
from jax.experimental import pallas as pl
from jax.experimental.pallas import tpu as pltpu

F32 = jnp.float32
MXU_DTYPE = jnp.bfloat16
WIRE_DTYPE = jnp.bfloat16
MESH_ID = pl.DeviceIdType.MESH

N_CHIPS = 4
FF_SLOT = 768
FF_PAD = N_CHIPS * FF_SLOT
S5_WIDTH = 512
S5_GROUPS = 32
S5_STATE = 64
S5_LANES = S5_GROUPS * S5_STATE
S5_BLOCK = 512
S5_DIAG = S5_LANES // S5_BLOCK
SB_HEADS = 8
SB_DH = 64
SB_SCALE = 0.125
SB_PACK = 2
SB_QUERIES = 1024
SB_KEYS = 256
EPS = 1e-6
ADAM_LR, ADAM_B1, ADAM_B2, ADAM_EPS, ADAM_WD, ADAM_STEP = 0.001, 0.9, 0.999, 1e-08, 0.01, 10
VMEM_LIMIT = 56 * 1024 * 1024

ANY_SPEC = pl.BlockSpec(memory_space=pl.ANY)


def _params(*sem):
    return pltpu.CompilerParams(dimension_semantics=sem or None, vmem_limit_bytes=VMEM_LIMIT)


def _mm(a, b, *, name, ta=False, tb=False, out_dtype=F32, epilogue=None, extras=(), tm=1024, tn=1024, tk=1024, job=None):
    m, k = (a.shape[1], a.shape[0]) if ta else a.shape
    n = b.shape[0] if tb else b.shape[1]
    tm, tn, tk = min(tm, m), min(tn, n), min(tk, k)
    assert m % tm == 0 and n % tn == 0 and k % tk == 0, (name, m, n, k)
    grid = (m // tm, n // tn, k // tk)
    a_spec = pl.BlockSpec((tk, tm), lambda i, j, kk: (kk, i)) if ta else pl.BlockSpec((tm, tk), lambda i, j, kk: (i, kk))
    b_spec = pl.BlockSpec((tn, tk), lambda i, j, kk: (j, kk)) if tb else pl.BlockSpec((tk, tn), lambda i, j, kk: (kk, j))
    nk = grid[2]
    ex_specs = []
    for e in extras:
        if e.shape == (m, n):
            ex_specs.append(pl.BlockSpec((tm, tn), lambda i, j, kk: (i, j)))
        elif e.shape == (1, n):
            ex_specs.append(pl.BlockSpec((1, tn), lambda i, j, kk: (0, j)))
        else:
            assert e.shape == (m, 1), (name, e.shape)
            ex_specs.append(pl.BlockSpec((tm, 1), lambda i, j, kk: (i, 0)))
    dims = (((0 if ta else 1,), (1 if tb else 0,)), ((), ()))
    n_ex = len(extras)

    out_dtypes = list(out_dtype) if isinstance(out_dtype, (list, tuple)) else [out_dtype]
    n_out = len(out_dtypes)

    def body(a_ref, b_ref, *rest):
        ex_refs, o_refs = rest[:n_ex], rest[n_ex:n_ex + n_out]

        def product():
            return lax.dot_general(a_ref[...].astype(MXU_DTYPE), b_ref[...].astype(MXU_DTYPE), dims, preferred_element_type=F32)

        def finish(r):
            if epilogue is not None:
                r = epilogue(r, *[e[...] for e in ex_refs])
            for o_ref, val in zip(o_refs, r if isinstance(r, (tuple, list)) else (r,)):
                o_ref[...] = val.astype(o_ref.dtype)

        if nk == 1:
            finish(product())
            return
        acc_ref, kk = rest[n_ex + n_out], pl.program_id(2)

        @pl.when(kk == 0)
        def _():
            acc_ref[...] = jnp.zeros_like(acc_ref)

        acc_ref[...] += product()

        @pl.when(kk == nk - 1)
        def _():
            finish(acc_ref[...])

    res, got = _carried_call(
        body, name=name, grid=grid,
        in_specs=[a_spec, b_spec, *ex_specs],
        out_specs=[pl.BlockSpec((tm, tn), lambda i, j, kk: (i, j))] * n_out,
        out_shape=[jax.ShapeDtypeStruct((m, n), dt) for dt in out_dtypes],
        scratch_shapes=[pltpu.VMEM((tm, tn), F32)] if nk > 1 else [],
        semantics=("parallel", "parallel", "arbitrary"), operands=(a, b, *extras), job=job)
    res = res if isinstance(out_dtype, (list, tuple)) else res[0]
    return res if job is None else (res, got)


def _row_block(rows, want, tile=8):
    for tl in range(min(want, rows), tile - 1, -1):
        if rows % tl == 0 and tl % tile == 0:
            return tl
    return rows


def _rowmap(fn, ins, in_kinds, outs, *, name, tl):
    rows = next(x.shape[0] for x, kd in zip(ins, in_kinds) if kd == "r")
    tl = _row_block(rows, tl)
    n_in = len(ins)

    def spec(shape, kind):
        if kind == "r":
            return pl.BlockSpec((tl,) + tuple(shape[1:]), lambda i: (i,) + (0,) * (len(shape) - 1))
        return pl.BlockSpec(tuple(shape), lambda i: (0,) * len(shape))

    def body(*refs):
        in_refs, out_refs = refs[:n_in], refs[n_in:]
        res = fn(*[r[...] for r in in_refs])
        if not isinstance(res, (tuple, list)):
            res = (res,)
        for o_ref, val, (_, dt, kind) in zip(out_refs, res, outs):
            if kind == "r":
                o_ref[...] = val.astype(dt)
            else:
                @pl.when(pl.program_id(0) == 0)
                def _():
                    o_ref[...] = jnp.zeros_like(o_ref)

                o_ref[...] += val.astype(dt)

    has_acc = any(kd == "a" for _, _, kd in outs)
    res = pl.pallas_call(
        body, name=name, grid=(rows // tl,),
        in_specs=[spec(x.shape, kd) for x, kd in zip(ins, in_kinds)],
        out_specs=[spec(s, kd) for s, _, kd in outs],
        out_shape=[jax.ShapeDtypeStruct(s, dt) for s, dt, _ in outs],
        compiler_params=_params("arbitrary" if has_acc else "parallel"),
    )(*ins)
    return res[0] if len(outs) == 1 else res


def _rms_fwd(x, g):
    r = lax.rsqrt(jnp.mean(x * x, axis=-1, keepdims=True) + EPS)
    return x * r * g


def _rms_bwd(dh, x, g):
    r = lax.rsqrt(jnp.mean(x * x, axis=-1, keepdims=True) + EPS)
    xh = x * r
    dxh = dh * g
    dx = r * (dxh - xh * jnp.mean(dxh * xh, axis=-1, keepdims=True))
    return dx, jnp.sum(dh * xh, axis=0, keepdims=True)


def _swiglu_act(a, b):
    return jax.nn.silu(a) * b


def _ffn_up(x, g, wg, wu, *, name, tm=1024, tn=1024, job=None):
    m, d = x.shape
    n = wg.shape[0]
    tm, tn = min(tm, m), min(tn, n)
    assert m % tm == 0 and n % tn == 0, (name, m, n)

    def body(x_ref, g_ref, wg_ref, wu_ref, h_ref, a_ref, b_ref, s_ref):
        @pl.when(pl.program_id(1) == 0)
        def _():
            h_ref[...] = _rms_fwd(x_ref[...], g_ref[...]).astype(h_ref.dtype)

        hv = h_ref[...]
        av = lax.dot_general(hv, wg_ref[...], NT_DIMS, preferred_element_type=F32)
        bv = lax.dot_general(hv, wu_ref[...], NT_DIMS, preferred_element_type=F32)
        a_ref[...] = av.astype(a_ref.dtype)
        b_ref[...] = bv.astype(b_ref.dtype)
        s_ref[...] = _swiglu_act(av, bv).astype(s_ref.dtype)

    rows = pl.BlockSpec((tm, d), lambda i, j: (i, 0))
    wgt = pl.BlockSpec((tn, d), lambda i, j: (j, 0))
    tile = pl.BlockSpec((tm, tn), lambda i, j: (i, j))
    return _carried_call(
        body, name=name, grid=(m // tm, n // tn),
        in_specs=[rows, pl.BlockSpec((1, d), lambda i, j: (0, 0)), wgt, wgt],
        out_specs=[rows, tile, tile, tile],
        out_shape=[jax.ShapeDtypeStruct((m, d), MXU_DTYPE)] + [jax.ShapeDtypeStruct((m, n), MXU_DTYPE)] * 3,
        semantics=("parallel", "arbitrary"), operands=(x, g, wg, wu), job=job)


def _norm_proj(x, g, w, *, name, tm=1024, tn=1024, job=None):
    m, d = x.shape
    n = w.shape[1]
    tm, tn = min(tm, m), min(tn, n)
    assert m % tm == 0 and n % tn == 0, (name, m, n)

    def body(x_ref, g_ref, w_ref, h_ref, o_ref):
        @pl.when(pl.program_id(1) == 0)
        def _():
            h_ref[...] = _rms_fwd(x_ref[...], g_ref[...]).astype(h_ref.dtype)

        o_ref[...] = jnp.dot(h_ref[...], w_ref[...], preferred_element_type=F32)

    rows = pl.BlockSpec((tm, d), lambda i, j: (i, 0))
    return _carried_call(
        body, name=name, grid=(m // tm, n // tn),
        in_specs=[rows, pl.BlockSpec((1, d), lambda i, j: (0, 0)), pl.BlockSpec((d, tn), lambda i, j: (0, j))],
        out_specs=[rows, pl.BlockSpec((tm, tn), lambda i, j: (i, j))],
        out_shape=[jax.ShapeDtypeStruct((m, d), MXU_DTYPE), jax.ShapeDtypeStruct((m, n), F32)],
        semantics=("parallel", "arbitrary"), operands=(x, g, w), job=job)


def _proj_norm_bwd(pairs, x, g, dres, *, name, tm=1024, tk=1024, job=None):
    m, f = pairs[0][0].shape
    d = x.shape[1]
    tm, tk = min(tm, m), min(tk, f)
    assert m % tm == 0 and f % tk == 0, (name, m, f)
    nk, n_pairs = f // tk, len(pairs)
    swapped = [kept == "FD" for _, _, kept in pairs]

    def body(*refs):
        dy_refs, w_refs = refs[:n_pairs], refs[n_pairs:2 * n_pairs]
        x_ref, g_ref, dr_ref, dx_ref, dg_ref = refs[2 * n_pairs:2 * n_pairs + 5]
        i, kk = pl.program_id(0), pl.program_id(1)

        part = None
        for dy_ref, w_ref, rows_are_f in zip(dy_refs, w_refs, swapped):
            dims = (((1,), (0,)), ((), ())) if rows_are_f else NT_DIMS
            term = lax.dot_general(dy_ref[...].astype(MXU_DTYPE), w_ref[...], dims, preferred_element_type=F32)
            part = term if part is None else part + term

        @pl.when(jnp.logical_and(i == 0, kk == 0))
        def _():
            dg_ref[...] = jnp.zeros_like(dg_ref)

        def finish(dh):
            dx, dg = _rms_bwd(dh, x_ref[...], g_ref[...])
            dx_ref[...] = dx + dr_ref[...]
            dg_ref[...] += dg

        if nk == 1:
            finish(part)
            return
        acc_ref = refs[2 * n_pairs + 5]

        @pl.when(kk == 0)
        def _():
            acc_ref[...] = jnp.zeros_like(acc_ref)

        acc_ref[...] += part

        @pl.when(kk == nk - 1)
        def _():
            finish(acc_ref[...])

    act = pl.BlockSpec((tm, tk), lambda i, kk: (i, kk))
    w_specs = [pl.BlockSpec((tk, d), lambda i, kk: (kk, 0)) if s else pl.BlockSpec((d, tk), lambda i, kk: (0, kk)) for s in swapped]
    rows = pl.BlockSpec((tm, d), lambda i, kk: (i, 0))
    one = pl.BlockSpec((1, d), lambda i, kk: (0, 0))
    return _carried_call(
        body, name=name, grid=(m // tm, nk),
        in_specs=[act] * n_pairs + w_specs + [rows, one, rows],
        out_specs=[rows, one],
        out_shape=[jax.ShapeDtypeStruct((m, d), F32), jax.ShapeDtypeStruct((1, d), F32)],
        scratch_shapes=[pltpu.VMEM((tm, d), F32)] if nk > 1 else [],
        semantics=("arbitrary", "arbitrary"), operands=(*[p[0] for p in pairs], *[p[1] for p in pairs], x, g, dres), job=job)


def _ffn_dx(da, db, wg, wu, x, g, dres, *, name, job=None):
    return _proj_norm_bwd([(da, wg, "FD"), (db, wu, "FD")], x, g, dres, name=name, tm=512, tk=FF_PAD, job=job)


def _ffn_fwd(x, g, full, which, layer, late):
    tag = f"{which}_{layer}"
    job, keys = late.gather_job(f"{tag}_up") if late else (None, [])
    (h, a, b, s), got = _ffn_up(x, g, full[(f"{which}_w_gate", layer)], full[(f"{which}_w_up", layer)], name=f"{tag}_up", job=job)
    full.update(zip(keys, got))
    job, keys = late.gather_job(f"{tag}_down") if late else (None, [])
    x2, got = _carried(_mm, s, full[(f"{which}_w_down", layer)], name=f"{tag}_down", epilogue=lambda acc, xv: xv + 0.5 * acc,
                       extras=[x], tk=FF_PAD, job=job)
    full.update(zip(keys, got))
    return x2, (x, h, a, b, s)


def _ffn_bwd(dx2, saved, g, full, which, layer, grads, late, inline):
    x, h, a, b, s = saved
    tag = f"{which}_{layer}"
    kg, ku, kd = [(f"{which}_w_{n}", layer) for n in ("gate", "up", "down")]
    wg, wu, wd = full[kg], full[ku], full[kd]
    send = (lambda only: late.scatter_job(grads, only)) if (late and inline) else (lambda only: (None, []))

    def act_bwd(ds, av, bv):
        a32, b32, half = av.astype(F32), bv.astype(F32), 0.5 * ds
        sig = jax.nn.sigmoid(a32)
        return half * b32 * (sig * (1.0 + a32 * (1.0 - sig))), half * (a32 * sig)

    grads[kd] = _mm(s, dx2, ta=True, name=f"{tag}_dwd", out_dtype=WIRE_DTYPE, epilogue=lambda acc: 0.5 * acc, tk=2048)
    job, keys = send([kd])
    (da, db), got = _carried(_mm, dx2, wd, tb=True, name=f"{tag}_dact", epilogue=act_bwd, extras=[a, b],
                             out_dtype=[MXU_DTYPE, MXU_DTYPE], job=job)
    _note(late, keys, got)
    grads[kg] = _mm(da, h, ta=True, name=f"{tag}_dwg", out_dtype=WIRE_DTYPE, tk=4096)
    job, keys = send([kg]) if inline == "chain" else (None, [])
    grads[ku], got = _carried(_mm, db, h, ta=True, name=f"{tag}_dwu", out_dtype=WIRE_DTYPE, tk=4096, job=job)
    _note(late, keys, got)
    job, keys = send([ku] if inline == "chain" else [kg])
    (dx, dg), got = _ffn_dx(da, db, wg, wu, x, g, dx2, name=f"{tag}_dx", job=job)
    _note(late, keys, got)
    return dx, dg


def _carried(fn, *args, job, **kwargs):
    return fn(*args, job=job, **kwargs) if job is not None else (fn(*args, **kwargs), [])


def _note(late, keys, got):
    if late:
        late.received.update(zip(keys, got))


def _softplus(z):
    return jnp.maximum(z, 0.0) + jnp.log(1.0 + jnp.exp(-jnp.abs(z)))


def _ones_dot(x, tri):
    if MXU_DTYPE == F32:
        return jnp.dot(x, tri, preferred_element_type=F32)
    hi = x.astype(MXU_DTYPE)
    lo = (x - hi.astype(F32)).astype(MXU_DTYPE)
    return jnp.dot(hi, tri, preferred_element_type=F32) + jnp.dot(lo, tri, preferred_element_type=F32)


NT_DIMS = (((1,), (1,)), ((), ()))
TN_DIMS = (((0,), (0,)), ((), ()))


SB_LANES = SB_PACK * SB_DH
Q_COL, K_COL, V_COL = (S5_WIDTH * n // SB_LANES for n in (1, 2, 3))


def _head_lanes(rows, hd):
    return lax.broadcasted_iota(jnp.int32, (rows, SB_LANES), 1) // SB_DH == hd


def _attend(proj, *, tq=SB_QUERIES, job=None):
    seq = proj.shape[0]
    tq = min(tq, seq)
    tk = min(SB_KEYS, tq)
    per, hp = tq // tk, SB_PACK

    def body(q_ref, k_ref, v_ref, o_ref, ls_ref):
        i = pl.program_id(1)
        r_idx = lax.broadcasted_iota(jnp.int32, (tk, tk), 0)
        c_idx = lax.broadcasted_iota(jnp.int32, (tk, tk), 1)
        after = (r_idx > c_idx).astype(MXU_DTYPE)
        lanes = [_head_lanes(tk, hd) for hd in range(hp)]

        def block(j, cs, acc, straddles):
            off = pl.multiple_of(j * tk, tk)
            k2, v2 = k_ref[pl.ds(off, tk), :], v_ref[pl.ds(off, tk), :]
            top = 0 if straddles is None else straddles * tk
            rows = tq - top
            q2 = (q_ref[pl.ds(top, rows), :] * SB_SCALE).astype(MXU_DTYPE)
            new_cs, out = [], acc[top:]
            for hd in range(hp):
                kv = jnp.where(lanes[hd], k2, 0.0).astype(MXU_DTYPE)
                vv = jnp.where(lanes[hd], v2, 0.0).astype(MXU_DTYPE)
                z = lax.dot_general(q2, kv, NT_DIMS, preferred_element_type=F32)
                sp = _softplus(z)
                c_in = cs[hd][top:]
                if straddles is None:
                    lk = -sp
                    w = jnp.exp(z - sp + _ones_dot(lk, after) + c_in)
                else:
                    before = lax.broadcasted_iota(jnp.int32, (rows, tk), 1) < lax.broadcasted_iota(jnp.int32, (rows, tk), 0)
                    lk = jnp.where(before, -sp, 0.0)
                    w = jnp.where(before, jnp.exp(z - sp + _ones_dot(lk, after) + c_in), 0.0)
                out = out + jnp.dot(w.astype(MXU_DTYPE), vv, preferred_element_type=F32)
                c_new = c_in + jnp.sum(lk, axis=1, keepdims=True)
                new_cs.append(jnp.concatenate([cs[hd][:top], c_new], axis=0) if top else c_new)
            return tuple(new_cs), (jnp.concatenate([acc[:top], out], axis=0) if top else out)

        carry = (tuple(jnp.zeros((tq, 1), F32) for _ in range(hp)), jnp.zeros((tq, SB_LANES), F32))
        for s in reversed(range(per)):
            carry = block(i * per + s, *carry, s)
        cs, acc = lax.fori_loop(0, i * per, lambda n, cr: block(i * per - 1 - n, *cr, None), carry)
        o_ref[...] = acc
        for hd in range(hp):
            ls_ref[hd] = cs[hd]

    whole = lambda col: pl.BlockSpec((seq, SB_LANES), lambda g, i: (0, col + g))
    return _carried_call(
        body, name="sb_fwd", grid=(SB_HEADS // hp, seq // tq),
        in_specs=[pl.BlockSpec((tq, SB_LANES), lambda g, i: (i, Q_COL + g)), whole(K_COL), whole(V_COL)],
        out_specs=[pl.BlockSpec((tq, SB_LANES), lambda g, i: (i, g)), pl.BlockSpec((hp, tq, 1), lambda g, i: (g, i, 0))],
        out_shape=[jax.ShapeDtypeStruct((seq, SB_HEADS * SB_DH), F32), jax.ShapeDtypeStruct((SB_HEADS, seq, 1), F32)],
        semantics=("parallel", "parallel"), operands=(proj, proj, proj), job=job)


def _attend_bwd(proj, lsum, dmix, *, tq=SB_QUERIES, job=None):
    seq = proj.shape[0]
    tq = min(tq, seq)
    tk = min(SB_KEYS, tq)
    per, hp = tq // tk, SB_PACK
    do_col = S5_WIDTH // SB_LANES

    def body(q_ref, k_ref, v_ref, ls_ref, do_ref, dq_ref, dk_ref, dv_ref):
        i = pl.program_id(1)

        @pl.when(i == 0)
        def _():
            dk_ref[...] = jnp.zeros_like(dk_ref)
            dv_ref[...] = jnp.zeros_like(dv_ref)

        r_idx = lax.broadcasted_iota(jnp.int32, (tk, tk), 0)
        c_idx = lax.broadcasted_iota(jnp.int32, (tk, tk), 1)
        upto = (r_idx <= c_idx).astype(MXU_DTYPE)
        before = (r_idx < c_idx).astype(MXU_DTYPE)
        lanes = [_head_lanes(tk, hd) for hd in range(hp)]

        def block(j, sums, dq, straddles):
            off = pl.multiple_of(j * tk, tk)
            k2, v2 = k_ref[pl.ds(off, tk), :], v_ref[pl.ds(off, tk), :]
            top = 0 if straddles is None else straddles * tk
            rows = tq - top
            part = pl.ds(top, rows)
            q2 = (q_ref[part, :] * SB_SCALE).astype(MXU_DTYPE)
            do2 = do_ref[part, :].astype(MXU_DTYPE)
            valid = None
            if straddles is not None:
                valid = lax.broadcasted_iota(jnp.int32, (rows, tk), 1) < lax.broadcasted_iota(jnp.int32, (rows, tk), 0)
            new_sums, out, dk, dv = [], dq[top:], jnp.zeros((tk, SB_LANES), F32), jnp.zeros((tk, SB_LANES), F32)
            for hd in range(hp):
                cp, ce = sums[hd]
                kv = jnp.where(lanes[hd], k2, 0.0).astype(MXU_DTYPE)
                vv = jnp.where(lanes[hd], v2, 0.0).astype(MXU_DTYPE)
                z = lax.dot_general(q2, kv, NT_DIMS, preferred_element_type=F32)
                sp = _softplus(z)
                lk = -sp if valid is None else jnp.where(valid, -sp, 0.0)
                w = jnp.exp(z - sp + (ls_ref[hd, part, :] - cp[top:]) - _ones_dot(lk, upto))
                if valid is not None:
                    w = jnp.where(valid, w, 0.0)
                e = w * lax.dot_general(do2, vv, NT_DIMS, preferred_element_type=F32)
                earlier = jnp.dot(e.astype(MXU_DTYPE), before, preferred_element_type=F32) + ce[top:]
                keep = jnp.exp(-sp)
                dz = e * keep - (1.0 - keep) * earlier
                if valid is not None:
                    dz = jnp.where(valid, dz, 0.0)
                dzm = dz.astype(MXU_DTYPE)
                out = out + jnp.dot(dzm, kv, preferred_element_type=F32)
                dk = dk + jnp.where(lanes[hd], lax.dot_general(dzm, q2, TN_DIMS, preferred_element_type=F32), 0.0)
                dv = dv + jnp.where(lanes[hd], lax.dot_general(w.astype(MXU_DTYPE), do2, TN_DIMS, preferred_element_type=F32), 0.0)
                new = (cp[top:] + jnp.sum(lk, axis=1, keepdims=True), ce[top:] + jnp.sum(e, axis=1, keepdims=True))
                new_sums.append(tuple(jnp.concatenate([old[:top], val], axis=0) for old, val in zip((cp, ce), new)) if top else new)
            dk_ref[pl.ds(off, tk), :] += dk
            dv_ref[pl.ds(off, tk), :] += dv
            return tuple(new_sums), (jnp.concatenate([dq[:top], out], axis=0) if top else out)

        zero = jnp.zeros((tq, 1), F32)
        carry = (tuple((zero, zero) for _ in range(hp)), jnp.zeros((tq, SB_LANES), F32))
        carry = lax.fori_loop(0, i * per, lambda j, cr: block(j, *cr, None), carry)
        for s in range(per):
            carry = block(i * per + s, *carry, s)
        dq_ref[...] = carry[1] * SB_SCALE

    whole = lambda col: pl.BlockSpec((seq, SB_LANES), lambda g, i: (0, col + g))
    tile = lambda col: pl.BlockSpec((tq, SB_LANES), lambda g, i: (i, col + g))
    acc = pl.BlockSpec((seq, SB_LANES), lambda g, i: (0, g))
    return _carried_call(
        body, name="sb_bwd", grid=(SB_HEADS // hp, seq // tq),
        in_specs=[tile(Q_COL), whole(K_COL), whole(V_COL), pl.BlockSpec((hp, tq, 1), lambda g, i: (g, i, 0)), tile(do_col)],
        out_specs=[tile(0), acc, acc],
        out_shape=[jax.ShapeDtypeStruct((seq, SB_HEADS * SB_DH), F32)] * 3,
        semantics=("parallel", "arbitrary"), operands=(proj, proj, proj, lsum, dmix), job=job)


def _s5_disc(lr, li, ldt, br, bi):
    dt = jnp.exp(ldt)
    mag = jnp.exp(lr * dt)
    ar = mag * jnp.cos(li * dt)
    ai = mag * jnp.sin(li * dt)
    den = lr * lr + li * li
    nr = ar - 1.0
    cr = (nr * lr + ai * li) / den
    ci = (ai * lr - nr * li) / den
    return ar, ai, cr[None] * br - ci[None] * bi, cr[None] * bi + ci[None] * br


def _s5_prep(lr, li, ldt, br, bi):
    shapes = [lr.shape, lr.shape, br.shape, br.shape]

    def body(lr_ref, li_ref, ldt_ref, br_ref, bi_ref, *outs):
        for o, val in zip(outs, _s5_disc(lr_ref[...], li_ref[...], ldt_ref[...], br_ref[...], bi_ref[...])):
            o[...] = val

    return pl.pallas_call(body, name="s5_prep", out_shape=[jax.ShapeDtypeStruct(s, F32) for s in shapes])(lr, li, ldt, br, bi)


def _s5_prep_bwd(lr, li, ldt, br, bi, cts):
    args = (lr, li, ldt, br, bi)

    def body(*refs):
        ins, ct_refs, outs = refs[:5], refs[5:9], refs[9:]
        _, vjp = jax.vjp(_s5_disc, *[r[...] for r in ins])
        for o, val in zip(outs, vjp(tuple(r[...] for r in ct_refs))):
            o[...] = val

    return pl.pallas_call(body, name="s5_prep_bwd", out_shape=[jax.ShapeDtypeStruct(a.shape, F32) for a in args])(*args, *cts)


SCAN_ROWS = 8


def _powers(ar, ai):
    out = [(ar, ai)]
    for _ in range(SCAN_ROWS - 1):
        pr, pi = out[-1]
        out.append((pr * ar - pi * ai, pr * ai + pi * ar))
    return out


def _s5_states(u, bmat, cmat, a, d, *, tc=512):
    seq, width = u.shape
    nj, cols, w2 = bmat.shape
    tw = w2 // 2
    tc = min(tc, seq)
    assert seq % tc == 0 and nj * cols == width and tw == S5_BLOCK

    def body(u_ref, bm_ref, cm_ref, a_ref, d_ref, h_ref, y_ref, cr_ref, ci_ref):
        @pl.when(pl.program_id(1) == 0)
        def _():
            cr_ref[...] = jnp.zeros_like(cr_ref)
            ci_ref[...] = jnp.zeros_like(ci_ref)

        uv = u_ref[...]
        h_ref[...] = jnp.dot(uv.astype(MXU_DTYPE), bm_ref[0], preferred_element_type=F32)
        re, im = pl.ds(0, tw), pl.ds(tw, tw)
        powers = _powers(a_ref[:, re], a_ref[:, im])
        pr = jnp.concatenate([p[0] for p in powers], axis=0)
        pi = jnp.concatenate([p[1] for p in powers], axis=0)
        row_id = lax.broadcasted_iota(jnp.int32, (SCAN_ROWS, tw), 0)
        reach = {dist: tuple(jnp.where(row_id >= dist, part, 0.0) for part in powers[dist - 1]) for dist in (1, 2, 4)}

        def block(n, carry):
            hr, hi = carry
            rows = pl.ds(pl.multiple_of(n * SCAN_ROWS, SCAN_ROWS), SCAN_ROWS)
            yr, yi = h_ref[rows, re], h_ref[rows, im]
            for dist in (1, 2, 4):
                cr, ci = reach[dist]
                sr, si = pltpu.roll(yr, dist, 0), pltpu.roll(yi, dist, 0)
                yr, yi = yr + cr * sr - ci * si, yi + cr * si + ci * sr
            yr, yi = yr + pr * hr - pi * hi, yi + pr * hi + pi * hr
            h_ref[rows, re] = yr
            h_ref[rows, im] = yi
            return yr[SCAN_ROWS - 1:], yi[SCAN_ROWS - 1:]

        hr, hi = lax.fori_loop(0, tc // SCAN_ROWS, block, (cr_ref[...], ci_ref[...]), unroll=4)
        cr_ref[...] = hr
        ci_ref[...] = hi
        y_ref[...] = jnp.dot(h_ref[...].astype(MXU_DTYPE), cm_ref[0], preferred_element_type=F32) + d_ref[...] * uv

    io = pl.BlockSpec((tc, cols), lambda j, t: (t, j))
    return pl.pallas_call(
        body, name="s5_states", grid=(nj, seq // tc),
        in_specs=[io, pl.BlockSpec((1, cols, w2), lambda j, t: (j, 0, 0)), pl.BlockSpec((1, w2, cols), lambda j, t: (j, 0, 0)),
                  pl.BlockSpec((1, w2), lambda j, t: (0, j)), pl.BlockSpec((1, cols), lambda j, t: (0, j))],
        out_specs=[pl.BlockSpec((tc, w2), lambda j, t: (t, j)), io],
        out_shape=[jax.ShapeDtypeStruct((seq, nj * w2), F32), jax.ShapeDtypeStruct((seq, width), F32)],
        scratch_shapes=[pltpu.VMEM((1, tw), F32)] * 2,
        compiler_params=_params("parallel", "arbitrary"),
    )(u, bmat, cmat, a, d)


def _s5_states_bwd(dy, h, u, bmat, cmat, a, du_skip, *, tc=512):
    seq, width = u.shape
    nj, cols, w2 = bmat.shape
    tw = w2 // 2
    tc = min(tc, seq)
    assert seq % tc == 0
    nt = seq // tc

    def body(dy_ref, h_ref, u_ref, bm_ref, cm_ref, a_ref, sk_ref, du_ref, da_ref, db_ref, dc_ref, g_ref, cr_ref, ci_ref):
        @pl.when(pl.program_id(1) == 0)
        def _():
            cr_ref[...] = jnp.zeros_like(cr_ref)
            ci_ref[...] = jnp.zeros_like(ci_ref)
            da_ref[...] = jnp.zeros_like(da_ref)
            db_ref[...] = jnp.zeros_like(db_ref)
            dc_ref[...] = jnp.zeros_like(dc_ref)

        dyv = dy_ref[...].astype(MXU_DTYPE)
        g_ref[...] = lax.dot_general(dyv, cm_ref[0], NT_DIMS, preferred_element_type=F32)
        re, im = pl.ds(0, tw), pl.ds(tw, tw)
        powers = _powers(a_ref[:, re], a_ref[:, im])
        pr = jnp.concatenate([p[0] for p in reversed(powers)], axis=0)
        pi = jnp.concatenate([p[1] for p in reversed(powers)], axis=0)
        row_id = lax.broadcasted_iota(jnp.int32, (SCAN_ROWS, tw), 0)
        last = SCAN_ROWS - 1
        reach = {dist: tuple(jnp.where(row_id < SCAN_ROWS - dist, part, 0.0) for part in powers[dist - 1]) for dist in (1, 2, 4)}

        def block(n, carry):
            gr, gi, sr, si = carry
            rows = pl.ds(pl.multiple_of((tc // SCAN_ROWS - 1 - n) * SCAN_ROWS, SCAN_ROWS), SCAN_ROWS)
            yr, yi = g_ref[rows, re], g_ref[rows, im]
            for dist in (1, 2, 4):
                cr, ci = reach[dist]
                ur, ui = pltpu.roll(yr, SCAN_ROWS - dist, 0), pltpu.roll(yi, SCAN_ROWS - dist, 0)
                yr, yi = yr + cr * ur + ci * ui, yi + cr * ui - ci * ur
            yr, yi = yr + pr * gr + pi * gi, yi + pr * gi - pi * gr
            g_ref[rows, re] = yr
            g_ref[rows, im] = yi
            nr = jnp.where(row_id < last, pltpu.roll(yr, last, 0), gr)
            ni = jnp.where(row_id < last, pltpu.roll(yi, last, 0), gi)
            hr, hi = h_ref[rows, re], h_ref[rows, im]
            return yr[:1], yi[:1], sr + nr * hr + ni * hi, si + ni * hr - nr * hi

        zero = jnp.zeros((SCAN_ROWS, tw), F32)
        gr, gi, sr, si = lax.fori_loop(0, tc // SCAN_ROWS, block, (cr_ref[...], ci_ref[...], zero, zero), unroll=4)
        cr_ref[...] = gr
        ci_ref[...] = gi
        da_ref[:, re] += jnp.sum(sr, axis=0, keepdims=True)
        da_ref[:, im] += jnp.sum(si, axis=0, keepdims=True)
        gv = g_ref[...].astype(MXU_DTYPE)
        du_ref[...] = (lax.dot_general(gv, bm_ref[0], NT_DIMS, preferred_element_type=F32) + sk_ref[...]).astype(du_ref.dtype)
        db_ref[0] += lax.dot_general(u_ref[...].astype(MXU_DTYPE), gv, TN_DIMS, preferred_element_type=F32)
        dc_ref[0] += lax.dot_general(h_ref[...].astype(MXU_DTYPE), dyv, TN_DIMS, preferred_element_type=F32)

    io = pl.BlockSpec((tc, cols), lambda j, t: (nt - 1 - t, j))
    bm = pl.BlockSpec((1, cols, w2), lambda j, t: (j, 0, 0))
    cm = pl.BlockSpec((1, w2, cols), lambda j, t: (j, 0, 0))
    row = pl.BlockSpec((1, w2), lambda j, t: (0, j))
    return pl.pallas_call(
        body, name="s5_states_bwd", grid=(nj, nt),
        in_specs=[io, pl.BlockSpec((tc, w2), lambda j, t: (nt - 1 - t, j)), io, bm, cm, row, io],
        out_specs=[io, row, bm, cm],
        out_shape=[jax.ShapeDtypeStruct((seq, width), MXU_DTYPE), jax.ShapeDtypeStruct((1, nj * w2), F32),
                   jax.ShapeDtypeStruct(bmat.shape, F32), jax.ShapeDtypeStruct(cmat.shape, F32)],
        scratch_shapes=[pltpu.VMEM((tc, w2), F32)] + [pltpu.VMEM((1, tw), F32)] * 2,
        compiler_params=_params("parallel", "arbitrary"),
    )(dy, h, u, bmat, cmat, a, du_skip)


def _pair_columns(re, im, axis):
    shape = re.shape
    split = shape[:axis] + (shape[axis] // S5_BLOCK, S5_BLOCK) + shape[axis + 1:]
    both = jnp.stack([re.reshape(split), im.reshape(split)], axis=axis + 1)
    return both.reshape(shape[:axis] + (2 * shape[axis],) + shape[axis + 1:])


def _unpair_columns(t, axis):
    shape = t.shape
    both = t.reshape(shape[:axis] + (shape[axis] // (2 * S5_BLOCK), 2, S5_BLOCK) + shape[axis + 1:])
    half = shape[:axis] + (shape[axis] // 2,) + shape[axis + 1:]
    return (lax.index_in_dim(both, 0, axis + 1, keepdims=False).reshape(half),
            lax.index_in_dim(both, 1, axis + 1, keepdims=False).reshape(half))


S5_PER_BLOCK = S5_GROUPS // S5_DIAG


def _block_diag(t):
    g, a, b = t.shape
    n = S5_PER_BLOCK
    eye = jnp.eye(n, dtype=t.dtype)
    return (t.reshape(g // n, n, a, 1, b) * eye[None, :, None, :, None]).reshape(g // n, n * a, n * b)


def _block_diag_part(m):
    j, n = m.shape[0], S5_PER_BLOCK
    a, b = m.shape[1] // n, m.shape[2] // n
    return jnp.moveaxis(jnp.diagonal(m.reshape(j, n, a, n, b), axis1=1, axis2=3), -1, 1).reshape(j * n, a, b)


def _gelu_glu(y, gate_pre):
    z = jax.nn.gelu(y)
    return z * jax.nn.sigmoid(gate_pre)


def _s5_fwd(u, p, w_glu):
    lr, li = p["s5_lambda_re"][0], p["s5_lambda_im"][0]
    ldt = p["s5_log_dt"][0][:, None]
    br = p["s5_b_re"][0].transpose(2, 0, 1)
    bi = p["s5_b_im"][0].transpose(2, 0, 1)
    ar, ai, bbr, bbi = _s5_prep(lr, li, ldt, br, bi)
    a = _pair_columns(ar.reshape(1, S5_LANES), ai.reshape(1, S5_LANES), 1)
    bmat = jnp.concatenate([_block_diag(bbr.transpose(1, 0, 2)), _block_diag(bbi.transpose(1, 0, 2))], axis=2)
    cmat = jnp.concatenate([_block_diag(p["s5_c_re"][0].transpose(0, 2, 1)),
                            -_block_diag(p["s5_c_im"][0].transpose(0, 2, 1))], axis=1)
    bmat, cmat = bmat.astype(MXU_DTYPE), cmat.astype(MXU_DTYPE)
    d = p["s5_d"]
    h, y = _s5_states(u, bmat, cmat, a, d)
    z = _rowmap(jax.nn.gelu, [y], "r", [(y.shape, MXU_DTYPE, "r")], name="s5_gelu", tl=512)
    gate_pre = _mm(z, w_glu, name="s5_glu")
    out = _rowmap(_gelu_glu, [y, gate_pre], "rr", [(y.shape, F32, "r")], name="s5_gate", tl=512)
    return out, (u, lr, li, ldt, br, bi, a, bmat, cmat, h, y, z, gate_pre)


def _s5_bwd(dout, saved, p, w_glu):
    u, lr, li, ldt, br, bi, a, bmat, cmat, h, y, z, gate_pre = saved
    d = p["s5_d"]

    def gate_bwd(dov, yv, gv):
        zv = jax.nn.gelu(yv)
        sg = jax.nn.sigmoid(gv)
        return dov * sg, dov * zv * sg * (1.0 - sg)

    dz_direct, dgate = _rowmap(gate_bwd, [dout, y, gate_pre], "rrr", [(y.shape, F32, "r"), (y.shape, MXU_DTYPE, "r")],
                               name="s5_gate_bwd", tl=512)
    dw_glu = _mm(z, dgate, ta=True, name="s5_dwglu", out_dtype=WIRE_DTYPE)
    dz = _mm(dgate, w_glu, tb=True, name="s5_dz", epilogue=lambda acc, prev: acc + prev, extras=[dz_direct])

    def gelu_bwd(dzv, yv, uv, dvv):
        _, vjp = jax.vjp(jax.nn.gelu, yv)
        dy = vjp(dzv)[0]
        return dy, dy * dvv, jnp.sum(dy * uv, axis=0, keepdims=True)

    dy, du_skip, dd = _rowmap(gelu_bwd, [dz, y, u, d], "rrrc",
                              [(y.shape, F32, "r"), (y.shape, F32, "r"), (d.shape, F32, "a")], name="s5_gelu_bwd", tl=512)
    du, da, dbmat, dcmat = _s5_states_bwd(dy, h, u, bmat, cmat, a, du_skip)
    dbbr, dbbi = (_block_diag_part(t).transpose(1, 0, 2) for t in (dbmat[:, :, :S5_BLOCK], dbmat[:, :, S5_BLOCK:]))
    dar, dai = _unpair_columns(da, 1)
    cts = (dar.reshape(S5_GROUPS, S5_STATE), dai.reshape(S5_GROUPS, S5_STATE), dbbr, dbbi)
    dlr, dli, dldt, dbr, dbi = _s5_prep_bwd(lr, li, ldt, br, bi, cts)
    dcr, dci = (_block_diag_part(t).transpose(0, 2, 1) for t in (dcmat[:, :S5_BLOCK], dcmat[:, S5_BLOCK:]))
    grads = {
        "s5_lambda_re": dlr[None], "s5_lambda_im": dli[None], "s5_log_dt": dldt[:, 0][None],
        "s5_b_re": dbr.transpose(1, 2, 0)[None], "s5_b_im": dbi.transpose(1, 2, 0)[None],
        "s5_c_re": dcr[None], "s5_c_im": -dci[None], "s5_d": dd,
    }
    return du, dw_glu, grads


def _mix0_fwd(x, g, p, full, late):
    (h, proj), _ = _norm_proj(x, g, full[("ab_w_in", 0)], name="mix0_in")
    u = proj[:, :S5_WIDTH]
    job, keys = late.gather_job("sb_fwd") if late else (None, [])
    (o, lsum), got = _attend(proj, job=job)
    full.update(zip(keys, got))
    w_glu, w_out = full[("s5_w_glu", 0)], full[("ab_w_out", 0)]
    y_a, s5_saved = _s5_fwd(u, p, w_glu)
    mix = jnp.concatenate([y_a, o], axis=1).astype(MXU_DTYPE)
    x2 = _mm(mix, w_out, name="mix0_out", epilogue=lambda acc, xv: xv + acc, extras=[x])
    return x2, (x, h, proj, lsum, mix, s5_saved)


def _mix0_bwd(dx2, saved, g, p, full, grads, late):
    x, h, proj, lsum, mix, s5_saved = saved
    w_in, w_glu, w_out = full[("ab_w_in", 0)], full[("s5_w_glu", 0)], full[("ab_w_out", 0)]
    dmix = _mm(dx2, w_out, tb=True, name="mix0_dmix")
    grads[("ab_w_out", 0)] = _mm(mix, dx2, ta=True, name="mix0_dwout", out_dtype=WIRE_DTYPE)
    du, grads[("s5_w_glu", 0)], s5_grads = _s5_bwd(dmix[:, :S5_WIDTH], s5_saved, p, w_glu)
    job, keys = late.scatter_job(grads) if late else (None, [])
    (dq, dk, dv), got = _attend_bwd(proj, lsum, dmix, job=job)
    _note(late, keys, got)
    dproj = jnp.concatenate([du] + [t.astype(MXU_DTYPE) for t in (dq, dk, dv)], axis=1)
    grads[("ab_w_in", 0)] = _mm(h, dproj, ta=True, name="mix0_dwin", out_dtype=WIRE_DTYPE)
    job, keys = late.scatter_job(grads) if late else (None, [])
    (dx, dg), got = _proj_norm_bwd([(dproj, w_in, "DF")], x, g, dx2, name="mix0_dh", job=job)
    _note(late, keys, got)
    return dx, dg, s5_grads


def _shift_down(t, n):
    rows = lax.broadcasted_iota(jnp.int32, t.shape, 0)
    return jnp.where(rows >= n, pltpu.roll(t, n, 0), 0.0)


def _shift_up(t, n):
    rows = lax.broadcasted_iota(jnp.int32, t.shape, 0)
    return jnp.where(rows < t.shape[0] - n, pltpu.roll(t, t.shape[0] - n, 0), 0.0)


def _conv_fwd(proj, cw, *, tc=128):
    seq, c3 = proj.shape
    ch = c3 // 3
    nb = ch // tc

    def body(b_ref, c_ref, v_ref, w_ref, m_ref):
        pv = c_ref[...] * v_ref[...]
        w = w_ref[...]
        y = w[2:3] * pv + w[1:2] * _shift_down(pv, 1) + w[0:1] * _shift_down(pv, 2)
        m_ref[...] = (b_ref[...] * y).astype(m_ref.dtype)

    col = lambda part: pl.BlockSpec((seq, tc), lambda j: (0, part * nb + j))
    return pl.pallas_call(
        body, name="conv_fwd", grid=(nb,),
        in_specs=[col(0), col(1), col(2), pl.BlockSpec((3, tc), lambda j: (0, j))],
        out_specs=pl.BlockSpec((seq, tc), lambda j: (0, j)),
        out_shape=jax.ShapeDtypeStruct((seq, ch), MXU_DTYPE),
        compiler_params=_params("parallel"),
    )(proj, proj, proj, cw)


def _conv_bwd(proj, cw, dm, *, tc=128):
    seq, c3 = proj.shape
    ch = c3 // 3
    nb = ch // tc

    def body(b_ref, c_ref, v_ref, w_ref, dm_ref, dproj_ref, dw_ref, dc_ref, dv_ref):
        part = pl.program_id(1)

        @pl.when(part == 0)
        def _():
            cv, vv, dmv = c_ref[...], v_ref[...], dm_ref[...]
            pv = cv * vv
            w = w_ref[...]
            p1, p2 = _shift_down(pv, 1), _shift_down(pv, 2)
            y = w[2:3] * pv + w[1:2] * p1 + w[0:1] * p2
            dproj_ref[...] = (dmv * y).astype(dproj_ref.dtype)
            dy = dmv * b_ref[...]
            dp = w[2:3] * dy + w[1:2] * _shift_up(dy, 1) + w[0:1] * _shift_up(dy, 2)
            dc_ref[...] = (dp * vv).astype(dc_ref.dtype)
            dv_ref[...] = (dp * cv).astype(dv_ref.dtype)
            dw_ref[...] = jnp.concatenate([jnp.sum(dy * p2, axis=0, keepdims=True), jnp.sum(dy * p1, axis=0, keepdims=True),
                                           jnp.sum(dy * pv, axis=0, keepdims=True)], axis=0)

        @pl.when(part == 1)
        def _():
            dproj_ref[...] = dc_ref[...]

        @pl.when(part == 2)
        def _():
            dproj_ref[...] = dv_ref[...]

    col = lambda part: pl.BlockSpec((seq, tc), lambda j, t: (0, part * nb + j))
    small = pl.BlockSpec((3, tc), lambda j, t: (0, j))
    return pl.pallas_call(
        body, name="conv_bwd", grid=(nb, 3),
        in_specs=[col(0), col(1), col(2), small, pl.BlockSpec((seq, tc), lambda j, t: (0, j))],
        out_specs=[pl.BlockSpec((seq, tc), lambda j, t: (0, t * nb + j)), small],
        out_shape=[jax.ShapeDtypeStruct((seq, c3), MXU_DTYPE), jax.ShapeDtypeStruct((3, ch), F32)],
        scratch_shapes=[pltpu.VMEM((seq, tc), MXU_DTYPE)] * 2,
        compiler_params=_params("parallel", "arbitrary"),
    )(proj, proj, proj, cw, dm)


def _mix1_fwd(x, g, full, late):
    job, keys = late.gather_job("mix1_in") if late else (None, [])
    (h, proj), got = _norm_proj(x, g, full[("sc_w_in", 0)], name="mix1_in", job=job)
    full.update(zip(keys, got))
    m = _conv_fwd(proj, full[("sc_conv_w", 0)])
    x2 = _mm(m, full[("sc_w_out", 0)], name="mix1_out", epilogue=lambda acc, xv: xv + acc, extras=[x])
    return x2, (x, h, proj, m)


def _mix1_bwd(dx2, saved, g, full, grads, late):
    x, h, proj, m = saved
    w_in, cw, w_out = full[("sc_w_in", 0)], full[("sc_conv_w", 0)], full[("sc_w_out", 0)]
    dm = _mm(dx2, w_out, tb=True, name="mix1_dm")
    grads[("sc_w_out", 0)] = _mm(m, dx2, ta=True, name="mix1_dwout", out_dtype=WIRE_DTYPE)
    dproj, dcw = _conv_bwd(proj, cw, dm)
    grads[("sc_conv_w", 0)] = dcw.astype(WIRE_DTYPE)
    grads[("sc_w_in", 0)] = _mm(h, dproj, ta=True, name="mix1_dwin", out_dtype=WIRE_DTYPE)
    job, keys = late.scatter_job(grads, [("ffn2_w_up", 1)]) if late else (None, [])
    (dx, dg), got = _proj_norm_bwd([(dproj, w_in, "DF")], x, g, dx2, name="mix1_dh", job=job)
    _note(late, keys, got)
    return dx, dg


def _loss_head(x, g, target):
    feat = x.shape[1]

    def fn(xv, gv, tv):
        err = _rms_fwd(xv, gv) - tv
        dx, dg = _rms_bwd(err / feat, xv, gv)
        return jnp.sum(err * err, keepdims=True) * (0.5 / feat), dx, dg

    return _rowmap(fn, [x, g, target], "rcr", [((1, 1), F32, "a"), (x.shape, F32, "r"), (g.shape, F32, "a")],
                   name="loss_head", tl=256)


def _slot(ref, place, chip=None, half=None, piece=(0, 1)):
    axis, width = place
    shape = list(ref.shape)
    start = [0, 0]
    if chip is not None:
        start[axis], shape[axis] = chip * width, width
    if half is not None:
        h_axis = 0 if shape[0] % 32 == 0 else 1
        shape[h_axis] //= 2 * piece[1]
        start[h_axis] = start[h_axis] + (half * piece[1] + piece[0]) * shape[h_axis]
    hint = lambda s, d: s if isinstance(s, int) else pl.multiple_of(s, 128 if d == 1 else 8)
    return ref.at[tuple(pl.ds(hint(s, d), n) for d, (s, n) in enumerate(zip(start, shape)))]


class _Exchange:
    def __init__(self, kind, arrays, places, pieces=1):
        self.kind, self.arrays, self.places, self.n, self.pieces = kind, list(arrays), list(places), len(arrays), pieces
        self.out_shape = []
        for t, (axis, width) in zip(self.arrays, self.places):
            if kind == "gather":
                shape = list(t.shape)
                shape[axis] = N_CHIPS * width
            else:
                shape = [N_CHIPS] + list(t.shape)
                shape[1 + axis] = width
            self.out_shape.append(jax.ShapeDtypeStruct(tuple(shape), t.dtype))
        n = self.n
        self.scratch = [pltpu.SemaphoreType.DMA((3 * n * pieces,)) for _ in range(4 if kind == "gather" else 2)]
        self.scratch.append(pltpu.SemaphoreType.DMA((n,)))

    def _copies(self, ins, outs, sems):
        x, y, c = lax.axis_index("x"), lax.axis_index("y"), lax.axis_index("c")
        peers = [(1 - x, y), (x, 1 - y), (1 - x, 1 - y)]
        remote = lambda src, dst, send, recv, k, to: pltpu.make_async_remote_copy(
            src_ref=src, dst_ref=dst, send_sem=send.at[k], recv_sem=recv.at[k], device_id=to, device_id_type=MESH_ID)
        local, ici, d2d = [], [], []
        for a in range(self.n):
            place = self.places[a]
            if self.kind == "gather":
                local.append(pltpu.make_async_copy(ins[a], _slot(outs[a], place, 2 * x + y), sems[4].at[a]))
                for q in range(self.pieces):
                    for r, (px, py) in enumerate(peers):
                        k, part = (3 * a + r) * self.pieces + q, (q, self.pieces)
                        ici.append(remote(_slot(ins[a], place, None, c, part), _slot(outs[a], place, 2 * x + y, c, part),
                                          sems[0], sems[1], k, (px, py, c)))
                        landed = _slot(outs[a], place, 2 * px + py, c, part)
                        d2d.append(remote(landed, landed, sems[2], sems[3], k, (x, y, 1 - c)))
            else:
                local.append(pltpu.make_async_copy(_slot(ins[a], place, 2 * x + y), outs[a].at[3], sems[2].at[a]))
                for r, (px, py) in enumerate(peers):
                    ici.append(remote(_slot(ins[a], place, 2 * px + py), outs[a].at[r], sems[0], sems[1], 3 * a + r, (px, py, c)))
        return local, ici, d2d

    def start(self, ins, outs, sems):
        local, ici, _ = self._copies(ins, outs, sems)
        for cp in local + ici:
            cp.start()

    def relay(self, ins, outs, sems):
        _, ici, d2d = self._copies(ins, outs, sems)
        for arrived, onward in zip(ici, d2d):
            arrived.wait_recv()
            onward.start()

    def finish(self, ins, outs, sems):
        local, ici, d2d = self._copies(ins, outs, sems)
        for cp in local + d2d:
            cp.wait()
        for cp in ici:
            cp.wait_send() if d2d else cp.wait()


def _exchange_call(job, name):
    n = job.n

    def body(*refs):
        ins, outs, sems = refs[:n], refs[n:2 * n], refs[2 * n:]
        job.start(ins, outs, sems)
        job.relay(ins, outs, sems)
        job.finish(ins, outs, sems)

    return pl.pallas_call(
        body, name=name, in_specs=[ANY_SPEC] * n, out_specs=[ANY_SPEC] * n, out_shape=job.out_shape,
        scratch_shapes=job.scratch, compiler_params=pltpu.CompilerParams(has_side_effects=True),
    )(*job.arrays)


def _carried_call(body, *, name, grid, in_specs, out_specs, out_shape, semantics, operands, scratch_shapes=(), job=None):
    scratch_shapes = list(scratch_shapes)
    if job is None:
        return pl.pallas_call(body, name=name, grid=grid, in_specs=in_specs, out_specs=out_specs, out_shape=out_shape,
                              scratch_shapes=scratch_shapes, compiler_params=_params(*semantics))(*operands), []
    n_in, n_out, n, n_scr = len(in_specs), len(out_specs), job.n, len(scratch_shapes)
    steps = math.prod(grid)

    def wrapped(*refs):
        ins, job_ins = refs[:n_in], refs[n_in:n_in + n]
        outs, job_outs = refs[n_in + n:n_in + n + n_out], refs[n_in + n + n_out:n_in + 2 * n + n_out]
        outs = outs + refs[n_in + 2 * n + n_out:n_in + 2 * n + n_out + n_scr]
        sems = refs[n_in + 2 * n + n_out + n_scr:]
        step = functools.reduce(lambda acc, d: acc * grid[d] + pl.program_id(d), range(len(grid)), 0)

        @pl.when(step == 0)
        def _():
            job.start(job_ins, job_outs, sems)

        @pl.when(step == (3 * steps) // 4)
        def _():
            job.relay(job_ins, job_outs, sems)

        body(*ins, *outs)

        @pl.when(step == steps - 1)
        def _():
            job.finish(job_ins, job_outs, sems)

    res = pl.pallas_call(
        wrapped, name=name, grid=grid, in_specs=list(in_specs) + [ANY_SPEC] * n, out_specs=list(out_specs) + [ANY_SPEC] * n,
        out_shape=list(out_shape) + job.out_shape, scratch_shapes=scratch_shapes + job.scratch,
        compiler_params=pltpu.CompilerParams(dimension_semantics=("arbitrary",) * len(grid), vmem_limit_bytes=VMEM_LIMIT,
                                             has_side_effects=True),
    )(*operands, *job.arrays)
    return res[:n_out], res[n_out:]


def _swap_and_spread(parts, t):
    n = len(parts)

    def body(*refs):
        ins, t_ref, outs, slots = refs[:n], refs[n], refs[n + 1:2 * n + 1], refs[2 * n + 1]
        send, recv, all_send, all_recv, own = refs[2 * n + 2:]
        x, y, c = lax.axis_index("x"), lax.axis_index("y"), lax.axis_index("c")
        mine = slots.at[4 * x + 2 * y + c]
        copies = [pltpu.make_async_copy(t_ref, mine, own)]
        copies += [pltpu.make_async_remote_copy(src_ref=ins[a], dst_ref=outs[a], send_sem=send.at[a], recv_sem=recv.at[a],
                                                device_id=(x, y, 1 - c), device_id_type=MESH_ID) for a in range(n)]
        for m in range(1, 8):
            peer = (x ^ (m >> 2), y ^ ((m >> 1) & 1), c ^ (m & 1))
            copies.append(pltpu.make_async_remote_copy(src_ref=t_ref, dst_ref=mine, send_sem=all_send.at[m - 1],
                                                       recv_sem=all_recv.at[m - 1], device_id=peer, device_id_type=MESH_ID))
        for cp in copies:
            cp.start()
        for cp in copies:
            cp.wait()

    res = pl.pallas_call(
        body, name="swap_and_spread",
        in_specs=[ANY_SPEC] * (n + 1), out_specs=[ANY_SPEC] * (n + 1),
        out_shape=[jax.ShapeDtypeStruct(p.shape, p.dtype) for p in parts] + [jax.ShapeDtypeStruct((8,) + t.shape, t.dtype)],
        scratch_shapes=[pltpu.SemaphoreType.DMA((n,)), pltpu.SemaphoreType.DMA((n,)), pltpu.SemaphoreType.DMA((7,)),
                        pltpu.SemaphoreType.DMA((7,)), pltpu.SemaphoreType.DMA(())],
        compiler_params=pltpu.CompilerParams(has_side_effects=True),
    )(*parts, t)
    return res[:n], res[n]


def _adamw(w, g, m, v):
    m = ADAM_B1 * m + (1.0 - ADAM_B1) * g
    v = ADAM_B2 * v + (1.0 - ADAM_B2) * jnp.square(g)
    m_hat = m / (1.0 - ADAM_B1 ** ADAM_STEP)
    v_hat = v / (1.0 - ADAM_B2 ** ADAM_STEP)
    return -ADAM_LR * (m_hat / (jnp.sqrt(v_hat) + ADAM_EPS) + ADAM_WD * w), m, v


def _chip_sums(group, name):
    rows, cols = group[0].shape[1:]
    count = len(group)
    tl = _row_block(rows, 512 if count == 1 else 128, tile=32 // group[0].dtype.itemsize)

    def body(*refs):
        for r_ref, o_ref in zip(refs[:count], refs[count:]):
            total = ((r_ref[0].astype(F32) + r_ref[1].astype(F32)) + r_ref[2].astype(F32)) + r_ref[3].astype(F32)
            o_ref[...] = total.astype(o_ref.dtype)

    return pl.pallas_call(body, name=name, grid=(rows // tl,),
                          in_specs=[pl.BlockSpec((N_CHIPS, tl, cols), lambda i: (0, i, 0))] * count,
                          out_specs=[pl.BlockSpec((tl, cols), lambda i: (i, 0))] * count,
                          out_shape=[jax.ShapeDtypeStruct((rows, cols), group[0].dtype)] * count,
                          compiler_params=_params("parallel"))(*group)


def _adamw_layer(w, m, v, p_mine, p_other, layer, prev, name):
    _, rows, cols = w.shape
    assert p_mine.shape[1] == cols and p_mine.shape[0] >= rows
    tl = _row_block(rows, 512, tile=32 // p_mine.dtype.itemsize)
    tc = cols
    if tl < 128 < rows and cols % 256 == 0:
        tl, tc = rows, 256

    def body(w_ref, m_ref, v_ref, pa_ref, pb_ref, *rest):
        g = pa_ref[...].astype(F32) + pb_ref[...].astype(F32)
        for o_ref, val in zip(rest[-4:], (g,) + _adamw(w_ref[...], g, m_ref[...], v_ref[...])):
            o_ref[...] = val

    stacked = pl.BlockSpec((None, tl, tc), lambda i, j: (layer, i, j))
    part = pl.BlockSpec((tl, tc), lambda i, j: (i, j))
    kept = list(prev) if prev else []
    return pl.pallas_call(
        body, name=name, grid=(rows // tl, cols // tc),
        in_specs=[stacked] * 3 + [part] * 2 + [ANY_SPEC] * len(kept),
        out_specs=[stacked] * 4, out_shape=[jax.ShapeDtypeStruct(w.shape, F32)] * 4,
        input_output_aliases={5 + k: k for k in range(len(kept))},
        compiler_params=_params("parallel", "parallel"),
    )(w, m, v, p_mine, p_other, *kept)


def _adamw_small(w, slots, m, v):
    def fn(wv, sv, mv, vv):
        g = sv[0]
        for dev in range(1, 8):
            g = g + sv[dev]
        return (g,) + _adamw(wv, g, mv, vv)

    return _rowmap(fn, [w, slots, m, v], "rcrr", [(w.shape, F32, "r")] * 4, name="adamw_small", tl=w.shape[0])


WEIGHTS = ['ffn1_norm', 'ffn1_w_gate', 'ffn1_w_up', 'ffn1_w_down', 'mix_norm', 'ffn2_norm', 'ffn2_w_gate', 'ffn2_w_up',
           'ffn2_w_down', 'ab_w_in', 's5_lambda_re', 's5_lambda_im', 's5_log_dt', 's5_b_re', 's5_b_im', 's5_c_re', 's5_c_im',
           's5_d', 's5_w_glu', 'ab_w_out', 'sc_w_in', 'sc_conv_w', 'sc_w_out', 'final_norm']
SHARDED = {'ffn1_w_gate': (0, FF_SLOT), 'ffn1_w_up': (0, FF_SLOT), 'ffn1_w_down': (0, FF_SLOT),
           'ffn2_w_gate': (0, FF_SLOT), 'ffn2_w_up': (0, FF_SLOT), 'ffn2_w_down': (0, FF_SLOT),
           'ab_w_in': (1, 512), 's5_w_glu': (0, 128), 'ab_w_out': (0, 256), 'sc_w_in': (1, 768), 'sc_conv_w': (1, 256),
           'sc_w_out': (0, 256)}
SWAPPED = ('ffn1_w_gate', 'ffn1_w_up', 'ffn2_w_gate', 'ffn2_w_up')
SMALL = [n for n in WEIGHTS if n not in SHARDED]


def _held(name, t):
    return jnp.swapaxes(t, 1, 2) if name in SWAPPED else t


def _pack(arrays):
    rows = []
    for t in arrays:
        flat = t.reshape(-1)
        rows.append(jnp.pad(flat, (0, (-flat.shape[0]) % 128)))
    flat = jnp.concatenate(rows)
    return jnp.pad(flat, (0, (-flat.shape[0]) % 1024)).reshape(-1, 128)


def _unpack(packed, like):
    flat, out, pos = packed.reshape(-1), [], 0
    for t in like:
        out.append(flat[pos:pos + t.size].reshape(t.shape))
        pos += t.size + (-t.size) % 128
    return out


def _local_grads(x, target, p, full, late=None):
    small, grads, saved = {}, {}, []
    for layer in range(2):
        x, s1 = _ffn_fwd(x, p["ffn1_norm"][layer:layer + 1], full, "ffn1", layer, late)
        if layer == 0:
            x, sm = _mix0_fwd(x, p["mix_norm"][0:1], p, full, late)
        else:
            x, sm = _mix1_fwd(x, p["mix_norm"][1:2], full, late)
        x, s2 = _ffn_fwd(x, p["ffn2_norm"][layer:layer + 1], full, "ffn2", layer, late)
        saved.append((s1, sm, s2))
    loss, dx, dg_final = _loss_head(x, p["final_norm"][None], target)
    small["final_norm"] = dg_final[0]
    gains = {n: [None, None] for n in ("ffn1_norm", "mix_norm", "ffn2_norm")}

    def ffn_bwd(which, layer, dx, s):
        dx, dg = _ffn_bwd(dx, s, p[f"{which}_norm"][layer:layer + 1], full, which, layer, grads, late,
                          inline={("ffn2", 1): "defer", ("ffn1", 0): "chain"}.get((which, layer)))
        gains[f"{which}_norm"][layer] = dg[0]
        return dx

    for layer in (1, 0):
        s1, sm, s2 = saved[layer]
        dx = ffn_bwd("ffn2", layer, dx, s2)
        if layer == 0:
            dx, dg, s5_grads = _mix0_bwd(dx, sm, p["mix_norm"][0:1], p, full, grads, late)
            small.update(s5_grads)
        else:
            dx, dg = _mix1_bwd(dx, sm, p["mix_norm"][1:2], full, grads, late)
        gains["mix_norm"][layer] = dg[0]
        dx = ffn_bwd("ffn1", layer, dx, s1)
    small.update({n: jnp.stack(pair) for n, pair in gains.items()})
    return loss, dx, small, grads


_GATHER_PLAN = {
    "gather_early": [("ffn1_w_gate", 0), ("ffn1_w_up", 0)],
    "ffn1_0_up": [("ffn1_w_down", 0), ("ab_w_in", 0)],
    "sb_fwd": [("s5_w_glu", 0), ("ab_w_out", 0), ("ffn2_w_gate", 0), ("ffn2_w_up", 0), ("ffn2_w_down", 0),
               ("ffn1_w_gate", 1), ("ffn1_w_up", 1), ("ffn1_w_down", 1)],
    "ffn2_0_up": [("sc_w_in", 0), ("sc_conv_w", 0), ("sc_w_out", 0)],
    "ffn2_0_down": [("ffn2_w_gate", 1)],
    "ffn1_1_up": [("ffn2_w_up", 1)],
    "ffn2_1_up": [("ffn2_w_down", 1)],
}


class _Late:
    def __init__(self, shards, places):
        self.shards, self.places = shards, places
        self.sent, self.received = set(), {}

    def gather_job(self, carrier):
        keys = _GATHER_PLAN.get(carrier, [])
        if not keys:
            return None, []
        pieces = 4 if carrier == "gather_early" else 1
        return _Exchange("gather", [self.shards[k] for k in keys], [self.places[k] for k in keys], pieces), keys

    def scatter_job(self, grads, only=None):
        keys = [k for k in grads if k not in self.sent and (only is None or k in only)]
        if not keys:
            return None, []
        self.sent.update(keys)
        return _Exchange("scatter", [grads[k] for k in keys], [self.places[k] for k in keys]), keys


def kernel(x, ffn1_norm, ffn1_w_gate, ffn1_w_up, ffn1_w_down, mix_norm, ffn2_norm, ffn2_w_gate, ffn2_w_up, ffn2_w_down, ab_w_in, s5_lambda_re, s5_lambda_im, s5_log_dt, s5_b_re, s5_b_im, s5_c_re, s5_c_im, s5_d, s5_w_glu, ab_w_out, sc_w_in, sc_conv_w, sc_w_out, final_norm, loss_target, m_ffn1_norm, m_ffn1_w_gate, m_ffn1_w_up, m_ffn1_w_down, m_mix_norm, m_ffn2_norm, m_ffn2_w_gate, m_ffn2_w_up, m_ffn2_w_down, m_ab_w_in, m_s5_lambda_re, m_s5_lambda_im, m_s5_log_dt, m_s5_b_re, m_s5_b_im, m_s5_c_re, m_s5_c_im, m_s5_d, m_s5_w_glu, m_ab_w_out, m_sc_w_in, m_sc_conv_w, m_sc_w_out, m_final_norm, v_ffn1_norm, v_ffn1_w_gate, v_ffn1_w_up, v_ffn1_w_down, v_mix_norm, v_ffn2_norm, v_ffn2_w_gate, v_ffn2_w_up, v_ffn2_w_down, v_ab_w_in, v_s5_lambda_re, v_s5_lambda_im, v_s5_log_dt, v_s5_b_re, v_s5_b_im, v_s5_c_re, v_s5_c_im, v_s5_d, v_s5_w_glu, v_ab_w_out, v_sc_w_in, v_sc_conv_w, v_sc_w_out, v_final_norm):
    args = dict(locals())
    p = {n: _held(n, args[n]) for n in WEIGHTS}
    mom = {n: _held(n, args["m_" + n]) for n in WEIGHTS}
    var = {n: _held(n, args["v_" + n]) for n in WEIGHTS}

    keys = [(n, layer) for n in SHARDED for layer in range(p[n].shape[0])]
    shards, places = {}, {}
    for n, layer in keys:
        axis, width = SHARDED[n]
        t = p[n][layer] if n == "sc_conv_w" else p[n][layer].astype(MXU_DTYPE)
        pad = [(0, 0), (0, 0)]
        pad[axis] = (0, width - t.shape[axis])
        shards[(n, layer)], places[(n, layer)] = jnp.pad(t, pad), (axis, width)
    late = _Late(shards, places)
    job, first = late.gather_job("gather_early")
    full = dict(zip(first, _exchange_call(job, "gather_early")))

    loss, dx, small, grads = _local_grads(x[0], loss_target[0], p, full, late)
    loss = lax.psum(loss[0, 0], ("x", "y", "c"))
    assert set(late.received) == set(keys), "a gradient was left without a carrier"

    alike = {}
    for key in keys:
        alike.setdefault((late.received[key].shape, late.received[key].dtype), []).append(key)
    summed = {}
    for group in alike.values():
        summed.update(zip(group, _chip_sums([late.received[k] for k in group], name=f"chip_sum_{group[0][0]}_x{len(group)}")))
    partial = [summed[key] for key in keys]
    other, small_slots = _swap_and_spread(partial, _pack([small[n] for n in SMALL]))
    out = {}
    for (n, layer), mine, theirs in zip(keys, partial, other):
        out[n] = _adamw_layer(p[n], mom[n], var[n], mine, theirs, layer, out.get(n), name=f"adamw_{n}_{layer}")
    out = {n: [_held(n, t) for t in res] for n, res in out.items()}

    like = [p[n] for n in SMALL]
    results = _adamw_small(_pack(like), small_slots, _pack([mom[n] for n in SMALL]), _pack([var[n] for n in SMALL]))
    for k, packed in enumerate(results):
        for n, t in zip(SMALL, _unpack(packed, like)):
            out.setdefault(n, [None] * 4)[k] = t

    return (loss, dx[None], *[out[n][0] for n in WEIGHTS], *[out[n][1] for n in WEIGHTS],
            *[out[n][2] for n in WEIGHTS], *[out[n][3] for n in WEIGHTS])
```

```python
import functools
import math

import jax
import jax.numpy as jnp
from jax import lax
from jax.experimental import pallas as pl
from jax.experimental.pallas import tpu as pltpu

F32 = jnp.float32
MXU_DTYPE = jnp.bfloat16
WIRE_DTYPE = jnp.bfloat16
MESH_ID = pl.DeviceIdType.MESH

N_CHIPS = 4
FF_SLOT = 768
FF_PAD = N_CHIPS * FF_SLOT
S5_WIDTH = 512
S5_GROUPS = 32
S5_STATE = 64
S5_LANES = S5_GROUPS * S5_STATE
S5_BLOCK = 512
S5_DIAG = S5_LANES // S5_BLOCK
SB_HEADS = 8
SB_DH = 64
SB_SCALE = 0.125
SB_PACK = 2
SB_QUERIES = 1024
SB_KEYS = 256
EPS = 1e-6
ADAM_LR, ADAM_B1, ADAM_B2, ADAM_EPS, ADAM_WD, ADAM_STEP = 0.001, 0.9, 0.999, 1e-08, 0.01, 10
VMEM_LIMIT = 56 * 1024 * 1024

ANY_SPEC = pl.BlockSpec(memory_space=pl.ANY)


def _params(*sem):
    return pltpu.CompilerParams(dimension_semantics=sem or None, vmem_limit_bytes=VMEM_LIMIT)


def _mm(a, b, *, name, ta=False, tb=False, out_dtype=F32, epilogue=None, extras=(), tm=1024, tn=1024, tk=1024, job=None):
    m, k = (a.shape[1], a.shape[0]) if ta else a.shape
    n = b.shape[0] if tb else b.shape[1]
    tm, tn, tk = min(tm, m), min(tn, n), min(tk, k)
    assert m % tm == 0 and n % tn == 0 and k % tk == 0, (name, m, n, k)
    grid = (m // tm, n // tn, k // tk)
    a_spec = pl.BlockSpec((tk, tm), lambda i, j, kk: (kk, i)) if ta else pl.BlockSpec((tm, tk), lambda i, j, kk: (i, kk))
    b_spec = pl.BlockSpec((tn, tk), lambda i, j, kk: (j, kk)) if tb else pl.BlockSpec((tk, tn), lambda i, j, kk: (kk, j))
    nk = grid[2]
    ex_specs = []
    for e in extras:
        if e.shape == (m, n):
            ex_specs.append(pl.BlockSpec((tm, tn), lambda i, j, kk: (i, j)))
        elif e.shape == (1, n):
            ex_specs.append(pl.BlockSpec((1, tn), lambda i, j, kk: (0, j)))
        else:
            assert e.shape == (m, 1), (name, e.shape)
            ex_specs.append(pl.BlockSpec((tm, 1), lambda i, j, kk: (i, 0)))
    dims = (((0 if ta else 1,), (1 if tb else 0,)), ((), ()))
    n_ex = len(extras)

    out_dtypes = list(out_dtype) if isinstance(out_dtype, (list, tuple)) else [out_dtype]
    n_out = len(out_dtypes)

    def body(a_ref, b_ref, *rest):
        ex_refs, o_refs = rest[:n_ex], rest[n_ex:n_ex + n_out]

        def product():
            return lax.dot_general(a_ref[...].astype(MXU_DTYPE), b_ref[...].astype(MXU_DTYPE), dims, preferred_element_type=F32)

        def finish(r):
            if epilogue is not None:
                r = epilogue(r, *[e[...] for e in ex_refs])
            for o_ref, val in zip(o_refs, r if isinstance(r, (tuple, list)) else (r,)):
                o_ref[...] = val.astype(o_ref.dtype)

        if nk == 1:
            finish(product())
            return
        acc_ref, kk = rest[n_ex + n_out], pl.program_id(2)

        @pl.when(kk == 0)
        def _():
            acc_ref[...] = jnp.zeros_like(acc_ref)

        acc_ref[...] += product()

        @pl.when(kk == nk - 1)
        def _():
            finish(acc_ref[...])

    res, got = _carried_call(
        body, name=name, grid=grid,
        in_specs=[a_spec, b_spec, *ex_specs],
        out_specs=[pl.BlockSpec((tm, tn), lambda i, j, kk: (i, j))] * n_out,
        out_shape=[jax.ShapeDtypeStruct((m, n), dt) for dt in out_dtypes],
        scratch_shapes=[pltpu.VMEM((tm, tn), F32)] if nk > 1 else [],
        semantics=("parallel", "parallel", "arbitrary"), operands=(a, b, *extras), job=job)
    res = res if isinstance(out_dtype, (list, tuple)) else res[0]
    return res if job is None else (res, got)


def _row_block(rows, want, tile=8):
    for tl in range(min(want, rows), tile - 1, -1):
        if rows % tl == 0 and tl % tile == 0:
            return tl
    return rows


def _rowmap(fn, ins, in_kinds, outs, *, name, tl):
    rows = next(x.shape[0] for x, kd in zip(ins, in_kinds) if kd == "r")
    tl = _row_block(rows, tl)
    n_in = len(ins)

    def spec(shape, kind):
        if kind == "r":
            return pl.BlockSpec((tl,) + tuple(shape[1:]), lambda i: (i,) + (0,) * (len(shape) - 1))
        return pl.BlockSpec(tuple(shape), lambda i: (0,) * len(shape))

    def body(*refs):
        in_refs, out_refs = refs[:n_in], refs[n_in:]
        res = fn(*[r[...] for r in in_refs])
        if not isinstance(res, (tuple, list)):
            res = (res,)
        for o_ref, val, (_, dt, kind) in zip(out_refs, res, outs):
            if kind == "r":
                o_ref[...] = val.astype(dt)
            else:
                @pl.when(pl.program_id(0) == 0)
                def _():
                    o_ref[...] = jnp.zeros_like(o_ref)

                o_ref[...] += val.astype(dt)

    has_acc = any(kd == "a" for _, _, kd in outs)
    res = pl.pallas_call(
        body, name=name, grid=(rows // tl,),
        in_specs=[spec(x.shape, kd) for x, kd in zip(ins, in_kinds)],
        out_specs=[spec(s, kd) for s, _, kd in outs],
        out_shape=[jax.ShapeDtypeStruct(s, dt) for s, dt, _ in outs],
        compiler_params=_params("arbitrary" if has_acc else "parallel"),
    )(*ins)
    return res[0] if len(outs) == 1 else res


def _rms_fwd(x, g):
    r = lax.rsqrt(jnp.mean(x * x, axis=-1, keepdims=True) + EPS)
    return x * r * g


def _rms_bwd(dh, x, g):
    r = lax.rsqrt(jnp.mean(x * x, axis=-1, keepdims=True) + EPS)
    xh = x * r
    dxh = dh * g
    dx = r * (dxh - xh * jnp.mean(dxh * xh, axis=-1, keepdims=True))
    return dx, jnp.sum(dh * xh, axis=0, keepdims=True)


def _swiglu_act(a, b):
    return jax.nn.silu(a) * b


def _ffn_up(x, g, wg, wu, *, name, tm=1024, tn=1024, job=None):
    m, d = x.shape
    n = wg.shape[0]
    tm, tn = min(tm, m), min(tn, n)
    assert m % tm == 0 and n % tn == 0, (name, m, n)

    def body(x_ref, g_ref, wg_ref, wu_ref, h_ref, a_ref, b_ref, s_ref):
        @pl.when(pl.program_id(1) == 0)
        def _():
            h_ref[...] = _rms_fwd(x_ref[...], g_ref[...]).astype(h_ref.dtype)

        hv = h_ref[...]
        av = lax.dot_general(hv, wg_ref[...], NT_DIMS, preferred_element_type=F32)
        bv = lax.dot_general(hv, wu_ref[...], NT_DIMS, preferred_element_type=F32)
        a_ref[...] = av.astype(a_ref.dtype)
        b_ref[...] = bv.astype(b_ref.dtype)
        s_ref[...] = _swiglu_act(av, bv).astype(s_ref.dtype)

    rows = pl.BlockSpec((tm, d), lambda i, j: (i, 0))
    wgt = pl.BlockSpec((tn, d), lambda i, j: (j, 0))
    tile = pl.BlockSpec((tm, tn), lambda i, j: (i, j))
    return _carried_call(
        body, name=name, grid=(m // tm, n // tn),
        in_specs=[rows, pl.BlockSpec((1, d), lambda i, j: (0, 0)), wgt, wgt],
        out_specs=[rows, tile, tile, tile],
        out_shape=[jax.ShapeDtypeStruct((m, d), MXU_DTYPE)] + [jax.ShapeDtypeStruct((m, n), MXU_DTYPE)] * 3,
        semantics=("parallel", "arbitrary"), operands=(x, g, wg, wu), job=job)


def _norm_proj(x, g, w, *, name, tm=1024, tn=1024, job=None):
    m, d = x.shape
    n = w.shape[1]
    tm, tn = min(tm, m), min(tn, n)
    assert m % tm == 0 and n % tn == 0, (name, m, n)

    def body(x_ref, g_ref, w_ref, h_ref, o_ref):
        @pl.when(pl.program_id(1) == 0)
        def _():
            h_ref[...] = _rms_fwd(x_ref[...], g_ref[...]).astype(h_ref.dtype)

        o_ref[...] = jnp.dot(h_ref[...], w_ref[...], preferred_element_type=F32)

    rows = pl.BlockSpec((tm, d), lambda i, j: (i, 0))
    return _carried_call(
        body, name=name, grid=(m // tm, n // tn),
        in_specs=[rows, pl.BlockSpec((1, d), lambda i, j: (0, 0)), pl.BlockSpec((d, tn), lambda i, j: (0, j))],
        out_specs=[rows, pl.BlockSpec((tm, tn), lambda i, j: (i, j))],
        out_shape=[jax.ShapeDtypeStruct((m, d), MXU_DTYPE), jax.ShapeDtypeStruct((m, n), F32)],
        semantics=("parallel", "arbitrary"), operands=(x, g, w), job=job)


def _proj_norm_bwd(pairs, x, g, dres, *, name, tm=1024, tk=1024, job=None):
    m, f = pairs[0][0].shape
    d = x.shape[1]
    tm, tk = min(tm, m), min(tk, f)
    assert m % tm == 0 and f % tk == 0, (name, m, f)
    nk, n_pairs = f // tk, len(pairs)
    swapped = [kept == "FD" for _, _, kept in pairs]

    def body(*refs):
        dy_refs, w_refs = refs[:n_pairs], refs[n_pairs:2 * n_pairs]
        x_ref, g_ref, dr_ref, dx_ref, dg_ref = refs[2 * n_pairs:2 * n_pairs + 5]
        i, kk = pl.program_id(0), pl.program_id(1)

        part = None
        for dy_ref, w_ref, rows_are_f in zip(dy_refs, w_refs, swapped):
            dims = (((1,), (0,)), ((), ())) if rows_are_f else NT_DIMS
            term = lax.dot_general(dy_ref[...].astype(MXU_DTYPE), w_ref[...], dims, preferred_element_type=F32)
            part = term if part is None else part + term

        @pl.when(jnp.logical_and(i == 0, kk == 0))
        def _():
            dg_ref[...] = jnp.zeros_like(dg_ref)

        def finish(dh):
            dx, dg = _rms_bwd(dh, x_ref[...], g_ref[...])
            dx_ref[...] = dx + dr_ref[...]
            dg_ref[...] += dg

        if nk == 1:
            finish(part)
            return
        acc_ref = refs[2 * n_pairs + 5]

        @pl.when(kk == 0)
        def _():
            acc_ref[...] = jnp.zeros_like(acc_ref)

        acc_ref[...] += part

        @pl.when(kk == nk - 1)
        def _():
            finish(acc_ref[...])

    act = pl.BlockSpec((tm, tk), lambda i, kk: (i, kk))
    w_specs = [pl.BlockSpec((tk, d), lambda i, kk: (kk, 0)) if s else pl.BlockSpec((d, tk), lambda i, kk: (0, kk)) for s in swapped]
    rows = pl.BlockSpec((tm, d), lambda i, kk: (i, 0))
    one = pl.BlockSpec((1, d), lambda i, kk: (0, 0))
    return _carried_call(
        body, name=name, grid=(m // tm, nk),
        in_specs=[act] * n_pairs + w_specs + [rows, one, rows],
        out_specs=[rows, one],
        out_shape=[jax.ShapeDtypeStruct((m, d), F32), jax.ShapeDtypeStruct((1, d), F32)],
        scratch_shapes=[pltpu.VMEM((tm, d), F32)] if nk > 1 else [],
        semantics=("arbitrary", "arbitrary"), operands=(*[p[0] for p in pairs], *[p[1] for p in pairs], x, g, dres), job=job)


def _ffn_dx(da, db, wg, wu, x, g, dres, *, name, job=None):
    return _proj_norm_bwd([(da, wg, "FD"), (db, wu, "FD")], x, g, dres, name=name, tm=512, tk=FF_PAD, job=job)


def _ffn_fwd(x, g, full, which, layer, late):
    tag = f"{which}_{layer}"
    job, keys = late.gather_job(f"{tag}_up") if late else (None, [])
    (h, a, b, s), got = _ffn_up(x, g, full[(f"{which}_w_gate", layer)], full[(f"{which}_w_up", layer)], name=f"{tag}_up", job=job)
    full.update(zip(keys, got))
    job, keys = late.gather_job(f"{tag}_down") if late else (None, [])
    x2, got = _carried(_mm, s, full[(f"{which}_w_down", layer)], name=f"{tag}_down", epilogue=lambda acc, xv: xv + 0.5 * acc,
                       extras=[x], tk=FF_PAD, job=job)
    full.update(zip(keys, got))
    return x2, (x, h, a, b, s)


def _ffn_bwd(dx2, saved, g, full, which, layer, grads, late, inline):
    x, h, a, b, s = saved
    tag = f"{which}_{layer}"
    kg, ku, kd = [(f"{which}_w_{n}", layer) for n in ("gate", "up", "down")]
    wg, wu, wd = full[kg], full[ku], full[kd]
    send = (lambda only: late.scatter_job(grads, only)) if (late and inline) else (lambda only: (None, []))

    def act_bwd(ds, av, bv):
        a32, b32, half = av.astype(F32), bv.astype(F32), 0.5 * ds
        sig = jax.nn.sigmoid(a32)
        return half * b32 * (sig * (1.0 + a32 * (1.0 - sig))), half * (a32 * sig)

    grads[kd] = _mm(s, dx2, ta=True, name=f"{tag}_dwd", out_dtype=WIRE_DTYPE, epilogue=lambda acc: 0.5 * acc, tk=2048)
    job, keys = send([kd])
    (da, db), got = _carried(_mm, dx2, wd, tb=True, name=f"{tag}_dact", epilogue=act_bwd, extras=[a, b],
                             out_dtype=[MXU_DTYPE, MXU_DTYPE], job=job)
    _note(late, keys, got)
    grads[kg] = _mm(da, h, ta=True, name=f"{tag}_dwg", out_dtype=WIRE_DTYPE, tk=4096)
    job, keys = send([kg]) if inline == "chain" else (None, [])
    grads[ku], got = _carried(_mm, db, h, ta=True, name=f"{tag}_dwu", out_dtype=WIRE_DTYPE, tk=4096, job=job)
    _note(late, keys, got)
    job, keys = send([ku] if inline == "chain" else [kg])
    (dx, dg), got = _ffn_dx(da, db, wg, wu, x, g, dx2, name=f"{tag}_dx", job=job)
    _note(late, keys, got)
    return dx, dg


def _carried(fn, *args, job, **kwargs):
    return fn(*args, job=job, **kwargs) if job is not None else (fn(*args, **kwargs), [])


def _note(late, keys, got):
    if late:
        late.received.update(zip(keys, got))


def _softplus(z):
    return jnp.maximum(z, 0.0) + jnp.log(1.0 + jnp.exp(-jnp.abs(z)))


def _ones_dot(x, tri):
    if MXU_DTYPE == F32:
        return jnp.dot(x, tri, preferred_element_type=F32)
    hi = x.astype(MXU_DTYPE)
    lo = (x - hi.astype(F32)).astype(MXU_DTYPE)
    return jnp.dot(hi, tri, preferred_element_type=F32) + jnp.dot(lo, tri, preferred_element_type=F32)


NT_DIMS = (((1,), (1,)), ((), ()))
TN_DIMS = (((0,), (0,)), ((), ()))


SB_LANES = SB_PACK * SB_DH
Q_COL, K_COL, V_COL = (S5_WIDTH * n // SB_LANES for n in (1, 2, 3))


def _head_lanes(rows, hd):
    return lax.broadcasted_iota(jnp.int32, (rows, SB_LANES), 1) // SB_DH == hd


def _attend(proj, *, tq=SB_QUERIES, job=None):
    seq = proj.shape[0]
    tq = min(tq, seq)
    tk = min(SB_KEYS, tq)
    per, hp = tq // tk, SB_PACK

    def body(q_ref, k_ref, v_ref, o_ref, ls_ref):
        i = pl.program_id(1)
        r_idx = lax.broadcasted_iota(jnp.int32, (tk, tk), 0)
        c_idx = lax.broadcasted_iota(jnp.int32, (tk, tk), 1)
        after = (r_idx > c_idx).astype(MXU_DTYPE)
        lanes = [_head_lanes(tk, hd) for hd in range(hp)]

        def block(j, cs, acc, straddles):
            off = pl.multiple_of(j * tk, tk)
            k2, v2 = k_ref[pl.ds(off, tk), :], v_ref[pl.ds(off, tk), :]
            top = 0 if straddles is None else straddles * tk
            rows = tq - top
            q2 = (q_ref[pl.ds(top, rows), :] * SB_SCALE).astype(MXU_DTYPE)
            new_cs, out = [], acc[top:]
            for hd in range(hp):
                kv = jnp.where(lanes[hd], k2, 0.0).astype(MXU_DTYPE)
                vv = jnp.where(lanes[hd], v2, 0.0).astype(MXU_DTYPE)
                z = lax.dot_general(q2, kv, NT_DIMS, preferred_element_type=F32)
                sp = _softplus(z)
                c_in = cs[hd][top:]
                if straddles is None:
                    lk = -sp
                    w = jnp.exp(z - sp + _ones_dot(lk, after) + c_in)
                else:
                    before = lax.broadcasted_iota(jnp.int32, (rows, tk), 1) < lax.broadcasted_iota(jnp.int32, (rows, tk), 0)
                    lk = jnp.where(before, -sp, 0.0)
                    w = jnp.where(before, jnp.exp(z - sp + _ones_dot(lk, after) + c_in), 0.0)
                out = out + jnp.dot(w.astype(MXU_DTYPE), vv, preferred_element_type=F32)
                c_new = c_in + jnp.sum(lk, axis=1, keepdims=True)
                new_cs.append(jnp.concatenate([cs[hd][:top], c_new], axis=0) if top else c_new)
            return tuple(new_cs), (jnp.concatenate([acc[:top], out], axis=0) if top else out)

        carry = (tuple(jnp.zeros((tq, 1), F32) for _ in range(hp)), jnp.zeros((tq, SB_LANES), F32))
        for s in reversed(range(per)):
            carry = block(i * per + s, *carry, s)
        cs, acc = lax.fori_loop(0, i * per, lambda n, cr: block(i * per - 1 - n, *cr, None), carry)
        o_ref[...] = acc
        for hd in range(hp):
            ls_ref[hd] = cs[hd]

    whole = lambda col: pl.BlockSpec((seq, SB_LANES), lambda g, i: (0, col + g))
    return _carried_call(
        body, name="sb_fwd", grid=(SB_HEADS // hp, seq // tq),
        in_specs=[pl.BlockSpec((tq, SB_LANES), lambda g, i: (i, Q_COL + g)), whole(K_COL), whole(V_COL)],
        out_specs=[pl.BlockSpec((tq, SB_LANES), lambda g, i: (i, g)), pl.BlockSpec((hp, tq, 1), lambda g, i: (g, i, 0))],
        out_shape=[jax.ShapeDtypeStruct((seq, SB_HEADS * SB_DH), F32), jax.ShapeDtypeStruct((SB_HEADS, seq, 1), F32)],
        semantics=("parallel", "parallel"), operands=(proj, proj, proj), job=job)


def _attend_bwd(proj, lsum, dmix, *, tq=SB_QUERIES, job=None):
    seq = proj.shape[0]
    tq = min(tq, seq)
    tk = min(SB_KEYS, tq)
    per, hp = tq // tk, SB_PACK
    do_col = S5_WIDTH // SB_LANES

    def body(q_ref, k_ref, v_ref, ls_ref, do_ref, dq_ref, dk_ref, dv_ref):
        i = pl.program_id(1)

        @pl.when(i == 0)
        def _():
            dk_ref[...] = jnp.zeros_like(dk_ref)
            dv_ref[...] = jnp.zeros_like(dv_ref)

        r_idx = lax.broadcasted_iota(jnp.int32, (tk, tk), 0)
        c_idx = lax.broadcasted_iota(jnp.int32, (tk, tk), 1)
        upto = (r_idx <= c_idx).astype(MXU_DTYPE)
        before = (r_idx < c_idx).astype(MXU_DTYPE)
        lanes = [_head_lanes(tk, hd) for hd in range(hp)]

        def block(j, sums, dq, straddles):
            off = pl.multiple_of(j * tk, tk)
            k2, v2 = k_ref[pl.ds(off, tk), :], v_ref[pl.ds(off, tk), :]
            top = 0 if straddles is None else straddles * tk
            rows = tq - top
            part = pl.ds(top, rows)
            q2 = (q_ref[part, :] * SB_SCALE).astype(MXU_DTYPE)
            do2 = do_ref[part, :].astype(MXU_DTYPE)
            valid = None
            if straddles is not None:
                valid = lax.broadcasted_iota(jnp.int32, (rows, tk), 1) < lax.broadcasted_iota(jnp.int32, (rows, tk), 0)
            new_sums, out, dk, dv = [], dq[top:], jnp.zeros((tk, SB_LANES), F32), jnp.zeros((tk, SB_LANES), F32)
            for hd in range(hp):
                cp, ce = sums[hd]
                kv = jnp.where(lanes[hd], k2, 0.0).astype(MXU_DTYPE)
                vv = jnp.where(lanes[hd], v2, 0.0).astype(MXU_DTYPE)
                z = lax.dot_general(q2, kv, NT_DIMS, preferred_element_type=F32)
                sp = _softplus(z)
                lk = -sp if valid is None else jnp.where(valid, -sp, 0.0)
                w = jnp.exp(z - sp + (ls_ref[hd, part, :] - cp[top:]) - _ones_dot(lk, upto))
                if valid is not None:
                    w = jnp.where(valid, w, 0.0)
                e = w * lax.dot_general(do2, vv, NT_DIMS, preferred_element_type=F32)
                earlier = jnp.dot(e.astype(MXU_DTYPE), before, preferred_element_type=F32) + ce[top:]
                keep = jnp.exp(-sp)
                dz = e * keep - (1.0 - keep) * earlier
                if valid is not None:
                    dz = jnp.where(valid, dz, 0.0)
                dzm = dz.astype(MXU_DTYPE)
                out = out + jnp.dot(dzm, kv, preferred_element_type=F32)
                dk = dk + jnp.where(lanes[hd], lax.dot_general(dzm, q2, TN_DIMS, preferred_element_type=F32), 0.0)
                dv = dv + jnp.where(lanes[hd], lax.dot_general(w.astype(MXU_DTYPE), do2, TN_DIMS, preferred_element_type=F32), 0.0)
                new = (cp[top:] + jnp.sum(lk, axis=1, keepdims=True), ce[top:] + jnp.sum(e, axis=1, keepdims=True))
                new_sums.append(tuple(jnp.concatenate([old[:top], val], axis=0) for old, val in zip((cp, ce), new)) if top else new)
            dk_ref[pl.ds(off, tk), :] += dk
            dv_ref[pl.ds(off, tk), :] += dv
            return tuple(new_sums), (jnp.concatenate([dq[:top], out], axis=0) if top else out)

        zero = jnp.zeros((tq, 1), F32)
        carry = (tuple((zero, zero) for _ in range(hp)), jnp.zeros((tq, SB_LANES), F32))
        carry = lax.fori_loop(0, i * per, lambda j, cr: block(j, *cr, None), carry)
        for s in range(per):
            carry = block(i * per + s, *carry, s)
        dq_ref[...] = carry[1] * SB_SCALE

    whole = lambda col: pl.BlockSpec((seq, SB_LANES), lambda g, i: (0, col + g))
    tile = lambda col: pl.BlockSpec((tq, SB_LANES), lambda g, i: (i, col + g))
    acc = pl.BlockSpec((seq, SB_LANES), lambda g, i: (0, g))
    return _carried_call(
        body, name="sb_bwd", grid=(SB_HEADS // hp, seq // tq),
        in_specs=[tile(Q_COL), whole(K_COL), whole(V_COL), pl.BlockSpec((hp, tq, 1), lambda g, i: (g, i, 0)), tile(do_col)],
        out_specs=[tile(0), acc, acc],
        out_shape=[jax.ShapeDtypeStruct((seq, SB_HEADS * SB_DH), F32)] * 3,
        semantics=("parallel", "arbitrary"), operands=(proj, proj, proj, lsum, dmix), job=job)


def _s5_disc(lr, li, ldt, br, bi):
    dt = jnp.exp(ldt)
    mag = jnp.exp(lr * dt)
    ar = mag * jnp.cos(li * dt)
    ai = mag * jnp.sin(li * dt)
    den = lr * lr + li * li
    nr = ar - 1.0
    cr = (nr * lr + ai * li) / den
    ci = (ai * lr - nr * li) / den
    return ar, ai, cr[None] * br - ci[None] * bi, cr[None] * bi + ci[None] * br


def _s5_prep(lr, li, ldt, br, bi):
    shapes = [lr.shape, lr.shape, br.shape, br.shape]

    def body(lr_ref, li_ref, ldt_ref, br_ref, bi_ref, *outs):
        for o, val in zip(outs, _s5_disc(lr_ref[...], li_ref[...], ldt_ref[...], br_ref[...], bi_ref[...])):
            o[...] = val

    return pl.pallas_call(body, name="s5_prep", out_shape=[jax.ShapeDtypeStruct(s, F32) for s in shapes])(lr, li, ldt, br, bi)


def _s5_prep_bwd(lr, li, ldt, br, bi, cts):
    args = (lr, li, ldt, br, bi)

    def body(*refs):
        ins, ct_refs, outs = refs[:5], refs[5:9], refs[9:]
        _, vjp = jax.vjp(_s5_disc, *[r[...] for r in ins])
        for o, val in zip(outs, vjp(tuple(r[...] for r in ct_refs))):
            o[...] = val

    return pl.pallas_call(body, name="s5_prep_bwd", out_shape=[jax.ShapeDtypeStruct(a.shape, F32) for a in args])(*args, *cts)


SCAN_ROWS = 8


def _powers(ar, ai):
    out = [(ar, ai)]
    for _ in range(SCAN_ROWS - 1):
        pr, pi = out[-1]
        out.append((pr * ar - pi * ai, pr * ai + pi * ar))
    return out


def _s5_states(u, bmat, cmat, a, d, *, tc=512):
    seq, width = u.shape
    nj, cols, w2 = bmat.shape
    tw = w2 // 2
    tc = min(tc, seq)
    assert seq % tc == 0 and nj * cols == width and tw == S5_BLOCK

    def body(u_ref, bm_ref, cm_ref, a_ref, d_ref, h_ref, y_ref, cr_ref, ci_ref):
        @pl.when(pl.program_id(1) == 0)
        def _():
            cr_ref[...] = jnp.zeros_like(cr_ref)
            ci_ref[...] = jnp.zeros_like(ci_ref)

        uv = u_ref[...]
        h_ref[...] = jnp.dot(uv.astype(MXU_DTYPE), bm_ref[0], preferred_element_type=F32)
        re, im = pl.ds(0, tw), pl.ds(tw, tw)
        powers = _powers(a_ref[:, re], a_ref[:, im])
        pr = jnp.concatenate([p[0] for p in powers], axis=0)
        pi = jnp.concatenate([p[1] for p in powers], axis=0)
        row_id = lax.broadcasted_iota(jnp.int32, (SCAN_ROWS, tw), 0)
        reach = {dist: tuple(jnp.where(row_id >= dist, part, 0.0) for part in powers[dist - 1]) for dist in (1, 2, 4)}

        def block(n, carry):
            hr, hi = carry
            rows = pl.ds(pl.multiple_of(n * SCAN_ROWS, SCAN_ROWS), SCAN_ROWS)
            yr, yi = h_ref[rows, re], h_ref[rows, im]
            for dist in (1, 2, 4):
                cr, ci = reach[dist]
                sr, si = pltpu.roll(yr, dist, 0), pltpu.roll(yi, dist, 0)
                yr, yi = yr + cr * sr - ci * si, yi + cr * si + ci * sr
            yr, yi = yr + pr * hr - pi * hi, yi + pr * hi + pi * hr
            h_ref[rows, re] = yr
            h_ref[rows, im] = yi
            return yr[SCAN_ROWS - 1:], yi[SCAN_ROWS - 1:]

        hr, hi = lax.fori_loop(0, tc // SCAN_ROWS, block, (cr_ref[...], ci_ref[...]), unroll=4)
        cr_ref[...] = hr
        ci_ref[...] = hi
        y_ref[...] = jnp.dot(h_ref[...].astype(MXU_DTYPE), cm_ref[0], preferred_element_type=F32) + d_ref[...] * uv

    io = pl.BlockSpec((tc, cols), lambda j, t: (t, j))
    return pl.pallas_call(
        body, name="s5_states", grid=(nj, seq // tc),
        in_specs=[io, pl.BlockSpec((1, cols, w2), lambda j, t: (j, 0, 0)), pl.BlockSpec((1, w2, cols), lambda j, t: (j, 0, 0)),
                  pl.BlockSpec((1, w2), lambda j, t: (0, j)), pl.BlockSpec((1, cols), lambda j, t: (0, j))],
        out_specs=[pl.BlockSpec((tc, w2), lambda j, t: (t, j)), io],
        out_shape=[jax.ShapeDtypeStruct((seq, nj * w2), F32), jax.ShapeDtypeStruct((seq, width), F32)],
        scratch_shapes=[pltpu.VMEM((1, tw), F32)] * 2,
        compiler_params=_params("parallel", "arbitrary"),
    )(u, bmat, cmat, a, d)


def _s5_states_bwd(dy, h, u, bmat, cmat, a, du_skip, *, tc=512):
    seq, width = u.shape
    nj, cols, w2 = bmat.shape
    tw = w2 // 2
    tc = min(tc, seq)
    assert seq % tc == 0
    nt = seq // tc

    def body(dy_ref, h_ref, u_ref, bm_ref, cm_ref, a_ref, sk_ref, du_ref, da_ref, db_ref, dc_ref, g_ref, cr_ref, ci_ref):
        @pl.when(pl.program_id(1) == 0)
        def _():
            cr_ref[...] = jnp.zeros_like(cr_ref)
            ci_ref[...] = jnp.zeros_like(ci_ref)
            da_ref[...] = jnp.zeros_like(da_ref)
            db_ref[...] = jnp.zeros_like(db_ref)
            dc_ref[...] = jnp.zeros_like(dc_ref)

        dyv = dy_ref[...].astype(MXU_DTYPE)
        g_ref[...] = lax.dot_general(dyv, cm_ref[0], NT_DIMS, preferred_element_type=F32)
        re, im = pl.ds(0, tw), pl.ds(tw, tw)
        powers = _powers(a_ref[:, re], a_ref[:, im])
        pr = jnp.concatenate([p[0] for p in reversed(powers)], axis=0)
        pi = jnp.concatenate([p[1] for p in reversed(powers)], axis=0)
        row_id = lax.broadcasted_iota(jnp.int32, (SCAN_ROWS, tw), 0)
        last = SCAN_ROWS - 1
        reach = {dist: tuple(jnp.where(row_id < SCAN_ROWS - dist, part, 0.0) for part in powers[dist - 1]) for dist in (1, 2, 4)}

        def block(n, carry):
            gr, gi, sr, si = carry
            rows = pl.ds(pl.multiple_of((tc // SCAN_ROWS - 1 - n) * SCAN_ROWS, SCAN_ROWS), SCAN_ROWS)
            yr, yi = g_ref[rows, re], g_ref[rows, im]
            for dist in (1, 2, 4):
                cr, ci = reach[dist]
                ur, ui = pltpu.roll(yr, SCAN_ROWS - dist, 0), pltpu.roll(yi, SCAN_ROWS - dist, 0)
                yr, yi = yr + cr * ur + ci * ui, yi + cr * ui - ci * ur
            yr, yi = yr + pr * gr + pi * gi, yi + pr * gi - pi * gr
            g_ref[rows, re] = yr
            g_ref[rows, im] = yi
            nr = jnp.where(row_id < last, pltpu.roll(yr, last, 0), gr)
            ni = jnp.where(row_id < last, pltpu.roll(yi, last, 0), gi)
            hr, hi = h_ref[rows, re], h_ref[rows, im]
            return yr[:1], yi[:1], sr + nr * hr + ni * hi, si + ni * hr - nr * hi

        zero = jnp.zeros((SCAN_ROWS, tw), F32)
        gr, gi, sr, si = lax.fori_loop(0, tc // SCAN_ROWS, block, (cr_ref[...], ci_ref[...], zero, zero), unroll=4)
        cr_ref[...] = gr
        ci_ref[...] = gi
        da_ref[:, re] += jnp.sum(sr, axis=0, keepdims=True)
        da_ref[:, im] += jnp.sum(si, axis=0, keepdims=True)
        gv = g_ref[...].astype(MXU_DTYPE)
        du_ref[...] = (lax.dot_general(gv, bm_ref[0], NT_DIMS, preferred_element_type=F32) + sk_ref[...]).astype(du_ref.dtype)
        db_ref[0] += lax.dot_general(u_ref[...].astype(MXU_DTYPE), gv, TN_DIMS, preferred_element_type=F32)
        dc_ref[0] += lax.dot_general(h_ref[...].astype(MXU_DTYPE), dyv, TN_DIMS, preferred_element_type=F32)

    io = pl.BlockSpec((tc, cols), lambda j, t: (nt - 1 - t, j))
    bm = pl.BlockSpec((1, cols, w2), lambda j, t: (j, 0, 0))
    cm = pl.BlockSpec((1, w2, cols), lambda j, t: (j, 0, 0))
    row = pl.BlockSpec((1, w2), lambda j, t: (0, j))
    return pl.pallas_call(
        body, name="s5_states_bwd", grid=(nj, nt),
        in_specs=[io, pl.BlockSpec((tc, w2), lambda j, t: (nt - 1 - t, j)), io, bm, cm, row, io],
        out_specs=[io, row, bm, cm],
        out_shape=[jax.ShapeDtypeStruct((seq, width), MXU_DTYPE), jax.ShapeDtypeStruct((1, nj * w2), F32),
                   jax.ShapeDtypeStruct(bmat.shape, F32), jax.ShapeDtypeStruct(cmat.shape, F32)],
        scratch_shapes=[pltpu.VMEM((tc, w2), F32)] + [pltpu.VMEM((1, tw), F32)] * 2,
        compiler_params=_params("parallel", "arbitrary"),
    )(dy, h, u, bmat, cmat, a, du_skip)


def _pair_columns(re, im, axis):
    shape = re.shape
    split = shape[:axis] + (shape[axis] // S5_BLOCK, S5_BLOCK) + shape[axis + 1:]
    both = jnp.stack([re.reshape(split), im.reshape(split)], axis=axis + 1)
    return both.reshape(shape[:axis] + (2 * shape[axis],) + shape[axis + 1:])


def _unpair_columns(t, axis):
    shape = t.shape
    both = t.reshape(shape[:axis] + (shape[axis] // (2 * S5_BLOCK), 2, S5_BLOCK) + shape[axis + 1:])
    half = shape[:axis] + (shape[axis] // 2,) + shape[axis + 1:]
    return (lax.index_in_dim(both, 0, axis + 1, keepdims=False).reshape(half),
            lax.index_in_dim(both, 1, axis + 1, keepdims=False).reshape(half))


S5_PER_BLOCK = S5_GROUPS // S5_DIAG


def _block_diag(t):
    g, a, b = t.shape
    n = S5_PER_BLOCK
    eye = jnp.eye(n, dtype=t.dtype)
    return (t.reshape(g // n, n, a, 1, b) * eye[None, :, None, :, None]).reshape(g // n, n * a, n * b)


def _block_diag_part(m):
    j, n = m.shape[0], S5_PER_BLOCK
    a, b = m.shape[1] // n, m.shape[2] // n
    return jnp.moveaxis(jnp.diagonal(m.reshape(j, n, a, n, b), axis1=1, axis2=3), -1, 1).reshape(j * n, a, b)


def _gelu_glu(y, gate_pre):
    z = jax.nn.gelu(y)
    return z * jax.nn.sigmoid(gate_pre)


def _s5_fwd(u, p, w_glu):
    lr, li = p["s5_lambda_re"][0], p["s5_lambda_im"][0]
    ldt = p["s5_log_dt"][0][:, None]
    br = p["s5_b_re"][0].transpose(2, 0, 1)
    bi = p["s5_b_im"][0].transpose(2, 0, 1)
    ar, ai, bbr, bbi = _s5_prep(lr, li, ldt, br, bi)
    a = _pair_columns(ar.reshape(1, S5_LANES), ai.reshape(1, S5_LANES), 1)
    bmat = jnp.concatenate([_block_diag(bbr.transpose(1, 0, 2)), _block_diag(bbi.transpose(1, 0, 2))], axis=2)
    cmat = jnp.concatenate([_block_diag(p["s5_c_re"][0].transpose(0, 2, 1)),
                            -_block_diag(p["s5_c_im"][0].transpose(0, 2, 1))], axis=1)
    bmat, cmat = bmat.astype(MXU_DTYPE), cmat.astype(MXU_DTYPE)
    d = p["s5_d"]
    h, y = _s5_states(u, bmat, cmat, a, d)
    z = _rowmap(jax.nn.gelu, [y], "r", [(y.shape, MXU_DTYPE, "r")], name="s5_gelu", tl=512)
    gate_pre = _mm(z, w_glu, name="s5_glu")
    out = _rowmap(_gelu_glu, [y, gate_pre], "rr", [(y.shape, F32, "r")], name="s5_gate", tl=512)
    return out, (u, lr, li, ldt, br, bi, a, bmat, cmat, h, y, z, gate_pre)


def _s5_bwd(dout, saved, p, w_glu):
    u, lr, li, ldt, br, bi, a, bmat, cmat, h, y, z, gate_pre = saved
    d = p["s5_d"]

    def gate_bwd(dov, yv, gv):
        zv = jax.nn.gelu(yv)
        sg = jax.nn.sigmoid(gv)
        return dov * sg, dov * zv * sg * (1.0 - sg)

    dz_direct, dgate = _rowmap(gate_bwd, [dout, y, gate_pre], "rrr", [(y.shape, F32, "r"), (y.shape, MXU_DTYPE, "r")],
                               name="s5_gate_bwd", tl=512)
    dw_glu = _mm(z, dgate, ta=True, name="s5_dwglu", out_dtype=WIRE_DTYPE)
    dz = _mm(dgate, w_glu, tb=True, name="s5_dz", epilogue=lambda acc, prev: acc + prev, extras=[dz_direct])

    def gelu_bwd(dzv, yv, uv, dvv):
        _, vjp = jax.vjp(jax.nn.gelu, yv)
        dy = vjp(dzv)[0]
        return dy, dy * dvv, jnp.sum(dy * uv, axis=0, keepdims=True)

    dy, du_skip, dd = _rowmap(gelu_bwd, [dz, y, u, d], "rrrc",
                              [(y.shape, F32, "r"), (y.shape, F32, "r"), (d.shape, F32, "a")], name="s5_gelu_bwd", tl=512)
    du, da, dbmat, dcmat = _s5_states_bwd(dy, h, u, bmat, cmat, a, du_skip)
    dbbr, dbbi = (_block_diag_part(t).transpose(1, 0, 2) for t in (dbmat[:, :, :S5_BLOCK], dbmat[:, :, S5_BLOCK:]))
    dar, dai = _unpair_columns(da, 1)
    cts = (dar.reshape(S5_GROUPS, S5_STATE), dai.reshape(S5_GROUPS, S5_STATE), dbbr, dbbi)
    dlr, dli, dldt, dbr, dbi = _s5_prep_bwd(lr, li, ldt, br, bi, cts)
    dcr, dci = (_block_diag_part(t).transpose(0, 2, 1) for t in (dcmat[:, :S5_BLOCK], dcmat[:, S5_BLOCK:]))
    grads = {
        "s5_lambda_re": dlr[None], "s5_lambda_im": dli[None], "s5_log_dt": dldt[:, 0][None],
        "s5_b_re": dbr.transpose(1, 2, 0)[None], "s5_b_im": dbi.transpose(1, 2, 0)[None],
        "s5_c_re": dcr[None], "s5_c_im": -dci[None], "s5_d": dd,
    }
    return du, dw_glu, grads


def _mix0_fwd(x, g, p, full, late):
    (h, proj), _ = _norm_proj(x, g, full[("ab_w_in", 0)], name="mix0_in")
    u = proj[:, :S5_WIDTH]
    job, keys = late.gather_job("sb_fwd") if late else (None, [])
    (o, lsum), got = _attend(proj, job=job)
    full.update(zip(keys, got))
    w_glu, w_out = full[("s5_w_glu", 0)], full[("ab_w_out", 0)]
    y_a, s5_saved = _s5_fwd(u, p, w_glu)
    mix = jnp.concatenate([y_a, o], axis=1).astype(MXU_DTYPE)
    x2 = _mm(mix, w_out, name="mix0_out", epilogue=lambda acc, xv: xv + acc, extras=[x])
    return x2, (x, h, proj, lsum, mix, s5_saved)


def _mix0_bwd(dx2, saved, g, p, full, grads, late):
    x, h, proj, lsum, mix, s5_saved = saved
    w_in, w_glu, w_out = full[("ab_w_in", 0)], full[("s5_w_glu", 0)], full[("ab_w_out", 0)]
    dmix = _mm(dx2, w_out, tb=True, name="mix0_dmix")
    grads[("ab_w_out", 0)] = _mm(mix, dx2, ta=True, name="mix0_dwout", out_dtype=WIRE_DTYPE)
    du, grads[("s5_w_glu", 0)], s5_grads = _s5_bwd(dmix[:, :S5_WIDTH], s5_saved, p, w_glu)
    job, keys = late.scatter_job(grads) if late else (None, [])
    (dq, dk, dv), got = _attend_bwd(proj, lsum, dmix, job=job)
    _note(late, keys, got)
    dproj = jnp.concatenate([du] + [t.astype(MXU_DTYPE) for t in (dq, dk, dv)], axis=1)
    grads[("ab_w_in", 0)] = _mm(h, dproj, ta=True, name="mix0_dwin", out_dtype=WIRE_DTYPE)
    job, keys = late.scatter_job(grads) if late else (None, [])
    (dx, dg), got = _proj_norm_bwd([(dproj, w_in, "DF")], x, g, dx2, name="mix0_dh", job=job)
    _note(late, keys, got)
    return dx, dg, s5_grads


def _shift_down(t, n):
    rows = lax.broadcasted_iota(jnp.int32, t.shape, 0)
    return jnp.where(rows >= n, pltpu.roll(t, n, 0), 0.0)


def _shift_up(t, n):
    rows = lax.broadcasted_iota(jnp.int32, t.shape, 0)
    return jnp.where(rows < t.shape[0] - n, pltpu.roll(t, t.shape[0] - n, 0), 0.0)


def _conv_fwd(proj, cw, *, tc=128):
    seq, c3 = proj.shape
    ch = c3 // 3
    nb = ch // tc

    def body(b_ref, c_ref, v_ref, w_ref, m_ref):
        pv = c_ref[...] * v_ref[...]
        w = w_ref[...]
        y = w[2:3] * pv + w[1:2] * _shift_down(pv, 1) + w[0:1] * _shift_down(pv, 2)
        m_ref[...] = (b_ref[...] * y).astype(m_ref.dtype)

    col = lambda part: pl.BlockSpec((seq, tc), lambda j: (0, part * nb + j))
    return pl.pallas_call(
        body, name="conv_fwd", grid=(nb,),
        in_specs=[col(0), col(1), col(2), pl.BlockSpec((3, tc), lambda j: (0, j))],
        out_specs=pl.BlockSpec((seq, tc), lambda j: (0, j)),
        out_shape=jax.ShapeDtypeStruct((seq, ch), MXU_DTYPE),
        compiler_params=_params("parallel"),
    )(proj, proj, proj, cw)


def _conv_bwd(proj, cw, dm, *, tc=128):
    seq, c3 = proj.shape
    ch = c3 // 3
    nb = ch // tc

    def body(b_ref, c_ref, v_ref, w_ref, dm_ref, dproj_ref, dw_ref, dc_ref, dv_ref):
        part = pl.program_id(1)

        @pl.when(part == 0)
        def _():
            cv, vv, dmv = c_ref[...], v_ref[...], dm_ref[...]
            pv = cv * vv
            w = w_ref[...]
            p1, p2 = _shift_down(pv, 1), _shift_down(pv, 2)
            y = w[2:3] * pv + w[1:2] * p1 + w[0:1] * p2
            dproj_ref[...] = (dmv * y).astype(dproj_ref.dtype)
            dy = dmv * b_ref[...]
            dp = w[2:3] * dy + w[1:2] * _shift_up(dy, 1) + w[0:1] * _shift_up(dy, 2)
            dc_ref[...] = (dp * vv).astype(dc_ref.dtype)
            dv_ref[...] = (dp * cv).astype(dv_ref.dtype)
            dw_ref[...] = jnp.concatenate([jnp.sum(dy * p2, axis=0, keepdims=True), jnp.sum(dy * p1, axis=0, keepdims=True),
                                           jnp.sum(dy * pv, axis=0, keepdims=True)], axis=0)

        @pl.when(part == 1)
        def _():
            dproj_ref[...] = dc_ref[...]

        @pl.when(part == 2)
        def _():
            dproj_ref[...] = dv_ref[...]

    col = lambda part: pl.BlockSpec((seq, tc), lambda j, t: (0, part * nb + j))
    small = pl.BlockSpec((3, tc), lambda j, t: (0, j))
    return pl.pallas_call(
        body, name="conv_bwd", grid=(nb, 3),
        in_specs=[col(0), col(1), col(2), small, pl.BlockSpec((seq, tc), lambda j, t: (0, j))],
        out_specs=[pl.BlockSpec((seq, tc), lambda j, t: (0, t * nb + j)), small],
        out_shape=[jax.ShapeDtypeStruct((seq, c3), MXU_DTYPE), jax.ShapeDtypeStruct((3, ch), F32)],
        scratch_shapes=[pltpu.VMEM((seq, tc), MXU_DTYPE)] * 2,
        compiler_params=_params("parallel", "arbitrary"),
    )(proj, proj, proj, cw, dm)


def _mix1_fwd(x, g, full, late):
    job, keys = late.gather_job("mix1_in") if late else (None, [])
    (h, proj), got = _norm_proj(x, g, full[("sc_w_in", 0)], name="mix1_in", job=job)
    full.update(zip(keys, got))
    m = _conv_fwd(proj, full[("sc_conv_w", 0)])
    x2 = _mm(m, full[("sc_w_out", 0)], name="mix1_out", epilogue=lambda acc, xv: xv + acc, extras=[x])
    return x2, (x, h, proj, m)


def _mix1_bwd(dx2, saved, g, full, grads, late):
    x, h, proj, m = saved
    w_in, cw, w_out = full[("sc_w_in", 0)], full[("sc_conv_w", 0)], full[("sc_w_out", 0)]
    dm = _mm(dx2, w_out, tb=True, name="mix1_dm")
    grads[("sc_w_out", 0)] = _mm(m, dx2, ta=True, name="mix1_dwout", out_dtype=WIRE_DTYPE)
    dproj, dcw = _conv_bwd(proj, cw, dm)
    grads[("sc_conv_w", 0)] = dcw.astype(WIRE_DTYPE)
    grads[("sc_w_in", 0)] = _mm(h, dproj, ta=True, name="mix1_dwin", out_dtype=WIRE_DTYPE)
    job, keys = late.scatter_job(grads, [("ffn2_w_up", 1)]) if late else (None, [])
    (dx, dg), got = _proj_norm_bwd([(dproj, w_in, "DF")], x, g, dx2, name="mix1_dh", job=job)
    _note(late, keys, got)
    return dx, dg


def _loss_head(x, g, target):
    feat = x.shape[1]

    def fn(xv, gv, tv):
        err = _rms_fwd(xv, gv) - tv
        dx, dg = _rms_bwd(err / feat, xv, gv)
        return jnp.sum(err * err, keepdims=True) * (0.5 / feat), dx, dg

    return _rowmap(fn, [x, g, target], "rcr", [((1, 1), F32, "a"), (x.shape, F32, "r"), (g.shape, F32, "a")],
                   name="loss_head", tl=256)


def _slot(ref, place, chip=None, half=None, piece=(0, 1)):
    axis, width = place
    shape = list(ref.shape)
    start = [0, 0]
    if chip is not None:
        start[axis], shape[axis] = chip * width, width
    if half is not None:
        h_axis = 0 if shape[0] % 32 == 0 else 1
        shape[h_axis] //= 2 * piece[1]
        start[h_axis] = start[h_axis] + (half * piece[1] + piece[0]) * shape[h_axis]
    hint = lambda s, d: s if isinstance(s, int) else pl.multiple_of(s, 128 if d == 1 else 8)
    return ref.at[tuple(pl.ds(hint(s, d), n) for d, (s, n) in enumerate(zip(start, shape)))]


class _Exchange:
    def __init__(self, kind, arrays, places, pieces=1):
        self.kind, self.arrays, self.places, self.n, self.pieces = kind, list(arrays), list(places), len(arrays), pieces
        self.out_shape = []
        for t, (axis, width) in zip(self.arrays, self.places):
            if kind == "gather":
                shape = list(t.shape)
                shape[axis] = N_CHIPS * width
            else:
                shape = [N_CHIPS] + list(t.shape)
                shape[1 + axis] = width
            self.out_shape.append(jax.ShapeDtypeStruct(tuple(shape), t.dtype))
        n = self.n
        self.scratch = [pltpu.SemaphoreType.DMA((3 * n * pieces,)) for _ in range(4 if kind == "gather" else 2)]
        self.scratch.append(pltpu.SemaphoreType.DMA((n,)))

    def _copies(self, ins, outs, sems):
        x, y, c = lax.axis_index("x"), lax.axis_index("y"), lax.axis_index("c")
        peers = [(1 - x, y), (x, 1 - y), (1 - x, 1 - y)]
        remote = lambda src, dst, send, recv, k, to: pltpu.make_async_remote_copy(
            src_ref=src, dst_ref=dst, send_sem=send.at[k], recv_sem=recv.at[k], device_id=to, device_id_type=MESH_ID)
        local, ici, d2d = [], [], []
        for a in range(self.n):
            place = self.places[a]
            if self.kind == "gather":
                local.append(pltpu.make_async_copy(ins[a], _slot(outs[a], place, 2 * x + y), sems[4].at[a]))
                for q in range(self.pieces):
                    for r, (px, py) in enumerate(peers):
                        k, part = (3 * a + r) * self.pieces + q, (q, self.pieces)
                        ici.append(remote(_slot(ins[a], place, None, c, part), _slot(outs[a], place, 2 * x + y, c, part),
                                          sems[0], sems[1], k, (px, py, c)))
                        landed = _slot(outs[a], place, 2 * px + py, c, part)
                        d2d.append(remote(landed, landed, sems[2], sems[3], k, (x, y, 1 - c)))
            else:
                local.append(pltpu.make_async_copy(_slot(ins[a], place, 2 * x + y), outs[a].at[3], sems[2].at[a]))
                for r, (px, py) in enumerate(peers):
                    ici.append(remote(_slot(ins[a], place, 2 * px + py), outs[a].at[r], sems[0], sems[1], 3 * a + r, (px, py, c)))
        return local, ici, d2d

    def start(self, ins, outs, sems):
        local, ici, _ = self._copies(ins, outs, sems)
        for cp in local + ici:
            cp.start()

    def relay(self, ins, outs, sems):
        _, ici, d2d = self._copies(ins, outs, sems)
        for arrived, onward in zip(ici, d2d):
            arrived.wait_recv()
            onward.start()

    def finish(self, ins, outs, sems):
        local, ici, d2d = self._copies(ins, outs, sems)
        for cp in local + d2d:
            cp.wait()
        for cp in ici:
            cp.wait_send() if d2d else cp.wait()


def _exchange_call(job, name):
    n = job.n

    def body(*refs):
        ins, outs, sems = refs[:n], refs[n:2 * n], refs[2 * n:]
        job.start(ins, outs, sems)
        job.relay(ins, outs, sems)
        job.finish(ins, outs, sems)

    return pl.pallas_call(
        body, name=name, in_specs=[ANY_SPEC] * n, out_specs=[ANY_SPEC] * n, out_shape=job.out_shape,
        scratch_shapes=job.scratch, compiler_params=pltpu.CompilerParams(has_side_effects=True),
    )(*job.arrays)


def _carried_call(body, *, name, grid, in_specs, out_specs, out_shape, semantics, operands, scratch_shapes=(), job=None):
    scratch_shapes = list(scratch_shapes)
    if job is None:
        return pl.pallas_call(body, name=name, grid=grid, in_specs=in_specs, out_specs=out_specs, out_shape=out_shape,
                              scratch_shapes=scratch_shapes, compiler_params=_params(*semantics))(*operands), []
    n_in, n_out, n, n_scr = len(in_specs), len(out_specs), job.n, len(scratch_shapes)
    steps = math.prod(grid)

    def wrapped(*refs):
        ins, job_ins = refs[:n_in], refs[n_in:n_in + n]
        outs, job_outs = refs[n_in + n:n_in + n + n_out], refs[n_in + n + n_out:n_in + 2 * n + n_out]
        outs = outs + refs[n_in + 2 * n + n_out:n_in + 2 * n + n_out + n_scr]
        sems = refs[n_in + 2 * n + n_out + n_scr:]
        step = functools.reduce(lambda acc, d: acc * grid[d] + pl.program_id(d), range(len(grid)), 0)

        @pl.when(step == 0)
        def _():
            job.start(job_ins, job_outs, sems)

        @pl.when(step == (3 * steps) // 4)
        def _():
            job.relay(job_ins, job_outs, sems)

        body(*ins, *outs)

        @pl.when(step == steps - 1)
        def _():
            job.finish(job_ins, job_outs, sems)

    res = pl.pallas_call(
        wrapped, name=name, grid=grid, in_specs=list(in_specs) + [ANY_SPEC] * n, out_specs=list(out_specs) + [ANY_SPEC] * n,
        out_shape=list(out_shape) + job.out_shape, scratch_shapes=scratch_shapes + job.scratch,
        compiler_params=pltpu.CompilerParams(dimension_semantics=("arbitrary",) * len(grid), vmem_limit_bytes=VMEM_LIMIT,
                                             has_side_effects=True),
    )(*operands, *job.arrays)
    return res[:n_out], res[n_out:]


def _swap_and_spread(parts, t):
    n = len(parts)

    def body(*refs):
        ins, t_ref, outs, slots = refs[:n], refs[n], refs[n + 1:2 * n + 1], refs[2 * n + 1]
        send, recv, all_send, all_recv, own = refs[2 * n + 2:]
        x, y, c = lax.axis_index("x"), lax.axis_index("y"), lax.axis_index("c")
        mine = slots.at[4 * x + 2 * y + c]
        copies = [pltpu.make_async_copy(t_ref, mine, own)]
        copies += [pltpu.make_async_remote_copy(src_ref=ins[a], dst_ref=outs[a], send_sem=send.at[a], recv_sem=recv.at[a],
                                                device_id=(x, y, 1 - c), device_id_type=MESH_ID) for a in range(n)]
        for m in range(1, 8):
            peer = (x ^ (m >> 2), y ^ ((m >> 1) & 1), c ^ (m & 1))
            copies.append(pltpu.make_async_remote_copy(src_ref=t_ref, dst_ref=mine, send_sem=all_send.at[m - 1],
                                                       recv_sem=all_recv.at[m - 1], device_id=peer, device_id_type=MESH_ID))
        for cp in copies:
            cp.start()
        for cp in copies:
            cp.wait()

    res = pl.pallas_call(
        body, name="swap_and_spread",
        in_specs=[ANY_SPEC] * (n + 1), out_specs=[ANY_SPEC] * (n + 1),
        out_shape=[jax.ShapeDtypeStruct(p.shape, p.dtype) for p in parts] + [jax.ShapeDtypeStruct((8,) + t.shape, t.dtype)],
        scratch_shapes=[pltpu.SemaphoreType.DMA((n,)), pltpu.SemaphoreType.DMA((n,)), pltpu.SemaphoreType.DMA((7,)),
                        pltpu.SemaphoreType.DMA((7,)), pltpu.SemaphoreType.DMA(())],
        compiler_params=pltpu.CompilerParams(has_side_effects=True),
    )(*parts, t)
    return res[:n], res[n]


def _adamw(w, g, m, v):
    m = ADAM_B1 * m + (1.0 - ADAM_B1) * g
    v = ADAM_B2 * v + (1.0 - ADAM_B2) * jnp.square(g)
    m_hat = m / (1.0 - ADAM_B1 ** ADAM_STEP)
    v_hat = v / (1.0 - ADAM_B2 ** ADAM_STEP)
    return -ADAM_LR * (m_hat / (jnp.sqrt(v_hat) + ADAM_EPS) + ADAM_WD * w), m, v


def _chip_sums(group, name):
    rows, cols = group[0].shape[1:]
    count = len(group)
    tl = _row_block(rows, 512 if count == 1 else 128, tile=32 // group[0].dtype.itemsize)

    def body(*refs):
        for r_ref, o_ref in zip(refs[:count], refs[count:]):
            total = ((r_ref[0].astype(F32) + r_ref[1].astype(F32)) + r_ref[2].astype(F32)) + r_ref[3].astype(F32)
            o_ref[...] = total.astype(o_ref.dtype)

    return pl.pallas_call(body, name=name, grid=(rows // tl,),
                          in_specs=[pl.BlockSpec((N_CHIPS, tl, cols), lambda i: (0, i, 0))] * count,
                          out_specs=[pl.BlockSpec((tl, cols), lambda i: (i, 0))] * count,
                          out_shape=[jax.ShapeDtypeStruct((rows, cols), group[0].dtype)] * count,
                          compiler_params=_params("parallel"))(*group)


def _adamw_layer(w, m, v, p_mine, p_other, layer, prev, name):
    _, rows, cols = w.shape
    assert p_mine.shape[1] == cols and p_mine.shape[0] >= rows
    tl = _row_block(rows, 512, tile=32 // p_mine.dtype.itemsize)
    tc = cols
    if tl < 128 < rows and cols % 256 == 0:
        tl, tc = rows, 256

    def body(w_ref, m_ref, v_ref, pa_ref, pb_ref, *rest):
        g = pa_ref[...].astype(F32) + pb_ref[...].astype(F32)
        for o_ref, val in zip(rest[-4:], (g,) + _adamw(w_ref[...], g, m_ref[...], v_ref[...])):
            o_ref[...] = val

    stacked = pl.BlockSpec((None, tl, tc), lambda i, j: (layer, i, j))
    part = pl.BlockSpec((tl, tc), lambda i, j: (i, j))
    kept = list(prev) if prev else []
    return pl.pallas_call(
        body, name=name, grid=(rows // tl, cols // tc),
        in_specs=[stacked] * 3 + [part] * 2 + [ANY_SPEC] * len(kept),
        out_specs=[stacked] * 4, out_shape=[jax.ShapeDtypeStruct(w.shape, F32)] * 4,
        input_output_aliases={5 + k: k for k in range(len(kept))},
        compiler_params=_params("parallel", "parallel"),
    )(w, m, v, p_mine, p_other, *kept)


def _adamw_small(w, slots, m, v):
    def fn(wv, sv, mv, vv):
        g = sv[0]
        for dev in range(1, 8):
            g = g + sv[dev]
        return (g,) + _adamw(wv, g, mv, vv)

    return _rowmap(fn, [w, slots, m, v], "rcrr", [(w.shape, F32, "r")] * 4, name="adamw_small", tl=w.shape[0])


WEIGHTS = ['ffn1_norm', 'ffn1_w_gate', 'ffn1_w_up', 'ffn1_w_down', 'mix_norm', 'ffn2_norm', 'ffn2_w_gate', 'ffn2_w_up',
           'ffn2_w_down', 'ab_w_in', 's5_lambda_re', 's5_lambda_im', 's5_log_dt', 's5_b_re', 's5_b_im', 's5_c_re', 's5_c_im',
           's5_d', 's5_w_glu', 'ab_w_out', 'sc_w_in', 'sc_conv_w', 'sc_w_out', 'final_norm']
SHARDED = {'ffn1_w_gate': (0, FF_SLOT), 'ffn1_w_up': (0, FF_SLOT), 'ffn1_w_down': (0, FF_SLOT),
           'ffn2_w_gate': (0, FF_SLOT), 'ffn2_w_up': (0, FF_SLOT), 'ffn2_w_down': (0, FF_SLOT),
           'ab_w_in': (1, 512), 's5_w_glu': (0, 128), 'ab_w_out': (0, 256), 'sc_w_in': (1, 768), 'sc_conv_w': (1, 256),
           'sc_w_out': (0, 256)}
SWAPPED = ('ffn1_w_gate', 'ffn1_w_up', 'ffn2_w_gate', 'ffn2_w_up')
SMALL = [n for n in WEIGHTS if n not in SHARDED]


def _held(name, t):
    return jnp.swapaxes(t, 1, 2) if name in SWAPPED else t


def _pack(arrays):
    rows = []
    for t in arrays:
        flat = t.reshape(-1)
        rows.append(jnp.pad(flat, (0, (-flat.shape[0]) % 128)))
    flat = jnp.concatenate(rows)
    return jnp.pad(flat, (0, (-flat.shape[0]) % 1024)).reshape(-1, 128)


def _unpack(packed, like):
    flat, out, pos = packed.reshape(-1), [], 0
    for t in like:
        out.append(flat[pos:pos + t.size].reshape(t.shape))
        pos += t.size + (-t.size) % 128
    return out


def _local_grads(x, target, p, full, late=None):
    small, grads, saved = {}, {}, []
    for layer in range(2):
        x, s1 = _ffn_fwd(x, p["ffn1_norm"][layer:layer + 1], full, "ffn1", layer, late)
        if layer == 0:
            x, sm = _mix0_fwd(x, p["mix_norm"][0:1], p, full, late)
        else:
            x, sm = _mix1_fwd(x, p["mix_norm"][1:2], full, late)
        x, s2 = _ffn_fwd(x, p["ffn2_norm"][layer:layer + 1], full, "ffn2", layer, late)
        saved.append((s1, sm, s2))
    loss, dx, dg_final = _loss_head(x, p["final_norm"][None], target)
    small["final_norm"] = dg_final[0]
    gains = {n: [None, None] for n in ("ffn1_norm", "mix_norm", "ffn2_norm")}

    def ffn_bwd(which, layer, dx, s):
        dx, dg = _ffn_bwd(dx, s, p[f"{which}_norm"][layer:layer + 1], full, which, layer, grads, late,
                          inline={("ffn2", 1): "defer", ("ffn1", 0): "chain"}.get((which, layer)))
        gains[f"{which}_norm"][layer] = dg[0]
        return dx

    for layer in (1, 0):
        s1, sm, s2 = saved[layer]
        dx = ffn_bwd("ffn2", layer, dx, s2)
        if layer == 0:
            dx, dg, s5_grads = _mix0_bwd(dx, sm, p["mix_norm"][0:1], p, full, grads, late)
            small.update(s5_grads)
        else:
            dx, dg = _mix1_bwd(dx, sm, p["mix_norm"][1:2], full, grads, late)
        gains["mix_norm"][layer] = dg[0]
        dx = ffn_bwd("ffn1", layer, dx, s1)
    small.update({n: jnp.stack(pair) for n, pair in gains.items()})
    return loss, dx, small, grads


_GATHER_PLAN = {
    "gather_early": [("ffn1_w_gate", 0), ("ffn1_w_up", 0)],
    "ffn1_0_up": [("ffn1_w_down", 0), ("ab_w_in", 0)],
    "sb_fwd": [("s5_w_glu", 0), ("ab_w_out", 0), ("ffn2_w_gate", 0), ("ffn2_w_up", 0), ("ffn2_w_down", 0),
               ("ffn1_w_gate", 1), ("ffn1_w_up", 1), ("ffn1_w_down", 1), ("ffn2_w_gate", 1)],
    "ffn2_0_up": [("sc_w_in", 0), ("sc_conv_w", 0), ("sc_w_out", 0)],
    "ffn1_1_up": [("ffn2_w_up", 1)],
    "ffn2_1_up": [("ffn2_w_down", 1)],
}


class _Late:
    def __init__(self, shards, places):
        self.shards, self.places = shards, places
        self.sent, self.received = set(), {}

    def gather_job(self, carrier):
        keys = _GATHER_PLAN.get(carrier, [])
        if not keys:
            return None, []
        pieces = 4 if carrier == "gather_early" else 1
        return _Exchange("gather", [self.shards[k] for k in keys], [self.places[k] for k in keys], pieces), keys

    def scatter_job(self, grads, only=None):
        keys = [k for k in grads if k not in self.sent and (only is None or k in only)]
        if not keys:
            return None, []
        self.sent.update(keys)
        return _Exchange("scatter", [grads[k] for k in keys], [self.places[k] for k in keys]), keys


def kernel(x, ffn1_norm, ffn1_w_gate, ffn1_w_up, ffn1_w_down, mix_norm, ffn2_norm, ffn2_w_gate, ffn2_w_up, ffn2_w_down, ab_w_in, s5_lambda_re, s5_lambda_im, s5_log_dt, s5_b_re, s5_b_im, s5_c_re, s5_c_im, s5_d, s5_w_glu, ab_w_out, sc_w_in, sc_conv_w, sc_w_out, final_norm, loss_target, m_ffn1_norm, m_ffn1_w_gate, m_ffn1_w_up, m_ffn1_w_down, m_mix_norm, m_ffn2_norm, m_ffn2_w_gate, m_ffn2_w_up, m_ffn2_w_down, m_ab_w_in, m_s5_lambda_re, m_s5_lambda_im, m_s5_log_dt, m_s5_b_re, m_s5_b_im, m_s5_c_re, m_s5_c_im, m_s5_d, m_s5_w_glu, m_ab_w_out, m_sc_w_in, m_sc_conv_w, m_sc_w_out, m_final_norm, v_ffn1_norm, v_ffn1_w_gate, v_ffn1_w_up, v_ffn1_w_down, v_mix_norm, v_ffn2_norm, v_ffn2_w_gate, v_ffn2_w_up, v_ffn2_w_down, v_ab_w_in, v_s5_lambda_re, v_s5_lambda_im, v_s5_log_dt, v_s5_b_re, v_s5_b_im, v_s5_c_re, v_s5_c_im, v_s5_d, v_s5_w_glu, v_ab_w_out, v_sc_w_in, v_sc_conv_w, v_sc_w_out, v_final_norm):
    args = dict(locals())
    p = {n: _held(n, args[n]) for n in WEIGHTS}
    mom = {n: _held(n, args["m_" + n]) for n in WEIGHTS}
    var = {n: _held(n, args["v_" + n]) for n in WEIGHTS}

    keys = [(n, layer) for n in SHARDED for layer in range(p[n].shape[0])]
    shards, places = {}, {}
    for n, layer in keys:
        axis, width = SHARDED[n]
        t = p[n][layer] if n == "sc_conv_w" else p[n][layer].astype(MXU_DTYPE)
        pad = [(0, 0), (0, 0)]
        pad[axis] = (0, width - t.shape[axis])
        shards[(n, layer)], places[(n, layer)] = jnp.pad(t, pad), (axis, width)
    late = _Late(shards, places)
    job, first = late.gather_job("gather_early")
    full = dict(zip(first, _exchange_call(job, "gather_early")))

    loss, dx, small, grads = _local_grads(x[0], loss_target[0], p, full, late)
    loss = lax.psum(loss[0, 0], ("x", "y", "c"))
    assert set(late.received) == set(keys), "a gradient was left without a carrier"

    alike = {}
    for key in keys:
        alike.setdefault((late.received[key].shape, late.received[key].dtype), []).append(key)
    summed = {}
    for group in alike.values():
        summed.update(zip(group, _chip_sums([late.received[k] for k in group], name=f"chip_sum_{group[0][0]}_x{len(group)}")))
    partial = [summed[key] for key in keys]
    other, small_slots = _swap_and_spread(partial, _pack([small[n] for n in SMALL]))
    out = {}
    for (n, layer), mine, theirs in zip(keys, partial, other):
        out[n] = _adamw_layer(p[n], mom[n], var[n], mine, theirs, layer, out.get(n), name=f"adamw_{n}_{layer}")
    out = {n: [_held(n, t) for t in res] for n, res in out.items()}

    like = [p[n] for n in SMALL]
    results = _adamw_small(_pack(like), small_slots, _pack([mom[n] for n in SMALL]), _pack([var[n] for n in SMALL]))
    for k, packed in enumerate(results):
        for n, t in zip(SMALL, _unpack(packed, like)):
            out.setdefault(n, [None] * 4)[k] = t

    return (loss, dx[None], *[out[n][0] for n in WEIGHTS], *[out[n][1] for n in WEIGHTS],
            *[out[n][2] for n in WEIGHTS], *[out[n][3] for n in WEIGHTS])
```

```python
import functools
import math

import jax
import jax.numpy as jnp
from jax import lax
from jax.experimental import pallas as pl
from jax.experimental.pallas import tpu as pltpu

F32 = jnp.float32
MXU_DTYPE = jnp.bfloat16
WIRE_DTYPE = jnp.bfloat16
MESH_ID = pl.DeviceIdType.MESH

N_CHIPS = 4
FF_SLOT = 768
FF_PAD = N_CHIPS * FF_SLOT
S5_WIDTH = 512
S5_GROUPS = 32
S5_STATE = 64
S5_LANES = S5_GROUPS * S5_STATE
S5_BLOCK = 512
S5_DIAG = S5_LANES // S5_BLOCK
SB_HEADS = 8
SB_DH = 64
SB_SCALE = 0.125
SB_PACK = 2
SB_QUERIES = 1024
SB_KEYS = 256
EPS = 1e-6
ADAM_LR, ADAM_B1, ADAM_B2, ADAM_EPS, ADAM_WD, ADAM_STEP = 0.001, 0.9, 0.999, 1e-08, 0.01, 10
VMEM_LIMIT = 56 * 1024 * 1024

ANY_SPEC = pl.BlockSpec(memory_space=pl.ANY)


def _params(*sem):
    return pltpu.CompilerParams(dimension_semantics=sem or None, vmem_limit_bytes=VMEM_LIMIT)


def _mm(a, b, *, name, ta=False, tb=False, out_dtype=F32, epilogue=None, extras=(), tm=1024, tn=1024, tk=1024, job=None):
    m, k = (a.shape[1], a.shape[0]) if ta else a.shape
    n = b.shape[0] if tb else b.shape[1]
    tm, tn, tk = min(tm, m), min(tn, n), min(tk, k)
    assert m % tm == 0 and n % tn == 0 and k % tk == 0, (name, m, n, k)
    grid = (m // tm, n // tn, k // tk)
    a_spec = pl.BlockSpec((tk, tm), lambda i, j, kk: (kk, i)) if ta else pl.BlockSpec((tm, tk), lambda i, j, kk: (i, kk))
    b_spec = pl.BlockSpec((tn, tk), lambda i, j, kk: (j, kk)) if tb else pl.BlockSpec((tk, tn), lambda i, j, kk: (kk, j))
    nk = grid[2]
    ex_specs = []
    for e in extras:
        if e.shape == (m, n):
            ex_specs.append(pl.BlockSpec((tm, tn), lambda i, j, kk: (i, j)))
        elif e.shape == (1, n):
            ex_specs.append(pl.BlockSpec((1, tn), lambda i, j, kk: (0, j)))
        else:
            assert e.shape == (m, 1), (name, e.shape)
            ex_specs.append(pl.BlockSpec((tm, 1), lambda i, j, kk: (i, 0)))
    dims = (((0 if ta else 1,), (1 if tb else 0,)), ((), ()))
    n_ex = len(extras)

    out_dtypes = list(out_dtype) if isinstance(out_dtype, (list, tuple)) else [out_dtype]
    n_out = len(out_dtypes)

    def body(a_ref, b_ref, *rest):
        ex_refs, o_refs = rest[:n_ex], rest[n_ex:n_ex + n_out]

        def product():
            return lax.dot_general(a_ref[...].astype(MXU_DTYPE), b_ref[...].astype(MXU_DTYPE), dims, preferred_element_type=F32)

        def finish(r):
            if epilogue is not None:
                r = epilogue(r, *[e[...] for e in ex_refs])
            for o_ref, val in zip(o_refs, r if isinstance(r, (tuple, list)) else (r,)):
                o_ref[...] = val.astype(o_ref.dtype)

        if nk == 1:
            finish(product())
            return
        acc_ref, kk = rest[n_ex + n_out], pl.program_id(2)

        @pl.when(kk == 0)
        def _():
            acc_ref[...] = jnp.zeros_like(acc_ref)

        acc_ref[...] += product()

        @pl.when(kk == nk - 1)
        def _():
            finish(acc_ref[...])

    res, got = _carried_call(
        body, name=name, grid=grid,
        in_specs=[a_spec, b_spec, *ex_specs],
        out_specs=[pl.BlockSpec((tm, tn), lambda i, j, kk: (i, j))] * n_out,
        out_shape=[jax.ShapeDtypeStruct((m, n), dt) for dt in out_dtypes],
        scratch_shapes=[pltpu.VMEM((tm, tn), F32)] if nk > 1 else [],
        semantics=("parallel", "parallel", "arbitrary"), operands=(a, b, *extras), job=job)
    res = res if isinstance(out_dtype, (list, tuple)) else res[0]
    return res if job is None else (res, got)


def _row_block(rows, want, tile=8):
    for tl in range(min(want, rows), tile - 1, -1):
        if rows % tl == 0 and tl % tile == 0:
            return tl
    return rows


def _rowmap(fn, ins, in_kinds, outs, *, name, tl):
    rows = next(x.shape[0] for x, kd in zip(ins, in_kinds) if kd == "r")
    tl = _row_block(rows, tl)
    n_in = len(ins)

    def spec(shape, kind):
        if kind == "r":
            return pl.BlockSpec((tl,) + tuple(shape[1:]), lambda i: (i,) + (0,) * (len(shape) - 1))
        return pl.BlockSpec(tuple(shape), lambda i: (0,) * len(shape))

    def body(*refs):
        in_refs, out_refs = refs[:n_in], refs[n_in:]
        res = fn(*[r[...] for r in in_refs])
        if not isinstance(res, (tuple, list)):
            res = (res,)
        for o_ref, val, (_, dt, kind) in zip(out_refs, res, outs):
            if kind == "r":
                o_ref[...] = val.astype(dt)
            else:
                @pl.when(pl.program_id(0) == 0)
                def _():
                    o_ref[...] = jnp.zeros_like(o_ref)

                o_ref[...] += val.astype(dt)

    has_acc = any(kd == "a" for _, _, kd in outs)
    res = pl.pallas_call(
        body, name=name, grid=(rows // tl,),
        in_specs=[spec(x.shape, kd) for x, kd in zip(ins, in_kinds)],
        out_specs=[spec(s, kd) for s, _, kd in outs],
        out_shape=[jax.ShapeDtypeStruct(s, dt) for s, dt, _ in outs],
        compiler_params=_params("arbitrary" if has_acc else "parallel"),
    )(*ins)
    return res[0] if len(outs) == 1 else res


def _rms_fwd(x, g):
    r = lax.rsqrt(jnp.mean(x * x, axis=-1, keepdims=True) + EPS)
    return x * r * g


def _rms_bwd(dh, x, g):
    r = lax.rsqrt(jnp.mean(x * x, axis=-1, keepdims=True) + EPS)
    xh = x * r
    dxh = dh * g
    dx = r * (dxh - xh * jnp.mean(dxh * xh, axis=-1, keepdims=True))
    return dx, jnp.sum(dh * xh, axis=0, keepdims=True)


def _swiglu_act(a, b):
    return jax.nn.silu(a) * b


def _ffn_up(x, g, wg, wu, *, name, tm=1024, tn=1024, job=None):
    m, d = x.shape
    n = wg.shape[0]
    tm, tn = min(tm, m), min(tn, n)
    assert m % tm == 0 and n % tn == 0, (name, m, n)

    def body(x_ref, g_ref, wg_ref, wu_ref, h_ref, a_ref, b_ref, s_ref):
        @pl.when(pl.program_id(1) == 0)
        def _():
            h_ref[...] = _rms_fwd(x_ref[...], g_ref[...]).astype(h_ref.dtype)

        hv = h_ref[...]
        av = lax.dot_general(hv, wg_ref[...], NT_DIMS, preferred_element_type=F32)
        bv = lax.dot_general(hv, wu_ref[...], NT_DIMS, preferred_element_type=F32)
        a_ref[...] = av.astype(a_ref.dtype)
        b_ref[...] = bv.astype(b_ref.dtype)
        s_ref[...] = _swiglu_act(av, bv).astype(s_ref.dtype)

    rows = pl.BlockSpec((tm, d), lambda i, j: (i, 0))
    wgt = pl.BlockSpec((tn, d), lambda i, j: (j, 0))
    tile = pl.BlockSpec((tm, tn), lambda i, j: (i, j))
    return _carried_call(
        body, name=name, grid=(m // tm, n // tn),
        in_specs=[rows, pl.BlockSpec((1, d), lambda i, j: (0, 0)), wgt, wgt],
        out_specs=[rows, tile, tile, tile],
        out_shape=[jax.ShapeDtypeStruct((m, d), MXU_DTYPE)] + [jax.ShapeDtypeStruct((m, n), MXU_DTYPE)] * 3,
        semantics=("parallel", "arbitrary"), operands=(x, g, wg, wu), job=job)


def _norm_proj(x, g, w, *, name, tm=1024, tn=1024, job=None):
    m, d = x.shape
    n = w.shape[1]
    tm, tn = min(tm, m), min(tn, n)
    assert m % tm == 0 and n % tn == 0, (name, m, n)

    def body(x_ref, g_ref, w_ref, h_ref, o_ref):
        @pl.when(pl.program_id(1) == 0)
        def _():
            h_ref[...] = _rms_fwd(x_ref[...], g_ref[...]).astype(h_ref.dtype)

        o_ref[...] = jnp.dot(h_ref[...], w_ref[...], preferred_element_type=F32)

    rows = pl.BlockSpec((tm, d), lambda i, j: (i, 0))
    return _carried_call(
        body, name=name, grid=(m // tm, n // tn),
        in_specs=[rows, pl.BlockSpec((1, d), lambda i, j: (0, 0)), pl.BlockSpec((d, tn), lambda i, j: (0, j))],
        out_specs=[rows, pl.BlockSpec((tm, tn), lambda i, j: (i, j))],
        out_shape=[jax.ShapeDtypeStruct((m, d), MXU_DTYPE), jax.ShapeDtypeStruct((m, n), F32)],
        semantics=("parallel", "arbitrary"), operands=(x, g, w), job=job)


def _proj_norm_bwd(pairs, x, g, dres, *, name, tm=1024, tk=1024, job=None):
    m, f = pairs[0][0].shape
    d = x.shape[1]
    tm, tk = min(tm, m), min(tk, f)
    assert m % tm == 0 and f % tk == 0, (name, m, f)
    nk, n_pairs = f // tk, len(pairs)
    swapped = [kept == "FD" for _, _, kept in pairs]

    def body(*refs):
        dy_refs, w_refs = refs[:n_pairs], refs[n_pairs:2 * n_pairs]
        x_ref, g_ref, dr_ref, dx_ref, dg_ref = refs[2 * n_pairs:2 * n_pairs + 5]
        i, kk = pl.program_id(0), pl.program_id(1)

        part = None
        for dy_ref, w_ref, rows_are_f in zip(dy_refs, w_refs, swapped):
            dims = (((1,), (0,)), ((), ())) if rows_are_f else NT_DIMS
            term = lax.dot_general(dy_ref[...].astype(MXU_DTYPE), w_ref[...], dims, preferred_element_type=F32)
            part = term if part is None else part + term

        @pl.when(jnp.logical_and(i == 0, kk == 0))
        def _():
            dg_ref[...] = jnp.zeros_like(dg_ref)

        def finish(dh):
            dx, dg = _rms_bwd(dh, x_ref[...], g_ref[...])
            dx_ref[...] = dx + dr_ref[...]
            dg_ref[...] += dg

        if nk == 1:
            finish(part)
            return
        acc_ref = refs[2 * n_pairs + 5]

        @pl.when(kk == 0)
        def _():
            acc_ref[...] = jnp.zeros_like(acc_ref)

        acc_ref[...] += part

        @pl.when(kk == nk - 1)
        def _():
            finish(acc_ref[...])

    act = pl.BlockSpec((tm, tk), lambda i, kk: (i, kk))
    w_specs = [pl.BlockSpec((tk, d), lambda i, kk: (kk, 0)) if s else pl.BlockSpec((d, tk), lambda i, kk: (0, kk)) for s in swapped]
    rows = pl.BlockSpec((tm, d), lambda i, kk: (i, 0))
    one = pl.BlockSpec((1, d), lambda i, kk: (0, 0))
    return _carried_call(
        body, name=name, grid=(m // tm, nk),
        in_specs=[act] * n_pairs + w_specs + [rows, one, rows],
        out_specs=[rows, one],
        out_shape=[jax.ShapeDtypeStruct((m, d), F32), jax.ShapeDtypeStruct((1, d), F32)],
        scratch_shapes=[pltpu.VMEM((tm, d), F32)] if nk > 1 else [],
        semantics=("arbitrary", "arbitrary"), operands=(*[p[0] for p in pairs], *[p[1] for p in pairs], x, g, dres), job=job)


def _ffn_dx(da, db, wg, wu, x, g, dres, *, name, job=None):
    return _proj_norm_bwd([(da, wg, "FD"), (db, wu, "FD")], x, g, dres, name=name, tm=512, tk=FF_PAD, job=job)


def _ffn_fwd(x, g, full, which, layer, late):
    tag = f"{which}_{layer}"
    job, keys = late.gather_job(f"{tag}_up") if late else (None, [])
    (h, a, b, s), got = _ffn_up(x, g, full[(f"{which}_w_gate", layer)], full[(f"{which}_w_up", layer)], name=f"{tag}_up", job=job)
    full.update(zip(keys, got))
    job, keys = late.gather_job(f"{tag}_down") if late else (None, [])
    x2, got = _carried(_mm, s, full[(f"{which}_w_down", layer)], name=f"{tag}_down", epilogue=lambda acc, xv: xv + 0.5 * acc,
                       extras=[x], tk=FF_PAD, job=job)
    full.update(zip(keys, got))
    return x2, (x, h, a, b, s)


def _ffn_bwd(dx2, saved, g, full, which, layer, grads, late, inline):
    x, h, a, b, s = saved
    tag = f"{which}_{layer}"
    kg, ku, kd = [(f"{which}_w_{n}", layer) for n in ("gate", "up", "down")]
    wg, wu, wd = full[kg], full[ku], full[kd]
    send = (lambda only: late.scatter_job(grads, only)) if (late and inline) else (lambda only: (None, []))

    def act_bwd(ds, av, bv):
        a32, b32, half = av.astype(F32), bv.astype(F32), 0.5 * ds
        sig = jax.nn.sigmoid(a32)
        return half * b32 * (sig * (1.0 + a32 * (1.0 - sig))), half * (a32 * sig)

    grads[kd] = _mm(s, dx2, ta=True, name=f"{tag}_dwd", out_dtype=WIRE_DTYPE, epilogue=lambda acc: 0.5 * acc, tk=2048)
    job, keys = send([kd])
    (da, db), got = _carried(_mm, dx2, wd, tb=True, name=f"{tag}_dact", epilogue=act_bwd, extras=[a, b],
                             out_dtype=[MXU_DTYPE, MXU_DTYPE], job=job)
    _note(late, keys, got)
    grads[kg] = _mm(da, h, ta=True, name=f"{tag}_dwg", out_dtype=WIRE_DTYPE, tk=4096)
    job, keys = send([kg]) if inline == "chain" else (None, [])
    grads[ku], got = _carried(_mm, db, h, ta=True, name=f"{tag}_dwu", out_dtype=WIRE_DTYPE, tk=4096, job=job)
    _note(late, keys, got)
    job, keys = send([ku] if inline == "chain" else [kg])
    (dx, dg), got = _ffn_dx(da, db, wg, wu, x, g, dx2, name=f"{tag}_dx", job=job)
    _note(late, keys, got)
    return dx, dg


def _carried(fn, *args, job, **kwargs):
    return fn(*args, job=job, **kwargs) if job is not None else (fn(*args, **kwargs), [])


def _note(late, keys, got):
    if late:
        late.received.update(zip(keys, got))


def _softplus(z):
    return jnp.maximum(z, 0.0) + jnp.log(1.0 + jnp.exp(-jnp.abs(z)))


def _ones_dot(x, tri):
    if MXU_DTYPE == F32:
        return jnp.dot(x, tri, preferred_element_type=F32)
    hi = x.astype(MXU_DTYPE)
    lo = (x - hi.astype(F32)).astype(MXU_DTYPE)
    return jnp.dot(hi, tri, preferred_element_type=F32) + jnp.dot(lo, tri, preferred_element_type=F32)


NT_DIMS = (((1,), (1,)), ((), ()))
TN_DIMS = (((0,), (0,)), ((), ()))


SB_LANES = SB_PACK * SB_DH
Q_COL, K_COL, V_COL = (S5_WIDTH * n // SB_LANES for n in (1, 2, 3))


def _head_lanes(rows, hd):
    return lax.broadcasted_iota(jnp.int32, (rows, SB_LANES), 1) // SB_DH == hd


def _attend(proj, *, tq=SB_QUERIES, job=None):
    seq = proj.shape[0]
    tq = min(tq, seq)
    tk = min(SB_KEYS, tq)
    per, hp = tq // tk, SB_PACK

    def body(q_ref, k_ref, v_ref, o_ref, ls_ref):
        i = pl.program_id(1)
        r_idx = lax.broadcasted_iota(jnp.int32, (tk, tk), 0)
        c_idx = lax.broadcasted_iota(jnp.int32, (tk, tk), 1)
        after = (r_idx > c_idx).astype(MXU_DTYPE)
        lanes = [_head_lanes(tk, hd) for hd in range(hp)]

        def block(j, cs, acc, straddles):
            off = pl.multiple_of(j * tk, tk)
            k2, v2 = k_ref[pl.ds(off, tk), :], v_ref[pl.ds(off, tk), :]
            top = 0 if straddles is None else straddles * tk
            rows = tq - top
            q2 = (q_ref[pl.ds(top, rows), :] * SB_SCALE).astype(MXU_DTYPE)
            new_cs, out = [], acc[top:]
            for hd in range(hp):
                kv = jnp.where(lanes[hd], k2, 0.0).astype(MXU_DTYPE)
                vv = jnp.where(lanes[hd], v2, 0.0).astype(MXU_DTYPE)
                z = lax.dot_general(q2, kv, NT_DIMS, preferred_element_type=F32)
                sp = _softplus(z)
                c_in = cs[hd][top:]
                if straddles is None:
                    lk = -sp
                    w = jnp.exp(z - sp + _ones_dot(lk, after) + c_in)
                else:
                    before = lax.broadcasted_iota(jnp.int32, (rows, tk), 1) < lax.broadcasted_iota(jnp.int32, (rows, tk), 0)
                    lk = jnp.where(before, -sp, 0.0)
                    w = jnp.where(before, jnp.exp(z - sp + _ones_dot(lk, after) + c_in), 0.0)
                out = out + jnp.dot(w.astype(MXU_DTYPE), vv, preferred_element_type=F32)
                c_new = c_in + jnp.sum(lk, axis=1, keepdims=True)
                new_cs.append(jnp.concatenate([cs[hd][:top], c_new], axis=0) if top else c_new)
            return tuple(new_cs), (jnp.concatenate([acc[:top], out], axis=0) if top else out)

        carry = (tuple(jnp.zeros((tq, 1), F32) for _ in range(hp)), jnp.zeros((tq, SB_LANES), F32))
        for s in reversed(range(per)):
            carry = block(i * per + s, *carry, s)
        cs, acc = lax.fori_loop(0, i * per, lambda n, cr: block(i * per - 1 - n, *cr, None), carry)
        o_ref[...] = acc
        for hd in range(hp):
            ls_ref[hd] = cs[hd]

    whole = lambda col: pl.BlockSpec((seq, SB_LANES), lambda g, i: (0, col + g))
    return _carried_call(
        body, name="sb_fwd", grid=(SB_HEADS // hp, seq // tq),
        in_specs=[pl.BlockSpec((tq, SB_LANES), lambda g, i: (i, Q_COL + g)), whole(K_COL), whole(V_COL)],
        out_specs=[pl.BlockSpec((tq, SB_LANES), lambda g, i: (i, g)), pl.BlockSpec((hp, tq, 1), lambda g, i: (g, i, 0))],
        out_shape=[jax.ShapeDtypeStruct((seq, SB_HEADS * SB_DH), F32), jax.ShapeDtypeStruct((SB_HEADS, seq, 1), F32)],
        semantics=("parallel", "parallel"), operands=(proj, proj, proj), job=job)


def _attend_bwd(proj, lsum, dmix, *, tq=SB_QUERIES, job=None):
    seq = proj.shape[0]
    tq = min(tq, seq)
    tk = min(SB_KEYS, tq)
    per, hp = tq // tk, SB_PACK
    do_col = S5_WIDTH // SB_LANES

    def body(q_ref, k_ref, v_ref, ls_ref, do_ref, dq_ref, dk_ref, dv_ref):
        i = pl.program_id(1)

        @pl.when(i == 0)
        def _():
            dk_ref[...] = jnp.zeros_like(dk_ref)
            dv_ref[...] = jnp.zeros_like(dv_ref)

        r_idx = lax.broadcasted_iota(jnp.int32, (tk, tk), 0)
        c_idx = lax.broadcasted_iota(jnp.int32, (tk, tk), 1)
        upto = (r_idx <= c_idx).astype(MXU_DTYPE)
        before = (r_idx < c_idx).astype(MXU_DTYPE)
        lanes = [_head_lanes(tk, hd) for hd in range(hp)]

        def block(j, sums, dq, straddles):
            off = pl.multiple_of(j * tk, tk)
            k2, v2 = k_ref[pl.ds(off, tk), :], v_ref[pl.ds(off, tk), :]
            top = 0 if straddles is None else straddles * tk
            rows = tq - top
            part = pl.ds(top, rows)
            q2 = (q_ref[part, :] * SB_SCALE).astype(MXU_DTYPE)
            do2 = do_ref[part, :].astype(MXU_DTYPE)
            valid = None
            if straddles is not None:
                valid = lax.broadcasted_iota(jnp.int32, (rows, tk), 1) < lax.broadcasted_iota(jnp.int32, (rows, tk), 0)
            new_sums, out, dk, dv = [], dq[top:], jnp.zeros((tk, SB_LANES), F32), jnp.zeros((tk, SB_LANES), F32)
            for hd in range(hp):
                cp, ce = sums[hd]
                kv = jnp.where(lanes[hd], k2, 0.0).astype(MXU_DTYPE)
                vv = jnp.where(lanes[hd], v2, 0.0).astype(MXU_DTYPE)
                z = lax.dot_general(q2, kv, NT_DIMS, preferred_element_type=F32)
                sp = _softplus(z)
                lk = -sp if valid is None else jnp.where(valid, -sp, 0.0)
                w = jnp.exp(z - sp + (ls_ref[hd, part, :] - cp[top:]) - _ones_dot(lk, upto))
                if valid is not None:
                    w = jnp.where(valid, w, 0.0)
                e = w * lax.dot_general(do2, vv, NT_DIMS, preferred_element_type=F32)
                earlier = jnp.dot(e.astype(MXU_DTYPE), before, preferred_element_type=F32) + ce[top:]
                keep = jnp.exp(-sp)
                dz = e * keep - (1.0 - keep) * earlier
                if valid is not None:
                    dz = jnp.where(valid, dz, 0.0)
                dzm = dz.astype(MXU_DTYPE)
                out = out + jnp.dot(dzm, kv, preferred_element_type=F32)
                dk = dk + jnp.where(lanes[hd], lax.dot_general(dzm, q2, TN_DIMS, preferred_element_type=F32), 0.0)
                dv = dv + jnp.where(lanes[hd], lax.dot_general(w.astype(MXU_DTYPE), do2, TN_DIMS, preferred_element_type=F32), 0.0)
                new = (cp[top:] + jnp.sum(lk, axis=1, keepdims=True), ce[top:] + jnp.sum(e, axis=1, keepdims=True))
                new_sums.append(tuple(jnp.concatenate([old[:top], val], axis=0) for old, val in zip((cp, ce), new)) if top else new)
            dk_ref[pl.ds(off, tk), :] += dk
            dv_ref[pl.ds(off, tk), :] += dv
            return tuple(new_sums), (jnp.concatenate([dq[:top], out], axis=0) if top else out)

        zero = jnp.zeros((tq, 1), F32)
        carry = (tuple((zero, zero) for _ in range(hp)), jnp.zeros((tq, SB_LANES), F32))
        carry = lax.fori_loop(0, i * per, lambda j, cr: block(j, *cr, None), carry)
        for s in range(per):
            carry = block(i * per + s, *carry, s)
        dq_ref[...] = carry[1] * SB_SCALE

    whole = lambda col: pl.BlockSpec((seq, SB_LANES), lambda g, i: (0, col + g))
    tile = lambda col: pl.BlockSpec((tq, SB_LANES), lambda g, i: (i, col + g))
    acc = pl.BlockSpec((seq, SB_LANES), lambda g, i: (0, g))
    return _carried_call(
        body, name="sb_bwd", grid=(SB_HEADS // hp, seq // tq),
        in_specs=[tile(Q_COL), whole(K_COL), whole(V_COL), pl.BlockSpec((hp, tq, 1), lambda g, i: (g, i, 0)), tile(do_col)],
        out_specs=[tile(0), acc, acc],
        out_shape=[jax.ShapeDtypeStruct((seq, SB_HEADS * SB_DH), F32)] * 3,
        semantics=("parallel", "arbitrary"), operands=(proj, proj, proj, lsum, dmix), job=job)


def _s5_disc(lr, li, ldt, br, bi):
    dt = jnp.exp(ldt)
    mag = jnp.exp(lr * dt)
    ar = mag * jnp.cos(li * dt)
    ai = mag * jnp.sin(li * dt)
    den = lr * lr + li * li
    nr = ar - 1.0
    cr = (nr * lr + ai * li) / den
    ci = (ai * lr - nr * li) / den
    return ar, ai, cr[None] * br - ci[None] * bi, cr[None] * bi + ci[None] * br


def _s5_prep(lr, li, ldt, br, bi):
    shapes = [lr.shape, lr.shape, br.shape, br.shape]

    def body(lr_ref, li_ref, ldt_ref, br_ref, bi_ref, *outs):
        for o, val in zip(outs, _s5_disc(lr_ref[...], li_ref[...], ldt_ref[...], br_ref[...], bi_ref[...])):
            o[...] = val

    return pl.pallas_call(body, name="s5_prep", out_shape=[jax.ShapeDtypeStruct(s, F32) for s in shapes])(lr, li, ldt, br, bi)


def _s5_prep_bwd(lr, li, ldt, br, bi, cts):
    args = (lr, li, ldt, br, bi)

    def body(*refs):
        ins, ct_refs, outs = refs[:5], refs[5:9], refs[9:]
        _, vjp = jax.vjp(_s5_disc, *[r[...] for r in ins])
        for o, val in zip(outs, vjp(tuple(r[...] for r in ct_refs))):
            o[...] = val

    return pl.pallas_call(body, name="s5_prep_bwd", out_shape=[jax.ShapeDtypeStruct(a.shape, F32) for a in args])(*args, *cts)


SCAN_ROWS = 8


def _powers(ar, ai):
    out = [(ar, ai)]
    for _ in range(SCAN_ROWS - 1):
        pr, pi = out[-1]
        out.append((pr * ar - pi * ai, pr * ai + pi * ar))
    return out


def _s5_states(u, bmat, cmat, a, d, *, tc=512):
    seq, width = u.shape
    nj, cols, w2 = bmat.shape
    tw = w2 // 2
    tc = min(tc, seq)
    assert seq % tc == 0 and nj * cols == width and tw == S5_BLOCK

    def body(u_ref, bm_ref, cm_ref, a_ref, d_ref, h_ref, y_ref, cr_ref, ci_ref):
        @pl.when(pl.program_id(1) == 0)
        def _():
            cr_ref[...] = jnp.zeros_like(cr_ref)
            ci_ref[...] = jnp.zeros_like(ci_ref)

        uv = u_ref[...]
        h_ref[...] = jnp.dot(uv.astype(MXU_DTYPE), bm_ref[0], preferred_element_type=F32)
        re, im = pl.ds(0, tw), pl.ds(tw, tw)
        powers = _powers(a_ref[:, re], a_ref[:, im])
        pr = jnp.concatenate([p[0] for p in powers], axis=0)
        pi = jnp.concatenate([p[1] for p in powers], axis=0)
        row_id = lax.broadcasted_iota(jnp.int32, (SCAN_ROWS, tw), 0)
        reach = {dist: tuple(jnp.where(row_id >= dist, part, 0.0) for part in powers[dist - 1]) for dist in (1, 2, 4)}

        def block(n, carry):
            hr, hi = carry
            rows = pl.ds(pl.multiple_of(n * SCAN_ROWS, SCAN_ROWS), SCAN_ROWS)
            yr, yi = h_ref[rows, re], h_ref[rows, im]
            for dist in (1, 2, 4):
                cr, ci = reach[dist]
                sr, si = pltpu.roll(yr, dist, 0), pltpu.roll(yi, dist, 0)
                yr, yi = yr + cr * sr - ci * si, yi + cr * si + ci * sr
            yr, yi = yr + pr * hr - pi * hi, yi + pr * hi + pi * hr
            h_ref[rows, re] = yr
            h_ref[rows, im] = yi
            return yr[SCAN_ROWS - 1:], yi[SCAN_ROWS - 1:]

        hr, hi = lax.fori_loop(0, tc // SCAN_ROWS, block, (cr_ref[...], ci_ref[...]), unroll=4)
        cr_ref[...] = hr
        ci_ref[...] = hi
        y_ref[...] = jnp.dot(h_ref[...].astype(MXU_DTYPE), cm_ref[0], preferred_element_type=F32) + d_ref[...] * uv

    io = pl.BlockSpec((tc, cols), lambda j, t: (t, j))
    return pl.pallas_call(
        body, name="s5_states", grid=(nj, seq // tc),
        in_specs=[io, pl.BlockSpec((1, cols, w2), lambda j, t: (j, 0, 0)), pl.BlockSpec((1, w2, cols), lambda j, t: (j, 0, 0)),
                  pl.BlockSpec((1, w2), lambda j, t: (0, j)), pl.BlockSpec((1, cols), lambda j, t: (0, j))],
        out_specs=[pl.BlockSpec((tc, w2), lambda j, t: (t, j)), io],
        out_shape=[jax.ShapeDtypeStruct((seq, nj * w2), F32), jax.ShapeDtypeStruct((seq, width), F32)],
        scratch_shapes=[pltpu.VMEM((1, tw), F32)] * 2,
        compiler_params=_params("parallel", "arbitrary"),
    )(u, bmat, cmat, a, d)


def _s5_states_bwd(dy, h, u, bmat, cmat, a, du_skip, *, tc=512):
    seq, width = u.shape
    nj, cols, w2 = bmat.shape
    tw = w2 // 2
    tc = min(tc, seq)
    assert seq % tc == 0
    nt = seq // tc

    def body(dy_ref, h_ref, u_ref, bm_ref, cm_ref, a_ref, sk_ref, du_ref, da_ref, db_ref, dc_ref, g_ref, cr_ref, ci_ref):
        @pl.when(pl.program_id(1) == 0)
        def _():
            cr_ref[...] = jnp.zeros_like(cr_ref)
            ci_ref[...] = jnp.zeros_like(ci_ref)
            da_ref[...] = jnp.zeros_like(da_ref)
            db_ref[...] = jnp.zeros_like(db_ref)
            dc_ref[...] = jnp.zeros_like(dc_ref)

        dyv = dy_ref[...].astype(MXU_DTYPE)
        g_ref[...] = lax.dot_general(dyv, cm_ref[0], NT_DIMS, preferred_element_type=F32)
        re, im = pl.ds(0, tw), pl.ds(tw, tw)
        powers = _powers(a_ref[:, re], a_ref[:, im])
        pr = jnp.concatenate([p[0] for p in reversed(powers)], axis=0)
        pi = jnp.concatenate([p[1] for p in reversed(powers)], axis=0)
        row_id = lax.broadcasted_iota(jnp.int32, (SCAN_ROWS, tw), 0)
        last = SCAN_ROWS - 1
        reach = {dist: tuple(jnp.where(row_id < SCAN_ROWS - dist, part, 0.0) for part in powers[dist - 1]) for dist in (1, 2, 4)}

        def block(n, carry):
            gr, gi, sr, si = carry
            rows = pl.ds(pl.multiple_of((tc // SCAN_ROWS - 1 - n) * SCAN_ROWS, SCAN_ROWS), SCAN_ROWS)
            yr, yi = g_ref[rows, re], g_ref[rows, im]
            for dist in (1, 2, 4):
                cr, ci = reach[dist]
                ur, ui = pltpu.roll(yr, SCAN_ROWS - dist, 0), pltpu.roll(yi, SCAN_ROWS - dist, 0)
                yr, yi = yr + cr * ur + ci * ui, yi + cr * ui - ci * ur
            yr, yi = yr + pr * gr + pi * gi, yi + pr * gi - pi * gr
            g_ref[rows, re] = yr
            g_ref[rows, im] = yi
            nr = jnp.where(row_id < last, pltpu.roll(yr, last, 0), gr)
            ni = jnp.where(row_id < last, pltpu.roll(yi, last, 0), gi)
            hr, hi = h_ref[rows, re], h_ref[rows, im]
            return yr[:1], yi[:1], sr + nr * hr + ni * hi, si + ni * hr - nr * hi

        zero = jnp.zeros((SCAN_ROWS, tw), F32)
        gr, gi, sr, si = lax.fori_loop(0, tc // SCAN_ROWS, block, (cr_ref[...], ci_ref[...], zero, zero), unroll=4)
        cr_ref[...] = gr
        ci_ref[...] = gi
        da_ref[:, re] += jnp.sum(sr, axis=0, keepdims=True)
        da_ref[:, im] += jnp.sum(si, axis=0, keepdims=True)
        gv = g_ref[...].astype(MXU_DTYPE)
        du_ref[...] = (lax.dot_general(gv, bm_ref[0], NT_DIMS, preferred_element_type=F32) + sk_ref[...]).astype(du_ref.dtype)
        db_ref[0] += lax.dot_general(u_ref[...].astype(MXU_DTYPE), gv, TN_DIMS, preferred_element_type=F32)
        dc_ref[0] += lax.dot_general(h_ref[...].astype(MXU_DTYPE), dyv, TN_DIMS, preferred_element_type=F32)

    io = pl.BlockSpec((tc, cols), lambda j, t: (nt - 1 - t, j))
    bm = pl.BlockSpec((1, cols, w2), lambda j, t: (j, 0, 0))
    cm = pl.BlockSpec((1, w2, cols), lambda j, t: (j, 0, 0))
    row = pl.BlockSpec((1, w2), lambda j, t: (0, j))
    return pl.pallas_call(
        body, name="s5_states_bwd", grid=(nj, nt),
        in_specs=[io, pl.BlockSpec((tc, w2), lambda j, t: (nt - 1 - t, j)), io, bm, cm, row, io],
        out_specs=[io, row, bm, cm],
        out_shape=[jax.ShapeDtypeStruct((seq, width), MXU_DTYPE), jax.ShapeDtypeStruct((1, nj * w2), F32),
                   jax.ShapeDtypeStruct(bmat.shape, F32), jax.ShapeDtypeStruct(cmat.shape, F32)],
        scratch_shapes=[pltpu.VMEM((tc, w2), F32)] + [pltpu.VMEM((1, tw), F32)] * 2,
        compiler_params=_params("parallel", "arbitrary"),
    )(dy, h, u, bmat, cmat, a, du_skip)


def _pair_columns(re, im, axis):
    shape = re.shape
    split = shape[:axis] + (shape[axis] // S5_BLOCK, S5_BLOCK) + shape[axis + 1:]
    both = jnp.stack([re.reshape(split), im.reshape(split)], axis=axis + 1)
    return both.reshape(shape[:axis] + (2 * shape[axis],) + shape[axis + 1:])


def _unpair_columns(t, axis):
    shape = t.shape
    both = t.reshape(shape[:axis] + (shape[axis] // (2 * S5_BLOCK), 2, S5_BLOCK) + shape[axis + 1:])
    half = shape[:axis] + (shape[axis] // 2,) + shape[axis + 1:]
    return (lax.index_in_dim(both, 0, axis + 1, keepdims=False).reshape(half),
            lax.index_in_dim(both, 1, axis + 1, keepdims=False).reshape(half))


S5_PER_BLOCK = S5_GROUPS // S5_DIAG


def _block_diag(t):
    g, a, b = t.shape
    n = S5_PER_BLOCK
    eye = jnp.eye(n, dtype=t.dtype)
    return (t.reshape(g // n, n, a, 1, b) * eye[None, :, None, :, None]).reshape(g // n, n * a, n * b)


def _block_diag_part(m):
    j, n = m.shape[0], S5_PER_BLOCK
    a, b = m.shape[1] // n, m.shape[2] // n
    return jnp.moveaxis(jnp.diagonal(m.reshape(j, n, a, n, b), axis1=1, axis2=3), -1, 1).reshape(j * n, a, b)


def _gelu_glu(y, gate_pre):
    z = jax.nn.gelu(y)
    return z * jax.nn.sigmoid(gate_pre)


def _s5_fwd(u, p, w_glu):
    lr, li = p["s5_lambda_re"][0], p["s5_lambda_im"][0]
    ldt = p["s5_log_dt"][0][:, None]
    br = p["s5_b_re"][0].transpose(2, 0, 1)
    bi = p["s5_b_im"][0].transpose(2, 0, 1)
    ar, ai, bbr, bbi = _s5_prep(lr, li, ldt, br, bi)
    a = _pair_columns(ar.reshape(1, S5_LANES), ai.reshape(1, S5_LANES), 1)
    bmat = jnp.concatenate([_block_diag(bbr.transpose(1, 0, 2)), _block_diag(bbi.transpose(1, 0, 2))], axis=2)
    cmat = jnp.concatenate([_block_diag(p["s5_c_re"][0].transpose(0, 2, 1)),
                            -_block_diag(p["s5_c_im"][0].transpose(0, 2, 1))], axis=1)
    bmat, cmat = bmat.astype(MXU_DTYPE), cmat.astype(MXU_DTYPE)
    d = p["s5_d"]
    h, y = _s5_states(u, bmat, cmat, a, d)
    z = _rowmap(jax.nn.gelu, [y], "r", [(y.shape, MXU_DTYPE, "r")], name="s5_gelu", tl=512)
    gate_pre = _mm(z, w_glu, name="s5_glu")
    out = _rowmap(_gelu_glu, [y, gate_pre], "rr", [(y.shape, F32, "r")], name="s5_gate", tl=512)
    return out, (u, lr, li, ldt, br, bi, a, bmat, cmat, h, y, z, gate_pre)


def _s5_bwd(dout, saved, p, w_glu):
    u, lr, li, ldt, br, bi, a, bmat, cmat, h, y, z, gate_pre = saved
    d = p["s5_d"]

    def gate_bwd(dov, yv, gv):
        zv = jax.nn.gelu(yv)
        sg = jax.nn.sigmoid(gv)
        return dov * sg, dov * zv * sg * (1.0 - sg)

    dz_direct, dgate = _rowmap(gate_bwd, [dout, y, gate_pre], "rrr", [(y.shape, F32, "r"), (y.shape, MXU_DTYPE, "r")],
                               name="s5_gate_bwd", tl=512)
    dw_glu = _mm(z, dgate, ta=True, name="s5_dwglu", out_dtype=WIRE_DTYPE)
    dz = _mm(dgate, w_glu, tb=True, name="s5_dz", epilogue=lambda acc, prev: acc + prev, extras=[dz_direct])

    def gelu_bwd(dzv, yv, uv, dvv):
        _, vjp = jax.vjp(jax.nn.gelu, yv)
        dy = vjp(dzv)[0]
        return dy, dy * dvv, jnp.sum(dy * uv, axis=0, keepdims=True)

    dy, du_skip, dd = _rowmap(gelu_bwd, [dz, y, u, d], "rrrc",
                              [(y.shape, F32, "r"), (y.shape, F32, "r"), (d.shape, F32, "a")], name="s5_gelu_bwd", tl=512)
    du, da, dbmat, dcmat = _s5_states_bwd(dy, h, u, bmat, cmat, a, du_skip)
    dbbr, dbbi = (_block_diag_part(t).transpose(1, 0, 2) for t in (dbmat[:, :, :S5_BLOCK], dbmat[:, :, S5_BLOCK:]))
    dar, dai = _unpair_columns(da, 1)
    cts = (dar.reshape(S5_GROUPS, S5_STATE), dai.reshape(S5_GROUPS, S5_STATE), dbbr, dbbi)
    dlr, dli, dldt, dbr, dbi = _s5_prep_bwd(lr, li, ldt, br, bi, cts)
    dcr, dci = (_block_diag_part(t).transpose(0, 2, 1) for t in (dcmat[:, :S5_BLOCK], dcmat[:, S5_BLOCK:]))
    grads = {
        "s5_lambda_re": dlr[None], "s5_lambda_im": dli[None], "s5_log_dt": dldt[:, 0][None],
        "s5_b_re": dbr.transpose(1, 2, 0)[None], "s5_b_im": dbi.transpose(1, 2, 0)[None],
        "s5_c_re": dcr[None], "s5_c_im": -dci[None], "s5_d": dd,
    }
    return du, dw_glu, grads


def _mix0_fwd(x, g, p, full, late):
    (h, proj), _ = _norm_proj(x, g, full[("ab_w_in", 0)], name="mix0_in")
    u = proj[:, :S5_WIDTH]
    job, keys = late.gather_job("sb_fwd") if late else (None, [])
    (o, lsum), got = _attend(proj, job=job)
    full.update(zip(keys, got))
    w_glu, w_out = full[("s5_w_glu", 0)], full[("ab_w_out", 0)]
    y_a, s5_saved = _s5_fwd(u, p, w_glu)
    mix = jnp.concatenate([y_a, o], axis=1).astype(MXU_DTYPE)
    x2 = _mm(mix, w_out, name="mix0_out", epilogue=lambda acc, xv: xv + acc, extras=[x])
    return x2, (x, h, proj, lsum, mix, s5_saved)


def _mix0_bwd(dx2, saved, g, p, full, grads, late):
    x, h, proj, lsum, mix, s5_saved = saved
    w_in, w_glu, w_out = full[("ab_w_in", 0)], full[("s5_w_glu", 0)], full[("ab_w_out", 0)]
    dmix = _mm(dx2, w_out, tb=True, name="mix0_dmix")
    grads[("ab_w_out", 0)] = _mm(mix, dx2, ta=True, name="mix0_dwout", out_dtype=WIRE_DTYPE)
    du, grads[("s5_w_glu", 0)], s5_grads = _s5_bwd(dmix[:, :S5_WIDTH], s5_saved, p, w_glu)
    job, keys = late.scatter_job(grads) if late else (None, [])
    (dq, dk, dv), got = _attend_bwd(proj, lsum, dmix, job=job)
    _note(late, keys, got)
    dproj = jnp.concatenate([du] + [t.astype(MXU_DTYPE) for t in (dq, dk, dv)], axis=1)
    grads[("ab_w_in", 0)] = _mm(h, dproj, ta=True, name="mix0_dwin", out_dtype=WIRE_DTYPE)
    job, keys = late.scatter_job(grads) if late else (None, [])
    (dx, dg), got = _proj_norm_bwd([(dproj, w_in, "DF")], x, g, dx2, name="mix0_dh", job=job)
    _note(late, keys, got)
    return dx, dg, s5_grads


def _shift_down(t, n):
    rows = lax.broadcasted_iota(jnp.int32, t.shape, 0)
    return jnp.where(rows >= n, pltpu.roll(t, n, 0), 0.0)


def _shift_up(t, n):
    rows = lax.broadcasted_iota(jnp.int32, t.shape, 0)
    return jnp.where(rows < t.shape[0] - n, pltpu.roll(t, t.shape[0] - n, 0), 0.0)


def _conv_fwd(proj, cw, *, tc=128):
    seq, c3 = proj.shape
    ch = c3 // 3
    nb = ch // tc

    def body(b_ref, c_ref, v_ref, w_ref, m_ref):
        pv = c_ref[...] * v_ref[...]
        w = w_ref[...]
        y = w[2:3] * pv + w[1:2] * _shift_down(pv, 1) + w[0:1] * _shift_down(pv, 2)
        m_ref[...] = (b_ref[...] * y).astype(m_ref.dtype)

    col = lambda part: pl.BlockSpec((seq, tc), lambda j: (0, part * nb + j))
    return pl.pallas_call(
        body, name="conv_fwd", grid=(nb,),
        in_specs=[col(0), col(1), col(2), pl.BlockSpec((3, tc), lambda j: (0, j))],
        out_specs=pl.BlockSpec((seq, tc), lambda j: (0, j)),
        out_shape=jax.ShapeDtypeStruct((seq, ch), MXU_DTYPE),
        compiler_params=_params("parallel"),
    )(proj, proj, proj, cw)


def _conv_bwd(proj, cw, dm, *, tc=128):
    seq, c3 = proj.shape
    ch = c3 // 3
    nb = ch // tc

    def body(b_ref, c_ref, v_ref, w_ref, dm_ref, dproj_ref, dw_ref, dc_ref, dv_ref):
        part = pl.program_id(1)

        @pl.when(part == 0)
        def _():
            cv, vv, dmv = c_ref[...], v_ref[...], dm_ref[...]
            pv = cv * vv
            w = w_ref[...]
            p1, p2 = _shift_down(pv, 1), _shift_down(pv, 2)
            y = w[2:3] * pv + w[1:2] * p1 + w[0:1] * p2
            dproj_ref[...] = (dmv * y).astype(dproj_ref.dtype)
            dy = dmv * b_ref[...]
            dp = w[2:3] * dy + w[1:2] * _shift_up(dy, 1) + w[0:1] * _shift_up(dy, 2)
            dc_ref[...] = (dp * vv).astype(dc_ref.dtype)
            dv_ref[...] = (dp * cv).astype(dv_ref.dtype)
            dw_ref[...] = jnp.concatenate([jnp.sum(dy * p2, axis=0, keepdims=True), jnp.sum(dy * p1, axis=0, keepdims=True),
                                           jnp.sum(dy * pv, axis=0, keepdims=True)], axis=0)

        @pl.when(part == 1)
        def _():
            dproj_ref[...] = dc_ref[...]

        @pl.when(part == 2)
        def _():
            dproj_ref[...] = dv_ref[...]

    col = lambda part: pl.BlockSpec((seq, tc), lambda j, t: (0, part * nb + j))
    small = pl.BlockSpec((3, tc), lambda j, t: (0, j))
    return pl.pallas_call(
        body, name="conv_bwd", grid=(nb, 3),
        in_specs=[col(0), col(1), col(2), small, pl.BlockSpec((seq, tc), lambda j, t: (0, j))],
        out_specs=[pl.BlockSpec((seq, tc), lambda j, t: (0, t * nb + j)), small],
        out_shape=[jax.ShapeDtypeStruct((seq, c3), MXU_DTYPE), jax.ShapeDtypeStruct((3, ch), F32)],
        scratch_shapes=[pltpu.VMEM((seq, tc), MXU_DTYPE)] * 2,
        compiler_params=_params("parallel", "arbitrary"),
    )(proj, proj, proj, cw, dm)


def _mix1_fwd(x, g, full, late):
    job, keys = late.gather_job("mix1_in") if late else (None, [])
    (h, proj), got = _norm_proj(x, g, full[("sc_w_in", 0)], name="mix1_in", job=job)
    full.update(zip(keys, got))
    m = _conv_fwd(proj, full[("sc_conv_w", 0)])
    x2 = _mm(m, full[("sc_w_out", 0)], name="mix1_out", epilogue=lambda acc, xv: xv + acc, extras=[x])
    return x2, (x, h, proj, m)


def _mix1_bwd(dx2, saved, g, full, grads, late):
    x, h, proj, m = saved
    w_in, cw, w_out = full[("sc_w_in", 0)], full[("sc_conv_w", 0)], full[("sc_w_out", 0)]
    dm = _mm(dx2, w_out, tb=True, name="mix1_dm")
    grads[("sc_w_out", 0)] = _mm(m, dx2, ta=True, name="mix1_dwout", out_dtype=WIRE_DTYPE)
    dproj, dcw = _conv_bwd(proj, cw, dm)
    grads[("sc_conv_w", 0)] = dcw.astype(WIRE_DTYPE)
    grads[("sc_w_in", 0)] = _mm(h, dproj, ta=True, name="mix1_dwin", out_dtype=WIRE_DTYPE)
    (dx, dg), _ = _proj_norm_bwd([(dproj, w_in, "DF")], x, g, dx2, name="mix1_dh")
    return dx, dg


def _loss_head(x, g, target):
    feat = x.shape[1]

    def fn(xv, gv, tv):
        err = _rms_fwd(xv, gv) - tv
        dx, dg = _rms_bwd(err / feat, xv, gv)
        return jnp.sum(err * err, keepdims=True) * (0.5 / feat), dx, dg

    return _rowmap(fn, [x, g, target], "rcr", [((1, 1), F32, "a"), (x.shape, F32, "r"), (g.shape, F32, "a")],
                   name="loss_head", tl=256)


def _slot(ref, place, chip=None, half=None, piece=(0, 1)):
    axis, width = place
    shape = list(ref.shape)
    start = [0, 0]
    if chip is not None:
        start[axis], shape[axis] = chip * width, width
    if half is not None:
        h_axis = 0 if shape[0] % 32 == 0 else 1
        shape[h_axis] //= 2 * piece[1]
        start[h_axis] = start[h_axis] + (half * piece[1] + piece[0]) * shape[h_axis]
    hint = lambda s, d: s if isinstance(s, int) else pl.multiple_of(s, 128 if d == 1 else 8)
    return ref.at[tuple(pl.ds(hint(s, d), n) for d, (s, n) in enumerate(zip(start, shape)))]


class _Exchange:
    def __init__(self, kind, arrays, places, pieces=1):
        self.kind, self.arrays, self.places, self.n, self.pieces = kind, list(arrays), list(places), len(arrays), pieces
        self.out_shape = []
        for t, (axis, width) in zip(self.arrays, self.places):
            if kind == "gather":
                shape = list(t.shape)
                shape[axis] = N_CHIPS * width
            else:
                shape = [N_CHIPS] + list(t.shape)
                shape[1 + axis] = width
            self.out_shape.append(jax.ShapeDtypeStruct(tuple(shape), t.dtype))
        n = self.n
        self.scratch = [pltpu.SemaphoreType.DMA((3 * n * pieces,)) for _ in range(4 if kind == "gather" else 2)]
        self.scratch.append(pltpu.SemaphoreType.DMA((n,)))

    def _copies(self, ins, outs, sems):
        x, y, c = lax.axis_index("x"), lax.axis_index("y"), lax.axis_index("c")
        peers = [(1 - x, y), (x, 1 - y), (1 - x, 1 - y)]
        remote = lambda src, dst, send, recv, k, to: pltpu.make_async_remote_copy(
            src_ref=src, dst_ref=dst, send_sem=send.at[k], recv_sem=recv.at[k], device_id=to, device_id_type=MESH_ID)
        local, ici, d2d = [], [], []
        for a in range(self.n):
            place = self.places[a]
            if self.kind == "gather":
                local.append(pltpu.make_async_copy(ins[a], _slot(outs[a], place, 2 * x + y), sems[4].at[a]))
                for q in range(self.pieces):
                    for r, (px, py) in enumerate(peers):
                        k, part = (3 * a + r) * self.pieces + q, (q, self.pieces)
                        ici.append(remote(_slot(ins[a], place, None, c, part), _slot(outs[a], place, 2 * x + y, c, part),
                                          sems[0], sems[1], k, (px, py, c)))
                        landed = _slot(outs[a], place, 2 * px + py, c, part)
                        d2d.append(remote(landed, landed, sems[2], sems[3], k, (x, y, 1 - c)))
            else:
                local.append(pltpu.make_async_copy(_slot(ins[a], place, 2 * x + y), outs[a].at[3], sems[2].at[a]))
                for r, (px, py) in enumerate(peers):
                    ici.append(remote(_slot(ins[a], place, 2 * px + py), outs[a].at[r], sems[0], sems[1], 3 * a + r, (px, py, c)))
        return local, ici, d2d

    def start(self, ins, outs, sems):
        local, ici, _ = self._copies(ins, outs, sems)
        for cp in local + ici:
            cp.start()

    def relay(self, ins, outs, sems):
        _, ici, d2d = self._copies(ins, outs, sems)
        for arrived, onward in zip(ici, d2d):
            arrived.wait_recv()
            onward.start()

    def finish(self, ins, outs, sems):
        local, ici, d2d = self._copies(ins, outs, sems)
        for cp in local + d2d:
            cp.wait()
        for cp in ici:
            cp.wait_send() if d2d else cp.wait()


def _exchange_call(job, name):
    n = job.n

    def body(*refs):
        ins, outs, sems = refs[:n], refs[n:2 * n], refs[2 * n:]
        job.start(ins, outs, sems)
        job.relay(ins, outs, sems)
        job.finish(ins, outs, sems)

    return pl.pallas_call(
        body, name=name, in_specs=[ANY_SPEC] * n, out_specs=[ANY_SPEC] * n, out_shape=job.out_shape,
        scratch_shapes=job.scratch, compiler_params=pltpu.CompilerParams(has_side_effects=True),
    )(*job.arrays)


def _carried_call(body, *, name, grid, in_specs, out_specs, out_shape, semantics, operands, scratch_shapes=(), job=None):
    scratch_shapes = list(scratch_shapes)
    if job is None:
        return pl.pallas_call(body, name=name, grid=grid, in_specs=in_specs, out_specs=out_specs, out_shape=out_shape,
                              scratch_shapes=scratch_shapes, compiler_params=_params(*semantics))(*operands), []
    n_in, n_out, n, n_scr = len(in_specs), len(out_specs), job.n, len(scratch_shapes)
    steps = math.prod(grid)

    def wrapped(*refs):
        ins, job_ins = refs[:n_in], refs[n_in:n_in + n]
        outs, job_outs = refs[n_in + n:n_in + n + n_out], refs[n_in + n + n_out:n_in + 2 * n + n_out]
        outs = outs + refs[n_in + 2 * n + n_out:n_in + 2 * n + n_out + n_scr]
        sems = refs[n_in + 2 * n + n_out + n_scr:]
        step = functools.reduce(lambda acc, d: acc * grid[d] + pl.program_id(d), range(len(grid)), 0)

        @pl.when(step == 0)
        def _():
            job.start(job_ins, job_outs, sems)

        @pl.when(step == (3 * steps) // 4)
        def _():
            job.relay(job_ins, job_outs, sems)

        body(*ins, *outs)

        @pl.when(step == steps - 1)
        def _():
            job.finish(job_ins, job_outs, sems)

    res = pl.pallas_call(
        wrapped, name=name, grid=grid, in_specs=list(in_specs) + [ANY_SPEC] * n, out_specs=list(out_specs) + [ANY_SPEC] * n,
        out_shape=list(out_shape) + job.out_shape, scratch_shapes=scratch_shapes + job.scratch,
        compiler_params=pltpu.CompilerParams(dimension_semantics=("arbitrary",) * len(grid), vmem_limit_bytes=VMEM_LIMIT,
                                             has_side_effects=True),
    )(*operands, *job.arrays)
    return res[:n_out], res[n_out:]


def _swap_and_spread(parts, t):
    n = len(parts)

    def body(*refs):
        ins, t_ref, outs, slots = refs[:n], refs[n], refs[n + 1:2 * n + 1], refs[2 * n + 1]
        send, recv, all_send, all_recv, own = refs[2 * n + 2:]
        x, y, c = lax.axis_index("x"), lax.axis_index("y"), lax.axis_index("c")
        mine = slots.at[4 * x + 2 * y + c]
        copies = [pltpu.make_async_copy(t_ref, mine, own)]
        copies += [pltpu.make_async_remote_copy(src_ref=ins[a], dst_ref=outs[a], send_sem=send.at[a], recv_sem=recv.at[a],
                                                device_id=(x, y, 1 - c), device_id_type=MESH_ID) for a in range(n)]
        for m in range(1, 8):
            peer = (x ^ (m >> 2), y ^ ((m >> 1) & 1), c ^ (m & 1))
            copies.append(pltpu.make_async_remote_copy(src_ref=t_ref, dst_ref=mine, send_sem=all_send.at[m - 1],
                                                       recv_sem=all_recv.at[m - 1], device_id=peer, device_id_type=MESH_ID))
        for cp in copies:
            cp.start()
        for cp in copies:
            cp.wait()

    res = pl.pallas_call(
        body, name="swap_and_spread",
        in_specs=[ANY_SPEC] * (n + 1), out_specs=[ANY_SPEC] * (n + 1),
        out_shape=[jax.ShapeDtypeStruct(p.shape, p.dtype) for p in parts] + [jax.ShapeDtypeStruct((8,) + t.shape, t.dtype)],
        scratch_shapes=[pltpu.SemaphoreType.DMA((n,)), pltpu.SemaphoreType.DMA((n,)), pltpu.SemaphoreType.DMA((7,)),
                        pltpu.SemaphoreType.DMA((7,)), pltpu.SemaphoreType.DMA(())],
        compiler_params=pltpu.CompilerParams(has_side_effects=True),
    )(*parts, t)
    return res[:n], res[n]


def _adamw(w, g, m, v):
    m = ADAM_B1 * m + (1.0 - ADAM_B1) * g
    v = ADAM_B2 * v + (1.0 - ADAM_B2) * jnp.square(g)
    m_hat = m / (1.0 - ADAM_B1 ** ADAM_STEP)
    v_hat = v / (1.0 - ADAM_B2 ** ADAM_STEP)
    return -ADAM_LR * (m_hat / (jnp.sqrt(v_hat) + ADAM_EPS) + ADAM_WD * w), m, v


def _chip_sums(group, name):
    rows, cols = group[0].shape[1:]
    count = len(group)
    tl = _row_block(rows, 512 if count == 1 else 128, tile=32 // group[0].dtype.itemsize)

    def body(*refs):
        for r_ref, o_ref in zip(refs[:count], refs[count:]):
            total = ((r_ref[0].astype(F32) + r_ref[1].astype(F32)) + r_ref[2].astype(F32)) + r_ref[3].astype(F32)
            o_ref[...] = total.astype(o_ref.dtype)

    return pl.pallas_call(body, name=name, grid=(rows // tl,),
                          in_specs=[pl.BlockSpec((N_CHIPS, tl, cols), lambda i: (0, i, 0))] * count,
                          out_specs=[pl.BlockSpec((tl, cols), lambda i: (i, 0))] * count,
                          out_shape=[jax.ShapeDtypeStruct((rows, cols), group[0].dtype)] * count,
                          compiler_params=_params("parallel"))(*group)


def _adamw_layer(w, m, v, p_mine, p_other, layer, prev, name):
    _, rows, cols = w.shape
    assert p_mine.shape[1] == cols and p_mine.shape[0] >= rows
    tl = _row_block(rows, 512, tile=32 // p_mine.dtype.itemsize)
    tc = cols
    if tl < 128 < rows and cols % 256 == 0:
        tl, tc = rows, 256

    def body(w_ref, m_ref, v_ref, pa_ref, pb_ref, *rest):
        g = pa_ref[...].astype(F32) + pb_ref[...].astype(F32)
        for o_ref, val in zip(rest[-4:], (g,) + _adamw(w_ref[...], g, m_ref[...], v_ref[...])):
            o_ref[...] = val

    stacked = pl.BlockSpec((None, tl, tc), lambda i, j: (layer, i, j))
    part = pl.BlockSpec((tl, tc), lambda i, j: (i, j))
    kept = list(prev) if prev else []
    return pl.pallas_call(
        body, name=name, grid=(rows // tl, cols // tc),
        in_specs=[stacked] * 3 + [part] * 2 + [ANY_SPEC] * len(kept),
        out_specs=[stacked] * 4, out_shape=[jax.ShapeDtypeStruct(w.shape, F32)] * 4,
        input_output_aliases={5 + k: k for k in range(len(kept))},
        compiler_params=_params("parallel", "parallel"),
    )(w, m, v, p_mine, p_other, *kept)


def _adamw_small(w, slots, m, v):
    def fn(wv, sv, mv, vv):
        g = sv[0]
        for dev in range(1, 8):
            g = g + sv[dev]
        return (g,) + _adamw(wv, g, mv, vv)

    return _rowmap(fn, [w, slots, m, v], "rcrr", [(w.shape, F32, "r")] * 4, name="adamw_small", tl=w.shape[0])


WEIGHTS = ['ffn1_norm', 'ffn1_w_gate', 'ffn1_w_up', 'ffn1_w_down', 'mix_norm', 'ffn2_norm', 'ffn2_w_gate', 'ffn2_w_up',
           'ffn2_w_down', 'ab_w_in', 's5_lambda_re', 's5_lambda_im', 's5_log_dt', 's5_b_re', 's5_b_im', 's5_c_re', 's5_c_im',
           's5_d', 's5_w_glu', 'ab_w_out', 'sc_w_in', 'sc_conv_w', 'sc_w_out', 'final_norm']
SHARDED = {'ffn1_w_gate': (0, FF_SLOT), 'ffn1_w_up': (0, FF_SLOT), 'ffn1_w_down': (0, FF_SLOT),
           'ffn2_w_gate': (0, FF_SLOT), 'ffn2_w_up': (0, FF_SLOT), 'ffn2_w_down': (0, FF_SLOT),
           'ab_w_in': (1, 512), 's5_w_glu': (0, 128), 'ab_w_out': (0, 256), 'sc_w_in': (1, 768), 'sc_conv_w': (1, 256),
           'sc_w_out': (0, 256)}
SWAPPED = ('ffn1_w_gate', 'ffn1_w_up', 'ffn2_w_gate', 'ffn2_w_up')
SMALL = [n for n in WEIGHTS if n not in SHARDED]


def _held(name, t):
    return jnp.swapaxes(t, 1, 2) if name in SWAPPED else t


def _pack(arrays):
    rows = []
    for t in arrays:
        flat = t.reshape(-1)
        rows.append(jnp.pad(flat, (0, (-flat.shape[0]) % 128)))
    flat = jnp.concatenate(rows)
    return jnp.pad(flat, (0, (-flat.shape[0]) % 1024)).reshape(-1, 128)


def _unpack(packed, like):
    flat, out, pos = packed.reshape(-1), [], 0
    for t in like:
        out.append(flat[pos:pos + t.size].reshape(t.shape))
        pos += t.size + (-t.size) % 128
    return out


def _local_grads(x, target, p, full, late=None):
    small, grads, saved = {}, {}, []
    for layer in range(2):
        x, s1 = _ffn_fwd(x, p["ffn1_norm"][layer:layer + 1], full, "ffn1", layer, late)
        if layer == 0:
            x, sm = _mix0_fwd(x, p["mix_norm"][0:1], p, full, late)
        else:
            x, sm = _mix1_fwd(x, p["mix_norm"][1:2], full, late)
        x, s2 = _ffn_fwd(x, p["ffn2_norm"][layer:layer + 1], full, "ffn2", layer, late)
        saved.append((s1, sm, s2))
    loss, dx, dg_final = _loss_head(x, p["final_norm"][None], target)
    small["final_norm"] = dg_final[0]
    gains = {n: [None, None] for n in ("ffn1_norm", "mix_norm", "ffn2_norm")}

    def ffn_bwd(which, layer, dx, s):
        dx, dg = _ffn_bwd(dx, s, p[f"{which}_norm"][layer:layer + 1], full, which, layer, grads, late,
                          inline={("ffn2", 1): "defer", ("ffn1", 0): "chain"}.get((which, layer)))
        gains[f"{which}_norm"][layer] = dg[0]
        return dx

    for layer in (1, 0):
        s1, sm, s2 = saved[layer]
        dx = ffn_bwd("ffn2", layer, dx, s2)
        if layer == 0:
            dx, dg, s5_grads = _mix0_bwd(dx, sm, p["mix_norm"][0:1], p, full, grads, late)
            small.update(s5_grads)
        else:
            dx, dg = _mix1_bwd(dx, sm, p["mix_norm"][1:2], full, grads, late)
        gains["mix_norm"][layer] = dg[0]
        dx = ffn_bwd("ffn1", layer, dx, s1)
    small.update({n: jnp.stack(pair) for n, pair in gains.items()})
    return loss, dx, small, grads


_GATHER_PLAN = {
    "gather_early": [("ffn1_w_gate", 0), ("ffn1_w_up", 0)],
    "ffn1_0_up": [("ffn1_w_down", 0), ("ab_w_in", 0)],
    "sb_fwd": [("s5_w_glu", 0), ("ab_w_out", 0), ("ffn2_w_gate", 0), ("ffn2_w_up", 0), ("ffn2_w_down", 0),
               ("ffn1_w_gate", 1), ("ffn1_w_up", 1), ("ffn1_w_down", 1), ("ffn2_w_gate", 1)],
    "ffn2_0_up": [("sc_w_in", 0), ("sc_conv_w", 0), ("sc_w_out", 0)],
    "ffn1_1_up": [("ffn2_w_up", 1)],
    "ffn2_1_up": [("ffn2_w_down", 1)],
}


class _Late:
    def __init__(self, shards, places):
        self.shards, self.places = shards, places
        self.sent, self.received = set(), {}

    def gather_job(self, carrier):
        keys = _GATHER_PLAN.get(carrier, [])
        if not keys:
            return None, []
        pieces = 4 if carrier == "gather_early" else 1
        return _Exchange("gather", [self.shards[k] for k in keys], [self.places[k] for k in keys], pieces), keys

    def scatter_job(self, grads, only=None):
        keys = [k for k in grads if k not in self.sent and (only is None or k in only)]
        if not keys:
            return None, []
        self.sent.update(keys)
        return _Exchange("scatter", [grads[k] for k in keys], [self.places[k] for k in keys]), keys


def kernel(x, ffn1_norm, ffn1_w_gate, ffn1_w_up, ffn1_w_down, mix_norm, ffn2_norm, ffn2_w_gate, ffn2_w_up, ffn2_w_down, ab_w_in, s5_lambda_re, s5_lambda_im, s5_log_dt, s5_b_re, s5_b_im, s5_c_re, s5_c_im, s5_d, s5_w_glu, ab_w_out, sc_w_in, sc_conv_w, sc_w_out, final_norm, loss_target, m_ffn1_norm, m_ffn1_w_gate, m_ffn1_w_up, m_ffn1_w_down, m_mix_norm, m_ffn2_norm, m_ffn2_w_gate, m_ffn2_w_up, m_ffn2_w_down, m_ab_w_in, m_s5_lambda_re, m_s5_lambda_im, m_s5_log_dt, m_s5_b_re, m_s5_b_im, m_s5_c_re, m_s5_c_im, m_s5_d, m_s5_w_glu, m_ab_w_out, m_sc_w_in, m_sc_conv_w, m_sc_w_out, m_final_norm, v_ffn1_norm, v_ffn1_w_gate, v_ffn1_w_up, v_ffn1_w_down, v_mix_norm, v_ffn2_norm, v_ffn2_w_gate, v_ffn2_w_up, v_ffn2_w_down, v_ab_w_in, v_s5_lambda_re, v_s5_lambda_im, v_s5_log_dt, v_s5_b_re, v_s5_b_im, v_s5_c_re, v_s5_c_im, v_s5_d, v_s5_w_glu, v_ab_w_out, v_sc_w_in, v_sc_conv_w, v_sc_w_out, v_final_norm):
    args = dict(locals())
    p = {n: _held(n, args[n]) for n in WEIGHTS}
    mom = {n: _held(n, args["m_" + n]) for n in WEIGHTS}
    var = {n: _held(n, args["v_" + n]) for n in WEIGHTS}

    keys = [(n, layer) for n in SHARDED for layer in range(p[n].shape[0])]
    shards, places = {}, {}
    for n, layer in keys:
        axis, width = SHARDED[n]
        t = p[n][layer] if n == "sc_conv_w" else p[n][layer].astype(MXU_DTYPE)
        pad = [(0, 0), (0, 0)]
        pad[axis] = (0, width - t.shape[axis])
        shards[(n, layer)], places[(n, layer)] = jnp.pad(t, pad), (axis, width)
    late = _Late(shards, places)
    job, first = late.gather_job("gather_early")
    full = dict(zip(first, _exchange_call(job, "gather_early")))

    loss, dx, small, grads = _local_grads(x[0], loss_target[0], p, full, late)
    loss = lax.psum(loss[0, 0], ("x", "y", "c"))
    assert set(late.received) == set(keys), "a gradient was left without a carrier"

    alike = {}
    for key in keys:
        alike.setdefault((late.received[key].shape, late.received[key].dtype), []).append(key)
    summed = {}
    for group in alike.values():
        summed.update(zip(group, _chip_sums([late.received[k] for k in group], name=f"chip_sum_{group[0][0]}_x{len(group)}")))
    partial = [summed[key] for key in keys]
    other, small_slots = _swap_and_spread(partial, _pack([small[n] for n in SMALL]))
    out = {}
    for (n, layer), mine, theirs in zip(keys, partial, other):
        out[n] = _adamw_layer(p[n], mom[n], var[n], mine, theirs, layer, out.get(n), name=f"adamw_{n}_{layer}")
    out = {n: [_held(n, t) for t in res] for n, res in out.items()}

    like = [p[n] for n in SMALL]
    results = _adamw_small(_pack(like), small_slots, _pack([mom[n] for n in SMALL]), _pack([var[n] for n in SMALL]))
    for k, packed in enumerate(results):
        for n, t in zip(SMALL, _unpack(packed, like)):
            out.setdefault(n, [None] * 4)[k] = t

    return (loss, dx[None], *[out[n][0] for n in WEIGHTS], *[out[n][1] for n in WEIGHTS],
            *[out[n][2] for n in WEIGHTS], *[out[n][3] for n in WEIGHTS])
```

```python
import functools
import math

import jax
import jax.numpy as jnp
from jax import lax
from jax.experimental import pallas as pl
from jax.experimental.pallas import tpu as pltpu

F32 = jnp.float32
MXU_DTYPE = jnp.bfloat16
WIRE_DTYPE = jnp.bfloat16
MESH_ID = pl.DeviceIdType.MESH

N_CHIPS = 4
FF_SLOT = 768
FF_PAD = N_CHIPS * FF_SLOT
S5_WIDTH = 512
S5_GROUPS = 32
S5_STATE = 64
S5_LANES = S5_GROUPS * S5_STATE
S5_BLOCK = 512
S5_DIAG = S5_LANES // S5_BLOCK
SB_HEADS = 8
SB_DH = 64
SB_SCALE = 0.125
SB_PACK = 2
SB_QUERIES = 1024
SB_KEYS = 256
EPS = 1e-6
ADAM_LR, ADAM_B1, ADAM_B2, ADAM_EPS, ADAM_WD, ADAM_STEP = 0.001, 0.9, 0.999, 1e-08, 0.01, 10
VMEM_LIMIT = 56 * 1024 * 1024

ANY_SPEC = pl.BlockSpec(memory_space=pl.ANY)


def _params(*sem):
    return pltpu.CompilerParams(dimension_semantics=sem or None, vmem_limit_bytes=VMEM_LIMIT)


def _mm(a, b, *, name, ta=False, tb=False, out_dtype=F32, epilogue=None, extras=(), tm=1024, tn=1024, tk=1024, job=None):
    m, k = (a.shape[1], a.shape[0]) if ta else a.shape
    n = b.shape[0] if tb else b.shape[1]
    tm, tn, tk = min(tm, m), min(tn, n), min(tk, k)
    assert m % tm == 0 and n % tn == 0 and k % tk == 0, (name, m, n, k)
    grid = (m // tm, n // tn, k // tk)
    a_spec = pl.BlockSpec((tk, tm), lambda i, j, kk: (kk, i)) if ta else pl.BlockSpec((tm, tk), lambda i, j, kk: (i, kk))
    b_spec = pl.BlockSpec((tn, tk), lambda i, j, kk: (j, kk)) if tb else pl.BlockSpec((tk, tn), lambda i, j, kk: (kk, j))
    nk = grid[2]
    ex_specs = []
    for e in extras:
        if e.shape == (m, n):
            ex_specs.append(pl.BlockSpec((tm, tn), lambda i, j, kk: (i, j)))
        elif e.shape == (1, n):
            ex_specs.append(pl.BlockSpec((1, tn), lambda i, j, kk: (0, j)))
        else:
            assert e.shape == (m, 1), (name, e.shape)
            ex_specs.append(pl.BlockSpec((tm, 1), lambda i, j, kk: (i, 0)))
    dims = (((0 if ta else 1,), (1 if tb else 0,)), ((), ()))
    n_ex = len(extras)

    out_dtypes = list(out_dtype) if isinstance(out_dtype, (list, tuple)) else [out_dtype]
    n_out = len(out_dtypes)

    def body(a_ref, b_ref, *rest):
        ex_refs, o_refs = rest[:n_ex], rest[n_ex:n_ex + n_out]

        def product():
            return lax.dot_general(a_ref[...].astype(MXU_DTYPE), b_ref[...].astype(MXU_DTYPE), dims, preferred_element_type=F32)

        def finish(r):
            if epilogue is not None:
                r = epilogue(r, *[e[...] for e in ex_refs])
            for o_ref, val in zip(o_refs, r if isinstance(r, (tuple, list)) else (r,)):
                o_ref[...] = val.astype(o_ref.dtype)

        if nk == 1:
            finish(product())
            return
        acc_ref, kk = rest[n_ex + n_out], pl.program_id(2)

        @pl.when(kk == 0)
        def _():
            acc_ref[...] = jnp.zeros_like(acc_ref)

        acc_ref[...] += product()

        @pl.when(kk == nk - 1)
        def _():
            finish(acc_ref[...])

    res, got = _carried_call(
        body, name=name, grid=grid,
        in_specs=[a_spec, b_spec, *ex_specs],
        out_specs=[pl.BlockSpec((tm, tn), lambda i, j, kk: (i, j))] * n_out,
        out_shape=[jax.ShapeDtypeStruct((m, n), dt) for dt in out_dtypes],
        scratch_shapes=[pltpu.VMEM((tm, tn), F32)] if nk > 1 else [],
        semantics=("parallel", "parallel", "arbitrary"), operands=(a, b, *extras), job=job)
    res = res if isinstance(out_dtype, (list, tuple)) else res[0]
    return res if job is None else (res, got)


def _row_block(rows, want, tile=8):
    for tl in range(min(want, rows), tile - 1, -1):
        if rows % tl == 0 and tl % tile == 0:
            return tl
    return rows


def _rowmap(fn, ins, in_kinds, outs, *, name, tl):
    rows = next(x.shape[0] for x, kd in zip(ins, in_kinds) if kd == "r")
    tl = _row_block(rows, tl)
    n_in = len(ins)

    def spec(shape, kind):
        if kind == "r":
            return pl.BlockSpec((tl,) + tuple(shape[1:]), lambda i: (i,) + (0,) * (len(shape) - 1))
        return pl.BlockSpec(tuple(shape), lambda i: (0,) * len(shape))

    def body(*refs):
        in_refs, out_refs = refs[:n_in], refs[n_in:]
        res = fn(*[r[...] for r in in_refs])
        if not isinstance(res, (tuple, list)):
            res = (res,)
        for o_ref, val, (_, dt, kind) in zip(out_refs, res, outs):
            if kind == "r":
                o_ref[...] = val.astype(dt)
            else:
                @pl.when(pl.program_id(0) == 0)
                def _():
                    o_ref[...] = jnp.zeros_like(o_ref)

                o_ref[...] += val.astype(dt)

    has_acc = any(kd == "a" for _, _, kd in outs)
    res = pl.pallas_call(
        body, name=name, grid=(rows // tl,),
        in_specs=[spec(x.shape, kd) for x, kd in zip(ins, in_kinds)],
        out_specs=[spec(s, kd) for s, _, kd in outs],
        out_shape=[jax.ShapeDtypeStruct(s, dt) for s, dt, _ in outs],
        compiler_params=_params("arbitrary" if has_acc else "parallel"),
    )(*ins)
    return res[0] if len(outs) == 1 else res


def _rms_fwd(x, g):
    r = lax.rsqrt(jnp.mean(x * x, axis=-1, keepdims=True) + EPS)
    return x * r * g


def _rms_bwd(dh, x, g):
    r = lax.rsqrt(jnp.mean(x * x, axis=-1, keepdims=True) + EPS)
    xh = x * r
    dxh = dh * g
    dx = r * (dxh - xh * jnp.mean(dxh * xh, axis=-1, keepdims=True))
    return dx, jnp.sum(dh * xh, axis=0, keepdims=True)


def _swiglu_act(a, b):
    return jax.nn.silu(a) * b


def _ffn_up(x, g, wg, wu, *, name, tm=1024, tn=1024, job=None):
    m, d = x.shape
    n = wg.shape[0]
    tm, tn = min(tm, m), min(tn, n)
    assert m % tm == 0 and n % tn == 0, (name, m, n)

    def body(x_ref, g_ref, wg_ref, wu_ref, h_ref, a_ref, b_ref, s_ref):
        @pl.when(pl.program_id(1) == 0)
        def _():
            h_ref[...] = _rms_fwd(x_ref[...], g_ref[...]).astype(h_ref.dtype)

        hv = h_ref[...]
        av = lax.dot_general(hv, wg_ref[...], NT_DIMS, preferred_element_type=F32)
        bv = lax.dot_general(hv, wu_ref[...], NT_DIMS, preferred_element_type=F32)
        a_ref[...] = av.astype(a_ref.dtype)
        b_ref[...] = bv.astype(b_ref.dtype)
        s_ref[...] = _swiglu_act(av, bv).astype(s_ref.dtype)

    rows = pl.BlockSpec((tm, d), lambda i, j: (i, 0))
    wgt = pl.BlockSpec((tn, d), lambda i, j: (j, 0))
    tile = pl.BlockSpec((tm, tn), lambda i, j: (i, j))
    return _carried_call(
        body, name=name, grid=(m // tm, n // tn),
        in_specs=[rows, pl.BlockSpec((1, d), lambda i, j: (0, 0)), wgt, wgt],
        out_specs=[rows, tile, tile, tile],
        out_shape=[jax.ShapeDtypeStruct((m, d), MXU_DTYPE)] + [jax.ShapeDtypeStruct((m, n), MXU_DTYPE)] * 3,
        semantics=("parallel", "arbitrary"), operands=(x, g, wg, wu), job=job)


def _norm_proj(x, g, w, *, name, tm=1024, tn=1024, job=None):
    m, d = x.shape
    n = w.shape[1]
    tm, tn = min(tm, m), min(tn, n)
    assert m % tm == 0 and n % tn == 0, (name, m, n)

    def body(x_ref, g_ref, w_ref, h_ref, o_ref):
        @pl.when(pl.program_id(1) == 0)
        def _():
            h_ref[...] = _rms_fwd(x_ref[...], g_ref[...]).astype(h_ref.dtype)

        o_ref[...] = jnp.dot(h_ref[...], w_ref[...], preferred_element_type=F32)

    rows = pl.BlockSpec((tm, d), lambda i, j: (i, 0))
    return _carried_call(
        body, name=name, grid=(m // tm, n // tn),
        in_specs=[rows, pl.BlockSpec((1, d), lambda i, j: (0, 0)), pl.BlockSpec((d, tn), lambda i, j: (0, j))],
        out_specs=[rows, pl.BlockSpec((tm, tn), lambda i, j: (i, j))],
        out_shape=[jax.ShapeDtypeStruct((m, d), MXU_DTYPE), jax.ShapeDtypeStruct((m, n), F32)],
        semantics=("parallel", "arbitrary"), operands=(x, g, w), job=job)


def _proj_norm_bwd(pairs, x, g, dres, *, name, tm=1024, tk=1024, job=None):
    m, f = pairs[0][0].shape
    d = x.shape[1]
    tm, tk = min(tm, m), min(tk, f)
    assert m % tm == 0 and f % tk == 0, (name, m, f)
    nk, n_pairs = f // tk, len(pairs)
    swapped = [kept == "FD" for _, _, kept in pairs]

    def body(*refs):
        dy_refs, w_refs = refs[:n_pairs], refs[n_pairs:2 * n_pairs]
        x_ref, g_ref, dr_ref, dx_ref, dg_ref = refs[2 * n_pairs:2 * n_pairs + 5]
        i, kk = pl.program_id(0), pl.program_id(1)

        part = None
        for dy_ref, w_ref, rows_are_f in zip(dy_refs, w_refs, swapped):
            dims = (((1,), (0,)), ((), ())) if rows_are_f else NT_DIMS
            term = lax.dot_general(dy_ref[...].astype(MXU_DTYPE), w_ref[...], dims, preferred_element_type=F32)
            part = term if part is None else part + term

        @pl.when(jnp.logical_and(i == 0, kk == 0))
        def _():
            dg_ref[...] = jnp.zeros_like(dg_ref)

        def finish(dh):
            dx, dg = _rms_bwd(dh, x_ref[...], g_ref[...])
            dx_ref[...] = dx + dr_ref[...]
            dg_ref[...] += dg

        if nk == 1:
            finish(part)
            return
        acc_ref = refs[2 * n_pairs + 5]

        @pl.when(kk == 0)
        def _():
            acc_ref[...] = jnp.zeros_like(acc_ref)

        acc_ref[...] += part

        @pl.when(kk == nk - 1)
        def _():
            finish(acc_ref[...])

    act = pl.BlockSpec((tm, tk), lambda i, kk: (i, kk))
    w_specs = [pl.BlockSpec((tk, d), lambda i, kk: (kk, 0)) if s else pl.BlockSpec((d, tk), lambda i, kk: (0, kk)) for s in swapped]
    rows = pl.BlockSpec((tm, d), lambda i, kk: (i, 0))
    one = pl.BlockSpec((1, d), lambda i, kk: (0, 0))
    return _carried_call(
        body, name=name, grid=(m // tm, nk),
        in_specs=[act] * n_pairs + w_specs + [rows, one, rows],
        out_specs=[rows, one],
        out_shape=[jax.ShapeDtypeStruct((m, d), F32), jax.ShapeDtypeStruct((1, d), F32)],
        scratch_shapes=[pltpu.VMEM((tm, d), F32)] if nk > 1 else [],
        semantics=("arbitrary", "arbitrary"), operands=(*[p[0] for p in pairs], *[p[1] for p in pairs], x, g, dres), job=job)


def _ffn_dx(da, db, wg, wu, x, g, dres, *, name, job=None):
    return _proj_norm_bwd([(da, wg, "FD"), (db, wu, "FD")], x, g, dres, name=name, tm=512, tk=FF_PAD, job=job)


def _ffn_fwd(x, g, full, which, layer, late):
    tag = f"{which}_{layer}"
    job, keys = late.gather_job(f"{tag}_up") if late else (None, [])
    (h, a, b, s), got = _ffn_up(x, g, full[(f"{which}_w_gate", layer)], full[(f"{which}_w_up", layer)], name=f"{tag}_up", job=job)
    full.update(zip(keys, got))
    job, keys = late.gather_job(f"{tag}_down") if late else (None, [])
    x2, got = _carried(_mm, s, full[(f"{which}_w_down", layer)], name=f"{tag}_down", epilogue=lambda acc, xv: xv + 0.5 * acc,
                       extras=[x], tk=FF_PAD, job=job)
    full.update(zip(keys, got))
    return x2, (x, h, a, b, s)


def _ffn_bwd(dx2, saved, g, full, which, layer, grads, late, inline):
    x, h, a, b, s = saved
    tag = f"{which}_{layer}"
    kg, ku, kd = [(f"{which}_w_{n}", layer) for n in ("gate", "up", "down")]
    wg, wu, wd = full[kg], full[ku], full[kd]
    send = (lambda only: late.scatter_job(grads, only)) if (late and inline) else (lambda only: (None, []))

    def act_bwd(ds, av, bv):
        a32, b32, half = av.astype(F32), bv.astype(F32), 0.5 * ds
        sig = jax.nn.sigmoid(a32)
        return half * b32 * (sig * (1.0 + a32 * (1.0 - sig))), half * (a32 * sig)

    grads[kd] = _mm(s, dx2, ta=True, name=f"{tag}_dwd", out_dtype=WIRE_DTYPE, epilogue=lambda acc: 0.5 * acc, tk=2048)
    job, keys = send([kd])
    (da, db), got = _carried(_mm, dx2, wd, tb=True, name=f"{tag}_dact", epilogue=act_bwd, extras=[a, b],
                             out_dtype=[MXU_DTYPE, MXU_DTYPE], job=job)
    _note(late, keys, got)
    if inline is None:
        grads[kg], grads[ku] = _dw_pair(da, db, h, name=f"{tag}_dwgu")
    else:
        grads[kg] = _mm(da, h, ta=True, name=f"{tag}_dwg", out_dtype=WIRE_DTYPE, tk=4096)
        job, keys = send([kg]) if inline == "chain" else (None, [])
        grads[ku], got = _carried(_mm, db, h, ta=True, name=f"{tag}_dwu", out_dtype=WIRE_DTYPE, tk=4096, job=job)
        _note(late, keys, got)
    job, keys = send([ku] if inline == "chain" else [kg])
    (dx, dg), got = _ffn_dx(da, db, wg, wu, x, g, dx2, name=f"{tag}_dx", job=job)
    _note(late, keys, got)
    return dx, dg


def _dw_pair(da, db, h, *, name, tm=1024, tk=2048):
    seq, f = da.shape
    d = h.shape[1]
    tm, tk = min(tm, f), min(tk, seq)
    assert f % tm == 0 and seq % tk == 0, (name, f, seq)
    nk = seq // tk

    def body(da_ref, db_ref, h_ref, og_ref, ou_ref, accg_ref, accu_ref):
        kk = pl.program_id(1)

        @pl.when(kk == 0)
        def _():
            accg_ref[...] = jnp.zeros_like(accg_ref)
            accu_ref[...] = jnp.zeros_like(accu_ref)

        hv = h_ref[...]
        accg_ref[...] += lax.dot_general(da_ref[...], hv, TN_DIMS, preferred_element_type=F32)
        accu_ref[...] += lax.dot_general(db_ref[...], hv, TN_DIMS, preferred_element_type=F32)

        @pl.when(kk == nk - 1)
        def _():
            og_ref[...] = accg_ref[...].astype(og_ref.dtype)
            ou_ref[...] = accu_ref[...].astype(ou_ref.dtype)

    act = pl.BlockSpec((tk, tm), lambda i, kk: (kk, i))
    out = pl.BlockSpec((tm, d), lambda i, kk: (i, 0))
    return pl.pallas_call(
        body, name=name, grid=(f // tm, nk),
        in_specs=[act, act, pl.BlockSpec((tk, d), lambda i, kk: (kk, 0))],
        out_specs=[out, out], out_shape=[jax.ShapeDtypeStruct((f, d), WIRE_DTYPE)] * 2,
        scratch_shapes=[pltpu.VMEM((tm, d), F32)] * 2,
        compiler_params=_params("parallel", "arbitrary"),
    )(da, db, h)


def _carried(fn, *args, job, **kwargs):
    return fn(*args, job=job, **kwargs) if job is not None else (fn(*args, **kwargs), [])


def _note(late, keys, got):
    if late:
        late.received.update(zip(keys, got))


def _softplus(z):
    return jnp.maximum(z, 0.0) + jnp.log(1.0 + jnp.exp(-jnp.abs(z)))


def _ones_dot(x, tri):
    if MXU_DTYPE == F32:
        return jnp.dot(x, tri, preferred_element_type=F32)
    hi = x.astype(MXU_DTYPE)
    lo = (x - hi.astype(F32)).astype(MXU_DTYPE)
    return jnp.dot(hi, tri, preferred_element_type=F32) + jnp.dot(lo, tri, preferred_element_type=F32)


NT_DIMS = (((1,), (1,)), ((), ()))
TN_DIMS = (((0,), (0,)), ((), ()))


SB_LANES = SB_PACK * SB_DH
Q_COL, K_COL, V_COL = (S5_WIDTH * n // SB_LANES for n in (1, 2, 3))


def _head_lanes(rows, hd):
    return lax.broadcasted_iota(jnp.int32, (rows, SB_LANES), 1) // SB_DH == hd


def _attend(proj, *, tq=SB_QUERIES, job=None):
    seq = proj.shape[0]
    tq = min(tq, seq)
    tk = min(SB_KEYS, tq)
    per, hp = tq // tk, SB_PACK

    def body(q_ref, k_ref, v_ref, o_ref, ls_ref):
        i = pl.program_id(1)
        r_idx = lax.broadcasted_iota(jnp.int32, (tk, tk), 0)
        c_idx = lax.broadcasted_iota(jnp.int32, (tk, tk), 1)
        after = (r_idx > c_idx).astype(MXU_DTYPE)
        lanes = [_head_lanes(tk, hd) for hd in range(hp)]

        def block(j, cs, acc, straddles):
            off = pl.multiple_of(j * tk, tk)
            k2, v2 = k_ref[pl.ds(off, tk), :], v_ref[pl.ds(off, tk), :]
            top = 0 if straddles is None else straddles * tk
            rows = tq - top
            q2 = (q_ref[pl.ds(top, rows), :] * SB_SCALE).astype(MXU_DTYPE)
            new_cs, out = [], acc[top:]
            for hd in range(hp):
                kv = jnp.where(lanes[hd], k2, 0.0).astype(MXU_DTYPE)
                vv = jnp.where(lanes[hd], v2, 0.0).astype(MXU_DTYPE)
                z = lax.dot_general(q2, kv, NT_DIMS, preferred_element_type=F32)
                sp = _softplus(z)
                c_in = cs[hd][top:]
                if straddles is None:
                    lk = -sp
                    w = jnp.exp(z - sp + _ones_dot(lk, after) + c_in)
                else:
                    before = lax.broadcasted_iota(jnp.int32, (rows, tk), 1) < lax.broadcasted_iota(jnp.int32, (rows, tk), 0)
                    lk = jnp.where(before, -sp, 0.0)
                    w = jnp.where(before, jnp.exp(z - sp + _ones_dot(lk, after) + c_in), 0.0)
                out = out + jnp.dot(w.astype(MXU_DTYPE), vv, preferred_element_type=F32)
                c_new = c_in + jnp.sum(lk, axis=1, keepdims=True)
                new_cs.append(jnp.concatenate([cs[hd][:top], c_new], axis=0) if top else c_new)
            return tuple(new_cs), (jnp.concatenate([acc[:top], out], axis=0) if top else out)

        carry = (tuple(jnp.zeros((tq, 1), F32) for _ in range(hp)), jnp.zeros((tq, SB_LANES), F32))
        for s in reversed(range(per)):
            carry = block(i * per + s, *carry, s)
        cs, acc = lax.fori_loop(0, i * per, lambda n, cr: block(i * per - 1 - n, *cr, None), carry)
        o_ref[...] = acc
        for hd in range(hp):
            ls_ref[hd] = cs[hd]

    whole = lambda col: pl.BlockSpec((seq, SB_LANES), lambda g, i: (0, col + g))
    return _carried_call(
        body, name="sb_fwd", grid=(SB_HEADS // hp, seq // tq),
        in_specs=[pl.BlockSpec((tq, SB_LANES), lambda g, i: (i, Q_COL + g)), whole(K_COL), whole(V_COL)],
        out_specs=[pl.BlockSpec((tq, SB_LANES), lambda g, i: (i, g)), pl.BlockSpec((hp, tq, 1), lambda g, i: (g, i, 0))],
        out_shape=[jax.ShapeDtypeStruct((seq, SB_HEADS * SB_DH), F32), jax.ShapeDtypeStruct((SB_HEADS, seq, 1), F32)],
        semantics=("parallel", "parallel"), operands=(proj, proj, proj), job=job)


def _attend_bwd(proj, lsum, dmix, *, tq=SB_QUERIES, job=None):
    seq = proj.shape[0]
    tq = min(tq, seq)
    tk = min(SB_KEYS, tq)
    per, hp = tq // tk, SB_PACK
    do_col = S5_WIDTH // SB_LANES

    def body(q_ref, k_ref, v_ref, ls_ref, do_ref, dq_ref, dk_ref, dv_ref):
        i = pl.program_id(1)

        @pl.when(i == 0)
        def _():
            dk_ref[...] = jnp.zeros_like(dk_ref)
            dv_ref[...] = jnp.zeros_like(dv_ref)

        r_idx = lax.broadcasted_iota(jnp.int32, (tk, tk), 0)
        c_idx = lax.broadcasted_iota(jnp.int32, (tk, tk), 1)
        upto = (r_idx <= c_idx).astype(MXU_DTYPE)
        before = (r_idx < c_idx).astype(MXU_DTYPE)
        lanes = [_head_lanes(tk, hd) for hd in range(hp)]

        def block(j, sums, dq, straddles):
            off = pl.multiple_of(j * tk, tk)
            k2, v2 = k_ref[pl.ds(off, tk), :], v_ref[pl.ds(off, tk), :]
            top = 0 if straddles is None else straddles * tk
            rows = tq - top
            part = pl.ds(top, rows)
            q2 = (q_ref[part, :] * SB_SCALE).astype(MXU_DTYPE)
            do2 = do_ref[part, :].astype(MXU_DTYPE)
            valid = None
            if straddles is not None:
                valid = lax.broadcasted_iota(jnp.int32, (rows, tk), 1) < lax.broadcasted_iota(jnp.int32, (rows, tk), 0)
            new_sums, out, dk, dv = [], dq[top:], jnp.zeros((tk, SB_LANES), F32), jnp.zeros((tk, SB_LANES), F32)
            for hd in range(hp):
                cp, ce = sums[hd]
                kv = jnp.where(lanes[hd], k2, 0.0).astype(MXU_DTYPE)
                vv = jnp.where(lanes[hd], v2, 0.0).astype(MXU_DTYPE)
                z = lax.dot_general(q2, kv, NT_DIMS, preferred_element_type=F32)
                sp = _softplus(z)
                lk = -sp if valid is None else jnp.where(valid, -sp, 0.0)
                w = jnp.exp(z - sp + (ls_ref[hd, part, :] - cp[top:]) - _ones_dot(lk, upto))
                if valid is not None:
                    w = jnp.where(valid, w, 0.0)
                e = w * lax.dot_general(do2, vv, NT_DIMS, preferred_element_type=F32)
                earlier = jnp.dot(e.astype(MXU_DTYPE), before, preferred_element_type=F32) + ce[top:]
                keep = jnp.exp(-sp)
                dz = e * keep - (1.0 - keep) * earlier
                if valid is not None:
                    dz = jnp.where(valid, dz, 0.0)
                dzm = dz.astype(MXU_DTYPE)
                out = out + jnp.dot(dzm, kv, preferred_element_type=F32)
                dk = dk + jnp.where(lanes[hd], lax.dot_general(dzm, q2, TN_DIMS, preferred_element_type=F32), 0.0)
                dv = dv + jnp.where(lanes[hd], lax.dot_general(w.astype(MXU_DTYPE), do2, TN_DIMS, preferred_element_type=F32), 0.0)
                new = (cp[top:] + jnp.sum(lk, axis=1, keepdims=True), ce[top:] + jnp.sum(e, axis=1, keepdims=True))
                new_sums.append(tuple(jnp.concatenate([old[:top], val], axis=0) for old, val in zip((cp, ce), new)) if top else new)
            dk_ref[pl.ds(off, tk), :] += dk
            dv_ref[pl.ds(off, tk), :] += dv
            return tuple(new_sums), (jnp.concatenate([dq[:top], out], axis=0) if top else out)

        zero = jnp.zeros((tq, 1), F32)
        carry = (tuple((zero, zero) for _ in range(hp)), jnp.zeros((tq, SB_LANES), F32))
        carry = lax.fori_loop(0, i * per, lambda j, cr: block(j, *cr, None), carry)
        for s in range(per):
            carry = block(i * per + s, *carry, s)
        dq_ref[...] = carry[1] * SB_SCALE

    whole = lambda col: pl.BlockSpec((seq, SB_LANES), lambda g, i: (0, col + g))
    tile = lambda col: pl.BlockSpec((tq, SB_LANES), lambda g, i: (i, col + g))
    acc = pl.BlockSpec((seq, SB_LANES), lambda g, i: (0, g))
    return _carried_call(
        body, name="sb_bwd", grid=(SB_HEADS // hp, seq // tq),
        in_specs=[tile(Q_COL), whole(K_COL), whole(V_COL), pl.BlockSpec((hp, tq, 1), lambda g, i: (g, i, 0)), tile(do_col)],
        out_specs=[tile(0), acc, acc],
        out_shape=[jax.ShapeDtypeStruct((seq, SB_HEADS * SB_DH), F32)] * 3,
        semantics=("parallel", "arbitrary"), operands=(proj, proj, proj, lsum, dmix), job=job)


def _s5_disc(lr, li, ldt, br, bi):
    dt = jnp.exp(ldt)
    mag = jnp.exp(lr * dt)
    ar = mag * jnp.cos(li * dt)
    ai = mag * jnp.sin(li * dt)
    den = lr * lr + li * li
    nr = ar - 1.0
    cr = (nr * lr + ai * li) / den
    ci = (ai * lr - nr * li) / den
    return ar, ai, cr[None] * br - ci[None] * bi, cr[None] * bi + ci[None] * br


def _s5_prep(lr, li, ldt, br, bi):
    shapes = [lr.shape, lr.shape, br.shape, br.shape]

    def body(lr_ref, li_ref, ldt_ref, br_ref, bi_ref, *outs):
        for o, val in zip(outs, _s5_disc(lr_ref[...], li_ref[...], ldt_ref[...], br_ref[...], bi_ref[...])):
            o[...] = val

    return pl.pallas_call(body, name="s5_prep", out_shape=[jax.ShapeDtypeStruct(s, F32) for s in shapes])(lr, li, ldt, br, bi)


def _s5_prep_bwd(lr, li, ldt, br, bi, cts):
    args = (lr, li, ldt, br, bi)

    def body(*refs):
        ins, ct_refs, outs = refs[:5], refs[5:9], refs[9:]
        _, vjp = jax.vjp(_s5_disc, *[r[...] for r in ins])
        for o, val in zip(outs, vjp(tuple(r[...] for r in ct_refs))):
            o[...] = val

    return pl.pallas_call(body, name="s5_prep_bwd", out_shape=[jax.ShapeDtypeStruct(a.shape, F32) for a in args])(*args, *cts)


SCAN_ROWS = 8


def _powers(ar, ai):
    out = [(ar, ai)]
    for _ in range(SCAN_ROWS - 1):
        pr, pi = out[-1]
        out.append((pr * ar - pi * ai, pr * ai + pi * ar))
    return out


def _s5_states(u, bmat, cmat, a, d, *, tc=512):
    seq, width = u.shape
    nj, cols, w2 = bmat.shape
    tw = w2 // 2
    tc = min(tc, seq)
    assert seq % tc == 0 and nj * cols == width and tw == S5_BLOCK

    def body(u_ref, bm_ref, cm_ref, a_ref, d_ref, h_ref, y_ref, cr_ref, ci_ref):
        @pl.when(pl.program_id(1) == 0)
        def _():
            cr_ref[...] = jnp.zeros_like(cr_ref)
            ci_ref[...] = jnp.zeros_like(ci_ref)

        uv = u_ref[...]
        h_ref[...] = jnp.dot(uv.astype(MXU_DTYPE), bm_ref[0], preferred_element_type=F32)
        re, im = pl.ds(0, tw), pl.ds(tw, tw)
        powers = _powers(a_ref[:, re], a_ref[:, im])
        pr = jnp.concatenate([p[0] for p in powers], axis=0)
        pi = jnp.concatenate([p[1] for p in powers], axis=0)
        row_id = lax.broadcasted_iota(jnp.int32, (SCAN_ROWS, tw), 0)
        reach = {dist: tuple(jnp.where(row_id >= dist, part, 0.0) for part in powers[dist - 1]) for dist in (1, 2, 4)}

        def block(n, carry):
            hr, hi = carry
            rows = pl.ds(pl.multiple_of(n * SCAN_ROWS, SCAN_ROWS), SCAN_ROWS)
            yr, yi = h_ref[rows, re], h_ref[rows, im]
            for dist in (1, 2, 4):
                cr, ci = reach[dist]
                sr, si = pltpu.roll(yr, dist, 0), pltpu.roll(yi, dist, 0)
                yr, yi = yr + cr * sr - ci * si, yi + cr * si + ci * sr
            yr, yi = yr + pr * hr - pi * hi, yi + pr * hi + pi * hr
            h_ref[rows, re] = yr
            h_ref[rows, im] = yi
            return yr[SCAN_ROWS - 1:], yi[SCAN_ROWS - 1:]

        hr, hi = lax.fori_loop(0, tc // SCAN_ROWS, block, (cr_ref[...], ci_ref[...]), unroll=4)
        cr_ref[...] = hr
        ci_ref[...] = hi
        y_ref[...] = jnp.dot(h_ref[...].astype(MXU_DTYPE), cm_ref[0], preferred_element_type=F32) + d_ref[...] * uv

    io = pl.BlockSpec((tc, cols), lambda j, t: (t, j))
    return pl.pallas_call(
        body, name="s5_states", grid=(nj, seq // tc),
        in_specs=[io, pl.BlockSpec((1, cols, w2), lambda j, t: (j, 0, 0)), pl.BlockSpec((1, w2, cols), lambda j, t: (j, 0, 0)),
                  pl.BlockSpec((1, w2), lambda j, t: (0, j)), pl.BlockSpec((1, cols), lambda j, t: (0, j))],
        out_specs=[pl.BlockSpec((tc, w2), lambda j, t: (t, j)), io],
        out_shape=[jax.ShapeDtypeStruct((seq, nj * w2), F32), jax.ShapeDtypeStruct((seq, width), F32)],
        scratch_shapes=[pltpu.VMEM((1, tw), F32)] * 2,
        compiler_params=_params("parallel", "arbitrary"),
    )(u, bmat, cmat, a, d)


def _s5_states_bwd(dy, h, u, bmat, cmat, a, du_skip, *, tc=512):
    seq, width = u.shape
    nj, cols, w2 = bmat.shape
    tw = w2 // 2
    tc = min(tc, seq)
    assert seq % tc == 0
    nt = seq // tc

    def body(dy_ref, h_ref, u_ref, bm_ref, cm_ref, a_ref, sk_ref, du_ref, da_ref, db_ref, dc_ref, g_ref, cr_ref, ci_ref):
        @pl.when(pl.program_id(1) == 0)
        def _():
            cr_ref[...] = jnp.zeros_like(cr_ref)
            ci_ref[...] = jnp.zeros_like(ci_ref)
            da_ref[...] = jnp.zeros_like(da_ref)
            db_ref[...] = jnp.zeros_like(db_ref)
            dc_ref[...] = jnp.zeros_like(dc_ref)

        dyv = dy_ref[...].astype(MXU_DTYPE)
        g_ref[...] = lax.dot_general(dyv, cm_ref[0], NT_DIMS, preferred_element_type=F32)
        re, im = pl.ds(0, tw), pl.ds(tw, tw)
        powers = _powers(a_ref[:, re], a_ref[:, im])
        pr = jnp.concatenate([p[0] for p in reversed(powers)], axis=0)
        pi = jnp.concatenate([p[1] for p in reversed(powers)], axis=0)
        row_id = lax.broadcasted_iota(jnp.int32, (SCAN_ROWS, tw), 0)
        last = SCAN_ROWS - 1
        reach = {dist: tuple(jnp.where(row_id < SCAN_ROWS - dist, part, 0.0) for part in powers[dist - 1]) for dist in (1, 2, 4)}

        def block(n, carry):
            gr, gi, sr, si = carry
            rows = pl.ds(pl.multiple_of((tc // SCAN_ROWS - 1 - n) * SCAN_ROWS, SCAN_ROWS), SCAN_ROWS)
            yr, yi = g_ref[rows, re], g_ref[rows, im]
            for dist in (1, 2, 4):
                cr, ci = reach[dist]
                ur, ui = pltpu.roll(yr, SCAN_ROWS - dist, 0), pltpu.roll(yi, SCAN_ROWS - dist, 0)
                yr, yi = yr + cr * ur + ci * ui, yi + cr * ui - ci * ur
            yr, yi = yr + pr * gr + pi * gi, yi + pr * gi - pi * gr
            g_ref[rows, re] = yr
            g_ref[rows, im] = yi
            nr = jnp.where(row_id < last, pltpu.roll(yr, last, 0), gr)
            ni = jnp.where(row_id < last, pltpu.roll(yi, last, 0), gi)
            hr, hi = h_ref[rows, re], h_ref[rows, im]
            return yr[:1], yi[:1], sr + nr * hr + ni * hi, si + ni * hr - nr * hi

        zero = jnp.zeros((SCAN_ROWS, tw), F32)
        gr, gi, sr, si = lax.fori_loop(0, tc // SCAN_ROWS, block, (cr_ref[...], ci_ref[...], zero, zero), unroll=4)
        cr_ref[...] = gr
        ci_ref[...] = gi
        da_ref[:, re] += jnp.sum(sr, axis=0, keepdims=True)
        da_ref[:, im] += jnp.sum(si, axis=0, keepdims=True)
        gv = g_ref[...].astype(MXU_DTYPE)
        du_ref[...] = (lax.dot_general(gv, bm_ref[0], NT_DIMS, preferred_element_type=F32) + sk_ref[...]).astype(du_ref.dtype)
        db_ref[0] += lax.dot_general(u_ref[...].astype(MXU_DTYPE), gv, TN_DIMS, preferred_element_type=F32)
        dc_ref[0] += lax.dot_general(h_ref[...].astype(MXU_DTYPE), dyv, TN_DIMS, preferred_element_type=F32)

    io = pl.BlockSpec((tc, cols), lambda j, t: (nt - 1 - t, j))
    bm = pl.BlockSpec((1, cols, w2), lambda j, t: (j, 0, 0))
    cm = pl.BlockSpec((1, w2, cols), lambda j, t: (j, 0, 0))
    row = pl.BlockSpec((1, w2), lambda j, t: (0, j))
    return pl.pallas_call(
        body, name="s5_states_bwd", grid=(nj, nt),
        in_specs=[io, pl.BlockSpec((tc, w2), lambda j, t: (nt - 1 - t, j)), io, bm, cm, row, io],
        out_specs=[io, row, bm, cm],
        out_shape=[jax.ShapeDtypeStruct((seq, width), MXU_DTYPE), jax.ShapeDtypeStruct((1, nj * w2), F32),
                   jax.ShapeDtypeStruct(bmat.shape, F32), jax.ShapeDtypeStruct(cmat.shape, F32)],
        scratch_shapes=[pltpu.VMEM((tc, w2), F32)] + [pltpu.VMEM((1, tw), F32)] * 2,
        compiler_params=_params("parallel", "arbitrary"),
    )(dy, h, u, bmat, cmat, a, du_skip)


def _pair_columns(re, im, axis):
    shape = re.shape
    split = shape[:axis] + (shape[axis] // S5_BLOCK, S5_BLOCK) + shape[axis + 1:]
    both = jnp.stack([re.reshape(split), im.reshape(split)], axis=axis + 1)
    return both.reshape(shape[:axis] + (2 * shape[axis],) + shape[axis + 1:])


def _unpair_columns(t, axis):
    shape = t.shape
    both = t.reshape(shape[:axis] + (shape[axis] // (2 * S5_BLOCK), 2, S5_BLOCK) + shape[axis + 1:])
    half = shape[:axis] + (shape[axis] // 2,) + shape[axis + 1:]
    return (lax.index_in_dim(both, 0, axis + 1, keepdims=False).reshape(half),
            lax.index_in_dim(both, 1, axis + 1, keepdims=False).reshape(half))


S5_PER_BLOCK = S5_GROUPS // S5_DIAG


def _block_diag(t):
    g, a, b = t.shape
    n = S5_PER_BLOCK
    eye = jnp.eye(n, dtype=t.dtype)
    return (t.reshape(g // n, n, a, 1, b) * eye[None, :, None, :, None]).reshape(g // n, n * a, n * b)


def _block_diag_part(m):
    j, n = m.shape[0], S5_PER_BLOCK
    a, b = m.shape[1] // n, m.shape[2] // n
    return jnp.moveaxis(jnp.diagonal(m.reshape(j, n, a, n, b), axis1=1, axis2=3), -1, 1).reshape(j * n, a, b)


def _gelu_glu(y, gate_pre):
    z = jax.nn.gelu(y)
    return z * jax.nn.sigmoid(gate_pre)


def _s5_fwd(u, p, w_glu):
    lr, li = p["s5_lambda_re"][0], p["s5_lambda_im"][0]
    ldt = p["s5_log_dt"][0][:, None]
    br = p["s5_b_re"][0].transpose(2, 0, 1)
    bi = p["s5_b_im"][0].transpose(2, 0, 1)
    ar, ai, bbr, bbi = _s5_prep(lr, li, ldt, br, bi)
    a = _pair_columns(ar.reshape(1, S5_LANES), ai.reshape(1, S5_LANES), 1)
    bmat = jnp.concatenate([_block_diag(bbr.transpose(1, 0, 2)), _block_diag(bbi.transpose(1, 0, 2))], axis=2)
    cmat = jnp.concatenate([_block_diag(p["s5_c_re"][0].transpose(0, 2, 1)),
                            -_block_diag(p["s5_c_im"][0].transpose(0, 2, 1))], axis=1)
    bmat, cmat = bmat.astype(MXU_DTYPE), cmat.astype(MXU_DTYPE)
    d = p["s5_d"]
    h, y = _s5_states(u, bmat, cmat, a, d)
    z = _rowmap(jax.nn.gelu, [y], "r", [(y.shape, MXU_DTYPE, "r")], name="s5_gelu", tl=512)
    gate_pre = _mm(z, w_glu, name="s5_glu")
    out = _rowmap(_gelu_glu, [y, gate_pre], "rr", [(y.shape, F32, "r")], name="s5_gate", tl=512)
    return out, (u, lr, li, ldt, br, bi, a, bmat, cmat, h, y, z, gate_pre)


def _s5_bwd(dout, saved, p, w_glu):
    u, lr, li, ldt, br, bi, a, bmat, cmat, h, y, z, gate_pre = saved
    d = p["s5_d"]

    def gate_bwd(dov, yv, gv):
        zv = jax.nn.gelu(yv)
        sg = jax.nn.sigmoid(gv)
        return dov * sg, dov * zv * sg * (1.0 - sg)

    dz_direct, dgate = _rowmap(gate_bwd, [dout, y, gate_pre], "rrr", [(y.shape, F32, "r"), (y.shape, MXU_DTYPE, "r")],
                               name="s5_gate_bwd", tl=512)
    dw_glu = _mm(z, dgate, ta=True, name="s5_dwglu", out_dtype=WIRE_DTYPE)
    dz = _mm(dgate, w_glu, tb=True, name="s5_dz", epilogue=lambda acc, prev: acc + prev, extras=[dz_direct])

    def gelu_bwd(dzv, yv, uv, dvv):
        _, vjp = jax.vjp(jax.nn.gelu, yv)
        dy = vjp(dzv)[0]
        return dy, dy * dvv, jnp.sum(dy * uv, axis=0, keepdims=True)

    dy, du_skip, dd = _rowmap(gelu_bwd, [dz, y, u, d], "rrrc",
                              [(y.shape, F32, "r"), (y.shape, F32, "r"), (d.shape, F32, "a")], name="s5_gelu_bwd", tl=512)
    du, da, dbmat, dcmat = _s5_states_bwd(dy, h, u, bmat, cmat, a, du_skip)
    dbbr, dbbi = (_block_diag_part(t).transpose(1, 0, 2) for t in (dbmat[:, :, :S5_BLOCK], dbmat[:, :, S5_BLOCK:]))
    dar, dai = _unpair_columns(da, 1)
    cts = (dar.reshape(S5_GROUPS, S5_STATE), dai.reshape(S5_GROUPS, S5_STATE), dbbr, dbbi)
    dlr, dli, dldt, dbr, dbi = _s5_prep_bwd(lr, li, ldt, br, bi, cts)
    dcr, dci = (_block_diag_part(t).transpose(0, 2, 1) for t in (dcmat[:, :S5_BLOCK], dcmat[:, S5_BLOCK:]))
    grads = {
        "s5_lambda_re": dlr[None], "s5_lambda_im": dli[None], "s5_log_dt": dldt[:, 0][None],
        "s5_b_re": dbr.transpose(1, 2, 0)[None], "s5_b_im": dbi.transpose(1, 2, 0)[None],
        "s5_c_re": dcr[None], "s5_c_im": -dci[None], "s5_d": dd,
    }
    return du, dw_glu, grads


def _mix0_fwd(x, g, p, full, late):
    (h, proj), _ = _norm_proj(x, g, full[("ab_w_in", 0)], name="mix0_in")
    u = proj[:, :S5_WIDTH]
    job, keys = late.gather_job("sb_fwd") if late else (None, [])
    (o, lsum), got = _attend(proj, job=job)
    full.update(zip(keys, got))
    w_glu, w_out = full[("s5_w_glu", 0)], full[("ab_w_out", 0)]
    y_a, s5_saved = _s5_fwd(u, p, w_glu)
    mix = jnp.concatenate([y_a, o], axis=1).astype(MXU_DTYPE)
    x2 = _mm(mix, w_out, name="mix0_out", epilogue=lambda acc, xv: xv + acc, extras=[x])
    return x2, (x, h, proj, lsum, mix, s5_saved)


def _mix0_bwd(dx2, saved, g, p, full, grads, late):
    x, h, proj, lsum, mix, s5_saved = saved
    w_in, w_glu, w_out = full[("ab_w_in", 0)], full[("s5_w_glu", 0)], full[("ab_w_out", 0)]
    dmix = _mm(dx2, w_out, tb=True, name="mix0_dmix")
    grads[("ab_w_out", 0)] = _mm(mix, dx2, ta=True, name="mix0_dwout", out_dtype=WIRE_DTYPE)
    du, grads[("s5_w_glu", 0)], s5_grads = _s5_bwd(dmix[:, :S5_WIDTH], s5_saved, p, w_glu)
    job, keys = late.scatter_job(grads) if late else (None, [])
    (dq, dk, dv), got = _attend_bwd(proj, lsum, dmix, job=job)
    _note(late, keys, got)
    dproj = jnp.concatenate([du] + [t.astype(MXU_DTYPE) for t in (dq, dk, dv)], axis=1)
    grads[("ab_w_in", 0)] = _mm(h, dproj, ta=True, name="mix0_dwin", out_dtype=WIRE_DTYPE)
    job, keys = late.scatter_job(grads) if late else (None, [])
    (dx, dg), got = _proj_norm_bwd([(dproj, w_in, "DF")], x, g, dx2, name="mix0_dh", job=job)
    _note(late, keys, got)
    return dx, dg, s5_grads


def _shift_down(t, n):
    rows = lax.broadcasted_iota(jnp.int32, t.shape, 0)
    return jnp.where(rows >= n, pltpu.roll(t, n, 0), 0.0)


def _shift_up(t, n):
    rows = lax.broadcasted_iota(jnp.int32, t.shape, 0)
    return jnp.where(rows < t.shape[0] - n, pltpu.roll(t, t.shape[0] - n, 0), 0.0)


def _conv_fwd(proj, cw, *, tc=128):
    seq, c3 = proj.shape
    ch = c3 // 3
    nb = ch // tc

    def body(b_ref, c_ref, v_ref, w_ref, m_ref):
        pv = c_ref[...] * v_ref[...]
        w = w_ref[...]
        y = w[2:3] * pv + w[1:2] * _shift_down(pv, 1) + w[0:1] * _shift_down(pv, 2)
        m_ref[...] = (b_ref[...] * y).astype(m_ref.dtype)

    col = lambda part: pl.BlockSpec((seq, tc), lambda j: (0, part * nb + j))
    return pl.pallas_call(
        body, name="conv_fwd", grid=(nb,),
        in_specs=[col(0), col(1), col(2), pl.BlockSpec((3, tc), lambda j: (0, j))],
        out_specs=pl.BlockSpec((seq, tc), lambda j: (0, j)),
        out_shape=jax.ShapeDtypeStruct((seq, ch), MXU_DTYPE),
        compiler_params=_params("parallel"),
    )(proj, proj, proj, cw)


def _conv_bwd(proj, cw, dm, *, tc=128):
    seq, c3 = proj.shape
    ch = c3 // 3
    nb = ch // tc

    def body(b_ref, c_ref, v_ref, w_ref, dm_ref, dproj_ref, dw_ref, dc_ref, dv_ref):
        part = pl.program_id(1)

        @pl.when(part == 0)
        def _():
            cv, vv, dmv = c_ref[...], v_ref[...], dm_ref[...]
            pv = cv * vv
            w = w_ref[...]
            p1, p2 = _shift_down(pv, 1), _shift_down(pv, 2)
            y = w[2:3] * pv + w[1:2] * p1 + w[0:1] * p2
            dproj_ref[...] = (dmv * y).astype(dproj_ref.dtype)
            dy = dmv * b_ref[...]
            dp = w[2:3] * dy + w[1:2] * _shift_up(dy, 1) + w[0:1] * _shift_up(dy, 2)
            dc_ref[...] = (dp * vv).astype(dc_ref.dtype)
            dv_ref[...] = (dp * cv).astype(dv_ref.dtype)
            dw_ref[...] = jnp.concatenate([jnp.sum(dy * p2, axis=0, keepdims=True), jnp.sum(dy * p1, axis=0, keepdims=True),
                                           jnp.sum(dy * pv, axis=0, keepdims=True)], axis=0)

        @pl.when(part == 1)
        def _():
            dproj_ref[...] = dc_ref[...]

        @pl.when(part == 2)
        def _():
            dproj_ref[...] = dv_ref[...]

    col = lambda part: pl.BlockSpec((seq, tc), lambda j, t: (0, part * nb + j))
    small = pl.BlockSpec((3, tc), lambda j, t: (0, j))
    return pl.pallas_call(
        body, name="conv_bwd", grid=(nb, 3),
        in_specs=[col(0), col(1), col(2), small, pl.BlockSpec((seq, tc), lambda j, t: (0, j))],
        out_specs=[pl.BlockSpec((seq, tc), lambda j, t: (0, t * nb + j)), small],
        out_shape=[jax.ShapeDtypeStruct((seq, c3), MXU_DTYPE), jax.ShapeDtypeStruct((3, ch), F32)],
        scratch_shapes=[pltpu.VMEM((seq, tc), MXU_DTYPE)] * 2,
        compiler_params=_params("parallel", "arbitrary"),
    )(proj, proj, proj, cw, dm)


def _mix1_fwd(x, g, full, late):
    job, keys = late.gather_job("mix1_in") if late else (None, [])
    (h, proj), got = _norm_proj(x, g, full[("sc_w_in", 0)], name="mix1_in", job=job)
    full.update(zip(keys, got))
    m = _conv_fwd(proj, full[("sc_conv_w", 0)])
    x2 = _mm(m, full[("sc_w_out", 0)], name="mix1_out", epilogue=lambda acc, xv: xv + acc, extras=[x])
    return x2, (x, h, proj, m)


def _mix1_bwd(dx2, saved, g, full, grads, late):
    x, h, proj, m = saved
    w_in, cw, w_out = full[("sc_w_in", 0)], full[("sc_conv_w", 0)], full[("sc_w_out", 0)]
    dm = _mm(dx2, w_out, tb=True, name="mix1_dm")
    grads[("sc_w_out", 0)] = _mm(m, dx2, ta=True, name="mix1_dwout", out_dtype=WIRE_DTYPE)
    dproj, dcw = _conv_bwd(proj, cw, dm)
    grads[("sc_conv_w", 0)] = dcw.astype(WIRE_DTYPE)
    grads[("sc_w_in", 0)] = _mm(h, dproj, ta=True, name="mix1_dwin", out_dtype=WIRE_DTYPE)
    job, keys = late.scatter_job(grads, [("ffn2_w_up", 1)]) if late else (None, [])
    (dx, dg), got = _proj_norm_bwd([(dproj, w_in, "DF")], x, g, dx2, name="mix1_dh", job=job)
    _note(late, keys, got)
    return dx, dg


def _loss_head(x, g, target):
    feat = x.shape[1]

    def fn(xv, gv, tv):
        err = _rms_fwd(xv, gv) - tv
        dx, dg = _rms_bwd(err / feat, xv, gv)
        return jnp.sum(err * err, keepdims=True) * (0.5 / feat), dx, dg

    return _rowmap(fn, [x, g, target], "rcr", [((1, 1), F32, "a"), (x.shape, F32, "r"), (g.shape, F32, "a")],
                   name="loss_head", tl=256)


def _slot(ref, place, chip=None, half=None, piece=(0, 1)):
    axis, width = place
    shape = list(ref.shape)
    start = [0, 0]
    if chip is not None:
        start[axis], shape[axis] = chip * width, width
    if half is not None:
        h_axis = 0 if shape[0] % 32 == 0 else 1
        shape[h_axis] //= 2 * piece[1]
        start[h_axis] = start[h_axis] + (half * piece[1] + piece[0]) * shape[h_axis]
    hint = lambda s, d: s if isinstance(s, int) else pl.multiple_of(s, 128 if d == 1 else 8)
    return ref.at[tuple(pl.ds(hint(s, d), n) for d, (s, n) in enumerate(zip(start, shape)))]


class _Exchange:
    def __init__(self, kind, arrays, places, pieces=1):
        self.kind, self.arrays, self.places, self.n, self.pieces = kind, list(arrays), list(places), len(arrays), pieces
        self.out_shape = []
        for t, (axis, width) in zip(self.arrays, self.places):
            if kind == "gather":
                shape = list(t.shape)
                shape[axis] = N_CHIPS * width
            else:
                shape = [N_CHIPS] + list(t.shape)
                shape[1 + axis] = width
            self.out_shape.append(jax.ShapeDtypeStruct(tuple(shape), t.dtype))
        n = self.n
        self.scratch = [pltpu.SemaphoreType.DMA((3 * n * pieces,)) for _ in range(4 if kind == "gather" else 2)]
        self.scratch.append(pltpu.SemaphoreType.DMA((n,)))

    def _copies(self, ins, outs, sems):
        x, y, c = lax.axis_index("x"), lax.axis_index("y"), lax.axis_index("c")
        peers = [(1 - x, y), (x, 1 - y), (1 - x, 1 - y)]
        remote = lambda src, dst, send, recv, k, to: pltpu.make_async_remote_copy(
            src_ref=src, dst_ref=dst, send_sem=send.at[k], recv_sem=recv.at[k], device_id=to, device_id_type=MESH_ID)
        local, ici, d2d = [], [], []
        for a in range(self.n):
            place = self.places[a]
            if self.kind == "gather":
                local.append(pltpu.make_async_copy(ins[a], _slot(outs[a], place, 2 * x + y), sems[4].at[a]))
                for q in range(self.pieces):
                    for r, (px, py) in enumerate(peers):
                        k, part = (3 * a + r) * self.pieces + q, (q, self.pieces)
                        ici.append(remote(_slot(ins[a], place, None, c, part), _slot(outs[a], place, 2 * x + y, c, part),
                                          sems[0], sems[1], k, (px, py, c)))
                        landed = _slot(outs[a], place, 2 * px + py, c, part)
                        d2d.append(remote(landed, landed, sems[2], sems[3], k, (x, y, 1 - c)))
            else:
                local.append(pltpu.make_async_copy(_slot(ins[a], place, 2 * x + y), outs[a].at[3], sems[2].at[a]))
                for r, (px, py) in enumerate(peers):
                    ici.append(remote(_slot(ins[a], place, 2 * px + py), outs[a].at[r], sems[0], sems[1], 3 * a + r, (px, py, c)))
        return local, ici, d2d

    def start(self, ins, outs, sems):
        local, ici, _ = self._copies(ins, outs, sems)
        for cp in local + ici:
            cp.start()

    def relay(self, ins, outs, sems):
        _, ici, d2d = self._copies(ins, outs, sems)
        for arrived, onward in zip(ici, d2d):
            arrived.wait_recv()
            onward.start()

    def finish(self, ins, outs, sems):
        local, ici, d2d = self._copies(ins, outs, sems)
        for cp in local + d2d:
            cp.wait()
        for cp in ici:
            cp.wait_send() if d2d else cp.wait()


def _exchange_call(job, name):
    n = job.n

    def body(*refs):
        ins, outs, sems = refs[:n], refs[n:2 * n], refs[2 * n:]
        job.start(ins, outs, sems)
        job.relay(ins, outs, sems)
        job.finish(ins, outs, sems)

    return pl.pallas_call(
        body, name=name, in_specs=[ANY_SPEC] * n, out_specs=[ANY_SPEC] * n, out_shape=job.out_shape,
        scratch_shapes=job.scratch, compiler_params=pltpu.CompilerParams(has_side_effects=True),
    )(*job.arrays)


def _carried_call(body, *, name, grid, in_specs, out_specs, out_shape, semantics, operands, scratch_shapes=(), job=None):
    scratch_shapes = list(scratch_shapes)
    if job is None:
        return pl.pallas_call(body, name=name, grid=grid, in_specs=in_specs, out_specs=out_specs, out_shape=out_shape,
                              scratch_shapes=scratch_shapes, compiler_params=_params(*semantics))(*operands), []
    n_in, n_out, n, n_scr = len(in_specs), len(out_specs), job.n, len(scratch_shapes)
    steps = math.prod(grid)

    def wrapped(*refs):
        ins, job_ins = refs[:n_in], refs[n_in:n_in + n]
        outs, job_outs = refs[n_in + n:n_in + n + n_out], refs[n_in + n + n_out:n_in + 2 * n + n_out]
        outs = outs + refs[n_in + 2 * n + n_out:n_in + 2 * n + n_out + n_scr]
        sems = refs[n_in + 2 * n + n_out + n_scr:]
        step = functools.reduce(lambda acc, d: acc * grid[d] + pl.program_id(d), range(len(grid)), 0)

        @pl.when(step == 0)
        def _():
            job.start(job_ins, job_outs, sems)

        @pl.when(step == (3 * steps) // 4)
        def _():
            job.relay(job_ins, job_outs, sems)

        body(*ins, *outs)

        @pl.when(step == steps - 1)
        def _():
            job.finish(job_ins, job_outs, sems)

    res = pl.pallas_call(
        wrapped, name=name, grid=grid, in_specs=list(in_specs) + [ANY_SPEC] * n, out_specs=list(out_specs) + [ANY_SPEC] * n,
        out_shape=list(out_shape) + job.out_shape, scratch_shapes=scratch_shapes + job.scratch,
        compiler_params=pltpu.CompilerParams(dimension_semantics=("arbitrary",) * len(grid), vmem_limit_bytes=VMEM_LIMIT,
                                             has_side_effects=True),
    )(*operands, *job.arrays)
    return res[:n_out], res[n_out:]


def _swap_and_spread(parts, t):
    n = len(parts)

    def body(*refs):
        ins, t_ref, outs, slots = refs[:n], refs[n], refs[n + 1:2 * n + 1], refs[2 * n + 1]
        send, recv, all_send, all_recv, own = refs[2 * n + 2:]
        x, y, c = lax.axis_index("x"), lax.axis_index("y"), lax.axis_index("c")
        mine = slots.at[4 * x + 2 * y + c]
        copies = [pltpu.make_async_copy(t_ref, mine, own)]
        copies += [pltpu.make_async_remote_copy(src_ref=ins[a], dst_ref=outs[a], send_sem=send.at[a], recv_sem=recv.at[a],
                                                device_id=(x, y, 1 - c), device_id_type=MESH_ID) for a in range(n)]
        for m in range(1, 8):
            peer = (x ^ (m >> 2), y ^ ((m >> 1) & 1), c ^ (m & 1))
            copies.append(pltpu.make_async_remote_copy(src_ref=t_ref, dst_ref=mine, send_sem=all_send.at[m - 1],
                                                       recv_sem=all_recv.at[m - 1], device_id=peer, device_id_type=MESH_ID))
        for cp in copies:
            cp.start()
        for cp in copies:
            cp.wait()

    res = pl.pallas_call(
        body, name="swap_and_spread",
        in_specs=[ANY_SPEC] * (n + 1), out_specs=[ANY_SPEC] * (n + 1),
        out_shape=[jax.ShapeDtypeStruct(p.shape, p.dtype) for p in parts] + [jax.ShapeDtypeStruct((8,) + t.shape, t.dtype)],
        scratch_shapes=[pltpu.SemaphoreType.DMA((n,)), pltpu.SemaphoreType.DMA((n,)), pltpu.SemaphoreType.DMA((7,)),
                        pltpu.SemaphoreType.DMA((7,)), pltpu.SemaphoreType.DMA(())],
        compiler_params=pltpu.CompilerParams(has_side_effects=True),
    )(*parts, t)
    return res[:n], res[n]


def _adamw(w, g, m, v):
    m = ADAM_B1 * m + (1.0 - ADAM_B1) * g
    v = ADAM_B2 * v + (1.0 - ADAM_B2) * jnp.square(g)
    m_hat = m / (1.0 - ADAM_B1 ** ADAM_STEP)
    v_hat = v / (1.0 - ADAM_B2 ** ADAM_STEP)
    return -ADAM_LR * (m_hat / (jnp.sqrt(v_hat) + ADAM_EPS) + ADAM_WD * w), m, v


def _chip_sums(group, name):
    rows, cols = group[0].shape[1:]
    count = len(group)
    tl = _row_block(rows, 512 if count == 1 else 128, tile=32 // group[0].dtype.itemsize)

    def body(*refs):
        for r_ref, o_ref in zip(refs[:count], refs[count:]):
            total = ((r_ref[0].astype(F32) + r_ref[1].astype(F32)) + r_ref[2].astype(F32)) + r_ref[3].astype(F32)
            o_ref[...] = total.astype(o_ref.dtype)

    return pl.pallas_call(body, name=name, grid=(rows // tl,),
                          in_specs=[pl.BlockSpec((N_CHIPS, tl, cols), lambda i: (0, i, 0))] * count,
                          out_specs=[pl.BlockSpec((tl, cols), lambda i: (i, 0))] * count,
                          out_shape=[jax.ShapeDtypeStruct((rows, cols), group[0].dtype)] * count,
                          compiler_params=_params("parallel"))(*group)


def _adamw_layer(w, m, v, p_mine, p_other, layer, prev, name):
    _, rows, cols = w.shape
    assert p_mine.shape[1] == cols and p_mine.shape[0] >= rows
    tl = _row_block(rows, 512, tile=32 // p_mine.dtype.itemsize)
    tc = cols
    if tl < 128 < rows and cols % 256 == 0:
        tl, tc = rows, 256

    def body(w_ref, m_ref, v_ref, pa_ref, pb_ref, *rest):
        g = pa_ref[...].astype(F32) + pb_ref[...].astype(F32)
        for o_ref, val in zip(rest[-4:], (g,) + _adamw(w_ref[...], g, m_ref[...], v_ref[...])):
            o_ref[...] = val

    stacked = pl.BlockSpec((None, tl, tc), lambda i, j: (layer, i, j))
    part = pl.BlockSpec((tl, tc), lambda i, j: (i, j))
    kept = list(prev) if prev else []
    return pl.pallas_call(
        body, name=name, grid=(rows // tl, cols // tc),
        in_specs=[stacked] * 3 + [part] * 2 + [ANY_SPEC] * len(kept),
        out_specs=[stacked] * 4, out_shape=[jax.ShapeDtypeStruct(w.shape, F32)] * 4,
        input_output_aliases={5 + k: k for k in range(len(kept))},
        compiler_params=_params("parallel", "parallel"),
    )(w, m, v, p_mine, p_other, *kept)


def _adamw_small(w, slots, m, v):
    def fn(wv, sv, mv, vv):
        g = sv[0]
        for dev in range(1, 8):
            g = g + sv[dev]
        return (g,) + _adamw(wv, g, mv, vv)

    return _rowmap(fn, [w, slots, m, v], "rcrr", [(w.shape, F32, "r")] * 4, name="adamw_small", tl=w.shape[0])


WEIGHTS = ['ffn1_norm', 'ffn1_w_gate', 'ffn1_w_up', 'ffn1_w_down', 'mix_norm', 'ffn2_norm', 'ffn2_w_gate', 'ffn2_w_up',
           'ffn2_w_down', 'ab_w_in', 's5_lambda_re', 's5_lambda_im', 's5_log_dt', 's5_b_re', 's5_b_im', 's5_c_re', 's5_c_im',
           's5_d', 's5_w_glu', 'ab_w_out', 'sc_w_in', 'sc_conv_w', 'sc_w_out', 'final_norm']
SHARDED = {'ffn1_w_gate': (0, FF_SLOT), 'ffn1_w_up': (0, FF_SLOT), 'ffn1_w_down': (0, FF_SLOT),
           'ffn2_w_gate': (0, FF_SLOT), 'ffn2_w_up': (0, FF_SLOT), 'ffn2_w_down': (0, FF_SLOT),
           'ab_w_in': (1, 512), 's5_w_glu': (0, 128), 'ab_w_out': (0, 256), 'sc_w_in': (1, 768), 'sc_conv_w': (1, 256),
           'sc_w_out': (0, 256)}
SWAPPED = ('ffn1_w_gate', 'ffn1_w_up', 'ffn2_w_gate', 'ffn2_w_up')
SMALL = [n for n in WEIGHTS if n not in SHARDED]


def _held(name, t):
    return jnp.swapaxes(t, 1, 2) if name in SWAPPED else t


def _pack(arrays):
    rows = []
    for t in arrays:
        flat = t.reshape(-1)
        rows.append(jnp.pad(flat, (0, (-flat.shape[0]) % 128)))
    flat = jnp.concatenate(rows)
    return jnp.pad(flat, (0, (-flat.shape[0]) % 1024)).reshape(-1, 128)


def _unpack(packed, like):
    flat, out, pos = packed.reshape(-1), [], 0
    for t in like:
        out.append(flat[pos:pos + t.size].reshape(t.shape))
        pos += t.size + (-t.size) % 128
    return out


def _local_grads(x, target, p, full, late=None):
    small, grads, saved = {}, {}, []
    for layer in range(2):
        x, s1 = _ffn_fwd(x, p["ffn1_norm"][layer:layer + 1], full, "ffn1", layer, late)
        if layer == 0:
            x, sm = _mix0_fwd(x, p["mix_norm"][0:1], p, full, late)
        else:
            x, sm = _mix1_fwd(x, p["mix_norm"][1:2], full, late)
        x, s2 = _ffn_fwd(x, p["ffn2_norm"][layer:layer + 1], full, "ffn2", layer, late)
        saved.append((s1, sm, s2))
    loss, dx, dg_final = _loss_head(x, p["final_norm"][None], target)
    small["final_norm"] = dg_final[0]
    gains = {n: [None, None] for n in ("ffn1_norm", "mix_norm", "ffn2_norm")}

    def ffn_bwd(which, layer, dx, s):
        dx, dg = _ffn_bwd(dx, s, p[f"{which}_norm"][layer:layer + 1], full, which, layer, grads, late,
                          inline={("ffn2", 1): "defer", ("ffn1", 0): "chain"}.get((which, layer)))
        gains[f"{which}_norm"][layer] = dg[0]
        return dx

    for layer in (1, 0):
        s1, sm, s2 = saved[layer]
        dx = ffn_bwd("ffn2", layer, dx, s2)
        if layer == 0:
            dx, dg, s5_grads = _mix0_bwd(dx, sm, p["mix_norm"][0:1], p, full, grads, late)
            small.update(s5_grads)
        else:
            dx, dg = _mix1_bwd(dx, sm, p["mix_norm"][1:2], full, grads, late)
        gains["mix_norm"][layer] = dg[0]
        dx = ffn_bwd("ffn1", layer, dx, s1)
    small.update({n: jnp.stack(pair) for n, pair in gains.items()})
    return loss, dx, small, grads


_GATHER_PLAN = {
    "gather_early": [("ffn1_w_gate", 0), ("ffn1_w_up", 0)],
    "ffn1_0_up": [("ffn1_w_down", 0), ("ab_w_in", 0)],
    "sb_fwd": [("s5_w_glu", 0), ("ab_w_out", 0), ("ffn2_w_gate", 0), ("ffn2_w_up", 0), ("ffn2_w_down", 0),
               ("ffn1_w_gate", 1), ("ffn1_w_up", 1), ("ffn1_w_down", 1), ("ffn2_w_gate", 1)],
    "ffn2_0_up": [("sc_w_in", 0), ("sc_conv_w", 0), ("sc_w_out", 0)],
    "ffn1_1_up": [("ffn2_w_up", 1)],
    "ffn2_1_up": [("ffn2_w_down", 1)],
}


class _Late:
    def __init__(self, shards, places):
        self.shards, self.places = shards, places
        self.sent, self.received = set(), {}

    def gather_job(self, carrier):
        keys = _GATHER_PLAN.get(carrier, [])
        if not keys:
            return None, []
        pieces = 4 if carrier == "gather_early" else 1
        return _Exchange("gather", [self.shards[k] for k in keys], [self.places[k] for k in keys], pieces), keys

    def scatter_job(self, grads, only=None):
        keys = [k for k in grads if k not in self.sent and (only is None or k in only)]
        if not keys:
            return None, []
        self.sent.update(keys)
        return _Exchange("scatter", [grads[k] for k in keys], [self.places[k] for k in keys]), keys


def kernel(x, ffn1_norm, ffn1_w_gate, ffn1_w_up, ffn1_w_down, mix_norm, ffn2_norm, ffn2_w_gate, ffn2_w_up, ffn2_w_down, ab_w_in, s5_lambda_re, s5_lambda_im, s5_log_dt, s5_b_re, s5_b_im, s5_c_re, s5_c_im, s5_d, s5_w_glu, ab_w_out, sc_w_in, sc_conv_w, sc_w_out, final_norm, loss_target, m_ffn1_norm, m_ffn1_w_gate, m_ffn1_w_up, m_ffn1_w_down, m_mix_norm, m_ffn2_norm, m_ffn2_w_gate, m_ffn2_w_up, m_ffn2_w_down, m_ab_w_in, m_s5_lambda_re, m_s5_lambda_im, m_s5_log_dt, m_s5_b_re, m_s5_b_im, m_s5_c_re, m_s5_c_im, m_s5_d, m_s5_w_glu, m_ab_w_out, m_sc_w_in, m_sc_conv_w, m_sc_w_out, m_final_norm, v_ffn1_norm, v_ffn1_w_gate, v_ffn1_w_up, v_ffn1_w_down, v_mix_norm, v_ffn2_norm, v_ffn2_w_gate, v_ffn2_w_up, v_ffn2_w_down, v_ab_w_in, v_s5_lambda_re, v_s5_lambda_im, v_s5_log_dt, v_s5_b_re, v_s5_b_im, v_s5_c_re, v_s5_c_im, v_s5_d, v_s5_w_glu, v_ab_w_out, v_sc_w_in, v_sc_conv_w, v_sc_w_out, v_final_norm):
    args = dict(locals())
    p = {n: _held(n, args[n]) for n in WEIGHTS}
    mom = {n: _held(n, args["m_" + n]) for n in WEIGHTS}
    var = {n: _held(n, args["v_" + n]) for n in WEIGHTS}

    keys = [(n, layer) for n in SHARDED for layer in range(p[n].shape[0])]
    shards, places = {}, {}
    for n, layer in keys:
        axis, width = SHARDED[n]
        t = p[n][layer] if n == "sc_conv_w" else p[n][layer].astype(MXU_DTYPE)
        pad = [(0, 0), (0, 0)]
        pad[axis] = (0, width - t.shape[axis])
        shards[(n, layer)], places[(n, layer)] = jnp.pad(t, pad), (axis, width)
    late = _Late(shards, places)
    job, first = late.gather_job("gather_early")
    full = dict(zip(first, _exchange_call(job, "gather_early")))

    loss, dx, small, grads = _local_grads(x[0], loss_target[0], p, full, late)
    loss = lax.psum(loss[0, 0], ("x", "y", "c"))
    assert set(late.received) == set(keys), "a gradient was left without a carrier"

    alike = {}
    for key in keys:
        alike.setdefault((late.received[key].shape, late.received[key].dtype), []).append(key)
    summed = {}
    for group in alike.values():
        summed.update(zip(group, _chip_sums([late.received[k] for k in group], name=f"chip_sum_{group[0][0]}_x{len(group)}")))
    partial = [summed[key] for key in keys]
    other, small_slots = _swap_and_spread(partial, _pack([small[n] for n in SMALL]))
    out = {}
    for (n, layer), mine, theirs in zip(keys, partial, other):
        out[n] = _adamw_layer(p[n], mom[n], var[n], mine, theirs, layer, out.get(n), name=f"adamw_{n}_{layer}")
    out = {n: [_held(n, t) for t in res] for n, res in out.items()}

    like = [p[n] for n in SMALL]
    results = _adamw_small(_pack(like), small_slots, _pack([mom[n] for n in SMALL]), _pack([var[n] for n in SMALL]))
    for k, packed in enumerate(results):
        for n, t in zip(SMALL, _unpack(packed, like)):
            out.setdefault(n, [None] * 4)[k] = t

    return (loss, dx[None], *[out[n][0] for n in WEIGHTS], *[out[n][1] for n in WEIGHTS],
            *[out[n][2] for n in WEIGHTS], *[out[n][3] for n in WEIGHTS])
```
